```python
import math
import jax, jax.numpy as jnp
from jax import lax
import numpy as np

D_MODEL = 2048
BATCH = 8
SEQ = 4096
DEPTH = 1

MIX_WIDTH = D_MODEL
GROUP_DIM = 128
CONV_WIDTH = MIX_WIDTH // 2
GM_WIDTH = MIX_WIDTH - CONV_WIDTH
CONV_GROUPS = CONV_WIDTH // GROUP_DIM
GM_HEADS = GM_WIDTH // GROUP_DIM
CONV_K = 3
CHUNK = 128
D_FF = 256 * math.ceil(8 * D_MODEL / 3 / 256)
N_MEM = 256
XA_HEADS = 4
XA_HEAD_DIM = D_MODEL // XA_HEADS
IN_COLS = 3 * CONV_WIDTH + 2 * GM_WIDTH
EPS = 1e-6

kernel_name = "hybrid_conv_gmlp_macaron_xattn"


def _rmsnorm(x, g):
    x32 = x.astype(jnp.float32)
    y = x32 * lax.rsqrt(jnp.mean(x32 * x32, axis=-1, keepdims=True) + EPS)
    return (y * g.astype(jnp.float32)).astype(x.dtype)


def _swiglu(x, w_in, w_out):
    gate, up = jnp.split(x @ w_in, 2, axis=-1)
    return (jax.nn.silu(gate) * up) @ w_out


def _causal_dwconv(z, conv_w, conv_b):
    s = z.shape[1]
    zp = jnp.pad(z, ((0, 0), (CONV_K - 1, 0), (0, 0)))
    y = conv_b + conv_w[CONV_K - 1] * z
    for k in range(CONV_K - 1):
        y = y + conv_w[k] * zp[:, k:k + s]
    return y


def _spatial_gating(u, v, g_v, w_s, b_s):
    b, s, _ = v.shape
    vh = v.reshape(b, s // CHUNK, CHUNK, GM_HEADS, GROUP_DIM)
    vh = _rmsnorm(vh, g_v.reshape(GM_HEADS, GROUP_DIM))
    mask = jnp.tril(jnp.ones((CHUNK, CHUNK), dtype=bool))
    w = jnp.where(mask[None], w_s, jnp.zeros_like(w_s)).astype(vh.dtype)
    sg = jnp.einsum('hts,bnshc->bnthc', w, vh) + b_s.T[None, None, :, :, None].astype(vh.dtype)
    return u * sg.reshape(b, s, GM_WIDTH)


def _cross_attention(n, mem_n, w_q, w_k, w_v, w_o):
    b, s, _ = n.shape
    m = mem_n.shape[1]
    q = (n @ w_q).reshape(b, s, XA_HEADS, XA_HEAD_DIM)
    k = (mem_n @ w_k).reshape(b, m, XA_HEADS, XA_HEAD_DIM)
    v = (mem_n @ w_v).reshape(b, m, XA_HEADS, XA_HEAD_DIM)
    scores = jnp.einsum('bshd,bmhd->bhsm', q, k).astype(jnp.float32) * (XA_HEAD_DIM ** -0.5)
    p = jax.nn.softmax(scores, axis=-1).astype(q.dtype)
    o = jnp.einsum('bhsm,bmhd->bshd', p, v).reshape(b, s, XA_HEADS * XA_HEAD_DIM)
    return o @ w_o


def _fwd_setup_inputs(seed: int = 0) -> dict:
    key = jax.random.key(seed)
    ks = jax.random.split(key, 32)
    f32 = jnp.float32

    def w(k, shape, fan_in):
        return jax.random.normal(k, shape, f32) * (fan_in ** -0.5)

    def gain(k, shape):
        return 1.0 + 0.1 * jax.random.normal(k, shape, f32)

    L = DEPTH
    return {
        "x": jax.random.normal(ks[0], (BATCH, SEQ, D_MODEL), f32),
        "mem": jax.random.normal(ks[1], (BATCH, N_MEM, D_MODEL), f32),
        "g_ffn1": gain(ks[2], (L, D_MODEL)),
        "w_ffn1_in": w(ks[3], (L, D_MODEL, 2 * D_FF), D_MODEL),
        "w_ffn1_out": w(ks[4], (L, D_FF, D_MODEL), D_FF),
        "g_mix": gain(ks[5], (L, D_MODEL)),
        "w_mix_in": w(ks[6], (L, D_MODEL, IN_COLS), D_MODEL),
        "conv_w": w(ks[7], (L, CONV_K, CONV_WIDTH), CONV_K),
        "conv_b": 0.01 * jax.random.normal(ks[8], (L, CONV_WIDTH), f32),
        "g_gm_v": gain(ks[9], (L, GM_WIDTH)),
        "w_spatial": w(ks[10], (L, GM_HEADS, CHUNK, CHUNK), CHUNK),
        "b_spatial": 1.0 + 0.1 * jax.random.normal(ks[11], (L, GM_HEADS, CHUNK), f32),
        "w_mix_out": w(ks[12], (L, MIX_WIDTH, D_MODEL), MIX_WIDTH),
        "g_xattn": gain(ks[13], (L, D_MODEL)),
        "g_mem": gain(ks[14], (L, D_MODEL)),
        "w_xq": w(ks[15], (L, D_MODEL, D_MODEL), D_MODEL),
        "w_xk": w(ks[16], (L, D_MODEL, D_MODEL), D_MODEL),
        "w_xv": w(ks[17], (L, D_MODEL, D_MODEL), D_MODEL),
        "w_xo": w(ks[18], (L, D_MODEL, D_MODEL), D_MODEL),
        "g_ffn2": gain(ks[19], (L, D_MODEL)),
        "w_ffn2_in": w(ks[20], (L, D_MODEL, 2 * D_FF), D_MODEL),
        "w_ffn2_out": w(ks[21], (L, D_FF, D_MODEL), D_FF),
        "g_final": gain(ks[22], (D_MODEL,)),
    }


def _fwd_reference(x, mem, g_ffn1, w_ffn1_in, w_ffn1_out, g_mix, w_mix_in, conv_w, conv_b,
              g_gm_v, w_spatial, b_spatial, w_mix_out, g_xattn, g_mem, w_xq, w_xk, w_xv,
              w_xo, g_ffn2, w_ffn2_in, w_ffn2_out, g_final):
    h = x
    splits = [CONV_WIDTH, 2 * CONV_WIDTH, 3 * CONV_WIDTH, 3 * CONV_WIDTH + GM_WIDTH]
    for l in range(DEPTH):
        h = h + 0.5 * _swiglu(_rmsnorm(h, g_ffn1[l]), w_ffn1_in[l], w_ffn1_out[l])

        z = _rmsnorm(h, g_mix[l]) @ w_mix_in[l]
        gate_b, gate_c, h_c, u, v = jnp.split(z, splits, axis=-1)
        y_conv = gate_b * _causal_dwconv(gate_c * h_c, conv_w[l], conv_b[l])
        y_gm = _spatial_gating(u, v, g_gm_v[l], w_spatial[l], b_spatial[l])
        h = h + jnp.concatenate([y_conv, y_gm], axis=-1) @ w_mix_out[l]

        h = h + _cross_attention(_rmsnorm(h, g_xattn[l]), _rmsnorm(mem, g_mem[l]),
                                 w_xq[l], w_xk[l], w_xv[l], w_xo[l])

        h = h + 0.5 * _swiglu(_rmsnorm(h, g_ffn2[l]), w_ffn2_in[l], w_ffn2_out[l])
    return _rmsnorm(h, g_final)


import jax as _jax
import jax.numpy as _jnp

TWIN_FORMAT = 'train_step'
FWD_PARAMS = ['x', 'mem', 'g_ffn1', 'w_ffn1_in', 'w_ffn1_out', 'g_mix', 'w_mix_in', 'conv_w', 'conv_b', 'g_gm_v', 'w_spatial', 'b_spatial', 'w_mix_out', 'g_xattn', 'g_mem', 'w_xq', 'w_xk', 'w_xv', 'w_xo', 'g_ffn2', 'w_ffn2_in', 'w_ffn2_out', 'g_final']
TWIN_WEIGHTS = ['g_ffn1', 'w_ffn1_in', 'w_ffn1_out', 'g_mix', 'w_mix_in', 'conv_w', 'conv_b', 'g_gm_v', 'w_spatial', 'b_spatial', 'w_mix_out', 'g_xattn', 'g_mem', 'w_xq', 'w_xk', 'w_xv', 'w_xo', 'g_ffn2', 'w_ffn2_in', 'w_ffn2_out', 'g_final']
TWIN_DIFF_INPUT = 'x'
TWIN_INPUTS = ['x', 'mem', 'g_ffn1', 'w_ffn1_in', 'w_ffn1_out', 'g_mix', 'w_mix_in', 'conv_w', 'conv_b', 'g_gm_v', 'w_spatial', 'b_spatial', 'w_mix_out', 'g_xattn', 'g_mem', 'w_xq', 'w_xk', 'w_xv', 'w_xo', 'g_ffn2', 'w_ffn2_in', 'w_ffn2_out', 'g_final', 'loss_target', 'm_g_ffn1', 'm_w_ffn1_in', 'm_w_ffn1_out', 'm_g_mix', 'm_w_mix_in', 'm_conv_w', 'm_conv_b', 'm_g_gm_v', 'm_w_spatial', 'm_b_spatial', 'm_w_mix_out', 'm_g_xattn', 'm_g_mem', 'm_w_xq', 'm_w_xk', 'm_w_xv', 'm_w_xo', 'm_g_ffn2', 'm_w_ffn2_in', 'm_w_ffn2_out', 'm_g_final', 'v_g_ffn1', 'v_w_ffn1_in', 'v_w_ffn1_out', 'v_g_mix', 'v_w_mix_in', 'v_conv_w', 'v_conv_b', 'v_g_gm_v', 'v_w_spatial', 'v_b_spatial', 'v_w_mix_out', 'v_g_xattn', 'v_g_mem', 'v_w_xq', 'v_w_xk', 'v_w_xv', 'v_w_xo', 'v_g_ffn2', 'v_w_ffn2_in', 'v_w_ffn2_out', 'v_g_final']
TWIN_OUTPUTS = ['loss', 'grad_x', 'grad_g_ffn1', 'grad_w_ffn1_in', 'grad_w_ffn1_out', 'grad_g_mix', 'grad_w_mix_in', 'grad_conv_w', 'grad_conv_b', 'grad_g_gm_v', 'grad_w_spatial', 'grad_b_spatial', 'grad_w_mix_out', 'grad_g_xattn', 'grad_g_mem', 'grad_w_xq', 'grad_w_xk', 'grad_w_xv', 'grad_w_xo', 'grad_g_ffn2', 'grad_w_ffn2_in', 'grad_w_ffn2_out', 'grad_g_final', 'delta_g_ffn1', 'delta_w_ffn1_in', 'delta_w_ffn1_out', 'delta_g_mix', 'delta_w_mix_in', 'delta_conv_w', 'delta_conv_b', 'delta_g_gm_v', 'delta_w_spatial', 'delta_b_spatial', 'delta_w_mix_out', 'delta_g_xattn', 'delta_g_mem', 'delta_w_xq', 'delta_w_xk', 'delta_w_xv', 'delta_w_xo', 'delta_g_ffn2', 'delta_w_ffn2_in', 'delta_w_ffn2_out', 'delta_g_final', 'new_m_g_ffn1', 'new_m_w_ffn1_in', 'new_m_w_ffn1_out', 'new_m_g_mix', 'new_m_w_mix_in', 'new_m_conv_w', 'new_m_conv_b', 'new_m_g_gm_v', 'new_m_w_spatial', 'new_m_b_spatial', 'new_m_w_mix_out', 'new_m_g_xattn', 'new_m_g_mem', 'new_m_w_xq', 'new_m_w_xk', 'new_m_w_xv', 'new_m_w_xo', 'new_m_g_ffn2', 'new_m_w_ffn2_in', 'new_m_w_ffn2_out', 'new_m_g_final', 'new_v_g_ffn1', 'new_v_w_ffn1_in', 'new_v_w_ffn1_out', 'new_v_g_mix', 'new_v_w_mix_in', 'new_v_conv_w', 'new_v_conv_b', 'new_v_g_gm_v', 'new_v_w_spatial', 'new_v_b_spatial', 'new_v_w_mix_out', 'new_v_g_xattn', 'new_v_g_mem', 'new_v_w_xq', 'new_v_w_xk', 'new_v_w_xv', 'new_v_w_xo', 'new_v_g_ffn2', 'new_v_w_ffn2_in', 'new_v_w_ffn2_out', 'new_v_g_final']
TWIN_LEAF_KINDS = {'loss': 'loss', 'grad_x': 'grad_x', 'grad_g_ffn1': 'grad_w', 'grad_w_ffn1_in': 'grad_w', 'grad_w_ffn1_out': 'grad_w', 'grad_g_mix': 'grad_w', 'grad_w_mix_in': 'grad_w', 'grad_conv_w': 'grad_w', 'grad_conv_b': 'grad_w', 'grad_g_gm_v': 'grad_w', 'grad_w_spatial': 'grad_w', 'grad_b_spatial': 'grad_w', 'grad_w_mix_out': 'grad_w', 'grad_g_xattn': 'grad_w', 'grad_g_mem': 'grad_w', 'grad_w_xq': 'grad_w', 'grad_w_xk': 'grad_w', 'grad_w_xv': 'grad_w', 'grad_w_xo': 'grad_w', 'grad_g_ffn2': 'grad_w', 'grad_w_ffn2_in': 'grad_w', 'grad_w_ffn2_out': 'grad_w', 'grad_g_final': 'grad_w', 'delta_g_ffn1': 'delta_w', 'delta_w_ffn1_in': 'delta_w', 'delta_w_ffn1_out': 'delta_w', 'delta_g_mix': 'delta_w', 'delta_w_mix_in': 'delta_w', 'delta_conv_w': 'delta_w', 'delta_conv_b': 'delta_w', 'delta_g_gm_v': 'delta_w', 'delta_w_spatial': 'delta_w', 'delta_b_spatial': 'delta_w', 'delta_w_mix_out': 'delta_w', 'delta_g_xattn': 'delta_w', 'delta_g_mem': 'delta_w', 'delta_w_xq': 'delta_w', 'delta_w_xk': 'delta_w', 'delta_w_xv': 'delta_w', 'delta_w_xo': 'delta_w', 'delta_g_ffn2': 'delta_w', 'delta_w_ffn2_in': 'delta_w', 'delta_w_ffn2_out': 'delta_w', 'delta_g_final': 'delta_w', 'new_m_g_ffn1': 'new_m', 'new_m_w_ffn1_in': 'new_m', 'new_m_w_ffn1_out': 'new_m', 'new_m_g_mix': 'new_m', 'new_m_w_mix_in': 'new_m', 'new_m_conv_w': 'new_m', 'new_m_conv_b': 'new_m', 'new_m_g_gm_v': 'new_m', 'new_m_w_spatial': 'new_m', 'new_m_b_spatial': 'new_m', 'new_m_w_mix_out': 'new_m', 'new_m_g_xattn': 'new_m', 'new_m_g_mem': 'new_m', 'new_m_w_xq': 'new_m', 'new_m_w_xk': 'new_m', 'new_m_w_xv': 'new_m', 'new_m_w_xo': 'new_m', 'new_m_g_ffn2': 'new_m', 'new_m_w_ffn2_in': 'new_m', 'new_m_w_ffn2_out': 'new_m', 'new_m_g_final': 'new_m', 'new_v_g_ffn1': 'new_v', 'new_v_w_ffn1_in': 'new_v', 'new_v_w_ffn1_out': 'new_v', 'new_v_g_mix': 'new_v', 'new_v_w_mix_in': 'new_v', 'new_v_conv_w': 'new_v', 'new_v_conv_b': 'new_v', 'new_v_g_gm_v': 'new_v', 'new_v_w_spatial': 'new_v', 'new_v_b_spatial': 'new_v', 'new_v_w_mix_out': 'new_v', 'new_v_g_xattn': 'new_v', 'new_v_g_mem': 'new_v', 'new_v_w_xq': 'new_v', 'new_v_w_xk': 'new_v', 'new_v_w_xv': 'new_v', 'new_v_w_xo': 'new_v', 'new_v_g_ffn2': 'new_v', 'new_v_w_ffn2_in': 'new_v', 'new_v_w_ffn2_out': 'new_v', 'new_v_g_final': 'new_v'}


def _forward(args):
    return _fwd_reference(*[args[k] for k in FWD_PARAMS])


def _output_shape():
    def fwd():
        inp = _fwd_setup_inputs(0)
        return _fwd_reference(*[inp[k] for k in FWD_PARAMS])
    out = _jax.eval_shape(fwd)
    return out.shape, out.dtype

N_MICROBATCH = 1
ADAM_LR = 0.001
ADAM_B1 = 0.9
ADAM_B2 = 0.999
ADAM_EPS = 1e-08
ADAM_WD = 0.01
ADAM_STEP = 10
PER_EXAMPLE_BATCH_AXIS = {'x': 0, 'mem': 0, 'loss_target': 0}
SHARED_INPUTS = []
_WEIGHT_DTYPES = {'g_ffn1': _jnp.float32, 'w_ffn1_in': _jnp.float32, 'w_ffn1_out': _jnp.float32, 'g_mix': _jnp.float32, 'w_mix_in': _jnp.float32, 'conv_w': _jnp.float32, 'conv_b': _jnp.float32, 'g_gm_v': _jnp.float32, 'w_spatial': _jnp.float32, 'b_spatial': _jnp.float32, 'w_mix_out': _jnp.float32, 'g_xattn': _jnp.float32, 'g_mem': _jnp.float32, 'w_xq': _jnp.float32, 'w_xk': _jnp.float32, 'w_xv': _jnp.float32, 'w_xo': _jnp.float32, 'g_ffn2': _jnp.float32, 'w_ffn2_in': _jnp.float32, 'w_ffn2_out': _jnp.float32, 'g_final': _jnp.float32}
MOMENT_SCALE = {'g_ffn1': 4.825980e-02, 'w_ffn1_in': 2.037399e-02, 'w_ffn1_out': 3.331988e-02, 'g_mix': 9.820569e-02, 'w_mix_in': 6.165741e-02, 'conv_w': 6.279835e-02, 'conv_b': 6.616255e-02, 'g_gm_v': 4.498481e-02, 'w_spatial': 4.247044e-02, 'b_spatial': 6.090432e-02, 'w_mix_out': 7.522649e-02, 'g_xattn': 6.285089e-03, 'g_mem': 9.849117e-03, 'w_xq': 6.386056e-03, 'w_xk': 6.395707e-03, 'w_xv': 6.486341e-03, 'w_xo': 6.554392e-03, 'g_ffn2': 2.572012e-02, 'w_ffn2_in': 1.080949e-02, 'w_ffn2_out': 1.789834e-02, 'g_final': 1.605600e+01}


def _to_microbatches(a, axis):
    t = _jnp.moveaxis(a, axis, 0)
    t = t.reshape((N_MICROBATCH, t.shape[0] // N_MICROBATCH) + t.shape[1:])
    return _jnp.moveaxis(t, 1, axis + 1)


def setup_inputs(seed: int = 0) -> dict:
    inp = _fwd_setup_inputs(seed)
    key = _jax.random.fold_in(_jax.random.key(seed), 7919)
    shape, _ = _output_shape()
    out = dict(inp)
    out["loss_target"] = _jax.random.normal(_jax.random.fold_in(key, 0), shape, _jnp.float32)
    for i, name in enumerate(TWIN_WEIGHTS):
        w = inp[name].astype(_jnp.float32)
        if MOMENT_SCALE is None:
            s = _jnp.sqrt(_jnp.mean(_jnp.square(w)) + 1e-30)
        else:
            s = MOMENT_SCALE[name]
        km, kv = _jax.random.split(_jax.random.fold_in(key, i + 1))
        out[name] = w
        out["m_" + name] = s * _jax.random.normal(km, w.shape, _jnp.float32)
        out["v_" + name] = (s * s) * _jax.random.uniform(kv, w.shape, _jnp.float32, 0.5, 1.5)
    if N_MICROBATCH > 1:
        for name, axis in PER_EXAMPLE_BATCH_AXIS.items():
            out[name] = _to_microbatches(out[name], axis)
    return {'x': out['x'], 'mem': out['mem'], 'g_ffn1': out['g_ffn1'], 'w_ffn1_in': out['w_ffn1_in'], 'w_ffn1_out': out['w_ffn1_out'], 'g_mix': out['g_mix'], 'w_mix_in': out['w_mix_in'], 'conv_w': out['conv_w'], 'conv_b': out['conv_b'], 'g_gm_v': out['g_gm_v'], 'w_spatial': out['w_spatial'], 'b_spatial': out['b_spatial'], 'w_mix_out': out['w_mix_out'], 'g_xattn': out['g_xattn'], 'g_mem': out['g_mem'], 'w_xq': out['w_xq'], 'w_xk': out['w_xk'], 'w_xv': out['w_xv'], 'w_xo': out['w_xo'], 'g_ffn2': out['g_ffn2'], 'w_ffn2_in': out['w_ffn2_in'], 'w_ffn2_out': out['w_ffn2_out'], 'g_final': out['g_final'], 'loss_target': out['loss_target'], 'm_g_ffn1': out['m_g_ffn1'], 'm_w_ffn1_in': out['m_w_ffn1_in'], 'm_w_ffn1_out': out['m_w_ffn1_out'], 'm_g_mix': out['m_g_mix'], 'm_w_mix_in': out['m_w_mix_in'], 'm_conv_w': out['m_conv_w'], 'm_conv_b': out['m_conv_b'], 'm_g_gm_v': out['m_g_gm_v'], 'm_w_spatial': out['m_w_spatial'], 'm_b_spatial': out['m_b_spatial'], 'm_w_mix_out': out['m_w_mix_out'], 'm_g_xattn': out['m_g_xattn'], 'm_g_mem': out['m_g_mem'], 'm_w_xq': out['m_w_xq'], 'm_w_xk': out['m_w_xk'], 'm_w_xv': out['m_w_xv'], 'm_w_xo': out['m_w_xo'], 'm_g_ffn2': out['m_g_ffn2'], 'm_w_ffn2_in': out['m_w_ffn2_in'], 'm_w_ffn2_out': out['m_w_ffn2_out'], 'm_g_final': out['m_g_final'], 'v_g_ffn1': out['v_g_ffn1'], 'v_w_ffn1_in': out['v_w_ffn1_in'], 'v_w_ffn1_out': out['v_w_ffn1_out'], 'v_g_mix': out['v_g_mix'], 'v_w_mix_in': out['v_w_mix_in'], 'v_conv_w': out['v_conv_w'], 'v_conv_b': out['v_conv_b'], 'v_g_gm_v': out['v_g_gm_v'], 'v_w_spatial': out['v_w_spatial'], 'v_b_spatial': out['v_b_spatial'], 'v_w_mix_out': out['v_w_mix_out'], 'v_g_xattn': out['v_g_xattn'], 'v_g_mem': out['v_g_mem'], 'v_w_xq': out['v_w_xq'], 'v_w_xk': out['v_w_xk'], 'v_w_xv': out['v_w_xv'], 'v_w_xo': out['v_w_xo'], 'v_g_ffn2': out['v_g_ffn2'], 'v_w_ffn2_in': out['v_w_ffn2_in'], 'v_w_ffn2_out': out['v_w_ffn2_out'], 'v_g_final': out['v_g_final']}


def _loss(weights, diff, rest, loss_target):
    with _jax.named_scope("forward"):
        args = {**rest, TWIN_DIFF_INPUT: diff, **{k: w.astype(_WEIGHT_DTYPES[k]) for k, w in weights.items()}}
        y = _forward(args)
    with _jax.named_scope("loss_head"):
        err = _jnp.square(y.astype(_jnp.float32) - loss_target)
        return 0.5 * _jnp.sum(_jnp.mean(err, axis=-1)) if err.ndim else 0.5 * err


def _adamw(w, g, m, v):
    m = ADAM_B1 * m + (1.0 - ADAM_B1) * g
    v = ADAM_B2 * v + (1.0 - ADAM_B2) * _jnp.square(g)
    m_hat = m / (1.0 - ADAM_B1 ** ADAM_STEP)
    v_hat = v / (1.0 - ADAM_B2 ** ADAM_STEP)
    delta = -ADAM_LR * (m_hat / (_jnp.sqrt(v_hat) + ADAM_EPS) + ADAM_WD * w)
    return delta, m, v


def reference(x, mem, g_ffn1, w_ffn1_in, w_ffn1_out, g_mix, w_mix_in, conv_w, conv_b, g_gm_v, w_spatial, b_spatial, w_mix_out, g_xattn, g_mem, w_xq, w_xk, w_xv, w_xo, g_ffn2, w_ffn2_in, w_ffn2_out, g_final, loss_target, m_g_ffn1, m_w_ffn1_in, m_w_ffn1_out, m_g_mix, m_w_mix_in, m_conv_w, m_conv_b, m_g_gm_v, m_w_spatial, m_b_spatial, m_w_mix_out, m_g_xattn, m_g_mem, m_w_xq, m_w_xk, m_w_xv, m_w_xo, m_g_ffn2, m_w_ffn2_in, m_w_ffn2_out, m_g_final, v_g_ffn1, v_w_ffn1_in, v_w_ffn1_out, v_g_mix, v_w_mix_in, v_conv_w, v_conv_b, v_g_gm_v, v_w_spatial, v_b_spatial, v_w_mix_out, v_g_xattn, v_g_mem, v_w_xq, v_w_xk, v_w_xv, v_w_xo, v_g_ffn2, v_w_ffn2_in, v_w_ffn2_out, v_g_final):
    given = dict(x=x, mem=mem, g_ffn1=g_ffn1, w_ffn1_in=w_ffn1_in, w_ffn1_out=w_ffn1_out, g_mix=g_mix, w_mix_in=w_mix_in, conv_w=conv_w, conv_b=conv_b, g_gm_v=g_gm_v, w_spatial=w_spatial, b_spatial=b_spatial, w_mix_out=w_mix_out, g_xattn=g_xattn, g_mem=g_mem, w_xq=w_xq, w_xk=w_xk, w_xv=w_xv, w_xo=w_xo, g_ffn2=g_ffn2, w_ffn2_in=w_ffn2_in, w_ffn2_out=w_ffn2_out, g_final=g_final, loss_target=loss_target, m_g_ffn1=m_g_ffn1, m_w_ffn1_in=m_w_ffn1_in, m_w_ffn1_out=m_w_ffn1_out, m_g_mix=m_g_mix, m_w_mix_in=m_w_mix_in, m_conv_w=m_conv_w, m_conv_b=m_conv_b, m_g_gm_v=m_g_gm_v, m_w_spatial=m_w_spatial, m_b_spatial=m_b_spatial, m_w_mix_out=m_w_mix_out, m_g_xattn=m_g_xattn, m_g_mem=m_g_mem, m_w_xq=m_w_xq, m_w_xk=m_w_xk, m_w_xv=m_w_xv, m_w_xo=m_w_xo, m_g_ffn2=m_g_ffn2, m_w_ffn2_in=m_w_ffn2_in, m_w_ffn2_out=m_w_ffn2_out, m_g_final=m_g_final, v_g_ffn1=v_g_ffn1, v_w_ffn1_in=v_w_ffn1_in, v_w_ffn1_out=v_w_ffn1_out, v_g_mix=v_g_mix, v_w_mix_in=v_w_mix_in, v_conv_w=v_conv_w, v_conv_b=v_conv_b, v_g_gm_v=v_g_gm_v, v_w_spatial=v_w_spatial, v_b_spatial=v_b_spatial, v_w_mix_out=v_w_mix_out, v_g_xattn=v_g_xattn, v_g_mem=v_g_mem, v_w_xq=v_w_xq, v_w_xk=v_w_xk, v_w_xv=v_w_xv, v_w_xo=v_w_xo, v_g_ffn2=v_g_ffn2, v_w_ffn2_in=v_w_ffn2_in, v_w_ffn2_out=v_w_ffn2_out, v_g_final=v_g_final)
    weights = {n: given[n] for n in TWIN_WEIGHTS}
    shared = {n: given[n] for n in SHARED_INPUTS}
    per_example = {n: given[n] for n in ['x', 'mem']}
    grad_fn = _jax.value_and_grad(_loss, argnums=(0, 1))

    def one_microbatch(ex, loss_target):
        ex = dict(ex)
        diff = ex.pop(TWIN_DIFF_INPUT)
        return grad_fn(weights, diff, {**shared, **ex}, loss_target)

    if N_MICROBATCH == 1:
        loss, (grad_w, grad_x) = one_microbatch(per_example, given["loss_target"])
    else:
        def body(carry, xs):
            loss_sum, grad_sum = carry
            l_k, (gw_k, gx_k) = one_microbatch(xs[0], xs[1])
            with _jax.named_scope("update"):
                return (loss_sum + l_k, _jax.tree.map(_jnp.add, grad_sum, gw_k)), gx_k

        init = (_jnp.zeros((), _jnp.float32), _jax.tree.map(_jnp.zeros_like, weights))
        (loss, grad_w), grad_x = _jax.lax.scan(body, init, (per_example, given["loss_target"]))
    with _jax.named_scope("update"):
        delta_w, new_m, new_v = {}, {}, {}
        for n in TWIN_WEIGHTS:
            delta_w[n], new_m[n], new_v[n] = _adamw(weights[n], grad_w[n], given["m_" + n], given["v_" + n])
    return (loss, grad_x, *[grad_w[n] for n in TWIN_WEIGHTS], *[delta_w[n] for n in TWIN_WEIGHTS],
            *[new_m[n] for n in TWIN_WEIGHTS], *[new_v[n] for n in TWIN_WEIGHTS])
```

```python
import functools
import math

import jax
import jax.numpy as jnp
from jax import lax
from jax.experimental import pallas as pl
from jax.experimental.pallas import tpu as pltpu

F32 = jnp.float32
BF16 = jnp.bfloat16
EPS = 1e-6
GROUP = 128
XA_HEADS = 4
CONV_K = 3
N_CHIPS = 4
VMEM_LIMIT_BYTES = 56 * 1024 * 1024

ADAM_LR = 0.001
ADAM_B1 = 0.9
ADAM_B2 = 0.999
ADAM_EPS = 1e-08
ADAM_WD = 0.01
ADAM_STEP = 10

MESH = pl.DeviceIdType.MESH
ANY = pl.BlockSpec(memory_space=pl.ANY)


def _tile(dim, pref, mult=128):
    if dim <= pref:
        return dim
    t = (pref // mult) * mult
    while t >= mult:
        if dim % t == 0:
            return t
        t -= mult
    raise ValueError(f"no tile for {dim} under {pref}")


def _params(sem):
    return pltpu.CompilerParams(dimension_semantics=sem, vmem_limit_bytes=VMEM_LIMIT_BYTES)


def _sds(shape, dtype):
    return jax.ShapeDtypeStruct(shape, dtype)


def _dot_nn(a, b):
    return jnp.dot(a, b, preferred_element_type=F32)


def _dot_nt(a, b):
    return lax.dot_general(a, b, (((1,), (1,)), ((), ())), preferred_element_type=F32)


def _dot_tn(a, b):
    return lax.dot_general(a, b, (((0,), (0,)), ((), ())), preferred_element_type=F32)


def rmsnorm_fwd(name, x, g):
    s, d = x.shape
    tm = _tile(s, 512, 8)

    def body(x_ref, g_ref, n_ref, r_ref):
        xv = x_ref[...]
        r = lax.rsqrt(jnp.mean(xv * xv, axis=-1, keepdims=True) + EPS)
        n_ref[...] = (xv * r * g_ref[...]).astype(BF16)
        r_ref[...] = r

    row = lambda i: (i, 0)
    return pl.pallas_call(
        body, name=name, grid=(s // tm,),
        in_specs=[pl.BlockSpec((tm, d), row), pl.BlockSpec((1, d), lambda i: (0, 0))],
        out_specs=[pl.BlockSpec((tm, d), row), pl.BlockSpec((tm, 1), row)],
        out_shape=[_sds((s, d), BF16), _sds((s, 1), F32)],
        compiler_params=_params(("arbitrary",)),
    )(x, g)


def rmsnorm_bwd(name, dn, x, r, g, dh_in):
    s, d = x.shape
    tm = _tile(s, 512, 8)

    def body(dn_ref, x_ref, r_ref, g_ref, dh_ref, out_ref, outb_ref, dg_ref):
        i = pl.program_id(0)
        xh = x_ref[...] * r_ref[...]
        dnv = dn_ref[...]
        dxh = dnv * g_ref[...]
        dx = r_ref[...] * (dxh - xh * jnp.mean(dxh * xh, axis=-1, keepdims=True))
        out = dh_ref[...] + dx
        out_ref[...] = out
        outb_ref[...] = out.astype(BF16)
        part = jnp.sum(dnv * xh, axis=0, keepdims=True)

        @pl.when(i == 0)
        def _():
            dg_ref[...] = part

        @pl.when(i > 0)
        def _():
            dg_ref[...] += part

    row = lambda i: (i, 0)
    fixed = lambda i: (0, 0)
    return pl.pallas_call(
        body, name=name, grid=(s // tm,),
        in_specs=[pl.BlockSpec((tm, d), row), pl.BlockSpec((tm, d), row), pl.BlockSpec((tm, 1), row),
                  pl.BlockSpec((1, d), fixed), pl.BlockSpec((tm, d), row)],
        out_specs=[pl.BlockSpec((tm, d), row), pl.BlockSpec((tm, d), row), pl.BlockSpec((1, d), fixed)],
        out_shape=[_sds((s, d), F32), _sds((s, d), BF16), _sds((1, d), F32)],
        compiler_params=_params(("arbitrary",)),
    )(dn, x, r, g, dh_in)


def gain_grad(name, dn_a, dn_b, x, r):
    s, d = x.shape
    tm = _tile(s, 512, 8)

    def body(a_ref, b_ref, x_ref, r_ref, dg_ref):
        i = pl.program_id(0)
        part = jnp.sum((a_ref[...] + b_ref[...]) * (x_ref[...] * r_ref[...]), axis=0, keepdims=True)

        @pl.when(i == 0)
        def _():
            dg_ref[...] = part

        @pl.when(i > 0)
        def _():
            dg_ref[...] += part

    row = lambda i: (i, 0)
    return pl.pallas_call(
        body, name=name, grid=(s // tm,),
        in_specs=[pl.BlockSpec((tm, d), row), pl.BlockSpec((tm, d), row), pl.BlockSpec((tm, d), row),
                  pl.BlockSpec((tm, 1), row)],
        out_specs=pl.BlockSpec((1, d), lambda i: (0, 0)),
        out_shape=_sds((1, d), F32),
        compiler_params=_params(("arbitrary",)),
    )(dn_a, dn_b, x, r)


def loss_head(name, h, g, target):
    s, d = h.shape
    tm = _tile(s, 512, 8)
    nsteps = s // tm

    def body(h_ref, g_ref, t_ref, loss_ref, dh_ref, dhb_ref, dg_ref, sq_ref):
        i = pl.program_id(0)
        hv = h_ref[...]
        gv = g_ref[...]
        r = lax.rsqrt(jnp.mean(hv * hv, axis=-1, keepdims=True) + EPS)
        xh = hv * r
        err = xh * gv - t_ref[...]
        dy = err * (1.0 / d)
        dxh = dy * gv
        dh = r * (dxh - xh * jnp.mean(dxh * xh, axis=-1, keepdims=True))
        dh_ref[...] = dh
        dhb_ref[...] = dh.astype(BF16)
        dg_part = jnp.sum(dy * xh, axis=0, keepdims=True)
        sq_part = jnp.sum(err * err, axis=0, keepdims=True)

        @pl.when(i == 0)
        def _():
            dg_ref[...] = dg_part
            sq_ref[...] = sq_part

        @pl.when(i > 0)
        def _():
            dg_ref[...] += dg_part
            sq_ref[...] += sq_part

        @pl.when(i == nsteps - 1)
        def _():
            total = jnp.sum(sq_ref[...], axis=-1, keepdims=True) * (0.5 / d)
            loss_ref[...] = jnp.broadcast_to(total, loss_ref.shape)

    row = lambda i: (i, 0)
    fixed = lambda i: (0, 0)
    return pl.pallas_call(
        body, name=name, grid=(nsteps,),
        in_specs=[pl.BlockSpec((tm, d), row), pl.BlockSpec((1, d), fixed), pl.BlockSpec((tm, d), row)],
        out_specs=[pl.BlockSpec((8, 128), fixed), pl.BlockSpec((tm, d), row), pl.BlockSpec((tm, d), row),
                   pl.BlockSpec((1, d), fixed)],
        out_shape=[_sds((8, 128), F32), _sds((s, d), F32), _sds((s, d), BF16), _sds((1, d), F32)],
        scratch_shapes=[pltpu.VMEM((1, d), F32)],
        compiler_params=_params(("arbitrary",)),
    )(h, g, target)


def _mm(name, grid, in_arrays, in_specs, out_shapes, out_specs, acc_tile, dot, epilogue):
    nk = grid[2]
    n_in = len(in_arrays)
    n_out = len(out_shapes)

    def body(*refs):
        ins, outs = refs[:n_in], refs[n_in:n_in + n_out]
        part = dot(*ins)
        if nk == 1:
            epilogue(part, ins, outs)
            return
        acc = refs[-1]
        k = pl.program_id(2)

        @pl.when(k == 0)
        def _():
            acc[...] = part

        @pl.when(jnp.logical_and(k > 0, k < nk - 1))
        def _():
            acc[...] += part

        @pl.when(k == nk - 1)
        def _():
            epilogue(acc[...] + part, ins, outs)

    scratch = [pltpu.VMEM(acc_tile, F32)] if nk > 1 else []
    return pl.pallas_call(
        body, name=name, grid=grid, in_specs=in_specs, out_specs=out_specs, out_shape=out_shapes,
        scratch_shapes=scratch, compiler_params=_params(("parallel", "parallel", "arbitrary")),
    )(*in_arrays)


def _store(scale, dtype):
    def epilogue(acc, ins, outs):
        outs[0][...] = (acc * scale if scale != 1.0 else acc).astype(dtype)
    return epilogue


def mm_nn(name, a, w, out_dtype, tm=1024, tn=1024, tk=2048):
    m, kd = a.shape
    n = w.shape[1]
    tm, tn, tk = _tile(m, tm, 8), _tile(n, tn), _tile(kd, tk)
    return _mm(
        name, (n // tn, m // tm, kd // tk), [a, w],
        [pl.BlockSpec((tm, tk), lambda j, i, k: (i, k)), pl.BlockSpec((tk, tn), lambda j, i, k: (k, j))],
        [_sds((m, n), out_dtype)], [pl.BlockSpec((tm, tn), lambda j, i, k: (i, j))], (tm, tn),
        lambda a_ref, w_ref: _dot_nn(a_ref[...], w_ref[...]), _store(1.0, out_dtype))[0]


def mm_nn_resid(name, a, w, x, scale, tm=1024, tn=1024, tk=1408):
    m, kd = a.shape
    n = w.shape[1]
    tm, tn, tk = _tile(m, tm, 8), _tile(n, tn), _tile(kd, tk)

    def epilogue(acc, ins, outs):
        outs[0][...] = ins[2][...] + scale * acc

    return _mm(
        name, (n // tn, m // tm, kd // tk), [a, w, x],
        [pl.BlockSpec((tm, tk), lambda j, i, k: (i, k)), pl.BlockSpec((tk, tn), lambda j, i, k: (k, j)),
         pl.BlockSpec((tm, tn), lambda j, i, k: (i, j))],
        [_sds((m, n), F32)], [pl.BlockSpec((tm, tn), lambda j, i, k: (i, j))], (tm, tn),
        lambda a_ref, w_ref, x_ref: _dot_nn(a_ref[...], w_ref[...]), epilogue)[0]


def mm_nt(name, a, w, out_dtype, scale=1.0, tm=1024, tn=1024, tk=2048):
    m, kd = a.shape
    n = w.shape[0]
    tm, tn, tk = _tile(m, tm, 8), _tile(n, tn), _tile(kd, tk)
    return _mm(
        name, (n // tn, m // tm, kd // tk), [a, w],
        [pl.BlockSpec((tm, tk), lambda j, i, k: (i, k)), pl.BlockSpec((tn, tk), lambda j, i, k: (j, k))],
        [_sds((m, n), out_dtype)], [pl.BlockSpec((tm, tn), lambda j, i, k: (i, j))], (tm, tn),
        lambda a_ref, w_ref: _dot_nt(a_ref[...], w_ref[...]), _store(scale, out_dtype))[0]


def mm_nt_pair(name, a3, w, out_dtype, tm=1024, tn=1024, tk=1408):
    _, m, f = a3.shape
    n = w.shape[0]
    tm, tn, tk = _tile(m, tm, 8), _tile(n, tn), _tile(f, tk)
    nkf = f // tk
    return _mm(
        name, (n // tn, m // tm, 2 * nkf), [a3, w],
        [pl.BlockSpec((None, tm, tk), lambda j, i, k: (k // nkf, i, k % nkf)),
         pl.BlockSpec((tn, tk), lambda j, i, k: (j, k))],
        [_sds((m, n), out_dtype)], [pl.BlockSpec((tm, tn), lambda j, i, k: (i, j))], (tm, tn),
        lambda a_ref, w_ref: _dot_nt(a_ref[...], w_ref[...]), _store(1.0, out_dtype))[0]


def mm_tn(name, a, b, out_dtype, scale=1.0, tm=1024, tn=1024, tk=1024):
    kd, m = a.shape
    n = b.shape[1]
    tm, tn, tk = _tile(m, tm), _tile(n, tn), _tile(kd, tk, 16)
    return _mm(
        name, (n // tn, m // tm, kd // tk), [a, b],
        [pl.BlockSpec((tk, tm), lambda j, i, k: (k, i)), pl.BlockSpec((tk, tn), lambda j, i, k: (k, j))],
        [_sds((m, n), out_dtype)], [pl.BlockSpec((tm, tn), lambda j, i, k: (i, j))], (tm, tn),
        lambda a_ref, b_ref: _dot_tn(a_ref[...], b_ref[...]), _store(scale, out_dtype))[0]


def mm_tn_pair(name, a, b3, out_dtype, tm=1024, tn=1408, tk=1024):
    kd, m = a.shape
    f = b3.shape[2]
    tm, tn, tk = _tile(m, tm), _tile(f, tn), _tile(kd, tk, 16)
    nf = f // tn
    return _mm(
        name, (2 * nf, m // tm, kd // tk), [a, b3],
        [pl.BlockSpec((tk, tm), lambda j, i, k: (k, i)),
         pl.BlockSpec((None, tk, tn), lambda j, i, k: (j // nf, k, j % nf))],
        [_sds((m, 2 * f), out_dtype)], [pl.BlockSpec((tm, tn), lambda j, i, k: (i, j))], (tm, tn),
        lambda a_ref, b_ref: _dot_tn(a_ref[...], b_ref[...]), _store(1.0, out_dtype))[0]


def swiglu_fwd(name, n, w_in, tm=1024, tn=512):
    s, d = n.shape
    f = w_in.shape[1] // 2
    tm, tn = _tile(s, tm, 8), _tile(f, tn)
    nf = f // tn

    def body(n_ref, wg_ref, wu_ref, gu_ref, a_ref):
        nv = n_ref[...]
        g = _dot_nn(nv, wg_ref[...])
        u = _dot_nn(nv, wu_ref[...])
        gu_ref[0] = g.astype(BF16)
        gu_ref[1] = u.astype(BF16)
        a_ref[...] = (g * jax.nn.sigmoid(g) * u).astype(BF16)

    return pl.pallas_call(
        body, name=name, grid=(nf, s // tm),
        in_specs=[pl.BlockSpec((tm, d), lambda j, i: (i, 0)), pl.BlockSpec((d, tn), lambda j, i: (0, j)),
                  pl.BlockSpec((d, tn), lambda j, i: (0, j + nf))],
        out_specs=[pl.BlockSpec((2, tm, tn), lambda j, i: (0, i, j)), pl.BlockSpec((tm, tn), lambda j, i: (i, j))],
        out_shape=[_sds((2, s, f), BF16), _sds((s, f), BF16)],
        compiler_params=_params(("parallel", "parallel")),
    )(n, w_in, w_in)


def swiglu_bwd(name, dh, w_out, gu, scale, tm=1024, tn=512):
    s, d = dh.shape
    f = w_out.shape[0]
    tm, tn = _tile(s, tm, 8), _tile(f, tn)

    def body(dh_ref, w_ref, gu_ref, out_ref):
        da = _dot_nt(dh_ref[...], w_ref[...]) * scale
        g = gu_ref[0].astype(F32)
        u = gu_ref[1].astype(F32)
        sg = jax.nn.sigmoid(g)
        out_ref[0] = (da * u * (sg * (1.0 + g * (1.0 - sg)))).astype(BF16)
        out_ref[1] = (da * (g * sg)).astype(BF16)

    return pl.pallas_call(
        body, name=name, grid=(f // tn, s // tm),
        in_specs=[pl.BlockSpec((tm, d), lambda j, i: (i, 0)), pl.BlockSpec((tn, d), lambda j, i: (j, 0)),
                  pl.BlockSpec((2, tm, tn), lambda j, i: (0, i, j))],
        out_specs=pl.BlockSpec((2, tm, tn), lambda j, i: (0, i, j)),
        out_shape=_sds((2, s, f), BF16),
        compiler_params=_params(("parallel", "parallel")),
    )(dh, w_out, gu)


HALO = 16


def _conv_inputs(z_ref, hgc_ref, hhc_ref, i, cw, tm):
    gc = z_ref[:, cw:2 * cw].astype(F32)
    hc = z_ref[:, 2 * cw:3 * cw].astype(F32)
    cin = gc * hc
    halo = hgc_ref[...].astype(F32) * hhc_ref[...].astype(F32) * (i > 0).astype(F32)
    row = lax.broadcasted_iota(jnp.int32, (tm, cw), 0)
    x1 = jnp.where(row == 0, halo[HALO - 1:HALO], pltpu.roll(cin, 1, 0))
    x2 = jnp.where(row == 0, halo[HALO - 2:HALO - 1], jnp.where(row == 1, halo[HALO - 1:HALO], pltpu.roll(cin, 2, 0)))
    return gc, hc, cin, x1, x2


def _tril(w):
    r = lax.broadcasted_iota(jnp.int32, w.shape, 0)
    c = lax.broadcasted_iota(jnp.int32, w.shape, 1)
    return jnp.where(r >= c, w, jnp.zeros_like(w))


def mixer_fwd(name, z, conv_w, conv_b, g_v, w_s, b_t, tm=256):
    s, zc = z.shape
    cw = conv_w.shape[1]
    gw = g_v.shape[1]
    heads = gw // GROUP
    tm = _tile(s, tm)
    hb = tm // HALO

    def body(z_ref, hgc_ref, hhc_ref, cw_ref, cb_ref, gv_ref, ws_ref, bt_ref, y_ref):
        i = pl.program_id(0)
        _, _, cin, x1, x2 = _conv_inputs(z_ref, hgc_ref, hhc_ref, i, cw, tm)
        cv = cb_ref[...] + cw_ref[2:3, :] * cin + cw_ref[1:2, :] * x1 + cw_ref[0:1, :] * x2
        y_ref[:, 0:cw] = (z_ref[:, 0:cw].astype(F32) * cv).astype(BF16)
        for h in range(heads):
            lo = h * GROUP
            vh = z_ref[:, 3 * cw + gw + lo:3 * cw + gw + lo + GROUP].astype(F32)
            rv = lax.rsqrt(jnp.mean(vh * vh, axis=-1, keepdims=True) + EPS)
            vn = (vh * rv * gv_ref[:, lo:lo + GROUP]).astype(BF16)
            w = _tril(ws_ref[h]).astype(BF16)
            for n in range(tm // GROUP):
                rows = slice(n * GROUP, (n + 1) * GROUP)
                sg = _dot_nn(w, vn[rows]) + bt_ref[:, h:h + 1]
                u = z_ref[rows, 3 * cw + lo:3 * cw + lo + GROUP].astype(F32)
                y_ref[rows, cw + lo:cw + lo + GROUP] = (u * sg).astype(BF16)

    fixed2 = lambda i: (0, 0)
    return pl.pallas_call(
        body, name=name, grid=(s // tm,),
        in_specs=[pl.BlockSpec((tm, zc), lambda i: (i, 0)),
                  pl.BlockSpec((HALO, cw), lambda i: (jnp.maximum(i * hb - 1, 0), 1)),
                  pl.BlockSpec((HALO, cw), lambda i: (jnp.maximum(i * hb - 1, 0), 2)),
                  pl.BlockSpec(conv_w.shape, fixed2), pl.BlockSpec(conv_b.shape, fixed2),
                  pl.BlockSpec(g_v.shape, fixed2), pl.BlockSpec(w_s.shape, lambda i: (0, 0, 0)),
                  pl.BlockSpec(b_t.shape, fixed2)],
        out_specs=pl.BlockSpec((tm, cw + gw), lambda i: (i, 0)),
        out_shape=_sds((s, cw + gw), BF16),
        compiler_params=_params(("arbitrary",)),
    )(z, z, z, conv_w, conv_b, g_v, w_s, b_t)


def mixer_bwd(name, z, dy, conv_w, conv_b, g_v, w_s, b_t, tm=256):
    s, zc = z.shape
    cw = conv_w.shape[1]
    gw = g_v.shape[1]
    heads = gw // GROUP
    tm = _tile(s, tm)
    hb = tm // HALO
    nsteps = s // tm
    last_halo = s // HALO - 1

    def body(z_ref, hgc_ref, hhc_ref, ngb_ref, dy_ref, ndy_ref, cw_ref, cb_ref, gv_ref, ws_ref, bt_ref,
             dz_ref, sm_ref, dws_ref, dbt_ref, dsg_ref):
        i = pl.program_id(0)

        @pl.when(i == 0)
        def _():
            sm_ref[...] = jnp.zeros_like(sm_ref)
            dws_ref[...] = jnp.zeros_like(dws_ref)
            dsg_ref[...] = jnp.zeros_like(dsg_ref)

        gc, hc, cin, x1, x2 = _conv_inputs(z_ref, hgc_ref, hhc_ref, i, cw, tm)
        w0, w1, w2 = cw_ref[0:1, :], cw_ref[1:2, :], cw_ref[2:3, :]
        cv = cb_ref[...] + w2 * cin + w1 * x1 + w0 * x2
        gb = z_ref[:, 0:cw].astype(F32)
        dyc = dy_ref[:, 0:cw].astype(F32)
        dz_ref[:, 0:cw] = (dyc * cv).astype(BF16)
        dcv = dyc * gb
        nxt = ndy_ref[...].astype(F32) * ngb_ref[...].astype(F32) * (i < nsteps - 1).astype(F32)
        row = lax.broadcasted_iota(jnp.int32, (tm, cw), 0)
        d1 = jnp.where(row == tm - 1, nxt[0:1], pltpu.roll(dcv, tm - 1, 0))
        d2 = jnp.where(row == tm - 1, nxt[1:2], jnp.where(row == tm - 2, nxt[0:1], pltpu.roll(dcv, tm - 2, 0)))
        dcin = w2 * dcv + w1 * d1 + w0 * d2
        dz_ref[:, cw:2 * cw] = (dcin * hc).astype(BF16)
        dz_ref[:, 2 * cw:3 * cw] = (dcin * gc).astype(BF16)
        sm_ref[0:1, :] += jnp.sum(dcv * x2, axis=0, keepdims=True)
        sm_ref[1:2, :] += jnp.sum(dcv * x1, axis=0, keepdims=True)
        sm_ref[2:3, :] += jnp.sum(dcv * cin, axis=0, keepdims=True)
        sm_ref[3:4, :] += jnp.sum(dcv, axis=0, keepdims=True)

        for h in range(heads):
            lo = h * GROUP
            vcol = slice(3 * cw + gw + lo, 3 * cw + gw + lo + GROUP)
            ucol = slice(3 * cw + lo, 3 * cw + lo + GROUP)
            vh = z_ref[:, vcol].astype(F32)
            rv = lax.rsqrt(jnp.mean(vh * vh, axis=-1, keepdims=True) + EPS)
            xh = vh * rv
            gvh = gv_ref[:, lo:lo + GROUP]
            vn = (xh * gvh).astype(BF16)
            w = _tril(ws_ref[h]).astype(BF16)
            dgv = jnp.zeros((1, GROUP), F32)
            for n in range(tm // GROUP):
                rows = slice(n * GROUP, (n + 1) * GROUP)
                sg = _dot_nn(w, vn[rows]) + bt_ref[:, h:h + 1]
                dyg = dy_ref[rows, cw + lo:cw + lo + GROUP].astype(F32)
                dsg = dyg * z_ref[rows, ucol].astype(F32)
                dz_ref[rows, ucol] = (dyg * sg).astype(BF16)
                dsgb = dsg.astype(BF16)
                dvn = _dot_tn(w, dsgb)
                dws_ref[h] += _dot_nt(dsgb, vn[rows])
                dsg_ref[:, lo:lo + GROUP] += dsg
                xhc = xh[rows]
                dgv = dgv + jnp.sum(dvn * xhc, axis=0, keepdims=True)
                dxh = dvn * gvh
                dv = rv[rows] * (dxh - xhc * jnp.mean(dxh * xhc, axis=-1, keepdims=True))
                dz_ref[rows, vcol] = dv.astype(BF16)
            sm_ref[4:5, lo:lo + GROUP] += dgv

        @pl.when(i == nsteps - 1)
        def _():
            for h in range(heads):
                dws_ref[h] = _tril(dws_ref[h])
                dbt_ref[:, h:h + 1] = jnp.sum(dsg_ref[:, h * GROUP:(h + 1) * GROUP], axis=-1, keepdims=True)

    fixed2 = lambda i: (0, 0)
    fixed3 = lambda i: (0, 0, 0)
    prev = lambda col: (lambda i: (jnp.maximum(i * hb - 1, 0), col))
    nxt_blk = lambda i: (jnp.minimum((i + 1) * hb, last_halo), 0)
    return pl.pallas_call(
        body, name=name, grid=(nsteps,),
        in_specs=[pl.BlockSpec((tm, zc), lambda i: (i, 0)),
                  pl.BlockSpec((HALO, cw), prev(1)), pl.BlockSpec((HALO, cw), prev(2)),
                  pl.BlockSpec((HALO, cw), nxt_blk),
                  pl.BlockSpec((tm, cw + gw), lambda i: (i, 0)), pl.BlockSpec((HALO, cw), nxt_blk),
                  pl.BlockSpec(conv_w.shape, fixed2), pl.BlockSpec(conv_b.shape, fixed2),
                  pl.BlockSpec(g_v.shape, fixed2), pl.BlockSpec(w_s.shape, fixed3), pl.BlockSpec(b_t.shape, fixed2)],
        out_specs=[pl.BlockSpec((tm, zc), lambda i: (i, 0)), pl.BlockSpec((8, cw), fixed2),
                   pl.BlockSpec(w_s.shape, fixed3), pl.BlockSpec(b_t.shape, fixed2)],
        out_shape=[_sds((s, zc), BF16), _sds((8, cw), F32), _sds(w_s.shape, F32), _sds(b_t.shape, F32)],
        scratch_shapes=[pltpu.VMEM((GROUP, gw), F32)],
        compiler_params=_params(("arbitrary",)),
    )(z, z, z, z, dy, dy, conv_w, conv_b, g_v, w_s, b_t)


def _softmax_rows(sc):
    e = jnp.exp(sc - jnp.max(sc, axis=-1, keepdims=True))
    return e / jnp.sum(e, axis=-1, keepdims=True)


def attn_fwd(name, q, k, v, tm=512):
    s, d = q.shape
    m = k.shape[0]
    hd = d // XA_HEADS
    scale = hd ** -0.5
    tm = _tile(s, tm, 8)

    def body(q_ref, k_ref, v_ref, o_ref):
        for h in range(XA_HEADS):
            cols = slice(h * hd, (h + 1) * hd)
            p = _softmax_rows(_dot_nt(q_ref[:, cols], k_ref[:, cols]) * scale)
            o_ref[:, cols] = _dot_nn(p.astype(BF16), v_ref[:, cols]).astype(BF16)

    return pl.pallas_call(
        body, name=name, grid=(s // tm,),
        in_specs=[pl.BlockSpec((tm, d), lambda i: (i, 0)), pl.BlockSpec((m, d), lambda i: (0, 0)),
                  pl.BlockSpec((m, d), lambda i: (0, 0))],
        out_specs=pl.BlockSpec((tm, d), lambda i: (i, 0)),
        out_shape=_sds((s, d), BF16),
        compiler_params=_params(("arbitrary",)),
    )(q, k, v)


def attn_bwd(name, q, k, v, do, tm=512):
    s, d = q.shape
    m = k.shape[0]
    hd = d // XA_HEADS
    scale = hd ** -0.5
    tm = _tile(s, tm, 8)

    def body(q_ref, k_ref, v_ref, do_ref, dq_ref, dk_ref, dv_ref):
        i = pl.program_id(0)

        @pl.when(i == 0)
        def _():
            dk_ref[...] = jnp.zeros_like(dk_ref)
            dv_ref[...] = jnp.zeros_like(dv_ref)

        for h in range(XA_HEADS):
            cols = slice(h * hd, (h + 1) * hd)
            qh = q_ref[:, cols]
            doh = do_ref[:, cols]
            p = _softmax_rows(_dot_nt(qh, k_ref[:, cols]) * scale)
            dp = _dot_nt(doh, v_ref[:, cols])
            ds = (p * (dp - jnp.sum(dp * p, axis=-1, keepdims=True)) * scale).astype(BF16)
            dq_ref[:, cols] = _dot_nn(ds, k_ref[:, cols]).astype(BF16)
            dk_ref[:, cols] += _dot_tn(ds, qh)
            dv_ref[:, cols] += _dot_tn(p.astype(BF16), doh)

    row = lambda i: (i, 0)
    fixed = lambda i: (0, 0)
    return pl.pallas_call(
        body, name=name, grid=(s // tm,),
        in_specs=[pl.BlockSpec((tm, d), row), pl.BlockSpec((m, d), fixed), pl.BlockSpec((m, d), fixed),
                  pl.BlockSpec((tm, d), row)],
        out_specs=[pl.BlockSpec((tm, d), row), pl.BlockSpec((m, d), fixed), pl.BlockSpec((m, d), fixed)],
        out_shape=[_sds((s, d), BF16), _sds((m, d), F32), _sds((m, d), F32)],
        compiler_params=_params(("arbitrary",)),
    )(q, k, v, do)


def _view(ref, axis, shape, blk=None, half=None):
    n = shape[axis] // N_CHIPS
    hs = shape[1 - axis] // 2
    idx = [slice(None), slice(None)]
    if blk is not None:
        idx[axis] = pl.ds(pl.multiple_of(blk * n, 8), n)
    if half is not None:
        idx[1 - axis] = pl.ds(pl.multiple_of(half * hs, 8), hs)
    return ref.at[tuple(idx)]


def _place():
    x, y, c = lax.axis_index("x"), lax.axis_index("y"), lax.axis_index("c")
    chips = [(1 - x, y), (x, 1 - y), (1 - x, 1 - y)]
    return x, y, c, chips


def _remote(src, dst, send_sem, recv_sem, dev):
    return pltpu.make_async_remote_copy(src_ref=src, dst_ref=dst, send_sem=send_sem, recv_sem=recv_sem,
                                        device_id=dev, device_id_type=MESH)


def gather_weights(name, shards, axes):
    nw = len(shards)
    shapes = []
    for sh, ax in zip(shards, axes):
        full = list(sh.shape)
        full[ax] *= N_CHIPS
        shapes.append(tuple(full))

    def body(*refs):
        srcs, outs = refs[:nw], refs[nw:2 * nw]
        send_sems, recv_sems, loc_sems = refs[2 * nw:]
        x, y, c, chips = _place()
        me = 2 * x + y
        local, sent = [], []
        for w in range(nw):
            ax, shp = axes[w], shapes[w]
            loc = pltpu.make_async_copy(srcs[w], _view(outs[w], ax, shp, blk=me), loc_sems.at[w])
            loc.start()
            local.append(loc)
            for j, (cx, cy) in enumerate(chips):
                cp = _remote(_view(srcs[w], ax, shp, half=c), _view(outs[w], ax, shp, blk=me, half=c),
                             send_sems.at[6 * w + j], recv_sems.at[6 * w + j], (cx, cy, c))
                cp.start()
                sent.append(cp)
        for w in range(nw):
            ax, shp = axes[w], shapes[w]
            for j, (cx, cy) in enumerate(chips):
                landed = _view(outs[w], ax, shp, blk=2 * cx + cy, half=c)
                _remote(landed, landed, send_sems.at[6 * w + j], recv_sems.at[6 * w + j], (cx, cy, c)).wait_recv()
                fwd = _remote(landed, landed, send_sems.at[6 * w + 3 + j], recv_sems.at[6 * w + 3 + j], (x, y, 1 - c))
                fwd.start()
                sent.append(fwd)
        for w in range(nw):
            ax, shp = axes[w], shapes[w]
            for j, (cx, cy) in enumerate(chips):
                passed = _view(outs[w], ax, shp, blk=2 * cx + cy, half=1 - c)
                _remote(passed, passed, send_sems.at[6 * w + 3 + j], recv_sems.at[6 * w + 3 + j],
                        (x, y, 1 - c)).wait_recv()
        for cp in sent:
            cp.wait_send()
        for cp in local:
            cp.wait()

    return pl.pallas_call(
        body, name=name,
        in_specs=[ANY] * nw, out_specs=[ANY] * nw,
        out_shape=[_sds(shp, sh.dtype) for shp, sh in zip(shapes, shards)],
        scratch_shapes=[pltpu.SemaphoreType.DMA((6 * nw,)), pltpu.SemaphoreType.DMA((6 * nw,)),
                        pltpu.SemaphoreType.DMA((nw,))],
        compiler_params=pltpu.CompilerParams(has_side_effects=True),
    )(*shards)


def _half_all(shape, axis):
    out = list(shape)
    out[1 - axis] //= 2
    return tuple(out)


def _block(shape, axis):
    out = list(shape)
    out[axis] //= N_CHIPS
    return tuple(out)


def _half_block(shape, axis):
    return _half_all(_block(shape, axis), axis)


def pair_exchange(name, grads, axes, small):
    nw = len(grads)
    shapes = [g.shape for g in grads]

    def body(*refs):
        srcs, small_ref = refs[:nw], refs[nw]
        peers, all_ref = refs[nw + 1:2 * nw + 1], refs[2 * nw + 1]
        send_sems, recv_sems, ssend_sems, srecv_sems, loc_sem = refs[2 * nw + 2:]
        x, y, c, _ = _place()
        me = 4 * x + 2 * y + c
        sent = []
        for w in range(nw):
            cp = _remote(_view(srcs[w], axes[w], shapes[w], half=1 - c), peers[w], send_sems.at[w], recv_sems.at[w],
                         (x, y, 1 - c))
            cp.start()
            sent.append(cp)
        loc = pltpu.make_async_copy(small_ref, all_ref.at[me], loc_sem)
        loc.start()
        others = []
        for k in range(1, 8):
            px = 1 - x if k & 4 else x
            py = 1 - y if k & 2 else y
            pc = 1 - c if k & 1 else c
            others.append((px, py, pc))
            cp = _remote(small_ref, all_ref.at[me], ssend_sems.at[k - 1], srecv_sems.at[k - 1], (px, py, pc))
            cp.start()
            sent.append(cp)
        for w in range(nw):
            _remote(peers[w], peers[w], send_sems.at[w], recv_sems.at[w], (x, y, 1 - c)).wait_recv()
        for k, (px, py, pc) in enumerate(others):
            slot = all_ref.at[4 * px + 2 * py + pc]
            _remote(slot, slot, ssend_sems.at[k], srecv_sems.at[k], (px, py, pc)).wait_recv()
        for cp in sent:
            cp.wait_send()
        loc.wait()

    outs = pl.pallas_call(
        body, name=name,
        in_specs=[ANY] * (nw + 1), out_specs=[ANY] * (nw + 1),
        out_shape=[_sds(_half_all(s, a), BF16) for s, a in zip(shapes, axes)] + [_sds((8,) + small.shape, small.dtype)],
        scratch_shapes=[pltpu.SemaphoreType.DMA((nw,)), pltpu.SemaphoreType.DMA((nw,)),
                        pltpu.SemaphoreType.DMA((7,)), pltpu.SemaphoreType.DMA((7,)), pltpu.SemaphoreType.DMA],
        compiler_params=pltpu.CompilerParams(has_side_effects=True),
    )(*grads, small)
    return outs[:nw], outs[nw]


def cross_exchange(name, parts, axes, shapes):
    nw = len(parts)

    def body(*refs):
        srcs, lands = refs[:nw], refs[nw:2 * nw]
        send_sems, recv_sems = refs[2 * nw:]
        x, y, c, chips = _place()
        sent = []
        for w in range(nw):
            for j, (cx, cy) in enumerate(chips):
                cp = _remote(_view(srcs[w], axes[w], shapes[w], blk=2 * cx + cy), lands[w].at[j],
                             send_sems.at[3 * w + j], recv_sems.at[3 * w + j], (cx, cy, c))
                cp.start()
                sent.append(cp)
        for w in range(nw):
            for j, (cx, cy) in enumerate(chips):
                slot = lands[w].at[j]
                _remote(slot, slot, send_sems.at[3 * w + j], recv_sems.at[3 * w + j], (cx, cy, c)).wait_recv()
        for cp in sent:
            cp.wait_send()

    return pl.pallas_call(
        body, name=name,
        in_specs=[ANY] * nw, out_specs=[ANY] * nw,
        out_shape=[_sds((3,) + _half_block(s, a), BF16) for s, a in zip(shapes, axes)],
        scratch_shapes=[pltpu.SemaphoreType.DMA((3 * nw,)), pltpu.SemaphoreType.DMA((3 * nw,))],
        compiler_params=pltpu.CompilerParams(has_side_effects=True),
    )(*parts)


def final_exchange(name, grads, axes, shapes):
    nw = len(grads)

    def body(*refs):
        outs = refs[nw:2 * nw]
        send_sems, recv_sems = refs[2 * nw:]
        x, y, c, _ = _place()
        sent = []
        for w in range(nw):
            mine = _view(outs[w], axes[w], shapes[w], half=c)
            cp = _remote(mine, mine, send_sems.at[w], recv_sems.at[w], (x, y, 1 - c))
            cp.start()
            sent.append(cp)
        for w in range(nw):
            theirs = _view(outs[w], axes[w], shapes[w], half=1 - c)
            _remote(theirs, theirs, send_sems.at[w], recv_sems.at[w], (x, y, 1 - c)).wait_recv()
        for cp in sent:
            cp.wait_send()

    return pl.pallas_call(
        body, name=name,
        in_specs=[ANY] * nw, out_specs=[ANY] * nw,
        out_shape=[_sds(g.shape, g.dtype) for g in grads],
        input_output_aliases={w: w for w in range(nw)},
        scratch_shapes=[pltpu.SemaphoreType.DMA((nw,)), pltpu.SemaphoreType.DMA((nw,))],
        compiler_params=pltpu.CompilerParams(has_side_effects=True),
    )(*grads)


def _grid2(rows, cols, row_mult):
    tr, tc = _tile(rows, 512, row_mult), _tile(cols, 2048)
    return tr, tc, rows // tr, cols // tc


def pair_add(name, grad, peer, axis, place):
    hr, hc = peer.shape
    tr, tc, nbr, nbc = _grid2(hr, hc, 16)
    if axis == 1:
        mine = lambda i, j, p: (i + p[1] * nbr, j)
    else:
        mine = lambda i, j, p: (i, j + p[1] * nbc)
    same = lambda i, j, p: (i, j)

    def body(p_ref, g_ref, q_ref, out_ref):
        out_ref[...] = (g_ref[...].astype(F32) + q_ref[...].astype(F32)).astype(BF16)

    return pl.pallas_call(
        body, name=name,
        grid_spec=pltpu.PrefetchScalarGridSpec(
            num_scalar_prefetch=1, grid=(nbr, nbc),
            in_specs=[pl.BlockSpec((tr, tc), mine), pl.BlockSpec((tr, tc), same)],
            out_specs=pl.BlockSpec((tr, tc), same)),
        out_shape=_sds((hr, hc), BF16),
        compiler_params=_params(("parallel", "parallel")),
    )(place, grad, peer)


def cross_sum(name, part, land, axis, shape, place):
    _, sr, sc = land.shape
    tr, tc, nbr, nbc = _grid2(sr, sc, 16)
    if axis == 1:
        own = lambda i, j, p: (i, j + p[0] * nbc)
        dst = lambda i, j, p: (i + p[1] * nbr, j)
    else:
        own = lambda i, j, p: (i + p[0] * nbr, j)
        dst = lambda i, j, p: (i, j + p[1] * nbc)

    def body(p_ref, own_ref, land_ref, out_ref):
        out_ref[...] = ((own_ref[...].astype(F32) + land_ref[0].astype(F32))
                        + (land_ref[1].astype(F32) + land_ref[2].astype(F32)))

    return pl.pallas_call(
        body, name=name,
        grid_spec=pltpu.PrefetchScalarGridSpec(
            num_scalar_prefetch=1, grid=(nbr, nbc),
            in_specs=[pl.BlockSpec((tr, tc), own), pl.BlockSpec((3, tr, tc), lambda i, j, p: (0, i, j))],
            out_specs=pl.BlockSpec((tr, tc), dst)),
        out_shape=_sds(_block(shape, axis), F32),
        compiler_params=_params(("parallel", "parallel")),
    )(place, part, land)


def _adam_math(w, g, m, v):
    m = ADAM_B1 * m + (1.0 - ADAM_B1) * g
    v = ADAM_B2 * v + (1.0 - ADAM_B2) * (g * g)
    m_hat = m / (1.0 - ADAM_B1 ** ADAM_STEP)
    v_hat = v / (1.0 - ADAM_B2 ** ADAM_STEP)
    delta = -ADAM_LR * (m_hat / (jnp.sqrt(v_hat) + ADAM_EPS) + ADAM_WD * w)
    return delta, m, v


def adamw(name, w, g, m, v):
    r, c = w.shape
    tr, tc = _tile(r, 256, 8), _tile(c, 1408)

    def body(w_ref, g_ref, m_ref, v_ref, d_out, m_out, v_out):
        d, mm, vv = _adam_math(w_ref[...], g_ref[...], m_ref[...], v_ref[...])
        d_out[...] = d
        m_out[...] = mm
        v_out[...] = vv

    spec = pl.BlockSpec((tr, tc), lambda i, j: (i, j))
    return pl.pallas_call(
        body, name=name, grid=(r // tr, c // tc),
        in_specs=[spec] * 4, out_specs=[spec] * 3, out_shape=[_sds((r, c), F32)] * 3,
        compiler_params=_params(("parallel", "parallel")),
    )(w, g, m, v)


def small_sum(name, stacked):
    nd, r, c = stacked.shape

    def body(s_ref, out_ref):
        acc = s_ref[0]
        for d in range(1, nd):
            acc = acc + s_ref[d]
        out_ref[...] = acc

    return pl.pallas_call(body, name=name, out_shape=_sds((r, c), F32))(stacked)


def gather_blocks(name, shard):
    r, cols = shard.shape

    def body(src, out, send_sems, recv_sems, loc_sem):
        x, y, c, chips = _place()
        me = 2 * x + y
        place = lambda b: out.at[:, pl.ds(pl.multiple_of(b * cols, 128), cols)]
        loc = pltpu.make_async_copy(src, place(me), loc_sem)
        loc.start()
        sent = []
        for j, (cx, cy) in enumerate(chips):
            cp = _remote(src, place(me), send_sems.at[j], recv_sems.at[j], (cx, cy, c))
            cp.start()
            sent.append(cp)
        for j, (cx, cy) in enumerate(chips):
            got = place(2 * cx + cy)
            _remote(got, got, send_sems.at[j], recv_sems.at[j], (cx, cy, c)).wait_recv()
        for cp in sent:
            cp.wait_send()
        loc.wait()

    return pl.pallas_call(
        body, name=name, in_specs=[ANY], out_specs=ANY, out_shape=_sds((r, N_CHIPS * cols), shard.dtype),
        scratch_shapes=[pltpu.SemaphoreType.DMA((3,)), pltpu.SemaphoreType.DMA((3,)), pltpu.SemaphoreType.DMA],
        compiler_params=pltpu.CompilerParams(has_side_effects=True),
    )(shard)


WEIGHTS = ["g_ffn1", "w_ffn1_in", "w_ffn1_out", "g_mix", "w_mix_in", "conv_w", "conv_b", "g_gm_v", "w_spatial",
           "b_spatial", "w_mix_out", "g_xattn", "g_mem", "w_xq", "w_xk", "w_xv", "w_xo", "g_ffn2", "w_ffn2_in",
           "w_ffn2_out", "g_final"]
BIG = {"w_ffn1_in": 1, "w_ffn1_out": 0, "w_mix_in": 1, "w_mix_out": 0, "w_xq": 0, "w_xk": 0, "w_xv": 0, "w_xo": 0,
       "w_ffn2_in": 1, "w_ffn2_out": 0}
SMALL = [n for n in WEIGHTS if n not in BIG]


def _pack(arrays):
    flat = jnp.concatenate([a.reshape(-1) for a in arrays])
    rows = -(-flat.shape[0] // 1024) * 8
    return jnp.pad(flat, (0, rows * 128 - flat.shape[0])).reshape(rows, 128)


def _unpack(buf, shapes):
    flat = buf.reshape(-1)
    out, pos = [], 0
    for shp in shapes:
        n = math.prod(shp)
        out.append(flat[pos:pos + n].reshape(shp))
        pos += n
    return out


def kernel(x, mem, g_ffn1, w_ffn1_in, w_ffn1_out, g_mix, w_mix_in, conv_w, conv_b, g_gm_v, w_spatial, b_spatial, w_mix_out, g_xattn, g_mem, w_xq, w_xk, w_xv, w_xo, g_ffn2, w_ffn2_in, w_ffn2_out, g_final, loss_target, m_g_ffn1, m_w_ffn1_in, m_w_ffn1_out, m_g_mix, m_w_mix_in, m_conv_w, m_conv_b, m_g_gm_v, m_w_spatial, m_b_spatial, m_w_mix_out, m_g_xattn, m_g_mem, m_w_xq, m_w_xk, m_w_xv, m_w_xo, m_g_ffn2, m_w_ffn2_in, m_w_ffn2_out, m_g_final, v_g_ffn1, v_w_ffn1_in, v_w_ffn1_out, v_g_mix, v_w_mix_in, v_conv_w, v_conv_b, v_g_gm_v, v_w_spatial, v_b_spatial, v_w_mix_out, v_g_xattn, v_g_mem, v_w_xq, v_w_xk, v_w_xv, v_w_xo, v_g_ffn2, v_w_ffn2_in, v_w_ffn2_out, v_g_final):
    given = dict(locals())
    wts = {n: given[n] for n in WEIGHTS}
    mom = {n: given["m_" + n] for n in WEIGHTS}
    var = {n: given["v_" + n] for n in WEIGHTS}

    xi, yi, ci = lax.axis_index("x"), lax.axis_index("y"), lax.axis_index("c")
    blk = 2 * xi + yi
    place = jnp.stack([blk, ci]).astype(jnp.int32)

    x2, mem2, tgt = x[0], mem[0], loss_target[0]
    big_names = list(BIG)
    axes = [BIG[n] for n in big_names]

    full = dict(zip(big_names, gather_weights("gather_weights", [wts[n][0].astype(BF16) for n in big_names], axes)))
    conv_taps = gather_blocks("gather_conv_taps", jnp.pad(conv_w[0], ((0, 8 - CONV_K), (0, 0))))
    w_s, b_t = w_spatial[0], b_spatial[0].T
    gf = g_final[None]

    n1, r1 = rmsnorm_fwd("norm1", x2, g_ffn1)
    gu1, a1 = swiglu_fwd("ffn1_in", n1, full["w_ffn1_in"])
    h1 = mm_nn_resid("ffn1_out", a1, full["w_ffn1_out"], x2, 0.5)
    n2, r2 = rmsnorm_fwd("norm2", h1, g_mix)
    z = mm_nn("mix_in", n2, full["w_mix_in"], BF16)
    ycat = mixer_fwd("mixer", z, conv_taps, conv_b, g_gm_v, w_s, b_t)
    h2 = mm_nn_resid("mix_out", ycat, full["w_mix_out"], h1, 1.0, tk=2048)
    n3, r3 = rmsnorm_fwd("norm3", h2, g_xattn)
    mn, rm = rmsnorm_fwd("norm_mem", mem2, g_mem)
    q = mm_nn("xq", n3, full["w_xq"], BF16)
    k = mm_nn("xk", mn, full["w_xk"], BF16)
    v = mm_nn("xv", mn, full["w_xv"], BF16)
    o = attn_fwd("attn", q, k, v)
    h3 = mm_nn_resid("xo", o, full["w_xo"], h2, 1.0, tk=2048)
    n4, r4 = rmsnorm_fwd("norm4", h3, g_ffn2)
    gu2, a2 = swiglu_fwd("ffn2_in", n4, full["w_ffn2_in"])
    h4 = mm_nn_resid("ffn2_out", a2, full["w_ffn2_out"], h3, 0.5)
    loss_blk, dh4, dh4b, dg_final = loss_head("loss_head", h4, gf, tgt)

    dw = {}
    dgu2 = swiglu_bwd("ffn2_dact", dh4b, full["w_ffn2_out"], gu2, 0.5)
    dw["w_ffn2_out"] = mm_tn("ffn2_dwout", a2, dh4b, BF16, scale=0.5)
    dw["w_ffn2_in"] = mm_tn_pair("ffn2_dwin", n4, dgu2, BF16)
    dn4 = mm_nt_pair("ffn2_dn", dgu2, full["w_ffn2_in"], F32)
    dh3, dh3b, dg_ffn2 = rmsnorm_bwd("norm4_bwd", dn4, h3, r4, g_ffn2, dh4)

    dw["w_xo"] = mm_tn("xo_dw", o, dh3b, BF16)
    do = mm_nt("xo_dx", dh3b, full["w_xo"], BF16)
    dq, dk, dv = attn_bwd("attn_bwd", q, k, v, do)
    dkb, dvb = dk.astype(BF16), dv.astype(BF16)
    dw["w_xq"] = mm_tn("xq_dw", n3, dq, BF16)
    dn3 = mm_nt("xq_dx", dq, full["w_xq"], F32)
    dh2, dh2b, dg_xattn = rmsnorm_bwd("norm3_bwd", dn3, h2, r3, g_xattn, dh3)
    dw["w_xk"] = mm_tn("xk_dw", mn, dkb, BF16)
    dw["w_xv"] = mm_tn("xv_dw", mn, dvb, BF16)
    dmn_k = mm_nt("xk_dx", dkb, full["w_xk"], F32)
    dmn_v = mm_nt("xv_dx", dvb, full["w_xv"], F32)
    dg_mem = gain_grad("norm_mem_bwd", dmn_k, dmn_v, mem2, rm)

    dw["w_mix_out"] = mm_tn("mix_out_dw", ycat, dh2b, BF16)
    dycat = mm_nt("mix_out_dx", dh2b, full["w_mix_out"], BF16)
    dz, dsmall, dws, dbt = mixer_bwd("mixer_bwd", z, dycat, conv_taps, conv_b, g_gm_v, w_s, b_t)
    dw["w_mix_in"] = mm_tn("mix_in_dw", n2, dz, BF16)
    dn2 = mm_nt("mix_in_dx", dz, full["w_mix_in"], F32, tk=1280)
    dh1, dh1b, dg_mix = rmsnorm_bwd("norm2_bwd", dn2, h1, r2, g_mix, dh2)

    dgu1 = swiglu_bwd("ffn1_dact", dh1b, full["w_ffn1_out"], gu1, 0.5)
    dw["w_ffn1_out"] = mm_tn("ffn1_dwout", a1, dh1b, BF16, scale=0.5)
    dw["w_ffn1_in"] = mm_tn_pair("ffn1_dwin", n1, dgu1, BF16)
    dn1 = mm_nt_pair("ffn1_dn", dgu1, full["w_ffn1_in"], F32)
    dx, _, dg_ffn1 = rmsnorm_bwd("norm1_bwd", dn1, x2, r1, g_ffn1, dh1)

    local_small = {"g_ffn1": dg_ffn1, "g_mix": dg_mix, "conv_w": dsmall[0:CONV_K], "conv_b": dsmall[3:4],
                   "g_gm_v": dsmall[4:5], "w_spatial": dws, "b_spatial": dbt.T, "g_xattn": dg_xattn,
                   "g_mem": dg_mem, "g_ffn2": dg_ffn2, "g_final": dg_final}
    full_shapes = [dw[n].shape for n in big_names]
    peers, small_all = pair_exchange("pair_exchange", [dw[n] for n in big_names], axes,
                                     _pack([local_small[n] for n in SMALL]))
    parts = [pair_add("pair_add_" + n, dw[n], peers[i], axes[i], place) for i, n in enumerate(big_names)]
    lands = cross_exchange("cross_exchange", parts, axes, full_shapes)
    halves = [cross_sum("cross_sum_" + n, parts[i], lands[i], axes[i], full_shapes[i], place)
              for i, n in enumerate(big_names)]
    grads = dict(zip(big_names, final_exchange("final_exchange", halves, axes, full_shapes)))

    small_shapes = [local_small[n].shape for n in SMALL]
    for n, g in zip(SMALL, _unpack(small_sum("small_sum", small_all), small_shapes)):
        grads[n] = g
    taps_cols = conv_w.shape[2]
    grads["conv_w"] = lax.dynamic_slice_in_dim(grads["conv_w"], blk * taps_cols, taps_cols, axis=1)

    delta, new_m, new_v = {}, {}, {}
    for n in big_names:
        delta[n], new_m[n], new_v[n] = adamw("adamw_" + n, wts[n][0], grads[n], mom[n][0], var[n][0])
    packed = [_pack([src[n] for n in SMALL]) for src in (wts, grads, mom, var)]
    own_shapes = [wts[n].shape for n in SMALL]
    for dst, buf in zip((delta, new_m, new_v), adamw("adamw_small", *packed)):
        for n, a in zip(SMALL, _unpack(buf, own_shapes)):
            dst[n] = a

    loss = lax.psum(loss_blk[0, 0], ("x", "y", "c"))
    outs = [loss, dx[None]]
    for group in (grads, delta, new_m, new_v):
        outs += [group[n].reshape(wts[n].shape) for n in WEIGHTS]
    return tuple(outs)
```

```python
import math

import jax
import jax.numpy as jnp
from jax import lax
from jax.experimental import pallas as pl
from jax.experimental.pallas import tpu as pltpu

F32 = jnp.float32
BF16 = jnp.bfloat16
EPS = 1e-6
GROUP = 128
XA_HEADS = 4
CONV_K = 3
N_CHIPS = 4
VMEM_LIMIT_BYTES = 56 * 1024 * 1024

ADAM_LR = 0.001
ADAM_B1 = 0.9
ADAM_B2 = 0.999
ADAM_EPS = 1e-08
ADAM_WD = 0.01
ADAM_STEP = 10

MESH = pl.DeviceIdType.MESH
ANY = pl.BlockSpec(memory_space=pl.ANY)


def _tile(dim, pref, mult=128):
    if dim <= pref:
        return dim
    t = (pref // mult) * mult
    while t >= mult:
        if dim % t == 0:
            return t
        t -= mult
    raise ValueError(f"no tile for {dim} under {pref}")


def _params(sem):
    return pltpu.CompilerParams(dimension_semantics=sem, vmem_limit_bytes=VMEM_LIMIT_BYTES)


def _sds(shape, dtype):
    return jax.ShapeDtypeStruct(shape, dtype)


def _dot_nn(a, b):
    return jnp.dot(a, b, preferred_element_type=F32)


def _dot_nt(a, b):
    return lax.dot_general(a, b, (((1,), (1,)), ((), ())), preferred_element_type=F32)


def _dot_tn(a, b):
    return lax.dot_general(a, b, (((0,), (0,)), ((), ())), preferred_element_type=F32)


class Job:
    def __init__(self, inputs, out_shapes, aliases, sems, start, middle, finish):
        self.inputs, self.out_shapes, self.aliases, self.sems = inputs, out_shapes, aliases, sems
        self.start, self.middle, self.finish = start, middle, finish


def _place():
    x, y, c = lax.axis_index("x"), lax.axis_index("y"), lax.axis_index("c")
    chips = [(1 - x, y), (x, 1 - y), (1 - x, 1 - y)]
    return x, y, c, chips


def _ds(start, size, lane):
    if not isinstance(start, int):
        start = pl.multiple_of(start, 128 if lane else 16)
    return pl.ds(start, size)


WHOLE = (0, 1, 1)


def _window(ref, axis, shape, blk=None, half=None, sub=WHOLE):
    n = shape[axis] // N_CHIPS
    hs = shape[1 - axis] // 2
    idx = [slice(None), slice(None)]
    if blk is not None:
        idx[axis] = _ds(blk * n, n, axis == 1)
    first, count, pieces = sub
    ext = hs // pieces
    if half is not None:
        idx[1 - axis] = _ds(half * hs + first * ext, count * ext, axis == 0)
    elif pieces > 1:
        idx[1 - axis] = _ds(first * ext, count * ext, axis == 0)
    return ref.at[tuple(idx)]


def _remote(src, dst, send_sem, recv_sem, dev):
    return pltpu.make_async_remote_copy(src_ref=src, dst_ref=dst, send_sem=send_sem, recv_sem=recv_sem,
                                        device_id=dev, device_id_type=MESH)


def _full_shape(block_shape, axis):
    out = list(block_shape)
    out[axis] *= N_CHIPS
    return tuple(out)


def _half_all(shape, axis):
    out = list(shape)
    out[1 - axis] //= 2
    return tuple(out)


def _block(shape, axis):
    out = list(shape)
    out[axis] //= N_CHIPS
    return tuple(out)


def _half_block(shape, axis):
    return _half_all(_block(shape, axis), axis)


def gather_job(items):
    nw = len(items)
    shapes = [_full_shape(b.shape, ax) for b, ax, _, _ in items]
    inputs, aliases, prev_at = [], {}, []
    for w, (blk_arr, ax, prev, sub) in enumerate(items):
        inputs.append(blk_arr)
        if prev is not None:
            aliases[len(inputs)] = w
            inputs.append(prev)
        prev_at.append(prev is not None)

    def start(pos, ins, outs, sems):
        x, y, c, chips = pos
        send_sems, recv_sems, loc_sems = sems
        me = 2 * x + y
        srcs = [r for r in ins]
        k = 0
        for w, (_, ax, prev, sub) in enumerate(items):
            src = srcs[k]
            k += 2 if prev is not None else 1
            if prev is None:
                pltpu.make_async_copy(src, _window(outs[w], ax, shapes[w], blk=me), loc_sems.at[w]).start()
            for j, (cx, cy) in enumerate(chips):
                _remote(_window(src, ax, shapes[w], half=c, sub=sub),
                        _window(outs[w], ax, shapes[w], blk=me, half=c, sub=sub),
                        send_sems.at[6 * w + j], recv_sems.at[6 * w + j], (cx, cy, c)).start()

    def middle(pos, ins, outs, sems):
        x, y, c, chips = pos
        send_sems, recv_sems, _ = sems
        for w, (_, ax, _, sub) in enumerate(items):
            for j, (cx, cy) in enumerate(chips):
                landed = _window(outs[w], ax, shapes[w], blk=2 * cx + cy, half=c, sub=sub)
                _remote(landed, landed, send_sems.at[6 * w + j], recv_sems.at[6 * w + j], (cx, cy, c)).wait_recv()
                _remote(landed, landed, send_sems.at[6 * w + 3 + j], recv_sems.at[6 * w + 3 + j],
                        (x, y, 1 - c)).start()

    def finish(pos, ins, outs, sems):
        x, y, c, chips = pos
        send_sems, recv_sems, loc_sems = sems
        me = 2 * x + y
        k = 0
        for w, (_, ax, prev, sub) in enumerate(items):
            src = ins[k]
            k += 2 if prev is not None else 1
            for j, (cx, cy) in enumerate(chips):
                passed = _window(outs[w], ax, shapes[w], blk=2 * cx + cy, half=1 - c, sub=sub)
                _remote(passed, passed, send_sems.at[6 * w + 3 + j], recv_sems.at[6 * w + 3 + j],
                        (x, y, 1 - c)).wait_recv()
            for j in range(6):
                one = _window(outs[w], ax, shapes[w], blk=me, half=c, sub=sub)
                _remote(one, one, send_sems.at[6 * w + j], recv_sems.at[6 * w + j], (x, y, 1 - c)).wait_send()
            if prev is None:
                pltpu.make_async_copy(src, _window(outs[w], ax, shapes[w], blk=me), loc_sems.at[w]).wait()

    sems = [pltpu.SemaphoreType.DMA((6 * nw,)), pltpu.SemaphoreType.DMA((6 * nw,)), pltpu.SemaphoreType.DMA((nw,))]
    return Job(inputs, [_sds(s, it[0].dtype) for s, it in zip(shapes, items)], aliases, sems, start, middle, finish)


def pair_job(grads, axes):
    nw = len(grads)
    shapes = [g.shape for g in grads]

    def start(pos, ins, outs, sems):
        x, y, c, _ = pos
        for w in range(nw):
            _remote(_window(ins[w], axes[w], shapes[w], half=1 - c), outs[w], sems[0].at[w], sems[1].at[w],
                    (x, y, 1 - c)).start()

    def finish(pos, ins, outs, sems):
        x, y, c, _ = pos
        for w in range(nw):
            cp = _remote(outs[w], outs[w], sems[0].at[w], sems[1].at[w], (x, y, 1 - c))
            cp.wait_recv()
            cp.wait_send()

    sems = [pltpu.SemaphoreType.DMA((nw,)), pltpu.SemaphoreType.DMA((nw,))]
    return Job(list(grads), [_sds(_half_all(s, a), BF16) for s, a in zip(shapes, axes)], {}, sems, start, None,
               finish)


def cross_job(items):
    nw = len(items)
    inputs, aliases = [], {}
    for w, (part, ax, shape, prev, sub) in enumerate(items):
        inputs.append(part)
        if prev is not None:
            aliases[len(inputs)] = w
            inputs.append(prev)

    def copies(pos, ins, outs, sems):
        x, y, c, chips = pos
        k = 0
        for w, (_, ax, shape, prev, sub) in enumerate(items):
            src = ins[k]
            k += 2 if prev is not None else 1
            for j, (cx, cy) in enumerate(chips):
                slot = _window(outs[w].at[j], ax, shape, sub=sub)
                yield (_remote(_window(src, ax, shape, blk=2 * cx + cy, sub=sub), slot,
                               sems[0].at[3 * w + j], sems[1].at[3 * w + j], (cx, cy, c)),
                       _remote(slot, slot, sems[0].at[3 * w + j], sems[1].at[3 * w + j], (cx, cy, c)))

    def start(pos, ins, outs, sems):
        for send, _ in copies(pos, ins, outs, sems):
            send.start()

    def finish(pos, ins, outs, sems):
        for send, recv in copies(pos, ins, outs, sems):
            recv.wait_recv()
            send.wait_send()

    sems = [pltpu.SemaphoreType.DMA((3 * nw,)), pltpu.SemaphoreType.DMA((3 * nw,))]
    out_shapes = [_sds((3,) + _half_block(shape, ax), BF16) for _, ax, shape, _, _ in items]
    return Job(inputs, out_shapes, aliases, sems, start, None, finish)


def final_job(blocks, axes, shapes):
    nw = len(blocks)

    def start(pos, ins, outs, sems):
        x, y, c, _ = pos
        for w in range(nw):
            mine = _window(outs[w], axes[w], shapes[w], half=c)
            _remote(mine, mine, sems[0].at[w], sems[1].at[w], (x, y, 1 - c)).start()

    def finish(pos, ins, outs, sems):
        x, y, c, _ = pos
        for w in range(nw):
            theirs = _window(outs[w], axes[w], shapes[w], half=1 - c)
            cp = _remote(theirs, theirs, sems[0].at[w], sems[1].at[w], (x, y, 1 - c))
            cp.wait_recv()
            cp.wait_send()

    sems = [pltpu.SemaphoreType.DMA((nw,)), pltpu.SemaphoreType.DMA((nw,))]
    return Job(list(blocks), [_sds(b.shape, b.dtype) for b in blocks], {w: w for w in range(nw)}, sems, start, None,
               finish)


def stack_job(small):
    def peers(pos):
        x, y, c, _ = pos
        for k in range(1, 8):
            yield k - 1, (1 - x if k & 4 else x, 1 - y if k & 2 else y, 1 - c if k & 1 else c)

    def start(pos, ins, outs, sems):
        x, y, c, _ = pos
        mine = outs[0].at[4 * x + 2 * y + c]
        pltpu.make_async_copy(ins[0], mine, sems[2]).start()
        for k, dev in peers(pos):
            _remote(ins[0], mine, sems[0].at[k], sems[1].at[k], dev).start()

    def finish(pos, ins, outs, sems):
        x, y, c, _ = pos
        for k, (px, py, pc) in peers(pos):
            slot = outs[0].at[4 * px + 2 * py + pc]
            cp = _remote(slot, slot, sems[0].at[k], sems[1].at[k], (px, py, pc))
            cp.wait_recv()
            cp.wait_send()
        pltpu.make_async_copy(ins[0], outs[0].at[4 * x + 2 * y + c], sems[2]).wait()

    sems = [pltpu.SemaphoreType.DMA((7,)), pltpu.SemaphoreType.DMA((7,)), pltpu.SemaphoreType.DMA]
    return Job([small], [_sds((8,) + small.shape, small.dtype)], {}, sems, start, None, finish)


def columns_job(block):
    cols = block.shape[1]
    place = lambda out, b: out.at[:, _ds(b * cols, cols, True)]

    def start(pos, ins, outs, sems):
        x, y, c, chips = pos
        pltpu.make_async_copy(ins[0], place(outs[0], 2 * x + y), sems[2]).start()
        for j, (cx, cy) in enumerate(chips):
            _remote(ins[0], place(outs[0], 2 * x + y), sems[0].at[j], sems[1].at[j], (cx, cy, c)).start()

    def finish(pos, ins, outs, sems):
        x, y, c, chips = pos
        for j, (cx, cy) in enumerate(chips):
            got = place(outs[0], 2 * cx + cy)
            cp = _remote(got, got, sems[0].at[j], sems[1].at[j], (cx, cy, c))
            cp.wait_recv()
            cp.wait_send()
        pltpu.make_async_copy(ins[0], place(outs[0], 2 * x + y), sems[2]).wait()

    sems = [pltpu.SemaphoreType.DMA((3,)), pltpu.SemaphoreType.DMA((3,)), pltpu.SemaphoreType.DMA]
    return Job([block], [_sds((block.shape[0], N_CHIPS * cols), block.dtype)], {}, sems, start, None, finish)


def _call(name, body, grid, in_specs, out_specs, out_shape, args, scratch=(), sem=None, jobs=()):
    n_in, n_out, n_sc = len(args), len(out_shape), len(scratch)
    if not jobs:
        outs = pl.pallas_call(
            body, name=name, grid=grid, in_specs=in_specs, out_specs=out_specs, out_shape=out_shape,
            scratch_shapes=list(scratch), compiler_params=_params(sem))(*args)
        return list(outs), []

    total = math.prod(grid) if grid else 1
    mid = min(total - 1, (3 * total) // 4)

    def split(refs, start, counts):
        out = []
        for n in counts:
            out.append(refs[start:start + n])
            start += n
        return out, start

    def wrapped(*refs):
        c_in = refs[:n_in]
        j_ins, p = split(refs, n_in, [len(j.inputs) for j in jobs])
        c_out = refs[p:p + n_out]
        j_outs, p = split(refs, p + n_out, [len(j.out_shapes) for j in jobs])
        c_sc = refs[p:p + n_sc]
        j_sems, p = split(refs, p + n_sc, [len(j.sems) for j in jobs])
        pos = _place()
        step = 0
        for axis, extent in enumerate(grid):
            step = step * extent + pl.program_id(axis)

        def run(phase):
            for j, ins, outs, sems in zip(jobs, j_ins, j_outs, j_sems):
                fn = getattr(j, phase)
                if fn is not None:
                    fn(pos, ins, outs, sems)

        if total == 1:
            run("start")
            body(*c_in, *c_out, *c_sc)
            run("middle")
            run("finish")
            return
        pl.when(step == 0)(lambda: run("start"))
        body(*c_in, *c_out, *c_sc)
        if any(j.middle is not None for j in jobs):
            pl.when(step == mid)(lambda: run("middle"))
        pl.when(step == total - 1)(lambda: run("finish"))

    aliases, in_at, out_at = {}, n_in, n_out
    for j in jobs:
        for src, dst in j.aliases.items():
            aliases[in_at + src] = out_at + dst
        in_at += len(j.inputs)
        out_at += len(j.out_shapes)
    outs = pl.pallas_call(
        wrapped, name=name, grid=grid,
        in_specs=list(in_specs) + [ANY] * (in_at - n_in),
        out_specs=list(out_specs) + [ANY] * (out_at - n_out),
        out_shape=list(out_shape) + [s for j in jobs for s in j.out_shapes],
        scratch_shapes=list(scratch) + [s for j in jobs for s in j.sems],
        input_output_aliases=aliases,
        compiler_params=_params(("arbitrary",) * len(grid)),
    )(*args, *[a for j in jobs for a in j.inputs])
    job_outs, p = split(outs, n_out, [len(j.out_shapes) for j in jobs])
    return list(outs[:n_out]), [list(o) for o in job_outs]


def comm_only(name, jobs):
    def body(dummy_ref, out_ref):
        out_ref[...] = dummy_ref[...]

    dummy = jnp.zeros((8, 128), F32)
    spec = pl.BlockSpec((8, 128), lambda: (0, 0))
    return _call(name, body, (), [spec], [spec], [_sds((8, 128), F32)], [dummy], jobs=jobs)[1]


def _ret(outs, job_outs, jobs, single=True):
    res = outs[0] if single else outs
    return (res, job_outs) if jobs else res


def rmsnorm_fwd(name, x, g):
    s, d = x.shape
    tm = _tile(s, 512, 8)

    def body(x_ref, g_ref, n_ref, r_ref):
        xv = x_ref[...]
        r = lax.rsqrt(jnp.mean(xv * xv, axis=-1, keepdims=True) + EPS)
        n_ref[...] = (xv * r * g_ref[...]).astype(BF16)
        r_ref[...] = r

    row = lambda i: (i, 0)
    return _call(
        name, body, (s // tm,),
        [pl.BlockSpec((tm, d), row), pl.BlockSpec((1, d), lambda i: (0, 0))],
        [pl.BlockSpec((tm, d), row), pl.BlockSpec((tm, 1), row)],
        [_sds((s, d), BF16), _sds((s, 1), F32)], [x, g], sem=("arbitrary",))[0]


def rmsnorm_bwd(name, dn, x, r, g, dh_in, jobs=()):
    s, d = x.shape
    tm = _tile(s, 512, 8)

    def body(dn_ref, x_ref, r_ref, g_ref, dh_ref, out_ref, outb_ref, dg_ref):
        i = pl.program_id(0)
        xh = x_ref[...] * r_ref[...]
        dnv = dn_ref[...]
        dxh = dnv * g_ref[...]
        dx = r_ref[...] * (dxh - xh * jnp.mean(dxh * xh, axis=-1, keepdims=True))
        out = dh_ref[...] + dx
        out_ref[...] = out
        outb_ref[...] = out.astype(BF16)
        part = jnp.sum(dnv * xh, axis=0, keepdims=True)

        @pl.when(i == 0)
        def _():
            dg_ref[...] = part

        @pl.when(i > 0)
        def _():
            dg_ref[...] += part

    row = lambda i: (i, 0)
    fixed = lambda i: (0, 0)
    outs, job_outs = _call(
        name, body, (s // tm,),
        [pl.BlockSpec((tm, d), row), pl.BlockSpec((tm, d), row), pl.BlockSpec((tm, 1), row),
         pl.BlockSpec((1, d), fixed), pl.BlockSpec((tm, d), row)],
        [pl.BlockSpec((tm, d), row), pl.BlockSpec((tm, d), row), pl.BlockSpec((1, d), fixed)],
        [_sds((s, d), F32), _sds((s, d), BF16), _sds((1, d), F32)], [dn, x, r, g, dh_in],
        sem=("arbitrary",), jobs=jobs)
    return _ret(outs, job_outs, jobs, single=False)


def gain_grad(name, dn_a, dn_b, x, r):
    s, d = x.shape
    tm = _tile(s, 512, 8)

    def body(a_ref, b_ref, x_ref, r_ref, dg_ref):
        i = pl.program_id(0)
        part = jnp.sum((a_ref[...] + b_ref[...]) * (x_ref[...] * r_ref[...]), axis=0, keepdims=True)

        @pl.when(i == 0)
        def _():
            dg_ref[...] = part

        @pl.when(i > 0)
        def _():
            dg_ref[...] += part

    row = lambda i: (i, 0)
    return _call(
        name, body, (s // tm,),
        [pl.BlockSpec((tm, d), row), pl.BlockSpec((tm, d), row), pl.BlockSpec((tm, d), row),
         pl.BlockSpec((tm, 1), row)],
        [pl.BlockSpec((1, d), lambda i: (0, 0))], [_sds((1, d), F32)], [dn_a, dn_b, x, r],
        sem=("arbitrary",))[0][0]


def loss_head(name, h, g, target):
    s, d = h.shape
    tm = _tile(s, 512, 8)
    nsteps = s // tm

    def body(h_ref, g_ref, t_ref, loss_ref, dh_ref, dhb_ref, dg_ref, sq_ref):
        i = pl.program_id(0)
        hv = h_ref[...]
        gv = g_ref[...]
        r = lax.rsqrt(jnp.mean(hv * hv, axis=-1, keepdims=True) + EPS)
        xh = hv * r
        err = xh * gv - t_ref[...]
        dy = err * (1.0 / d)
        dxh = dy * gv
        dh = r * (dxh - xh * jnp.mean(dxh * xh, axis=-1, keepdims=True))
        dh_ref[...] = dh
        dhb_ref[...] = dh.astype(BF16)
        dg_part = jnp.sum(dy * xh, axis=0, keepdims=True)
        sq_part = jnp.sum(err * err, axis=0, keepdims=True)

        @pl.when(i == 0)
        def _():
            dg_ref[...] = dg_part
            sq_ref[...] = sq_part

        @pl.when(i > 0)
        def _():
            dg_ref[...] += dg_part
            sq_ref[...] += sq_part

        @pl.when(i == nsteps - 1)
        def _():
            total = jnp.sum(sq_ref[...], axis=-1, keepdims=True) * (0.5 / d)
            loss_ref[...] = jnp.broadcast_to(total, loss_ref.shape)

    row = lambda i: (i, 0)
    fixed = lambda i: (0, 0)
    return _call(
        name, body, (nsteps,),
        [pl.BlockSpec((tm, d), row), pl.BlockSpec((1, d), fixed), pl.BlockSpec((tm, d), row)],
        [pl.BlockSpec((8, 128), fixed), pl.BlockSpec((tm, d), row), pl.BlockSpec((tm, d), row),
         pl.BlockSpec((1, d), fixed)],
        [_sds((8, 128), F32), _sds((s, d), F32), _sds((s, d), BF16), _sds((1, d), F32)], [h, g, target],
        scratch=[pltpu.VMEM((1, d), F32)], sem=("arbitrary",))[0]


def _mm(name, grid, in_arrays, in_specs, out_shapes, out_specs, acc_tile, dot, epilogue, jobs=()):
    nk = grid[2]
    n_in = len(in_arrays)
    n_out = len(out_shapes)

    def body(*refs):
        ins, outs = refs[:n_in], refs[n_in:n_in + n_out]
        part = dot(*ins)
        if nk == 1:
            epilogue(part, ins, outs)
            return
        acc = refs[n_in + n_out]
        k = pl.program_id(2)

        @pl.when(k == 0)
        def _():
            acc[...] = part

        @pl.when(jnp.logical_and(k > 0, k < nk - 1))
        def _():
            acc[...] += part

        @pl.when(k == nk - 1)
        def _():
            epilogue(acc[...] + part, ins, outs)

    scratch = [pltpu.VMEM(acc_tile, F32)] if nk > 1 else []
    outs, job_outs = _call(name, body, grid, in_specs, out_specs, out_shapes, in_arrays, scratch=scratch,
                           sem=("parallel", "parallel", "arbitrary"), jobs=jobs)
    return _ret(outs, job_outs, jobs)


def _store(scale, dtype):
    def epilogue(acc, ins, outs):
        outs[0][...] = (acc * scale if scale != 1.0 else acc).astype(dtype)
    return epilogue


def mm_nn(name, a, w, out_dtype, tm=1024, tn=1024, tk=2048, jobs=()):
    m, kd = a.shape
    n = w.shape[1]
    tm, tn, tk = _tile(m, tm, 8), _tile(n, tn), _tile(kd, tk)
    return _mm(
        name, (n // tn, m // tm, kd // tk), [a, w],
        [pl.BlockSpec((tm, tk), lambda j, i, k: (i, k)), pl.BlockSpec((tk, tn), lambda j, i, k: (k, j))],
        [_sds((m, n), out_dtype)], [pl.BlockSpec((tm, tn), lambda j, i, k: (i, j))], (tm, tn),
        lambda a_ref, w_ref: _dot_nn(a_ref[...], w_ref[...]), _store(1.0, out_dtype), jobs)


def mm_nn_resid(name, a, w, x, scale, tm=1024, tn=1024, tk=1408, jobs=()):
    m, kd = a.shape
    n = w.shape[1]
    tm, tn, tk = _tile(m, tm, 8), _tile(n, tn), _tile(kd, tk)

    def epilogue(acc, ins, outs):
        outs[0][...] = ins[2][...] + scale * acc

    return _mm(
        name, (n // tn, m // tm, kd // tk), [a, w, x],
        [pl.BlockSpec((tm, tk), lambda j, i, k: (i, k)), pl.BlockSpec((tk, tn), lambda j, i, k: (k, j)),
         pl.BlockSpec((tm, tn), lambda j, i, k: (i, j))],
        [_sds((m, n), F32)], [pl.BlockSpec((tm, tn), lambda j, i, k: (i, j))], (tm, tn),
        lambda a_ref, w_ref, x_ref: _dot_nn(a_ref[...], w_ref[...]), epilogue, jobs)


def mm_nt(name, a, w, out_dtype, scale=1.0, tm=1024, tn=1024, tk=2048, jobs=()):
    m, kd = a.shape
    n = w.shape[0]
    tm, tn, tk = _tile(m, tm, 8), _tile(n, tn), _tile(kd, tk)
    return _mm(
        name, (n // tn, m // tm, kd // tk), [a, w],
        [pl.BlockSpec((tm, tk), lambda j, i, k: (i, k)), pl.BlockSpec((tn, tk), lambda j, i, k: (j, k))],
        [_sds((m, n), out_dtype)], [pl.BlockSpec((tm, tn), lambda j, i, k: (i, j))], (tm, tn),
        lambda a_ref, w_ref: _dot_nt(a_ref[...], w_ref[...]), _store(scale, out_dtype), jobs)


def mm_nt_pair(name, a3, w, out_dtype, tm=1024, tn=1024, tk=1408, jobs=()):
    _, m, f = a3.shape
    n = w.shape[0]
    tm, tn, tk = _tile(m, tm, 8), _tile(n, tn), _tile(f, tk)
    nkf = f // tk
    return _mm(
        name, (n // tn, m // tm, 2 * nkf), [a3, w],
        [pl.BlockSpec((None, tm, tk), lambda j, i, k: (k // nkf, i, k % nkf)),
         pl.BlockSpec((tn, tk), lambda j, i, k: (j, k))],
        [_sds((m, n), out_dtype)], [pl.BlockSpec((tm, tn), lambda j, i, k: (i, j))], (tm, tn),
        lambda a_ref, w_ref: _dot_nt(a_ref[...], w_ref[...]), _store(1.0, out_dtype), jobs)


def mm_tn(name, a, b, out_dtype, scale=1.0, tm=1024, tn=1024, tk=1024, jobs=()):
    kd, m = a.shape
    n = b.shape[1]
    tm, tn, tk = _tile(m, tm), _tile(n, tn), _tile(kd, tk, 16)
    return _mm(
        name, (n // tn, m // tm, kd // tk), [a, b],
        [pl.BlockSpec((tk, tm), lambda j, i, k: (k, i)), pl.BlockSpec((tk, tn), lambda j, i, k: (k, j))],
        [_sds((m, n), out_dtype)], [pl.BlockSpec((tm, tn), lambda j, i, k: (i, j))], (tm, tn),
        lambda a_ref, b_ref: _dot_tn(a_ref[...], b_ref[...]), _store(scale, out_dtype), jobs)


def mm_tn_pair(name, a, b3, out_dtype, tm=1024, tn=1408, tk=1024, jobs=()):
    kd, m = a.shape
    f = b3.shape[2]
    tm, tn, tk = _tile(m, tm), _tile(f, tn), _tile(kd, tk, 16)
    nf = f // tn
    return _mm(
        name, (2 * nf, m // tm, kd // tk), [a, b3],
        [pl.BlockSpec((tk, tm), lambda j, i, k: (k, i)),
         pl.BlockSpec((None, tk, tn), lambda j, i, k: (j // nf, k, j % nf))],
        [_sds((m, 2 * f), out_dtype)], [pl.BlockSpec((tm, tn), lambda j, i, k: (i, j))], (tm, tn),
        lambda a_ref, b_ref: _dot_tn(a_ref[...], b_ref[...]), _store(1.0, out_dtype), jobs)


def swiglu_fwd(name, n, w_in, tm=1024, tn=512, jobs=()):
    s, d = n.shape
    f = w_in.shape[1] // 2
    tm, tn = _tile(s, tm, 8), _tile(f, tn)
    nf = f // tn

    def body(n_ref, wg_ref, wu_ref, gu_ref, a_ref):
        nv = n_ref[...]
        g = _dot_nn(nv, wg_ref[...])
        u = _dot_nn(nv, wu_ref[...])
        gu_ref[0] = g.astype(BF16)
        gu_ref[1] = u.astype(BF16)
        a_ref[...] = (g * jax.nn.sigmoid(g) * u).astype(BF16)

    outs, job_outs = _call(
        name, body, (nf, s // tm),
        [pl.BlockSpec((tm, d), lambda j, i: (i, 0)), pl.BlockSpec((d, tn), lambda j, i: (0, j)),
         pl.BlockSpec((d, tn), lambda j, i: (0, j + nf))],
        [pl.BlockSpec((2, tm, tn), lambda j, i: (0, i, j)), pl.BlockSpec((tm, tn), lambda j, i: (i, j))],
        [_sds((2, s, f), BF16), _sds((s, f), BF16)], [n, w_in, w_in], sem=("parallel", "parallel"), jobs=jobs)
    return _ret(outs, job_outs, jobs, single=False)


def swiglu_bwd(name, dh, w_out, gu, scale, tm=1024, tn=512, jobs=()):
    s, d = dh.shape
    f = w_out.shape[0]
    tm, tn = _tile(s, tm, 8), _tile(f, tn)

    def body(dh_ref, w_ref, gu_ref, out_ref):
        da = _dot_nt(dh_ref[...], w_ref[...]) * scale
        g = gu_ref[0].astype(F32)
        u = gu_ref[1].astype(F32)
        sg = jax.nn.sigmoid(g)
        out_ref[0] = (da * u * (sg * (1.0 + g * (1.0 - sg)))).astype(BF16)
        out_ref[1] = (da * (g * sg)).astype(BF16)

    outs, job_outs = _call(
        name, body, (f // tn, s // tm),
        [pl.BlockSpec((tm, d), lambda j, i: (i, 0)), pl.BlockSpec((tn, d), lambda j, i: (j, 0)),
         pl.BlockSpec((2, tm, tn), lambda j, i: (0, i, j))],
        [pl.BlockSpec((2, tm, tn), lambda j, i: (0, i, j))],
        [_sds((2, s, f), BF16)], [dh, w_out, gu], sem=("parallel", "parallel"), jobs=jobs)
    return _ret(outs, job_outs, jobs)


HALO = 16


def _conv_inputs(z_ref, hgc_ref, hhc_ref, i, cw, tm):
    gc = z_ref[:, cw:2 * cw].astype(F32)
    hc = z_ref[:, 2 * cw:3 * cw].astype(F32)
    cin = gc * hc
    halo = hgc_ref[...].astype(F32) * hhc_ref[...].astype(F32) * (i > 0).astype(F32)
    row = lax.broadcasted_iota(jnp.int32, (tm, cw), 0)
    x1 = jnp.where(row == 0, halo[HALO - 1:HALO], pltpu.roll(cin, 1, 0))
    x2 = jnp.where(row == 0, halo[HALO - 2:HALO - 1], jnp.where(row == 1, halo[HALO - 1:HALO], pltpu.roll(cin, 2, 0)))
    return gc, hc, cin, x1, x2


def _tril(w):
    r = lax.broadcasted_iota(jnp.int32, w.shape, 0)
    c = lax.broadcasted_iota(jnp.int32, w.shape, 1)
    return jnp.where(r >= c, w, jnp.zeros_like(w))


def mixer_fwd(name, z, conv_w, conv_b, g_v, w_s, b_t, tm=256, jobs=()):
    s, zc = z.shape
    cw = conv_w.shape[1]
    gw = g_v.shape[1]
    heads = gw // GROUP
    tm = _tile(s, tm)
    hb = tm // HALO

    def body(z_ref, hgc_ref, hhc_ref, cw_ref, cb_ref, gv_ref, ws_ref, bt_ref, y_ref):
        i = pl.program_id(0)
        _, _, cin, x1, x2 = _conv_inputs(z_ref, hgc_ref, hhc_ref, i, cw, tm)
        cv = cb_ref[...] + cw_ref[2:3, :] * cin + cw_ref[1:2, :] * x1 + cw_ref[0:1, :] * x2
        y_ref[:, 0:cw] = (z_ref[:, 0:cw].astype(F32) * cv).astype(BF16)
        for h in range(heads):
            lo = h * GROUP
            vh = z_ref[:, 3 * cw + gw + lo:3 * cw + gw + lo + GROUP].astype(F32)
            rv = lax.rsqrt(jnp.mean(vh * vh, axis=-1, keepdims=True) + EPS)
            vn = (vh * rv * gv_ref[:, lo:lo + GROUP]).astype(BF16)
            w = _tril(ws_ref[h]).astype(BF16)
            for n in range(tm // GROUP):
                rows = slice(n * GROUP, (n + 1) * GROUP)
                sg = _dot_nn(w, vn[rows]) + bt_ref[:, h:h + 1]
                u = z_ref[rows, 3 * cw + lo:3 * cw + lo + GROUP].astype(F32)
                y_ref[rows, cw + lo:cw + lo + GROUP] = (u * sg).astype(BF16)

    fixed2 = lambda i: (0, 0)
    outs, job_outs = _call(
        name, body, (s // tm,),
        [pl.BlockSpec((tm, zc), lambda i: (i, 0)),
         pl.BlockSpec((HALO, cw), lambda i: (jnp.maximum(i * hb - 1, 0), 1)),
         pl.BlockSpec((HALO, cw), lambda i: (jnp.maximum(i * hb - 1, 0), 2)),
         pl.BlockSpec(conv_w.shape, fixed2), pl.BlockSpec(conv_b.shape, fixed2),
         pl.BlockSpec(g_v.shape, fixed2), pl.BlockSpec(w_s.shape, lambda i: (0, 0, 0)),
         pl.BlockSpec(b_t.shape, fixed2)],
        [pl.BlockSpec((tm, cw + gw), lambda i: (i, 0))], [_sds((s, cw + gw), BF16)],
        [z, z, z, conv_w, conv_b, g_v, w_s, b_t], sem=("arbitrary",), jobs=jobs)
    return _ret(outs, job_outs, jobs)


def mixer_bwd(name, z, dy, conv_w, conv_b, g_v, w_s, b_t, tm=256, jobs=()):
    s, zc = z.shape
    cw = conv_w.shape[1]
    gw = g_v.shape[1]
    heads = gw // GROUP
    tm = _tile(s, tm)
    hb = tm // HALO
    nsteps = s // tm
    last_halo = s // HALO - 1

    def body(z_ref, hgc_ref, hhc_ref, ngb_ref, dy_ref, ndy_ref, cw_ref, cb_ref, gv_ref, ws_ref, bt_ref,
             dz_ref, sm_ref, dws_ref, dbt_ref, dsg_ref):
        i = pl.program_id(0)

        @pl.when(i == 0)
        def _():
            sm_ref[...] = jnp.zeros_like(sm_ref)
            dws_ref[...] = jnp.zeros_like(dws_ref)
            dsg_ref[...] = jnp.zeros_like(dsg_ref)

        gc, hc, cin, x1, x2 = _conv_inputs(z_ref, hgc_ref, hhc_ref, i, cw, tm)
        w0, w1, w2 = cw_ref[0:1, :], cw_ref[1:2, :], cw_ref[2:3, :]
        cv = cb_ref[...] + w2 * cin + w1 * x1 + w0 * x2
        gb = z_ref[:, 0:cw].astype(F32)
        dyc = dy_ref[:, 0:cw].astype(F32)
        dz_ref[:, 0:cw] = (dyc * cv).astype(BF16)
        dcv = dyc * gb
        nxt = ndy_ref[...].astype(F32) * ngb_ref[...].astype(F32) * (i < nsteps - 1).astype(F32)
        row = lax.broadcasted_iota(jnp.int32, (tm, cw), 0)
        d1 = jnp.where(row == tm - 1, nxt[0:1], pltpu.roll(dcv, tm - 1, 0))
        d2 = jnp.where(row == tm - 1, nxt[1:2], jnp.where(row == tm - 2, nxt[0:1], pltpu.roll(dcv, tm - 2, 0)))
        dcin = w2 * dcv + w1 * d1 + w0 * d2
        dz_ref[:, cw:2 * cw] = (dcin * hc).astype(BF16)
        dz_ref[:, 2 * cw:3 * cw] = (dcin * gc).astype(BF16)
        sm_ref[0:1, :] += jnp.sum(dcv * x2, axis=0, keepdims=True)
        sm_ref[1:2, :] += jnp.sum(dcv * x1, axis=0, keepdims=True)
        sm_ref[2:3, :] += jnp.sum(dcv * cin, axis=0, keepdims=True)
        sm_ref[3:4, :] += jnp.sum(dcv, axis=0, keepdims=True)

        for h in range(heads):
            lo = h * GROUP
            vcol = slice(3 * cw + gw + lo, 3 * cw + gw + lo + GROUP)
            ucol = slice(3 * cw + lo, 3 * cw + lo + GROUP)
            vh = z_ref[:, vcol].astype(F32)
            rv = lax.rsqrt(jnp.mean(vh * vh, axis=-1, keepdims=True) + EPS)
            xh = vh * rv
            gvh = gv_ref[:, lo:lo + GROUP]
            vn = (xh * gvh).astype(BF16)
            w = _tril(ws_ref[h]).astype(BF16)
            dgv = jnp.zeros((1, GROUP), F32)
            for n in range(tm // GROUP):
                rows = slice(n * GROUP, (n + 1) * GROUP)
                sg = _dot_nn(w, vn[rows]) + bt_ref[:, h:h + 1]
                dyg = dy_ref[rows, cw + lo:cw + lo + GROUP].astype(F32)
                dsg = dyg * z_ref[rows, ucol].astype(F32)
                dz_ref[rows, ucol] = (dyg * sg).astype(BF16)
                dsgb = dsg.astype(BF16)
                dvn = _dot_tn(w, dsgb)
                dws_ref[h] += _dot_nt(dsgb, vn[rows])
                dsg_ref[:, lo:lo + GROUP] += dsg
                xhc = xh[rows]
                dgv = dgv + jnp.sum(dvn * xhc, axis=0, keepdims=True)
                dxh = dvn * gvh
                dv = rv[rows] * (dxh - xhc * jnp.mean(dxh * xhc, axis=-1, keepdims=True))
                dz_ref[rows, vcol] = dv.astype(BF16)
            sm_ref[4:5, lo:lo + GROUP] += dgv

        @pl.when(i == nsteps - 1)
        def _():
            for h in range(heads):
                dws_ref[h] = _tril(dws_ref[h])
                dbt_ref[:, h:h + 1] = jnp.sum(dsg_ref[:, h * GROUP:(h + 1) * GROUP], axis=-1, keepdims=True)

    fixed2 = lambda i: (0, 0)
    fixed3 = lambda i: (0, 0, 0)
    prev = lambda col: (lambda i: (jnp.maximum(i * hb - 1, 0), col))
    nxt_blk = lambda i: (jnp.minimum((i + 1) * hb, last_halo), 0)
    outs, job_outs = _call(
        name, body, (nsteps,),
        [pl.BlockSpec((tm, zc), lambda i: (i, 0)),
         pl.BlockSpec((HALO, cw), prev(1)), pl.BlockSpec((HALO, cw), prev(2)),
         pl.BlockSpec((HALO, cw), nxt_blk),
         pl.BlockSpec((tm, cw + gw), lambda i: (i, 0)), pl.BlockSpec((HALO, cw), nxt_blk),
         pl.BlockSpec(conv_w.shape, fixed2), pl.BlockSpec(conv_b.shape, fixed2),
         pl.BlockSpec(g_v.shape, fixed2), pl.BlockSpec(w_s.shape, fixed3), pl.BlockSpec(b_t.shape, fixed2)],
        [pl.BlockSpec((tm, zc), lambda i: (i, 0)), pl.BlockSpec((8, cw), fixed2),
         pl.BlockSpec(w_s.shape, fixed3), pl.BlockSpec(b_t.shape, fixed2)],
        [_sds((s, zc), BF16), _sds((8, cw), F32), _sds(w_s.shape, F32), _sds(b_t.shape, F32)],
        [z, z, z, z, dy, dy, conv_w, conv_b, g_v, w_s, b_t],
        scratch=[pltpu.VMEM((GROUP, gw), F32)], sem=("arbitrary",), jobs=jobs)
    return _ret(outs, job_outs, jobs, single=False)


def _softmax_rows(sc):
    e = jnp.exp(sc - jnp.max(sc, axis=-1, keepdims=True))
    return e / jnp.sum(e, axis=-1, keepdims=True)


def attn_fwd(name, q, k, v, tm=512):
    s, d = q.shape
    m = k.shape[0]
    hd = d // XA_HEADS
    scale = hd ** -0.5
    tm = _tile(s, tm, 8)

    def body(q_ref, k_ref, v_ref, o_ref):
        for h in range(XA_HEADS):
            cols = slice(h * hd, (h + 1) * hd)
            p = _softmax_rows(_dot_nt(q_ref[:, cols], k_ref[:, cols]) * scale)
            o_ref[:, cols] = _dot_nn(p.astype(BF16), v_ref[:, cols]).astype(BF16)

    return _call(
        name, body, (s // tm,),
        [pl.BlockSpec((tm, d), lambda i: (i, 0)), pl.BlockSpec((m, d), lambda i: (0, 0)),
         pl.BlockSpec((m, d), lambda i: (0, 0))],
        [pl.BlockSpec((tm, d), lambda i: (i, 0))], [_sds((s, d), BF16)], [q, k, v], sem=("arbitrary",))[0][0]


def attn_bwd(name, q, k, v, do, tm=512):
    s, d = q.shape
    m = k.shape[0]
    hd = d // XA_HEADS
    scale = hd ** -0.5
    tm = _tile(s, tm, 8)

    def body(q_ref, k_ref, v_ref, do_ref, dq_ref, dk_ref, dv_ref):
        i = pl.program_id(0)

        @pl.when(i == 0)
        def _():
            dk_ref[...] = jnp.zeros_like(dk_ref)
            dv_ref[...] = jnp.zeros_like(dv_ref)

        for h in range(XA_HEADS):
            cols = slice(h * hd, (h + 1) * hd)
            qh = q_ref[:, cols]
            doh = do_ref[:, cols]
            p = _softmax_rows(_dot_nt(qh, k_ref[:, cols]) * scale)
            dp = _dot_nt(doh, v_ref[:, cols])
            ds = (p * (dp - jnp.sum(dp * p, axis=-1, keepdims=True)) * scale).astype(BF16)
            dq_ref[:, cols] = _dot_nn(ds, k_ref[:, cols]).astype(BF16)
            dk_ref[:, cols] += _dot_tn(ds, qh)
            dv_ref[:, cols] += _dot_tn(p.astype(BF16), doh)

    row = lambda i: (i, 0)
    fixed = lambda i: (0, 0)
    return _call(
        name, body, (s // tm,),
        [pl.BlockSpec((tm, d), row), pl.BlockSpec((m, d), fixed), pl.BlockSpec((m, d), fixed),
         pl.BlockSpec((tm, d), row)],
        [pl.BlockSpec((tm, d), row), pl.BlockSpec((m, d), fixed), pl.BlockSpec((m, d), fixed)],
        [_sds((s, d), BF16), _sds((m, d), F32), _sds((m, d), F32)], [q, k, v, do], sem=("arbitrary",))[0]


def _grid2(rows, cols, row_mult):
    tr, tc = _tile(rows, 512, row_mult), _tile(cols, 2048)
    return tr, tc, rows // tr, cols // tc


def pair_add(name, grad, peer, axis, place):
    hr, hc = peer.shape
    tr, tc, nbr, nbc = _grid2(hr, hc, 16)
    if axis == 1:
        mine = lambda i, j, p: (i + p[1] * nbr, j)
    else:
        mine = lambda i, j, p: (i, j + p[1] * nbc)
    same = lambda i, j, p: (i, j)

    def body(p_ref, g_ref, q_ref, out_ref):
        out_ref[...] = (g_ref[...].astype(F32) + q_ref[...].astype(F32)).astype(BF16)

    return pl.pallas_call(
        body, name=name,
        grid_spec=pltpu.PrefetchScalarGridSpec(
            num_scalar_prefetch=1, grid=(nbr, nbc),
            in_specs=[pl.BlockSpec((tr, tc), mine), pl.BlockSpec((tr, tc), same)],
            out_specs=pl.BlockSpec((tr, tc), same)),
        out_shape=_sds((hr, hc), BF16),
        compiler_params=_params(("parallel", "parallel")),
    )(place, grad, peer)


def cross_sum(name, part, land, axis, shape, place):
    _, sr, sc = land.shape
    tr, tc, nbr, nbc = _grid2(sr, sc, 16)
    if axis == 1:
        own = lambda i, j, p: (i, j + p[0] * nbc)
        dst = lambda i, j, p: (i + p[1] * nbr, j)
    else:
        own = lambda i, j, p: (i + p[0] * nbr, j)
        dst = lambda i, j, p: (i, j + p[1] * nbc)

    def body(p_ref, own_ref, land_ref, out_ref):
        out_ref[...] = ((own_ref[...].astype(F32) + land_ref[0].astype(F32))
                        + (land_ref[1].astype(F32) + land_ref[2].astype(F32)))

    return pl.pallas_call(
        body, name=name,
        grid_spec=pltpu.PrefetchScalarGridSpec(
            num_scalar_prefetch=1, grid=(nbr, nbc),
            in_specs=[pl.BlockSpec((tr, tc), own), pl.BlockSpec((3, tr, tc), lambda i, j, p: (0, i, j))],
            out_specs=pl.BlockSpec((tr, tc), dst)),
        out_shape=_sds(_block(shape, axis), F32),
        compiler_params=_params(("parallel", "parallel")),
    )(place, part, land)


def _adam_math(w, g, m, v):
    m = ADAM_B1 * m + (1.0 - ADAM_B1) * g
    v = ADAM_B2 * v + (1.0 - ADAM_B2) * (g * g)
    m_hat = m / (1.0 - ADAM_B1 ** ADAM_STEP)
    v_hat = v / (1.0 - ADAM_B2 ** ADAM_STEP)
    delta = -ADAM_LR * (m_hat / (jnp.sqrt(v_hat) + ADAM_EPS) + ADAM_WD * w)
    return delta, m, v


def adamw(name, w, g, m, v):
    r, c = w.shape
    tr, tc = _tile(r, 256, 8), _tile(c, 1408)

    def body(w_ref, g_ref, m_ref, v_ref, d_out, m_out, v_out):
        d, mm, vv = _adam_math(w_ref[...], g_ref[...], m_ref[...], v_ref[...])
        d_out[...] = d
        m_out[...] = mm
        v_out[...] = vv

    spec = pl.BlockSpec((tr, tc), lambda i, j: (i, j))
    return _call(name, body, (r // tr, c // tc), [spec] * 4, [spec] * 3, [_sds((r, c), F32)] * 3, [w, g, m, v],
                 sem=("parallel", "parallel"))[0]


def small_sum(name, stacks):
    def body(*refs):
        for s_ref, out_ref in zip(refs[:len(stacks)], refs[len(stacks):]):
            acc = s_ref[0]
            for d in range(1, s_ref.shape[0]):
                acc = acc + s_ref[d]
            out_ref[...] = acc

    return pl.pallas_call(body, name=name, out_shape=[_sds(s.shape[1:], F32) for s in stacks])(*stacks)


WEIGHTS = ["g_ffn1", "w_ffn1_in", "w_ffn1_out", "g_mix", "w_mix_in", "conv_w", "conv_b", "g_gm_v", "w_spatial",
           "b_spatial", "w_mix_out", "g_xattn", "g_mem", "w_xq", "w_xk", "w_xv", "w_xo", "g_ffn2", "w_ffn2_in",
           "w_ffn2_out", "g_final"]
BIG = {"w_ffn1_in": 1, "w_ffn1_out": 0, "w_mix_in": 1, "w_mix_out": 0, "w_xq": 0, "w_xk": 0, "w_xv": 0, "w_xo": 0,
       "w_ffn2_in": 1, "w_ffn2_out": 0}
SMALL = [n for n in WEIGHTS if n not in BIG]
LATE_SMALL = ["g_ffn1"]
EARLY_SMALL = [n for n in SMALL if n not in LATE_SMALL]


def _pack(arrays):
    flat = jnp.concatenate([a.reshape(-1) for a in arrays])
    rows = -(-flat.shape[0] // 1024) * 8
    return jnp.pad(flat, (0, rows * 128 - flat.shape[0])).reshape(rows, 128)


def _unpack(buf, shapes):
    flat = buf.reshape(-1)
    out, pos = [], 0
    for shp in shapes:
        n = math.prod(shp)
        out.append(flat[pos:pos + n].reshape(shp))
        pos += n
    return out


def kernel(x, mem, g_ffn1, w_ffn1_in, w_ffn1_out, g_mix, w_mix_in, conv_w, conv_b, g_gm_v, w_spatial, b_spatial, w_mix_out, g_xattn, g_mem, w_xq, w_xk, w_xv, w_xo, g_ffn2, w_ffn2_in, w_ffn2_out, g_final, loss_target, m_g_ffn1, m_w_ffn1_in, m_w_ffn1_out, m_g_mix, m_w_mix_in, m_conv_w, m_conv_b, m_g_gm_v, m_w_spatial, m_b_spatial, m_w_mix_out, m_g_xattn, m_g_mem, m_w_xq, m_w_xk, m_w_xv, m_w_xo, m_g_ffn2, m_w_ffn2_in, m_w_ffn2_out, m_g_final, v_g_ffn1, v_w_ffn1_in, v_w_ffn1_out, v_g_mix, v_w_mix_in, v_conv_w, v_conv_b, v_g_gm_v, v_w_spatial, v_b_spatial, v_w_mix_out, v_g_xattn, v_g_mem, v_w_xq, v_w_xk, v_w_xv, v_w_xo, v_g_ffn2, v_w_ffn2_in, v_w_ffn2_out, v_g_final):
    given = dict(locals())
    wts = {n: given[n] for n in WEIGHTS}
    mom = {n: given["m_" + n] for n in WEIGHTS}
    var = {n: given["v_" + n] for n in WEIGHTS}

    xi, yi, ci = lax.axis_index("x"), lax.axis_index("y"), lax.axis_index("c")
    blk = 2 * xi + yi
    place = jnp.stack([blk, ci]).astype(jnp.int32)

    x2, mem2, tgt = x[0], mem[0], loss_target[0]
    wb = {n: wts[n][0].astype(BF16) for n in BIG}
    shape = {n: _full_shape(wb[n].shape, BIG[n]) for n in BIG}
    w_s, b_t = w_spatial[0], b_spatial[0].T
    gf = g_final[None]

    def gather(*names):
        return gather_job([(wb[n], BIG[n], None, WHOLE) for n in names])

    full = {}

    (full["w_ffn1_in"],), (conv_taps,) = comm_only(
        "gather_first", [gather("w_ffn1_in"), columns_job(jnp.pad(conv_w[0], ((0, 8 - CONV_K), (0, 0))))])
    n1, r1 = rmsnorm_fwd("norm1", x2, g_ffn1)
    (gu1, a1), (got,) = swiglu_fwd("ffn1_in", n1, full["w_ffn1_in"], jobs=[gather("w_ffn1_out", "w_mix_in")])
    full["w_ffn1_out"], full["w_mix_in"] = got
    h1, (got,) = mm_nn_resid("ffn1_out", a1, full["w_ffn1_out"], x2, 0.5, jobs=[gather("w_mix_out", "w_xq", "w_xk")])
    full["w_mix_out"], full["w_xq"], full["w_xk"] = got
    n2, r2 = rmsnorm_fwd("norm2", h1, g_mix)
    z, (got,) = mm_nn("mix_in", n2, full["w_mix_in"], BF16, jobs=[gather("w_xv", "w_xo")])
    full["w_xv"], full["w_xo"] = got

    pieces = 4
    def ffn2_piece(p, prev):
        return [gather_job([(wb["w_ffn2_in"], 1, prev, (p, 1, pieces))])]

    ycat, ((w2in,),) = mixer_fwd("mixer", z, conv_taps, conv_b, g_gm_v, w_s, b_t, jobs=ffn2_piece(0, None))
    h2, ((w2in,),) = mm_nn_resid("mix_out", ycat, full["w_mix_out"], h1, 1.0, tk=2048, jobs=ffn2_piece(1, w2in))
    n3, r3 = rmsnorm_fwd("norm3", h2, g_xattn)
    mn, rm = rmsnorm_fwd("norm_mem", mem2, g_mem)
    q, ((w2in,),) = mm_nn("xq", n3, full["w_xq"], BF16, jobs=ffn2_piece(2, w2in))
    k = mm_nn("xk", mn, full["w_xk"], BF16)
    v = mm_nn("xv", mn, full["w_xv"], BF16)
    o = attn_fwd("attn", q, k, v)
    h3, ((w2in,),) = mm_nn_resid("xo", o, full["w_xo"], h2, 1.0, tk=2048, jobs=ffn2_piece(3, w2in))
    full["w_ffn2_in"] = w2in
    n4, r4 = rmsnorm_fwd("norm4", h3, g_ffn2)
    (gu2, a2), ((full["w_ffn2_out"],),) = swiglu_fwd("ffn2_in", n4, full["w_ffn2_in"], jobs=[gather("w_ffn2_out")])
    h4 = mm_nn_resid("ffn2_out", a2, full["w_ffn2_out"], h3, 0.5)
    loss_blk, dh4, dh4b, dg_final = loss_head("loss_head", h4, gf, tgt)

    dw, peer, part, land, half, grads = {}, {}, {}, {}, {}, {}

    def send_pair(*names):
        return pair_job([dw[n] for n in names], [BIG[n] for n in names])

    def take_pair(names, got):
        for n, p in zip(names, got):
            part[n] = pair_add("pair_add_" + n, dw[n], p, BIG[n], place)

    def send_cross(*names, sub=WHOLE):
        return cross_job([(part[n], BIG[n], shape[n], land.get(n), sub) for n in names])

    def take_cross(names, got, last=True):
        for n, l in zip(names, got):
            land[n] = l
            if last:
                half[n] = cross_sum("cross_sum_" + n, part[n], l, BIG[n], shape[n], place)

    def send_final(*names):
        return final_job([half[n] for n in names], [BIG[n] for n in names], [shape[n] for n in names])

    delta, new_m, new_v = {}, {}, {}

    def take_final(names, got):
        for n, g in zip(names, got):
            grads[n] = g
            delta[n], new_m[n], new_v[n] = adamw("adamw_" + n, wts[n][0], g, mom[n][0], var[n][0])

    dw["w_ffn2_out"] = mm_tn("ffn2_dwout", a2, dh4b, BF16, scale=0.5)
    dgu2, (got,) = swiglu_bwd("ffn2_dact", dh4b, full["w_ffn2_out"], gu2, 0.5, jobs=[send_pair("w_ffn2_out")])
    take_pair(["w_ffn2_out"], got)
    dw["w_ffn2_in"], (got,) = mm_tn_pair("ffn2_dwin", n4, dgu2, BF16, jobs=[send_cross("w_ffn2_out")])
    take_cross(["w_ffn2_out"], got)
    dn4, (got_f, got_p) = mm_nt_pair("ffn2_dn", dgu2, full["w_ffn2_in"], F32,
                                     jobs=[send_final("w_ffn2_out"), send_pair("w_ffn2_in")])
    take_final(["w_ffn2_out"], got_f)
    take_pair(["w_ffn2_in"], got_p)
    dh3, dh3b, dg_ffn2 = rmsnorm_bwd("norm4_bwd", dn4, h3, r4, g_ffn2, dh4)

    dw["w_xo"] = mm_tn("xo_dw", o, dh3b, BF16)
    do = mm_nt("xo_dx", dh3b, full["w_xo"], BF16)
    dq, dk, dv = attn_bwd("attn_bwd", q, k, v, do)
    dkb, dvb = dk.astype(BF16), dv.astype(BF16)
    dw["w_xq"] = mm_tn("xq_dw", n3, dq, BF16)
    dn3 = mm_nt("xq_dx", dq, full["w_xq"], F32)
    dh2, dh2b, dg_xattn = rmsnorm_bwd("norm3_bwd", dn3, h2, r3, g_xattn, dh3)
    dw["w_xk"] = mm_tn("xk_dw", mn, dkb, BF16)
    dw["w_xv"] = mm_tn("xv_dw", mn, dvb, BF16)
    dmn_k = mm_nt("xk_dx", dkb, full["w_xk"], F32)
    dmn_v = mm_nt("xv_dx", dvb, full["w_xv"], F32)
    dg_mem = gain_grad("norm_mem_bwd", dmn_k, dmn_v, mem2, rm)

    dw["w_mix_out"] = mm_tn("mix_out_dw", ycat, dh2b, BF16)
    dycat = mm_nt("mix_out_dx", dh2b, full["w_mix_out"], BF16)
    attn_names = ["w_xo", "w_xq", "w_xk", "w_xv", "w_mix_out"]
    (dz, dsmall, dws, dbt), (got_c, got_p) = mixer_bwd(
        "mixer_bwd", z, dycat, conv_taps, conv_b, g_gm_v, w_s, b_t,
        jobs=[send_cross("w_ffn2_in", sub=(0, 2, 8)), send_pair(*attn_names)])
    take_cross(["w_ffn2_in"], got_c, last=False)
    take_pair(attn_names, got_p)
    dw["w_mix_in"], (got_c,) = mm_tn("mix_in_dw", n2, dz, BF16, jobs=[send_cross("w_ffn2_in", sub=(2, 3, 8))])
    take_cross(["w_ffn2_in"], got_c, last=False)
    dn2, (got_c, got_p) = mm_nt("mix_in_dx", dz, full["w_mix_in"], F32, tk=1280,
                                jobs=[send_cross("w_ffn2_in", sub=(5, 3, 8)), send_pair("w_mix_in")])
    take_cross(["w_ffn2_in"], got_c)
    take_pair(["w_mix_in"], got_p)
    dh1, dh1b, dg_mix = rmsnorm_bwd("norm2_bwd", dn2, h1, r2, g_mix, dh2)

    xa_names = ["w_xo", "w_xq", "w_xk", "w_xv"]
    dw["w_ffn1_out"], (got_c, got_f) = mm_tn("ffn1_dwout", a1, dh1b, BF16, scale=0.5,
                                             jobs=[send_cross(*xa_names), send_final("w_ffn2_in")])
    take_cross(xa_names, got_c)
    take_final(["w_ffn2_in"], got_f)
    mix_names = ["w_mix_out", "w_mix_in"]
    dgu1, (got_c, got_p, got_f) = swiglu_bwd(
        "ffn1_dact", dh1b, full["w_ffn1_out"], gu1, 0.5,
        jobs=[send_cross(*mix_names), send_pair("w_ffn1_out"), send_final(*xa_names)])
    take_cross(mix_names, got_c)
    take_pair(["w_ffn1_out"], got_p)
    take_final(xa_names, got_f)
    dw["w_ffn1_in"], (got_c, got_f) = mm_tn_pair("ffn1_dwin", n1, dgu1, BF16,
                                                 jobs=[send_cross("w_ffn1_out"), send_final(*mix_names)])
    take_cross(["w_ffn1_out"], got_c)
    take_final(mix_names, got_f)

    early = {"g_mix": dg_mix, "conv_w": dsmall[0:CONV_K], "conv_b": dsmall[3:4], "g_gm_v": dsmall[4:5],
             "w_spatial": dws, "b_spatial": dbt.T, "g_xattn": dg_xattn, "g_mem": dg_mem, "g_ffn2": dg_ffn2,
             "g_final": dg_final}
    dn1, (got_p, got_f, (early_all,)) = mm_nt_pair(
        "ffn1_dn", dgu1, full["w_ffn1_in"], F32,
        jobs=[send_pair("w_ffn1_in"), send_final("w_ffn1_out"), stack_job(_pack([early[n] for n in EARLY_SMALL]))])
    take_pair(["w_ffn1_in"], got_p)
    take_final(["w_ffn1_out"], got_f)
    (dx, _, dg_ffn1), (got_c,) = rmsnorm_bwd("norm1_bwd", dn1, x2, r1, g_ffn1, dh1, jobs=[send_cross("w_ffn1_in")])
    take_cross(["w_ffn1_in"], got_c)
    got_f, (late_all,) = comm_only("tail_exchange", [send_final("w_ffn1_in"), stack_job(_pack([dg_ffn1]))])
    take_final(["w_ffn1_in"], got_f)

    early_sum, late_sum = small_sum("small_sum", [early_all, late_all])
    for n, g in zip(EARLY_SMALL, _unpack(early_sum, [early[n].shape for n in EARLY_SMALL])):
        grads[n] = g
    grads["g_ffn1"] = _unpack(late_sum, [dg_ffn1.shape])[0]
    taps_cols = conv_w.shape[2]
    grads["conv_w"] = lax.dynamic_slice_in_dim(grads["conv_w"], blk * taps_cols, taps_cols, axis=1)
    packed = [_pack([src[n] for n in SMALL]) for src in (wts, grads, mom, var)]
    own_shapes = [wts[n].shape for n in SMALL]
    for dst, buf in zip((delta, new_m, new_v), adamw("adamw_small", *packed)):
        for n, a in zip(SMALL, _unpack(buf, own_shapes)):
            dst[n] = a

    loss = lax.psum(loss_blk[0, 0], ("x", "y", "c"))
    outs = [loss, dx[None]]
    for group in (grads, delta, new_m, new_v):
        outs += [group[n].reshape(wts[n].shape) for n in WEIGHTS]
    return tuple(outs)
```

```python
import math

import jax
import jax.numpy as jnp
from jax import lax
from jax.experimental import pallas as pl
from jax.experimental.pallas import tpu as pltpu

F32 = jnp.float32
BF16 = jnp.bfloat16
EPS = 1e-6
GROUP = 128
XA_HEADS = 4
CONV_K = 3
N_CHIPS = 4
VMEM_LIMIT_BYTES = 56 * 1024 * 1024

ADAM_LR = 0.001
ADAM_B1 = 0.9
ADAM_B2 = 0.999
ADAM_EPS = 1e-08
ADAM_WD = 0.01
ADAM_STEP = 10

MESH = pl.DeviceIdType.MESH
ANY = pl.BlockSpec(memory_space=pl.ANY)


def _tile(dim, pref, mult=128):
    if dim <= pref:
        return dim
    t = (pref // mult) * mult
    while t >= mult:
        if dim % t == 0:
            return t
        t -= mult
    raise ValueError(f"no tile for {dim} under {pref}")


def _params(sem):
    return pltpu.CompilerParams(dimension_semantics=sem, vmem_limit_bytes=VMEM_LIMIT_BYTES)


def _sds(shape, dtype):
    return jax.ShapeDtypeStruct(shape, dtype)


def _dot_nn(a, b):
    return jnp.dot(a, b, preferred_element_type=F32)


def _dot_nt(a, b):
    return lax.dot_general(a, b, (((1,), (1,)), ((), ())), preferred_element_type=F32)


def _dot_tn(a, b):
    return lax.dot_general(a, b, (((0,), (0,)), ((), ())), preferred_element_type=F32)


class Job:
    def __init__(self, inputs, out_shapes, aliases, sems, start, middle, finish):
        self.inputs, self.out_shapes, self.aliases, self.sems = inputs, out_shapes, aliases, sems
        self.start, self.middle, self.finish = start, middle, finish


def _place():
    x, y, c = lax.axis_index("x"), lax.axis_index("y"), lax.axis_index("c")
    chips = [(1 - x, y), (x, 1 - y), (1 - x, 1 - y)]
    return x, y, c, chips


def _ds(start, size, lane):
    if not isinstance(start, int):
        start = pl.multiple_of(start, 128 if lane else 16)
    return pl.ds(start, size)


WHOLE = (0, 1, 1)


def _window(ref, axis, shape, blk=None, half=None, sub=WHOLE):
    n = shape[axis] // N_CHIPS
    hs = shape[1 - axis] // 2
    idx = [slice(None), slice(None)]
    if blk is not None:
        idx[axis] = _ds(blk * n, n, axis == 1)
    first, count, pieces = sub
    ext = hs // pieces
    if half is not None:
        idx[1 - axis] = _ds(half * hs + first * ext, count * ext, axis == 0)
    elif pieces > 1:
        idx[1 - axis] = _ds(first * ext, count * ext, axis == 0)
    return ref.at[tuple(idx)]


def _remote(src, dst, send_sem, recv_sem, dev):
    return pltpu.make_async_remote_copy(src_ref=src, dst_ref=dst, send_sem=send_sem, recv_sem=recv_sem,
                                        device_id=dev, device_id_type=MESH)


def _full_shape(block_shape, axis):
    out = list(block_shape)
    out[axis] *= N_CHIPS
    return tuple(out)


def _half_all(shape, axis):
    out = list(shape)
    out[1 - axis] //= 2
    return tuple(out)


def _block(shape, axis):
    out = list(shape)
    out[axis] //= N_CHIPS
    return tuple(out)


def _half_block(shape, axis):
    return _half_all(_block(shape, axis), axis)


def gather_job(items):
    nw = len(items)
    shapes = [full.shape for full, _, _ in items]

    def start(pos, ins, outs, sems):
        x, y, c, chips = pos
        for w, (_, ax, sub) in enumerate(items):
            mine = _window(outs[w], ax, shapes[w], blk=2 * x + y, half=c, sub=sub)
            for j, (cx, cy) in enumerate(chips):
                _remote(mine, mine, sems[0].at[6 * w + j], sems[1].at[6 * w + j], (cx, cy, c)).start()

    def middle(pos, ins, outs, sems):
        x, y, c, chips = pos
        for w, (_, ax, sub) in enumerate(items):
            for j, (cx, cy) in enumerate(chips):
                landed = _window(outs[w], ax, shapes[w], blk=2 * cx + cy, half=c, sub=sub)
                _remote(landed, landed, sems[0].at[6 * w + j], sems[1].at[6 * w + j], (cx, cy, c)).wait_recv()
                _remote(landed, landed, sems[0].at[6 * w + 3 + j], sems[1].at[6 * w + 3 + j], (x, y, 1 - c)).start()

    def finish(pos, ins, outs, sems):
        x, y, c, chips = pos
        for w, (_, ax, sub) in enumerate(items):
            for j, (cx, cy) in enumerate(chips):
                passed = _window(outs[w], ax, shapes[w], blk=2 * cx + cy, half=1 - c, sub=sub)
                _remote(passed, passed, sems[0].at[6 * w + 3 + j], sems[1].at[6 * w + 3 + j],
                        (x, y, 1 - c)).wait_recv()
            mine = _window(outs[w], ax, shapes[w], blk=2 * x + y, half=c, sub=sub)
            for j in range(6):
                _remote(mine, mine, sems[0].at[6 * w + j], sems[1].at[6 * w + j], (x, y, 1 - c)).wait_send()

    sems = [pltpu.SemaphoreType.DMA((6 * nw,)), pltpu.SemaphoreType.DMA((6 * nw,))]
    return Job([full for full, _, _ in items], [_sds(full.shape, full.dtype) for full, _, _ in items],
               {w: w for w in range(nw)}, sems, start, middle, finish)


def pair_job(grads, axes):
    nw = len(grads)
    shapes = [g.shape for g in grads]

    def start(pos, ins, outs, sems):
        x, y, c, _ = pos
        for w in range(nw):
            _remote(_window(ins[w], axes[w], shapes[w], half=1 - c), outs[w], sems[0].at[w], sems[1].at[w],
                    (x, y, 1 - c)).start()

    def finish(pos, ins, outs, sems):
        x, y, c, _ = pos
        for w in range(nw):
            cp = _remote(outs[w], outs[w], sems[0].at[w], sems[1].at[w], (x, y, 1 - c))
            cp.wait_recv()
            cp.wait_send()

    sems = [pltpu.SemaphoreType.DMA((nw,)), pltpu.SemaphoreType.DMA((nw,))]
    return Job(list(grads), [_sds(_half_all(s, a), BF16) for s, a in zip(shapes, axes)], {}, sems, start, None,
               finish)


def cross_job(items):
    nw = len(items)
    inputs, aliases = [], {}
    for w, (part, ax, shape, prev, sub) in enumerate(items):
        inputs.append(part)
        if prev is not None:
            aliases[len(inputs)] = w
            inputs.append(prev)

    def copies(pos, ins, outs, sems):
        x, y, c, chips = pos
        k = 0
        for w, (_, ax, shape, prev, sub) in enumerate(items):
            src = ins[k]
            k += 2 if prev is not None else 1
            for j, (cx, cy) in enumerate(chips):
                slot = _window(outs[w].at[j], ax, shape, sub=sub)
                yield (_remote(_window(src, ax, shape, blk=2 * cx + cy, sub=sub), slot,
                               sems[0].at[3 * w + j], sems[1].at[3 * w + j], (cx, cy, c)),
                       _remote(slot, slot, sems[0].at[3 * w + j], sems[1].at[3 * w + j], (cx, cy, c)))

    def start(pos, ins, outs, sems):
        for send, _ in copies(pos, ins, outs, sems):
            send.start()

    def finish(pos, ins, outs, sems):
        for send, recv in copies(pos, ins, outs, sems):
            recv.wait_recv()
            send.wait_send()

    sems = [pltpu.SemaphoreType.DMA((3 * nw,)), pltpu.SemaphoreType.DMA((3 * nw,))]
    out_shapes = [_sds((3,) + _half_block(shape, ax), BF16) for _, ax, shape, _, _ in items]
    return Job(inputs, out_shapes, aliases, sems, start, None, finish)


def final_job(blocks, axes, shapes):
    nw = len(blocks)

    def start(pos, ins, outs, sems):
        x, y, c, _ = pos
        for w in range(nw):
            mine = _window(outs[w], axes[w], shapes[w], half=c)
            _remote(mine, mine, sems[0].at[w], sems[1].at[w], (x, y, 1 - c)).start()

    def finish(pos, ins, outs, sems):
        x, y, c, _ = pos
        for w in range(nw):
            theirs = _window(outs[w], axes[w], shapes[w], half=1 - c)
            cp = _remote(theirs, theirs, sems[0].at[w], sems[1].at[w], (x, y, 1 - c))
            cp.wait_recv()
            cp.wait_send()

    sems = [pltpu.SemaphoreType.DMA((nw,)), pltpu.SemaphoreType.DMA((nw,))]
    return Job(list(blocks), [_sds(b.shape, b.dtype) for b in blocks], {w: w for w in range(nw)}, sems, start, None,
               finish)


def stack_job(small):
    def peers(pos):
        x, y, c, _ = pos
        for k in range(1, 8):
            yield k - 1, (1 - x if k & 4 else x, 1 - y if k & 2 else y, 1 - c if k & 1 else c)

    def start(pos, ins, outs, sems):
        x, y, c, _ = pos
        mine = outs[0].at[4 * x + 2 * y + c]
        pltpu.make_async_copy(ins[0], mine, sems[2]).start()
        for k, dev in peers(pos):
            _remote(ins[0], mine, sems[0].at[k], sems[1].at[k], dev).start()

    def finish(pos, ins, outs, sems):
        x, y, c, _ = pos
        for k, (px, py, pc) in peers(pos):
            slot = outs[0].at[4 * px + 2 * py + pc]
            cp = _remote(slot, slot, sems[0].at[k], sems[1].at[k], (px, py, pc))
            cp.wait_recv()
            cp.wait_send()
        pltpu.make_async_copy(ins[0], outs[0].at[4 * x + 2 * y + c], sems[2]).wait()

    sems = [pltpu.SemaphoreType.DMA((7,)), pltpu.SemaphoreType.DMA((7,)), pltpu.SemaphoreType.DMA]
    return Job([small], [_sds((8,) + small.shape, small.dtype)], {}, sems, start, None, finish)


def columns_job(block):
    cols = block.shape[1]
    place = lambda out, b: out.at[:, _ds(b * cols, cols, True)]

    def start(pos, ins, outs, sems):
        x, y, c, chips = pos
        pltpu.make_async_copy(ins[0], place(outs[0], 2 * x + y), sems[2]).start()
        for j, (cx, cy) in enumerate(chips):
            _remote(ins[0], place(outs[0], 2 * x + y), sems[0].at[j], sems[1].at[j], (cx, cy, c)).start()

    def finish(pos, ins, outs, sems):
        x, y, c, chips = pos
        for j, (cx, cy) in enumerate(chips):
            got = place(outs[0], 2 * cx + cy)
            cp = _remote(got, got, sems[0].at[j], sems[1].at[j], (cx, cy, c))
            cp.wait_recv()
            cp.wait_send()
        pltpu.make_async_copy(ins[0], place(outs[0], 2 * x + y), sems[2]).wait()

    sems = [pltpu.SemaphoreType.DMA((3,)), pltpu.SemaphoreType.DMA((3,)), pltpu.SemaphoreType.DMA]
    return Job([block], [_sds((block.shape[0], N_CHIPS * cols), block.dtype)], {}, sems, start, None, finish)


def _call(name, body, grid, in_specs, out_specs, out_shape, args, scratch=(), sem=None, jobs=()):
    n_in, n_out, n_sc = len(args), len(out_shape), len(scratch)
    if not jobs:
        outs = pl.pallas_call(
            body, name=name, grid=grid, in_specs=in_specs, out_specs=out_specs, out_shape=out_shape,
            scratch_shapes=list(scratch), compiler_params=_params(sem))(*args)
        return list(outs), []

    total = math.prod(grid) if grid else 1
    mid = min(total - 1, (3 * total) // 4)

    def split(refs, start, counts):
        out = []
        for n in counts:
            out.append(refs[start:start + n])
            start += n
        return out, start

    def wrapped(*refs):
        c_in = refs[:n_in]
        j_ins, p = split(refs, n_in, [len(j.inputs) for j in jobs])
        c_out = refs[p:p + n_out]
        j_outs, p = split(refs, p + n_out, [len(j.out_shapes) for j in jobs])
        c_sc = refs[p:p + n_sc]
        j_sems, p = split(refs, p + n_sc, [len(j.sems) for j in jobs])
        pos = _place()
        step = 0
        for axis, extent in enumerate(grid):
            step = step * extent + pl.program_id(axis)

        def run(phase):
            for j, ins, outs, sems in zip(jobs, j_ins, j_outs, j_sems):
                fn = getattr(j, phase)
                if fn is not None:
                    fn(pos, ins, outs, sems)

        if total == 1:
            run("start")
            body(*c_in, *c_out, *c_sc)
            run("middle")
            run("finish")
            return
        pl.when(step == 0)(lambda: run("start"))
        body(*c_in, *c_out, *c_sc)
        if any(j.middle is not None for j in jobs):
            pl.when(step == mid)(lambda: run("middle"))
        pl.when(step == total - 1)(lambda: run("finish"))

    aliases, in_at, out_at = {}, n_in, n_out
    for j in jobs:
        for src, dst in j.aliases.items():
            aliases[in_at + src] = out_at + dst
        in_at += len(j.inputs)
        out_at += len(j.out_shapes)
    outs = pl.pallas_call(
        wrapped, name=name, grid=grid,
        in_specs=list(in_specs) + [ANY] * (in_at - n_in),
        out_specs=list(out_specs) + [ANY] * (out_at - n_out),
        out_shape=list(out_shape) + [s for j in jobs for s in j.out_shapes],
        scratch_shapes=list(scratch) + [s for j in jobs for s in j.sems],
        input_output_aliases=aliases,
        compiler_params=_params(("arbitrary",) * len(grid)),
    )(*args, *[a for j in jobs for a in j.inputs])
    job_outs, p = split(outs, n_out, [len(j.out_shapes) for j in jobs])
    return list(outs[:n_out]), [list(o) for o in job_outs]


def comm_only(name, jobs):
    def body(dummy_ref, out_ref):
        out_ref[...] = dummy_ref[...]

    dummy = jnp.zeros((8, 128), F32)
    spec = pl.BlockSpec((8, 128), lambda: (0, 0))
    return _call(name, body, (), [spec], [spec], [_sds((8, 128), F32)], [dummy], jobs=jobs)[1]


def _ret(outs, job_outs, jobs, single=True):
    res = outs[0] if single else outs
    return (res, job_outs) if jobs else res


def rmsnorm_fwd(name, x, g):
    s, d = x.shape
    tm = _tile(s, 512, 8)

    def body(x_ref, g_ref, n_ref, r_ref):
        xv = x_ref[...]
        r = lax.rsqrt(jnp.mean(xv * xv, axis=-1, keepdims=True) + EPS)
        n_ref[...] = (xv * r * g_ref[...]).astype(BF16)
        r_ref[...] = r

    row = lambda i: (i, 0)
    return _call(
        name, body, (s // tm,),
        [pl.BlockSpec((tm, d), row), pl.BlockSpec((1, d), lambda i: (0, 0))],
        [pl.BlockSpec((tm, d), row), pl.BlockSpec((tm, 1), row)],
        [_sds((s, d), BF16), _sds((s, 1), F32)], [x, g], sem=("arbitrary",))[0]


def rmsnorm_bwd(name, dn, x, r, g, dh_in, jobs=()):
    s, d = x.shape
    tm = _tile(s, 512, 8)

    def body(dn_ref, x_ref, r_ref, g_ref, dh_ref, out_ref, outb_ref, dg_ref):
        i = pl.program_id(0)
        xh = x_ref[...] * r_ref[...]
        dnv = dn_ref[...]
        dxh = dnv * g_ref[...]
        dx = r_ref[...] * (dxh - xh * jnp.mean(dxh * xh, axis=-1, keepdims=True))
        out = dh_ref[...] + dx
        out_ref[...] = out
        outb_ref[...] = out.astype(BF16)
        part = jnp.sum(dnv * xh, axis=0, keepdims=True)

        @pl.when(i == 0)
        def _():
            dg_ref[...] = part

        @pl.when(i > 0)
        def _():
            dg_ref[...] += part

    row = lambda i: (i, 0)
    fixed = lambda i: (0, 0)
    outs, job_outs = _call(
        name, body, (s // tm,),
        [pl.BlockSpec((tm, d), row), pl.BlockSpec((tm, d), row), pl.BlockSpec((tm, 1), row),
         pl.BlockSpec((1, d), fixed), pl.BlockSpec((tm, d), row)],
        [pl.BlockSpec((tm, d), row), pl.BlockSpec((tm, d), row), pl.BlockSpec((1, d), fixed)],
        [_sds((s, d), F32), _sds((s, d), BF16), _sds((1, d), F32)], [dn, x, r, g, dh_in],
        sem=("arbitrary",), jobs=jobs)
    return _ret(outs, job_outs, jobs, single=False)


def gain_grad(name, dn_a, dn_b, x, r):
    s, d = x.shape
    tm = _tile(s, 512, 8)

    def body(a_ref, b_ref, x_ref, r_ref, dg_ref):
        i = pl.program_id(0)
        part = jnp.sum((a_ref[...] + b_ref[...]) * (x_ref[...] * r_ref[...]), axis=0, keepdims=True)

        @pl.when(i == 0)
        def _():
            dg_ref[...] = part

        @pl.when(i > 0)
        def _():
            dg_ref[...] += part

    row = lambda i: (i, 0)
    return _call(
        name, body, (s // tm,),
        [pl.BlockSpec((tm, d), row), pl.BlockSpec((tm, d), row), pl.BlockSpec((tm, d), row),
         pl.BlockSpec((tm, 1), row)],
        [pl.BlockSpec((1, d), lambda i: (0, 0))], [_sds((1, d), F32)], [dn_a, dn_b, x, r],
        sem=("arbitrary",))[0][0]


def loss_head(name, h, g, target):
    s, d = h.shape
    tm = _tile(s, 512, 8)
    nsteps = s // tm

    def body(h_ref, g_ref, t_ref, loss_ref, dh_ref, dhb_ref, dg_ref, sq_ref):
        i = pl.program_id(0)
        hv = h_ref[...]
        gv = g_ref[...]
        r = lax.rsqrt(jnp.mean(hv * hv, axis=-1, keepdims=True) + EPS)
        xh = hv * r
        err = xh * gv - t_ref[...]
        dy = err * (1.0 / d)
        dxh = dy * gv
        dh = r * (dxh - xh * jnp.mean(dxh * xh, axis=-1, keepdims=True))
        dh_ref[...] = dh
        dhb_ref[...] = dh.astype(BF16)
        dg_part = jnp.sum(dy * xh, axis=0, keepdims=True)
        sq_part = jnp.sum(err * err, axis=0, keepdims=True)

        @pl.when(i == 0)
        def _():
            dg_ref[...] = dg_part
            sq_ref[...] = sq_part

        @pl.when(i > 0)
        def _():
            dg_ref[...] += dg_part
            sq_ref[...] += sq_part

        @pl.when(i == nsteps - 1)
        def _():
            total = jnp.sum(sq_ref[...], axis=-1, keepdims=True) * (0.5 / d)
            loss_ref[...] = jnp.broadcast_to(total, loss_ref.shape)

    row = lambda i: (i, 0)
    fixed = lambda i: (0, 0)
    return _call(
        name, body, (nsteps,),
        [pl.BlockSpec((tm, d), row), pl.BlockSpec((1, d), fixed), pl.BlockSpec((tm, d), row)],
        [pl.BlockSpec((8, 128), fixed), pl.BlockSpec((tm, d), row), pl.BlockSpec((tm, d), row),
         pl.BlockSpec((1, d), fixed)],
        [_sds((8, 128), F32), _sds((s, d), F32), _sds((s, d), BF16), _sds((1, d), F32)], [h, g, target],
        scratch=[pltpu.VMEM((1, d), F32)], sem=("arbitrary",))[0]


def _mm(name, grid, in_arrays, in_specs, out_shapes, out_specs, acc_tile, dot, epilogue, jobs=()):
    nk = grid[2]
    n_in = len(in_arrays)
    n_out = len(out_shapes)

    def body(*refs):
        ins, outs = refs[:n_in], refs[n_in:n_in + n_out]
        part = dot(*ins)
        if nk == 1:
            epilogue(part, ins, outs)
            return
        acc = refs[n_in + n_out]
        k = pl.program_id(2)

        @pl.when(k == 0)
        def _():
            acc[...] = part

        @pl.when(jnp.logical_and(k > 0, k < nk - 1))
        def _():
            acc[...] += part

        @pl.when(k == nk - 1)
        def _():
            epilogue(acc[...] + part, ins, outs)

    scratch = [pltpu.VMEM(acc_tile, F32)] if nk > 1 else []
    outs, job_outs = _call(name, body, grid, in_specs, out_specs, out_shapes, in_arrays, scratch=scratch,
                           sem=("parallel", "parallel", "arbitrary"), jobs=jobs)
    return _ret(outs, job_outs, jobs)


def _store(scale, dtype):
    def epilogue(acc, ins, outs):
        outs[0][...] = (acc * scale if scale != 1.0 else acc).astype(dtype)
    return epilogue


def mm_nn(name, a, w, out_dtype, tm=1024, tn=1024, tk=2048, jobs=()):
    m, kd = a.shape
    n = w.shape[1]
    tm, tn, tk = _tile(m, tm, 8), _tile(n, tn), _tile(kd, tk)
    return _mm(
        name, (n // tn, m // tm, kd // tk), [a, w],
        [pl.BlockSpec((tm, tk), lambda j, i, k: (i, k)), pl.BlockSpec((tk, tn), lambda j, i, k: (k, j))],
        [_sds((m, n), out_dtype)], [pl.BlockSpec((tm, tn), lambda j, i, k: (i, j))], (tm, tn),
        lambda a_ref, w_ref: _dot_nn(a_ref[...], w_ref[...]), _store(1.0, out_dtype), jobs)


def mm_nn_resid(name, a, w, x, scale, tm=1024, tn=1024, tk=1408, jobs=()):
    m, kd = a.shape
    n = w.shape[1]
    tm, tn, tk = _tile(m, tm, 8), _tile(n, tn), _tile(kd, tk)

    def epilogue(acc, ins, outs):
        outs[0][...] = ins[2][...] + scale * acc

    return _mm(
        name, (n // tn, m // tm, kd // tk), [a, w, x],
        [pl.BlockSpec((tm, tk), lambda j, i, k: (i, k)), pl.BlockSpec((tk, tn), lambda j, i, k: (k, j)),
         pl.BlockSpec((tm, tn), lambda j, i, k: (i, j))],
        [_sds((m, n), F32)], [pl.BlockSpec((tm, tn), lambda j, i, k: (i, j))], (tm, tn),
        lambda a_ref, w_ref, x_ref: _dot_nn(a_ref[...], w_ref[...]), epilogue, jobs)


def mm_nt(name, a, w, out_dtype, scale=1.0, tm=1024, tn=1024, tk=2048, jobs=()):
    m, kd = a.shape
    n = w.shape[0]
    tm, tn, tk = _tile(m, tm, 8), _tile(n, tn), _tile(kd, tk)
    return _mm(
        name, (n // tn, m // tm, kd // tk), [a, w],
        [pl.BlockSpec((tm, tk), lambda j, i, k: (i, k)), pl.BlockSpec((tn, tk), lambda j, i, k: (j, k))],
        [_sds((m, n), out_dtype)], [pl.BlockSpec((tm, tn), lambda j, i, k: (i, j))], (tm, tn),
        lambda a_ref, w_ref: _dot_nt(a_ref[...], w_ref[...]), _store(scale, out_dtype), jobs)


def mm_nt_pair(name, a3, w, out_dtype, tm=1024, tn=1024, tk=1408, jobs=()):
    _, m, f = a3.shape
    n = w.shape[0]
    tm, tn, tk = _tile(m, tm, 8), _tile(n, tn), _tile(f, tk)
    nkf = f // tk
    return _mm(
        name, (n // tn, m // tm, 2 * nkf), [a3, w],
        [pl.BlockSpec((None, tm, tk), lambda j, i, k: (k // nkf, i, k % nkf)),
         pl.BlockSpec((tn, tk), lambda j, i, k: (j, k))],
        [_sds((m, n), out_dtype)], [pl.BlockSpec((tm, tn), lambda j, i, k: (i, j))], (tm, tn),
        lambda a_ref, w_ref: _dot_nt(a_ref[...], w_ref[...]), _store(1.0, out_dtype), jobs)


def mm_tn(name, a, b, out_dtype, scale=1.0, tm=1024, tn=1024, tk=1024, jobs=()):
    kd, m = a.shape
    n = b.shape[1]
    tm, tn, tk = _tile(m, tm), _tile(n, tn), _tile(kd, tk, 16)
    return _mm(
        name, (n // tn, m // tm, kd // tk), [a, b],
        [pl.BlockSpec((tk, tm), lambda j, i, k: (k, i)), pl.BlockSpec((tk, tn), lambda j, i, k: (k, j))],
        [_sds((m, n), out_dtype)], [pl.BlockSpec((tm, tn), lambda j, i, k: (i, j))], (tm, tn),
        lambda a_ref, b_ref: _dot_tn(a_ref[...], b_ref[...]), _store(scale, out_dtype), jobs)


def mm_tn_pair(name, a, b3, out_dtype, tm=1024, tn=1408, tk=1024, jobs=()):
    kd, m = a.shape
    f = b3.shape[2]
    tm, tn, tk = _tile(m, tm), _tile(f, tn), _tile(kd, tk, 16)
    nf = f // tn
    return _mm(
        name, (2 * nf, m // tm, kd // tk), [a, b3],
        [pl.BlockSpec((tk, tm), lambda j, i, k: (k, i)),
         pl.BlockSpec((None, tk, tn), lambda j, i, k: (j // nf, k, j % nf))],
        [_sds((m, 2 * f), out_dtype)], [pl.BlockSpec((tm, tn), lambda j, i, k: (i, j))], (tm, tn),
        lambda a_ref, b_ref: _dot_tn(a_ref[...], b_ref[...]), _store(1.0, out_dtype), jobs)


def swiglu_fwd(name, n, w_in, tm=1024, tn=512, jobs=()):
    s, d = n.shape
    f = w_in.shape[1] // 2
    tm, tn = _tile(s, tm, 8), _tile(f, tn)
    nf = f // tn

    def body(n_ref, wg_ref, wu_ref, gu_ref, a_ref):
        nv = n_ref[...]
        g = _dot_nn(nv, wg_ref[...])
        u = _dot_nn(nv, wu_ref[...])
        gu_ref[0] = g.astype(BF16)
        gu_ref[1] = u.astype(BF16)
        a_ref[...] = (g * jax.nn.sigmoid(g) * u).astype(BF16)

    outs, job_outs = _call(
        name, body, (nf, s // tm),
        [pl.BlockSpec((tm, d), lambda j, i: (i, 0)), pl.BlockSpec((d, tn), lambda j, i: (0, j)),
         pl.BlockSpec((d, tn), lambda j, i: (0, j + nf))],
        [pl.BlockSpec((2, tm, tn), lambda j, i: (0, i, j)), pl.BlockSpec((tm, tn), lambda j, i: (i, j))],
        [_sds((2, s, f), BF16), _sds((s, f), BF16)], [n, w_in, w_in], sem=("parallel", "parallel"), jobs=jobs)
    return _ret(outs, job_outs, jobs, single=False)


def swiglu_bwd(name, dh, w_out, gu, scale, tm=1024, tn=512, jobs=()):
    s, d = dh.shape
    f = w_out.shape[0]
    tm, tn = _tile(s, tm, 8), _tile(f, tn)

    def body(dh_ref, w_ref, gu_ref, out_ref):
        da = _dot_nt(dh_ref[...], w_ref[...]) * scale
        g = gu_ref[0].astype(F32)
        u = gu_ref[1].astype(F32)
        sg = jax.nn.sigmoid(g)
        out_ref[0] = (da * u * (sg * (1.0 + g * (1.0 - sg)))).astype(BF16)
        out_ref[1] = (da * (g * sg)).astype(BF16)

    outs, job_outs = _call(
        name, body, (f // tn, s // tm),
        [pl.BlockSpec((tm, d), lambda j, i: (i, 0)), pl.BlockSpec((tn, d), lambda j, i: (j, 0)),
         pl.BlockSpec((2, tm, tn), lambda j, i: (0, i, j))],
        [pl.BlockSpec((2, tm, tn), lambda j, i: (0, i, j))],
        [_sds((2, s, f), BF16)], [dh, w_out, gu], sem=("parallel", "parallel"), jobs=jobs)
    return _ret(outs, job_outs, jobs)


HALO = 16


def _conv_inputs(z_ref, hgc_ref, hhc_ref, i, cw, tm):
    gc = z_ref[:, cw:2 * cw].astype(F32)
    hc = z_ref[:, 2 * cw:3 * cw].astype(F32)
    cin = gc * hc
    halo = hgc_ref[...].astype(F32) * hhc_ref[...].astype(F32) * (i > 0).astype(F32)
    row = lax.broadcasted_iota(jnp.int32, (tm, cw), 0)
    x1 = jnp.where(row == 0, halo[HALO - 1:HALO], pltpu.roll(cin, 1, 0))
    x2 = jnp.where(row == 0, halo[HALO - 2:HALO - 1], jnp.where(row == 1, halo[HALO - 1:HALO], pltpu.roll(cin, 2, 0)))
    return gc, hc, cin, x1, x2


def _tril(w):
    r = lax.broadcasted_iota(jnp.int32, w.shape, 0)
    c = lax.broadcasted_iota(jnp.int32, w.shape, 1)
    return jnp.where(r >= c, w, jnp.zeros_like(w))


def mixer_fwd(name, z, conv_w, conv_b, g_v, w_s, b_t, tm=256, jobs=()):
    s, zc = z.shape
    cw = conv_w.shape[1]
    gw = g_v.shape[1]
    heads = gw // GROUP
    tm = _tile(s, tm)
    hb = tm // HALO

    def body(z_ref, hgc_ref, hhc_ref, cw_ref, cb_ref, gv_ref, ws_ref, bt_ref, y_ref):
        i = pl.program_id(0)
        _, _, cin, x1, x2 = _conv_inputs(z_ref, hgc_ref, hhc_ref, i, cw, tm)
        cv = cb_ref[...] + cw_ref[2:3, :] * cin + cw_ref[1:2, :] * x1 + cw_ref[0:1, :] * x2
        y_ref[:, 0:cw] = (z_ref[:, 0:cw].astype(F32) * cv).astype(BF16)
        for h in range(heads):
            lo = h * GROUP
            vh = z_ref[:, 3 * cw + gw + lo:3 * cw + gw + lo + GROUP].astype(F32)
            rv = lax.rsqrt(jnp.mean(vh * vh, axis=-1, keepdims=True) + EPS)
            vn = (vh * rv * gv_ref[:, lo:lo + GROUP]).astype(BF16)
            w = _tril(ws_ref[h]).astype(BF16)
            for n in range(tm // GROUP):
                rows = slice(n * GROUP, (n + 1) * GROUP)
                sg = _dot_nn(w, vn[rows]) + bt_ref[:, h:h + 1]
                u = z_ref[rows, 3 * cw + lo:3 * cw + lo + GROUP].astype(F32)
                y_ref[rows, cw + lo:cw + lo + GROUP] = (u * sg).astype(BF16)

    fixed2 = lambda i: (0, 0)
    outs, job_outs = _call(
        name, body, (s // tm,),
        [pl.BlockSpec((tm, zc), lambda i: (i, 0)),
         pl.BlockSpec((HALO, cw), lambda i: (jnp.maximum(i * hb - 1, 0), 1)),
         pl.BlockSpec((HALO, cw), lambda i: (jnp.maximum(i * hb - 1, 0), 2)),
         pl.BlockSpec(conv_w.shape, fixed2), pl.BlockSpec(conv_b.shape, fixed2),
         pl.BlockSpec(g_v.shape, fixed2), pl.BlockSpec(w_s.shape, lambda i: (0, 0, 0)),
         pl.BlockSpec(b_t.shape, fixed2)],
        [pl.BlockSpec((tm, cw + gw), lambda i: (i, 0))], [_sds((s, cw + gw), BF16)],
        [z, z, z, conv_w, conv_b, g_v, w_s, b_t], sem=("arbitrary",), jobs=jobs)
    return _ret(outs, job_outs, jobs)


def mixer_bwd(name, z, dy, conv_w, conv_b, g_v, w_s, b_t, tm=256, jobs=()):
    s, zc = z.shape
    cw = conv_w.shape[1]
    gw = g_v.shape[1]
    heads = gw // GROUP
    tm = _tile(s, tm)
    hb = tm // HALO
    nsteps = s // tm
    last_halo = s // HALO - 1

    def body(z_ref, hgc_ref, hhc_ref, ngb_ref, dy_ref, ndy_ref, cw_ref, cb_ref, gv_ref, ws_ref, bt_ref,
             dz_ref, sm_ref, dws_ref, dbt_ref, dsg_ref):
        i = pl.program_id(0)

        @pl.when(i == 0)
        def _():
            sm_ref[...] = jnp.zeros_like(sm_ref)
            dws_ref[...] = jnp.zeros_like(dws_ref)
            dsg_ref[...] = jnp.zeros_like(dsg_ref)

        gc, hc, cin, x1, x2 = _conv_inputs(z_ref, hgc_ref, hhc_ref, i, cw, tm)
        w0, w1, w2 = cw_ref[0:1, :], cw_ref[1:2, :], cw_ref[2:3, :]
        cv = cb_ref[...] + w2 * cin + w1 * x1 + w0 * x2
        gb = z_ref[:, 0:cw].astype(F32)
        dyc = dy_ref[:, 0:cw].astype(F32)
        dz_ref[:, 0:cw] = (dyc * cv).astype(BF16)
        dcv = dyc * gb
        nxt = ndy_ref[...].astype(F32) * ngb_ref[...].astype(F32) * (i < nsteps - 1).astype(F32)
        row = lax.broadcasted_iota(jnp.int32, (tm, cw), 0)
        d1 = jnp.where(row == tm - 1, nxt[0:1], pltpu.roll(dcv, tm - 1, 0))
        d2 = jnp.where(row == tm - 1, nxt[1:2], jnp.where(row == tm - 2, nxt[0:1], pltpu.roll(dcv, tm - 2, 0)))
        dcin = w2 * dcv + w1 * d1 + w0 * d2
        dz_ref[:, cw:2 * cw] = (dcin * hc).astype(BF16)
        dz_ref[:, 2 * cw:3 * cw] = (dcin * gc).astype(BF16)
        sm_ref[0:1, :] += jnp.sum(dcv * x2, axis=0, keepdims=True)
        sm_ref[1:2, :] += jnp.sum(dcv * x1, axis=0, keepdims=True)
        sm_ref[2:3, :] += jnp.sum(dcv * cin, axis=0, keepdims=True)
        sm_ref[3:4, :] += jnp.sum(dcv, axis=0, keepdims=True)

        for h in range(heads):
            lo = h * GROUP
            vcol = slice(3 * cw + gw + lo, 3 * cw + gw + lo + GROUP)
            ucol = slice(3 * cw + lo, 3 * cw + lo + GROUP)
            vh = z_ref[:, vcol].astype(F32)
            rv = lax.rsqrt(jnp.mean(vh * vh, axis=-1, keepdims=True) + EPS)
            xh = vh * rv
            gvh = gv_ref[:, lo:lo + GROUP]
            vn = (xh * gvh).astype(BF16)
            w = _tril(ws_ref[h]).astype(BF16)
            dgv = jnp.zeros((1, GROUP), F32)
            for n in range(tm // GROUP):
                rows = slice(n * GROUP, (n + 1) * GROUP)
                sg = _dot_nn(w, vn[rows]) + bt_ref[:, h:h + 1]
                dyg = dy_ref[rows, cw + lo:cw + lo + GROUP].astype(F32)
                dsg = dyg * z_ref[rows, ucol].astype(F32)
                dz_ref[rows, ucol] = (dyg * sg).astype(BF16)
                dsgb = dsg.astype(BF16)
                dvn = _dot_tn(w, dsgb)
                dws_ref[h] += _dot_nt(dsgb, vn[rows])
                dsg_ref[:, lo:lo + GROUP] += dsg
                xhc = xh[rows]
                dgv = dgv + jnp.sum(dvn * xhc, axis=0, keepdims=True)
                dxh = dvn * gvh
                dv = rv[rows] * (dxh - xhc * jnp.mean(dxh * xhc, axis=-1, keepdims=True))
                dz_ref[rows, vcol] = dv.astype(BF16)
            sm_ref[4:5, lo:lo + GROUP] += dgv

        @pl.when(i == nsteps - 1)
        def _():
            for h in range(heads):
                dws_ref[h] = _tril(dws_ref[h])
                dbt_ref[:, h:h + 1] = jnp.sum(dsg_ref[:, h * GROUP:(h + 1) * GROUP], axis=-1, keepdims=True)

    fixed2 = lambda i: (0, 0)
    fixed3 = lambda i: (0, 0, 0)
    prev = lambda col: (lambda i: (jnp.maximum(i * hb - 1, 0), col))
    nxt_blk = lambda i: (jnp.minimum((i + 1) * hb, last_halo), 0)
    outs, job_outs = _call(
        name, body, (nsteps,),
        [pl.BlockSpec((tm, zc), lambda i: (i, 0)),
         pl.BlockSpec((HALO, cw), prev(1)), pl.BlockSpec((HALO, cw), prev(2)),
         pl.BlockSpec((HALO, cw), nxt_blk),
         pl.BlockSpec((tm, cw + gw), lambda i: (i, 0)), pl.BlockSpec((HALO, cw), nxt_blk),
         pl.BlockSpec(conv_w.shape, fixed2), pl.BlockSpec(conv_b.shape, fixed2),
         pl.BlockSpec(g_v.shape, fixed2), pl.BlockSpec(w_s.shape, fixed3), pl.BlockSpec(b_t.shape, fixed2)],
        [pl.BlockSpec((tm, zc), lambda i: (i, 0)), pl.BlockSpec((8, cw), fixed2),
         pl.BlockSpec(w_s.shape, fixed3), pl.BlockSpec(b_t.shape, fixed2)],
        [_sds((s, zc), BF16), _sds((8, cw), F32), _sds(w_s.shape, F32), _sds(b_t.shape, F32)],
        [z, z, z, z, dy, dy, conv_w, conv_b, g_v, w_s, b_t],
        scratch=[pltpu.VMEM((GROUP, gw), F32)], sem=("arbitrary",), jobs=jobs)
    return _ret(outs, job_outs, jobs, single=False)


def _softmax_rows(sc):
    e = jnp.exp(sc - jnp.max(sc, axis=-1, keepdims=True))
    return e / jnp.sum(e, axis=-1, keepdims=True)


def attn_fwd(name, q, k, v, tm=512):
    s, d = q.shape
    m = k.shape[0]
    hd = d // XA_HEADS
    scale = hd ** -0.5
    tm = _tile(s, tm, 8)

    def body(q_ref, k_ref, v_ref, o_ref):
        for h in range(XA_HEADS):
            cols = slice(h * hd, (h + 1) * hd)
            p = _softmax_rows(_dot_nt(q_ref[:, cols], k_ref[:, cols]) * scale)
            o_ref[:, cols] = _dot_nn(p.astype(BF16), v_ref[:, cols]).astype(BF16)

    return _call(
        name, body, (s // tm,),
        [pl.BlockSpec((tm, d), lambda i: (i, 0)), pl.BlockSpec((m, d), lambda i: (0, 0)),
         pl.BlockSpec((m, d), lambda i: (0, 0))],
        [pl.BlockSpec((tm, d), lambda i: (i, 0))], [_sds((s, d), BF16)], [q, k, v], sem=("arbitrary",))[0][0]


def attn_bwd(name, q, k, v, do, tm=512):
    s, d = q.shape
    m = k.shape[0]
    hd = d // XA_HEADS
    scale = hd ** -0.5
    tm = _tile(s, tm, 8)

    def body(q_ref, k_ref, v_ref, do_ref, dq_ref, dk_ref, dv_ref):
        i = pl.program_id(0)

        @pl.when(i == 0)
        def _():
            dk_ref[...] = jnp.zeros_like(dk_ref)
            dv_ref[...] = jnp.zeros_like(dv_ref)

        for h in range(XA_HEADS):
            cols = slice(h * hd, (h + 1) * hd)
            qh = q_ref[:, cols]
            doh = do_ref[:, cols]
            p = _softmax_rows(_dot_nt(qh, k_ref[:, cols]) * scale)
            dp = _dot_nt(doh, v_ref[:, cols])
            ds = (p * (dp - jnp.sum(dp * p, axis=-1, keepdims=True)) * scale).astype(BF16)
            dq_ref[:, cols] = _dot_nn(ds, k_ref[:, cols]).astype(BF16)
            dk_ref[:, cols] += _dot_tn(ds, qh)
            dv_ref[:, cols] += _dot_tn(p.astype(BF16), doh)

    row = lambda i: (i, 0)
    fixed = lambda i: (0, 0)
    return _call(
        name, body, (s // tm,),
        [pl.BlockSpec((tm, d), row), pl.BlockSpec((m, d), fixed), pl.BlockSpec((m, d), fixed),
         pl.BlockSpec((tm, d), row)],
        [pl.BlockSpec((tm, d), row), pl.BlockSpec((m, d), fixed), pl.BlockSpec((m, d), fixed)],
        [_sds((s, d), BF16), _sds((m, d), F32), _sds((m, d), F32)], [q, k, v, do], sem=("arbitrary",))[0]


def _grid2(rows, cols, row_mult):
    tr, tc = _tile(rows, 512, row_mult), _tile(cols, 2048)
    return tr, tc, rows // tr, cols // tc


def cast_place(name, block, axis, place):
    r, c = block.shape
    tr, tc, nbr, nbc = _grid2(r, c, 16)
    if axis == 1:
        dst = lambda i, j, p: (i, j + p[0] * nbc)
    else:
        dst = lambda i, j, p: (i + p[0] * nbr, j)

    def body(p_ref, w_ref, out_ref):
        out_ref[...] = w_ref[...].astype(BF16)

    return pl.pallas_call(
        body, name=name,
        grid_spec=pltpu.PrefetchScalarGridSpec(
            num_scalar_prefetch=1, grid=(nbr, nbc),
            in_specs=[pl.BlockSpec((tr, tc), lambda i, j, p: (i, j))],
            out_specs=pl.BlockSpec((tr, tc), dst)),
        out_shape=_sds(_full_shape(block.shape, axis), BF16),
        compiler_params=_params(("parallel", "parallel")),
    )(place, block)


def pair_add(name, grad, peer, axis, place):
    hr, hc = peer.shape
    tr, tc, nbr, nbc = _grid2(hr, hc, 16)
    if axis == 1:
        mine = lambda i, j, p: (i + p[1] * nbr, j)
    else:
        mine = lambda i, j, p: (i, j + p[1] * nbc)
    same = lambda i, j, p: (i, j)

    def body(p_ref, g_ref, q_ref, out_ref):
        out_ref[...] = (g_ref[...].astype(F32) + q_ref[...].astype(F32)).astype(BF16)

    return pl.pallas_call(
        body, name=name,
        grid_spec=pltpu.PrefetchScalarGridSpec(
            num_scalar_prefetch=1, grid=(nbr, nbc),
            in_specs=[pl.BlockSpec((tr, tc), mine), pl.BlockSpec((tr, tc), same)],
            out_specs=pl.BlockSpec((tr, tc), same)),
        out_shape=_sds((hr, hc), BF16),
        compiler_params=_params(("parallel", "parallel")),
    )(place, grad, peer)


def cross_sum(name, part, land, axis, shape, place):
    _, sr, sc = land.shape
    tr, tc, nbr, nbc = _grid2(sr, sc, 16)
    if axis == 1:
        own = lambda i, j, p: (i, j + p[0] * nbc)
        dst = lambda i, j, p: (i + p[1] * nbr, j)
    else:
        own = lambda i, j, p: (i + p[0] * nbr, j)
        dst = lambda i, j, p: (i, j + p[1] * nbc)

    def body(p_ref, own_ref, land_ref, out_ref):
        out_ref[...] = ((own_ref[...].astype(F32) + land_ref[0].astype(F32))
                        + (land_ref[1].astype(F32) + land_ref[2].astype(F32)))

    return pl.pallas_call(
        body, name=name,
        grid_spec=pltpu.PrefetchScalarGridSpec(
            num_scalar_prefetch=1, grid=(nbr, nbc),
            in_specs=[pl.BlockSpec((tr, tc), own), pl.BlockSpec((3, tr, tc), lambda i, j, p: (0, i, j))],
            out_specs=pl.BlockSpec((tr, tc), dst)),
        out_shape=_sds(_block(shape, axis), F32),
        compiler_params=_params(("parallel", "parallel")),
    )(place, part, land)


def _adam_math(w, g, m, v):
    m = ADAM_B1 * m + (1.0 - ADAM_B1) * g
    v = ADAM_B2 * v + (1.0 - ADAM_B2) * (g * g)
    m_hat = m / (1.0 - ADAM_B1 ** ADAM_STEP)
    v_hat = v / (1.0 - ADAM_B2 ** ADAM_STEP)
    delta = -ADAM_LR * (m_hat / (jnp.sqrt(v_hat) + ADAM_EPS) + ADAM_WD * w)
    return delta, m, v


def adamw(name, w, g, m, v):
    r, c = w.shape
    tr, tc = _tile(r, 256, 8), _tile(c, 1408)

    def body(w_ref, g_ref, m_ref, v_ref, d_out, m_out, v_out):
        d, mm, vv = _adam_math(w_ref[...], g_ref[...], m_ref[...], v_ref[...])
        d_out[...] = d
        m_out[...] = mm
        v_out[...] = vv

    spec = pl.BlockSpec((tr, tc), lambda i, j: (i, j))
    return _call(name, body, (r // tr, c // tc), [spec] * 4, [spec] * 3, [_sds((r, c), F32)] * 3, [w, g, m, v],
                 sem=("parallel", "parallel"))[0]


def small_sum(name, stacks):
    def body(*refs):
        for s_ref, out_ref in zip(refs[:len(stacks)], refs[len(stacks):]):
            acc = s_ref[0]
            for d in range(1, s_ref.shape[0]):
                acc = acc + s_ref[d]
            out_ref[...] = acc

    return pl.pallas_call(body, name=name, out_shape=[_sds(s.shape[1:], F32) for s in stacks])(*stacks)


WEIGHTS = ["g_ffn1", "w_ffn1_in", "w_ffn1_out", "g_mix", "w_mix_in", "conv_w", "conv_b", "g_gm_v", "w_spatial",
           "b_spatial", "w_mix_out", "g_xattn", "g_mem", "w_xq", "w_xk", "w_xv", "w_xo", "g_ffn2", "w_ffn2_in",
           "w_ffn2_out", "g_final"]
BIG = {"w_ffn1_in": 1, "w_ffn1_out": 0, "w_mix_in": 1, "w_mix_out": 0, "w_xq": 0, "w_xk": 0, "w_xv": 0, "w_xo": 0,
       "w_ffn2_in": 1, "w_ffn2_out": 0}
SMALL = [n for n in WEIGHTS if n not in BIG]
LATE_SMALL = ["g_ffn1"]
EARLY_SMALL = [n for n in SMALL if n not in LATE_SMALL]


def _pack(arrays):
    flat = jnp.concatenate([a.reshape(-1) for a in arrays])
    rows = -(-flat.shape[0] // 1024) * 8
    return jnp.pad(flat, (0, rows * 128 - flat.shape[0])).reshape(rows, 128)


def _unpack(buf, shapes):
    flat = buf.reshape(-1)
    out, pos = [], 0
    for shp in shapes:
        n = math.prod(shp)
        out.append(flat[pos:pos + n].reshape(shp))
        pos += n
    return out


def kernel(x, mem, g_ffn1, w_ffn1_in, w_ffn1_out, g_mix, w_mix_in, conv_w, conv_b, g_gm_v, w_spatial, b_spatial, w_mix_out, g_xattn, g_mem, w_xq, w_xk, w_xv, w_xo, g_ffn2, w_ffn2_in, w_ffn2_out, g_final, loss_target, m_g_ffn1, m_w_ffn1_in, m_w_ffn1_out, m_g_mix, m_w_mix_in, m_conv_w, m_conv_b, m_g_gm_v, m_w_spatial, m_b_spatial, m_w_mix_out, m_g_xattn, m_g_mem, m_w_xq, m_w_xk, m_w_xv, m_w_xo, m_g_ffn2, m_w_ffn2_in, m_w_ffn2_out, m_g_final, v_g_ffn1, v_w_ffn1_in, v_w_ffn1_out, v_g_mix, v_w_mix_in, v_conv_w, v_conv_b, v_g_gm_v, v_w_spatial, v_b_spatial, v_w_mix_out, v_g_xattn, v_g_mem, v_w_xq, v_w_xk, v_w_xv, v_w_xo, v_g_ffn2, v_w_ffn2_in, v_w_ffn2_out, v_g_final):
    given = dict(locals())
    wts = {n: given[n] for n in WEIGHTS}
    mom = {n: given["m_" + n] for n in WEIGHTS}
    var = {n: given["v_" + n] for n in WEIGHTS}

    xi, yi, ci = lax.axis_index("x"), lax.axis_index("y"), lax.axis_index("c")
    blk = 2 * xi + yi
    place = jnp.stack([blk, ci]).astype(jnp.int32)

    x2, mem2, tgt = x[0], mem[0], loss_target[0]
    own = {n: cast_place("cast_" + n, wts[n][0], BIG[n], place) for n in BIG}
    shape = {n: own[n].shape for n in BIG}
    w_s, b_t = w_spatial[0], b_spatial[0].T
    gf = g_final[None]

    def gather(*names):
        return gather_job([(own[n], BIG[n], WHOLE) for n in names])

    full = {}

    (full["w_ffn1_in"],), (conv_taps,) = comm_only(
        "gather_first", [gather("w_ffn1_in"), columns_job(jnp.pad(conv_w[0], ((0, 8 - CONV_K), (0, 0))))])
    n1, r1 = rmsnorm_fwd("norm1", x2, g_ffn1)
    (gu1, a1), (got,) = swiglu_fwd("ffn1_in", n1, full["w_ffn1_in"], jobs=[gather("w_ffn1_out", "w_mix_in")])
    full["w_ffn1_out"], full["w_mix_in"] = got
    h1, (got,) = mm_nn_resid("ffn1_out", a1, full["w_ffn1_out"], x2, 0.5, jobs=[gather("w_mix_out", "w_xq", "w_xk")])
    full["w_mix_out"], full["w_xq"], full["w_xk"] = got
    n2, r2 = rmsnorm_fwd("norm2", h1, g_mix)
    z, (got,) = mm_nn("mix_in", n2, full["w_mix_in"], BF16, jobs=[gather("w_xv", "w_xo")])
    full["w_xv"], full["w_xo"] = got

    pieces = 4
    def ffn2_piece(p, prev):
        return [gather_job([(prev, 1, (p, 1, pieces))])]

    ycat, ((w2in,),) = mixer_fwd("mixer", z, conv_taps, conv_b, g_gm_v, w_s, b_t,
                                 jobs=ffn2_piece(0, own["w_ffn2_in"]))
    h2, ((w2in,),) = mm_nn_resid("mix_out", ycat, full["w_mix_out"], h1, 1.0, tk=2048, jobs=ffn2_piece(1, w2in))
    n3, r3 = rmsnorm_fwd("norm3", h2, g_xattn)
    mn, rm = rmsnorm_fwd("norm_mem", mem2, g_mem)
    q, ((w2in,),) = mm_nn("xq", n3, full["w_xq"], BF16, jobs=ffn2_piece(2, w2in))
    k = mm_nn("xk", mn, full["w_xk"], BF16)
    v = mm_nn("xv", mn, full["w_xv"], BF16)
    o = attn_fwd("attn", q, k, v)
    h3, ((w2in,),) = mm_nn_resid("xo", o, full["w_xo"], h2, 1.0, tk=2048, jobs=ffn2_piece(3, w2in))
    full["w_ffn2_in"] = w2in
    n4, r4 = rmsnorm_fwd("norm4", h3, g_ffn2)
    (gu2, a2), ((full["w_ffn2_out"],),) = swiglu_fwd("ffn2_in", n4, full["w_ffn2_in"], jobs=[gather("w_ffn2_out")])
    h4 = mm_nn_resid("ffn2_out", a2, full["w_ffn2_out"], h3, 0.5)
    loss_blk, dh4, dh4b, dg_final = loss_head("loss_head", h4, gf, tgt)

    dw, peer, part, land, half, grads = {}, {}, {}, {}, {}, {}

    def send_pair(*names):
        return pair_job([dw[n] for n in names], [BIG[n] for n in names])

    def take_pair(names, got):
        for n, p in zip(names, got):
            part[n] = pair_add("pair_add_" + n, dw[n], p, BIG[n], place)

    def send_cross(*names, sub=WHOLE):
        return cross_job([(part[n], BIG[n], shape[n], land.get(n), sub) for n in names])

    def take_cross(names, got, last=True):
        for n, l in zip(names, got):
            land[n] = l
            if last:
                half[n] = cross_sum("cross_sum_" + n, part[n], l, BIG[n], shape[n], place)

    def send_final(*names):
        return final_job([half[n] for n in names], [BIG[n] for n in names], [shape[n] for n in names])

    delta, new_m, new_v = {}, {}, {}

    def take_final(names, got):
        for n, g in zip(names, got):
            grads[n] = g
            delta[n], new_m[n], new_v[n] = adamw("adamw_" + n, wts[n][0], g, mom[n][0], var[n][0])

    dw["w_ffn2_out"] = mm_tn("ffn2_dwout", a2, dh4b, BF16, scale=0.5)
    dgu2, (got,) = swiglu_bwd("ffn2_dact", dh4b, full["w_ffn2_out"], gu2, 0.5, jobs=[send_pair("w_ffn2_out")])
    take_pair(["w_ffn2_out"], got)
    dw["w_ffn2_in"], (got,) = mm_tn_pair("ffn2_dwin", n4, dgu2, BF16, jobs=[send_cross("w_ffn2_out")])
    take_cross(["w_ffn2_out"], got)
    dn4, (got_f, got_p) = mm_nt_pair("ffn2_dn", dgu2, full["w_ffn2_in"], F32,
                                     jobs=[send_final("w_ffn2_out"), send_pair("w_ffn2_in")])
    take_final(["w_ffn2_out"], got_f)
    take_pair(["w_ffn2_in"], got_p)
    dh3, dh3b, dg_ffn2 = rmsnorm_bwd("norm4_bwd", dn4, h3, r4, g_ffn2, dh4)

    dw["w_xo"] = mm_tn("xo_dw", o, dh3b, BF16)
    do = mm_nt("xo_dx", dh3b, full["w_xo"], BF16)
    dq, dk, dv = attn_bwd("attn_bwd", q, k, v, do)
    dkb, dvb = dk.astype(BF16), dv.astype(BF16)
    dw["w_xq"] = mm_tn("xq_dw", n3, dq, BF16)
    dn3 = mm_nt("xq_dx", dq, full["w_xq"], F32)
    dh2, dh2b, dg_xattn = rmsnorm_bwd("norm3_bwd", dn3, h2, r3, g_xattn, dh3)
    dw["w_xk"] = mm_tn("xk_dw", mn, dkb, BF16)
    dw["w_xv"] = mm_tn("xv_dw", mn, dvb, BF16)
    dmn_k = mm_nt("xk_dx", dkb, full["w_xk"], F32)
    dmn_v = mm_nt("xv_dx", dvb, full["w_xv"], F32)
    dg_mem = gain_grad("norm_mem_bwd", dmn_k, dmn_v, mem2, rm)

    dw["w_mix_out"] = mm_tn("mix_out_dw", ycat, dh2b, BF16)
    dycat = mm_nt("mix_out_dx", dh2b, full["w_mix_out"], BF16)
    attn_names = ["w_xo", "w_xq", "w_xk", "w_xv", "w_mix_out"]
    (dz, dsmall, dws, dbt), (got_c, got_p) = mixer_bwd(
        "mixer_bwd", z, dycat, conv_taps, conv_b, g_gm_v, w_s, b_t,
        jobs=[send_cross("w_ffn2_in", sub=(0, 2, 8)), send_pair(*attn_names)])
    take_cross(["w_ffn2_in"], got_c, last=False)
    take_pair(attn_names, got_p)
    dw["w_mix_in"], (got_c,) = mm_tn("mix_in_dw", n2, dz, BF16, jobs=[send_cross("w_ffn2_in", sub=(2, 3, 8))])
    take_cross(["w_ffn2_in"], got_c, last=False)
    dn2, (got_c, got_p) = mm_nt("mix_in_dx", dz, full["w_mix_in"], F32, tk=1280,
                                jobs=[send_cross("w_ffn2_in", sub=(5, 3, 8)), send_pair("w_mix_in")])
    take_cross(["w_ffn2_in"], got_c)
    take_pair(["w_mix_in"], got_p)
    dh1, dh1b, dg_mix = rmsnorm_bwd("norm2_bwd", dn2, h1, r2, g_mix, dh2)

    xa_names = ["w_xo", "w_xq", "w_xk", "w_xv"]
    dgu1, (got_c, got_f) = swiglu_bwd("ffn1_dact", dh1b, full["w_ffn1_out"], gu1, 0.5,
                                      jobs=[send_cross(*xa_names), send_final("w_ffn2_in")])
    take_cross(xa_names, got_c)
    take_final(["w_ffn2_in"], got_f)
    mix_names = ["w_mix_out", "w_mix_in"]
    dw["w_ffn1_in"], (got_c, got_f) = mm_tn_pair("ffn1_dwin", n1, dgu1, BF16,
                                                 jobs=[send_cross(*mix_names), send_final(*xa_names)])
    take_cross(mix_names, got_c)
    take_final(xa_names, got_f)
    dw["w_ffn1_out"], (got_p, got_f) = mm_tn("ffn1_dwout", a1, dh1b, BF16, scale=0.5,
                                             jobs=[send_pair("w_ffn1_in"), send_final(*mix_names)])
    take_pair(["w_ffn1_in"], got_p)
    take_final(mix_names, got_f)

    early = {"g_mix": dg_mix, "conv_w": dsmall[0:CONV_K], "conv_b": dsmall[3:4], "g_gm_v": dsmall[4:5],
             "w_spatial": dws, "b_spatial": dbt.T, "g_xattn": dg_xattn, "g_mem": dg_mem, "g_ffn2": dg_ffn2,
             "g_final": dg_final}
    dn1, (got_c, got_p, (early_all,)) = mm_nt_pair(
        "ffn1_dn", dgu1, full["w_ffn1_in"], F32,
        jobs=[send_cross("w_ffn1_in"), send_pair("w_ffn1_out"), stack_job(_pack([early[n] for n in EARLY_SMALL]))])
    take_cross(["w_ffn1_in"], got_c)
    take_pair(["w_ffn1_out"], got_p)
    (dx, _, dg_ffn1), (got_c, got_f) = rmsnorm_bwd("norm1_bwd", dn1, x2, r1, g_ffn1, dh1,
                                                   jobs=[send_cross("w_ffn1_out"), send_final("w_ffn1_in")])
    take_cross(["w_ffn1_out"], got_c)
    take_final(["w_ffn1_in"], got_f)
    got_f, (late_all,) = comm_only("tail_exchange", [send_final("w_ffn1_out"), stack_job(_pack([dg_ffn1]))])
    take_final(["w_ffn1_out"], got_f)

    early_sum, late_sum = small_sum("small_sum", [early_all, late_all])
    for n, g in zip(EARLY_SMALL, _unpack(early_sum, [early[n].shape for n in EARLY_SMALL])):
        grads[n] = g
    grads["g_ffn1"] = _unpack(late_sum, [dg_ffn1.shape])[0]
    taps_cols = conv_w.shape[2]
    grads["conv_w"] = lax.dynamic_slice_in_dim(grads["conv_w"], blk * taps_cols, taps_cols, axis=1)
    packed = [_pack([src[n] for n in SMALL]) for src in (wts, grads, mom, var)]
    own_shapes = [wts[n].shape for n in SMALL]
    for dst, buf in zip((delta, new_m, new_v), adamw("adamw_small", *packed)):
        for n, a in zip(SMALL, _unpack(buf, own_shapes)):
            dst[n] = a

    loss = lax.psum(loss_blk[0, 0], ("x", "y", "c"))
    outs = [loss, dx[None]]
    for group in (grads, delta, new_m, new_v):
        outs += [group[n].reshape(wts[n].shape) for n in WEIGHTS]
    return tuple(outs)
```

```python
import math

import jax
import jax.numpy as jnp
from jax import lax
from jax.experimental import pallas as pl
from jax.experimental.pallas import tpu as pltpu

F32 = jnp.float32
BF16 = jnp.bfloat16
EPS = 1e-6
GROUP = 128
XA_HEADS = 4
CONV_K = 3
N_CHIPS = 4
VMEM_LIMIT_BYTES = 56 * 1024 * 1024

ADAM_LR = 0.001
ADAM_B1 = 0.9
ADAM_B2 = 0.999
ADAM_EPS = 1e-08
ADAM_WD = 0.01
ADAM_STEP = 10

MESH = pl.DeviceIdType.MESH
ANY = pl.BlockSpec(memory_space=pl.ANY)


def _tile(dim, pref, mult=128):
    if dim <= pref:
        return dim
    t = (pref // mult) * mult
    while t >= mult:
        if dim % t == 0:
            return t
        t -= mult
    raise ValueError(f"no tile for {dim} under {pref}")


def _params(sem):
    return pltpu.CompilerParams(dimension_semantics=sem, vmem_limit_bytes=VMEM_LIMIT_BYTES)


def _sds(shape, dtype):
    return jax.ShapeDtypeStruct(shape, dtype)


def _dot_nn(a, b):
    return jnp.dot(a, b, preferred_element_type=F32)


def _dot_nt(a, b):
    return lax.dot_general(a, b, (((1,), (1,)), ((), ())), preferred_element_type=F32)


def _dot_tn(a, b):
    return lax.dot_general(a, b, (((0,), (0,)), ((), ())), preferred_element_type=F32)


class Job:
    def __init__(self, inputs, out_shapes, aliases, sems, start, middle, finish):
        self.inputs, self.out_shapes, self.aliases, self.sems = inputs, out_shapes, aliases, sems
        self.start, self.middle, self.finish = start, middle, finish


def _place():
    x, y, c = lax.axis_index("x"), lax.axis_index("y"), lax.axis_index("c")
    chips = [(1 - x, y), (x, 1 - y), (1 - x, 1 - y)]
    return x, y, c, chips


def _ds(start, size, lane):
    if not isinstance(start, int):
        start = pl.multiple_of(start, 128 if lane else 16)
    return pl.ds(start, size)


WHOLE = (0, 1, 1)


def _window(ref, axis, shape, blk=None, half=None, sub=WHOLE):
    n = shape[axis] // N_CHIPS
    hs = shape[1 - axis] // 2
    idx = [slice(None), slice(None)]
    if blk is not None:
        idx[axis] = _ds(blk * n, n, axis == 1)
    first, count, pieces = sub
    ext = hs // pieces
    if half is not None:
        idx[1 - axis] = _ds(half * hs + first * ext, count * ext, axis == 0)
    elif pieces > 1:
        idx[1 - axis] = _ds(first * ext, count * ext, axis == 0)
    return ref.at[tuple(idx)]


def _remote(src, dst, send_sem, recv_sem, dev):
    return pltpu.make_async_remote_copy(src_ref=src, dst_ref=dst, send_sem=send_sem, recv_sem=recv_sem,
                                        device_id=dev, device_id_type=MESH)


def _full_shape(block_shape, axis):
    out = list(block_shape)
    out[axis] *= N_CHIPS
    return tuple(out)


def _half_all(shape, axis):
    out = list(shape)
    out[1 - axis] //= 2
    return tuple(out)


def _block(shape, axis):
    out = list(shape)
    out[axis] //= N_CHIPS
    return tuple(out)


def _half_block(shape, axis):
    return _half_all(_block(shape, axis), axis)


def gather_job(items):
    nw = len(items)
    shapes = [full.shape for full, _, _ in items]

    def start(pos, ins, outs, sems):
        x, y, c, chips = pos
        for w, (_, ax, sub) in enumerate(items):
            mine = _window(outs[w], ax, shapes[w], blk=2 * x + y, half=c, sub=sub)
            for j, (cx, cy) in enumerate(chips):
                _remote(mine, mine, sems[0].at[6 * w + j], sems[1].at[6 * w + j], (cx, cy, c)).start()

    def middle(pos, ins, outs, sems):
        x, y, c, chips = pos
        for w, (_, ax, sub) in enumerate(items):
            for j, (cx, cy) in enumerate(chips):
                landed = _window(outs[w], ax, shapes[w], blk=2 * cx + cy, half=c, sub=sub)
                _remote(landed, landed, sems[0].at[6 * w + j], sems[1].at[6 * w + j], (cx, cy, c)).wait_recv()
                _remote(landed, landed, sems[0].at[6 * w + 3 + j], sems[1].at[6 * w + 3 + j], (x, y, 1 - c)).start()

    def finish(pos, ins, outs, sems):
        x, y, c, chips = pos
        for w, (_, ax, sub) in enumerate(items):
            for j, (cx, cy) in enumerate(chips):
                passed = _window(outs[w], ax, shapes[w], blk=2 * cx + cy, half=1 - c, sub=sub)
                _remote(passed, passed, sems[0].at[6 * w + 3 + j], sems[1].at[6 * w + 3 + j],
                        (x, y, 1 - c)).wait_recv()
            mine = _window(outs[w], ax, shapes[w], blk=2 * x + y, half=c, sub=sub)
            for j in range(6):
                _remote(mine, mine, sems[0].at[6 * w + j], sems[1].at[6 * w + j], (x, y, 1 - c)).wait_send()

    sems = [pltpu.SemaphoreType.DMA((6 * nw,)), pltpu.SemaphoreType.DMA((6 * nw,))]
    return Job([full for full, _, _ in items], [_sds(full.shape, full.dtype) for full, _, _ in items],
               {w: w for w in range(nw)}, sems, start, middle, finish)


def pair_job(grads, axes):
    nw = len(grads)
    shapes = [g.shape for g in grads]

    def start(pos, ins, outs, sems):
        x, y, c, _ = pos
        for w in range(nw):
            _remote(_window(ins[w], axes[w], shapes[w], half=1 - c), outs[w], sems[0].at[w], sems[1].at[w],
                    (x, y, 1 - c)).start()

    def finish(pos, ins, outs, sems):
        x, y, c, _ = pos
        for w in range(nw):
            cp = _remote(outs[w], outs[w], sems[0].at[w], sems[1].at[w], (x, y, 1 - c))
            cp.wait_recv()
            cp.wait_send()

    sems = [pltpu.SemaphoreType.DMA((nw,)), pltpu.SemaphoreType.DMA((nw,))]
    return Job(list(grads), [_sds(_half_all(s, a), BF16) for s, a in zip(shapes, axes)], {}, sems, start, None,
               finish)


def cross_job(items):
    nw = len(items)
    inputs, aliases = [], {}
    for w, (part, ax, shape, prev, sub) in enumerate(items):
        inputs.append(part)
        if prev is not None:
            aliases[len(inputs)] = w
            inputs.append(prev)

    def copies(pos, ins, outs, sems):
        x, y, c, chips = pos
        k = 0
        for w, (_, ax, shape, prev, sub) in enumerate(items):
            src = ins[k]
            k += 2 if prev is not None else 1
            for j, (cx, cy) in enumerate(chips):
                slot = _window(outs[w].at[j], ax, shape, sub=sub)
                yield (_remote(_window(src, ax, shape, blk=2 * cx + cy, sub=sub), slot,
                               sems[0].at[3 * w + j], sems[1].at[3 * w + j], (cx, cy, c)),
                       _remote(slot, slot, sems[0].at[3 * w + j], sems[1].at[3 * w + j], (cx, cy, c)))

    def start(pos, ins, outs, sems):
        for send, _ in copies(pos, ins, outs, sems):
            send.start()

    def finish(pos, ins, outs, sems):
        for send, recv in copies(pos, ins, outs, sems):
            recv.wait_recv()
            send.wait_send()

    sems = [pltpu.SemaphoreType.DMA((3 * nw,)), pltpu.SemaphoreType.DMA((3 * nw,))]
    out_shapes = [_sds((3,) + _half_block(shape, ax), BF16) for _, ax, shape, _, _ in items]
    return Job(inputs, out_shapes, aliases, sems, start, None, finish)


def final_job(blocks, axes, shapes):
    nw = len(blocks)

    def start(pos, ins, outs, sems):
        x, y, c, _ = pos
        for w in range(nw):
            mine = _window(outs[w], axes[w], shapes[w], half=c)
            _remote(mine, mine, sems[0].at[w], sems[1].at[w], (x, y, 1 - c)).start()

    def finish(pos, ins, outs, sems):
        x, y, c, _ = pos
        for w in range(nw):
            theirs = _window(outs[w], axes[w], shapes[w], half=1 - c)
            cp = _remote(theirs, theirs, sems[0].at[w], sems[1].at[w], (x, y, 1 - c))
            cp.wait_recv()
            cp.wait_send()

    sems = [pltpu.SemaphoreType.DMA((nw,)), pltpu.SemaphoreType.DMA((nw,))]
    return Job(list(blocks), [_sds(b.shape, b.dtype) for b in blocks], {w: w for w in range(nw)}, sems, start, None,
               finish)


def stack_job(small):
    def peers(pos):
        x, y, c, _ = pos
        for k in range(1, 8):
            yield k - 1, (1 - x if k & 4 else x, 1 - y if k & 2 else y, 1 - c if k & 1 else c)

    def start(pos, ins, outs, sems):
        x, y, c, _ = pos
        mine = outs[0].at[4 * x + 2 * y + c]
        pltpu.make_async_copy(ins[0], mine, sems[2]).start()
        for k, dev in peers(pos):
            _remote(ins[0], mine, sems[0].at[k], sems[1].at[k], dev).start()

    def finish(pos, ins, outs, sems):
        x, y, c, _ = pos
        for k, (px, py, pc) in peers(pos):
            slot = outs[0].at[4 * px + 2 * py + pc]
            cp = _remote(slot, slot, sems[0].at[k], sems[1].at[k], (px, py, pc))
            cp.wait_recv()
            cp.wait_send()
        pltpu.make_async_copy(ins[0], outs[0].at[4 * x + 2 * y + c], sems[2]).wait()

    sems = [pltpu.SemaphoreType.DMA((7,)), pltpu.SemaphoreType.DMA((7,)), pltpu.SemaphoreType.DMA]
    return Job([small], [_sds((8,) + small.shape, small.dtype)], {}, sems, start, None, finish)


def columns_job(block):
    cols = block.shape[1]
    place = lambda out, b: out.at[:, _ds(b * cols, cols, True)]

    def start(pos, ins, outs, sems):
        x, y, c, chips = pos
        pltpu.make_async_copy(ins[0], place(outs[0], 2 * x + y), sems[2]).start()
        for j, (cx, cy) in enumerate(chips):
            _remote(ins[0], place(outs[0], 2 * x + y), sems[0].at[j], sems[1].at[j], (cx, cy, c)).start()

    def finish(pos, ins, outs, sems):
        x, y, c, chips = pos
        for j, (cx, cy) in enumerate(chips):
            got = place(outs[0], 2 * cx + cy)
            cp = _remote(got, got, sems[0].at[j], sems[1].at[j], (cx, cy, c))
            cp.wait_recv()
            cp.wait_send()
        pltpu.make_async_copy(ins[0], place(outs[0], 2 * x + y), sems[2]).wait()

    sems = [pltpu.SemaphoreType.DMA((3,)), pltpu.SemaphoreType.DMA((3,)), pltpu.SemaphoreType.DMA]
    return Job([block], [_sds((block.shape[0], N_CHIPS * cols), block.dtype)], {}, sems, start, None, finish)


def _call(name, body, grid, in_specs, out_specs, out_shape, args, scratch=(), sem=None, jobs=()):
    n_in, n_out, n_sc = len(args), len(out_shape), len(scratch)
    if not jobs:
        outs = pl.pallas_call(
            body, name=name, grid=grid, in_specs=in_specs, out_specs=out_specs, out_shape=out_shape,
            scratch_shapes=list(scratch), compiler_params=_params(sem))(*args)
        return list(outs), []

    total = math.prod(grid) if grid else 1
    mid = min(total - 1, (3 * total) // 4)

    def split(refs, start, counts):
        out = []
        for n in counts:
            out.append(refs[start:start + n])
            start += n
        return out, start

    def wrapped(*refs):
        c_in = refs[:n_in]
        j_ins, p = split(refs, n_in, [len(j.inputs) for j in jobs])
        c_out = refs[p:p + n_out]
        j_outs, p = split(refs, p + n_out, [len(j.out_shapes) for j in jobs])
        c_sc = refs[p:p + n_sc]
        j_sems, p = split(refs, p + n_sc, [len(j.sems) for j in jobs])
        pos = _place()
        step = 0
        for axis, extent in enumerate(grid):
            step = step * extent + pl.program_id(axis)

        def run(phase):
            for j, ins, outs, sems in zip(jobs, j_ins, j_outs, j_sems):
                fn = getattr(j, phase)
                if fn is not None:
                    fn(pos, ins, outs, sems)

        if total == 1:
            run("start")
            body(*c_in, *c_out, *c_sc)
            run("middle")
            run("finish")
            return
        pl.when(step == 0)(lambda: run("start"))
        body(*c_in, *c_out, *c_sc)
        if any(j.middle is not None for j in jobs):
            pl.when(step == mid)(lambda: run("middle"))
        pl.when(step == total - 1)(lambda: run("finish"))

    aliases, in_at, out_at = {}, n_in, n_out
    for j in jobs:
        for src, dst in j.aliases.items():
            aliases[in_at + src] = out_at + dst
        in_at += len(j.inputs)
        out_at += len(j.out_shapes)
    outs = pl.pallas_call(
        wrapped, name=name, grid=grid,
        in_specs=list(in_specs) + [ANY] * (in_at - n_in),
        out_specs=list(out_specs) + [ANY] * (out_at - n_out),
        out_shape=list(out_shape) + [s for j in jobs for s in j.out_shapes],
        scratch_shapes=list(scratch) + [s for j in jobs for s in j.sems],
        input_output_aliases=aliases,
        compiler_params=_params(("arbitrary",) * len(grid)),
    )(*args, *[a for j in jobs for a in j.inputs])
    job_outs, p = split(outs, n_out, [len(j.out_shapes) for j in jobs])
    return list(outs[:n_out]), [list(o) for o in job_outs]


def comm_only(name, jobs):
    def body(dummy_ref, out_ref):
        out_ref[...] = dummy_ref[...]

    dummy = jnp.zeros((8, 128), F32)
    spec = pl.BlockSpec((8, 128), lambda: (0, 0))
    return _call(name, body, (), [spec], [spec], [_sds((8, 128), F32)], [dummy], jobs=jobs)[1]


def _ret(outs, job_outs, jobs, single=True):
    res = outs[0] if single else outs
    return (res, job_outs) if jobs else res


def rmsnorm_fwd(name, x, g):
    s, d = x.shape
    tm = _tile(s, 512, 8)

    def body(x_ref, g_ref, n_ref, r_ref):
        xv = x_ref[...]
        r = lax.rsqrt(jnp.mean(xv * xv, axis=-1, keepdims=True) + EPS)
        n_ref[...] = (xv * r * g_ref[...]).astype(BF16)
        r_ref[...] = r

    row = lambda i: (i, 0)
    return _call(
        name, body, (s // tm,),
        [pl.BlockSpec((tm, d), row), pl.BlockSpec((1, d), lambda i: (0, 0))],
        [pl.BlockSpec((tm, d), row), pl.BlockSpec((tm, 1), row)],
        [_sds((s, d), BF16), _sds((s, 1), F32)], [x, g], sem=("arbitrary",))[0]


def rmsnorm_bwd(name, dn, x, r, g, dh_in, jobs=()):
    s, d = x.shape
    tm = _tile(s, 512, 8)

    def body(dn_ref, x_ref, r_ref, g_ref, dh_ref, out_ref, outb_ref, dg_ref):
        i = pl.program_id(0)
        xh = x_ref[...] * r_ref[...]
        dnv = dn_ref[...]
        dxh = dnv * g_ref[...]
        dx = r_ref[...] * (dxh - xh * jnp.mean(dxh * xh, axis=-1, keepdims=True))
        out = dh_ref[...] + dx
        out_ref[...] = out
        outb_ref[...] = out.astype(BF16)
        part = jnp.sum(dnv * xh, axis=0, keepdims=True)

        @pl.when(i == 0)
        def _():
            dg_ref[...] = part

        @pl.when(i > 0)
        def _():
            dg_ref[...] += part

    row = lambda i: (i, 0)
    fixed = lambda i: (0, 0)
    outs, job_outs = _call(
        name, body, (s // tm,),
        [pl.BlockSpec((tm, d), row), pl.BlockSpec((tm, d), row), pl.BlockSpec((tm, 1), row),
         pl.BlockSpec((1, d), fixed), pl.BlockSpec((tm, d), row)],
        [pl.BlockSpec((tm, d), row), pl.BlockSpec((tm, d), row), pl.BlockSpec((1, d), fixed)],
        [_sds((s, d), F32), _sds((s, d), BF16), _sds((1, d), F32)], [dn, x, r, g, dh_in],
        sem=("arbitrary",), jobs=jobs)
    return _ret(outs, job_outs, jobs, single=False)


def gain_grad(name, dn_a, dn_b, x, r):
    s, d = x.shape
    tm = _tile(s, 512, 8)

    def body(a_ref, b_ref, x_ref, r_ref, dg_ref):
        i = pl.program_id(0)
        part = jnp.sum((a_ref[...] + b_ref[...]) * (x_ref[...] * r_ref[...]), axis=0, keepdims=True)

        @pl.when(i == 0)
        def _():
            dg_ref[...] = part

        @pl.when(i > 0)
        def _():
            dg_ref[...] += part

    row = lambda i: (i, 0)
    return _call(
        name, body, (s // tm,),
        [pl.BlockSpec((tm, d), row), pl.BlockSpec((tm, d), row), pl.BlockSpec((tm, d), row),
         pl.BlockSpec((tm, 1), row)],
        [pl.BlockSpec((1, d), lambda i: (0, 0))], [_sds((1, d), F32)], [dn_a, dn_b, x, r],
        sem=("arbitrary",))[0][0]


def loss_head(name, h, g, target):
    s, d = h.shape
    tm = _tile(s, 512, 8)
    nsteps = s // tm

    def body(h_ref, g_ref, t_ref, loss_ref, dh_ref, dhb_ref, dg_ref, sq_ref):
        i = pl.program_id(0)
        hv = h_ref[...]
        gv = g_ref[...]
        r = lax.rsqrt(jnp.mean(hv * hv, axis=-1, keepdims=True) + EPS)
        xh = hv * r
        err = xh * gv - t_ref[...]
        dy = err * (1.0 / d)
        dxh = dy * gv
        dh = r * (dxh - xh * jnp.mean(dxh * xh, axis=-1, keepdims=True))
        dh_ref[...] = dh
        dhb_ref[...] = dh.astype(BF16)
        dg_part = jnp.sum(dy * xh, axis=0, keepdims=True)
        sq_part = jnp.sum(err * err, axis=0, keepdims=True)

        @pl.when(i == 0)
        def _():
            dg_ref[...] = dg_part
            sq_ref[...] = sq_part

        @pl.when(i > 0)
        def _():
            dg_ref[...] += dg_part
            sq_ref[...] += sq_part

        @pl.when(i == nsteps - 1)
        def _():
            total = jnp.sum(sq_ref[...], axis=-1, keepdims=True) * (0.5 / d)
            loss_ref[...] = jnp.broadcast_to(total, loss_ref.shape)

    row = lambda i: (i, 0)
    fixed = lambda i: (0, 0)
    return _call(
        name, body, (nsteps,),
        [pl.BlockSpec((tm, d), row), pl.BlockSpec((1, d), fixed), pl.BlockSpec((tm, d), row)],
        [pl.BlockSpec((8, 128), fixed), pl.BlockSpec((tm, d), row), pl.BlockSpec((tm, d), row),
         pl.BlockSpec((1, d), fixed)],
        [_sds((8, 128), F32), _sds((s, d), F32), _sds((s, d), BF16), _sds((1, d), F32)], [h, g, target],
        scratch=[pltpu.VMEM((1, d), F32)], sem=("arbitrary",))[0]


def _mm(name, grid, in_arrays, in_specs, out_shapes, out_specs, acc_tile, dot, epilogue, jobs=()):
    nk = grid[2]
    n_in = len(in_arrays)
    n_out = len(out_shapes)

    def body(*refs):
        ins, outs = refs[:n_in], refs[n_in:n_in + n_out]
        if nk == 1:
            epilogue(dot(*ins), ins, outs)
            return
        acc = refs[n_in + n_out]
        k = pl.program_id(2)

        @pl.when(k == 0)
        def _():
            acc[...] = dot(*ins)

        @pl.when(jnp.logical_and(k > 0, k < nk - 1))
        def _():
            acc[...] += dot(*ins)

        @pl.when(k == nk - 1)
        def _():
            epilogue(acc[...] + dot(*ins), ins, outs)

    scratch = [pltpu.VMEM(acc_tile, F32)] if nk > 1 else []
    outs, job_outs = _call(name, body, grid, in_specs, out_specs, out_shapes, in_arrays, scratch=scratch,
                           sem=("parallel", "parallel", "arbitrary"), jobs=jobs)
    return _ret(outs, job_outs, jobs)


def _store(scale, dtype):
    def epilogue(acc, ins, outs):
        outs[0][...] = (acc * scale if scale != 1.0 else acc).astype(dtype)
    return epilogue


def mm_nn(name, a, w, out_dtype, tm=1024, tn=1024, tk=2048, jobs=()):
    m, kd = a.shape
    n = w.shape[1]
    tm, tn, tk = _tile(m, tm, 8), _tile(n, tn), _tile(kd, tk)
    return _mm(
        name, (n // tn, m // tm, kd // tk), [a, w],
        [pl.BlockSpec((tm, tk), lambda j, i, k: (i, k)), pl.BlockSpec((tk, tn), lambda j, i, k: (k, j))],
        [_sds((m, n), out_dtype)], [pl.BlockSpec((tm, tn), lambda j, i, k: (i, j))], (tm, tn),
        lambda a_ref, w_ref: _dot_nn(a_ref[...], w_ref[...]), _store(1.0, out_dtype), jobs)


def mm_nn_resid(name, a, w, x, scale, tm=1024, tn=1024, tk=1408, jobs=()):
    m, kd = a.shape
    n = w.shape[1]
    tm, tn, tk = _tile(m, tm, 8), _tile(n, tn), _tile(kd, tk)

    def epilogue(acc, ins, outs):
        outs[0][...] = ins[2][...] + scale * acc

    return _mm(
        name, (n // tn, m // tm, kd // tk), [a, w, x],
        [pl.BlockSpec((tm, tk), lambda j, i, k: (i, k)), pl.BlockSpec((tk, tn), lambda j, i, k: (k, j)),
         pl.BlockSpec((tm, tn), lambda j, i, k: (i, j))],
        [_sds((m, n), F32)], [pl.BlockSpec((tm, tn), lambda j, i, k: (i, j))], (tm, tn),
        lambda a_ref, w_ref, x_ref: _dot_nn(a_ref[...], w_ref[...]), epilogue, jobs)


def mm_nt(name, a, w, out_dtype, scale=1.0, tm=1024, tn=1024, tk=2048, jobs=()):
    m, kd = a.shape
    n = w.shape[0]
    tm, tn, tk = _tile(m, tm, 8), _tile(n, tn), _tile(kd, tk)
    return _mm(
        name, (n // tn, m // tm, kd // tk), [a, w],
        [pl.BlockSpec((tm, tk), lambda j, i, k: (i, k)), pl.BlockSpec((tn, tk), lambda j, i, k: (j, k))],
        [_sds((m, n), out_dtype)], [pl.BlockSpec((tm, tn), lambda j, i, k: (i, j))], (tm, tn),
        lambda a_ref, w_ref: _dot_nt(a_ref[...], w_ref[...]), _store(scale, out_dtype), jobs)


def mm_nt_pair(name, a3, w, out_dtype, tm=1024, tn=1024, tk=2816, jobs=()):
    _, m, f = a3.shape
    n = w.shape[0]
    tm, tn, tk = _tile(m, tm, 8), _tile(n, tn), _tile(f, tk)
    nkf = f // tk
    return _mm(
        name, (n // tn, m // tm, 2 * nkf), [a3, w],
        [pl.BlockSpec((None, tm, tk), lambda j, i, k: (k // nkf, i, k % nkf)),
         pl.BlockSpec((tn, tk), lambda j, i, k: (j, k))],
        [_sds((m, n), out_dtype)], [pl.BlockSpec((tm, tn), lambda j, i, k: (i, j))], (tm, tn),
        lambda a_ref, w_ref: _dot_nt(a_ref[...], w_ref[...]), _store(1.0, out_dtype), jobs)


def mm_tn(name, a, b, out_dtype, scale=1.0, tm=1024, tn=1024, tk=4096, jobs=()):
    kd, m = a.shape
    n = b.shape[1]
    tm, tn, tk = _tile(m, tm), _tile(n, tn), _tile(kd, tk, 16)
    return _mm(
        name, (n // tn, m // tm, kd // tk), [a, b],
        [pl.BlockSpec((tk, tm), lambda j, i, k: (k, i)), pl.BlockSpec((tk, tn), lambda j, i, k: (k, j))],
        [_sds((m, n), out_dtype)], [pl.BlockSpec((tm, tn), lambda j, i, k: (i, j))], (tm, tn),
        lambda a_ref, b_ref: _dot_tn(a_ref[...], b_ref[...]), _store(scale, out_dtype), jobs)


def mm_tn_pair(name, a, b3, out_dtype, tm=1024, tn=512, tk=4096, jobs=()):
    kd, m = a.shape
    f = b3.shape[2]
    tm, tn, tk = _tile(m, tm), _tile(f, tn), _tile(kd, tk, 16)
    nf = f // tn
    return _mm(
        name, (m // tm, 2 * nf, kd // tk), [a, b3],
        [pl.BlockSpec((tk, tm), lambda i, j, k: (k, i)),
         pl.BlockSpec((None, tk, tn), lambda i, j, k: (j // nf, k, j % nf))],
        [_sds((m, 2 * f), out_dtype)], [pl.BlockSpec((tm, tn), lambda i, j, k: (i, j))], (tm, tn),
        lambda a_ref, b_ref: _dot_tn(a_ref[...], b_ref[...]), _store(1.0, out_dtype), jobs)


def swiglu_fwd(name, n, w_in, tm=1024, tn=512, jobs=()):
    s, d = n.shape
    f = w_in.shape[1] // 2
    tm, tn = _tile(s, tm, 8), _tile(f, tn)
    nf = f // tn

    def body(n_ref, wg_ref, wu_ref, gu_ref, a_ref):
        nv = n_ref[...]
        g = _dot_nn(nv, wg_ref[...])
        u = _dot_nn(nv, wu_ref[...])
        gu_ref[0] = g.astype(BF16)
        gu_ref[1] = u.astype(BF16)
        a_ref[...] = (g * jax.nn.sigmoid(g) * u).astype(BF16)

    outs, job_outs = _call(
        name, body, (nf, s // tm),
        [pl.BlockSpec((tm, d), lambda j, i: (i, 0)), pl.BlockSpec((d, tn), lambda j, i: (0, j)),
         pl.BlockSpec((d, tn), lambda j, i: (0, j + nf))],
        [pl.BlockSpec((2, tm, tn), lambda j, i: (0, i, j)), pl.BlockSpec((tm, tn), lambda j, i: (i, j))],
        [_sds((2, s, f), BF16), _sds((s, f), BF16)], [n, w_in, w_in], sem=("parallel", "parallel"), jobs=jobs)
    return _ret(outs, job_outs, jobs, single=False)


def swiglu_bwd(name, dh, w_out, gu, scale, tm=1024, tn=512, jobs=()):
    s, d = dh.shape
    f = w_out.shape[0]
    tm, tn = _tile(s, tm, 8), _tile(f, tn)

    sub = _tile(tm, 256, 8)

    def body(dh_ref, w_ref, gu_ref, out_ref):
        for lo in range(0, tm, sub):
            rows = slice(lo, lo + sub)
            da = _dot_nt(dh_ref[rows, :], w_ref[...]) * scale
            g = gu_ref[0, rows, :].astype(F32)
            u = gu_ref[1, rows, :].astype(F32)
            sg = jax.nn.sigmoid(g)
            out_ref[0, rows, :] = (da * u * (sg * (1.0 + g * (1.0 - sg)))).astype(BF16)
            out_ref[1, rows, :] = (da * (g * sg)).astype(BF16)

    outs, job_outs = _call(
        name, body, (f // tn, s // tm),
        [pl.BlockSpec((tm, d), lambda j, i: (i, 0)), pl.BlockSpec((tn, d), lambda j, i: (j, 0)),
         pl.BlockSpec((2, tm, tn), lambda j, i: (0, i, j))],
        [pl.BlockSpec((2, tm, tn), lambda j, i: (0, i, j))],
        [_sds((2, s, f), BF16)], [dh, w_out, gu], sem=("parallel", "parallel"), jobs=jobs)
    return _ret(outs, job_outs, jobs)


HALO = 16


def _conv_inputs(z_ref, hgc_ref, hhc_ref, i, cw, tm):
    gc = z_ref[:, cw:2 * cw].astype(F32)
    hc = z_ref[:, 2 * cw:3 * cw].astype(F32)
    cin = gc * hc
    halo = hgc_ref[...].astype(F32) * hhc_ref[...].astype(F32) * (i > 0).astype(F32)
    row = lax.broadcasted_iota(jnp.int32, (tm, cw), 0)
    x1 = jnp.where(row == 0, halo[HALO - 1:HALO], pltpu.roll(cin, 1, 0))
    x2 = jnp.where(row == 0, halo[HALO - 2:HALO - 1], jnp.where(row == 1, halo[HALO - 1:HALO], pltpu.roll(cin, 2, 0)))
    return gc, hc, cin, x1, x2


def _tril(w):
    r = lax.broadcasted_iota(jnp.int32, w.shape, 0)
    c = lax.broadcasted_iota(jnp.int32, w.shape, 1)
    return jnp.where(r >= c, w, jnp.zeros_like(w))


def mixer_fwd(name, z, conv_w, conv_b, g_v, w_s, b_t, tm=256, jobs=()):
    s, zc = z.shape
    cw = conv_w.shape[1]
    gw = g_v.shape[1]
    heads = gw // GROUP
    tm = _tile(s, tm)
    hb = tm // HALO

    def body(z_ref, hgc_ref, hhc_ref, cw_ref, cb_ref, gv_ref, ws_ref, bt_ref, y_ref):
        i = pl.program_id(0)
        _, _, cin, x1, x2 = _conv_inputs(z_ref, hgc_ref, hhc_ref, i, cw, tm)
        cv = cb_ref[...] + cw_ref[2:3, :] * cin + cw_ref[1:2, :] * x1 + cw_ref[0:1, :] * x2
        y_ref[:, 0:cw] = (z_ref[:, 0:cw].astype(F32) * cv).astype(BF16)
        for h in range(heads):
            lo = h * GROUP
            vh = z_ref[:, 3 * cw + gw + lo:3 * cw + gw + lo + GROUP].astype(F32)
            rv = lax.rsqrt(jnp.mean(vh * vh, axis=-1, keepdims=True) + EPS)
            vn = (vh * rv * gv_ref[:, lo:lo + GROUP]).astype(BF16)
            w = _tril(ws_ref[h]).astype(BF16)
            for n in range(tm // GROUP):
                rows = slice(n * GROUP, (n + 1) * GROUP)
                sg = _dot_nn(w, vn[rows]) + bt_ref[:, h:h + 1]
                u = z_ref[rows, 3 * cw + lo:3 * cw + lo + GROUP].astype(F32)
                y_ref[rows, cw + lo:cw + lo + GROUP] = (u * sg).astype(BF16)

    fixed2 = lambda i: (0, 0)
    outs, job_outs = _call(
        name, body, (s // tm,),
        [pl.BlockSpec((tm, zc), lambda i: (i, 0)),
         pl.BlockSpec((HALO, cw), lambda i: (jnp.maximum(i * hb - 1, 0), 1)),
         pl.BlockSpec((HALO, cw), lambda i: (jnp.maximum(i * hb - 1, 0), 2)),
         pl.BlockSpec(conv_w.shape, fixed2), pl.BlockSpec(conv_b.shape, fixed2),
         pl.BlockSpec(g_v.shape, fixed2), pl.BlockSpec(w_s.shape, lambda i: (0, 0, 0)),
         pl.BlockSpec(b_t.shape, fixed2)],
        [pl.BlockSpec((tm, cw + gw), lambda i: (i, 0))], [_sds((s, cw + gw), BF16)],
        [z, z, z, conv_w, conv_b, g_v, w_s, b_t], sem=("arbitrary",), jobs=jobs)
    return _ret(outs, job_outs, jobs)


def mixer_bwd(name, z, dy, conv_w, conv_b, g_v, w_s, b_t, tm=256, jobs=()):
    s, zc = z.shape
    cw = conv_w.shape[1]
    gw = g_v.shape[1]
    heads = gw // GROUP
    tm = _tile(s, tm)
    hb = tm // HALO
    nsteps = s // tm
    last_halo = s // HALO - 1

    def body(z_ref, hgc_ref, hhc_ref, ngb_ref, dy_ref, ndy_ref, cw_ref, cb_ref, gv_ref, ws_ref, bt_ref,
             dz_ref, sm_ref, dws_ref, dbt_ref, dsg_ref):
        i = pl.program_id(0)

        @pl.when(i == 0)
        def _():
            sm_ref[...] = jnp.zeros_like(sm_ref)
            dws_ref[...] = jnp.zeros_like(dws_ref)
            dsg_ref[...] = jnp.zeros_like(dsg_ref)

        gc, hc, cin, x1, x2 = _conv_inputs(z_ref, hgc_ref, hhc_ref, i, cw, tm)
        w0, w1, w2 = cw_ref[0:1, :], cw_ref[1:2, :], cw_ref[2:3, :]
        cv = cb_ref[...] + w2 * cin + w1 * x1 + w0 * x2
        gb = z_ref[:, 0:cw].astype(F32)
        dyc = dy_ref[:, 0:cw].astype(F32)
        dz_ref[:, 0:cw] = (dyc * cv).astype(BF16)
        dcv = dyc * gb
        nxt = ndy_ref[...].astype(F32) * ngb_ref[...].astype(F32) * (i < nsteps - 1).astype(F32)
        row = lax.broadcasted_iota(jnp.int32, (tm, cw), 0)
        d1 = jnp.where(row == tm - 1, nxt[0:1], pltpu.roll(dcv, tm - 1, 0))
        d2 = jnp.where(row == tm - 1, nxt[1:2], jnp.where(row == tm - 2, nxt[0:1], pltpu.roll(dcv, tm - 2, 0)))
        dcin = w2 * dcv + w1 * d1 + w0 * d2
        dz_ref[:, cw:2 * cw] = (dcin * hc).astype(BF16)
        dz_ref[:, 2 * cw:3 * cw] = (dcin * gc).astype(BF16)
        sm_ref[0:1, :] += jnp.sum(dcv * x2, axis=0, keepdims=True)
        sm_ref[1:2, :] += jnp.sum(dcv * x1, axis=0, keepdims=True)
        sm_ref[2:3, :] += jnp.sum(dcv * cin, axis=0, keepdims=True)
        sm_ref[3:4, :] += jnp.sum(dcv, axis=0, keepdims=True)

        for h in range(heads):
            lo = h * GROUP
            vcol = slice(3 * cw + gw + lo, 3 * cw + gw + lo + GROUP)
            ucol = slice(3 * cw + lo, 3 * cw + lo + GROUP)
            vh = z_ref[:, vcol].astype(F32)
            rv = lax.rsqrt(jnp.mean(vh * vh, axis=-1, keepdims=True) + EPS)
            xh = vh * rv
            gvh = gv_ref[:, lo:lo + GROUP]
            vn = (xh * gvh).astype(BF16)
            w = _tril(ws_ref[h]).astype(BF16)
            dgv = jnp.zeros((1, GROUP), F32)
            for n in range(tm // GROUP):
                rows = slice(n * GROUP, (n + 1) * GROUP)
                sg = _dot_nn(w, vn[rows]) + bt_ref[:, h:h + 1]
                dyg = dy_ref[rows, cw + lo:cw + lo + GROUP].astype(F32)
                dsg = dyg * z_ref[rows, ucol].astype(F32)
                dz_ref[rows, ucol] = (dyg * sg).astype(BF16)
                dsgb = dsg.astype(BF16)
                dvn = _dot_tn(w, dsgb)
                dws_ref[h] += _dot_nt(dsgb, vn[rows])
                dsg_ref[:, lo:lo + GROUP] += dsg
                xhc = xh[rows]
                dgv = dgv + jnp.sum(dvn * xhc, axis=0, keepdims=True)
                dxh = dvn * gvh
                dv = rv[rows] * (dxh - xhc * jnp.mean(dxh * xhc, axis=-1, keepdims=True))
                dz_ref[rows, vcol] = dv.astype(BF16)
            sm_ref[4:5, lo:lo + GROUP] += dgv

        @pl.when(i == nsteps - 1)
        def _():
            for h in range(heads):
                dws_ref[h] = _tril(dws_ref[h])
                dbt_ref[:, h:h + 1] = jnp.sum(dsg_ref[:, h * GROUP:(h + 1) * GROUP], axis=-1, keepdims=True)

    fixed2 = lambda i: (0, 0)
    fixed3 = lambda i: (0, 0, 0)
    prev = lambda col: (lambda i: (jnp.maximum(i * hb - 1, 0), col))
    nxt_blk = lambda i: (jnp.minimum((i + 1) * hb, last_halo), 0)
    outs, job_outs = _call(
        name, body, (nsteps,),
        [pl.BlockSpec((tm, zc), lambda i: (i, 0)),
         pl.BlockSpec((HALO, cw), prev(1)), pl.BlockSpec((HALO, cw), prev(2)),
         pl.BlockSpec((HALO, cw), nxt_blk),
         pl.BlockSpec((tm, cw + gw), lambda i: (i, 0)), pl.BlockSpec((HALO, cw), nxt_blk),
         pl.BlockSpec(conv_w.shape, fixed2), pl.BlockSpec(conv_b.shape, fixed2),
         pl.BlockSpec(g_v.shape, fixed2), pl.BlockSpec(w_s.shape, fixed3), pl.BlockSpec(b_t.shape, fixed2)],
        [pl.BlockSpec((tm, zc), lambda i: (i, 0)), pl.BlockSpec((8, cw), fixed2),
         pl.BlockSpec(w_s.shape, fixed3), pl.BlockSpec(b_t.shape, fixed2)],
        [_sds((s, zc), BF16), _sds((8, cw), F32), _sds(w_s.shape, F32), _sds(b_t.shape, F32)],
        [z, z, z, z, dy, dy, conv_w, conv_b, g_v, w_s, b_t],
        scratch=[pltpu.VMEM((GROUP, gw), F32)], sem=("arbitrary",), jobs=jobs)
    return _ret(outs, job_outs, jobs, single=False)


def _softmax_rows(sc):
    e = jnp.exp(sc - jnp.max(sc, axis=-1, keepdims=True))
    return e / jnp.sum(e, axis=-1, keepdims=True)


def attn_fwd(name, q, k, v, tm=512):
    s, d = q.shape
    m = k.shape[0]
    hd = d // XA_HEADS
    scale = hd ** -0.5
    tm = _tile(s, tm, 8)

    def body(q_ref, k_ref, v_ref, o_ref):
        for h in range(XA_HEADS):
            cols = slice(h * hd, (h + 1) * hd)
            p = _softmax_rows(_dot_nt(q_ref[:, cols], k_ref[:, cols]) * scale)
            o_ref[:, cols] = _dot_nn(p.astype(BF16), v_ref[:, cols]).astype(BF16)

    return _call(
        name, body, (s // tm,),
        [pl.BlockSpec((tm, d), lambda i: (i, 0)), pl.BlockSpec((m, d), lambda i: (0, 0)),
         pl.BlockSpec((m, d), lambda i: (0, 0))],
        [pl.BlockSpec((tm, d), lambda i: (i, 0))], [_sds((s, d), BF16)], [q, k, v], sem=("arbitrary",))[0][0]


def attn_bwd(name, q, k, v, do, tm=512):
    s, d = q.shape
    m = k.shape[0]
    hd = d // XA_HEADS
    scale = hd ** -0.5
    tm = _tile(s, tm, 8)

    def body(q_ref, k_ref, v_ref, do_ref, dq_ref, dk_ref, dv_ref):
        i = pl.program_id(0)

        @pl.when(i == 0)
        def _():
            dk_ref[...] = jnp.zeros_like(dk_ref)
            dv_ref[...] = jnp.zeros_like(dv_ref)

        for h in range(XA_HEADS):
            cols = slice(h * hd, (h + 1) * hd)
            qh = q_ref[:, cols]
            doh = do_ref[:, cols]
            p = _softmax_rows(_dot_nt(qh, k_ref[:, cols]) * scale)
            dp = _dot_nt(doh, v_ref[:, cols])
            ds = (p * (dp - jnp.sum(dp * p, axis=-1, keepdims=True)) * scale).astype(BF16)
            dq_ref[:, cols] = _dot_nn(ds, k_ref[:, cols]).astype(BF16)
            dk_ref[:, cols] += _dot_tn(ds, qh)
            dv_ref[:, cols] += _dot_tn(p.astype(BF16), doh)

    row = lambda i: (i, 0)
    fixed = lambda i: (0, 0)
    return _call(
        name, body, (s // tm,),
        [pl.BlockSpec((tm, d), row), pl.BlockSpec((m, d), fixed), pl.BlockSpec((m, d), fixed),
         pl.BlockSpec((tm, d), row)],
        [pl.BlockSpec((tm, d), row), pl.BlockSpec((m, d), fixed), pl.BlockSpec((m, d), fixed)],
        [_sds((s, d), BF16), _sds((m, d), F32), _sds((m, d), F32)], [q, k, v, do], sem=("arbitrary",))[0]


def _grid2(rows, cols, row_mult):
    tr, tc = _tile(rows, 512, row_mult), _tile(cols, 2048)
    return tr, tc, rows // tr, cols // tc


def cast_place(name, block, axis, place):
    r, c = block.shape
    tr, tc, nbr, nbc = _grid2(r, c, 16)
    if axis == 1:
        dst = lambda i, j, p: (i, j + p[0] * nbc)
    else:
        dst = lambda i, j, p: (i + p[0] * nbr, j)

    def body(p_ref, w_ref, out_ref):
        out_ref[...] = w_ref[...].astype(BF16)

    return pl.pallas_call(
        body, name=name,
        grid_spec=pltpu.PrefetchScalarGridSpec(
            num_scalar_prefetch=1, grid=(nbr, nbc),
            in_specs=[pl.BlockSpec((tr, tc), lambda i, j, p: (i, j))],
            out_specs=pl.BlockSpec((tr, tc), dst)),
        out_shape=_sds(_full_shape(block.shape, axis), BF16),
        compiler_params=_params(("parallel", "parallel")),
    )(place, block)


def pair_add(name, grad, peer, axis, place):
    hr, hc = peer.shape
    tr, tc, nbr, nbc = _grid2(hr, hc, 16)
    if axis == 1:
        mine = lambda i, j, p: (i + p[1] * nbr, j)
    else:
        mine = lambda i, j, p: (i, j + p[1] * nbc)
    same = lambda i, j, p: (i, j)

    def body(p_ref, g_ref, q_ref, out_ref):
        out_ref[...] = (g_ref[...].astype(F32) + q_ref[...].astype(F32)).astype(BF16)

    return pl.pallas_call(
        body, name=name,
        grid_spec=pltpu.PrefetchScalarGridSpec(
            num_scalar_prefetch=1, grid=(nbr, nbc),
            in_specs=[pl.BlockSpec((tr, tc), mine), pl.BlockSpec((tr, tc), same)],
            out_specs=pl.BlockSpec((tr, tc), same)),
        out_shape=_sds((hr, hc), BF16),
        compiler_params=_params(("parallel", "parallel")),
    )(place, grad, peer)


def cross_sum(name, part, land, axis, shape, place):
    _, sr, sc = land.shape
    tr, tc, nbr, nbc = _grid2(sr, sc, 16)
    if axis == 1:
        own = lambda i, j, p: (i, j + p[0] * nbc)
        dst = lambda i, j, p: (i + p[1] * nbr, j)
    else:
        own = lambda i, j, p: (i + p[0] * nbr, j)
        dst = lambda i, j, p: (i, j + p[1] * nbc)

    def body(p_ref, own_ref, land_ref, out_ref):
        out_ref[...] = ((own_ref[...].astype(F32) + land_ref[0].astype(F32))
                        + (land_ref[1].astype(F32) + land_ref[2].astype(F32)))

    return pl.pallas_call(
        body, name=name,
        grid_spec=pltpu.PrefetchScalarGridSpec(
            num_scalar_prefetch=1, grid=(nbr, nbc),
            in_specs=[pl.BlockSpec((tr, tc), own), pl.BlockSpec((3, tr, tc), lambda i, j, p: (0, i, j))],
            out_specs=pl.BlockSpec((tr, tc), dst)),
        out_shape=_sds(_block(shape, axis), F32),
        compiler_params=_params(("parallel", "parallel")),
    )(place, part, land)


def _adam_math(w, g, m, v):
    m = ADAM_B1 * m + (1.0 - ADAM_B1) * g
    v = ADAM_B2 * v + (1.0 - ADAM_B2) * (g * g)
    m_hat = m / (1.0 - ADAM_B1 ** ADAM_STEP)
    v_hat = v / (1.0 - ADAM_B2 ** ADAM_STEP)
    delta = -ADAM_LR * (m_hat / (jnp.sqrt(v_hat) + ADAM_EPS) + ADAM_WD * w)
    return delta, m, v


def adamw(name, w, g, m, v):
    r, c = w.shape
    tr, tc = _tile(r, 256, 8), _tile(c, 1408)

    def body(w_ref, g_ref, m_ref, v_ref, g_out, d_out, m_out, v_out):
        d, mm, vv = _adam_math(w_ref[...], g_ref[...], m_ref[...], v_ref[...])
        g_out[...] = g_ref[...]
        d_out[...] = d
        m_out[...] = mm
        v_out[...] = vv

    spec = pl.BlockSpec((tr, tc), lambda i, j: (i, j))
    return _call(name, body, (r // tr, c // tc), [spec] * 4, [spec] * 4, [_sds((r, c), F32)] * 4, [w, g, m, v],
                 sem=("parallel", "parallel"))[0]


def small_sum(name, stacks):
    def body(*refs):
        for s_ref, out_ref in zip(refs[:len(stacks)], refs[len(stacks):]):
            acc = s_ref[0]
            for d in range(1, s_ref.shape[0]):
                acc = acc + s_ref[d]
            out_ref[...] = acc

    return pl.pallas_call(body, name=name, out_shape=[_sds(s.shape[1:], F32) for s in stacks])(*stacks)


WEIGHTS = ["g_ffn1", "w_ffn1_in", "w_ffn1_out", "g_mix", "w_mix_in", "conv_w", "conv_b", "g_gm_v", "w_spatial",
           "b_spatial", "w_mix_out", "g_xattn", "g_mem", "w_xq", "w_xk", "w_xv", "w_xo", "g_ffn2", "w_ffn2_in",
           "w_ffn2_out", "g_final"]
BIG = {"w_ffn1_in": 1, "w_ffn1_out": 0, "w_mix_in": 1, "w_mix_out": 0, "w_xq": 0, "w_xk": 0, "w_xv": 0, "w_xo": 0,
       "w_ffn2_in": 1, "w_ffn2_out": 0}
SMALL = [n for n in WEIGHTS if n not in BIG]
LATE_SMALL = ["g_ffn1"]
EARLY_SMALL = [n for n in SMALL if n not in LATE_SMALL]


def _pack(arrays):
    flat = jnp.concatenate([a.reshape(-1) for a in arrays])
    rows = -(-flat.shape[0] // 1024) * 8
    return jnp.pad(flat, (0, rows * 128 - flat.shape[0])).reshape(rows, 128)


def _unpack(buf, shapes):
    flat = buf.reshape(-1)
    out, pos = [], 0
    for shp in shapes:
        n = math.prod(shp)
        out.append(flat[pos:pos + n].reshape(shp))
        pos += n
    return out


def kernel(x, mem, g_ffn1, w_ffn1_in, w_ffn1_out, g_mix, w_mix_in, conv_w, conv_b, g_gm_v, w_spatial, b_spatial, w_mix_out, g_xattn, g_mem, w_xq, w_xk, w_xv, w_xo, g_ffn2, w_ffn2_in, w_ffn2_out, g_final, loss_target, m_g_ffn1, m_w_ffn1_in, m_w_ffn1_out, m_g_mix, m_w_mix_in, m_conv_w, m_conv_b, m_g_gm_v, m_w_spatial, m_b_spatial, m_w_mix_out, m_g_xattn, m_g_mem, m_w_xq, m_w_xk, m_w_xv, m_w_xo, m_g_ffn2, m_w_ffn2_in, m_w_ffn2_out, m_g_final, v_g_ffn1, v_w_ffn1_in, v_w_ffn1_out, v_g_mix, v_w_mix_in, v_conv_w, v_conv_b, v_g_gm_v, v_w_spatial, v_b_spatial, v_w_mix_out, v_g_xattn, v_g_mem, v_w_xq, v_w_xk, v_w_xv, v_w_xo, v_g_ffn2, v_w_ffn2_in, v_w_ffn2_out, v_g_final):
    given = dict(locals())
    wts = {n: given[n] for n in WEIGHTS}
    mom = {n: given["m_" + n] for n in WEIGHTS}
    var = {n: given["v_" + n] for n in WEIGHTS}

    xi, yi, ci = lax.axis_index("x"), lax.axis_index("y"), lax.axis_index("c")
    blk = 2 * xi + yi
    place = jnp.stack([blk, ci]).astype(jnp.int32)

    x2, mem2, tgt = x[0], mem[0], loss_target[0]
    own = {n: cast_place("cast_" + n, wts[n][0], BIG[n], place) for n in BIG}
    shape = {n: own[n].shape for n in BIG}
    w_s, b_t = w_spatial[0], b_spatial[0].T
    gf = g_final[None]

    def gather(*names):
        return gather_job([(own[n], BIG[n], WHOLE) for n in names])

    full = {}

    (full["w_ffn1_in"],), (conv_taps,) = comm_only(
        "gather_first", [gather("w_ffn1_in"), columns_job(jnp.pad(conv_w[0], ((0, 8 - CONV_K), (0, 0))))])
    n1, r1 = rmsnorm_fwd("norm1", x2, g_ffn1)
    (gu1, a1), (got,) = swiglu_fwd("ffn1_in", n1, full["w_ffn1_in"], jobs=[gather("w_ffn1_out", "w_mix_in")])
    full["w_ffn1_out"], full["w_mix_in"] = got
    h1, (got,) = mm_nn_resid("ffn1_out", a1, full["w_ffn1_out"], x2, 0.5, tm=512, tk=5632,
                             jobs=[gather("w_mix_out", "w_xq", "w_xk")])
    full["w_mix_out"], full["w_xq"], full["w_xk"] = got
    n2, r2 = rmsnorm_fwd("norm2", h1, g_mix)
    z, (got,) = mm_nn("mix_in", n2, full["w_mix_in"], BF16, jobs=[gather("w_xv", "w_xo")])
    full["w_xv"], full["w_xo"] = got

    pieces = 4
    def ffn2_piece(p, prev):
        return [gather_job([(prev, 1, (p, 1, pieces))])]

    ycat, ((w2in,),) = mixer_fwd("mixer", z, conv_taps, conv_b, g_gm_v, w_s, b_t,
                                 jobs=ffn2_piece(0, own["w_ffn2_in"]))
    h2, ((w2in,),) = mm_nn_resid("mix_out", ycat, full["w_mix_out"], h1, 1.0, tk=2048, jobs=ffn2_piece(1, w2in))
    n3, r3 = rmsnorm_fwd("norm3", h2, g_xattn)
    mn, rm = rmsnorm_fwd("norm_mem", mem2, g_mem)
    q, ((w2in,),) = mm_nn("xq", n3, full["w_xq"], BF16, jobs=ffn2_piece(2, w2in))
    k = mm_nn("xk", mn, full["w_xk"], BF16)
    v = mm_nn("xv", mn, full["w_xv"], BF16)
    o = attn_fwd("attn", q, k, v)
    h3, ((w2in,),) = mm_nn_resid("xo", o, full["w_xo"], h2, 1.0, tk=2048, jobs=ffn2_piece(3, w2in))
    full["w_ffn2_in"] = w2in
    n4, r4 = rmsnorm_fwd("norm4", h3, g_ffn2)
    (gu2, a2), ((full["w_ffn2_out"],),) = swiglu_fwd("ffn2_in", n4, full["w_ffn2_in"], jobs=[gather("w_ffn2_out")])
    h4 = mm_nn_resid("ffn2_out", a2, full["w_ffn2_out"], h3, 0.5, tm=512, tk=5632)
    loss_blk, dh4, dh4b, dg_final = loss_head("loss_head", h4, gf, tgt)

    dw, peer, part, land, half, grads = {}, {}, {}, {}, {}, {}

    def send_pair(*names):
        return pair_job([dw[n] for n in names], [BIG[n] for n in names])

    def take_pair(names, got):
        for n, p in zip(names, got):
            part[n] = pair_add("pair_add_" + n, dw[n], p, BIG[n], place)

    def send_cross(*names, sub=WHOLE):
        return cross_job([(part[n], BIG[n], shape[n], land.get(n), sub) for n in names])

    def take_cross(names, got, last=True):
        for n, l in zip(names, got):
            land[n] = l
            if last:
                half[n] = cross_sum("cross_sum_" + n, part[n], l, BIG[n], shape[n], place)

    def send_final(*names):
        return final_job([half[n] for n in names], [BIG[n] for n in names], [shape[n] for n in names])

    delta, new_m, new_v = {}, {}, {}

    def take_final(names, got):
        for n, g in zip(names, got):
            grads[n], delta[n], new_m[n], new_v[n] = adamw("adamw_" + n, wts[n][0], g, mom[n][0], var[n][0])

    dw["w_ffn2_out"] = mm_tn("ffn2_dwout", a2, dh4b, BF16, scale=0.5)
    dgu2, (got,) = swiglu_bwd("ffn2_dact", dh4b, full["w_ffn2_out"], gu2, 0.5, jobs=[send_pair("w_ffn2_out")])
    take_pair(["w_ffn2_out"], got)
    dw["w_ffn2_in"], (got,) = mm_tn_pair("ffn2_dwin", n4, dgu2, BF16, jobs=[send_cross("w_ffn2_out")])
    take_cross(["w_ffn2_out"], got)
    dn4, (got_f, got_p) = mm_nt_pair("ffn2_dn", dgu2, full["w_ffn2_in"], F32,
                                     jobs=[send_final("w_ffn2_out"), send_pair("w_ffn2_in")])
    take_final(["w_ffn2_out"], got_f)
    take_pair(["w_ffn2_in"], got_p)
    dh3, dh3b, dg_ffn2 = rmsnorm_bwd("norm4_bwd", dn4, h3, r4, g_ffn2, dh4)

    dw["w_xo"] = mm_tn("xo_dw", o, dh3b, BF16)
    do = mm_nt("xo_dx", dh3b, full["w_xo"], BF16)
    dq, dk, dv = attn_bwd("attn_bwd", q, k, v, do)
    dkb, dvb = dk.astype(BF16), dv.astype(BF16)
    dw["w_xq"] = mm_tn("xq_dw", n3, dq, BF16)
    dn3 = mm_nt("xq_dx", dq, full["w_xq"], F32)
    dh2, dh2b, dg_xattn = rmsnorm_bwd("norm3_bwd", dn3, h2, r3, g_xattn, dh3)
    dw["w_xk"] = mm_tn("xk_dw", mn, dkb, BF16)
    dw["w_xv"] = mm_tn("xv_dw", mn, dvb, BF16)
    dmn_k = mm_nt("xk_dx", dkb, full["w_xk"], F32)
    dmn_v = mm_nt("xv_dx", dvb, full["w_xv"], F32)
    dg_mem = gain_grad("norm_mem_bwd", dmn_k, dmn_v, mem2, rm)

    dw["w_mix_out"] = mm_tn("mix_out_dw", ycat, dh2b, BF16)
    dycat = mm_nt("mix_out_dx", dh2b, full["w_mix_out"], BF16)
    attn_names = ["w_xo", "w_xq", "w_xk", "w_xv", "w_mix_out"]
    (dz, dsmall, dws, dbt), (got_c, got_p) = mixer_bwd(
        "mixer_bwd", z, dycat, conv_taps, conv_b, g_gm_v, w_s, b_t,
        jobs=[send_cross("w_ffn2_in", sub=(0, 2, 8)), send_pair(*attn_names)])
    take_cross(["w_ffn2_in"], got_c, last=False)
    take_pair(attn_names, got_p)
    dw["w_mix_in"], (got_c,) = mm_tn("mix_in_dw", n2, dz, BF16, jobs=[send_cross("w_ffn2_in", sub=(2, 3, 8))])
    take_cross(["w_ffn2_in"], got_c, last=False)
    dn2, (got_c, got_p) = mm_nt("mix_in_dx", dz, full["w_mix_in"], F32, tk=2560,
                                jobs=[send_cross("w_ffn2_in", sub=(5, 3, 8)), send_pair("w_mix_in")])
    take_cross(["w_ffn2_in"], got_c)
    take_pair(["w_mix_in"], got_p)
    dh1, dh1b, dg_mix = rmsnorm_bwd("norm2_bwd", dn2, h1, r2, g_mix, dh2)

    xa_names = ["w_xo", "w_xq", "w_xk", "w_xv"]
    dgu1, (got_c, got_f) = swiglu_bwd("ffn1_dact", dh1b, full["w_ffn1_out"], gu1, 0.5,
                                      jobs=[send_cross(*xa_names), send_final("w_ffn2_in")])
    take_cross(xa_names, got_c)
    take_final(["w_ffn2_in"], got_f)
    mix_names = ["w_mix_out", "w_mix_in"]
    dw["w_ffn1_in"], (got_c, got_f) = mm_tn_pair("ffn1_dwin", n1, dgu1, BF16,
                                                 jobs=[send_cross(*mix_names), send_final(*xa_names)])
    take_cross(mix_names, got_c)
    take_final(xa_names, got_f)
    dw["w_ffn1_out"], (got_p, got_f) = mm_tn("ffn1_dwout", a1, dh1b, BF16, scale=0.5,
                                             jobs=[send_pair("w_ffn1_in"), send_final(*mix_names)])
    take_pair(["w_ffn1_in"], got_p)
    take_final(mix_names, got_f)

    early = {"g_mix": dg_mix, "conv_w": dsmall[0:CONV_K], "conv_b": dsmall[3:4], "g_gm_v": dsmall[4:5],
             "w_spatial": dws, "b_spatial": dbt.T, "g_xattn": dg_xattn, "g_mem": dg_mem, "g_ffn2": dg_ffn2,
             "g_final": dg_final}
    dn1, (got_c, got_p, (early_all,)) = mm_nt_pair(
        "ffn1_dn", dgu1, full["w_ffn1_in"], F32,
        jobs=[send_cross("w_ffn1_in"), send_pair("w_ffn1_out"), stack_job(_pack([early[n] for n in EARLY_SMALL]))])
    take_cross(["w_ffn1_in"], got_c)
    take_pair(["w_ffn1_out"], got_p)
    (dx, _, dg_ffn1), (got_c, got_f) = rmsnorm_bwd("norm1_bwd", dn1, x2, r1, g_ffn1, dh1,
                                                   jobs=[send_cross("w_ffn1_out"), send_final("w_ffn1_in")])
    take_cross(["w_ffn1_out"], got_c)
    take_final(["w_ffn1_in"], got_f)
    got_f, (late_all,) = comm_only("tail_exchange", [send_final("w_ffn1_out"), stack_job(_pack([dg_ffn1]))])
    take_final(["w_ffn1_out"], got_f)

    early_sum, late_sum = small_sum("small_sum", [early_all, late_all])
    for n, g in zip(EARLY_SMALL, _unpack(early_sum, [early[n].shape for n in EARLY_SMALL])):
        grads[n] = g
    grads["g_ffn1"] = _unpack(late_sum, [dg_ffn1.shape])[0]
    taps_cols = conv_w.shape[2]
    grads["conv_w"] = lax.dynamic_slice_in_dim(grads["conv_w"], blk * taps_cols, taps_cols, axis=1)
    packed = [_pack([src[n] for n in SMALL]) for src in (wts, grads, mom, var)]
    own_shapes = [wts[n].shape for n in SMALL]
    for dst, buf in zip((delta, new_m, new_v), adamw("adamw_small", *packed)[1:]):
        for n, a in zip(SMALL, _unpack(buf, own_shapes)):
            dst[n] = a

    loss = lax.psum(loss_blk[0, 0], ("x", "y", "c"))
    outs = [loss, dx[None]]
    for group in (grads, delta, new_m, new_v):
        outs += [group[n].reshape(wts[n].shape) for n in WEIGHTS]
    return tuple(outs)
```

```python
import math

import jax
import jax.numpy as jnp
from jax import lax
from jax.experimental import pallas as pl
from jax.experimental.pallas import tpu as pltpu

F32 = jnp.float32
BF16 = jnp.bfloat16
EPS = 1e-6
GROUP = 128
XA_HEADS = 4
CONV_K = 3
N_CHIPS = 4
VMEM_LIMIT_BYTES = 56 * 1024 * 1024

ADAM_LR = 0.001
ADAM_B1 = 0.9
ADAM_B2 = 0.999
ADAM_EPS = 1e-08
ADAM_WD = 0.01
ADAM_STEP = 10

MESH = pl.DeviceIdType.MESH
ANY = pl.BlockSpec(memory_space=pl.ANY)


def _tile(dim, pref, mult=128):
    if dim <= pref:
        return dim
    t = (pref // mult) * mult
    while t >= mult:
        if dim % t == 0:
            return t
        t -= mult
    raise ValueError(f"no tile for {dim} under {pref}")


def _params(sem):
    return pltpu.CompilerParams(dimension_semantics=sem, vmem_limit_bytes=VMEM_LIMIT_BYTES)


def _sds(shape, dtype):
    return jax.ShapeDtypeStruct(shape, dtype)


def _dot_nn(a, b):
    return jnp.dot(a, b, preferred_element_type=F32)


def _dot_nt(a, b):
    return lax.dot_general(a, b, (((1,), (1,)), ((), ())), preferred_element_type=F32)


def _dot_tn(a, b):
    return lax.dot_general(a, b, (((0,), (0,)), ((), ())), preferred_element_type=F32)


class Job:
    def __init__(self, inputs, out_shapes, aliases, sems, start, middle, finish):
        self.inputs, self.out_shapes, self.aliases, self.sems = inputs, out_shapes, aliases, sems
        self.start, self.middle, self.finish = start, middle, finish


def _place():
    x, y, c = lax.axis_index("x"), lax.axis_index("y"), lax.axis_index("c")
    chips = [(1 - x, y), (x, 1 - y), (1 - x, 1 - y)]
    return x, y, c, chips


def _ds(start, size, lane):
    if not isinstance(start, int):
        start = pl.multiple_of(start, 128 if lane else 16)
    return pl.ds(start, size)


WHOLE = (0, 1, 1)


def _window(ref, axis, shape, blk=None, half=None, sub=WHOLE):
    n = shape[axis] // N_CHIPS
    hs = shape[1 - axis] // 2
    idx = [slice(None), slice(None)]
    if blk is not None:
        idx[axis] = _ds(blk * n, n, axis == 1)
    first, count, pieces = sub
    ext = hs // pieces
    if half is not None:
        idx[1 - axis] = _ds(half * hs + first * ext, count * ext, axis == 0)
    elif pieces > 1:
        idx[1 - axis] = _ds(first * ext, count * ext, axis == 0)
    return ref.at[tuple(idx)]


def _remote(src, dst, send_sem, recv_sem, dev):
    return pltpu.make_async_remote_copy(src_ref=src, dst_ref=dst, send_sem=send_sem, recv_sem=recv_sem,
                                        device_id=dev, device_id_type=MESH)


def _full_shape(block_shape, axis):
    out = list(block_shape)
    out[axis] *= N_CHIPS
    return tuple(out)


def _half_all(shape, axis):
    out = list(shape)
    out[1 - axis] //= 2
    return tuple(out)


def _block(shape, axis):
    out = list(shape)
    out[axis] //= N_CHIPS
    return tuple(out)


def _half_block(shape, axis):
    return _half_all(_block(shape, axis), axis)


def gather_job(items):
    nw = len(items)
    shapes = [full.shape for full, _, _ in items]
    n_sem = 8

    def parts(sub):
        first, count, pieces = sub
        return (2 * first, count, 2 * pieces), (2 * first + count, count, 2 * pieces)

    def start(pos, ins, outs, sems):
        x, y, c, chips = pos
        for w, (_, ax, sub) in enumerate(items):
            mine = _window(outs[w], ax, shapes[w], blk=2 * x + y, half=c, sub=sub)
            for j in range(2):
                _remote(mine, mine, sems[0].at[n_sem * w + j], sems[1].at[n_sem * w + j], (*chips[j], c)).start()

    def middle(pos, ins, outs, sems):
        x, y, c, chips = pos
        for w, (_, ax, sub) in enumerate(items):
            for j in range(2):
                cx, cy = chips[j]
                landed = _window(outs[w], ax, shapes[w], blk=2 * cx + cy, half=c, sub=sub)
                _remote(landed, landed, sems[0].at[n_sem * w + j], sems[1].at[n_sem * w + j], (cx, cy, c)).wait_recv()
                part = _window(outs[w], ax, shapes[w], blk=2 * cx + cy, half=c, sub=parts(sub)[j])
                _remote(part, part, sems[0].at[n_sem * w + 2 + j], sems[1].at[n_sem * w + 2 + j],
                        (*chips[1 - j], c)).start()
                _remote(landed, landed, sems[0].at[n_sem * w + 4 + j], sems[1].at[n_sem * w + 4 + j],
                        (x, y, 1 - c)).start()

    def finish(pos, ins, outs, sems):
        x, y, c, chips = pos
        sib = (x, y, 1 - c)
        for w, (_, ax, sub) in enumerate(items):
            dx, dy = chips[2]
            for j in range(2):
                part = _window(outs[w], ax, shapes[w], blk=2 * dx + dy, half=c, sub=parts(sub)[j])
                cp = _remote(part, part, sems[0].at[n_sem * w + 2 + j], sems[1].at[n_sem * w + 2 + j], sib)
                cp.wait_recv()
                cp.wait_send()
            diag = _window(outs[w], ax, shapes[w], blk=2 * dx + dy, half=c, sub=sub)
            _remote(diag, diag, sems[0].at[n_sem * w + 6], sems[1].at[n_sem * w + 6], sib).start()
        for w, (_, ax, sub) in enumerate(items):
            for j, (cx, cy) in enumerate(chips):
                passed = _window(outs[w], ax, shapes[w], blk=2 * cx + cy, half=1 - c, sub=sub)
                cp = _remote(passed, passed, sems[0].at[n_sem * w + 4 + j], sems[1].at[n_sem * w + 4 + j], sib)
                cp.wait_recv()
                cp.wait_send()
            mine = _window(outs[w], ax, shapes[w], blk=2 * x + y, half=c, sub=sub)
            for j in range(2):
                _remote(mine, mine, sems[0].at[n_sem * w + j], sems[1].at[n_sem * w + j], sib).wait_send()

    sems = [pltpu.SemaphoreType.DMA((n_sem * nw,)), pltpu.SemaphoreType.DMA((n_sem * nw,))]
    return Job([full for full, _, _ in items], [_sds(full.shape, full.dtype) for full, _, _ in items],
               {w: w for w in range(nw)}, sems, start, middle, finish)


def pair_job(grads, axes):
    nw = len(grads)
    shapes = [g.shape for g in grads]

    def start(pos, ins, outs, sems):
        x, y, c, _ = pos
        for w in range(nw):
            _remote(_window(ins[w], axes[w], shapes[w], half=1 - c), outs[w], sems[0].at[w], sems[1].at[w],
                    (x, y, 1 - c)).start()

    def finish(pos, ins, outs, sems):
        x, y, c, _ = pos
        for w in range(nw):
            cp = _remote(outs[w], outs[w], sems[0].at[w], sems[1].at[w], (x, y, 1 - c))
            cp.wait_recv()
            cp.wait_send()

    sems = [pltpu.SemaphoreType.DMA((nw,)), pltpu.SemaphoreType.DMA((nw,))]
    return Job(list(grads), [_sds(_half_all(s, a), BF16) for s, a in zip(shapes, axes)], {}, sems, start, None,
               finish)


def cross_job(items):
    nw = len(items)
    inputs, aliases = [], {}
    for w, (part, ax, shape, prev, sub) in enumerate(items):
        inputs.append(part)
        if prev is not None:
            aliases[len(inputs)] = w
            inputs.append(prev)

    def copies(pos, ins, outs, sems):
        x, y, c, chips = pos
        k = 0
        for w, (_, ax, shape, prev, sub) in enumerate(items):
            src = ins[k]
            k += 2 if prev is not None else 1
            for j, (cx, cy) in enumerate(chips):
                slot = _window(outs[w].at[j], ax, shape, sub=sub)
                yield (_remote(_window(src, ax, shape, blk=2 * cx + cy, sub=sub), slot,
                               sems[0].at[3 * w + j], sems[1].at[3 * w + j], (cx, cy, c)),
                       _remote(slot, slot, sems[0].at[3 * w + j], sems[1].at[3 * w + j], (cx, cy, c)))

    def start(pos, ins, outs, sems):
        for send, _ in copies(pos, ins, outs, sems):
            send.start()

    def finish(pos, ins, outs, sems):
        for send, recv in copies(pos, ins, outs, sems):
            recv.wait_recv()
            send.wait_send()

    sems = [pltpu.SemaphoreType.DMA((3 * nw,)), pltpu.SemaphoreType.DMA((3 * nw,))]
    out_shapes = [_sds((3,) + _half_block(shape, ax), BF16) for _, ax, shape, _, _ in items]
    return Job(inputs, out_shapes, aliases, sems, start, None, finish)


def final_job(blocks, axes, shapes):
    nw = len(blocks)

    def start(pos, ins, outs, sems):
        x, y, c, _ = pos
        for w in range(nw):
            mine = _window(outs[w], axes[w], shapes[w], half=c)
            _remote(mine, mine, sems[0].at[w], sems[1].at[w], (x, y, 1 - c)).start()

    def finish(pos, ins, outs, sems):
        x, y, c, _ = pos
        for w in range(nw):
            theirs = _window(outs[w], axes[w], shapes[w], half=1 - c)
            cp = _remote(theirs, theirs, sems[0].at[w], sems[1].at[w], (x, y, 1 - c))
            cp.wait_recv()
            cp.wait_send()

    sems = [pltpu.SemaphoreType.DMA((nw,)), pltpu.SemaphoreType.DMA((nw,))]
    return Job(list(blocks), [_sds(b.shape, b.dtype) for b in blocks], {w: w for w in range(nw)}, sems, start, None,
               finish)


def stack_job(small):
    def peers(pos):
        x, y, c, _ = pos
        for k in range(1, 8):
            yield k - 1, (1 - x if k & 4 else x, 1 - y if k & 2 else y, 1 - c if k & 1 else c)

    def start(pos, ins, outs, sems):
        x, y, c, _ = pos
        mine = outs[0].at[4 * x + 2 * y + c]
        pltpu.make_async_copy(ins[0], mine, sems[2]).start()
        for k, dev in peers(pos):
            _remote(ins[0], mine, sems[0].at[k], sems[1].at[k], dev).start()

    def finish(pos, ins, outs, sems):
        x, y, c, _ = pos
        for k, (px, py, pc) in peers(pos):
            slot = outs[0].at[4 * px + 2 * py + pc]
            cp = _remote(slot, slot, sems[0].at[k], sems[1].at[k], (px, py, pc))
            cp.wait_recv()
            cp.wait_send()
        pltpu.make_async_copy(ins[0], outs[0].at[4 * x + 2 * y + c], sems[2]).wait()

    sems = [pltpu.SemaphoreType.DMA((7,)), pltpu.SemaphoreType.DMA((7,)), pltpu.SemaphoreType.DMA]
    return Job([small], [_sds((8,) + small.shape, small.dtype)], {}, sems, start, None, finish)


def columns_job(block):
    cols = block.shape[1]
    place = lambda out, b: out.at[:, _ds(b * cols, cols, True)]

    def start(pos, ins, outs, sems):
        x, y, c, chips = pos
        pltpu.make_async_copy(ins[0], place(outs[0], 2 * x + y), sems[2]).start()
        for j, (cx, cy) in enumerate(chips):
            _remote(ins[0], place(outs[0], 2 * x + y), sems[0].at[j], sems[1].at[j], (cx, cy, c)).start()

    def finish(pos, ins, outs, sems):
        x, y, c, chips = pos
        for j, (cx, cy) in enumerate(chips):
            got = place(outs[0], 2 * cx + cy)
            cp = _remote(got, got, sems[0].at[j], sems[1].at[j], (cx, cy, c))
            cp.wait_recv()
            cp.wait_send()
        pltpu.make_async_copy(ins[0], place(outs[0], 2 * x + y), sems[2]).wait()

    sems = [pltpu.SemaphoreType.DMA((3,)), pltpu.SemaphoreType.DMA((3,)), pltpu.SemaphoreType.DMA]
    return Job([block], [_sds((block.shape[0], N_CHIPS * cols), block.dtype)], {}, sems, start, None, finish)


def _call(name, body, grid, in_specs, out_specs, out_shape, args, scratch=(), sem=None, jobs=()):
    n_in, n_out, n_sc = len(args), len(out_shape), len(scratch)
    if not jobs:
        outs = pl.pallas_call(
            body, name=name, grid=grid, in_specs=in_specs, out_specs=out_specs, out_shape=out_shape,
            scratch_shapes=list(scratch), compiler_params=_params(sem))(*args)
        return list(outs), []

    total = math.prod(grid) if grid else 1
    mid = min(total - 1, (2 * total) // 3)

    def split(refs, start, counts):
        out = []
        for n in counts:
            out.append(refs[start:start + n])
            start += n
        return out, start

    def wrapped(*refs):
        c_in = refs[:n_in]
        j_ins, p = split(refs, n_in, [len(j.inputs) for j in jobs])
        c_out = refs[p:p + n_out]
        j_outs, p = split(refs, p + n_out, [len(j.out_shapes) for j in jobs])
        c_sc = refs[p:p + n_sc]
        j_sems, p = split(refs, p + n_sc, [len(j.sems) for j in jobs])
        pos = _place()
        step = 0
        for axis, extent in enumerate(grid):
            step = step * extent + pl.program_id(axis)

        def run(phase):
            for j, ins, outs, sems in zip(jobs, j_ins, j_outs, j_sems):
                fn = getattr(j, phase)
                if fn is not None:
                    fn(pos, ins, outs, sems)

        if total == 1:
            run("start")
            body(*c_in, *c_out, *c_sc)
            run("middle")
            run("finish")
            return
        pl.when(step == 0)(lambda: run("start"))
        body(*c_in, *c_out, *c_sc)
        if any(j.middle is not None for j in jobs):
            pl.when(step == mid)(lambda: run("middle"))
        pl.when(step == total - 1)(lambda: run("finish"))

    aliases, in_at, out_at = {}, n_in, n_out
    for j in jobs:
        for src, dst in j.aliases.items():
            aliases[in_at + src] = out_at + dst
        in_at += len(j.inputs)
        out_at += len(j.out_shapes)
    outs = pl.pallas_call(
        wrapped, name=name, grid=grid,
        in_specs=list(in_specs) + [ANY] * (in_at - n_in),
        out_specs=list(out_specs) + [ANY] * (out_at - n_out),
        out_shape=list(out_shape) + [s for j in jobs for s in j.out_shapes],
        scratch_shapes=list(scratch) + [s for j in jobs for s in j.sems],
        input_output_aliases=aliases,
        compiler_params=_params(("arbitrary",) * len(grid)),
    )(*args, *[a for j in jobs for a in j.inputs])
    job_outs, p = split(outs, n_out, [len(j.out_shapes) for j in jobs])
    return list(outs[:n_out]), [list(o) for o in job_outs]


def comm_only(name, jobs):
    def body(dummy_ref, out_ref):
        out_ref[...] = dummy_ref[...]

    dummy = jnp.zeros((8, 128), F32)
    spec = pl.BlockSpec((8, 128), lambda: (0, 0))
    return _call(name, body, (), [spec], [spec], [_sds((8, 128), F32)], [dummy], jobs=jobs)[1]


def _ret(outs, job_outs, jobs, single=True):
    res = outs[0] if single else outs
    return (res, job_outs) if jobs else res


def rmsnorm_fwd(name, x, g):
    s, d = x.shape
    tm = _tile(s, 512, 8)

    def body(x_ref, g_ref, n_ref, r_ref):
        xv = x_ref[...]
        r = lax.rsqrt(jnp.mean(xv * xv, axis=-1, keepdims=True) + EPS)
        n_ref[...] = (xv * r * g_ref[...]).astype(BF16)
        r_ref[...] = r

    row = lambda i: (i, 0)
    return _call(
        name, body, (s // tm,),
        [pl.BlockSpec((tm, d), row), pl.BlockSpec((1, d), lambda i: (0, 0))],
        [pl.BlockSpec((tm, d), row), pl.BlockSpec((tm, 1), row)],
        [_sds((s, d), BF16), _sds((s, 1), F32)], [x, g], sem=("arbitrary",))[0]


def rmsnorm_bwd(name, dn, x, r, g, dh_in, jobs=()):
    s, d = x.shape
    tm = _tile(s, 512, 8)

    def body(dn_ref, x_ref, r_ref, g_ref, dh_ref, out_ref, outb_ref, dg_ref):
        i = pl.program_id(0)
        xh = x_ref[...] * r_ref[...]
        dnv = dn_ref[...]
        dxh = dnv * g_ref[...]
        dx = r_ref[...] * (dxh - xh * jnp.mean(dxh * xh, axis=-1, keepdims=True))
        out = dh_ref[...] + dx
        out_ref[...] = out
        outb_ref[...] = out.astype(BF16)
        part = jnp.sum(dnv * xh, axis=0, keepdims=True)

        @pl.when(i == 0)
        def _():
            dg_ref[...] = part

        @pl.when(i > 0)
        def _():
            dg_ref[...] += part

    row = lambda i: (i, 0)
    fixed = lambda i: (0, 0)
    outs, job_outs = _call(
        name, body, (s // tm,),
        [pl.BlockSpec((tm, d), row), pl.BlockSpec((tm, d), row), pl.BlockSpec((tm, 1), row),
         pl.BlockSpec((1, d), fixed), pl.BlockSpec((tm, d), row)],
        [pl.BlockSpec((tm, d), row), pl.BlockSpec((tm, d), row), pl.BlockSpec((1, d), fixed)],
        [_sds((s, d), F32), _sds((s, d), BF16), _sds((1, d), F32)], [dn, x, r, g, dh_in],
        sem=("arbitrary",), jobs=jobs)
    return _ret(outs, job_outs, jobs, single=False)


def gain_grad(name, dn_a, dn_b, x, r):
    s, d = x.shape
    tm = _tile(s, 512, 8)

    def body(a_ref, b_ref, x_ref, r_ref, dg_ref):
        i = pl.program_id(0)
        part = jnp.sum((a_ref[...] + b_ref[...]) * (x_ref[...] * r_ref[...]), axis=0, keepdims=True)

        @pl.when(i == 0)
        def _():
            dg_ref[...] = part

        @pl.when(i > 0)
        def _():
            dg_ref[...] += part

    row = lambda i: (i, 0)
    return _call(
        name, body, (s // tm,),
        [pl.BlockSpec((tm, d), row), pl.BlockSpec((tm, d), row), pl.BlockSpec((tm, d), row),
         pl.BlockSpec((tm, 1), row)],
        [pl.BlockSpec((1, d), lambda i: (0, 0))], [_sds((1, d), F32)], [dn_a, dn_b, x, r],
        sem=("arbitrary",))[0][0]


def loss_head(name, h, g, target):
    s, d = h.shape
    tm = _tile(s, 512, 8)
    nsteps = s // tm

    def body(h_ref, g_ref, t_ref, loss_ref, dh_ref, dhb_ref, dg_ref, sq_ref):
        i = pl.program_id(0)
        hv = h_ref[...]
        gv = g_ref[...]
        r = lax.rsqrt(jnp.mean(hv * hv, axis=-1, keepdims=True) + EPS)
        xh = hv * r
        err = xh * gv - t_ref[...]
        dy = err * (1.0 / d)
        dxh = dy * gv
        dh = r * (dxh - xh * jnp.mean(dxh * xh, axis=-1, keepdims=True))
        dh_ref[...] = dh
        dhb_ref[...] = dh.astype(BF16)
        dg_part = jnp.sum(dy * xh, axis=0, keepdims=True)
        sq_part = jnp.sum(err * err, axis=0, keepdims=True)

        @pl.when(i == 0)
        def _():
            dg_ref[...] = dg_part
            sq_ref[...] = sq_part

        @pl.when(i > 0)
        def _():
            dg_ref[...] += dg_part
            sq_ref[...] += sq_part

        @pl.when(i == nsteps - 1)
        def _():
            total = jnp.sum(sq_ref[...], axis=-1, keepdims=True) * (0.5 / d)
            loss_ref[...] = jnp.broadcast_to(total, loss_ref.shape)

    row = lambda i: (i, 0)
    fixed = lambda i: (0, 0)
    return _call(
        name, body, (nsteps,),
        [pl.BlockSpec((tm, d), row), pl.BlockSpec((1, d), fixed), pl.BlockSpec((tm, d), row)],
        [pl.BlockSpec((8, 128), fixed), pl.BlockSpec((tm, d), row), pl.BlockSpec((tm, d), row),
         pl.BlockSpec((1, d), fixed)],
        [_sds((8, 128), F32), _sds((s, d), F32), _sds((s, d), BF16), _sds((1, d), F32)], [h, g, target],
        scratch=[pltpu.VMEM((1, d), F32)], sem=("arbitrary",))[0]


def _mm(name, grid, in_arrays, in_specs, out_shapes, out_specs, acc_tile, dot, epilogue, jobs=()):
    nk = grid[2]
    n_in = len(in_arrays)
    n_out = len(out_shapes)

    def body(*refs):
        ins, outs = refs[:n_in], refs[n_in:n_in + n_out]
        if nk == 1:
            epilogue(dot(*ins), ins, outs)
            return
        acc = refs[n_in + n_out]
        k = pl.program_id(2)

        @pl.when(k == 0)
        def _():
            acc[...] = dot(*ins)

        @pl.when(jnp.logical_and(k > 0, k < nk - 1))
        def _():
            acc[...] += dot(*ins)

        @pl.when(k == nk - 1)
        def _():
            epilogue(acc[...] + dot(*ins), ins, outs)

    scratch = [pltpu.VMEM(acc_tile, F32)] if nk > 1 else []
    outs, job_outs = _call(name, body, grid, in_specs, out_specs, out_shapes, in_arrays, scratch=scratch,
                           sem=("parallel", "parallel", "arbitrary"), jobs=jobs)
    return _ret(outs, job_outs, jobs)


def _store(scale, dtype):
    def epilogue(acc, ins, outs):
        outs[0][...] = (acc * scale if scale != 1.0 else acc).astype(dtype)
    return epilogue


def mm_nn(name, a, w, out_dtype, tm=1024, tn=1024, tk=2048, jobs=()):
    m, kd = a.shape
    n = w.shape[1]
    tm, tn, tk = _tile(m, tm, 8), _tile(n, tn), _tile(kd, tk)
    return _mm(
        name, (n // tn, m // tm, kd // tk), [a, w],
        [pl.BlockSpec((tm, tk), lambda j, i, k: (i, k)), pl.BlockSpec((tk, tn), lambda j, i, k: (k, j))],
        [_sds((m, n), out_dtype)], [pl.BlockSpec((tm, tn), lambda j, i, k: (i, j))], (tm, tn),
        lambda a_ref, w_ref: _dot_nn(a_ref[...], w_ref[...]), _store(1.0, out_dtype), jobs)


def mm_nn_resid(name, a, w, x, scale, tm=1024, tn=1024, tk=1408, jobs=()):
    m, kd = a.shape
    n = w.shape[1]
    tm, tn, tk = _tile(m, tm, 8), _tile(n, tn), _tile(kd, tk)

    def epilogue(acc, ins, outs):
        outs[0][...] = ins[2][...] + scale * acc

    return _mm(
        name, (n // tn, m // tm, kd // tk), [a, w, x],
        [pl.BlockSpec((tm, tk), lambda j, i, k: (i, k)), pl.BlockSpec((tk, tn), lambda j, i, k: (k, j)),
         pl.BlockSpec((tm, tn), lambda j, i, k: (i, j))],
        [_sds((m, n), F32)], [pl.BlockSpec((tm, tn), lambda j, i, k: (i, j))], (tm, tn),
        lambda a_ref, w_ref, x_ref: _dot_nn(a_ref[...], w_ref[...]), epilogue, jobs)


def mm_nt(name, a, w, out_dtype, scale=1.0, tm=1024, tn=1024, tk=2048, jobs=()):
    m, kd = a.shape
    n = w.shape[0]
    tm, tn, tk = _tile(m, tm, 8), _tile(n, tn), _tile(kd, tk)
    return _mm(
        name, (n // tn, m // tm, kd // tk), [a, w],
        [pl.BlockSpec((tm, tk), lambda j, i, k: (i, k)), pl.BlockSpec((tn, tk), lambda j, i, k: (j, k))],
        [_sds((m, n), out_dtype)], [pl.BlockSpec((tm, tn), lambda j, i, k: (i, j))], (tm, tn),
        lambda a_ref, w_ref: _dot_nt(a_ref[...], w_ref[...]), _store(scale, out_dtype), jobs)


def mm_nt_pair(name, a3, w, out_dtype, tm=1024, tn=1024, tk=2816, jobs=()):
    _, m, f = a3.shape
    n = w.shape[0]
    tm, tn, tk = _tile(m, tm, 8), _tile(n, tn), _tile(f, tk)
    nkf = f // tk
    return _mm(
        name, (n // tn, m // tm, 2 * nkf), [a3, w],
        [pl.BlockSpec((None, tm, tk), lambda j, i, k: (k // nkf, i, k % nkf)),
         pl.BlockSpec((tn, tk), lambda j, i, k: (j, k))],
        [_sds((m, n), out_dtype)], [pl.BlockSpec((tm, tn), lambda j, i, k: (i, j))], (tm, tn),
        lambda a_ref, w_ref: _dot_nt(a_ref[...], w_ref[...]), _store(1.0, out_dtype), jobs)


def mm_tn(name, a, b, out_dtype, scale=1.0, tm=1024, tn=1024, tk=4096, jobs=()):
    kd, m = a.shape
    n = b.shape[1]
    tm, tn, tk = _tile(m, tm), _tile(n, tn), _tile(kd, tk, 16)
    return _mm(
        name, (n // tn, m // tm, kd // tk), [a, b],
        [pl.BlockSpec((tk, tm), lambda j, i, k: (k, i)), pl.BlockSpec((tk, tn), lambda j, i, k: (k, j))],
        [_sds((m, n), out_dtype)], [pl.BlockSpec((tm, tn), lambda j, i, k: (i, j))], (tm, tn),
        lambda a_ref, b_ref: _dot_tn(a_ref[...], b_ref[...]), _store(scale, out_dtype), jobs)


def mm_tn_pair(name, a, b3, out_dtype, tm=1024, tn=512, tk=4096, jobs=()):
    kd, m = a.shape
    f = b3.shape[2]
    tm, tn, tk = _tile(m, tm), _tile(f, tn), _tile(kd, tk, 16)
    nf = f // tn
    return _mm(
        name, (m // tm, 2 * nf, kd // tk), [a, b3],
        [pl.BlockSpec((tk, tm), lambda i, j, k: (k, i)),
         pl.BlockSpec((None, tk, tn), lambda i, j, k: (j // nf, k, j % nf))],
        [_sds((m, 2 * f), out_dtype)], [pl.BlockSpec((tm, tn), lambda i, j, k: (i, j))], (tm, tn),
        lambda a_ref, b_ref: _dot_tn(a_ref[...], b_ref[...]), _store(1.0, out_dtype), jobs)


def swiglu_fwd(name, n, w_in, tm=1024, tn=512, jobs=()):
    s, d = n.shape
    f = w_in.shape[1] // 2
    tm, tn = _tile(s, tm, 8), _tile(f, tn)
    nf = f // tn

    def body(n_ref, wg_ref, wu_ref, gu_ref, a_ref):
        nv = n_ref[...]
        g = _dot_nn(nv, wg_ref[...])
        u = _dot_nn(nv, wu_ref[...])
        gu_ref[0] = g.astype(BF16)
        gu_ref[1] = u.astype(BF16)
        a_ref[...] = (g * jax.nn.sigmoid(g) * u).astype(BF16)

    outs, job_outs = _call(
        name, body, (nf, s // tm),
        [pl.BlockSpec((tm, d), lambda j, i: (i, 0)), pl.BlockSpec((d, tn), lambda j, i: (0, j)),
         pl.BlockSpec((d, tn), lambda j, i: (0, j + nf))],
        [pl.BlockSpec((2, tm, tn), lambda j, i: (0, i, j)), pl.BlockSpec((tm, tn), lambda j, i: (i, j))],
        [_sds((2, s, f), BF16), _sds((s, f), BF16)], [n, w_in, w_in], sem=("parallel", "parallel"), jobs=jobs)
    return _ret(outs, job_outs, jobs, single=False)


def swiglu_bwd(name, dh, w_out, gu, scale, tm=1024, tn=512, jobs=()):
    s, d = dh.shape
    f = w_out.shape[0]
    tm, tn = _tile(s, tm, 8), _tile(f, tn)

    sub = _tile(tm, 256, 8)

    def body(dh_ref, w_ref, gu_ref, out_ref):
        for lo in range(0, tm, sub):
            rows = slice(lo, lo + sub)
            da = _dot_nt(dh_ref[rows, :], w_ref[...]) * scale
            g = gu_ref[0, rows, :].astype(F32)
            u = gu_ref[1, rows, :].astype(F32)
            sg = jax.nn.sigmoid(g)
            out_ref[0, rows, :] = (da * u * (sg * (1.0 + g * (1.0 - sg)))).astype(BF16)
            out_ref[1, rows, :] = (da * (g * sg)).astype(BF16)

    outs, job_outs = _call(
        name, body, (f // tn, s // tm),
        [pl.BlockSpec((tm, d), lambda j, i: (i, 0)), pl.BlockSpec((tn, d), lambda j, i: (j, 0)),
         pl.BlockSpec((2, tm, tn), lambda j, i: (0, i, j))],
        [pl.BlockSpec((2, tm, tn), lambda j, i: (0, i, j))],
        [_sds((2, s, f), BF16)], [dh, w_out, gu], sem=("parallel", "parallel"), jobs=jobs)
    return _ret(outs, job_outs, jobs)


HALO = 16


def _conv_inputs(z_ref, hgc_ref, hhc_ref, i, cw, tm):
    gc = z_ref[:, cw:2 * cw].astype(F32)
    hc = z_ref[:, 2 * cw:3 * cw].astype(F32)
    cin = gc * hc
    halo = hgc_ref[...].astype(F32) * hhc_ref[...].astype(F32) * (i > 0).astype(F32)
    row = lax.broadcasted_iota(jnp.int32, (tm, cw), 0)
    x1 = jnp.where(row == 0, halo[HALO - 1:HALO], pltpu.roll(cin, 1, 0))
    x2 = jnp.where(row == 0, halo[HALO - 2:HALO - 1], jnp.where(row == 1, halo[HALO - 1:HALO], pltpu.roll(cin, 2, 0)))
    return gc, hc, cin, x1, x2


def _tril(w):
    r = lax.broadcasted_iota(jnp.int32, w.shape, 0)
    c = lax.broadcasted_iota(jnp.int32, w.shape, 1)
    return jnp.where(r >= c, w, jnp.zeros_like(w))


def mixer_fwd(name, z, conv_w, conv_b, g_v, w_s, b_t, tm=256, jobs=()):
    s, zc = z.shape
    cw = conv_w.shape[1]
    gw = g_v.shape[1]
    heads = gw // GROUP
    tm = _tile(s, tm)
    hb = tm // HALO

    def body(z_ref, hgc_ref, hhc_ref, cw_ref, cb_ref, gv_ref, ws_ref, bt_ref, y_ref):
        i = pl.program_id(0)
        _, _, cin, x1, x2 = _conv_inputs(z_ref, hgc_ref, hhc_ref, i, cw, tm)
        cv = cb_ref[...] + cw_ref[2:3, :] * cin + cw_ref[1:2, :] * x1 + cw_ref[0:1, :] * x2
        y_ref[:, 0:cw] = (z_ref[:, 0:cw].astype(F32) * cv).astype(BF16)
        for h in range(heads):
            lo = h * GROUP
            vh = z_ref[:, 3 * cw + gw + lo:3 * cw + gw + lo + GROUP].astype(F32)
            rv = lax.rsqrt(jnp.mean(vh * vh, axis=-1, keepdims=True) + EPS)
            vn = (vh * rv * gv_ref[:, lo:lo + GROUP]).astype(BF16)
            w = _tril(ws_ref[h]).astype(BF16)
            for n in range(tm // GROUP):
                rows = slice(n * GROUP, (n + 1) * GROUP)
                sg = _dot_nn(w, vn[rows]) + bt_ref[:, h:h + 1]
                u = z_ref[rows, 3 * cw + lo:3 * cw + lo + GROUP].astype(F32)
                y_ref[rows, cw + lo:cw + lo + GROUP] = (u * sg).astype(BF16)

    fixed2 = lambda i: (0, 0)
    outs, job_outs = _call(
        name, body, (s // tm,),
        [pl.BlockSpec((tm, zc), lambda i: (i, 0)),
         pl.BlockSpec((HALO, cw), lambda i: (jnp.maximum(i * hb - 1, 0), 1)),
         pl.BlockSpec((HALO, cw), lambda i: (jnp.maximum(i * hb - 1, 0), 2)),
         pl.BlockSpec(conv_w.shape, fixed2), pl.BlockSpec(conv_b.shape, fixed2),
         pl.BlockSpec(g_v.shape, fixed2), pl.BlockSpec(w_s.shape, lambda i: (0, 0, 0)),
         pl.BlockSpec(b_t.shape, fixed2)],
        [pl.BlockSpec((tm, cw + gw), lambda i: (i, 0))], [_sds((s, cw + gw), BF16)],
        [z, z, z, conv_w, conv_b, g_v, w_s, b_t], sem=("arbitrary",), jobs=jobs)
    return _ret(outs, job_outs, jobs)


def mixer_bwd(name, z, dy, conv_w, conv_b, g_v, w_s, b_t, tm=256, jobs=()):
    s, zc = z.shape
    cw = conv_w.shape[1]
    gw = g_v.shape[1]
    heads = gw // GROUP
    tm = _tile(s, tm)
    hb = tm // HALO
    nsteps = s // tm
    last_halo = s // HALO - 1

    def body(z_ref, hgc_ref, hhc_ref, ngb_ref, dy_ref, ndy_ref, cw_ref, cb_ref, gv_ref, ws_ref, bt_ref,
             dz_ref, sm_ref, dws_ref, dbt_ref, dsg_ref):
        i = pl.program_id(0)

        @pl.when(i == 0)
        def _():
            sm_ref[...] = jnp.zeros_like(sm_ref)
            dws_ref[...] = jnp.zeros_like(dws_ref)
            dsg_ref[...] = jnp.zeros_like(dsg_ref)

        gc, hc, cin, x1, x2 = _conv_inputs(z_ref, hgc_ref, hhc_ref, i, cw, tm)
        w0, w1, w2 = cw_ref[0:1, :], cw_ref[1:2, :], cw_ref[2:3, :]
        cv = cb_ref[...] + w2 * cin + w1 * x1 + w0 * x2
        gb = z_ref[:, 0:cw].astype(F32)
        dyc = dy_ref[:, 0:cw].astype(F32)
        dz_ref[:, 0:cw] = (dyc * cv).astype(BF16)
        dcv = dyc * gb
        nxt = ndy_ref[...].astype(F32) * ngb_ref[...].astype(F32) * (i < nsteps - 1).astype(F32)
        row = lax.broadcasted_iota(jnp.int32, (tm, cw), 0)
        d1 = jnp.where(row == tm - 1, nxt[0:1], pltpu.roll(dcv, tm - 1, 0))
        d2 = jnp.where(row == tm - 1, nxt[1:2], jnp.where(row == tm - 2, nxt[0:1], pltpu.roll(dcv, tm - 2, 0)))
        dcin = w2 * dcv + w1 * d1 + w0 * d2
        dz_ref[:, cw:2 * cw] = (dcin * hc).astype(BF16)
        dz_ref[:, 2 * cw:3 * cw] = (dcin * gc).astype(BF16)
        sm_ref[0:1, :] += jnp.sum(dcv * x2, axis=0, keepdims=True)
        sm_ref[1:2, :] += jnp.sum(dcv * x1, axis=0, keepdims=True)
        sm_ref[2:3, :] += jnp.sum(dcv * cin, axis=0, keepdims=True)
        sm_ref[3:4, :] += jnp.sum(dcv, axis=0, keepdims=True)

        for h in range(heads):
            lo = h * GROUP
            vcol = slice(3 * cw + gw + lo, 3 * cw + gw + lo + GROUP)
            ucol = slice(3 * cw + lo, 3 * cw + lo + GROUP)
            vh = z_ref[:, vcol].astype(F32)
            rv = lax.rsqrt(jnp.mean(vh * vh, axis=-1, keepdims=True) + EPS)
            xh = vh * rv
            gvh = gv_ref[:, lo:lo + GROUP]
            vn = (xh * gvh).astype(BF16)
            w = _tril(ws_ref[h]).astype(BF16)
            dgv = jnp.zeros((1, GROUP), F32)
            for n in range(tm // GROUP):
                rows = slice(n * GROUP, (n + 1) * GROUP)
                sg = _dot_nn(w, vn[rows]) + bt_ref[:, h:h + 1]
                dyg = dy_ref[rows, cw + lo:cw + lo + GROUP].astype(F32)
                dsg = dyg * z_ref[rows, ucol].astype(F32)
                dz_ref[rows, ucol] = (dyg * sg).astype(BF16)
                dsgb = dsg.astype(BF16)
                dvn = _dot_tn(w, dsgb)
                dws_ref[h] += _dot_nt(dsgb, vn[rows])
                dsg_ref[:, lo:lo + GROUP] += dsg
                xhc = xh[rows]
                dgv = dgv + jnp.sum(dvn * xhc, axis=0, keepdims=True)
                dxh = dvn * gvh
                dv = rv[rows] * (dxh - xhc * jnp.mean(dxh * xhc, axis=-1, keepdims=True))
                dz_ref[rows, vcol] = dv.astype(BF16)
            sm_ref[4:5, lo:lo + GROUP] += dgv

        @pl.when(i == nsteps - 1)
        def _():
            for h in range(heads):
                dws_ref[h] = _tril(dws_ref[h])
                dbt_ref[:, h:h + 1] = jnp.sum(dsg_ref[:, h * GROUP:(h + 1) * GROUP], axis=-1, keepdims=True)

    fixed2 = lambda i: (0, 0)
    fixed3 = lambda i: (0, 0, 0)
    prev = lambda col: (lambda i: (jnp.maximum(i * hb - 1, 0), col))
    nxt_blk = lambda i: (jnp.minimum((i + 1) * hb, last_halo), 0)
    outs, job_outs = _call(
        name, body, (nsteps,),
        [pl.BlockSpec((tm, zc), lambda i: (i, 0)),
         pl.BlockSpec((HALO, cw), prev(1)), pl.BlockSpec((HALO, cw), prev(2)),
         pl.BlockSpec((HALO, cw), nxt_blk),
         pl.BlockSpec((tm, cw + gw), lambda i: (i, 0)), pl.BlockSpec((HALO, cw), nxt_blk),
         pl.BlockSpec(conv_w.shape, fixed2), pl.BlockSpec(conv_b.shape, fixed2),
         pl.BlockSpec(g_v.shape, fixed2), pl.BlockSpec(w_s.shape, fixed3), pl.BlockSpec(b_t.shape, fixed2)],
        [pl.BlockSpec((tm, zc), lambda i: (i, 0)), pl.BlockSpec((8, cw), fixed2),
         pl.BlockSpec(w_s.shape, fixed3), pl.BlockSpec(b_t.shape, fixed2)],
        [_sds((s, zc), BF16), _sds((8, cw), F32), _sds(w_s.shape, F32), _sds(b_t.shape, F32)],
        [z, z, z, z, dy, dy, conv_w, conv_b, g_v, w_s, b_t],
        scratch=[pltpu.VMEM((GROUP, gw), F32)], sem=("arbitrary",), jobs=jobs)
    return _ret(outs, job_outs, jobs, single=False)


def _softmax_rows(sc):
    e = jnp.exp(sc - jnp.max(sc, axis=-1, keepdims=True))
    return e / jnp.sum(e, axis=-1, keepdims=True)


def attn_fwd(name, q, k, v, tm=512):
    s, d = q.shape
    m = k.shape[0]
    hd = d // XA_HEADS
    scale = hd ** -0.5
    tm = _tile(s, tm, 8)

    def body(q_ref, k_ref, v_ref, o_ref):
        for h in range(XA_HEADS):
            cols = slice(h * hd, (h + 1) * hd)
            p = _softmax_rows(_dot_nt(q_ref[:, cols], k_ref[:, cols]) * scale)
            o_ref[:, cols] = _dot_nn(p.astype(BF16), v_ref[:, cols]).astype(BF16)

    return _call(
        name, body, (s // tm,),
        [pl.BlockSpec((tm, d), lambda i: (i, 0)), pl.BlockSpec((m, d), lambda i: (0, 0)),
         pl.BlockSpec((m, d), lambda i: (0, 0))],
        [pl.BlockSpec((tm, d), lambda i: (i, 0))], [_sds((s, d), BF16)], [q, k, v], sem=("arbitrary",))[0][0]


def attn_bwd(name, q, k, v, do, tm=512):
    s, d = q.shape
    m = k.shape[0]
    hd = d // XA_HEADS
    scale = hd ** -0.5
    tm = _tile(s, tm, 8)

    def body(q_ref, k_ref, v_ref, do_ref, dq_ref, dk_ref, dv_ref):
        i = pl.program_id(0)

        @pl.when(i == 0)
        def _():
            dk_ref[...] = jnp.zeros_like(dk_ref)
            dv_ref[...] = jnp.zeros_like(dv_ref)

        for h in range(XA_HEADS):
            cols = slice(h * hd, (h + 1) * hd)
            qh = q_ref[:, cols]
            doh = do_ref[:, cols]
            p = _softmax_rows(_dot_nt(qh, k_ref[:, cols]) * scale)
            dp = _dot_nt(doh, v_ref[:, cols])
            ds = (p * (dp - jnp.sum(dp * p, axis=-1, keepdims=True)) * scale).astype(BF16)
            dq_ref[:, cols] = _dot_nn(ds, k_ref[:, cols]).astype(BF16)
            dk_ref[:, cols] += _dot_tn(ds, qh)
            dv_ref[:, cols] += _dot_tn(p.astype(BF16), doh)

    row = lambda i: (i, 0)
    fixed = lambda i: (0, 0)
    return _call(
        name, body, (s // tm,),
        [pl.BlockSpec((tm, d), row), pl.BlockSpec((m, d), fixed), pl.BlockSpec((m, d), fixed),
         pl.BlockSpec((tm, d), row)],
        [pl.BlockSpec((tm, d), row), pl.BlockSpec((m, d), fixed), pl.BlockSpec((m, d), fixed)],
        [_sds((s, d), BF16), _sds((m, d), F32), _sds((m, d), F32)], [q, k, v, do], sem=("arbitrary",))[0]


def _grid2(rows, cols, row_mult):
    tr, tc = _tile(rows, 512, row_mult), _tile(cols, 2048)
    return tr, tc, rows // tr, cols // tc


def cast_place(name, block, axis, place):
    r, c = block.shape
    tr, tc, nbr, nbc = _grid2(r, c, 16)
    if axis == 1:
        dst = lambda i, j, p: (i, j + p[0] * nbc)
    else:
        dst = lambda i, j, p: (i + p[0] * nbr, j)

    def body(p_ref, w_ref, out_ref):
        out_ref[...] = w_ref[...].astype(BF16)

    return pl.pallas_call(
        body, name=name,
        grid_spec=pltpu.PrefetchScalarGridSpec(
            num_scalar_prefetch=1, grid=(nbr, nbc),
            in_specs=[pl.BlockSpec((tr, tc), lambda i, j, p: (i, j))],
            out_specs=pl.BlockSpec((tr, tc), dst)),
        out_shape=_sds(_full_shape(block.shape, axis), BF16),
        compiler_params=_params(("parallel", "parallel")),
    )(place, block)


def pair_add(name, grad, peer, axis, place):
    hr, hc = peer.shape
    tr, tc, nbr, nbc = _grid2(hr, hc, 16)
    if axis == 1:
        mine = lambda i, j, p: (i + p[1] * nbr, j)
    else:
        mine = lambda i, j, p: (i, j + p[1] * nbc)
    same = lambda i, j, p: (i, j)

    def body(p_ref, g_ref, q_ref, out_ref):
        out_ref[...] = (g_ref[...].astype(F32) + q_ref[...].astype(F32)).astype(BF16)

    return pl.pallas_call(
        body, name=name,
        grid_spec=pltpu.PrefetchScalarGridSpec(
            num_scalar_prefetch=1, grid=(nbr, nbc),
            in_specs=[pl.BlockSpec((tr, tc), mine), pl.BlockSpec((tr, tc), same)],
            out_specs=pl.BlockSpec((tr, tc), same)),
        out_shape=_sds((hr, hc), BF16),
        compiler_params=_params(("parallel", "parallel")),
    )(place, grad, peer)


def cross_sum(name, part, land, axis, shape, place):
    _, sr, sc = land.shape
    tr, tc, nbr, nbc = _grid2(sr, sc, 16)
    if axis == 1:
        own = lambda i, j, p: (i, j + p[0] * nbc)
        dst = lambda i, j, p: (i + p[1] * nbr, j)
    else:
        own = lambda i, j, p: (i + p[0] * nbr, j)
        dst = lambda i, j, p: (i, j + p[1] * nbc)

    def body(p_ref, own_ref, land_ref, out_ref):
        out_ref[...] = ((own_ref[...].astype(F32) + land_ref[0].astype(F32))
                        + (land_ref[1].astype(F32) + land_ref[2].astype(F32)))

    return pl.pallas_call(
        body, name=name,
        grid_spec=pltpu.PrefetchScalarGridSpec(
            num_scalar_prefetch=1, grid=(nbr, nbc),
            in_specs=[pl.BlockSpec((tr, tc), own), pl.BlockSpec((3, tr, tc), lambda i, j, p: (0, i, j))],
            out_specs=pl.BlockSpec((tr, tc), dst)),
        out_shape=_sds(_block(shape, axis), F32),
        compiler_params=_params(("parallel", "parallel")),
    )(place, part, land)


def _adam_math(w, g, m, v):
    m = ADAM_B1 * m + (1.0 - ADAM_B1) * g
    v = ADAM_B2 * v + (1.0 - ADAM_B2) * (g * g)
    m_hat = m / (1.0 - ADAM_B1 ** ADAM_STEP)
    v_hat = v / (1.0 - ADAM_B2 ** ADAM_STEP)
    delta = -ADAM_LR * (m_hat / (jnp.sqrt(v_hat) + ADAM_EPS) + ADAM_WD * w)
    return delta, m, v


def adamw(name, w, g, m, v):
    r, c = w.shape
    tr, tc = _tile(r, 256, 8), _tile(c, 1408)

    def body(w_ref, g_ref, m_ref, v_ref, g_out, d_out, m_out, v_out):
        d, mm, vv = _adam_math(w_ref[...], g_ref[...], m_ref[...], v_ref[...])
        g_out[...] = g_ref[...]
        d_out[...] = d
        m_out[...] = mm
        v_out[...] = vv

    spec = pl.BlockSpec((tr, tc), lambda i, j: (i, j))
    return _call(name, body, (r // tr, c // tc), [spec] * 4, [spec] * 4, [_sds((r, c), F32)] * 4, [w, g, m, v],
                 sem=("parallel", "parallel"))[0]


def small_sum(name, stacks):
    def body(*refs):
        for s_ref, out_ref in zip(refs[:len(stacks)], refs[len(stacks):]):
            acc = s_ref[0]
            for d in range(1, s_ref.shape[0]):
                acc = acc + s_ref[d]
            out_ref[...] = acc

    return pl.pallas_call(body, name=name, out_shape=[_sds(s.shape[1:], F32) for s in stacks])(*stacks)


WEIGHTS = ["g_ffn1", "w_ffn1_in", "w_ffn1_out", "g_mix", "w_mix_in", "conv_w", "conv_b", "g_gm_v", "w_spatial",
           "b_spatial", "w_mix_out", "g_xattn", "g_mem", "w_xq", "w_xk", "w_xv", "w_xo", "g_ffn2", "w_ffn2_in",
           "w_ffn2_out", "g_final"]
BIG = {"w_ffn1_in": 1, "w_ffn1_out": 0, "w_mix_in": 1, "w_mix_out": 0, "w_xq": 0, "w_xk": 0, "w_xv": 0, "w_xo": 0,
       "w_ffn2_in": 1, "w_ffn2_out": 0}
SMALL = [n for n in WEIGHTS if n not in BIG]
LATE_SMALL = ["g_ffn1"]
EARLY_SMALL = [n for n in SMALL if n not in LATE_SMALL]


def _pack(arrays):
    flat = jnp.concatenate([a.reshape(-1) for a in arrays])
    rows = -(-flat.shape[0] // 1024) * 8
    return jnp.pad(flat, (0, rows * 128 - flat.shape[0])).reshape(rows, 128)


def _unpack(buf, shapes):
    flat = buf.reshape(-1)
    out, pos = [], 0
    for shp in shapes:
        n = math.prod(shp)
        out.append(flat[pos:pos + n].reshape(shp))
        pos += n
    return out


def kernel(x, mem, g_ffn1, w_ffn1_in, w_ffn1_out, g_mix, w_mix_in, conv_w, conv_b, g_gm_v, w_spatial, b_spatial, w_mix_out, g_xattn, g_mem, w_xq, w_xk, w_xv, w_xo, g_ffn2, w_ffn2_in, w_ffn2_out, g_final, loss_target, m_g_ffn1, m_w_ffn1_in, m_w_ffn1_out, m_g_mix, m_w_mix_in, m_conv_w, m_conv_b, m_g_gm_v, m_w_spatial, m_b_spatial, m_w_mix_out, m_g_xattn, m_g_mem, m_w_xq, m_w_xk, m_w_xv, m_w_xo, m_g_ffn2, m_w_ffn2_in, m_w_ffn2_out, m_g_final, v_g_ffn1, v_w_ffn1_in, v_w_ffn1_out, v_g_mix, v_w_mix_in, v_conv_w, v_conv_b, v_g_gm_v, v_w_spatial, v_b_spatial, v_w_mix_out, v_g_xattn, v_g_mem, v_w_xq, v_w_xk, v_w_xv, v_w_xo, v_g_ffn2, v_w_ffn2_in, v_w_ffn2_out, v_g_final):
    given = dict(locals())
    wts = {n: given[n] for n in WEIGHTS}
    mom = {n: given["m_" + n] for n in WEIGHTS}
    var = {n: given["v_" + n] for n in WEIGHTS}

    xi, yi, ci = lax.axis_index("x"), lax.axis_index("y"), lax.axis_index("c")
    blk = 2 * xi + yi
    place = jnp.stack([blk, ci]).astype(jnp.int32)

    x2, mem2, tgt = x[0], mem[0], loss_target[0]
    own = {n: cast_place("cast_" + n, wts[n][0], BIG[n], place) for n in BIG}
    shape = {n: own[n].shape for n in BIG}
    w_s, b_t = w_spatial[0], b_spatial[0].T
    gf = g_final[None]

    def gather(*names):
        return gather_job([(own[n], BIG[n], WHOLE) for n in names])

    full = {}

    (full["w_ffn1_in"],), (conv_taps,) = comm_only(
        "gather_first", [gather("w_ffn1_in"), columns_job(jnp.pad(conv_w[0], ((0, 8 - CONV_K), (0, 0))))])
    n1, r1 = rmsnorm_fwd("norm1", x2, g_ffn1)
    (gu1, a1), (got,) = swiglu_fwd("ffn1_in", n1, full["w_ffn1_in"], jobs=[gather("w_ffn1_out", "w_mix_in")])
    full["w_ffn1_out"], full["w_mix_in"] = got
    h1, (got,) = mm_nn_resid("ffn1_out", a1, full["w_ffn1_out"], x2, 0.5, tm=512, tk=5632,
                             jobs=[gather("w_mix_out", "w_xq", "w_xk")])
    full["w_mix_out"], full["w_xq"], full["w_xk"] = got
    n2, r2 = rmsnorm_fwd("norm2", h1, g_mix)
    z, (got,) = mm_nn("mix_in", n2, full["w_mix_in"], BF16, jobs=[gather("w_xv", "w_xo")])
    full["w_xv"], full["w_xo"] = got

    pieces = 4
    def ffn2_piece(p, prev):
        return [gather_job([(prev, 1, (p, 1, pieces))])]

    ycat, ((w2in,),) = mixer_fwd("mixer", z, conv_taps, conv_b, g_gm_v, w_s, b_t,
                                 jobs=ffn2_piece(0, own["w_ffn2_in"]))
    h2, ((w2in,),) = mm_nn_resid("mix_out", ycat, full["w_mix_out"], h1, 1.0, tk=2048, jobs=ffn2_piece(1, w2in))
    n3, r3 = rmsnorm_fwd("norm3", h2, g_xattn)
    mn, rm = rmsnorm_fwd("norm_mem", mem2, g_mem)
    q, ((w2in,),) = mm_nn("xq", n3, full["w_xq"], BF16, jobs=ffn2_piece(2, w2in))
    k = mm_nn("xk", mn, full["w_xk"], BF16)
    v = mm_nn("xv", mn, full["w_xv"], BF16)
    o = attn_fwd("attn", q, k, v)
    h3, ((w2in,),) = mm_nn_resid("xo", o, full["w_xo"], h2, 1.0, tk=2048, jobs=ffn2_piece(3, w2in))
    full["w_ffn2_in"] = w2in
    n4, r4 = rmsnorm_fwd("norm4", h3, g_ffn2)
    (gu2, a2), ((full["w_ffn2_out"],),) = swiglu_fwd("ffn2_in", n4, full["w_ffn2_in"], jobs=[gather("w_ffn2_out")])
    h4 = mm_nn_resid("ffn2_out", a2, full["w_ffn2_out"], h3, 0.5, tm=512, tk=5632)
    loss_blk, dh4, dh4b, dg_final = loss_head("loss_head", h4, gf, tgt)

    dw, peer, part, land, half, grads = {}, {}, {}, {}, {}, {}

    def send_pair(*names):
        return pair_job([dw[n] for n in names], [BIG[n] for n in names])

    def take_pair(names, got):
        for n, p in zip(names, got):
            part[n] = pair_add("pair_add_" + n, dw[n], p, BIG[n], place)

    def send_cross(*names, sub=WHOLE):
        return cross_job([(part[n], BIG[n], shape[n], land.get(n), sub) for n in names])

    def take_cross(names, got, last=True):
        for n, l in zip(names, got):
            land[n] = l
            if last:
                half[n] = cross_sum("cross_sum_" + n, part[n], l, BIG[n], shape[n], place)

    def send_final(*names):
        return final_job([half[n] for n in names], [BIG[n] for n in names], [shape[n] for n in names])

    delta, new_m, new_v = {}, {}, {}

    def take_final(names, got):
        for n, g in zip(names, got):
            grads[n], delta[n], new_m[n], new_v[n] = adamw("adamw_" + n, wts[n][0], g, mom[n][0], var[n][0])

    dw["w_ffn2_out"] = mm_tn("ffn2_dwout", a2, dh4b, BF16, scale=0.5)
    dgu2, (got,) = swiglu_bwd("ffn2_dact", dh4b, full["w_ffn2_out"], gu2, 0.5, jobs=[send_pair("w_ffn2_out")])
    take_pair(["w_ffn2_out"], got)
    dw["w_ffn2_in"], (got,) = mm_tn_pair("ffn2_dwin", n4, dgu2, BF16, jobs=[send_cross("w_ffn2_out")])
    take_cross(["w_ffn2_out"], got)
    dn4, (got_f, got_p) = mm_nt_pair("ffn2_dn", dgu2, full["w_ffn2_in"], F32,
                                     jobs=[send_final("w_ffn2_out"), send_pair("w_ffn2_in")])
    take_final(["w_ffn2_out"], got_f)
    take_pair(["w_ffn2_in"], got_p)
    dh3, dh3b, dg_ffn2 = rmsnorm_bwd("norm4_bwd", dn4, h3, r4, g_ffn2, dh4)

    dw["w_xo"] = mm_tn("xo_dw", o, dh3b, BF16)
    do = mm_nt("xo_dx", dh3b, full["w_xo"], BF16)
    dq, dk, dv = attn_bwd("attn_bwd", q, k, v, do)
    dkb, dvb = dk.astype(BF16), dv.astype(BF16)
    dw["w_xq"] = mm_tn("xq_dw", n3, dq, BF16)
    dn3 = mm_nt("xq_dx", dq, full["w_xq"], F32)
    dh2, dh2b, dg_xattn = rmsnorm_bwd("norm3_bwd", dn3, h2, r3, g_xattn, dh3)
    dw["w_xk"] = mm_tn("xk_dw", mn, dkb, BF16)
    dw["w_xv"] = mm_tn("xv_dw", mn, dvb, BF16)
    dmn_k = mm_nt("xk_dx", dkb, full["w_xk"], F32)
    dmn_v = mm_nt("xv_dx", dvb, full["w_xv"], F32)
    dg_mem = gain_grad("norm_mem_bwd", dmn_k, dmn_v, mem2, rm)

    dw["w_mix_out"] = mm_tn("mix_out_dw", ycat, dh2b, BF16)
    dycat = mm_nt("mix_out_dx", dh2b, full["w_mix_out"], BF16)
    attn_names = ["w_xo", "w_xq", "w_xk", "w_xv", "w_mix_out"]
    (dz, dsmall, dws, dbt), (got_c, got_p) = mixer_bwd(
        "mixer_bwd", z, dycat, conv_taps, conv_b, g_gm_v, w_s, b_t,
        jobs=[send_cross("w_ffn2_in", sub=(0, 2, 8)), send_pair(*attn_names)])
    take_cross(["w_ffn2_in"], got_c, last=False)
    take_pair(attn_names, got_p)
    dw["w_mix_in"], (got_c,) = mm_tn("mix_in_dw", n2, dz, BF16, jobs=[send_cross("w_ffn2_in", sub=(2, 3, 8))])
    take_cross(["w_ffn2_in"], got_c, last=False)
    dn2, (got_c, got_p) = mm_nt("mix_in_dx", dz, full["w_mix_in"], F32, tk=2560,
                                jobs=[send_cross("w_ffn2_in", sub=(5, 3, 8)), send_pair("w_mix_in")])
    take_cross(["w_ffn2_in"], got_c)
    take_pair(["w_mix_in"], got_p)
    dh1, dh1b, dg_mix = rmsnorm_bwd("norm2_bwd", dn2, h1, r2, g_mix, dh2)

    xa_names = ["w_xo", "w_xq", "w_xk", "w_xv"]
    dgu1, (got_c, got_f) = swiglu_bwd("ffn1_dact", dh1b, full["w_ffn1_out"], gu1, 0.5,
                                      jobs=[send_cross(*xa_names), send_final("w_ffn2_in")])
    take_cross(xa_names, got_c)
    take_final(["w_ffn2_in"], got_f)
    mix_names = ["w_mix_out", "w_mix_in"]
    dw["w_ffn1_in"], (got_c, got_f) = mm_tn_pair("ffn1_dwin", n1, dgu1, BF16,
                                                 jobs=[send_cross(*mix_names), send_final(*xa_names)])
    take_cross(mix_names, got_c)
    take_final(xa_names, got_f)
    dw["w_ffn1_out"], (got_p, got_f) = mm_tn("ffn1_dwout", a1, dh1b, BF16, scale=0.5,
                                             jobs=[send_pair("w_ffn1_in"), send_final(*mix_names)])
    take_pair(["w_ffn1_in"], got_p)
    take_final(mix_names, got_f)

    early = {"g_mix": dg_mix, "conv_w": dsmall[0:CONV_K], "conv_b": dsmall[3:4], "g_gm_v": dsmall[4:5],
             "w_spatial": dws, "b_spatial": dbt.T, "g_xattn": dg_xattn, "g_mem": dg_mem, "g_ffn2": dg_ffn2,
             "g_final": dg_final}
    dn1, (got_c, got_p, (early_all,)) = mm_nt_pair(
        "ffn1_dn", dgu1, full["w_ffn1_in"], F32,
        jobs=[send_cross("w_ffn1_in"), send_pair("w_ffn1_out"), stack_job(_pack([early[n] for n in EARLY_SMALL]))])
    take_cross(["w_ffn1_in"], got_c)
    take_pair(["w_ffn1_out"], got_p)
    (dx, _, dg_ffn1), (got_c, got_f) = rmsnorm_bwd("norm1_bwd", dn1, x2, r1, g_ffn1, dh1,
                                                   jobs=[send_cross("w_ffn1_out"), send_final("w_ffn1_in")])
    take_cross(["w_ffn1_out"], got_c)
    take_final(["w_ffn1_in"], got_f)
    got_f, (late_all,) = comm_only("tail_exchange", [send_final("w_ffn1_out"), stack_job(_pack([dg_ffn1]))])
    take_final(["w_ffn1_out"], got_f)

    early_sum, late_sum = small_sum("small_sum", [early_all, late_all])
    for n, g in zip(EARLY_SMALL, _unpack(early_sum, [early[n].shape for n in EARLY_SMALL])):
        grads[n] = g
    grads["g_ffn1"] = _unpack(late_sum, [dg_ffn1.shape])[0]
    taps_cols = conv_w.shape[2]
    grads["conv_w"] = lax.dynamic_slice_in_dim(grads["conv_w"], blk * taps_cols, taps_cols, axis=1)
    packed = [_pack([src[n] for n in SMALL]) for src in (wts, grads, mom, var)]
    own_shapes = [wts[n].shape for n in SMALL]
    for dst, buf in zip((delta, new_m, new_v), adamw("adamw_small", *packed)[1:]):
        for n, a in zip(SMALL, _unpack(buf, own_shapes)):
            dst[n] = a

    loss = lax.psum(loss_blk[0, 0], ("x", "y", "c"))
    outs = [loss, dx[None]]
    for group in (grads, delta, new_m, new_v):
        outs += [group[n].reshape(wts[n].shape) for n in WEIGHTS]
    return tuple(outs)
```

```python
import math

import jax
import jax.numpy as jnp
from jax import lax
from jax.experimental import pallas as pl
from jax.experimental.pallas import tpu as pltpu

F32 = jnp.float32
BF16 = jnp.bfloat16
EPS = 1e-6
GROUP = 128
XA_HEADS = 4
CONV_K = 3
N_CHIPS = 4
VMEM_LIMIT_BYTES = 56 * 1024 * 1024

ADAM_LR = 0.001
ADAM_B1 = 0.9
ADAM_B2 = 0.999
ADAM_EPS = 1e-08
ADAM_WD = 0.01
ADAM_STEP = 10

MESH = pl.DeviceIdType.MESH
ANY = pl.BlockSpec(memory_space=pl.ANY)


def _tile(dim, pref, mult=128):
    if dim <= pref:
        return dim
    t = (pref // mult) * mult
    while t >= mult:
        if dim % t == 0:
            return t
        t -= mult
    raise ValueError(f"no tile for {dim} under {pref}")


def _params(sem):
    return pltpu.CompilerParams(dimension_semantics=sem, vmem_limit_bytes=VMEM_LIMIT_BYTES)


def _sds(shape, dtype):
    return jax.ShapeDtypeStruct(shape, dtype)


def _dot_nn(a, b):
    return jnp.dot(a, b, preferred_element_type=F32)


def _dot_nt(a, b):
    return lax.dot_general(a, b, (((1,), (1,)), ((), ())), preferred_element_type=F32)


def _dot_tn(a, b):
    return lax.dot_general(a, b, (((0,), (0,)), ((), ())), preferred_element_type=F32)


class Job:
    def __init__(self, inputs, out_shapes, aliases, sems, start, middle, finish):
        self.inputs, self.out_shapes, self.aliases, self.sems = inputs, out_shapes, aliases, sems
        self.start, self.middle, self.finish = start, middle, finish


def _place():
    x, y, c = lax.axis_index("x"), lax.axis_index("y"), lax.axis_index("c")
    chips = [(1 - x, y), (x, 1 - y), (1 - x, 1 - y)]
    return x, y, c, chips


def _ds(start, size, lane):
    if not isinstance(start, int):
        start = pl.multiple_of(start, 128 if lane else 16)
    return pl.ds(start, size)


WHOLE = (0, 1, 1)


def _window(ref, axis, shape, blk=None, half=None, sub=WHOLE):
    n = shape[axis] // N_CHIPS
    hs = shape[1 - axis] // 2
    idx = [slice(None), slice(None)]
    if blk is not None:
        idx[axis] = _ds(blk * n, n, axis == 1)
    first, count, pieces = sub
    ext = hs // pieces
    if half is not None:
        idx[1 - axis] = _ds(half * hs + first * ext, count * ext, axis == 0)
    elif pieces > 1:
        idx[1 - axis] = _ds(first * ext, count * ext, axis == 0)
    return ref.at[tuple(idx)]


def _remote(src, dst, send_sem, recv_sem, dev):
    return pltpu.make_async_remote_copy(src_ref=src, dst_ref=dst, send_sem=send_sem, recv_sem=recv_sem,
                                        device_id=dev, device_id_type=MESH)


def _full_shape(block_shape, axis):
    out = list(block_shape)
    out[axis] *= N_CHIPS
    return tuple(out)


def _half_all(shape, axis):
    out = list(shape)
    out[1 - axis] //= 2
    return tuple(out)


def _block(shape, axis):
    out = list(shape)
    out[axis] //= N_CHIPS
    return tuple(out)


def _half_block(shape, axis):
    return _half_all(_block(shape, axis), axis)


def gather_job(items):
    nw = len(items)
    shapes = [full.shape for full, _, _ in items]
    n_sem = 8

    def parts(sub):
        first, count, pieces = sub
        return (2 * first, count, 2 * pieces), (2 * first + count, count, 2 * pieces)

    def start(pos, ins, outs, sems):
        x, y, c, chips = pos
        for w, (_, ax, sub) in enumerate(items):
            mine = _window(outs[w], ax, shapes[w], blk=2 * x + y, half=c, sub=sub)
            for j in range(2):
                _remote(mine, mine, sems[0].at[n_sem * w + j], sems[1].at[n_sem * w + j], (*chips[j], c)).start()

    def middle(pos, ins, outs, sems):
        x, y, c, chips = pos
        for w, (_, ax, sub) in enumerate(items):
            for j in range(2):
                cx, cy = chips[j]
                landed = _window(outs[w], ax, shapes[w], blk=2 * cx + cy, half=c, sub=sub)
                _remote(landed, landed, sems[0].at[n_sem * w + j], sems[1].at[n_sem * w + j], (cx, cy, c)).wait_recv()
                part = _window(outs[w], ax, shapes[w], blk=2 * cx + cy, half=c, sub=parts(sub)[j])
                _remote(part, part, sems[0].at[n_sem * w + 2 + j], sems[1].at[n_sem * w + 2 + j],
                        (*chips[1 - j], c)).start()
                _remote(landed, landed, sems[0].at[n_sem * w + 4 + j], sems[1].at[n_sem * w + 4 + j],
                        (x, y, 1 - c)).start()

    def finish(pos, ins, outs, sems):
        x, y, c, chips = pos
        sib = (x, y, 1 - c)
        for w, (_, ax, sub) in enumerate(items):
            dx, dy = chips[2]
            for j in range(2):
                part = _window(outs[w], ax, shapes[w], blk=2 * dx + dy, half=c, sub=parts(sub)[j])
                cp = _remote(part, part, sems[0].at[n_sem * w + 2 + j], sems[1].at[n_sem * w + 2 + j], sib)
                cp.wait_recv()
                cp.wait_send()
            diag = _window(outs[w], ax, shapes[w], blk=2 * dx + dy, half=c, sub=sub)
            _remote(diag, diag, sems[0].at[n_sem * w + 6], sems[1].at[n_sem * w + 6], sib).start()
        for w, (_, ax, sub) in enumerate(items):
            for j, (cx, cy) in enumerate(chips):
                passed = _window(outs[w], ax, shapes[w], blk=2 * cx + cy, half=1 - c, sub=sub)
                cp = _remote(passed, passed, sems[0].at[n_sem * w + 4 + j], sems[1].at[n_sem * w + 4 + j], sib)
                cp.wait_recv()
                cp.wait_send()
            mine = _window(outs[w], ax, shapes[w], blk=2 * x + y, half=c, sub=sub)
            for j in range(2):
                _remote(mine, mine, sems[0].at[n_sem * w + j], sems[1].at[n_sem * w + j], sib).wait_send()

    sems = [pltpu.SemaphoreType.DMA((n_sem * nw,)), pltpu.SemaphoreType.DMA((n_sem * nw,))]
    return Job([full for full, _, _ in items], [_sds(full.shape, full.dtype) for full, _, _ in items],
               {w: w for w in range(nw)}, sems, start, middle, finish)


def pair_job(grads, axes):
    nw = len(grads)
    shapes = [g.shape for g in grads]

    def start(pos, ins, outs, sems):
        x, y, c, _ = pos
        for w in range(nw):
            _remote(_window(ins[w], axes[w], shapes[w], half=1 - c), outs[w], sems[0].at[w], sems[1].at[w],
                    (x, y, 1 - c)).start()

    def finish(pos, ins, outs, sems):
        x, y, c, _ = pos
        for w in range(nw):
            cp = _remote(outs[w], outs[w], sems[0].at[w], sems[1].at[w], (x, y, 1 - c))
            cp.wait_recv()
            cp.wait_send()

    sems = [pltpu.SemaphoreType.DMA((nw,)), pltpu.SemaphoreType.DMA((nw,))]
    return Job(list(grads), [_sds(_half_all(s, a), BF16) for s, a in zip(shapes, axes)], {}, sems, start, None,
               finish)


def cross_job(items):
    nw = len(items)
    inputs, aliases = [], {}
    for w, (part, ax, shape, prev, sub) in enumerate(items):
        inputs.append(part)
        if prev is not None:
            aliases[len(inputs)] = w
            inputs.append(prev)

    def copies(pos, ins, outs, sems):
        x, y, c, chips = pos
        k = 0
        for w, (_, ax, shape, prev, sub) in enumerate(items):
            src = ins[k]
            k += 2 if prev is not None else 1
            for j, (cx, cy) in enumerate(chips):
                slot = _window(outs[w].at[j], ax, shape, sub=sub)
                yield (_remote(_window(src, ax, shape, blk=2 * cx + cy, sub=sub), slot,
                               sems[0].at[3 * w + j], sems[1].at[3 * w + j], (cx, cy, c)),
                       _remote(slot, slot, sems[0].at[3 * w + j], sems[1].at[3 * w + j], (cx, cy, c)))

    def start(pos, ins, outs, sems):
        for send, _ in copies(pos, ins, outs, sems):
            send.start()

    def finish(pos, ins, outs, sems):
        for send, recv in copies(pos, ins, outs, sems):
            recv.wait_recv()
            send.wait_send()

    sems = [pltpu.SemaphoreType.DMA((3 * nw,)), pltpu.SemaphoreType.DMA((3 * nw,))]
    out_shapes = [_sds((3,) + _half_block(shape, ax), BF16) for _, ax, shape, _, _ in items]
    return Job(inputs, out_shapes, aliases, sems, start, None, finish)


def final_job(blocks, axes, shapes):
    nw = len(blocks)

    def start(pos, ins, outs, sems):
        x, y, c, _ = pos
        for w in range(nw):
            mine = _window(outs[w], axes[w], shapes[w], half=c)
            _remote(mine, mine, sems[0].at[w], sems[1].at[w], (x, y, 1 - c)).start()

    def finish(pos, ins, outs, sems):
        x, y, c, _ = pos
        for w in range(nw):
            theirs = _window(outs[w], axes[w], shapes[w], half=1 - c)
            cp = _remote(theirs, theirs, sems[0].at[w], sems[1].at[w], (x, y, 1 - c))
            cp.wait_recv()
            cp.wait_send()

    sems = [pltpu.SemaphoreType.DMA((nw,)), pltpu.SemaphoreType.DMA((nw,))]
    return Job(list(blocks), [_sds(b.shape, b.dtype) for b in blocks], {w: w for w in range(nw)}, sems, start, None,
               finish)


def stack_job(small):
    def peers(pos):
        x, y, c, _ = pos
        for k in range(1, 8):
            yield k - 1, (1 - x if k & 4 else x, 1 - y if k & 2 else y, 1 - c if k & 1 else c)

    def start(pos, ins, outs, sems):
        x, y, c, _ = pos
        mine = outs[0].at[4 * x + 2 * y + c]
        pltpu.make_async_copy(ins[0], mine, sems[2]).start()
        for k, dev in peers(pos):
            _remote(ins[0], mine, sems[0].at[k], sems[1].at[k], dev).start()

    def finish(pos, ins, outs, sems):
        x, y, c, _ = pos
        for k, (px, py, pc) in peers(pos):
            slot = outs[0].at[4 * px + 2 * py + pc]
            cp = _remote(slot, slot, sems[0].at[k], sems[1].at[k], (px, py, pc))
            cp.wait_recv()
            cp.wait_send()
        pltpu.make_async_copy(ins[0], outs[0].at[4 * x + 2 * y + c], sems[2]).wait()

    sems = [pltpu.SemaphoreType.DMA((7,)), pltpu.SemaphoreType.DMA((7,)), pltpu.SemaphoreType.DMA]
    return Job([small], [_sds((8,) + small.shape, small.dtype)], {}, sems, start, None, finish)


def columns_job(block):
    cols = block.shape[1]
    place = lambda out, b: out.at[:, _ds(b * cols, cols, True)]

    def start(pos, ins, outs, sems):
        x, y, c, chips = pos
        pltpu.make_async_copy(ins[0], place(outs[0], 2 * x + y), sems[2]).start()
        for j, (cx, cy) in enumerate(chips):
            _remote(ins[0], place(outs[0], 2 * x + y), sems[0].at[j], sems[1].at[j], (cx, cy, c)).start()

    def finish(pos, ins, outs, sems):
        x, y, c, chips = pos
        for j, (cx, cy) in enumerate(chips):
            got = place(outs[0], 2 * cx + cy)
            cp = _remote(got, got, sems[0].at[j], sems[1].at[j], (cx, cy, c))
            cp.wait_recv()
            cp.wait_send()
        pltpu.make_async_copy(ins[0], place(outs[0], 2 * x + y), sems[2]).wait()

    sems = [pltpu.SemaphoreType.DMA((3,)), pltpu.SemaphoreType.DMA((3,)), pltpu.SemaphoreType.DMA]
    return Job([block], [_sds((block.shape[0], N_CHIPS * cols), block.dtype)], {}, sems, start, None, finish)


def _call(name, body, grid, in_specs, out_specs, out_shape, args, scratch=(), sem=None, jobs=()):
    n_in, n_out, n_sc = len(args), len(out_shape), len(scratch)
    if not jobs:
        outs = pl.pallas_call(
            body, name=name, grid=grid, in_specs=in_specs, out_specs=out_specs, out_shape=out_shape,
            scratch_shapes=list(scratch), compiler_params=_params(sem))(*args)
        return list(outs), []

    total = math.prod(grid) if grid else 1
    mid = min(total - 1, (2 * total) // 3)

    def split(refs, start, counts):
        out = []
        for n in counts:
            out.append(refs[start:start + n])
            start += n
        return out, start

    def wrapped(*refs):
        c_in = refs[:n_in]
        j_ins, p = split(refs, n_in, [len(j.inputs) for j in jobs])
        c_out = refs[p:p + n_out]
        j_outs, p = split(refs, p + n_out, [len(j.out_shapes) for j in jobs])
        c_sc = refs[p:p + n_sc]
        j_sems, p = split(refs, p + n_sc, [len(j.sems) for j in jobs])
        pos = _place()
        step = 0
        for axis, extent in enumerate(grid):
            step = step * extent + pl.program_id(axis)

        def run(phase):
            for j, ins, outs, sems in zip(jobs, j_ins, j_outs, j_sems):
                fn = getattr(j, phase)
                if fn is not None:
                    fn(pos, ins, outs, sems)

        if total == 1:
            run("start")
            body(*c_in, *c_out, *c_sc)
            run("middle")
            run("finish")
            return
        pl.when(step == 0)(lambda: run("start"))
        body(*c_in, *c_out, *c_sc)
        if any(j.middle is not None for j in jobs):
            pl.when(step == mid)(lambda: run("middle"))
        pl.when(step == total - 1)(lambda: run("finish"))

    aliases, in_at, out_at = {}, n_in, n_out
    for j in jobs:
        for src, dst in j.aliases.items():
            aliases[in_at + src] = out_at + dst
        in_at += len(j.inputs)
        out_at += len(j.out_shapes)
    outs = pl.pallas_call(
        wrapped, name=name, grid=grid,
        in_specs=list(in_specs) + [ANY] * (in_at - n_in),
        out_specs=list(out_specs) + [ANY] * (out_at - n_out),
        out_shape=list(out_shape) + [s for j in jobs for s in j.out_shapes],
        scratch_shapes=list(scratch) + [s for j in jobs for s in j.sems],
        input_output_aliases=aliases,
        compiler_params=_params(("arbitrary",) * len(grid)),
    )(*args, *[a for j in jobs for a in j.inputs])
    job_outs, p = split(outs, n_out, [len(j.out_shapes) for j in jobs])
    return list(outs[:n_out]), [list(o) for o in job_outs]


def comm_only(name, jobs):
    def body(dummy_ref, out_ref):
        out_ref[...] = dummy_ref[...]

    dummy = jnp.zeros((8, 128), F32)
    spec = pl.BlockSpec((8, 128), lambda: (0, 0))
    return _call(name, body, (), [spec], [spec], [_sds((8, 128), F32)], [dummy], jobs=jobs)[1]


def _ret(outs, job_outs, jobs, single=True):
    res = outs[0] if single else outs
    return (res, job_outs) if jobs else res


def rmsnorm_fwd(name, x, g):
    s, d = x.shape
    tm = _tile(s, 512, 8)

    def body(x_ref, g_ref, n_ref, r_ref):
        xv = x_ref[...]
        r = lax.rsqrt(jnp.mean(xv * xv, axis=-1, keepdims=True) + EPS)
        n_ref[...] = (xv * r * g_ref[...]).astype(BF16)
        r_ref[...] = r

    row = lambda i: (i, 0)
    return _call(
        name, body, (s // tm,),
        [pl.BlockSpec((tm, d), row), pl.BlockSpec((1, d), lambda i: (0, 0))],
        [pl.BlockSpec((tm, d), row), pl.BlockSpec((tm, 1), row)],
        [_sds((s, d), BF16), _sds((s, 1), F32)], [x, g], sem=("arbitrary",))[0]


def rmsnorm_bwd(name, dn, x, r, g, dh_in, jobs=()):
    s, d = x.shape
    tm = _tile(s, 512, 8)

    def body(dn_ref, x_ref, r_ref, g_ref, dh_ref, out_ref, outb_ref, dg_ref):
        i = pl.program_id(0)
        xh = x_ref[...] * r_ref[...]
        dnv = dn_ref[...]
        dxh = dnv * g_ref[...]
        dx = r_ref[...] * (dxh - xh * jnp.mean(dxh * xh, axis=-1, keepdims=True))
        out = dh_ref[...] + dx
        out_ref[...] = out
        outb_ref[...] = out.astype(BF16)
        part = jnp.sum(dnv * xh, axis=0, keepdims=True)

        @pl.when(i == 0)
        def _():
            dg_ref[...] = part

        @pl.when(i > 0)
        def _():
            dg_ref[...] += part

    row = lambda i: (i, 0)
    fixed = lambda i: (0, 0)
    outs, job_outs = _call(
        name, body, (s // tm,),
        [pl.BlockSpec((tm, d), row), pl.BlockSpec((tm, d), row), pl.BlockSpec((tm, 1), row),
         pl.BlockSpec((1, d), fixed), pl.BlockSpec((tm, d), row)],
        [pl.BlockSpec((tm, d), row), pl.BlockSpec((tm, d), row), pl.BlockSpec((1, d), fixed)],
        [_sds((s, d), F32), _sds((s, d), BF16), _sds((1, d), F32)], [dn, x, r, g, dh_in],
        sem=("arbitrary",), jobs=jobs)
    return _ret(outs, job_outs, jobs, single=False)


def gain_grad(name, dn_a, dn_b, x, r):
    s, d = x.shape
    tm = _tile(s, 512, 8)

    def body(a_ref, b_ref, x_ref, r_ref, dg_ref):
        i = pl.program_id(0)
        part = jnp.sum((a_ref[...] + b_ref[...]) * (x_ref[...] * r_ref[...]), axis=0, keepdims=True)

        @pl.when(i == 0)
        def _():
            dg_ref[...] = part

        @pl.when(i > 0)
        def _():
            dg_ref[...] += part

    row = lambda i: (i, 0)
    return _call(
        name, body, (s // tm,),
        [pl.BlockSpec((tm, d), row), pl.BlockSpec((tm, d), row), pl.BlockSpec((tm, d), row),
         pl.BlockSpec((tm, 1), row)],
        [pl.BlockSpec((1, d), lambda i: (0, 0))], [_sds((1, d), F32)], [dn_a, dn_b, x, r],
        sem=("arbitrary",))[0][0]


def loss_head(name, h, g, target):
    s, d = h.shape
    tm = _tile(s, 512, 8)
    nsteps = s // tm

    def body(h_ref, g_ref, t_ref, loss_ref, dh_ref, dhb_ref, dg_ref, sq_ref):
        i = pl.program_id(0)
        hv = h_ref[...]
        gv = g_ref[...]
        r = lax.rsqrt(jnp.mean(hv * hv, axis=-1, keepdims=True) + EPS)
        xh = hv * r
        err = xh * gv - t_ref[...]
        dy = err * (1.0 / d)
        dxh = dy * gv
        dh = r * (dxh - xh * jnp.mean(dxh * xh, axis=-1, keepdims=True))
        dh_ref[...] = dh
        dhb_ref[...] = dh.astype(BF16)
        dg_part = jnp.sum(dy * xh, axis=0, keepdims=True)
        sq_part = jnp.sum(err * err, axis=0, keepdims=True)

        @pl.when(i == 0)
        def _():
            dg_ref[...] = dg_part
            sq_ref[...] = sq_part

        @pl.when(i > 0)
        def _():
            dg_ref[...] += dg_part
            sq_ref[...] += sq_part

        @pl.when(i == nsteps - 1)
        def _():
            total = jnp.sum(sq_ref[...], axis=-1, keepdims=True) * (0.5 / d)
            loss_ref[...] = jnp.broadcast_to(total, loss_ref.shape)

    row = lambda i: (i, 0)
    fixed = lambda i: (0, 0)
    return _call(
        name, body, (nsteps,),
        [pl.BlockSpec((tm, d), row), pl.BlockSpec((1, d), fixed), pl.BlockSpec((tm, d), row)],
        [pl.BlockSpec((8, 128), fixed), pl.BlockSpec((tm, d), row), pl.BlockSpec((tm, d), row),
         pl.BlockSpec((1, d), fixed)],
        [_sds((8, 128), F32), _sds((s, d), F32), _sds((s, d), BF16), _sds((1, d), F32)], [h, g, target],
        scratch=[pltpu.VMEM((1, d), F32)], sem=("arbitrary",))[0]


def _mm(name, grid, in_arrays, in_specs, out_shapes, out_specs, acc_tile, dot, epilogue, jobs=()):
    nk = grid[2]
    n_in = len(in_arrays)
    n_out = len(out_shapes)

    def body(*refs):
        ins, outs = refs[:n_in], refs[n_in:n_in + n_out]
        if nk == 1:
            epilogue(dot(*ins), ins, outs)
            return
        acc = refs[n_in + n_out]
        k = pl.program_id(2)

        @pl.when(k == 0)
        def _():
            acc[...] = dot(*ins)

        @pl.when(jnp.logical_and(k > 0, k < nk - 1))
        def _():
            acc[...] += dot(*ins)

        @pl.when(k == nk - 1)
        def _():
            epilogue(acc[...] + dot(*ins), ins, outs)

    scratch = [pltpu.VMEM(acc_tile, F32)] if nk > 1 else []
    outs, job_outs = _call(name, body, grid, in_specs, out_specs, out_shapes, in_arrays, scratch=scratch,
                           sem=("parallel", "parallel", "arbitrary"), jobs=jobs)
    return _ret(outs, job_outs, jobs)


def _store(scale, dtype):
    def epilogue(acc, ins, outs):
        outs[0][...] = (acc * scale if scale != 1.0 else acc).astype(dtype)
    return epilogue


def mm_nn(name, a, w, out_dtype, tm=1024, tn=1024, tk=2048, jobs=()):
    m, kd = a.shape
    n = w.shape[1]
    tm, tn, tk = _tile(m, tm, 8), _tile(n, tn), _tile(kd, tk)
    return _mm(
        name, (n // tn, m // tm, kd // tk), [a, w],
        [pl.BlockSpec((tm, tk), lambda j, i, k: (i, k)), pl.BlockSpec((tk, tn), lambda j, i, k: (k, j))],
        [_sds((m, n), out_dtype)], [pl.BlockSpec((tm, tn), lambda j, i, k: (i, j))], (tm, tn),
        lambda a_ref, w_ref: _dot_nn(a_ref[...], w_ref[...]), _store(1.0, out_dtype), jobs)


def mm_nn_resid(name, a, w, x, scale, tm=1024, tn=1024, tk=1408, jobs=()):
    m, kd = a.shape
    n = w.shape[1]
    tm, tn, tk = _tile(m, tm, 8), _tile(n, tn), _tile(kd, tk)

    def epilogue(acc, ins, outs):
        outs[0][...] = ins[2][...] + scale * acc

    return _mm(
        name, (n // tn, m // tm, kd // tk), [a, w, x],
        [pl.BlockSpec((tm, tk), lambda j, i, k: (i, k)), pl.BlockSpec((tk, tn), lambda j, i, k: (k, j)),
         pl.BlockSpec((tm, tn), lambda j, i, k: (i, j))],
        [_sds((m, n), F32)], [pl.BlockSpec((tm, tn), lambda j, i, k: (i, j))], (tm, tn),
        lambda a_ref, w_ref, x_ref: _dot_nn(a_ref[...], w_ref[...]), epilogue, jobs)


def mm_nt(name, a, w, out_dtype, scale=1.0, tm=1024, tn=1024, tk=2048, jobs=()):
    m, kd = a.shape
    n = w.shape[0]
    tm, tn, tk = _tile(m, tm, 8), _tile(n, tn), _tile(kd, tk)
    return _mm(
        name, (n // tn, m // tm, kd // tk), [a, w],
        [pl.BlockSpec((tm, tk), lambda j, i, k: (i, k)), pl.BlockSpec((tn, tk), lambda j, i, k: (j, k))],
        [_sds((m, n), out_dtype)], [pl.BlockSpec((tm, tn), lambda j, i, k: (i, j))], (tm, tn),
        lambda a_ref, w_ref: _dot_nt(a_ref[...], w_ref[...]), _store(scale, out_dtype), jobs)


def mm_nt_pair(name, a3, w, out_dtype, tm=1024, tn=1024, tk=2816, jobs=()):
    _, m, f = a3.shape
    n = w.shape[0]
    tm, tn, tk = _tile(m, tm, 8), _tile(n, tn), _tile(f, tk)
    nkf = f // tk
    return _mm(
        name, (n // tn, m // tm, 2 * nkf), [a3, w],
        [pl.BlockSpec((None, tm, tk), lambda j, i, k: (k // nkf, i, k % nkf)),
         pl.BlockSpec((tn, tk), lambda j, i, k: (j, k))],
        [_sds((m, n), out_dtype)], [pl.BlockSpec((tm, tn), lambda j, i, k: (i, j))], (tm, tn),
        lambda a_ref, w_ref: _dot_nt(a_ref[...], w_ref[...]), _store(1.0, out_dtype), jobs)


def mm_tn(name, a, b, out_dtype, scale=1.0, tm=1024, tn=1024, tk=4096, jobs=()):
    kd, m = a.shape
    n = b.shape[1]
    tm, tn, tk = _tile(m, tm), _tile(n, tn), _tile(kd, tk, 16)
    return _mm(
        name, (n // tn, m // tm, kd // tk), [a, b],
        [pl.BlockSpec((tk, tm), lambda j, i, k: (k, i)), pl.BlockSpec((tk, tn), lambda j, i, k: (k, j))],
        [_sds((m, n), out_dtype)], [pl.BlockSpec((tm, tn), lambda j, i, k: (i, j))], (tm, tn),
        lambda a_ref, b_ref: _dot_tn(a_ref[...], b_ref[...]), _store(scale, out_dtype), jobs)


def mm_tn_pair(name, a, b3, out_dtype, tm=1024, tn=512, tk=4096, jobs=()):
    kd, m = a.shape
    f = b3.shape[2]
    tm, tn, tk = _tile(m, tm), _tile(f, tn), _tile(kd, tk, 16)
    nf = f // tn
    return _mm(
        name, (m // tm, 2 * nf, kd // tk), [a, b3],
        [pl.BlockSpec((tk, tm), lambda i, j, k: (k, i)),
         pl.BlockSpec((None, tk, tn), lambda i, j, k: (j // nf, k, j % nf))],
        [_sds((m, 2 * f), out_dtype)], [pl.BlockSpec((tm, tn), lambda i, j, k: (i, j))], (tm, tn),
        lambda a_ref, b_ref: _dot_tn(a_ref[...], b_ref[...]), _store(1.0, out_dtype), jobs)


def swiglu_fwd(name, n, w_in, tm=1024, tn=512, jobs=()):
    s, d = n.shape
    f = w_in.shape[1] // 2
    tm, tn = _tile(s, tm, 8), _tile(f, tn)
    nf = f // tn

    def body(n_ref, wg_ref, wu_ref, gu_ref, a_ref):
        nv = n_ref[...]
        g = _dot_nn(nv, wg_ref[...])
        u = _dot_nn(nv, wu_ref[...])
        gu_ref[0] = g.astype(BF16)
        gu_ref[1] = u.astype(BF16)
        a_ref[...] = (g * jax.nn.sigmoid(g) * u).astype(BF16)

    outs, job_outs = _call(
        name, body, (nf, s // tm),
        [pl.BlockSpec((tm, d), lambda j, i: (i, 0)), pl.BlockSpec((d, tn), lambda j, i: (0, j)),
         pl.BlockSpec((d, tn), lambda j, i: (0, j + nf))],
        [pl.BlockSpec((2, tm, tn), lambda j, i: (0, i, j)), pl.BlockSpec((tm, tn), lambda j, i: (i, j))],
        [_sds((2, s, f), BF16), _sds((s, f), BF16)], [n, w_in, w_in], sem=("parallel", "parallel"), jobs=jobs)
    return _ret(outs, job_outs, jobs, single=False)


def swiglu_bwd(name, dh, w_out, gu, scale, tm=1024, tn=512, jobs=()):
    s, d = dh.shape
    f = w_out.shape[0]
    tm, tn = _tile(s, tm, 8), _tile(f, tn)

    sub = _tile(tm, 256, 8)

    def body(dh_ref, w_ref, gu_ref, out_ref):
        for lo in range(0, tm, sub):
            rows = slice(lo, lo + sub)
            da = _dot_nt(dh_ref[rows, :], w_ref[...]) * scale
            g = gu_ref[0, rows, :].astype(F32)
            u = gu_ref[1, rows, :].astype(F32)
            sg = jax.nn.sigmoid(g)
            out_ref[0, rows, :] = (da * u * (sg * (1.0 + g * (1.0 - sg)))).astype(BF16)
            out_ref[1, rows, :] = (da * (g * sg)).astype(BF16)

    outs, job_outs = _call(
        name, body, (f // tn, s // tm),
        [pl.BlockSpec((tm, d), lambda j, i: (i, 0)), pl.BlockSpec((tn, d), lambda j, i: (j, 0)),
         pl.BlockSpec((2, tm, tn), lambda j, i: (0, i, j))],
        [pl.BlockSpec((2, tm, tn), lambda j, i: (0, i, j))],
        [_sds((2, s, f), BF16)], [dh, w_out, gu], sem=("parallel", "parallel"), jobs=jobs)
    return _ret(outs, job_outs, jobs)


HALO = 16


def _conv_inputs(z_ref, hgc_ref, hhc_ref, i, cw, tm):
    gc = z_ref[:, cw:2 * cw].astype(F32)
    hc = z_ref[:, 2 * cw:3 * cw].astype(F32)
    cin = gc * hc
    halo = hgc_ref[...].astype(F32) * hhc_ref[...].astype(F32) * (i > 0).astype(F32)
    row = lax.broadcasted_iota(jnp.int32, (tm, cw), 0)
    x1 = jnp.where(row == 0, halo[HALO - 1:HALO], pltpu.roll(cin, 1, 0))
    x2 = jnp.where(row == 0, halo[HALO - 2:HALO - 1], jnp.where(row == 1, halo[HALO - 1:HALO], pltpu.roll(cin, 2, 0)))
    return gc, hc, cin, x1, x2


def _tril(w):
    r = lax.broadcasted_iota(jnp.int32, w.shape, 0)
    c = lax.broadcasted_iota(jnp.int32, w.shape, 1)
    return jnp.where(r >= c, w, jnp.zeros_like(w))


def mixer_fwd(name, z, conv_w, conv_b, g_v, w_s, b_t, tm=256, jobs=()):
    s, zc = z.shape
    cw = conv_w.shape[1]
    gw = g_v.shape[1]
    heads = gw // GROUP
    tm = _tile(s, tm)
    hb = tm // HALO

    def body(z_ref, hgc_ref, hhc_ref, cw_ref, cb_ref, gv_ref, ws_ref, bt_ref, y_ref):
        i = pl.program_id(0)
        _, _, cin, x1, x2 = _conv_inputs(z_ref, hgc_ref, hhc_ref, i, cw, tm)
        cv = cb_ref[...] + cw_ref[2:3, :] * cin + cw_ref[1:2, :] * x1 + cw_ref[0:1, :] * x2
        y_ref[:, 0:cw] = (z_ref[:, 0:cw].astype(F32) * cv).astype(BF16)
        for h in range(heads):
            lo = h * GROUP
            vh = z_ref[:, 3 * cw + gw + lo:3 * cw + gw + lo + GROUP].astype(F32)
            rv = lax.rsqrt(jnp.mean(vh * vh, axis=-1, keepdims=True) + EPS)
            vn = (vh * rv * gv_ref[:, lo:lo + GROUP]).astype(BF16)
            w = _tril(ws_ref[h]).astype(BF16)
            for n in range(tm // GROUP):
                rows = slice(n * GROUP, (n + 1) * GROUP)
                sg = _dot_nn(w, vn[rows]) + bt_ref[:, h:h + 1]
                u = z_ref[rows, 3 * cw + lo:3 * cw + lo + GROUP].astype(F32)
                y_ref[rows, cw + lo:cw + lo + GROUP] = (u * sg).astype(BF16)

    fixed2 = lambda i: (0, 0)
    outs, job_outs = _call(
        name, body, (s // tm,),
        [pl.BlockSpec((tm, zc), lambda i: (i, 0)),
         pl.BlockSpec((HALO, cw), lambda i: (jnp.maximum(i * hb - 1, 0), 1)),
         pl.BlockSpec((HALO, cw), lambda i: (jnp.maximum(i * hb - 1, 0), 2)),
         pl.BlockSpec(conv_w.shape, fixed2), pl.BlockSpec(conv_b.shape, fixed2),
         pl.BlockSpec(g_v.shape, fixed2), pl.BlockSpec(w_s.shape, lambda i: (0, 0, 0)),
         pl.BlockSpec(b_t.shape, fixed2)],
        [pl.BlockSpec((tm, cw + gw), lambda i: (i, 0))], [_sds((s, cw + gw), BF16)],
        [z, z, z, conv_w, conv_b, g_v, w_s, b_t], sem=("arbitrary",), jobs=jobs)
    return _ret(outs, job_outs, jobs)


def mixer_bwd(name, z, dy, conv_w, conv_b, g_v, w_s, b_t, tm=256, jobs=()):
    s, zc = z.shape
    cw = conv_w.shape[1]
    gw = g_v.shape[1]
    heads = gw // GROUP
    tm = _tile(s, tm)
    hb = tm // HALO
    nsteps = s // tm
    last_halo = s // HALO - 1

    def body(z_ref, hgc_ref, hhc_ref, ngb_ref, dy_ref, ndy_ref, cw_ref, cb_ref, gv_ref, ws_ref, bt_ref,
             dz_ref, sm_ref, dws_ref, dbt_ref, dsg_ref):
        i = pl.program_id(0)

        @pl.when(i == 0)
        def _():
            sm_ref[...] = jnp.zeros_like(sm_ref)
            dws_ref[...] = jnp.zeros_like(dws_ref)
            dsg_ref[...] = jnp.zeros_like(dsg_ref)

        gc, hc, cin, x1, x2 = _conv_inputs(z_ref, hgc_ref, hhc_ref, i, cw, tm)
        w0, w1, w2 = cw_ref[0:1, :], cw_ref[1:2, :], cw_ref[2:3, :]
        cv = cb_ref[...] + w2 * cin + w1 * x1 + w0 * x2
        gb = z_ref[:, 0:cw].astype(F32)
        dyc = dy_ref[:, 0:cw].astype(F32)
        dz_ref[:, 0:cw] = (dyc * cv).astype(BF16)
        dcv = dyc * gb
        nxt = ndy_ref[...].astype(F32) * ngb_ref[...].astype(F32) * (i < nsteps - 1).astype(F32)
        row = lax.broadcasted_iota(jnp.int32, (tm, cw), 0)
        d1 = jnp.where(row == tm - 1, nxt[0:1], pltpu.roll(dcv, tm - 1, 0))
        d2 = jnp.where(row == tm - 1, nxt[1:2], jnp.where(row == tm - 2, nxt[0:1], pltpu.roll(dcv, tm - 2, 0)))
        dcin = w2 * dcv + w1 * d1 + w0 * d2
        dz_ref[:, cw:2 * cw] = (dcin * hc).astype(BF16)
        dz_ref[:, 2 * cw:3 * cw] = (dcin * gc).astype(BF16)
        sm_ref[0:1, :] += jnp.sum(dcv * x2, axis=0, keepdims=True)
        sm_ref[1:2, :] += jnp.sum(dcv * x1, axis=0, keepdims=True)
        sm_ref[2:3, :] += jnp.sum(dcv * cin, axis=0, keepdims=True)
        sm_ref[3:4, :] += jnp.sum(dcv, axis=0, keepdims=True)

        for h in range(heads):
            lo = h * GROUP
            vcol = slice(3 * cw + gw + lo, 3 * cw + gw + lo + GROUP)
            ucol = slice(3 * cw + lo, 3 * cw + lo + GROUP)
            vh = z_ref[:, vcol].astype(F32)
            rv = lax.rsqrt(jnp.mean(vh * vh, axis=-1, keepdims=True) + EPS)
            xh = vh * rv
            gvh = gv_ref[:, lo:lo + GROUP]
            vn = (xh * gvh).astype(BF16)
            w = _tril(ws_ref[h]).astype(BF16)
            dgv = jnp.zeros((1, GROUP), F32)
            for n in range(tm // GROUP):
                rows = slice(n * GROUP, (n + 1) * GROUP)
                sg = _dot_nn(w, vn[rows]) + bt_ref[:, h:h + 1]
                dyg = dy_ref[rows, cw + lo:cw + lo + GROUP].astype(F32)
                dsg = dyg * z_ref[rows, ucol].astype(F32)
                dz_ref[rows, ucol] = (dyg * sg).astype(BF16)
                dsgb = dsg.astype(BF16)
                dvn = _dot_tn(w, dsgb)
                dws_ref[h] += _dot_nt(dsgb, vn[rows])
                dsg_ref[:, lo:lo + GROUP] += dsg
                xhc = xh[rows]
                dgv = dgv + jnp.sum(dvn * xhc, axis=0, keepdims=True)
                dxh = dvn * gvh
                dv = rv[rows] * (dxh - xhc * jnp.mean(dxh * xhc, axis=-1, keepdims=True))
                dz_ref[rows, vcol] = dv.astype(BF16)
            sm_ref[4:5, lo:lo + GROUP] += dgv

        @pl.when(i == nsteps - 1)
        def _():
            for h in range(heads):
                dws_ref[h] = _tril(dws_ref[h])
                dbt_ref[:, h:h + 1] = jnp.sum(dsg_ref[:, h * GROUP:(h + 1) * GROUP], axis=-1, keepdims=True)

    fixed2 = lambda i: (0, 0)
    fixed3 = lambda i: (0, 0, 0)
    prev = lambda col: (lambda i: (jnp.maximum(i * hb - 1, 0), col))
    nxt_blk = lambda i: (jnp.minimum((i + 1) * hb, last_halo), 0)
    outs, job_outs = _call(
        name, body, (nsteps,),
        [pl.BlockSpec((tm, zc), lambda i: (i, 0)),
         pl.BlockSpec((HALO, cw), prev(1)), pl.BlockSpec((HALO, cw), prev(2)),
         pl.BlockSpec((HALO, cw), nxt_blk),
         pl.BlockSpec((tm, cw + gw), lambda i: (i, 0)), pl.BlockSpec((HALO, cw), nxt_blk),
         pl.BlockSpec(conv_w.shape, fixed2), pl.BlockSpec(conv_b.shape, fixed2),
         pl.BlockSpec(g_v.shape, fixed2), pl.BlockSpec(w_s.shape, fixed3), pl.BlockSpec(b_t.shape, fixed2)],
        [pl.BlockSpec((tm, zc), lambda i: (i, 0)), pl.BlockSpec((8, cw), fixed2),
         pl.BlockSpec(w_s.shape, fixed3), pl.BlockSpec(b_t.shape, fixed2)],
        [_sds((s, zc), BF16), _sds((8, cw), F32), _sds(w_s.shape, F32), _sds(b_t.shape, F32)],
        [z, z, z, z, dy, dy, conv_w, conv_b, g_v, w_s, b_t],
        scratch=[pltpu.VMEM((GROUP, gw), F32)], sem=("arbitrary",), jobs=jobs)
    return _ret(outs, job_outs, jobs, single=False)


def _softmax_rows(sc):
    e = jnp.exp(sc - jnp.max(sc, axis=-1, keepdims=True))
    return e / jnp.sum(e, axis=-1, keepdims=True)


def attn_fwd(name, q, k, v, tm=512):
    s, d = q.shape
    m = k.shape[0]
    hd = d // XA_HEADS
    scale = hd ** -0.5
    tm = _tile(s, tm, 8)

    def body(q_ref, k_ref, v_ref, o_ref):
        for h in range(XA_HEADS):
            cols = slice(h * hd, (h + 1) * hd)
            p = _softmax_rows(_dot_nt(q_ref[:, cols], k_ref[:, cols]) * scale)
            o_ref[:, cols] = _dot_nn(p.astype(BF16), v_ref[:, cols]).astype(BF16)

    return _call(
        name, body, (s // tm,),
        [pl.BlockSpec((tm, d), lambda i: (i, 0)), pl.BlockSpec((m, d), lambda i: (0, 0)),
         pl.BlockSpec((m, d), lambda i: (0, 0))],
        [pl.BlockSpec((tm, d), lambda i: (i, 0))], [_sds((s, d), BF16)], [q, k, v], sem=("arbitrary",))[0][0]


def attn_bwd(name, q, k, v, do, tm=512):
    s, d = q.shape
    m = k.shape[0]
    hd = d // XA_HEADS
    scale = hd ** -0.5
    tm = _tile(s, tm, 8)

    def body(q_ref, k_ref, v_ref, do_ref, dq_ref, dk_ref, dv_ref):
        i = pl.program_id(0)

        @pl.when(i == 0)
        def _():
            dk_ref[...] = jnp.zeros_like(dk_ref)
            dv_ref[...] = jnp.zeros_like(dv_ref)

        for h in range(XA_HEADS):
            cols = slice(h * hd, (h + 1) * hd)
            qh = q_ref[:, cols]
            doh = do_ref[:, cols]
            p = _softmax_rows(_dot_nt(qh, k_ref[:, cols]) * scale)
            dp = _dot_nt(doh, v_ref[:, cols])
            ds = (p * (dp - jnp.sum(dp * p, axis=-1, keepdims=True)) * scale).astype(BF16)
            dq_ref[:, cols] = _dot_nn(ds, k_ref[:, cols]).astype(BF16)
            dk_ref[:, cols] += _dot_tn(ds, qh)
            dv_ref[:, cols] += _dot_tn(p.astype(BF16), doh)

    row = lambda i: (i, 0)
    fixed = lambda i: (0, 0)
    return _call(
        name, body, (s // tm,),
        [pl.BlockSpec((tm, d), row), pl.BlockSpec((m, d), fixed), pl.BlockSpec((m, d), fixed),
         pl.BlockSpec((tm, d), row)],
        [pl.BlockSpec((tm, d), row), pl.BlockSpec((m, d), fixed), pl.BlockSpec((m, d), fixed)],
        [_sds((s, d), BF16), _sds((m, d), F32), _sds((m, d), F32)], [q, k, v, do], sem=("arbitrary",))[0]


def _grid2(rows, cols, row_mult):
    tr, tc = _tile(rows, 512, row_mult), _tile(cols, 2048)
    return tr, tc, rows // tr, cols // tc


def cast_place(name, block, axis, place):
    r, c = block.shape
    tr, tc, nbr, nbc = _grid2(r, c, 16)
    if axis == 1:
        dst = lambda i, j, p: (i, j + p[0] * nbc)
    else:
        dst = lambda i, j, p: (i + p[0] * nbr, j)

    def body(p_ref, w_ref, out_ref):
        out_ref[...] = w_ref[...].astype(BF16)

    return pl.pallas_call(
        body, name=name,
        grid_spec=pltpu.PrefetchScalarGridSpec(
            num_scalar_prefetch=1, grid=(nbr, nbc),
            in_specs=[pl.BlockSpec((tr, tc), lambda i, j, p: (i, j))],
            out_specs=pl.BlockSpec((tr, tc), dst)),
        out_shape=_sds(_full_shape(block.shape, axis), BF16),
        compiler_params=_params(("parallel", "parallel")),
    )(place, block)


def pair_add(name, grad, peer, axis, place):
    hr, hc = peer.shape
    tr, tc, nbr, nbc = _grid2(hr, hc, 16)
    if axis == 1:
        mine = lambda i, j, p: (i + p[1] * nbr, j)
    else:
        mine = lambda i, j, p: (i, j + p[1] * nbc)
    same = lambda i, j, p: (i, j)

    def body(p_ref, g_ref, q_ref, out_ref):
        out_ref[...] = (g_ref[...].astype(F32) + q_ref[...].astype(F32)).astype(BF16)

    return pl.pallas_call(
        body, name=name,
        grid_spec=pltpu.PrefetchScalarGridSpec(
            num_scalar_prefetch=1, grid=(nbr, nbc),
            in_specs=[pl.BlockSpec((tr, tc), mine), pl.BlockSpec((tr, tc), same)],
            out_specs=pl.BlockSpec((tr, tc), same)),
        out_shape=_sds((hr, hc), BF16),
        compiler_params=_params(("parallel", "parallel")),
    )(place, grad, peer)


def cross_sum(name, part, land, axis, shape, place):
    _, sr, sc = land.shape
    tr, tc, nbr, nbc = _grid2(sr, sc, 16)
    if axis == 1:
        own = lambda i, j, p: (i, j + p[0] * nbc)
        dst = lambda i, j, p: (i + p[1] * nbr, j)
    else:
        own = lambda i, j, p: (i + p[0] * nbr, j)
        dst = lambda i, j, p: (i, j + p[1] * nbc)

    def body(p_ref, own_ref, land_ref, out_ref):
        out_ref[...] = ((own_ref[...].astype(F32) + land_ref[0].astype(F32))
                        + (land_ref[1].astype(F32) + land_ref[2].astype(F32)))

    return pl.pallas_call(
        body, name=name,
        grid_spec=pltpu.PrefetchScalarGridSpec(
            num_scalar_prefetch=1, grid=(nbr, nbc),
            in_specs=[pl.BlockSpec((tr, tc), own), pl.BlockSpec((3, tr, tc), lambda i, j, p: (0, i, j))],
            out_specs=pl.BlockSpec((tr, tc), dst)),
        out_shape=_sds(_block(shape, axis), F32),
        compiler_params=_params(("parallel", "parallel")),
    )(place, part, land)


def _adam_math(w, g, m, v):
    m = ADAM_B1 * m + (1.0 - ADAM_B1) * g
    v = ADAM_B2 * v + (1.0 - ADAM_B2) * (g * g)
    m_hat = m / (1.0 - ADAM_B1 ** ADAM_STEP)
    v_hat = v / (1.0 - ADAM_B2 ** ADAM_STEP)
    delta = -ADAM_LR * (m_hat / (jnp.sqrt(v_hat) + ADAM_EPS) + ADAM_WD * w)
    return delta, m, v


def adamw(name, w, g, m, v, jobs=()):
    r, c = w.shape
    tr, tc = _tile(r, 256, 8), _tile(c, 1408)

    def body(w_ref, g_ref, m_ref, v_ref, g_out, d_out, m_out, v_out):
        d, mm, vv = _adam_math(w_ref[...], g_ref[...], m_ref[...], v_ref[...])
        g_out[...] = g_ref[...]
        d_out[...] = d
        m_out[...] = mm
        v_out[...] = vv

    spec = pl.BlockSpec((tr, tc), lambda i, j: (i, j))
    outs, job_outs = _call(name, body, (r // tr, c // tc), [spec] * 4, [spec] * 4, [_sds((r, c), F32)] * 4,
                           [w, g, m, v], sem=("parallel", "parallel"), jobs=jobs)
    return _ret(outs, job_outs, jobs, single=False)


def small_sum(name, stacks):
    def body(*refs):
        for s_ref, out_ref in zip(refs[:len(stacks)], refs[len(stacks):]):
            acc = s_ref[0]
            for d in range(1, s_ref.shape[0]):
                acc = acc + s_ref[d]
            out_ref[...] = acc

    return pl.pallas_call(body, name=name, out_shape=[_sds(s.shape[1:], F32) for s in stacks])(*stacks)


WEIGHTS = ["g_ffn1", "w_ffn1_in", "w_ffn1_out", "g_mix", "w_mix_in", "conv_w", "conv_b", "g_gm_v", "w_spatial",
           "b_spatial", "w_mix_out", "g_xattn", "g_mem", "w_xq", "w_xk", "w_xv", "w_xo", "g_ffn2", "w_ffn2_in",
           "w_ffn2_out", "g_final"]
BIG = {"w_ffn1_in": 1, "w_ffn1_out": 0, "w_mix_in": 1, "w_mix_out": 0, "w_xq": 0, "w_xk": 0, "w_xv": 0, "w_xo": 0,
       "w_ffn2_in": 1, "w_ffn2_out": 0}
SMALL = [n for n in WEIGHTS if n not in BIG]
LATE_SMALL = ["g_ffn1"]
EARLY_SMALL = [n for n in SMALL if n not in LATE_SMALL]


def _pack(arrays):
    flat = jnp.concatenate([a.reshape(-1) for a in arrays])
    rows = -(-flat.shape[0] // 1024) * 8
    return jnp.pad(flat, (0, rows * 128 - flat.shape[0])).reshape(rows, 128)


def _unpack(buf, shapes):
    flat = buf.reshape(-1)
    out, pos = [], 0
    for shp in shapes:
        n = math.prod(shp)
        out.append(flat[pos:pos + n].reshape(shp))
        pos += n
    return out


def kernel(x, mem, g_ffn1, w_ffn1_in, w_ffn1_out, g_mix, w_mix_in, conv_w, conv_b, g_gm_v, w_spatial, b_spatial, w_mix_out, g_xattn, g_mem, w_xq, w_xk, w_xv, w_xo, g_ffn2, w_ffn2_in, w_ffn2_out, g_final, loss_target, m_g_ffn1, m_w_ffn1_in, m_w_ffn1_out, m_g_mix, m_w_mix_in, m_conv_w, m_conv_b, m_g_gm_v, m_w_spatial, m_b_spatial, m_w_mix_out, m_g_xattn, m_g_mem, m_w_xq, m_w_xk, m_w_xv, m_w_xo, m_g_ffn2, m_w_ffn2_in, m_w_ffn2_out, m_g_final, v_g_ffn1, v_w_ffn1_in, v_w_ffn1_out, v_g_mix, v_w_mix_in, v_conv_w, v_conv_b, v_g_gm_v, v_w_spatial, v_b_spatial, v_w_mix_out, v_g_xattn, v_g_mem, v_w_xq, v_w_xk, v_w_xv, v_w_xo, v_g_ffn2, v_w_ffn2_in, v_w_ffn2_out, v_g_final):
    given = dict(locals())
    wts = {n: given[n] for n in WEIGHTS}
    mom = {n: given["m_" + n] for n in WEIGHTS}
    var = {n: given["v_" + n] for n in WEIGHTS}

    xi, yi, ci = lax.axis_index("x"), lax.axis_index("y"), lax.axis_index("c")
    blk = 2 * xi + yi
    place = jnp.stack([blk, ci]).astype(jnp.int32)

    x2, mem2, tgt = x[0], mem[0], loss_target[0]
    own = {n: cast_place("cast_" + n, wts[n][0], BIG[n], place) for n in BIG}
    shape = {n: own[n].shape for n in BIG}
    w_s, b_t = w_spatial[0], b_spatial[0].T
    gf = g_final[None]

    def gather(*names):
        return gather_job([(own[n], BIG[n], WHOLE) for n in names])

    full = {}

    (full["w_ffn1_in"],), (conv_taps,) = comm_only(
        "gather_first", [gather("w_ffn1_in"), columns_job(jnp.pad(conv_w[0], ((0, 8 - CONV_K), (0, 0))))])
    n1, r1 = rmsnorm_fwd("norm1", x2, g_ffn1)
    (gu1, a1), (got,) = swiglu_fwd("ffn1_in", n1, full["w_ffn1_in"], jobs=[gather("w_ffn1_out", "w_mix_in")])
    full["w_ffn1_out"], full["w_mix_in"] = got
    h1, (got,) = mm_nn_resid("ffn1_out", a1, full["w_ffn1_out"], x2, 0.5, tm=512, tk=5632,
                             jobs=[gather("w_mix_out", "w_xq", "w_xk")])
    full["w_mix_out"], full["w_xq"], full["w_xk"] = got
    n2, r2 = rmsnorm_fwd("norm2", h1, g_mix)
    z, (got,) = mm_nn("mix_in", n2, full["w_mix_in"], BF16, jobs=[gather("w_xv", "w_xo")])
    full["w_xv"], full["w_xo"] = got

    pieces = 4
    def ffn2_piece(p, prev):
        return [gather_job([(prev, 1, (p, 1, pieces))])]

    ycat, ((w2in,),) = mixer_fwd("mixer", z, conv_taps, conv_b, g_gm_v, w_s, b_t,
                                 jobs=ffn2_piece(0, own["w_ffn2_in"]))
    h2, ((w2in,),) = mm_nn_resid("mix_out", ycat, full["w_mix_out"], h1, 1.0, tk=2048, jobs=ffn2_piece(1, w2in))
    n3, r3 = rmsnorm_fwd("norm3", h2, g_xattn)
    mn, rm = rmsnorm_fwd("norm_mem", mem2, g_mem)
    q, ((w2in,),) = mm_nn("xq", n3, full["w_xq"], BF16, jobs=ffn2_piece(2, w2in))
    k = mm_nn("xk", mn, full["w_xk"], BF16)
    v = mm_nn("xv", mn, full["w_xv"], BF16)
    o = attn_fwd("attn", q, k, v)
    h3, ((w2in,),) = mm_nn_resid("xo", o, full["w_xo"], h2, 1.0, tk=2048, jobs=ffn2_piece(3, w2in))
    full["w_ffn2_in"] = w2in
    n4, r4 = rmsnorm_fwd("norm4", h3, g_ffn2)
    (gu2, a2), ((full["w_ffn2_out"],),) = swiglu_fwd("ffn2_in", n4, full["w_ffn2_in"], jobs=[gather("w_ffn2_out")])
    h4 = mm_nn_resid("ffn2_out", a2, full["w_ffn2_out"], h3, 0.5, tm=512, tk=5632)
    loss_blk, dh4, dh4b, dg_final = loss_head("loss_head", h4, gf, tgt)

    dw, peer, part, land, half, grads = {}, {}, {}, {}, {}, {}

    def send_pair(*names):
        return pair_job([dw[n] for n in names], [BIG[n] for n in names])

    def take_pair(names, got):
        for n, p in zip(names, got):
            part[n] = pair_add("pair_add_" + n, dw[n], p, BIG[n], place)

    def send_cross(*names, sub=WHOLE):
        return cross_job([(part[n], BIG[n], shape[n], land.get(n), sub) for n in names])

    def take_cross(names, got, last=True):
        for n, l in zip(names, got):
            land[n] = l
            if last:
                half[n] = cross_sum("cross_sum_" + n, part[n], l, BIG[n], shape[n], place)

    def send_final(*names):
        return final_job([half[n] for n in names], [BIG[n] for n in names], [shape[n] for n in names])

    delta, new_m, new_v = {}, {}, {}

    reduced = {}

    def take_final(names, got):
        for n, g in zip(names, got):
            reduced[n] = g

    def update(n, jobs=()):
        res = adamw("adamw_" + n, wts[n][0], reduced[n], mom[n][0], var[n][0], jobs=jobs)
        (grads[n], delta[n], new_m[n], new_v[n]), job_outs = res if jobs else (res, [])
        return job_outs

    dw["w_ffn2_out"] = mm_tn("ffn2_dwout", a2, dh4b, BF16, scale=0.5)
    dgu2, (got,) = swiglu_bwd("ffn2_dact", dh4b, full["w_ffn2_out"], gu2, 0.5, jobs=[send_pair("w_ffn2_out")])
    take_pair(["w_ffn2_out"], got)
    dw["w_ffn2_in"], (got,) = mm_tn_pair("ffn2_dwin", n4, dgu2, BF16, jobs=[send_cross("w_ffn2_out")])
    take_cross(["w_ffn2_out"], got)
    dn4, (got_f, got_p) = mm_nt_pair("ffn2_dn", dgu2, full["w_ffn2_in"], F32,
                                     jobs=[send_final("w_ffn2_out"), send_pair("w_ffn2_in")])
    take_final(["w_ffn2_out"], got_f)
    take_pair(["w_ffn2_in"], got_p)
    dh3, dh3b, dg_ffn2 = rmsnorm_bwd("norm4_bwd", dn4, h3, r4, g_ffn2, dh4)

    dw["w_xo"] = mm_tn("xo_dw", o, dh3b, BF16)
    do = mm_nt("xo_dx", dh3b, full["w_xo"], BF16)
    dq, dk, dv = attn_bwd("attn_bwd", q, k, v, do)
    dkb, dvb = dk.astype(BF16), dv.astype(BF16)
    dw["w_xq"] = mm_tn("xq_dw", n3, dq, BF16)
    dn3 = mm_nt("xq_dx", dq, full["w_xq"], F32)
    dh2, dh2b, dg_xattn = rmsnorm_bwd("norm3_bwd", dn3, h2, r3, g_xattn, dh3)
    dw["w_xk"] = mm_tn("xk_dw", mn, dkb, BF16)
    dw["w_xv"] = mm_tn("xv_dw", mn, dvb, BF16)
    dmn_k = mm_nt("xk_dx", dkb, full["w_xk"], F32)
    dmn_v = mm_nt("xv_dx", dvb, full["w_xv"], F32)
    dg_mem = gain_grad("norm_mem_bwd", dmn_k, dmn_v, mem2, rm)

    dw["w_mix_out"] = mm_tn("mix_out_dw", ycat, dh2b, BF16)
    dycat = mm_nt("mix_out_dx", dh2b, full["w_mix_out"], BF16)
    attn_names = ["w_xo", "w_xq", "w_xk", "w_xv", "w_mix_out"]
    (dz, dsmall, dws, dbt), (got_c, got_p) = mixer_bwd(
        "mixer_bwd", z, dycat, conv_taps, conv_b, g_gm_v, w_s, b_t,
        jobs=[send_cross("w_ffn2_in", sub=(0, 2, 8)), send_pair(*attn_names)])
    take_cross(["w_ffn2_in"], got_c, last=False)
    take_pair(attn_names, got_p)
    dw["w_mix_in"], (got_c,) = mm_tn("mix_in_dw", n2, dz, BF16, jobs=[send_cross("w_ffn2_in", sub=(2, 3, 8))])
    take_cross(["w_ffn2_in"], got_c, last=False)
    dn2, (got_c, got_p) = mm_nt("mix_in_dx", dz, full["w_mix_in"], F32, tk=2560,
                                jobs=[send_cross("w_ffn2_in", sub=(5, 3, 8)), send_pair("w_mix_in")])
    take_cross(["w_ffn2_in"], got_c)
    take_pair(["w_mix_in"], got_p)
    (dh1, dh1b, dg_mix), (got_c,) = rmsnorm_bwd("norm2_bwd", dn2, h1, r2, g_mix, dh2, jobs=[send_cross("w_xo")])
    take_cross(["w_xo"], got_c)

    xa_names = ["w_xo", "w_xq", "w_xk", "w_xv"]
    dgu1, (got_c, got_f) = swiglu_bwd("ffn1_dact", dh1b, full["w_ffn1_out"], gu1, 0.5,
                                      jobs=[send_cross(*xa_names[1:]), send_final("w_ffn2_in")])
    take_cross(xa_names[1:], got_c)
    take_final(["w_ffn2_in"], got_f)
    mix_names = ["w_mix_out", "w_mix_in"]
    early = {"g_mix": dg_mix, "conv_w": dsmall[0:CONV_K], "conv_b": dsmall[3:4], "g_gm_v": dsmall[4:5],
             "w_spatial": dws, "b_spatial": dbt.T, "g_xattn": dg_xattn, "g_mem": dg_mem, "g_ffn2": dg_ffn2,
             "g_final": dg_final}
    dw["w_ffn1_in"], (got_c, got_f, (early_all,)) = mm_tn_pair(
        "ffn1_dwin", n1, dgu1, BF16,
        jobs=[send_cross(*mix_names), send_final(*xa_names), stack_job(_pack([early[n] for n in EARLY_SMALL]))])
    take_cross(mix_names, got_c)
    take_final(xa_names, got_f)
    for n in xa_names:
        update(n)
    dw["w_ffn1_out"], (got_p, got_f) = mm_tn("ffn1_dwout", a1, dh1b, BF16, scale=0.5,
                                             jobs=[send_pair("w_ffn1_in"), send_final(*mix_names)])
    take_pair(["w_ffn1_in"], got_p)
    take_final(mix_names, got_f)
    update("w_mix_out")

    dn1, (got_c, got_p) = mm_nt_pair("ffn1_dn", dgu1, full["w_ffn1_in"], F32,
                                     jobs=[send_cross("w_ffn1_in", sub=(0, 6, 8)), send_pair("w_ffn1_out")])
    take_cross(["w_ffn1_in"], got_c, last=False)
    take_pair(["w_ffn1_out"], got_p)
    (dx, _, dg_ffn1), (got_i, got_o) = rmsnorm_bwd(
        "norm1_bwd", dn1, x2, r1, g_ffn1, dh1,
        jobs=[send_cross("w_ffn1_in", sub=(6, 1, 8)), send_cross("w_ffn1_out", sub=(0, 1, 8))])
    take_cross(["w_ffn1_in"], got_i, last=False)
    take_cross(["w_ffn1_out"], got_o, last=False)
    got_i, got_o = update("w_ffn2_in", jobs=[send_cross("w_ffn1_in", sub=(7, 1, 8)),
                                             send_cross("w_ffn1_out", sub=(1, 2, 8))])
    take_cross(["w_ffn1_in"], got_i)
    take_cross(["w_ffn1_out"], got_o, last=False)
    got_o, got_f = update("w_ffn2_out", jobs=[send_cross("w_ffn1_out", sub=(3, 2, 8)), send_final("w_ffn1_in")])
    take_cross(["w_ffn1_out"], got_o, last=False)
    take_final(["w_ffn1_in"], got_f)
    got_o, (late_all,) = update("w_mix_in", jobs=[send_cross("w_ffn1_out", sub=(5, 3, 8)),
                                                  stack_job(_pack([dg_ffn1]))])
    take_cross(["w_ffn1_out"], got_o)
    (got_f,) = update("w_ffn1_in", jobs=[send_final("w_ffn1_out")])
    take_final(["w_ffn1_out"], got_f)
    update("w_ffn1_out")

    early_sum, late_sum = small_sum("small_sum", [early_all, late_all])
    for n, g in zip(EARLY_SMALL, _unpack(early_sum, [early[n].shape for n in EARLY_SMALL])):
        grads[n] = g
    grads["g_ffn1"] = _unpack(late_sum, [dg_ffn1.shape])[0]
    taps_cols = conv_w.shape[2]
    grads["conv_w"] = lax.dynamic_slice_in_dim(grads["conv_w"], blk * taps_cols, taps_cols, axis=1)
    packed = [_pack([src[n] for n in SMALL]) for src in (wts, grads, mom, var)]
    own_shapes = [wts[n].shape for n in SMALL]
    for dst, buf in zip((delta, new_m, new_v), adamw("adamw_small", *packed)[1:]):
        for n, a in zip(SMALL, _unpack(buf, own_shapes)):
            dst[n] = a

    loss = lax.psum(loss_blk[0, 0], ("x", "y", "c"))
    outs = [loss, dx[None]]
    for group in (grads, delta, new_m, new_v):
        outs += [group[n].reshape(wts[n].shape) for n in WEIGHTS]
    return tuple(outs)
```

```python
import math

import jax
import jax.numpy as jnp
from jax import lax
from jax.experimental import pallas as pl
from jax.experimental.pallas import tpu as pltpu

F32 = jnp.float32
BF16 = jnp.bfloat16
EPS = 1e-6
GROUP = 128
XA_HEADS = 4
CONV_K = 3
N_CHIPS = 4
VMEM_LIMIT_BYTES = 56 * 1024 * 1024

ADAM_LR = 0.001
ADAM_B1 = 0.9
ADAM_B2 = 0.999
ADAM_EPS = 1e-08
ADAM_WD = 0.01
ADAM_STEP = 10

MESH = pl.DeviceIdType.MESH
ANY = pl.BlockSpec(memory_space=pl.ANY)


def _tile(dim, pref, mult=128):
    if dim <= pref:
        return dim
    t = (pref // mult) * mult
    while t >= mult:
        if dim % t == 0:
            return t
        t -= mult
    raise ValueError(f"no tile for {dim} under {pref}")


def _params(sem):
    return pltpu.CompilerParams(dimension_semantics=sem, vmem_limit_bytes=VMEM_LIMIT_BYTES)


def _sds(shape, dtype):
    return jax.ShapeDtypeStruct(shape, dtype)


def _dot_nn(a, b):
    return jnp.dot(a, b, preferred_element_type=F32)


def _dot_nt(a, b):
    return lax.dot_general(a, b, (((1,), (1,)), ((), ())), preferred_element_type=F32)


def _dot_tn(a, b):
    return lax.dot_general(a, b, (((0,), (0,)), ((), ())), preferred_element_type=F32)


class Job:
    def __init__(self, inputs, out_shapes, aliases, sems, start, middle, finish):
        self.inputs, self.out_shapes, self.aliases, self.sems = inputs, out_shapes, aliases, sems
        self.start, self.middle, self.finish = start, middle, finish


def _place():
    x, y, c = lax.axis_index("x"), lax.axis_index("y"), lax.axis_index("c")
    chips = [(1 - x, y), (x, 1 - y), (1 - x, 1 - y)]
    return x, y, c, chips


def _ds(start, size, lane):
    if not isinstance(start, int):
        start = pl.multiple_of(start, 128 if lane else 16)
    return pl.ds(start, size)


WHOLE = (0, 1, 1)


def _window(ref, axis, shape, blk=None, half=None, sub=WHOLE):
    n = shape[axis] // N_CHIPS
    hs = shape[1 - axis] // 2
    idx = [slice(None), slice(None)]
    if blk is not None:
        idx[axis] = _ds(blk * n, n, axis == 1)
    first, count, pieces = sub
    ext = hs // pieces
    if half is not None:
        idx[1 - axis] = _ds(half * hs + first * ext, count * ext, axis == 0)
    elif pieces > 1:
        idx[1 - axis] = _ds(first * ext, count * ext, axis == 0)
    return ref.at[tuple(idx)]


def _remote(src, dst, send_sem, recv_sem, dev):
    return pltpu.make_async_remote_copy(src_ref=src, dst_ref=dst, send_sem=send_sem, recv_sem=recv_sem,
                                        device_id=dev, device_id_type=MESH)


def _full_shape(block_shape, axis):
    out = list(block_shape)
    out[axis] *= N_CHIPS
    return tuple(out)


def _half_all(shape, axis):
    out = list(shape)
    out[1 - axis] //= 2
    return tuple(out)


def _block(shape, axis):
    out = list(shape)
    out[axis] //= N_CHIPS
    return tuple(out)


def _half_block(shape, axis):
    return _half_all(_block(shape, axis), axis)


def gather_job(items):
    nw = len(items)
    shapes = [full.shape for full, _, _ in items]
    n_sem = 8

    def parts(sub):
        first, count, pieces = sub
        return (2 * first, count, 2 * pieces), (2 * first + count, count, 2 * pieces)

    def start(pos, ins, outs, sems):
        x, y, c, chips = pos
        for w, (_, ax, sub) in enumerate(items):
            mine = _window(outs[w], ax, shapes[w], blk=2 * x + y, half=c, sub=sub)
            for j in range(2):
                _remote(mine, mine, sems[0].at[n_sem * w + j], sems[1].at[n_sem * w + j], (*chips[j], c)).start()

    def middle(pos, ins, outs, sems):
        x, y, c, chips = pos
        for w, (_, ax, sub) in enumerate(items):
            for j in range(2):
                cx, cy = chips[j]
                landed = _window(outs[w], ax, shapes[w], blk=2 * cx + cy, half=c, sub=sub)
                _remote(landed, landed, sems[0].at[n_sem * w + j], sems[1].at[n_sem * w + j], (cx, cy, c)).wait_recv()
                part = _window(outs[w], ax, shapes[w], blk=2 * cx + cy, half=c, sub=parts(sub)[j])
                _remote(part, part, sems[0].at[n_sem * w + 2 + j], sems[1].at[n_sem * w + 2 + j],
                        (*chips[1 - j], c)).start()
                _remote(landed, landed, sems[0].at[n_sem * w + 4 + j], sems[1].at[n_sem * w + 4 + j],
                        (x, y, 1 - c)).start()

    def finish(pos, ins, outs, sems):
        x, y, c, chips = pos
        sib = (x, y, 1 - c)
        for w, (_, ax, sub) in enumerate(items):
            dx, dy = chips[2]
            for j in range(2):
                part = _window(outs[w], ax, shapes[w], blk=2 * dx + dy, half=c, sub=parts(sub)[j])
                cp = _remote(part, part, sems[0].at[n_sem * w + 2 + j], sems[1].at[n_sem * w + 2 + j], sib)
                cp.wait_recv()
                cp.wait_send()
            diag = _window(outs[w], ax, shapes[w], blk=2 * dx + dy, half=c, sub=sub)
            _remote(diag, diag, sems[0].at[n_sem * w + 6], sems[1].at[n_sem * w + 6], sib).start()
        for w, (_, ax, sub) in enumerate(items):
            for j, (cx, cy) in enumerate(chips):
                passed = _window(outs[w], ax, shapes[w], blk=2 * cx + cy, half=1 - c, sub=sub)
                cp = _remote(passed, passed, sems[0].at[n_sem * w + 4 + j], sems[1].at[n_sem * w + 4 + j], sib)
                cp.wait_recv()
                cp.wait_send()
            mine = _window(outs[w], ax, shapes[w], blk=2 * x + y, half=c, sub=sub)
            for j in range(2):
                _remote(mine, mine, sems[0].at[n_sem * w + j], sems[1].at[n_sem * w + j], sib).wait_send()

    sems = [pltpu.SemaphoreType.DMA((n_sem * nw,)), pltpu.SemaphoreType.DMA((n_sem * nw,))]
    return Job([full for full, _, _ in items], [_sds(full.shape, full.dtype) for full, _, _ in items],
               {w: w for w in range(nw)}, sems, start, middle, finish)


def pair_job(grads, axes, is_half=False):
    nw = len(grads)
    shapes = [g.shape for g in grads]

    def start(pos, ins, outs, sems):
        x, y, c, _ = pos
        for w in range(nw):
            src = ins[w] if is_half else _window(ins[w], axes[w], shapes[w], half=1 - c)
            _remote(src, outs[w], sems[0].at[w], sems[1].at[w], (x, y, 1 - c)).start()

    def finish(pos, ins, outs, sems):
        x, y, c, _ = pos
        for w in range(nw):
            cp = _remote(outs[w], outs[w], sems[0].at[w], sems[1].at[w], (x, y, 1 - c))
            cp.wait_recv()
            cp.wait_send()

    sems = [pltpu.SemaphoreType.DMA((nw,)), pltpu.SemaphoreType.DMA((nw,))]
    out_shapes = [_sds(s if is_half else _half_all(s, a), BF16) for s, a in zip(shapes, axes)]
    return Job(list(grads), out_shapes, {}, sems, start, None, finish)


def cross_job(items):
    nw = len(items)
    inputs, aliases = [], {}
    for w, (part, ax, shape, prev, sub) in enumerate(items):
        inputs.append(part)
        if prev is not None:
            aliases[len(inputs)] = w
            inputs.append(prev)

    def copies(pos, ins, outs, sems):
        x, y, c, chips = pos
        k = 0
        for w, (_, ax, shape, prev, sub) in enumerate(items):
            src = ins[k]
            k += 2 if prev is not None else 1
            for j, (cx, cy) in enumerate(chips):
                slot = _window(outs[w].at[j], ax, shape, sub=sub)
                yield (_remote(_window(src, ax, shape, blk=2 * cx + cy, sub=sub), slot,
                               sems[0].at[3 * w + j], sems[1].at[3 * w + j], (cx, cy, c)),
                       _remote(slot, slot, sems[0].at[3 * w + j], sems[1].at[3 * w + j], (cx, cy, c)))

    def start(pos, ins, outs, sems):
        for send, _ in copies(pos, ins, outs, sems):
            send.start()

    def finish(pos, ins, outs, sems):
        for send, recv in copies(pos, ins, outs, sems):
            recv.wait_recv()
            send.wait_send()

    sems = [pltpu.SemaphoreType.DMA((3 * nw,)), pltpu.SemaphoreType.DMA((3 * nw,))]
    out_shapes = [_sds((3,) + _half_block(shape, ax), BF16) for _, ax, shape, _, _ in items]
    return Job(inputs, out_shapes, aliases, sems, start, None, finish)


def final_job(blocks, axes, shapes):
    nw = len(blocks)

    def start(pos, ins, outs, sems):
        x, y, c, _ = pos
        for w in range(nw):
            mine = _window(outs[w], axes[w], shapes[w], half=c)
            _remote(mine, mine, sems[0].at[w], sems[1].at[w], (x, y, 1 - c)).start()

    def finish(pos, ins, outs, sems):
        x, y, c, _ = pos
        for w in range(nw):
            theirs = _window(outs[w], axes[w], shapes[w], half=1 - c)
            cp = _remote(theirs, theirs, sems[0].at[w], sems[1].at[w], (x, y, 1 - c))
            cp.wait_recv()
            cp.wait_send()

    sems = [pltpu.SemaphoreType.DMA((nw,)), pltpu.SemaphoreType.DMA((nw,))]
    return Job(list(blocks), [_sds(b.shape, b.dtype) for b in blocks], {w: w for w in range(nw)}, sems, start, None,
               finish)


def stack_job(small):
    def peers(pos):
        x, y, c, _ = pos
        for k in range(1, 8):
            yield k - 1, (1 - x if k & 4 else x, 1 - y if k & 2 else y, 1 - c if k & 1 else c)

    def start(pos, ins, outs, sems):
        x, y, c, _ = pos
        mine = outs[0].at[4 * x + 2 * y + c]
        pltpu.make_async_copy(ins[0], mine, sems[2]).start()
        for k, dev in peers(pos):
            _remote(ins[0], mine, sems[0].at[k], sems[1].at[k], dev).start()

    def finish(pos, ins, outs, sems):
        x, y, c, _ = pos
        for k, (px, py, pc) in peers(pos):
            slot = outs[0].at[4 * px + 2 * py + pc]
            cp = _remote(slot, slot, sems[0].at[k], sems[1].at[k], (px, py, pc))
            cp.wait_recv()
            cp.wait_send()
        pltpu.make_async_copy(ins[0], outs[0].at[4 * x + 2 * y + c], sems[2]).wait()

    sems = [pltpu.SemaphoreType.DMA((7,)), pltpu.SemaphoreType.DMA((7,)), pltpu.SemaphoreType.DMA]
    return Job([small], [_sds((8,) + small.shape, small.dtype)], {}, sems, start, None, finish)


def columns_job(block):
    cols = block.shape[1]
    place = lambda out, b: out.at[:, _ds(b * cols, cols, True)]

    def start(pos, ins, outs, sems):
        x, y, c, chips = pos
        pltpu.make_async_copy(ins[0], place(outs[0], 2 * x + y), sems[2]).start()
        for j, (cx, cy) in enumerate(chips):
            _remote(ins[0], place(outs[0], 2 * x + y), sems[0].at[j], sems[1].at[j], (cx, cy, c)).start()

    def finish(pos, ins, outs, sems):
        x, y, c, chips = pos
        for j, (cx, cy) in enumerate(chips):
            got = place(outs[0], 2 * cx + cy)
            cp = _remote(got, got, sems[0].at[j], sems[1].at[j], (cx, cy, c))
            cp.wait_recv()
            cp.wait_send()
        pltpu.make_async_copy(ins[0], place(outs[0], 2 * x + y), sems[2]).wait()

    sems = [pltpu.SemaphoreType.DMA((3,)), pltpu.SemaphoreType.DMA((3,)), pltpu.SemaphoreType.DMA]
    return Job([block], [_sds((block.shape[0], N_CHIPS * cols), block.dtype)], {}, sems, start, None, finish)


def _call(name, body, grid, in_specs, out_specs, out_shape, args, scratch=(), sem=None, jobs=(), place=None):
    n_in, n_out, n_sc = len(args), len(out_shape), len(scratch)
    lead = [] if place is None else [place]

    def launch(fn, in_specs, out_specs, out_shape, scratch, aliases, sem, operands):
        if place is None:
            return pl.pallas_call(
                fn, name=name, grid=grid, in_specs=in_specs, out_specs=out_specs, out_shape=out_shape,
                scratch_shapes=scratch, input_output_aliases=aliases, compiler_params=_params(sem))(*operands)
        spec = pltpu.PrefetchScalarGridSpec(num_scalar_prefetch=1, grid=grid, in_specs=in_specs,
                                            out_specs=out_specs, scratch_shapes=scratch)
        return pl.pallas_call(
            lambda p_ref, *refs: fn(*refs), name=name, grid_spec=spec, out_shape=out_shape,
            input_output_aliases={k + 1: v for k, v in aliases.items()}, compiler_params=_params(sem),
        )(place, *operands)

    if not jobs:
        outs = launch(body, list(in_specs), list(out_specs), list(out_shape), list(scratch), {}, sem, args)
        return list(outs), []

    total = math.prod(grid) if grid else 1
    mid = min(total - 1, (2 * total) // 3)

    def split(refs, start, counts):
        out = []
        for n in counts:
            out.append(refs[start:start + n])
            start += n
        return out, start

    def wrapped(*refs):
        c_in = refs[:n_in]
        j_ins, p = split(refs, n_in, [len(j.inputs) for j in jobs])
        c_out = refs[p:p + n_out]
        j_outs, p = split(refs, p + n_out, [len(j.out_shapes) for j in jobs])
        c_sc = refs[p:p + n_sc]
        j_sems, p = split(refs, p + n_sc, [len(j.sems) for j in jobs])
        pos = _place()
        step = 0
        for axis, extent in enumerate(grid):
            step = step * extent + pl.program_id(axis)

        def run(phase):
            for j, ins, outs, sems in zip(jobs, j_ins, j_outs, j_sems):
                fn = getattr(j, phase)
                if fn is not None:
                    fn(pos, ins, outs, sems)

        if total == 1:
            run("start")
            body(*c_in, *c_out, *c_sc)
            run("middle")
            run("finish")
            return
        pl.when(step == 0)(lambda: run("start"))
        body(*c_in, *c_out, *c_sc)
        if any(j.middle is not None for j in jobs):
            pl.when(step == mid)(lambda: run("middle"))
        pl.when(step == total - 1)(lambda: run("finish"))

    aliases, in_at, out_at = {}, n_in, n_out
    for j in jobs:
        for src, dst in j.aliases.items():
            aliases[in_at + src] = out_at + dst
        in_at += len(j.inputs)
        out_at += len(j.out_shapes)
    outs = launch(
        wrapped, list(in_specs) + [ANY] * (in_at - n_in), list(out_specs) + [ANY] * (out_at - n_out),
        list(out_shape) + [s for j in jobs for s in j.out_shapes],
        list(scratch) + [s for j in jobs for s in j.sems], aliases, ("arbitrary",) * len(grid),
        [*args, *[a for j in jobs for a in j.inputs]])
    job_outs, p = split(outs, n_out, [len(j.out_shapes) for j in jobs])
    return list(outs[:n_out]), [list(o) for o in job_outs]


def comm_only(name, jobs):
    def body(dummy_ref, out_ref):
        out_ref[...] = dummy_ref[...]

    dummy = jnp.zeros((8, 128), F32)
    spec = pl.BlockSpec((8, 128), lambda: (0, 0))
    return _call(name, body, (), [spec], [spec], [_sds((8, 128), F32)], [dummy], jobs=jobs)[1]


def _ret(outs, job_outs, jobs, single=True):
    res = outs[0] if single else outs
    return (res, job_outs) if jobs else res


def rmsnorm_fwd(name, x, g):
    s, d = x.shape
    tm = _tile(s, 512, 8)

    def body(x_ref, g_ref, n_ref, r_ref):
        xv = x_ref[...]
        r = lax.rsqrt(jnp.mean(xv * xv, axis=-1, keepdims=True) + EPS)
        n_ref[...] = (xv * r * g_ref[...]).astype(BF16)
        r_ref[...] = r

    row = lambda i: (i, 0)
    return _call(
        name, body, (s // tm,),
        [pl.BlockSpec((tm, d), row), pl.BlockSpec((1, d), lambda i: (0, 0))],
        [pl.BlockSpec((tm, d), row), pl.BlockSpec((tm, 1), row)],
        [_sds((s, d), BF16), _sds((s, 1), F32)], [x, g], sem=("arbitrary",))[0]


def rmsnorm_bwd(name, dn, x, r, g, dh_in, jobs=()):
    s, d = x.shape
    tm = _tile(s, 512, 8)

    def body(dn_ref, x_ref, r_ref, g_ref, dh_ref, out_ref, outb_ref, dg_ref):
        i = pl.program_id(0)
        xh = x_ref[...] * r_ref[...]
        dnv = dn_ref[...]
        dxh = dnv * g_ref[...]
        dx = r_ref[...] * (dxh - xh * jnp.mean(dxh * xh, axis=-1, keepdims=True))
        out = dh_ref[...] + dx
        out_ref[...] = out
        outb_ref[...] = out.astype(BF16)
        part = jnp.sum(dnv * xh, axis=0, keepdims=True)

        @pl.when(i == 0)
        def _():
            dg_ref[...] = part

        @pl.when(i > 0)
        def _():
            dg_ref[...] += part

    row = lambda i: (i, 0)
    fixed = lambda i: (0, 0)
    outs, job_outs = _call(
        name, body, (s // tm,),
        [pl.BlockSpec((tm, d), row), pl.BlockSpec((tm, d), row), pl.BlockSpec((tm, 1), row),
         pl.BlockSpec((1, d), fixed), pl.BlockSpec((tm, d), row)],
        [pl.BlockSpec((tm, d), row), pl.BlockSpec((tm, d), row), pl.BlockSpec((1, d), fixed)],
        [_sds((s, d), F32), _sds((s, d), BF16), _sds((1, d), F32)], [dn, x, r, g, dh_in],
        sem=("arbitrary",), jobs=jobs)
    return _ret(outs, job_outs, jobs, single=False)


def gain_grad(name, dn_a, dn_b, x, r):
    s, d = x.shape
    tm = _tile(s, 512, 8)

    def body(a_ref, b_ref, x_ref, r_ref, dg_ref):
        i = pl.program_id(0)
        part = jnp.sum((a_ref[...] + b_ref[...]) * (x_ref[...] * r_ref[...]), axis=0, keepdims=True)

        @pl.when(i == 0)
        def _():
            dg_ref[...] = part

        @pl.when(i > 0)
        def _():
            dg_ref[...] += part

    row = lambda i: (i, 0)
    return _call(
        name, body, (s // tm,),
        [pl.BlockSpec((tm, d), row), pl.BlockSpec((tm, d), row), pl.BlockSpec((tm, d), row),
         pl.BlockSpec((tm, 1), row)],
        [pl.BlockSpec((1, d), lambda i: (0, 0))], [_sds((1, d), F32)], [dn_a, dn_b, x, r],
        sem=("arbitrary",))[0][0]


def loss_head(name, h, g, target):
    s, d = h.shape
    tm = _tile(s, 512, 8)
    nsteps = s // tm

    def body(h_ref, g_ref, t_ref, loss_ref, dh_ref, dhb_ref, dg_ref, sq_ref):
        i = pl.program_id(0)
        hv = h_ref[...]
        gv = g_ref[...]
        r = lax.rsqrt(jnp.mean(hv * hv, axis=-1, keepdims=True) + EPS)
        xh = hv * r
        err = xh * gv - t_ref[...]
        dy = err * (1.0 / d)
        dxh = dy * gv
        dh = r * (dxh - xh * jnp.mean(dxh * xh, axis=-1, keepdims=True))
        dh_ref[...] = dh
        dhb_ref[...] = dh.astype(BF16)
        dg_part = jnp.sum(dy * xh, axis=0, keepdims=True)
        sq_part = jnp.sum(err * err, axis=0, keepdims=True)

        @pl.when(i == 0)
        def _():
            dg_ref[...] = dg_part
            sq_ref[...] = sq_part

        @pl.when(i > 0)
        def _():
            dg_ref[...] += dg_part
            sq_ref[...] += sq_part

        @pl.when(i == nsteps - 1)
        def _():
            total = jnp.sum(sq_ref[...], axis=-1, keepdims=True) * (0.5 / d)
            loss_ref[...] = jnp.broadcast_to(total, loss_ref.shape)

    row = lambda i: (i, 0)
    fixed = lambda i: (0, 0)
    return _call(
        name, body, (nsteps,),
        [pl.BlockSpec((tm, d), row), pl.BlockSpec((1, d), fixed), pl.BlockSpec((tm, d), row)],
        [pl.BlockSpec((8, 128), fixed), pl.BlockSpec((tm, d), row), pl.BlockSpec((tm, d), row),
         pl.BlockSpec((1, d), fixed)],
        [_sds((8, 128), F32), _sds((s, d), F32), _sds((s, d), BF16), _sds((1, d), F32)], [h, g, target],
        scratch=[pltpu.VMEM((1, d), F32)], sem=("arbitrary",))[0]


def _mm(name, grid, in_arrays, in_specs, out_shapes, out_specs, acc_tile, dot, epilogue, jobs=(), place=None):
    nk = grid[2]
    n_in = len(in_arrays)
    n_out = len(out_shapes)

    def body(*refs):
        ins, outs = refs[:n_in], refs[n_in:n_in + n_out]
        if nk == 1:
            epilogue(dot(*ins), ins, outs)
            return
        acc = refs[n_in + n_out]
        k = pl.program_id(2)

        @pl.when(k == 0)
        def _():
            acc[...] = dot(*ins)

        @pl.when(jnp.logical_and(k > 0, k < nk - 1))
        def _():
            acc[...] += dot(*ins)

        @pl.when(k == nk - 1)
        def _():
            epilogue(acc[...] + dot(*ins), ins, outs)

    scratch = [pltpu.VMEM(acc_tile, F32)] if nk > 1 else []
    outs, job_outs = _call(name, body, grid, in_specs, out_specs, out_shapes, in_arrays, scratch=scratch,
                           sem=("parallel", "parallel", "arbitrary"), jobs=jobs, place=place)
    return _ret(outs, job_outs, jobs)


def _store(scale, dtype):
    def epilogue(acc, ins, outs):
        outs[0][...] = (acc * scale if scale != 1.0 else acc).astype(dtype)
    return epilogue


def mm_nn(name, a, w, out_dtype, tm=1024, tn=1024, tk=2048, jobs=()):
    m, kd = a.shape
    n = w.shape[1]
    tm, tn, tk = _tile(m, tm, 8), _tile(n, tn), _tile(kd, tk)
    return _mm(
        name, (n // tn, m // tm, kd // tk), [a, w],
        [pl.BlockSpec((tm, tk), lambda j, i, k: (i, k)), pl.BlockSpec((tk, tn), lambda j, i, k: (k, j))],
        [_sds((m, n), out_dtype)], [pl.BlockSpec((tm, tn), lambda j, i, k: (i, j))], (tm, tn),
        lambda a_ref, w_ref: _dot_nn(a_ref[...], w_ref[...]), _store(1.0, out_dtype), jobs)


def mm_nn_resid(name, a, w, x, scale, tm=1024, tn=1024, tk=1408, jobs=()):
    m, kd = a.shape
    n = w.shape[1]
    tm, tn, tk = _tile(m, tm, 8), _tile(n, tn), _tile(kd, tk)

    def epilogue(acc, ins, outs):
        outs[0][...] = ins[2][...] + scale * acc

    return _mm(
        name, (n // tn, m // tm, kd // tk), [a, w, x],
        [pl.BlockSpec((tm, tk), lambda j, i, k: (i, k)), pl.BlockSpec((tk, tn), lambda j, i, k: (k, j)),
         pl.BlockSpec((tm, tn), lambda j, i, k: (i, j))],
        [_sds((m, n), F32)], [pl.BlockSpec((tm, tn), lambda j, i, k: (i, j))], (tm, tn),
        lambda a_ref, w_ref, x_ref: _dot_nn(a_ref[...], w_ref[...]), epilogue, jobs)


def mm_nt(name, a, w, out_dtype, scale=1.0, tm=1024, tn=1024, tk=2048, jobs=()):
    m, kd = a.shape
    n = w.shape[0]
    tm, tn, tk = _tile(m, tm, 8), _tile(n, tn), _tile(kd, tk)
    return _mm(
        name, (n // tn, m // tm, kd // tk), [a, w],
        [pl.BlockSpec((tm, tk), lambda j, i, k: (i, k)), pl.BlockSpec((tn, tk), lambda j, i, k: (j, k))],
        [_sds((m, n), out_dtype)], [pl.BlockSpec((tm, tn), lambda j, i, k: (i, j))], (tm, tn),
        lambda a_ref, w_ref: _dot_nt(a_ref[...], w_ref[...]), _store(scale, out_dtype), jobs)


def mm_nt_pair(name, a3, w, out_dtype, tm=1024, tn=1024, tk=2816, jobs=()):
    _, m, f = a3.shape
    n = w.shape[0]
    tm, tn, tk = _tile(m, tm, 8), _tile(n, tn), _tile(f, tk)
    nkf = f // tk
    return _mm(
        name, (n // tn, m // tm, 2 * nkf), [a3, w],
        [pl.BlockSpec((None, tm, tk), lambda j, i, k: (k // nkf, i, k % nkf)),
         pl.BlockSpec((tn, tk), lambda j, i, k: (j, k))],
        [_sds((m, n), out_dtype)], [pl.BlockSpec((tm, tn), lambda j, i, k: (i, j))], (tm, tn),
        lambda a_ref, w_ref: _dot_nt(a_ref[...], w_ref[...]), _store(1.0, out_dtype), jobs)


def mm_tn(name, a, b, out_dtype, scale=1.0, tm=1024, tn=1024, tk=4096, jobs=()):
    kd, m = a.shape
    n = b.shape[1]
    tm, tn, tk = _tile(m, tm), _tile(n, tn), _tile(kd, tk, 16)
    return _mm(
        name, (n // tn, m // tm, kd // tk), [a, b],
        [pl.BlockSpec((tk, tm), lambda j, i, k: (k, i)), pl.BlockSpec((tk, tn), lambda j, i, k: (k, j))],
        [_sds((m, n), out_dtype)], [pl.BlockSpec((tm, tn), lambda j, i, k: (i, j))], (tm, tn),
        lambda a_ref, b_ref: _dot_tn(a_ref[...], b_ref[...]), _store(scale, out_dtype), jobs)


def mm_tn_pair(name, a, b3, out_dtype, tm=1024, tn=512, tk=4096, jobs=()):
    kd, m = a.shape
    f = b3.shape[2]
    tm, tn, tk = _tile(m, tm), _tile(f, tn), _tile(kd, tk, 16)
    nf = f // tn
    return _mm(
        name, (m // tm, 2 * nf, kd // tk), [a, b3],
        [pl.BlockSpec((tk, tm), lambda i, j, k: (k, i)),
         pl.BlockSpec((None, tk, tn), lambda i, j, k: (j // nf, k, j % nf))],
        [_sds((m, 2 * f), out_dtype)], [pl.BlockSpec((tm, tn), lambda i, j, k: (i, j))], (tm, tn),
        lambda a_ref, b_ref: _dot_tn(a_ref[...], b_ref[...]), _store(1.0, out_dtype), jobs)


def mm_tn_pair_half(name, a, b3, out_dtype, place, mine, tm=1024, tn=512, tk=4096, jobs=()):
    kd, m = a.shape
    f = b3.shape[2]
    tm, tn, tk = _tile(m // 2, tm), _tile(f, tn), _tile(kd, tk, 16)
    nf, nbm = f // tn, m // 2 // tm
    which = (lambda p: p[1]) if mine else (lambda p: 1 - p[1])
    return _mm(
        name, (nbm, 2 * nf, kd // tk), [a, b3],
        [pl.BlockSpec((tk, tm), lambda i, j, k, p: (k, i + which(p) * nbm)),
         pl.BlockSpec((None, tk, tn), lambda i, j, k, p: (j // nf, k, j % nf))],
        [_sds((m // 2, 2 * f), out_dtype)], [pl.BlockSpec((tm, tn), lambda i, j, k, p: (i, j))], (tm, tn),
        lambda a_ref, b_ref: _dot_tn(a_ref[...], b_ref[...]), _store(1.0, out_dtype), jobs, place)


def swiglu_fwd(name, n, w_in, tm=1024, tn=512, jobs=()):
    s, d = n.shape
    f = w_in.shape[1] // 2
    tm, tn = _tile(s, tm, 8), _tile(f, tn)
    nf = f // tn

    def body(n_ref, wg_ref, wu_ref, gu_ref, a_ref):
        nv = n_ref[...]
        g = _dot_nn(nv, wg_ref[...])
        u = _dot_nn(nv, wu_ref[...])
        gu_ref[0] = g.astype(BF16)
        gu_ref[1] = u.astype(BF16)
        a_ref[...] = (g * jax.nn.sigmoid(g) * u).astype(BF16)

    outs, job_outs = _call(
        name, body, (nf, s // tm),
        [pl.BlockSpec((tm, d), lambda j, i: (i, 0)), pl.BlockSpec((d, tn), lambda j, i: (0, j)),
         pl.BlockSpec((d, tn), lambda j, i: (0, j + nf))],
        [pl.BlockSpec((2, tm, tn), lambda j, i: (0, i, j)), pl.BlockSpec((tm, tn), lambda j, i: (i, j))],
        [_sds((2, s, f), BF16), _sds((s, f), BF16)], [n, w_in, w_in], sem=("parallel", "parallel"), jobs=jobs)
    return _ret(outs, job_outs, jobs, single=False)


def swiglu_bwd(name, dh, w_out, gu, scale, tm=1024, tn=512, jobs=()):
    s, d = dh.shape
    f = w_out.shape[0]
    tm, tn = _tile(s, tm, 8), _tile(f, tn)

    sub = _tile(tm, 256, 8)

    def body(dh_ref, w_ref, gu_ref, out_ref):
        for lo in range(0, tm, sub):
            rows = slice(lo, lo + sub)
            da = _dot_nt(dh_ref[rows, :], w_ref[...]) * scale
            g = gu_ref[0, rows, :].astype(F32)
            u = gu_ref[1, rows, :].astype(F32)
            sg = jax.nn.sigmoid(g)
            out_ref[0, rows, :] = (da * u * (sg * (1.0 + g * (1.0 - sg)))).astype(BF16)
            out_ref[1, rows, :] = (da * (g * sg)).astype(BF16)

    outs, job_outs = _call(
        name, body, (f // tn, s // tm),
        [pl.BlockSpec((tm, d), lambda j, i: (i, 0)), pl.BlockSpec((tn, d), lambda j, i: (j, 0)),
         pl.BlockSpec((2, tm, tn), lambda j, i: (0, i, j))],
        [pl.BlockSpec((2, tm, tn), lambda j, i: (0, i, j))],
        [_sds((2, s, f), BF16)], [dh, w_out, gu], sem=("parallel", "parallel"), jobs=jobs)
    return _ret(outs, job_outs, jobs)


HALO = 16


def _conv_inputs(z_ref, hgc_ref, hhc_ref, i, cw, tm):
    gc = z_ref[:, cw:2 * cw].astype(F32)
    hc = z_ref[:, 2 * cw:3 * cw].astype(F32)
    cin = gc * hc
    halo = hgc_ref[...].astype(F32) * hhc_ref[...].astype(F32) * (i > 0).astype(F32)
    row = lax.broadcasted_iota(jnp.int32, (tm, cw), 0)
    x1 = jnp.where(row == 0, halo[HALO - 1:HALO], pltpu.roll(cin, 1, 0))
    x2 = jnp.where(row == 0, halo[HALO - 2:HALO - 1], jnp.where(row == 1, halo[HALO - 1:HALO], pltpu.roll(cin, 2, 0)))
    return gc, hc, cin, x1, x2


def _tril(w):
    r = lax.broadcasted_iota(jnp.int32, w.shape, 0)
    c = lax.broadcasted_iota(jnp.int32, w.shape, 1)
    return jnp.where(r >= c, w, jnp.zeros_like(w))


def mixer_fwd(name, z, conv_w, conv_b, g_v, w_s, b_t, tm=256, jobs=()):
    s, zc = z.shape
    cw = conv_w.shape[1]
    gw = g_v.shape[1]
    heads = gw // GROUP
    tm = _tile(s, tm)
    hb = tm // HALO

    def body(z_ref, hgc_ref, hhc_ref, cw_ref, cb_ref, gv_ref, ws_ref, bt_ref, y_ref):
        i = pl.program_id(0)
        _, _, cin, x1, x2 = _conv_inputs(z_ref, hgc_ref, hhc_ref, i, cw, tm)
        cv = cb_ref[...] + cw_ref[2:3, :] * cin + cw_ref[1:2, :] * x1 + cw_ref[0:1, :] * x2
        y_ref[:, 0:cw] = (z_ref[:, 0:cw].astype(F32) * cv).astype(BF16)
        for h in range(heads):
            lo = h * GROUP
            vh = z_ref[:, 3 * cw + gw + lo:3 * cw + gw + lo + GROUP].astype(F32)
            rv = lax.rsqrt(jnp.mean(vh * vh, axis=-1, keepdims=True) + EPS)
            vn = (vh * rv * gv_ref[:, lo:lo + GROUP]).astype(BF16)
            w = _tril(ws_ref[h]).astype(BF16)
            for n in range(tm // GROUP):
                rows = slice(n * GROUP, (n + 1) * GROUP)
                sg = _dot_nn(w, vn[rows]) + bt_ref[:, h:h + 1]
                u = z_ref[rows, 3 * cw + lo:3 * cw + lo + GROUP].astype(F32)
                y_ref[rows, cw + lo:cw + lo + GROUP] = (u * sg).astype(BF16)

    fixed2 = lambda i: (0, 0)
    outs, job_outs = _call(
        name, body, (s // tm,),
        [pl.BlockSpec((tm, zc), lambda i: (i, 0)),
         pl.BlockSpec((HALO, cw), lambda i: (jnp.maximum(i * hb - 1, 0), 1)),
         pl.BlockSpec((HALO, cw), lambda i: (jnp.maximum(i * hb - 1, 0), 2)),
         pl.BlockSpec(conv_w.shape, fixed2), pl.BlockSpec(conv_b.shape, fixed2),
         pl.BlockSpec(g_v.shape, fixed2), pl.BlockSpec(w_s.shape, lambda i: (0, 0, 0)),
         pl.BlockSpec(b_t.shape, fixed2)],
        [pl.BlockSpec((tm, cw + gw), lambda i: (i, 0))], [_sds((s, cw + gw), BF16)],
        [z, z, z, conv_w, conv_b, g_v, w_s, b_t], sem=("arbitrary",), jobs=jobs)
    return _ret(outs, job_outs, jobs)


def mixer_bwd(name, z, dy, conv_w, conv_b, g_v, w_s, b_t, tm=256, jobs=()):
    s, zc = z.shape
    cw = conv_w.shape[1]
    gw = g_v.shape[1]
    heads = gw // GROUP
    tm = _tile(s, tm)
    hb = tm // HALO
    nsteps = s // tm
    last_halo = s // HALO - 1

    def body(z_ref, hgc_ref, hhc_ref, ngb_ref, dy_ref, ndy_ref, cw_ref, cb_ref, gv_ref, ws_ref, bt_ref,
             dz_ref, sm_ref, dws_ref, dbt_ref, dsg_ref):
        i = pl.program_id(0)

        @pl.when(i == 0)
        def _():
            sm_ref[...] = jnp.zeros_like(sm_ref)
            dws_ref[...] = jnp.zeros_like(dws_ref)
            dsg_ref[...] = jnp.zeros_like(dsg_ref)

        gc, hc, cin, x1, x2 = _conv_inputs(z_ref, hgc_ref, hhc_ref, i, cw, tm)
        w0, w1, w2 = cw_ref[0:1, :], cw_ref[1:2, :], cw_ref[2:3, :]
        cv = cb_ref[...] + w2 * cin + w1 * x1 + w0 * x2
        gb = z_ref[:, 0:cw].astype(F32)
        dyc = dy_ref[:, 0:cw].astype(F32)
        dz_ref[:, 0:cw] = (dyc * cv).astype(BF16)
        dcv = dyc * gb
        nxt = ndy_ref[...].astype(F32) * ngb_ref[...].astype(F32) * (i < nsteps - 1).astype(F32)
        row = lax.broadcasted_iota(jnp.int32, (tm, cw), 0)
        d1 = jnp.where(row == tm - 1, nxt[0:1], pltpu.roll(dcv, tm - 1, 0))
        d2 = jnp.where(row == tm - 1, nxt[1:2], jnp.where(row == tm - 2, nxt[0:1], pltpu.roll(dcv, tm - 2, 0)))
        dcin = w2 * dcv + w1 * d1 + w0 * d2
        dz_ref[:, cw:2 * cw] = (dcin * hc).astype(BF16)
        dz_ref[:, 2 * cw:3 * cw] = (dcin * gc).astype(BF16)
        sm_ref[0:1, :] += jnp.sum(dcv * x2, axis=0, keepdims=True)
        sm_ref[1:2, :] += jnp.sum(dcv * x1, axis=0, keepdims=True)
        sm_ref[2:3, :] += jnp.sum(dcv * cin, axis=0, keepdims=True)
        sm_ref[3:4, :] += jnp.sum(dcv, axis=0, keepdims=True)

        for h in range(heads):
            lo = h * GROUP
            vcol = slice(3 * cw + gw + lo, 3 * cw + gw + lo + GROUP)
            ucol = slice(3 * cw + lo, 3 * cw + lo + GROUP)
            vh = z_ref[:, vcol].astype(F32)
            rv = lax.rsqrt(jnp.mean(vh * vh, axis=-1, keepdims=True) + EPS)
            xh = vh * rv
            gvh = gv_ref[:, lo:lo + GROUP]
            vn = (xh * gvh).astype(BF16)
            w = _tril(ws_ref[h]).astype(BF16)
            dgv = jnp.zeros((1, GROUP), F32)
            for n in range(tm // GROUP):
                rows = slice(n * GROUP, (n + 1) * GROUP)
                sg = _dot_nn(w, vn[rows]) + bt_ref[:, h:h + 1]
                dyg = dy_ref[rows, cw + lo:cw + lo + GROUP].astype(F32)
                dsg = dyg * z_ref[rows, ucol].astype(F32)
                dz_ref[rows, ucol] = (dyg * sg).astype(BF16)
                dsgb = dsg.astype(BF16)
                dvn = _dot_tn(w, dsgb)
                dws_ref[h] += _dot_nt(dsgb, vn[rows])
                dsg_ref[:, lo:lo + GROUP] += dsg
                xhc = xh[rows]
                dgv = dgv + jnp.sum(dvn * xhc, axis=0, keepdims=True)
                dxh = dvn * gvh
                dv = rv[rows] * (dxh - xhc * jnp.mean(dxh * xhc, axis=-1, keepdims=True))
                dz_ref[rows, vcol] = dv.astype(BF16)
            sm_ref[4:5, lo:lo + GROUP] += dgv

        @pl.when(i == nsteps - 1)
        def _():
            for h in range(heads):
                dws_ref[h] = _tril(dws_ref[h])
                dbt_ref[:, h:h + 1] = jnp.sum(dsg_ref[:, h * GROUP:(h + 1) * GROUP], axis=-1, keepdims=True)

    fixed2 = lambda i: (0, 0)
    fixed3 = lambda i: (0, 0, 0)
    prev = lambda col: (lambda i: (jnp.maximum(i * hb - 1, 0), col))
    nxt_blk = lambda i: (jnp.minimum((i + 1) * hb, last_halo), 0)
    outs, job_outs = _call(
        name, body, (nsteps,),
        [pl.BlockSpec((tm, zc), lambda i: (i, 0)),
         pl.BlockSpec((HALO, cw), prev(1)), pl.BlockSpec((HALO, cw), prev(2)),
         pl.BlockSpec((HALO, cw), nxt_blk),
         pl.BlockSpec((tm, cw + gw), lambda i: (i, 0)), pl.BlockSpec((HALO, cw), nxt_blk),
         pl.BlockSpec(conv_w.shape, fixed2), pl.BlockSpec(conv_b.shape, fixed2),
         pl.BlockSpec(g_v.shape, fixed2), pl.BlockSpec(w_s.shape, fixed3), pl.BlockSpec(b_t.shape, fixed2)],
        [pl.BlockSpec((tm, zc), lambda i: (i, 0)), pl.BlockSpec((8, cw), fixed2),
         pl.BlockSpec(w_s.shape, fixed3), pl.BlockSpec(b_t.shape, fixed2)],
        [_sds((s, zc), BF16), _sds((8, cw), F32), _sds(w_s.shape, F32), _sds(b_t.shape, F32)],
        [z, z, z, z, dy, dy, conv_w, conv_b, g_v, w_s, b_t],
        scratch=[pltpu.VMEM((GROUP, gw), F32)], sem=("arbitrary",), jobs=jobs)
    return _ret(outs, job_outs, jobs, single=False)


def _softmax_rows(sc):
    e = jnp.exp(sc - jnp.max(sc, axis=-1, keepdims=True))
    return e / jnp.sum(e, axis=-1, keepdims=True)


def attn_fwd(name, q, k, v, tm=512):
    s, d = q.shape
    m = k.shape[0]
    hd = d // XA_HEADS
    scale = hd ** -0.5
    tm = _tile(s, tm, 8)

    def body(q_ref, k_ref, v_ref, o_ref):
        for h in range(XA_HEADS):
            cols = slice(h * hd, (h + 1) * hd)
            p = _softmax_rows(_dot_nt(q_ref[:, cols], k_ref[:, cols]) * scale)
            o_ref[:, cols] = _dot_nn(p.astype(BF16), v_ref[:, cols]).astype(BF16)

    return _call(
        name, body, (s // tm,),
        [pl.BlockSpec((tm, d), lambda i: (i, 0)), pl.BlockSpec((m, d), lambda i: (0, 0)),
         pl.BlockSpec((m, d), lambda i: (0, 0))],
        [pl.BlockSpec((tm, d), lambda i: (i, 0))], [_sds((s, d), BF16)], [q, k, v], sem=("arbitrary",))[0][0]


def attn_bwd(name, q, k, v, do, tm=512):
    s, d = q.shape
    m = k.shape[0]
    hd = d // XA_HEADS
    scale = hd ** -0.5
    tm = _tile(s, tm, 8)

    def body(q_ref, k_ref, v_ref, do_ref, dq_ref, dk_ref, dv_ref):
        i = pl.program_id(0)

        @pl.when(i == 0)
        def _():
            dk_ref[...] = jnp.zeros_like(dk_ref)
            dv_ref[...] = jnp.zeros_like(dv_ref)

        for h in range(XA_HEADS):
            cols = slice(h * hd, (h + 1) * hd)
            qh = q_ref[:, cols]
            doh = do_ref[:, cols]
            p = _softmax_rows(_dot_nt(qh, k_ref[:, cols]) * scale)
            dp = _dot_nt(doh, v_ref[:, cols])
            ds = (p * (dp - jnp.sum(dp * p, axis=-1, keepdims=True)) * scale).astype(BF16)
            dq_ref[:, cols] = _dot_nn(ds, k_ref[:, cols]).astype(BF16)
            dk_ref[:, cols] += _dot_tn(ds, qh)
            dv_ref[:, cols] += _dot_tn(p.astype(BF16), doh)

    row = lambda i: (i, 0)
    fixed = lambda i: (0, 0)
    return _call(
        name, body, (s // tm,),
        [pl.BlockSpec((tm, d), row), pl.BlockSpec((m, d), fixed), pl.BlockSpec((m, d), fixed),
         pl.BlockSpec((tm, d), row)],
        [pl.BlockSpec((tm, d), row), pl.BlockSpec((m, d), fixed), pl.BlockSpec((m, d), fixed)],
        [_sds((s, d), BF16), _sds((m, d), F32), _sds((m, d), F32)], [q, k, v, do], sem=("arbitrary",))[0]


def _grid2(rows, cols, row_mult):
    tr, tc = _tile(rows, 512, row_mult), _tile(cols, 2048)
    return tr, tc, rows // tr, cols // tc


def cast_place(name, block, axis, place):
    r, c = block.shape
    tr, tc, nbr, nbc = _grid2(r, c, 16)
    if axis == 1:
        dst = lambda i, j, p: (i, j + p[0] * nbc)
    else:
        dst = lambda i, j, p: (i + p[0] * nbr, j)

    def body(p_ref, w_ref, out_ref):
        out_ref[...] = w_ref[...].astype(BF16)

    return pl.pallas_call(
        body, name=name,
        grid_spec=pltpu.PrefetchScalarGridSpec(
            num_scalar_prefetch=1, grid=(nbr, nbc),
            in_specs=[pl.BlockSpec((tr, tc), lambda i, j, p: (i, j))],
            out_specs=pl.BlockSpec((tr, tc), dst)),
        out_shape=_sds(_full_shape(block.shape, axis), BF16),
        compiler_params=_params(("parallel", "parallel")),
    )(place, block)


def pair_add(name, grad, peer, axis, place):
    hr, hc = peer.shape
    tr, tc, nbr, nbc = _grid2(hr, hc, 16)
    same = lambda i, j, p: (i, j)
    if grad.shape == peer.shape:
        mine = same
    elif axis == 1:
        mine = lambda i, j, p: (i + p[1] * nbr, j)
    else:
        mine = lambda i, j, p: (i, j + p[1] * nbc)

    def body(p_ref, g_ref, q_ref, out_ref):
        out_ref[...] = (g_ref[...].astype(F32) + q_ref[...].astype(F32)).astype(BF16)

    return pl.pallas_call(
        body, name=name,
        grid_spec=pltpu.PrefetchScalarGridSpec(
            num_scalar_prefetch=1, grid=(nbr, nbc),
            in_specs=[pl.BlockSpec((tr, tc), mine), pl.BlockSpec((tr, tc), same)],
            out_specs=pl.BlockSpec((tr, tc), same)),
        out_shape=_sds((hr, hc), BF16),
        compiler_params=_params(("parallel", "parallel")),
    )(place, grad, peer)


def cross_sum(name, part, land, axis, shape, place):
    _, sr, sc = land.shape
    tr, tc, nbr, nbc = _grid2(sr, sc, 16)
    if axis == 1:
        own = lambda i, j, p: (i, j + p[0] * nbc)
        dst = lambda i, j, p: (i + p[1] * nbr, j)
    else:
        own = lambda i, j, p: (i + p[0] * nbr, j)
        dst = lambda i, j, p: (i, j + p[1] * nbc)

    def body(p_ref, own_ref, land_ref, out_ref):
        out_ref[...] = ((own_ref[...].astype(F32) + land_ref[0].astype(F32))
                        + (land_ref[1].astype(F32) + land_ref[2].astype(F32)))

    return pl.pallas_call(
        body, name=name,
        grid_spec=pltpu.PrefetchScalarGridSpec(
            num_scalar_prefetch=1, grid=(nbr, nbc),
            in_specs=[pl.BlockSpec((tr, tc), own), pl.BlockSpec((3, tr, tc), lambda i, j, p: (0, i, j))],
            out_specs=pl.BlockSpec((tr, tc), dst)),
        out_shape=_sds(_block(shape, axis), F32),
        compiler_params=_params(("parallel", "parallel")),
    )(place, part, land)


def _adam_math(w, g, m, v):
    m = ADAM_B1 * m + (1.0 - ADAM_B1) * g
    v = ADAM_B2 * v + (1.0 - ADAM_B2) * (g * g)
    m_hat = m / (1.0 - ADAM_B1 ** ADAM_STEP)
    v_hat = v / (1.0 - ADAM_B2 ** ADAM_STEP)
    delta = -ADAM_LR * (m_hat / (jnp.sqrt(v_hat) + ADAM_EPS) + ADAM_WD * w)
    return delta, m, v


def adamw(name, w, g, m, v, jobs=()):
    r, c = w.shape
    tr, tc = _tile(r, 256, 8), _tile(c, 1408)

    def body(w_ref, g_ref, m_ref, v_ref, g_out, d_out, m_out, v_out):
        d, mm, vv = _adam_math(w_ref[...], g_ref[...], m_ref[...], v_ref[...])
        g_out[...] = g_ref[...]
        d_out[...] = d
        m_out[...] = mm
        v_out[...] = vv

    spec = pl.BlockSpec((tr, tc), lambda i, j: (i, j))
    outs, job_outs = _call(name, body, (r // tr, c // tc), [spec] * 4, [spec] * 4, [_sds((r, c), F32)] * 4,
                           [w, g, m, v], sem=("parallel", "parallel"), jobs=jobs)
    return _ret(outs, job_outs, jobs, single=False)


def small_sum(name, stacks):
    def body(*refs):
        for s_ref, out_ref in zip(refs[:len(stacks)], refs[len(stacks):]):
            acc = s_ref[0]
            for d in range(1, s_ref.shape[0]):
                acc = acc + s_ref[d]
            out_ref[...] = acc

    return pl.pallas_call(body, name=name, out_shape=[_sds(s.shape[1:], F32) for s in stacks])(*stacks)


WEIGHTS = ["g_ffn1", "w_ffn1_in", "w_ffn1_out", "g_mix", "w_mix_in", "conv_w", "conv_b", "g_gm_v", "w_spatial",
           "b_spatial", "w_mix_out", "g_xattn", "g_mem", "w_xq", "w_xk", "w_xv", "w_xo", "g_ffn2", "w_ffn2_in",
           "w_ffn2_out", "g_final"]
BIG = {"w_ffn1_in": 1, "w_ffn1_out": 0, "w_mix_in": 1, "w_mix_out": 0, "w_xq": 0, "w_xk": 0, "w_xv": 0, "w_xo": 0,
       "w_ffn2_in": 1, "w_ffn2_out": 0}
SMALL = [n for n in WEIGHTS if n not in BIG]
LATE_SMALL = ["g_ffn1"]
EARLY_SMALL = [n for n in SMALL if n not in LATE_SMALL]


def _pack(arrays):
    flat = jnp.concatenate([a.reshape(-1) for a in arrays])
    rows = -(-flat.shape[0] // 1024) * 8
    return jnp.pad(flat, (0, rows * 128 - flat.shape[0])).reshape(rows, 128)


def _unpack(buf, shapes):
    flat = buf.reshape(-1)
    out, pos = [], 0
    for shp in shapes:
        n = math.prod(shp)
        out.append(flat[pos:pos + n].reshape(shp))
        pos += n
    return out


def kernel(x, mem, g_ffn1, w_ffn1_in, w_ffn1_out, g_mix, w_mix_in, conv_w, conv_b, g_gm_v, w_spatial, b_spatial, w_mix_out, g_xattn, g_mem, w_xq, w_xk, w_xv, w_xo, g_ffn2, w_ffn2_in, w_ffn2_out, g_final, loss_target, m_g_ffn1, m_w_ffn1_in, m_w_ffn1_out, m_g_mix, m_w_mix_in, m_conv_w, m_conv_b, m_g_gm_v, m_w_spatial, m_b_spatial, m_w_mix_out, m_g_xattn, m_g_mem, m_w_xq, m_w_xk, m_w_xv, m_w_xo, m_g_ffn2, m_w_ffn2_in, m_w_ffn2_out, m_g_final, v_g_ffn1, v_w_ffn1_in, v_w_ffn1_out, v_g_mix, v_w_mix_in, v_conv_w, v_conv_b, v_g_gm_v, v_w_spatial, v_b_spatial, v_w_mix_out, v_g_xattn, v_g_mem, v_w_xq, v_w_xk, v_w_xv, v_w_xo, v_g_ffn2, v_w_ffn2_in, v_w_ffn2_out, v_g_final):
    given = dict(locals())
    wts = {n: given[n] for n in WEIGHTS}
    mom = {n: given["m_" + n] for n in WEIGHTS}
    var = {n: given["v_" + n] for n in WEIGHTS}

    xi, yi, ci = lax.axis_index("x"), lax.axis_index("y"), lax.axis_index("c")
    blk = 2 * xi + yi
    place = jnp.stack([blk, ci]).astype(jnp.int32)

    x2, mem2, tgt = x[0], mem[0], loss_target[0]
    own = {n: cast_place("cast_" + n, wts[n][0], BIG[n], place) for n in BIG}
    shape = {n: own[n].shape for n in BIG}
    w_s, b_t = w_spatial[0], b_spatial[0].T
    gf = g_final[None]

    def gather(*names):
        return gather_job([(own[n], BIG[n], WHOLE) for n in names])

    full = {}

    (full["w_ffn1_in"],), (conv_taps,) = comm_only(
        "gather_first", [gather("w_ffn1_in"), columns_job(jnp.pad(conv_w[0], ((0, 8 - CONV_K), (0, 0))))])
    n1, r1 = rmsnorm_fwd("norm1", x2, g_ffn1)
    (gu1, a1), (got,) = swiglu_fwd("ffn1_in", n1, full["w_ffn1_in"], jobs=[gather("w_ffn1_out", "w_mix_in")])
    full["w_ffn1_out"], full["w_mix_in"] = got
    h1, (got,) = mm_nn_resid("ffn1_out", a1, full["w_ffn1_out"], x2, 0.5, tm=512, tk=5632,
                             jobs=[gather("w_mix_out", "w_xq", "w_xk")])
    full["w_mix_out"], full["w_xq"], full["w_xk"] = got
    n2, r2 = rmsnorm_fwd("norm2", h1, g_mix)
    z, (got,) = mm_nn("mix_in", n2, full["w_mix_in"], BF16, jobs=[gather("w_xv", "w_xo")])
    full["w_xv"], full["w_xo"] = got

    pieces = 4
    def ffn2_piece(p, prev):
        return [gather_job([(prev, 1, (p, 1, pieces))])]

    ycat, ((w2in,),) = mixer_fwd("mixer", z, conv_taps, conv_b, g_gm_v, w_s, b_t,
                                 jobs=ffn2_piece(0, own["w_ffn2_in"]))
    h2, ((w2in,),) = mm_nn_resid("mix_out", ycat, full["w_mix_out"], h1, 1.0, tk=2048, jobs=ffn2_piece(1, w2in))
    n3, r3 = rmsnorm_fwd("norm3", h2, g_xattn)
    mn, rm = rmsnorm_fwd("norm_mem", mem2, g_mem)
    q, ((w2in,),) = mm_nn("xq", n3, full["w_xq"], BF16, jobs=ffn2_piece(2, w2in))
    k = mm_nn("xk", mn, full["w_xk"], BF16)
    v = mm_nn("xv", mn, full["w_xv"], BF16)
    o = attn_fwd("attn", q, k, v)
    h3, ((w2in,),) = mm_nn_resid("xo", o, full["w_xo"], h2, 1.0, tk=2048, jobs=ffn2_piece(3, w2in))
    full["w_ffn2_in"] = w2in
    n4, r4 = rmsnorm_fwd("norm4", h3, g_ffn2)
    (gu2, a2), ((full["w_ffn2_out"],),) = swiglu_fwd("ffn2_in", n4, full["w_ffn2_in"], jobs=[gather("w_ffn2_out")])
    h4 = mm_nn_resid("ffn2_out", a2, full["w_ffn2_out"], h3, 0.5, tm=512, tk=5632)
    loss_blk, dh4, dh4b, dg_final = loss_head("loss_head", h4, gf, tgt)

    dw, peer, part, land, half, grads = {}, {}, {}, {}, {}, {}

    def send_pair(*names):
        return pair_job([dw[n] for n in names], [BIG[n] for n in names])

    def take_pair(names, got):
        for n, p in zip(names, got):
            part[n] = pair_add("pair_add_" + n, dw[n], p, BIG[n], place)

    def send_cross(*names, sub=WHOLE):
        return cross_job([(part[n], BIG[n], shape[n], land.get(n), sub) for n in names])

    def take_cross(names, got, last=True):
        for n, l in zip(names, got):
            land[n] = l
            if last:
                half[n] = cross_sum("cross_sum_" + n, part[n], l, BIG[n], shape[n], place)

    def send_final(*names):
        return final_job([half[n] for n in names], [BIG[n] for n in names], [shape[n] for n in names])

    delta, new_m, new_v = {}, {}, {}

    reduced = {}

    def take_final(names, got):
        for n, g in zip(names, got):
            reduced[n] = g

    def update(n, jobs=()):
        res = adamw("adamw_" + n, wts[n][0], reduced[n], mom[n][0], var[n][0], jobs=jobs)
        (grads[n], delta[n], new_m[n], new_v[n]), job_outs = res if jobs else (res, [])
        return job_outs

    dgu2 = swiglu_bwd("ffn2_dact", dh4b, full["w_ffn2_out"], gu2, 0.5)
    dw["w_ffn2_in"] = mm_tn_pair("ffn2_dwin", n4, dgu2, BF16)
    dw["w_ffn2_out"], (got,) = mm_tn("ffn2_dwout", a2, dh4b, BF16, scale=0.5, jobs=[send_pair("w_ffn2_in")])
    take_pair(["w_ffn2_in"], got)
    dn4, (got_c, got_p) = mm_nt_pair("ffn2_dn", dgu2, full["w_ffn2_in"], F32,
                                     jobs=[send_cross("w_ffn2_in", sub=(0, 7, 8)), send_pair("w_ffn2_out")])
    take_cross(["w_ffn2_in"], got_c, last=False)
    take_pair(["w_ffn2_out"], got_p)
    (dh3, dh3b, dg_ffn2), (got_c,) = rmsnorm_bwd("norm4_bwd", dn4, h3, r4, g_ffn2, dh4,
                                                 jobs=[send_cross("w_ffn2_in", sub=(7, 1, 8))])
    take_cross(["w_ffn2_in"], got_c)

    dw["w_xo"], (got_c,) = mm_tn("xo_dw", o, dh3b, BF16, jobs=[send_cross("w_ffn2_out", sub=(0, 2, 8))])
    take_cross(["w_ffn2_out"], got_c, last=False)
    do, (got_c, got_f) = mm_nt("xo_dx", dh3b, full["w_xo"], BF16,
                               jobs=[send_cross("w_ffn2_out", sub=(2, 2, 8)), send_final("w_ffn2_in")])
    take_cross(["w_ffn2_out"], got_c, last=False)
    take_final(["w_ffn2_in"], got_f)
    update("w_ffn2_in")
    dq, dk, dv = attn_bwd("attn_bwd", q, k, v, do)
    dkb, dvb = dk.astype(BF16), dv.astype(BF16)
    dw["w_xq"], (got_c,) = mm_tn("xq_dw", n3, dq, BF16, jobs=[send_cross("w_ffn2_out", sub=(4, 2, 8))])
    take_cross(["w_ffn2_out"], got_c, last=False)
    dn3, (got_c,) = mm_nt("xq_dx", dq, full["w_xq"], F32, jobs=[send_cross("w_ffn2_out", sub=(6, 2, 8))])
    take_cross(["w_ffn2_out"], got_c)
    (dh2, dh2b, dg_xattn), (got_f,) = rmsnorm_bwd("norm3_bwd", dn3, h2, r3, g_xattn, dh3,
                                                  jobs=[send_final("w_ffn2_out")])
    take_final(["w_ffn2_out"], got_f)
    update("w_ffn2_out")
    dw["w_xk"] = mm_tn("xk_dw", mn, dkb, BF16)
    dw["w_xv"] = mm_tn("xv_dw", mn, dvb, BF16)
    dmn_k = mm_nt("xk_dx", dkb, full["w_xk"], F32)
    dmn_v = mm_nt("xv_dx", dvb, full["w_xv"], F32)
    dg_mem = gain_grad("norm_mem_bwd", dmn_k, dmn_v, mem2, rm)

    dw["w_mix_out"] = mm_tn("mix_out_dw", ycat, dh2b, BF16)
    attn_names = ["w_xo", "w_xq", "w_xk", "w_xv", "w_mix_out"]
    dycat, (got_p,) = mm_nt("mix_out_dx", dh2b, full["w_mix_out"], BF16, jobs=[send_pair(*attn_names)])
    take_pair(attn_names, got_p)
    (dz, dsmall, dws, dbt), (got_c,) = mixer_bwd("mixer_bwd", z, dycat, conv_taps, conv_b, g_gm_v, w_s, b_t,
                                                 jobs=[send_cross("w_xo", "w_xq")])
    take_cross(["w_xo", "w_xq"], got_c)
    dw["w_mix_in"], (got_c,) = mm_tn("mix_in_dw", n2, dz, BF16, jobs=[send_cross("w_xk", "w_xv")])
    take_cross(["w_xk", "w_xv"], got_c)
    dn2, (got_c, got_p) = mm_nt("mix_in_dx", dz, full["w_mix_in"], F32, tk=2560,
                                jobs=[send_cross("w_mix_out"), send_pair("w_mix_in")])
    take_cross(["w_mix_out"], got_c)
    take_pair(["w_mix_in"], got_p)
    (dh1, dh1b, dg_mix), (got_f,) = rmsnorm_bwd("norm2_bwd", dn2, h1, r2, g_mix, dh2,
                                                jobs=[send_final(*attn_names)])
    take_final(attn_names, got_f)
    for n in attn_names:
        update(n)

    dw["w_ffn1_out"], (got_c,) = mm_tn("ffn1_dwout", a1, dh1b, BF16, scale=0.5, jobs=[send_cross("w_mix_in")])
    take_cross(["w_mix_in"], got_c)
    early = {"g_mix": dg_mix, "conv_w": dsmall[0:CONV_K], "conv_b": dsmall[3:4], "g_gm_v": dsmall[4:5],
             "w_spatial": dws, "b_spatial": dbt.T, "g_xattn": dg_xattn, "g_mem": dg_mem, "g_ffn2": dg_ffn2,
             "g_final": dg_final}
    dgu1, (got_p, got_f, (early_all,)) = swiglu_bwd(
        "ffn1_dact", dh1b, full["w_ffn1_out"], gu1, 0.5,
        jobs=[send_pair("w_ffn1_out"), send_final("w_mix_in"), stack_job(_pack([early[n] for n in EARLY_SMALL]))])
    take_pair(["w_ffn1_out"], got_p)
    take_final(["w_mix_in"], got_f)
    update("w_mix_in")
    theirs, (got_c,) = mm_tn_pair_half("ffn1_dwin_theirs", n1, dgu1, BF16, place, False,
                                       jobs=[send_cross("w_ffn1_out", sub=(0, 7, 8))])
    take_cross(["w_ffn1_out"], got_c, last=False)
    mine, (got_c, (from_sibling,)) = mm_tn_pair_half(
        "ffn1_dwin_mine", n1, dgu1, BF16, place, True,
        jobs=[send_cross("w_ffn1_out", sub=(7, 1, 8)), pair_job([theirs], [1], is_half=True)])
    take_cross(["w_ffn1_out"], got_c)
    part["w_ffn1_in"] = pair_add("pair_add_w_ffn1_in", mine, from_sibling, 1, place)
    dn1, (got_c, got_f) = mm_nt_pair("ffn1_dn", dgu1, full["w_ffn1_in"], F32,
                                     jobs=[send_cross("w_ffn1_in", sub=(0, 7, 8)), send_final("w_ffn1_out")])
    take_cross(["w_ffn1_in"], got_c, last=False)
    take_final(["w_ffn1_out"], got_f)
    update("w_ffn1_out")
    dx, _, dg_ffn1 = rmsnorm_bwd("norm1_bwd", dn1, x2, r1, g_ffn1, dh1)
    got_c, (late_all,) = comm_only("tail_cross", [send_cross("w_ffn1_in", sub=(7, 1, 8)),
                                                  stack_job(_pack([dg_ffn1]))])
    take_cross(["w_ffn1_in"], got_c)
    (got_f,) = comm_only("tail_final", [send_final("w_ffn1_in")])
    take_final(["w_ffn1_in"], got_f)
    update("w_ffn1_in")

    early_sum, late_sum = small_sum("small_sum", [early_all, late_all])
    for n, g in zip(EARLY_SMALL, _unpack(early_sum, [early[n].shape for n in EARLY_SMALL])):
        grads[n] = g
    grads["g_ffn1"] = _unpack(late_sum, [dg_ffn1.shape])[0]
    taps_cols = conv_w.shape[2]
    grads["conv_w"] = lax.dynamic_slice_in_dim(grads["conv_w"], blk * taps_cols, taps_cols, axis=1)
    packed = [_pack([src[n] for n in SMALL]) for src in (wts, grads, mom, var)]
    own_shapes = [wts[n].shape for n in SMALL]
    for dst, buf in zip((delta, new_m, new_v), adamw("adamw_small", *packed)[1:]):
        for n, a in zip(SMALL, _unpack(buf, own_shapes)):
            dst[n] = a

    loss = lax.psum(loss_blk[0, 0], ("x", "y", "c"))
    outs = [loss, dx[None]]
    for group in (grads, delta, new_m, new_v):
        outs += [group[n].reshape(wts[n].shape) for n in WEIGHTS]
    return tuple(outs)
```

```python
import math

import jax
import jax.numpy as jnp
from jax import lax
from jax.experimental import pallas as pl
from jax.experimental.pallas import tpu as pltpu

F32 = jnp.float32
BF16 = jnp.bfloat16
EPS = 1e-6
GROUP = 128
XA_HEADS = 4
CONV_K = 3
N_CHIPS = 4
VMEM_LIMIT_BYTES = 56 * 1024 * 1024

ADAM_LR = 0.001
ADAM_B1 = 0.9
ADAM_B2 = 0.999
ADAM_EPS = 1e-08
ADAM_WD = 0.01
ADAM_STEP = 10

MESH = pl.DeviceIdType.MESH
ANY = pl.BlockSpec(memory_space=pl.ANY)


def _tile(dim, pref, mult=128):
    if dim <= pref:
        return dim
    t = (pref // mult) * mult
    while t >= mult:
        if dim % t == 0:
            return t
        t -= mult
    raise ValueError(f"no tile for {dim} under {pref}")


def _params(sem):
    return pltpu.CompilerParams(dimension_semantics=sem, vmem_limit_bytes=VMEM_LIMIT_BYTES)


def _sds(shape, dtype):
    return jax.ShapeDtypeStruct(shape, dtype)


def _dot_nn(a, b):
    return jnp.dot(a, b, preferred_element_type=F32)


def _dot_nt(a, b):
    return lax.dot_general(a, b, (((1,), (1,)), ((), ())), preferred_element_type=F32)


def _dot_tn(a, b):
    return lax.dot_general(a, b, (((0,), (0,)), ((), ())), preferred_element_type=F32)


class Job:
    def __init__(self, inputs, out_shapes, aliases, sems, start, middle, finish):
        self.inputs, self.out_shapes, self.aliases, self.sems = inputs, out_shapes, aliases, sems
        self.start, self.middle, self.finish = start, middle, finish


def _place():
    x, y, c = lax.axis_index("x"), lax.axis_index("y"), lax.axis_index("c")
    chips = [(1 - x, y), (x, 1 - y), (1 - x, 1 - y)]
    return x, y, c, chips


def _ds(start, size, lane):
    if not isinstance(start, int):
        start = pl.multiple_of(start, 128 if lane else 16)
    return pl.ds(start, size)


WHOLE = (0, 1, 1)


def _window(ref, axis, shape, blk=None, half=None, sub=WHOLE, within=WHOLE):
    n = shape[axis] // N_CHIPS
    hs = shape[1 - axis] // 2
    idx = [slice(None), slice(None)]
    if blk is not None:
        b_first, b_count, b_pieces = within
        b_ext = n // b_pieces
        idx[axis] = _ds(blk * n + b_first * b_ext, b_count * b_ext, axis == 1)
    first, count, pieces = sub
    ext = hs // pieces
    if half is not None:
        idx[1 - axis] = _ds(half * hs + first * ext, count * ext, axis == 0)
    elif pieces > 1:
        idx[1 - axis] = _ds(first * ext, count * ext, axis == 0)
    return ref.at[tuple(idx)]


def _remote(src, dst, send_sem, recv_sem, dev):
    return pltpu.make_async_remote_copy(src_ref=src, dst_ref=dst, send_sem=send_sem, recv_sem=recv_sem,
                                        device_id=dev, device_id_type=MESH)


def _full_shape(block_shape, axis):
    out = list(block_shape)
    out[axis] *= N_CHIPS
    return tuple(out)


def _half_all(shape, axis):
    out = list(shape)
    out[1 - axis] //= 2
    return tuple(out)


def _block(shape, axis):
    out = list(shape)
    out[axis] //= N_CHIPS
    return tuple(out)


def _half_block(shape, axis):
    return _half_all(_block(shape, axis), axis)


def gather_job(items):
    nw = len(items)
    shapes = [item[0].shape for item in items]
    n_sem = 8

    def parts(sub):
        first, count, pieces = sub
        return (2 * first, count, 2 * pieces), (2 * first + count, count, 2 * pieces)

    def start(pos, ins, outs, sems):
        x, y, c, chips = pos
        for w, (_, ax, sub, within) in enumerate(items):
            mine = _window(outs[w], ax, shapes[w], blk=2 * x + y, half=c, sub=sub, within=within)
            for j in range(2):
                _remote(mine, mine, sems[0].at[n_sem * w + j], sems[1].at[n_sem * w + j], (*chips[j], c)).start()

    def middle(pos, ins, outs, sems):
        x, y, c, chips = pos
        for w, (_, ax, sub, within) in enumerate(items):
            for j in range(2):
                cx, cy = chips[j]
                landed = _window(outs[w], ax, shapes[w], blk=2 * cx + cy, half=c, sub=sub, within=within)
                _remote(landed, landed, sems[0].at[n_sem * w + j], sems[1].at[n_sem * w + j], (cx, cy, c)).wait_recv()
                part = _window(outs[w], ax, shapes[w], blk=2 * cx + cy, half=c, sub=parts(sub)[j], within=within)
                _remote(part, part, sems[0].at[n_sem * w + 2 + j], sems[1].at[n_sem * w + 2 + j],
                        (*chips[1 - j], c)).start()
                _remote(landed, landed, sems[0].at[n_sem * w + 4 + j], sems[1].at[n_sem * w + 4 + j],
                        (x, y, 1 - c)).start()

    def finish(pos, ins, outs, sems):
        x, y, c, chips = pos
        sib = (x, y, 1 - c)
        for w, (_, ax, sub, within) in enumerate(items):
            dx, dy = chips[2]
            for j in range(2):
                part = _window(outs[w], ax, shapes[w], blk=2 * dx + dy, half=c, sub=parts(sub)[j], within=within)
                cp = _remote(part, part, sems[0].at[n_sem * w + 2 + j], sems[1].at[n_sem * w + 2 + j], sib)
                cp.wait_recv()
                cp.wait_send()
            diag = _window(outs[w], ax, shapes[w], blk=2 * dx + dy, half=c, sub=sub, within=within)
            _remote(diag, diag, sems[0].at[n_sem * w + 6], sems[1].at[n_sem * w + 6], sib).start()
        for w, (_, ax, sub, within) in enumerate(items):
            for j, (cx, cy) in enumerate(chips):
                passed = _window(outs[w], ax, shapes[w], blk=2 * cx + cy, half=1 - c, sub=sub, within=within)
                cp = _remote(passed, passed, sems[0].at[n_sem * w + 4 + j], sems[1].at[n_sem * w + 4 + j], sib)
                cp.wait_recv()
                cp.wait_send()
            mine = _window(outs[w], ax, shapes[w], blk=2 * x + y, half=c, sub=sub, within=within)
            for j in range(2):
                _remote(mine, mine, sems[0].at[n_sem * w + j], sems[1].at[n_sem * w + j], sib).wait_send()

    sems = [pltpu.SemaphoreType.DMA((n_sem * nw,)), pltpu.SemaphoreType.DMA((n_sem * nw,))]
    return Job([item[0] for item in items], [_sds(item[0].shape, item[0].dtype) for item in items],
               {w: w for w in range(nw)}, sems, start, middle, finish)


def pair_job(grads, axes, is_half=False):
    nw = len(grads)
    shapes = [g.shape for g in grads]

    def start(pos, ins, outs, sems):
        x, y, c, _ = pos
        for w in range(nw):
            src = ins[w] if is_half else _window(ins[w], axes[w], shapes[w], half=1 - c)
            _remote(src, outs[w], sems[0].at[w], sems[1].at[w], (x, y, 1 - c)).start()

    def finish(pos, ins, outs, sems):
        x, y, c, _ = pos
        for w in range(nw):
            cp = _remote(outs[w], outs[w], sems[0].at[w], sems[1].at[w], (x, y, 1 - c))
            cp.wait_recv()
            cp.wait_send()

    sems = [pltpu.SemaphoreType.DMA((nw,)), pltpu.SemaphoreType.DMA((nw,))]
    out_shapes = [_sds(s if is_half else _half_all(s, a), BF16) for s, a in zip(shapes, axes)]
    return Job(list(grads), out_shapes, {}, sems, start, None, finish)


def cross_job(items):
    nw = len(items)
    inputs, aliases = [], {}
    for w, (part, ax, shape, prev, sub) in enumerate(items):
        inputs.append(part)
        if prev is not None:
            aliases[len(inputs)] = w
            inputs.append(prev)

    def copies(pos, ins, outs, sems):
        x, y, c, chips = pos
        k = 0
        for w, (_, ax, shape, prev, sub) in enumerate(items):
            src = ins[k]
            k += 2 if prev is not None else 1
            for j, (cx, cy) in enumerate(chips):
                slot = _window(outs[w].at[j], ax, shape, sub=sub)
                yield (_remote(_window(src, ax, shape, blk=2 * cx + cy, sub=sub), slot,
                               sems[0].at[3 * w + j], sems[1].at[3 * w + j], (cx, cy, c)),
                       _remote(slot, slot, sems[0].at[3 * w + j], sems[1].at[3 * w + j], (cx, cy, c)))

    def start(pos, ins, outs, sems):
        for send, _ in copies(pos, ins, outs, sems):
            send.start()

    def finish(pos, ins, outs, sems):
        for send, recv in copies(pos, ins, outs, sems):
            recv.wait_recv()
            send.wait_send()

    sems = [pltpu.SemaphoreType.DMA((3 * nw,)), pltpu.SemaphoreType.DMA((3 * nw,))]
    out_shapes = [_sds((3,) + _half_block(shape, ax), BF16) for _, ax, shape, _, _ in items]
    return Job(inputs, out_shapes, aliases, sems, start, None, finish)


def final_job(blocks, axes, shapes):
    nw = len(blocks)

    def start(pos, ins, outs, sems):
        x, y, c, _ = pos
        for w in range(nw):
            mine = _window(outs[w], axes[w], shapes[w], half=c)
            _remote(mine, mine, sems[0].at[w], sems[1].at[w], (x, y, 1 - c)).start()

    def finish(pos, ins, outs, sems):
        x, y, c, _ = pos
        for w in range(nw):
            theirs = _window(outs[w], axes[w], shapes[w], half=1 - c)
            cp = _remote(theirs, theirs, sems[0].at[w], sems[1].at[w], (x, y, 1 - c))
            cp.wait_recv()
            cp.wait_send()

    sems = [pltpu.SemaphoreType.DMA((nw,)), pltpu.SemaphoreType.DMA((nw,))]
    return Job(list(blocks), [_sds(b.shape, b.dtype) for b in blocks], {w: w for w in range(nw)}, sems, start, None,
               finish)


def stack_job(small):
    def peers(pos):
        x, y, c, _ = pos
        for k in range(1, 8):
            yield k - 1, (1 - x if k & 4 else x, 1 - y if k & 2 else y, 1 - c if k & 1 else c)

    def start(pos, ins, outs, sems):
        x, y, c, _ = pos
        mine = outs[0].at[4 * x + 2 * y + c]
        pltpu.make_async_copy(ins[0], mine, sems[2]).start()
        for k, dev in peers(pos):
            _remote(ins[0], mine, sems[0].at[k], sems[1].at[k], dev).start()

    def finish(pos, ins, outs, sems):
        x, y, c, _ = pos
        for k, (px, py, pc) in peers(pos):
            slot = outs[0].at[4 * px + 2 * py + pc]
            cp = _remote(slot, slot, sems[0].at[k], sems[1].at[k], (px, py, pc))
            cp.wait_recv()
            cp.wait_send()
        pltpu.make_async_copy(ins[0], outs[0].at[4 * x + 2 * y + c], sems[2]).wait()

    sems = [pltpu.SemaphoreType.DMA((7,)), pltpu.SemaphoreType.DMA((7,)), pltpu.SemaphoreType.DMA]
    return Job([small], [_sds((8,) + small.shape, small.dtype)], {}, sems, start, None, finish)


def columns_job(block):
    cols = block.shape[1]
    place = lambda out, b: out.at[:, _ds(b * cols, cols, True)]

    def start(pos, ins, outs, sems):
        x, y, c, chips = pos
        pltpu.make_async_copy(ins[0], place(outs[0], 2 * x + y), sems[2]).start()
        for j, (cx, cy) in enumerate(chips):
            _remote(ins[0], place(outs[0], 2 * x + y), sems[0].at[j], sems[1].at[j], (cx, cy, c)).start()

    def finish(pos, ins, outs, sems):
        x, y, c, chips = pos
        for j, (cx, cy) in enumerate(chips):
            got = place(outs[0], 2 * cx + cy)
            cp = _remote(got, got, sems[0].at[j], sems[1].at[j], (cx, cy, c))
            cp.wait_recv()
            cp.wait_send()
        pltpu.make_async_copy(ins[0], place(outs[0], 2 * x + y), sems[2]).wait()

    sems = [pltpu.SemaphoreType.DMA((3,)), pltpu.SemaphoreType.DMA((3,)), pltpu.SemaphoreType.DMA]
    return Job([block], [_sds((block.shape[0], N_CHIPS * cols), block.dtype)], {}, sems, start, None, finish)


def _call(name, body, grid, in_specs, out_specs, out_shape, args, scratch=(), sem=None, jobs=(), place=None,
          carried=None):
    n_in, n_out, n_sc = len(args), len(out_shape), len(scratch)
    carried = dict(carried or {})

    def launch(fn, in_specs, out_specs, out_shape, scratch, aliases, sem, operands):
        if place is None:
            return pl.pallas_call(
                fn, name=name, grid=grid, in_specs=in_specs, out_specs=out_specs, out_shape=out_shape,
                scratch_shapes=scratch, input_output_aliases=aliases, compiler_params=_params(sem))(*operands)
        spec = pltpu.PrefetchScalarGridSpec(num_scalar_prefetch=1, grid=grid, in_specs=in_specs,
                                            out_specs=out_specs, scratch_shapes=scratch)
        return pl.pallas_call(
            lambda p_ref, *refs: fn(*refs), name=name, grid_spec=spec, out_shape=out_shape,
            input_output_aliases={k + 1: v for k, v in aliases.items()}, compiler_params=_params(sem),
        )(place, *operands)

    if not jobs:
        outs = launch(body, list(in_specs), list(out_specs), list(out_shape), list(scratch), carried, sem, args)
        return list(outs), []

    total = math.prod(grid) if grid else 1
    mid = min(total - 1, (2 * total) // 3)

    def split(refs, start, counts):
        out = []
        for n in counts:
            out.append(refs[start:start + n])
            start += n
        return out, start

    def wrapped(*refs):
        c_in = refs[:n_in]
        j_ins, p = split(refs, n_in, [len(j.inputs) for j in jobs])
        c_out = refs[p:p + n_out]
        j_outs, p = split(refs, p + n_out, [len(j.out_shapes) for j in jobs])
        c_sc = refs[p:p + n_sc]
        j_sems, p = split(refs, p + n_sc, [len(j.sems) for j in jobs])
        pos = _place()
        step = 0
        for axis, extent in enumerate(grid):
            step = step * extent + pl.program_id(axis)

        def run(phase):
            for j, ins, outs, sems in zip(jobs, j_ins, j_outs, j_sems):
                fn = getattr(j, phase)
                if fn is not None:
                    fn(pos, ins, outs, sems)

        if total == 1:
            run("start")
            body(*c_in, *c_out, *c_sc)
            run("middle")
            run("finish")
            return
        pl.when(step == 0)(lambda: run("start"))
        body(*c_in, *c_out, *c_sc)
        if any(j.middle is not None for j in jobs):
            pl.when(step == mid)(lambda: run("middle"))
        pl.when(step == total - 1)(lambda: run("finish"))

    aliases, in_at, out_at = carried, n_in, n_out
    for j in jobs:
        for src, dst in j.aliases.items():
            aliases[in_at + src] = out_at + dst
        in_at += len(j.inputs)
        out_at += len(j.out_shapes)
    outs = launch(
        wrapped, list(in_specs) + [ANY] * (in_at - n_in), list(out_specs) + [ANY] * (out_at - n_out),
        list(out_shape) + [s for j in jobs for s in j.out_shapes],
        list(scratch) + [s for j in jobs for s in j.sems], aliases, ("arbitrary",) * len(grid),
        [*args, *[a for j in jobs for a in j.inputs]])
    job_outs, p = split(outs, n_out, [len(j.out_shapes) for j in jobs])
    return list(outs[:n_out]), [list(o) for o in job_outs]


def comm_only(name, jobs):
    def body(dummy_ref, out_ref):
        out_ref[...] = dummy_ref[...]

    dummy = jnp.zeros((8, 128), F32)
    spec = pl.BlockSpec((8, 128), lambda: (0, 0))
    return _call(name, body, (), [spec], [spec], [_sds((8, 128), F32)], [dummy], jobs=jobs)[1]


def _ret(outs, job_outs, jobs, single=True):
    res = outs[0] if single else outs
    return (res, job_outs) if jobs else res


def rmsnorm_fwd(name, x, g):
    s, d = x.shape
    tm = _tile(s, 512, 8)

    def body(x_ref, g_ref, n_ref, r_ref):
        xv = x_ref[...]
        r = lax.rsqrt(jnp.mean(xv * xv, axis=-1, keepdims=True) + EPS)
        n_ref[...] = (xv * r * g_ref[...]).astype(BF16)
        r_ref[...] = r

    row = lambda i: (i, 0)
    return _call(
        name, body, (s // tm,),
        [pl.BlockSpec((tm, d), row), pl.BlockSpec((1, d), lambda i: (0, 0))],
        [pl.BlockSpec((tm, d), row), pl.BlockSpec((tm, 1), row)],
        [_sds((s, d), BF16), _sds((s, 1), F32)], [x, g], sem=("arbitrary",))[0]


def rmsnorm_bwd(name, dn, x, r, g, dh_in, jobs=()):
    s, d = x.shape
    tm = _tile(s, 512, 8)

    def body(dn_ref, x_ref, r_ref, g_ref, dh_ref, out_ref, outb_ref, dg_ref):
        i = pl.program_id(0)
        xh = x_ref[...] * r_ref[...]
        dnv = dn_ref[...]
        dxh = dnv * g_ref[...]
        dx = r_ref[...] * (dxh - xh * jnp.mean(dxh * xh, axis=-1, keepdims=True))
        out = dh_ref[...] + dx
        out_ref[...] = out
        outb_ref[...] = out.astype(BF16)
        part = jnp.sum(dnv * xh, axis=0, keepdims=True)

        @pl.when(i == 0)
        def _():
            dg_ref[...] = part

        @pl.when(i > 0)
        def _():
            dg_ref[...] += part

    row = lambda i: (i, 0)
    fixed = lambda i: (0, 0)
    outs, job_outs = _call(
        name, body, (s // tm,),
        [pl.BlockSpec((tm, d), row), pl.BlockSpec((tm, d), row), pl.BlockSpec((tm, 1), row),
         pl.BlockSpec((1, d), fixed), pl.BlockSpec((tm, d), row)],
        [pl.BlockSpec((tm, d), row), pl.BlockSpec((tm, d), row), pl.BlockSpec((1, d), fixed)],
        [_sds((s, d), F32), _sds((s, d), BF16), _sds((1, d), F32)], [dn, x, r, g, dh_in],
        sem=("arbitrary",), jobs=jobs)
    return _ret(outs, job_outs, jobs, single=False)


def gain_grad(name, dn_a, dn_b, x, r):
    s, d = x.shape
    tm = _tile(s, 512, 8)

    def body(a_ref, b_ref, x_ref, r_ref, dg_ref):
        i = pl.program_id(0)
        part = jnp.sum((a_ref[...] + b_ref[...]) * (x_ref[...] * r_ref[...]), axis=0, keepdims=True)

        @pl.when(i == 0)
        def _():
            dg_ref[...] = part

        @pl.when(i > 0)
        def _():
            dg_ref[...] += part

    row = lambda i: (i, 0)
    return _call(
        name, body, (s // tm,),
        [pl.BlockSpec((tm, d), row), pl.BlockSpec((tm, d), row), pl.BlockSpec((tm, d), row),
         pl.BlockSpec((tm, 1), row)],
        [pl.BlockSpec((1, d), lambda i: (0, 0))], [_sds((1, d), F32)], [dn_a, dn_b, x, r],
        sem=("arbitrary",))[0][0]


def loss_head(name, h, g, target):
    s, d = h.shape
    tm = _tile(s, 512, 8)
    nsteps = s // tm

    def body(h_ref, g_ref, t_ref, loss_ref, dh_ref, dhb_ref, dg_ref, sq_ref):
        i = pl.program_id(0)
        hv = h_ref[...]
        gv = g_ref[...]
        r = lax.rsqrt(jnp.mean(hv * hv, axis=-1, keepdims=True) + EPS)
        xh = hv * r
        err = xh * gv - t_ref[...]
        dy = err * (1.0 / d)
        dxh = dy * gv
        dh = r * (dxh - xh * jnp.mean(dxh * xh, axis=-1, keepdims=True))
        dh_ref[...] = dh
        dhb_ref[...] = dh.astype(BF16)
        dg_part = jnp.sum(dy * xh, axis=0, keepdims=True)
        sq_part = jnp.sum(err * err, axis=0, keepdims=True)

        @pl.when(i == 0)
        def _():
            dg_ref[...] = dg_part
            sq_ref[...] = sq_part

        @pl.when(i > 0)
        def _():
            dg_ref[...] += dg_part
            sq_ref[...] += sq_part

        @pl.when(i == nsteps - 1)
        def _():
            total = jnp.sum(sq_ref[...], axis=-1, keepdims=True) * (0.5 / d)
            loss_ref[...] = jnp.broadcast_to(total, loss_ref.shape)

    row = lambda i: (i, 0)
    fixed = lambda i: (0, 0)
    return _call(
        name, body, (nsteps,),
        [pl.BlockSpec((tm, d), row), pl.BlockSpec((1, d), fixed), pl.BlockSpec((tm, d), row)],
        [pl.BlockSpec((8, 128), fixed), pl.BlockSpec((tm, d), row), pl.BlockSpec((tm, d), row),
         pl.BlockSpec((1, d), fixed)],
        [_sds((8, 128), F32), _sds((s, d), F32), _sds((s, d), BF16), _sds((1, d), F32)], [h, g, target],
        scratch=[pltpu.VMEM((1, d), F32)], sem=("arbitrary",))[0]


def _mm(name, grid, in_arrays, in_specs, out_shapes, out_specs, acc_tile, dot, epilogue, jobs=(), place=None):
    nk = grid[2]
    n_in = len(in_arrays)
    n_out = len(out_shapes)

    def body(*refs):
        ins, outs = refs[:n_in], refs[n_in:n_in + n_out]
        if nk == 1:
            epilogue(dot(*ins), ins, outs)
            return
        acc = refs[n_in + n_out]
        k = pl.program_id(2)

        @pl.when(k == 0)
        def _():
            acc[...] = dot(*ins)

        @pl.when(jnp.logical_and(k > 0, k < nk - 1))
        def _():
            acc[...] += dot(*ins)

        @pl.when(k == nk - 1)
        def _():
            epilogue(acc[...] + dot(*ins), ins, outs)

    scratch = [pltpu.VMEM(acc_tile, F32)] if nk > 1 else []
    outs, job_outs = _call(name, body, grid, in_specs, out_specs, out_shapes, in_arrays, scratch=scratch,
                           sem=("parallel", "parallel", "arbitrary"), jobs=jobs, place=place)
    return _ret(outs, job_outs, jobs)


def _store(scale, dtype):
    def epilogue(acc, ins, outs):
        outs[0][...] = (acc * scale if scale != 1.0 else acc).astype(dtype)
    return epilogue


def mm_nn(name, a, w, out_dtype, tm=1024, tn=1024, tk=2048, jobs=()):
    m, kd = a.shape
    n = w.shape[1]
    tm, tn, tk = _tile(m, tm, 8), _tile(n, tn), _tile(kd, tk)
    return _mm(
        name, (n // tn, m // tm, kd // tk), [a, w],
        [pl.BlockSpec((tm, tk), lambda j, i, k: (i, k)), pl.BlockSpec((tk, tn), lambda j, i, k: (k, j))],
        [_sds((m, n), out_dtype)], [pl.BlockSpec((tm, tn), lambda j, i, k: (i, j))], (tm, tn),
        lambda a_ref, w_ref: _dot_nn(a_ref[...], w_ref[...]), _store(1.0, out_dtype), jobs)


def mm_nn_resid(name, a, w, x, scale, tm=1024, tn=1024, tk=1408, jobs=()):
    m, kd = a.shape
    n = w.shape[1]
    tm, tn, tk = _tile(m, tm, 8), _tile(n, tn), _tile(kd, tk)

    def epilogue(acc, ins, outs):
        outs[0][...] = ins[2][...] + scale * acc

    return _mm(
        name, (n // tn, m // tm, kd // tk), [a, w, x],
        [pl.BlockSpec((tm, tk), lambda j, i, k: (i, k)), pl.BlockSpec((tk, tn), lambda j, i, k: (k, j)),
         pl.BlockSpec((tm, tn), lambda j, i, k: (i, j))],
        [_sds((m, n), F32)], [pl.BlockSpec((tm, tn), lambda j, i, k: (i, j))], (tm, tn),
        lambda a_ref, w_ref, x_ref: _dot_nn(a_ref[...], w_ref[...]), epilogue, jobs)


def mm_nt(name, a, w, out_dtype, scale=1.0, tm=1024, tn=1024, tk=2048, jobs=()):
    m, kd = a.shape
    n = w.shape[0]
    tm, tn, tk = _tile(m, tm, 8), _tile(n, tn), _tile(kd, tk)
    return _mm(
        name, (n // tn, m // tm, kd // tk), [a, w],
        [pl.BlockSpec((tm, tk), lambda j, i, k: (i, k)), pl.BlockSpec((tn, tk), lambda j, i, k: (j, k))],
        [_sds((m, n), out_dtype)], [pl.BlockSpec((tm, tn), lambda j, i, k: (i, j))], (tm, tn),
        lambda a_ref, w_ref: _dot_nt(a_ref[...], w_ref[...]), _store(scale, out_dtype), jobs)


def mm_nt_pair(name, a3, w, out_dtype, tm=1024, tn=1024, tk=2816, jobs=()):
    _, m, f = a3.shape
    n = w.shape[0]
    tm, tn, tk = _tile(m, tm, 8), _tile(n, tn), _tile(f, tk)
    nkf = f // tk
    return _mm(
        name, (n // tn, m // tm, 2 * nkf), [a3, w],
        [pl.BlockSpec((None, tm, tk), lambda j, i, k: (k // nkf, i, k % nkf)),
         pl.BlockSpec((tn, tk), lambda j, i, k: (j, k))],
        [_sds((m, n), out_dtype)], [pl.BlockSpec((tm, tn), lambda j, i, k: (i, j))], (tm, tn),
        lambda a_ref, w_ref: _dot_nt(a_ref[...], w_ref[...]), _store(1.0, out_dtype), jobs)


def mm_tn(name, a, b, out_dtype, scale=1.0, tm=1024, tn=1024, tk=4096, jobs=()):
    kd, m = a.shape
    n = b.shape[1]
    tm, tn, tk = _tile(m, tm), _tile(n, tn), _tile(kd, tk, 16)
    return _mm(
        name, (n // tn, m // tm, kd // tk), [a, b],
        [pl.BlockSpec((tk, tm), lambda j, i, k: (k, i)), pl.BlockSpec((tk, tn), lambda j, i, k: (k, j))],
        [_sds((m, n), out_dtype)], [pl.BlockSpec((tm, tn), lambda j, i, k: (i, j))], (tm, tn),
        lambda a_ref, b_ref: _dot_tn(a_ref[...], b_ref[...]), _store(scale, out_dtype), jobs)


def mm_tn_pair(name, a, b3, out_dtype, tm=1024, tn=512, tk=4096, jobs=()):
    kd, m = a.shape
    f = b3.shape[2]
    tm, tn, tk = _tile(m, tm), _tile(f, tn), _tile(kd, tk, 16)
    nf = f // tn
    return _mm(
        name, (m // tm, 2 * nf, kd // tk), [a, b3],
        [pl.BlockSpec((tk, tm), lambda i, j, k: (k, i)),
         pl.BlockSpec((None, tk, tn), lambda i, j, k: (j // nf, k, j % nf))],
        [_sds((m, 2 * f), out_dtype)], [pl.BlockSpec((tm, tn), lambda i, j, k: (i, j))], (tm, tn),
        lambda a_ref, b_ref: _dot_tn(a_ref[...], b_ref[...]), _store(1.0, out_dtype), jobs)


def mm_tn_pair_half(name, a, b3, out_dtype, place, mine, tm=1024, tn=512, tk=4096, jobs=()):
    kd, m = a.shape
    f = b3.shape[2]
    tm, tn, tk = _tile(m // 2, tm), _tile(f, tn), _tile(kd, tk, 16)
    nf, nbm = f // tn, m // 2 // tm
    which = (lambda p: p[1]) if mine else (lambda p: 1 - p[1])
    return _mm(
        name, (nbm, 2 * nf, kd // tk), [a, b3],
        [pl.BlockSpec((tk, tm), lambda i, j, k, p: (k, i + which(p) * nbm)),
         pl.BlockSpec((None, tk, tn), lambda i, j, k, p: (j // nf, k, j % nf))],
        [_sds((m // 2, 2 * f), out_dtype)], [pl.BlockSpec((tm, tn), lambda i, j, k, p: (i, j))], (tm, tn),
        lambda a_ref, b_ref: _dot_tn(a_ref[...], b_ref[...]), _store(1.0, out_dtype), jobs, place)


def swiglu_fwd(name, n, w_in, tm=1024, tn=512, jobs=(), stride=1, phase=0, prev=None):
    s, d = n.shape
    f = w_in.shape[1] // 2
    tm, tn = _tile(s, tm, 8), _tile(f, tn)
    nf = f // tn
    col = lambda j: j * stride + phase

    def body(n_ref, wg_ref, wu_ref, *rest):
        gu_ref, a_ref = rest[-2:]
        nv = n_ref[...]
        g = _dot_nn(nv, wg_ref[...])
        u = _dot_nn(nv, wu_ref[...])
        gu_ref[0] = g.astype(BF16)
        gu_ref[1] = u.astype(BF16)
        a_ref[...] = (g * jax.nn.sigmoid(g) * u).astype(BF16)

    kept = list(prev) if prev is not None else []
    outs, job_outs = _call(
        name, body, (nf // stride, s // tm),
        [pl.BlockSpec((tm, d), lambda j, i: (i, 0)), pl.BlockSpec((d, tn), lambda j, i: (0, col(j))),
         pl.BlockSpec((d, tn), lambda j, i: (0, col(j) + nf))] + [ANY] * len(kept),
        [pl.BlockSpec((2, tm, tn), lambda j, i: (0, i, col(j))), pl.BlockSpec((tm, tn), lambda j, i: (i, col(j)))],
        [_sds((2, s, f), BF16), _sds((s, f), BF16)], [n, w_in, w_in] + kept, sem=("parallel", "parallel"),
        jobs=jobs, carried={3 + k: k for k in range(len(kept))})
    return _ret(outs, job_outs, jobs, single=False)


def swiglu_bwd(name, dh, w_out, gu, scale, tm=1024, tn=512, jobs=()):
    s, d = dh.shape
    f = w_out.shape[0]
    tm, tn = _tile(s, tm, 8), _tile(f, tn)

    sub = _tile(tm, 256, 8)

    def body(dh_ref, w_ref, gu_ref, out_ref):
        for lo in range(0, tm, sub):
            rows = slice(lo, lo + sub)
            da = (_dot_nt(dh_ref[rows, :], w_ref[...]) * scale).astype(BF16)
            g = gu_ref[0, rows, :]
            u = gu_ref[1, rows, :]
            sg = 0.5 * jnp.tanh(0.5 * g) + 0.5
            t = g * sg
            out_ref[0, rows, :] = da * (u * (sg + t * (1.0 - sg)))
            out_ref[1, rows, :] = da * t

    outs, job_outs = _call(
        name, body, (f // tn, s // tm),
        [pl.BlockSpec((tm, d), lambda j, i: (i, 0)), pl.BlockSpec((tn, d), lambda j, i: (j, 0)),
         pl.BlockSpec((2, tm, tn), lambda j, i: (0, i, j))],
        [pl.BlockSpec((2, tm, tn), lambda j, i: (0, i, j))],
        [_sds((2, s, f), BF16)], [dh, w_out, gu], sem=("parallel", "parallel"), jobs=jobs)
    return _ret(outs, job_outs, jobs)


HALO = 16


def _conv_inputs(z_ref, hgc_ref, hhc_ref, i, cw, tm):
    gc = z_ref[:, cw:2 * cw].astype(F32)
    hc = z_ref[:, 2 * cw:3 * cw].astype(F32)
    cin = gc * hc
    halo = hgc_ref[...].astype(F32) * hhc_ref[...].astype(F32) * (i > 0).astype(F32)
    row = lax.broadcasted_iota(jnp.int32, (tm, cw), 0)
    x1 = jnp.where(row == 0, halo[HALO - 1:HALO], pltpu.roll(cin, 1, 0))
    x2 = jnp.where(row == 0, halo[HALO - 2:HALO - 1], jnp.where(row == 1, halo[HALO - 1:HALO], pltpu.roll(cin, 2, 0)))
    return gc, hc, cin, x1, x2


def _tril(w):
    r = lax.broadcasted_iota(jnp.int32, w.shape, 0)
    c = lax.broadcasted_iota(jnp.int32, w.shape, 1)
    return jnp.where(r >= c, w, jnp.zeros_like(w))


def mixer_fwd(name, z, conv_w, conv_b, g_v, w_s, b_t, tm=256, jobs=()):
    s, zc = z.shape
    cw = conv_w.shape[1]
    gw = g_v.shape[1]
    heads = gw // GROUP
    tm = _tile(s, tm)
    hb = tm // HALO

    def body(z_ref, hgc_ref, hhc_ref, cw_ref, cb_ref, gv_ref, ws_ref, bt_ref, y_ref):
        i = pl.program_id(0)
        _, _, cin, x1, x2 = _conv_inputs(z_ref, hgc_ref, hhc_ref, i, cw, tm)
        cv = cb_ref[...] + cw_ref[2:3, :] * cin + cw_ref[1:2, :] * x1 + cw_ref[0:1, :] * x2
        y_ref[:, 0:cw] = (z_ref[:, 0:cw].astype(F32) * cv).astype(BF16)
        for h in range(heads):
            lo = h * GROUP
            vh = z_ref[:, 3 * cw + gw + lo:3 * cw + gw + lo + GROUP].astype(F32)
            rv = lax.rsqrt(jnp.mean(vh * vh, axis=-1, keepdims=True) + EPS)
            vn = (vh * rv * gv_ref[:, lo:lo + GROUP]).astype(BF16)
            w = _tril(ws_ref[h]).astype(BF16)
            for n in range(tm // GROUP):
                rows = slice(n * GROUP, (n + 1) * GROUP)
                sg = _dot_nn(w, vn[rows]) + bt_ref[:, h:h + 1]
                u = z_ref[rows, 3 * cw + lo:3 * cw + lo + GROUP].astype(F32)
                y_ref[rows, cw + lo:cw + lo + GROUP] = (u * sg).astype(BF16)

    fixed2 = lambda i: (0, 0)
    outs, job_outs = _call(
        name, body, (s // tm,),
        [pl.BlockSpec((tm, zc), lambda i: (i, 0)),
         pl.BlockSpec((HALO, cw), lambda i: (jnp.maximum(i * hb - 1, 0), 1)),
         pl.BlockSpec((HALO, cw), lambda i: (jnp.maximum(i * hb - 1, 0), 2)),
         pl.BlockSpec(conv_w.shape, fixed2), pl.BlockSpec(conv_b.shape, fixed2),
         pl.BlockSpec(g_v.shape, fixed2), pl.BlockSpec(w_s.shape, lambda i: (0, 0, 0)),
         pl.BlockSpec(b_t.shape, fixed2)],
        [pl.BlockSpec((tm, cw + gw), lambda i: (i, 0))], [_sds((s, cw + gw), BF16)],
        [z, z, z, conv_w, conv_b, g_v, w_s, b_t], sem=("arbitrary",), jobs=jobs)
    return _ret(outs, job_outs, jobs)


def mixer_bwd(name, z, dy, conv_w, conv_b, g_v, w_s, b_t, tm=256, jobs=()):
    s, zc = z.shape
    cw = conv_w.shape[1]
    gw = g_v.shape[1]
    heads = gw // GROUP
    tm = _tile(s, tm)
    hb = tm // HALO
    nsteps = s // tm
    last_halo = s // HALO - 1

    def body(z_ref, hgc_ref, hhc_ref, ngb_ref, dy_ref, ndy_ref, cw_ref, cb_ref, gv_ref, ws_ref, bt_ref,
             dz_ref, sm_ref, dws_ref, dbt_ref, dsg_ref):
        i = pl.program_id(0)

        @pl.when(i == 0)
        def _():
            sm_ref[...] = jnp.zeros_like(sm_ref)
            dws_ref[...] = jnp.zeros_like(dws_ref)
            dsg_ref[...] = jnp.zeros_like(dsg_ref)

        gc, hc, cin, x1, x2 = _conv_inputs(z_ref, hgc_ref, hhc_ref, i, cw, tm)
        w0, w1, w2 = cw_ref[0:1, :], cw_ref[1:2, :], cw_ref[2:3, :]
        cv = cb_ref[...] + w2 * cin + w1 * x1 + w0 * x2
        gb = z_ref[:, 0:cw].astype(F32)
        dyc = dy_ref[:, 0:cw].astype(F32)
        dz_ref[:, 0:cw] = (dyc * cv).astype(BF16)
        dcv = dyc * gb
        nxt = ndy_ref[...].astype(F32) * ngb_ref[...].astype(F32) * (i < nsteps - 1).astype(F32)
        row = lax.broadcasted_iota(jnp.int32, (tm, cw), 0)
        d1 = jnp.where(row == tm - 1, nxt[0:1], pltpu.roll(dcv, tm - 1, 0))
        d2 = jnp.where(row == tm - 1, nxt[1:2], jnp.where(row == tm - 2, nxt[0:1], pltpu.roll(dcv, tm - 2, 0)))
        dcin = w2 * dcv + w1 * d1 + w0 * d2
        dz_ref[:, cw:2 * cw] = (dcin * hc).astype(BF16)
        dz_ref[:, 2 * cw:3 * cw] = (dcin * gc).astype(BF16)
        sm_ref[0:1, :] += jnp.sum(dcv * x2, axis=0, keepdims=True)
        sm_ref[1:2, :] += jnp.sum(dcv * x1, axis=0, keepdims=True)
        sm_ref[2:3, :] += jnp.sum(dcv * cin, axis=0, keepdims=True)
        sm_ref[3:4, :] += jnp.sum(dcv, axis=0, keepdims=True)

        for h in range(heads):
            lo = h * GROUP
            vcol = slice(3 * cw + gw + lo, 3 * cw + gw + lo + GROUP)
            ucol = slice(3 * cw + lo, 3 * cw + lo + GROUP)
            vh = z_ref[:, vcol].astype(F32)
            rv = lax.rsqrt(jnp.mean(vh * vh, axis=-1, keepdims=True) + EPS)
            xh = vh * rv
            gvh = gv_ref[:, lo:lo + GROUP]
            vn = (xh * gvh).astype(BF16)
            w = _tril(ws_ref[h]).astype(BF16)
            dgv = jnp.zeros((1, GROUP), F32)
            for n in range(tm // GROUP):
                rows = slice(n * GROUP, (n + 1) * GROUP)
                sg = _dot_nn(w, vn[rows]) + bt_ref[:, h:h + 1]
                dyg = dy_ref[rows, cw + lo:cw + lo + GROUP].astype(F32)
                dsg = dyg * z_ref[rows, ucol].astype(F32)
                dz_ref[rows, ucol] = (dyg * sg).astype(BF16)
                dsgb = dsg.astype(BF16)
                dvn = _dot_tn(w, dsgb)
                dws_ref[h] += _dot_nt(dsgb, vn[rows])
                dsg_ref[:, lo:lo + GROUP] += dsg
                xhc = xh[rows]
                dgv = dgv + jnp.sum(dvn * xhc, axis=0, keepdims=True)
                dxh = dvn * gvh
                dv = rv[rows] * (dxh - xhc * jnp.mean(dxh * xhc, axis=-1, keepdims=True))
                dz_ref[rows, vcol] = dv.astype(BF16)
            sm_ref[4:5, lo:lo + GROUP] += dgv

        @pl.when(i == nsteps - 1)
        def _():
            for h in range(heads):
                dws_ref[h] = _tril(dws_ref[h])
                dbt_ref[:, h:h + 1] = jnp.sum(dsg_ref[:, h * GROUP:(h + 1) * GROUP], axis=-1, keepdims=True)

    fixed2 = lambda i: (0, 0)
    fixed3 = lambda i: (0, 0, 0)
    prev = lambda col: (lambda i: (jnp.maximum(i * hb - 1, 0), col))
    nxt_blk = lambda i: (jnp.minimum((i + 1) * hb, last_halo), 0)
    outs, job_outs = _call(
        name, body, (nsteps,),
        [pl.BlockSpec((tm, zc), lambda i: (i, 0)),
         pl.BlockSpec((HALO, cw), prev(1)), pl.BlockSpec((HALO, cw), prev(2)),
         pl.BlockSpec((HALO, cw), nxt_blk),
         pl.BlockSpec((tm, cw + gw), lambda i: (i, 0)), pl.BlockSpec((HALO, cw), nxt_blk),
         pl.BlockSpec(conv_w.shape, fixed2), pl.BlockSpec(conv_b.shape, fixed2),
         pl.BlockSpec(g_v.shape, fixed2), pl.BlockSpec(w_s.shape, fixed3), pl.BlockSpec(b_t.shape, fixed2)],
        [pl.BlockSpec((tm, zc), lambda i: (i, 0)), pl.BlockSpec((8, cw), fixed2),
         pl.BlockSpec(w_s.shape, fixed3), pl.BlockSpec(b_t.shape, fixed2)],
        [_sds((s, zc), BF16), _sds((8, cw), F32), _sds(w_s.shape, F32), _sds(b_t.shape, F32)],
        [z, z, z, z, dy, dy, conv_w, conv_b, g_v, w_s, b_t],
        scratch=[pltpu.VMEM((GROUP, gw), F32)], sem=("arbitrary",), jobs=jobs)
    return _ret(outs, job_outs, jobs, single=False)


def _softmax_rows(sc):
    e = jnp.exp(sc - jnp.max(sc, axis=-1, keepdims=True))
    return e / jnp.sum(e, axis=-1, keepdims=True)


def attn_fwd(name, q, k, v, tm=512, jobs=()):
    s, d = q.shape
    m = k.shape[0]
    hd = d // XA_HEADS
    scale = hd ** -0.5
    tm = _tile(s, tm, 8)

    def body(q_ref, k_ref, v_ref, o_ref):
        for h in range(XA_HEADS):
            cols = slice(h * hd, (h + 1) * hd)
            p = _softmax_rows(_dot_nt(q_ref[:, cols], k_ref[:, cols]) * scale)
            o_ref[:, cols] = _dot_nn(p.astype(BF16), v_ref[:, cols]).astype(BF16)

    outs, job_outs = _call(
        name, body, (s // tm,),
        [pl.BlockSpec((tm, d), lambda i: (i, 0)), pl.BlockSpec((m, d), lambda i: (0, 0)),
         pl.BlockSpec((m, d), lambda i: (0, 0))],
        [pl.BlockSpec((tm, d), lambda i: (i, 0))], [_sds((s, d), BF16)], [q, k, v], sem=("arbitrary",), jobs=jobs)
    return _ret(outs, job_outs, jobs)


def attn_bwd(name, q, k, v, do, tm=512):
    s, d = q.shape
    m = k.shape[0]
    hd = d // XA_HEADS
    scale = hd ** -0.5
    tm = _tile(s, tm, 8)

    def body(q_ref, k_ref, v_ref, do_ref, dq_ref, dk_ref, dv_ref):
        i = pl.program_id(0)

        @pl.when(i == 0)
        def _():
            dk_ref[...] = jnp.zeros_like(dk_ref)
            dv_ref[...] = jnp.zeros_like(dv_ref)

        for h in range(XA_HEADS):
            cols = slice(h * hd, (h + 1) * hd)
            qh = q_ref[:, cols]
            doh = do_ref[:, cols]
            p = _softmax_rows(_dot_nt(qh, k_ref[:, cols]) * scale)
            dp = _dot_nt(doh, v_ref[:, cols])
            ds = (p * (dp - jnp.sum(dp * p, axis=-1, keepdims=True)) * scale).astype(BF16)
            dq_ref[:, cols] = _dot_nn(ds, k_ref[:, cols]).astype(BF16)
            dk_ref[:, cols] += _dot_tn(ds, qh)
            dv_ref[:, cols] += _dot_tn(p.astype(BF16), doh)

    row = lambda i: (i, 0)
    fixed = lambda i: (0, 0)
    return _call(
        name, body, (s // tm,),
        [pl.BlockSpec((tm, d), row), pl.BlockSpec((m, d), fixed), pl.BlockSpec((m, d), fixed),
         pl.BlockSpec((tm, d), row)],
        [pl.BlockSpec((tm, d), row), pl.BlockSpec((m, d), fixed), pl.BlockSpec((m, d), fixed)],
        [_sds((s, d), BF16), _sds((m, d), F32), _sds((m, d), F32)], [q, k, v, do], sem=("arbitrary",))[0]


def _grid2(rows, cols, row_mult):
    tr, tc = _tile(rows, 512, row_mult), _tile(cols, 2048)
    return tr, tc, rows // tr, cols // tc


def cast_place(name, block, axis, place):
    r, c = block.shape
    tr, tc, nbr, nbc = _grid2(r, c, 16)
    if axis == 1:
        dst = lambda i, j, p: (i, j + p[0] * nbc)
    else:
        dst = lambda i, j, p: (i + p[0] * nbr, j)

    def body(p_ref, w_ref, out_ref):
        out_ref[...] = w_ref[...].astype(BF16)

    return pl.pallas_call(
        body, name=name,
        grid_spec=pltpu.PrefetchScalarGridSpec(
            num_scalar_prefetch=1, grid=(nbr, nbc),
            in_specs=[pl.BlockSpec((tr, tc), lambda i, j, p: (i, j))],
            out_specs=pl.BlockSpec((tr, tc), dst)),
        out_shape=_sds(_full_shape(block.shape, axis), BF16),
        compiler_params=_params(("parallel", "parallel")),
    )(place, block)


def pair_add(name, grad, peer, axis, place):
    hr, hc = peer.shape
    tr, tc, nbr, nbc = _grid2(hr, hc, 16)
    same = lambda i, j, p: (i, j)
    if grad.shape == peer.shape:
        mine = same
    elif axis == 1:
        mine = lambda i, j, p: (i + p[1] * nbr, j)
    else:
        mine = lambda i, j, p: (i, j + p[1] * nbc)

    def body(p_ref, g_ref, q_ref, out_ref):
        out_ref[...] = (g_ref[...].astype(F32) + q_ref[...].astype(F32)).astype(BF16)

    return pl.pallas_call(
        body, name=name,
        grid_spec=pltpu.PrefetchScalarGridSpec(
            num_scalar_prefetch=1, grid=(nbr, nbc),
            in_specs=[pl.BlockSpec((tr, tc), mine), pl.BlockSpec((tr, tc), same)],
            out_specs=pl.BlockSpec((tr, tc), same)),
        out_shape=_sds((hr, hc), BF16),
        compiler_params=_params(("parallel", "parallel")),
    )(place, grad, peer)


def cross_sum(name, part, land, axis, shape, place):
    _, sr, sc = land.shape
    tr, tc, nbr, nbc = _grid2(sr, sc, 16)
    if axis == 1:
        own = lambda i, j, p: (i, j + p[0] * nbc)
        dst = lambda i, j, p: (i + p[1] * nbr, j)
    else:
        own = lambda i, j, p: (i + p[0] * nbr, j)
        dst = lambda i, j, p: (i, j + p[1] * nbc)

    def body(p_ref, own_ref, land_ref, out_ref):
        out_ref[...] = ((own_ref[...].astype(F32) + land_ref[0].astype(F32))
                        + (land_ref[1].astype(F32) + land_ref[2].astype(F32)))

    return pl.pallas_call(
        body, name=name,
        grid_spec=pltpu.PrefetchScalarGridSpec(
            num_scalar_prefetch=1, grid=(nbr, nbc),
            in_specs=[pl.BlockSpec((tr, tc), own), pl.BlockSpec((3, tr, tc), lambda i, j, p: (0, i, j))],
            out_specs=pl.BlockSpec((tr, tc), dst)),
        out_shape=_sds(_block(shape, axis), F32),
        compiler_params=_params(("parallel", "parallel")),
    )(place, part, land)


def _adam_math(w, g, m, v):
    m = ADAM_B1 * m + (1.0 - ADAM_B1) * g
    v = ADAM_B2 * v + (1.0 - ADAM_B2) * (g * g)
    m_hat = m / (1.0 - ADAM_B1 ** ADAM_STEP)
    v_hat = v / (1.0 - ADAM_B2 ** ADAM_STEP)
    delta = -ADAM_LR * (m_hat / (jnp.sqrt(v_hat) + ADAM_EPS) + ADAM_WD * w)
    return delta, m, v


def adamw(name, w, g, m, v, jobs=()):
    r, c = w.shape
    tr, tc = _tile(r, 256, 8), _tile(c, 1408)

    def body(w_ref, g_ref, m_ref, v_ref, g_out, d_out, m_out, v_out):
        d, mm, vv = _adam_math(w_ref[...], g_ref[...], m_ref[...], v_ref[...])
        g_out[...] = g_ref[...]
        d_out[...] = d
        m_out[...] = mm
        v_out[...] = vv

    spec = pl.BlockSpec((tr, tc), lambda i, j: (i, j))
    outs, job_outs = _call(name, body, (r // tr, c // tc), [spec] * 4, [spec] * 4, [_sds((r, c), F32)] * 4,
                           [w, g, m, v], sem=("parallel", "parallel"), jobs=jobs)
    return _ret(outs, job_outs, jobs, single=False)


def small_sum(name, stacks):
    def body(*refs):
        for s_ref, out_ref in zip(refs[:len(stacks)], refs[len(stacks):]):
            acc = s_ref[0]
            for d in range(1, s_ref.shape[0]):
                acc = acc + s_ref[d]
            out_ref[...] = acc

    return pl.pallas_call(body, name=name, out_shape=[_sds(s.shape[1:], F32) for s in stacks])(*stacks)


WEIGHTS = ["g_ffn1", "w_ffn1_in", "w_ffn1_out", "g_mix", "w_mix_in", "conv_w", "conv_b", "g_gm_v", "w_spatial",
           "b_spatial", "w_mix_out", "g_xattn", "g_mem", "w_xq", "w_xk", "w_xv", "w_xo", "g_ffn2", "w_ffn2_in",
           "w_ffn2_out", "g_final"]
BIG = {"w_ffn1_in": 1, "w_ffn1_out": 0, "w_mix_in": 1, "w_mix_out": 0, "w_xq": 0, "w_xk": 0, "w_xv": 0, "w_xo": 0,
       "w_ffn2_in": 1, "w_ffn2_out": 0}
SMALL = [n for n in WEIGHTS if n not in BIG]
LATE_SMALL = ["g_ffn1"]
EARLY_SMALL = [n for n in SMALL if n not in LATE_SMALL]


def _pack(arrays):
    flat = jnp.concatenate([a.reshape(-1) for a in arrays])
    rows = -(-flat.shape[0] // 1024) * 8
    return jnp.pad(flat, (0, rows * 128 - flat.shape[0])).reshape(rows, 128)


def _unpack(buf, shapes):
    flat = buf.reshape(-1)
    out, pos = [], 0
    for shp in shapes:
        n = math.prod(shp)
        out.append(flat[pos:pos + n].reshape(shp))
        pos += n
    return out


def kernel(x, mem, g_ffn1, w_ffn1_in, w_ffn1_out, g_mix, w_mix_in, conv_w, conv_b, g_gm_v, w_spatial, b_spatial, w_mix_out, g_xattn, g_mem, w_xq, w_xk, w_xv, w_xo, g_ffn2, w_ffn2_in, w_ffn2_out, g_final, loss_target, m_g_ffn1, m_w_ffn1_in, m_w_ffn1_out, m_g_mix, m_w_mix_in, m_conv_w, m_conv_b, m_g_gm_v, m_w_spatial, m_b_spatial, m_w_mix_out, m_g_xattn, m_g_mem, m_w_xq, m_w_xk, m_w_xv, m_w_xo, m_g_ffn2, m_w_ffn2_in, m_w_ffn2_out, m_g_final, v_g_ffn1, v_w_ffn1_in, v_w_ffn1_out, v_g_mix, v_w_mix_in, v_conv_w, v_conv_b, v_g_gm_v, v_w_spatial, v_b_spatial, v_w_mix_out, v_g_xattn, v_g_mem, v_w_xq, v_w_xk, v_w_xv, v_w_xo, v_g_ffn2, v_w_ffn2_in, v_w_ffn2_out, v_g_final):
    given = dict(locals())
    wts = {n: given[n] for n in WEIGHTS}
    mom = {n: given["m_" + n] for n in WEIGHTS}
    var = {n: given["v_" + n] for n in WEIGHTS}

    xi, yi, ci = lax.axis_index("x"), lax.axis_index("y"), lax.axis_index("c")
    blk = 2 * xi + yi
    place = jnp.stack([blk, ci]).astype(jnp.int32)

    x2, mem2, tgt = x[0], mem[0], loss_target[0]
    own = {n: cast_place("cast_" + n, wts[n][0], BIG[n], place) for n in BIG}
    shape = {n: own[n].shape for n in BIG}
    w_s, b_t = w_spatial[0], b_spatial[0].T
    gf = g_final[None]

    def gather(*names):
        return gather_job([(own[n], BIG[n], WHOLE, WHOLE) for n in names])

    def gather_part(arr, sub=WHOLE, within=WHOLE):
        return gather_job([(arr, 1, sub, within)])

    full = {}
    left, right = (0, 1, 2), (1, 1, 2)
    half_cols = dict(tm=512, tn=shape["w_ffn1_in"][1] // (2 * N_CHIPS), stride=2)

    (w1in,), (conv_taps,) = comm_only(
        "gather_first", [gather_part(own["w_ffn1_in"], within=left),
                         columns_job(jnp.pad(conv_w[0], ((0, 8 - CONV_K), (0, 0))))])
    n1, r1 = rmsnorm_fwd("norm1", x2, g_ffn1)
    halves, ((w1in,),) = swiglu_fwd("ffn1_in_left", n1, w1in, phase=0, jobs=[gather_part(w1in, within=right)],
                                    **half_cols)
    full["w_ffn1_in"] = w1in
    (gu1, a1), ((full["w_ffn1_out"],),) = swiglu_fwd("ffn1_in_right", n1, w1in, phase=1, prev=halves,
                                                     jobs=[gather("w_ffn1_out")], **half_cols)
    h1, ((full["w_mix_in"],),) = mm_nn_resid("ffn1_out", a1, full["w_ffn1_out"], x2, 0.5, tm=512, tk=5632,
                                             jobs=[gather("w_mix_in")])
    n2, r2 = rmsnorm_fwd("norm2", h1, g_mix)
    z, ((full["w_mix_out"], full["w_xq"]), (w2in,)) = mm_nn(
        "mix_in", n2, full["w_mix_in"], BF16,
        jobs=[gather("w_mix_out", "w_xq"), gather_part(own["w_ffn2_in"], (0, 1, 4), left)])
    ycat, ((full["w_xk"],),) = mixer_fwd("mixer", z, conv_taps, conv_b, g_gm_v, w_s, b_t, jobs=[gather("w_xk")])
    h2, ((full["w_xv"],),) = mm_nn_resid("mix_out", ycat, full["w_mix_out"], h1, 1.0, tk=2048,
                                         jobs=[gather("w_xv")])
    n3, r3 = rmsnorm_fwd("norm3", h2, g_xattn)
    mn, rm = rmsnorm_fwd("norm_mem", mem2, g_mem)
    q, ((full["w_xo"],),) = mm_nn("xq", n3, full["w_xq"], BF16, jobs=[gather("w_xo")])
    k = mm_nn("xk", mn, full["w_xk"], BF16)
    v = mm_nn("xv", mn, full["w_xv"], BF16)
    o, ((w2in,),) = attn_fwd("attn", q, k, v, jobs=[gather_part(w2in, (1, 1, 4), left)])
    h3, ((w2in,),) = mm_nn_resid("xo", o, full["w_xo"], h2, 1.0, tk=2048, jobs=[gather_part(w2in, (2, 2, 4), left)])
    n4, r4 = rmsnorm_fwd("norm4", h3, g_ffn2)
    halves, ((w2in,),) = swiglu_fwd("ffn2_in_left", n4, w2in, phase=0, jobs=[gather_part(w2in, within=right)],
                                    **half_cols)
    full["w_ffn2_in"] = w2in
    (gu2, a2), ((full["w_ffn2_out"],),) = swiglu_fwd("ffn2_in_right", n4, w2in, phase=1, prev=halves,
                                                     jobs=[gather("w_ffn2_out")], **half_cols)
    h4 = mm_nn_resid("ffn2_out", a2, full["w_ffn2_out"], h3, 0.5, tm=512, tk=5632)
    loss_blk, dh4, dh4b, dg_final = loss_head("loss_head", h4, gf, tgt)

    dw, peer, part, land, half, grads = {}, {}, {}, {}, {}, {}

    def send_pair(*names):
        return pair_job([dw[n] for n in names], [BIG[n] for n in names])

    def take_pair(names, got):
        for n, p in zip(names, got):
            part[n] = pair_add("pair_add_" + n, dw[n], p, BIG[n], place)

    def send_cross(*names, sub=WHOLE):
        return cross_job([(part[n], BIG[n], shape[n], land.get(n), sub) for n in names])

    def take_cross(names, got, last=True):
        for n, l in zip(names, got):
            land[n] = l
            if last:
                half[n] = cross_sum("cross_sum_" + n, part[n], l, BIG[n], shape[n], place)

    def send_final(*names):
        return final_job([half[n] for n in names], [BIG[n] for n in names], [shape[n] for n in names])

    delta, new_m, new_v = {}, {}, {}

    reduced = {}

    def take_final(names, got):
        for n, g in zip(names, got):
            reduced[n] = g

    def update(n, jobs=()):
        res = adamw("adamw_" + n, wts[n][0], reduced[n], mom[n][0], var[n][0], jobs=jobs)
        (grads[n], delta[n], new_m[n], new_v[n]), job_outs = res if jobs else (res, [])
        return job_outs

    dgu2 = swiglu_bwd("ffn2_dact", dh4b, full["w_ffn2_out"], gu2, 0.5)
    dw["w_ffn2_in"] = mm_tn_pair("ffn2_dwin", n4, dgu2, BF16)
    dw["w_ffn2_out"], (got,) = mm_tn("ffn2_dwout", a2, dh4b, BF16, scale=0.5, jobs=[send_pair("w_ffn2_in")])
    take_pair(["w_ffn2_in"], got)
    dn4, (got_c, got_p) = mm_nt_pair("ffn2_dn", dgu2, full["w_ffn2_in"], F32,
                                     jobs=[send_cross("w_ffn2_in", sub=(0, 7, 8)), send_pair("w_ffn2_out")])
    take_cross(["w_ffn2_in"], got_c, last=False)
    take_pair(["w_ffn2_out"], got_p)
    (dh3, dh3b, dg_ffn2), (got_c,) = rmsnorm_bwd("norm4_bwd", dn4, h3, r4, g_ffn2, dh4,
                                                 jobs=[send_cross("w_ffn2_in", sub=(7, 1, 8))])
    take_cross(["w_ffn2_in"], got_c)

    dw["w_xo"], (got_c,) = mm_tn("xo_dw", o, dh3b, BF16, jobs=[send_cross("w_ffn2_out", sub=(0, 2, 8))])
    take_cross(["w_ffn2_out"], got_c, last=False)
    do, (got_c, got_f) = mm_nt("xo_dx", dh3b, full["w_xo"], BF16,
                               jobs=[send_cross("w_ffn2_out", sub=(2, 2, 8)), send_final("w_ffn2_in")])
    take_cross(["w_ffn2_out"], got_c, last=False)
    take_final(["w_ffn2_in"], got_f)
    update("w_ffn2_in")
    dq, dk, dv = attn_bwd("attn_bwd", q, k, v, do)
    dkb, dvb = dk.astype(BF16), dv.astype(BF16)
    dw["w_xq"], (got_c,) = mm_tn("xq_dw", n3, dq, BF16, jobs=[send_cross("w_ffn2_out", sub=(4, 2, 8))])
    take_cross(["w_ffn2_out"], got_c, last=False)
    dn3, (got_c,) = mm_nt("xq_dx", dq, full["w_xq"], F32, jobs=[send_cross("w_ffn2_out", sub=(6, 2, 8))])
    take_cross(["w_ffn2_out"], got_c)
    (dh2, dh2b, dg_xattn), (got_f,) = rmsnorm_bwd("norm3_bwd", dn3, h2, r3, g_xattn, dh3,
                                                  jobs=[send_final("w_ffn2_out")])
    take_final(["w_ffn2_out"], got_f)
    update("w_ffn2_out")
    dw["w_xk"] = mm_tn("xk_dw", mn, dkb, BF16)
    dw["w_xv"] = mm_tn("xv_dw", mn, dvb, BF16)
    dmn_k = mm_nt("xk_dx", dkb, full["w_xk"], F32)
    dmn_v = mm_nt("xv_dx", dvb, full["w_xv"], F32)
    dg_mem = gain_grad("norm_mem_bwd", dmn_k, dmn_v, mem2, rm)

    dw["w_mix_out"] = mm_tn("mix_out_dw", ycat, dh2b, BF16)
    attn_names = ["w_xo", "w_xq", "w_xk", "w_xv", "w_mix_out"]
    dycat, (got_p,) = mm_nt("mix_out_dx", dh2b, full["w_mix_out"], BF16, jobs=[send_pair(*attn_names)])
    take_pair(attn_names, got_p)
    (dz, dsmall, dws, dbt), (got_c,) = mixer_bwd("mixer_bwd", z, dycat, conv_taps, conv_b, g_gm_v, w_s, b_t,
                                                 jobs=[send_cross("w_xo", "w_xq")])
    take_cross(["w_xo", "w_xq"], got_c)
    dw["w_mix_in"], (got_c,) = mm_tn("mix_in_dw", n2, dz, BF16, jobs=[send_cross("w_xk", "w_xv")])
    take_cross(["w_xk", "w_xv"], got_c)
    dn2, (got_c, got_p) = mm_nt("mix_in_dx", dz, full["w_mix_in"], F32, tk=2560,
                                jobs=[send_cross("w_mix_out"), send_pair("w_mix_in")])
    take_cross(["w_mix_out"], got_c)
    take_pair(["w_mix_in"], got_p)
    (dh1, dh1b, dg_mix), (got_f,) = rmsnorm_bwd("norm2_bwd", dn2, h1, r2, g_mix, dh2,
                                                jobs=[send_final(*attn_names)])
    take_final(attn_names, got_f)
    for n in attn_names:
        update(n)

    dw["w_ffn1_out"], (got_c,) = mm_tn("ffn1_dwout", a1, dh1b, BF16, scale=0.5, jobs=[send_cross("w_mix_in")])
    take_cross(["w_mix_in"], got_c)
    early = {"g_mix": dg_mix, "conv_w": dsmall[0:CONV_K], "conv_b": dsmall[3:4], "g_gm_v": dsmall[4:5],
             "w_spatial": dws, "b_spatial": dbt.T, "g_xattn": dg_xattn, "g_mem": dg_mem, "g_ffn2": dg_ffn2,
             "g_final": dg_final}
    dgu1, (got_p, got_f, (early_all,)) = swiglu_bwd(
        "ffn1_dact", dh1b, full["w_ffn1_out"], gu1, 0.5,
        jobs=[send_pair("w_ffn1_out"), send_final("w_mix_in"), stack_job(_pack([early[n] for n in EARLY_SMALL]))])
    take_pair(["w_ffn1_out"], got_p)
    take_final(["w_mix_in"], got_f)
    update("w_mix_in")
    theirs, (got_c,) = mm_tn_pair_half("ffn1_dwin_theirs", n1, dgu1, BF16, place, False,
                                       jobs=[send_cross("w_ffn1_out", sub=(0, 7, 8))])
    take_cross(["w_ffn1_out"], got_c, last=False)
    mine, (got_c, (from_sibling,)) = mm_tn_pair_half(
        "ffn1_dwin_mine", n1, dgu1, BF16, place, True,
        jobs=[send_cross("w_ffn1_out", sub=(7, 1, 8)), pair_job([theirs], [1], is_half=True)])
    take_cross(["w_ffn1_out"], got_c)
    part["w_ffn1_in"] = pair_add("pair_add_w_ffn1_in", mine, from_sibling, 1, place)
    dn1, (got_c, got_f) = mm_nt_pair("ffn1_dn", dgu1, full["w_ffn1_in"], F32,
                                     jobs=[send_cross("w_ffn1_in", sub=(0, 7, 8)), send_final("w_ffn1_out")])
    take_cross(["w_ffn1_in"], got_c, last=False)
    take_final(["w_ffn1_out"], got_f)
    update("w_ffn1_out")
    dx, _, dg_ffn1 = rmsnorm_bwd("norm1_bwd", dn1, x2, r1, g_ffn1, dh1)
    got_c, (late_all,) = comm_only("tail_cross", [send_cross("w_ffn1_in", sub=(7, 1, 8)),
                                                  stack_job(_pack([dg_ffn1]))])
    take_cross(["w_ffn1_in"], got_c)
    (got_f,) = comm_only("tail_final", [send_final("w_ffn1_in")])
    take_final(["w_ffn1_in"], got_f)
    update("w_ffn1_in")

    early_sum, late_sum = small_sum("small_sum", [early_all, late_all])
    for n, g in zip(EARLY_SMALL, _unpack(early_sum, [early[n].shape for n in EARLY_SMALL])):
        grads[n] = g
    grads["g_ffn1"] = _unpack(late_sum, [dg_ffn1.shape])[0]
    taps_cols = conv_w.shape[2]
    grads["conv_w"] = lax.dynamic_slice_in_dim(grads["conv_w"], blk * taps_cols, taps_cols, axis=1)
    packed = [_pack([src[n] for n in SMALL]) for src in (wts, grads, mom, var)]
    own_shapes = [wts[n].shape for n in SMALL]
    for dst, buf in zip((delta, new_m, new_v), adamw("adamw_small", *packed)[1:]):
        for n, a in zip(SMALL, _unpack(buf, own_shapes)):
            dst[n] = a

    loss = lax.psum(loss_blk[0, 0], ("x", "y", "c"))
    outs = [loss, dx[None]]
    for group in (grads, delta, new_m, new_v):
        outs += [group[n].reshape(wts[n].shape) for n in WEIGHTS]
    return tuple(outs)
```

```python
import math

import jax
import jax.numpy as jnp
from jax import lax
from jax.experimental import pallas as pl
from jax.experimental.pallas import tpu as pltpu

F32 = jnp.float32
BF16 = jnp.bfloat16
EPS = 1e-6
GROUP = 128
XA_HEADS = 4
CONV_K = 3
N_CHIPS = 4
VMEM_LIMIT_BYTES = 56 * 1024 * 1024

ADAM_LR = 0.001
ADAM_B1 = 0.9
ADAM_B2 = 0.999
ADAM_EPS = 1e-08
ADAM_WD = 0.01
ADAM_STEP = 10

MESH = pl.DeviceIdType.MESH
ANY = pl.BlockSpec(memory_space=pl.ANY)


def _tile(dim, pref, mult=128):
    if dim <= pref:
        return dim
    t = (pref // mult) * mult
    while t >= mult:
        if dim % t == 0:
            return t
        t -= mult
    raise ValueError(f"no tile for {dim} under {pref}")


def _params(sem):
    return pltpu.CompilerParams(dimension_semantics=sem, vmem_limit_bytes=VMEM_LIMIT_BYTES)


def _sds(shape, dtype):
    return jax.ShapeDtypeStruct(shape, dtype)


def _dot_nn(a, b):
    return jnp.dot(a, b, preferred_element_type=F32)


def _dot_nt(a, b):
    return lax.dot_general(a, b, (((1,), (1,)), ((), ())), preferred_element_type=F32)


def _dot_tn(a, b):
    return lax.dot_general(a, b, (((0,), (0,)), ((), ())), preferred_element_type=F32)


class Job:
    def __init__(self, inputs, out_shapes, aliases, sems, start, middle, finish):
        self.inputs, self.out_shapes, self.aliases, self.sems = inputs, out_shapes, aliases, sems
        self.start, self.middle, self.finish = start, middle, finish


def _place():
    x, y, c = lax.axis_index("x"), lax.axis_index("y"), lax.axis_index("c")
    chips = [(1 - x, y), (x, 1 - y), (1 - x, 1 - y)]
    return x, y, c, chips


def _ds(start, size, lane):
    if not isinstance(start, int):
        start = pl.multiple_of(start, 128 if lane else 16)
    return pl.ds(start, size)


WHOLE = (0, 1, 1)


def _window(ref, axis, shape, blk=None, half=None, sub=WHOLE, within=WHOLE):
    n = shape[axis] // N_CHIPS
    hs = shape[1 - axis] // 2
    idx = [slice(None), slice(None)]
    if blk is not None:
        b_first, b_count, b_pieces = within
        b_ext = n // b_pieces
        idx[axis] = _ds(blk * n + b_first * b_ext, b_count * b_ext, axis == 1)
    first, count, pieces = sub
    ext = hs // pieces
    if half is not None:
        idx[1 - axis] = _ds(half * hs + first * ext, count * ext, axis == 0)
    elif pieces > 1:
        idx[1 - axis] = _ds(first * ext, count * ext, axis == 0)
    return ref.at[tuple(idx)]


def _remote(src, dst, send_sem, recv_sem, dev):
    return pltpu.make_async_remote_copy(src_ref=src, dst_ref=dst, send_sem=send_sem, recv_sem=recv_sem,
                                        device_id=dev, device_id_type=MESH)


def _full_shape(block_shape, axis):
    out = list(block_shape)
    out[axis] *= N_CHIPS
    return tuple(out)


def _half_all(shape, axis):
    out = list(shape)
    out[1 - axis] //= 2
    return tuple(out)


def _block(shape, axis):
    out = list(shape)
    out[axis] //= N_CHIPS
    return tuple(out)


def _half_block(shape, axis):
    return _half_all(_block(shape, axis), axis)


def gather_job(items):
    nw = len(items)
    shapes = [item[0].shape for item in items]
    n_sem = 8

    def parts(sub):
        first, count, pieces = sub
        return (2 * first, count, 2 * pieces), (2 * first + count, count, 2 * pieces)

    def start(pos, ins, outs, sems):
        x, y, c, chips = pos
        for w, (_, ax, sub, within) in enumerate(items):
            mine = _window(outs[w], ax, shapes[w], blk=2 * x + y, half=c, sub=sub, within=within)
            for j in range(2):
                _remote(mine, mine, sems[0].at[n_sem * w + j], sems[1].at[n_sem * w + j], (*chips[j], c)).start()

    def middle(pos, ins, outs, sems):
        x, y, c, chips = pos
        for w, (_, ax, sub, within) in enumerate(items):
            for j in range(2):
                cx, cy = chips[j]
                landed = _window(outs[w], ax, shapes[w], blk=2 * cx + cy, half=c, sub=sub, within=within)
                _remote(landed, landed, sems[0].at[n_sem * w + j], sems[1].at[n_sem * w + j], (cx, cy, c)).wait_recv()
                part = _window(outs[w], ax, shapes[w], blk=2 * cx + cy, half=c, sub=parts(sub)[j], within=within)
                _remote(part, part, sems[0].at[n_sem * w + 2 + j], sems[1].at[n_sem * w + 2 + j],
                        (*chips[1 - j], c)).start()
                _remote(landed, landed, sems[0].at[n_sem * w + 4 + j], sems[1].at[n_sem * w + 4 + j],
                        (x, y, 1 - c)).start()

    def finish(pos, ins, outs, sems):
        x, y, c, chips = pos
        sib = (x, y, 1 - c)
        for w, (_, ax, sub, within) in enumerate(items):
            dx, dy = chips[2]
            for j in range(2):
                part = _window(outs[w], ax, shapes[w], blk=2 * dx + dy, half=c, sub=parts(sub)[j], within=within)
                cp = _remote(part, part, sems[0].at[n_sem * w + 2 + j], sems[1].at[n_sem * w + 2 + j], sib)
                cp.wait_recv()
                cp.wait_send()
            diag = _window(outs[w], ax, shapes[w], blk=2 * dx + dy, half=c, sub=sub, within=within)
            _remote(diag, diag, sems[0].at[n_sem * w + 6], sems[1].at[n_sem * w + 6], sib).start()
        for w, (_, ax, sub, within) in enumerate(items):
            for j, (cx, cy) in enumerate(chips):
                passed = _window(outs[w], ax, shapes[w], blk=2 * cx + cy, half=1 - c, sub=sub, within=within)
                cp = _remote(passed, passed, sems[0].at[n_sem * w + 4 + j], sems[1].at[n_sem * w + 4 + j], sib)
                cp.wait_recv()
                cp.wait_send()
            mine = _window(outs[w], ax, shapes[w], blk=2 * x + y, half=c, sub=sub, within=within)
            for j in range(2):
                _remote(mine, mine, sems[0].at[n_sem * w + j], sems[1].at[n_sem * w + j], sib).wait_send()

    sems = [pltpu.SemaphoreType.DMA((n_sem * nw,)), pltpu.SemaphoreType.DMA((n_sem * nw,))]
    return Job([item[0] for item in items], [_sds(item[0].shape, item[0].dtype) for item in items],
               {w: w for w in range(nw)}, sems, start, middle, finish)


def pair_job(grads, axes, is_half=False):
    nw = len(grads)
    shapes = [g.shape for g in grads]

    def start(pos, ins, outs, sems):
        x, y, c, _ = pos
        for w in range(nw):
            src = ins[w] if is_half else _window(ins[w], axes[w], shapes[w], half=1 - c)
            _remote(src, outs[w], sems[0].at[w], sems[1].at[w], (x, y, 1 - c)).start()

    def finish(pos, ins, outs, sems):
        x, y, c, _ = pos
        for w in range(nw):
            cp = _remote(outs[w], outs[w], sems[0].at[w], sems[1].at[w], (x, y, 1 - c))
            cp.wait_recv()
            cp.wait_send()

    sems = [pltpu.SemaphoreType.DMA((nw,)), pltpu.SemaphoreType.DMA((nw,))]
    out_shapes = [_sds(s if is_half else _half_all(s, a), BF16) for s, a in zip(shapes, axes)]
    return Job(list(grads), out_shapes, {}, sems, start, None, finish)


def cross_job(items):
    nw = len(items)
    inputs, aliases = [], {}
    for w, (part, ax, shape, prev, sub) in enumerate(items):
        inputs.append(part)
        if prev is not None:
            aliases[len(inputs)] = w
            inputs.append(prev)

    def copies(pos, ins, outs, sems):
        x, y, c, chips = pos
        k = 0
        for w, (_, ax, shape, prev, sub) in enumerate(items):
            src = ins[k]
            k += 2 if prev is not None else 1
            for j, (cx, cy) in enumerate(chips):
                slot = _window(outs[w].at[j], ax, shape, sub=sub)
                yield (_remote(_window(src, ax, shape, blk=2 * cx + cy, sub=sub), slot,
                               sems[0].at[3 * w + j], sems[1].at[3 * w + j], (cx, cy, c)),
                       _remote(slot, slot, sems[0].at[3 * w + j], sems[1].at[3 * w + j], (cx, cy, c)))

    def start(pos, ins, outs, sems):
        for send, _ in copies(pos, ins, outs, sems):
            send.start()

    def finish(pos, ins, outs, sems):
        for send, recv in copies(pos, ins, outs, sems):
            recv.wait_recv()
            send.wait_send()

    sems = [pltpu.SemaphoreType.DMA((3 * nw,)), pltpu.SemaphoreType.DMA((3 * nw,))]
    out_shapes = [_sds((3,) + _half_block(shape, ax), BF16) for _, ax, shape, _, _ in items]
    return Job(inputs, out_shapes, aliases, sems, start, None, finish)


def final_job(blocks, axes, shapes):
    nw = len(blocks)

    def start(pos, ins, outs, sems):
        x, y, c, _ = pos
        for w in range(nw):
            mine = _window(outs[w], axes[w], shapes[w], half=c)
            _remote(mine, mine, sems[0].at[w], sems[1].at[w], (x, y, 1 - c)).start()

    def finish(pos, ins, outs, sems):
        x, y, c, _ = pos
        for w in range(nw):
            theirs = _window(outs[w], axes[w], shapes[w], half=1 - c)
            cp = _remote(theirs, theirs, sems[0].at[w], sems[1].at[w], (x, y, 1 - c))
            cp.wait_recv()
            cp.wait_send()

    sems = [pltpu.SemaphoreType.DMA((nw,)), pltpu.SemaphoreType.DMA((nw,))]
    return Job(list(blocks), [_sds(b.shape, b.dtype) for b in blocks], {w: w for w in range(nw)}, sems, start, None,
               finish)


def stack_job(small):
    def peers(pos):
        x, y, c, _ = pos
        for k in range(1, 8):
            yield k - 1, (1 - x if k & 4 else x, 1 - y if k & 2 else y, 1 - c if k & 1 else c)

    def start(pos, ins, outs, sems):
        x, y, c, _ = pos
        mine = outs[0].at[4 * x + 2 * y + c]
        pltpu.make_async_copy(ins[0], mine, sems[2]).start()
        for k, dev in peers(pos):
            _remote(ins[0], mine, sems[0].at[k], sems[1].at[k], dev).start()

    def finish(pos, ins, outs, sems):
        x, y, c, _ = pos
        for k, (px, py, pc) in peers(pos):
            slot = outs[0].at[4 * px + 2 * py + pc]
            cp = _remote(slot, slot, sems[0].at[k], sems[1].at[k], (px, py, pc))
            cp.wait_recv()
            cp.wait_send()
        pltpu.make_async_copy(ins[0], outs[0].at[4 * x + 2 * y + c], sems[2]).wait()

    sems = [pltpu.SemaphoreType.DMA((7,)), pltpu.SemaphoreType.DMA((7,)), pltpu.SemaphoreType.DMA]
    return Job([small], [_sds((8,) + small.shape, small.dtype)], {}, sems, start, None, finish)


def columns_job(block):
    cols = block.shape[1]
    place = lambda out, b: out.at[:, _ds(b * cols, cols, True)]

    def start(pos, ins, outs, sems):
        x, y, c, chips = pos
        pltpu.make_async_copy(ins[0], place(outs[0], 2 * x + y), sems[2]).start()
        for j, (cx, cy) in enumerate(chips):
            _remote(ins[0], place(outs[0], 2 * x + y), sems[0].at[j], sems[1].at[j], (cx, cy, c)).start()

    def finish(pos, ins, outs, sems):
        x, y, c, chips = pos
        for j, (cx, cy) in enumerate(chips):
            got = place(outs[0], 2 * cx + cy)
            cp = _remote(got, got, sems[0].at[j], sems[1].at[j], (cx, cy, c))
            cp.wait_recv()
            cp.wait_send()
        pltpu.make_async_copy(ins[0], place(outs[0], 2 * x + y), sems[2]).wait()

    sems = [pltpu.SemaphoreType.DMA((3,)), pltpu.SemaphoreType.DMA((3,)), pltpu.SemaphoreType.DMA]
    return Job([block], [_sds((block.shape[0], N_CHIPS * cols), block.dtype)], {}, sems, start, None, finish)


def _call(name, body, grid, in_specs, out_specs, out_shape, args, scratch=(), sem=None, jobs=(), place=None,
          carried=None):
    n_in, n_out, n_sc = len(args), len(out_shape), len(scratch)
    carried = dict(carried or {})

    def launch(fn, in_specs, out_specs, out_shape, scratch, aliases, sem, operands):
        if place is None:
            return pl.pallas_call(
                fn, name=name, grid=grid, in_specs=in_specs, out_specs=out_specs, out_shape=out_shape,
                scratch_shapes=scratch, input_output_aliases=aliases, compiler_params=_params(sem))(*operands)
        spec = pltpu.PrefetchScalarGridSpec(num_scalar_prefetch=1, grid=grid, in_specs=in_specs,
                                            out_specs=out_specs, scratch_shapes=scratch)
        return pl.pallas_call(
            lambda p_ref, *refs: fn(*refs), name=name, grid_spec=spec, out_shape=out_shape,
            input_output_aliases={k + 1: v for k, v in aliases.items()}, compiler_params=_params(sem),
        )(place, *operands)

    if not jobs:
        outs = launch(body, list(in_specs), list(out_specs), list(out_shape), list(scratch), carried, sem, args)
        return list(outs), []

    total = math.prod(grid) if grid else 1
    mid = min(total - 1, (2 * total) // 3)

    def split(refs, start, counts):
        out = []
        for n in counts:
            out.append(refs[start:start + n])
            start += n
        return out, start

    def wrapped(*refs):
        c_in = refs[:n_in]
        j_ins, p = split(refs, n_in, [len(j.inputs) for j in jobs])
        c_out = refs[p:p + n_out]
        j_outs, p = split(refs, p + n_out, [len(j.out_shapes) for j in jobs])
        c_sc = refs[p:p + n_sc]
        j_sems, p = split(refs, p + n_sc, [len(j.sems) for j in jobs])
        pos = _place()
        step = 0
        for axis, extent in enumerate(grid):
            step = step * extent + pl.program_id(axis)

        def run(phase):
            for j, ins, outs, sems in zip(jobs, j_ins, j_outs, j_sems):
                fn = getattr(j, phase)
                if fn is not None:
                    fn(pos, ins, outs, sems)

        if total == 1:
            run("start")
            body(*c_in, *c_out, *c_sc)
            run("middle")
            run("finish")
            return
        pl.when(step == 0)(lambda: run("start"))
        body(*c_in, *c_out, *c_sc)
        if any(j.middle is not None for j in jobs):
            pl.when(step == mid)(lambda: run("middle"))
        pl.when(step == total - 1)(lambda: run("finish"))

    aliases, in_at, out_at = carried, n_in, n_out
    for j in jobs:
        for src, dst in j.aliases.items():
            aliases[in_at + src] = out_at + dst
        in_at += len(j.inputs)
        out_at += len(j.out_shapes)
    outs = launch(
        wrapped, list(in_specs) + [ANY] * (in_at - n_in), list(out_specs) + [ANY] * (out_at - n_out),
        list(out_shape) + [s for j in jobs for s in j.out_shapes],
        list(scratch) + [s for j in jobs for s in j.sems], aliases, ("arbitrary",) * len(grid),
        [*args, *[a for j in jobs for a in j.inputs]])
    job_outs, p = split(outs, n_out, [len(j.out_shapes) for j in jobs])
    return list(outs[:n_out]), [list(o) for o in job_outs]


def comm_only(name, jobs):
    def body(dummy_ref, out_ref):
        out_ref[...] = dummy_ref[...]

    dummy = jnp.zeros((8, 128), F32)
    spec = pl.BlockSpec((8, 128), lambda: (0, 0))
    return _call(name, body, (), [spec], [spec], [_sds((8, 128), F32)], [dummy], jobs=jobs)[1]


def _ret(outs, job_outs, jobs, single=True):
    res = outs[0] if single else outs
    return (res, job_outs) if jobs else res


def rmsnorm_fwd(name, x, g):
    s, d = x.shape
    tm = _tile(s, 512, 8)

    def body(x_ref, g_ref, n_ref, r_ref):
        xv = x_ref[...]
        r = lax.rsqrt(jnp.mean(xv * xv, axis=-1, keepdims=True) + EPS)
        n_ref[...] = (xv * r * g_ref[...]).astype(BF16)
        r_ref[...] = r

    row = lambda i: (i, 0)
    return _call(
        name, body, (s // tm,),
        [pl.BlockSpec((tm, d), row), pl.BlockSpec((1, d), lambda i: (0, 0))],
        [pl.BlockSpec((tm, d), row), pl.BlockSpec((tm, 1), row)],
        [_sds((s, d), BF16), _sds((s, 1), F32)], [x, g], sem=("arbitrary",))[0]


def rmsnorm_bwd(name, dn, x, r, g, dh_in, jobs=()):
    s, d = x.shape
    tm = _tile(s, 512, 8)

    def body(dn_ref, x_ref, r_ref, g_ref, dh_ref, out_ref, outb_ref, dg_ref):
        i = pl.program_id(0)
        xh = x_ref[...] * r_ref[...]
        dnv = dn_ref[...]
        dxh = dnv * g_ref[...]
        dx = r_ref[...] * (dxh - xh * jnp.mean(dxh * xh, axis=-1, keepdims=True))
        out = dh_ref[...] + dx
        out_ref[...] = out
        outb_ref[...] = out.astype(BF16)
        part = jnp.sum(dnv * xh, axis=0, keepdims=True)

        @pl.when(i == 0)
        def _():
            dg_ref[...] = part

        @pl.when(i > 0)
        def _():
            dg_ref[...] += part

    row = lambda i: (i, 0)
    fixed = lambda i: (0, 0)
    outs, job_outs = _call(
        name, body, (s // tm,),
        [pl.BlockSpec((tm, d), row), pl.BlockSpec((tm, d), row), pl.BlockSpec((tm, 1), row),
         pl.BlockSpec((1, d), fixed), pl.BlockSpec((tm, d), row)],
        [pl.BlockSpec((tm, d), row), pl.BlockSpec((tm, d), row), pl.BlockSpec((1, d), fixed)],
        [_sds((s, d), F32), _sds((s, d), BF16), _sds((1, d), F32)], [dn, x, r, g, dh_in],
        sem=("arbitrary",), jobs=jobs)
    return _ret(outs, job_outs, jobs, single=False)


def gain_grad(name, dn_a, dn_b, x, r):
    s, d = x.shape
    tm = _tile(s, 512, 8)

    def body(a_ref, b_ref, x_ref, r_ref, dg_ref):
        i = pl.program_id(0)
        part = jnp.sum((a_ref[...] + b_ref[...]) * (x_ref[...] * r_ref[...]), axis=0, keepdims=True)

        @pl.when(i == 0)
        def _():
            dg_ref[...] = part

        @pl.when(i > 0)
        def _():
            dg_ref[...] += part

    row = lambda i: (i, 0)
    return _call(
        name, body, (s // tm,),
        [pl.BlockSpec((tm, d), row), pl.BlockSpec((tm, d), row), pl.BlockSpec((tm, d), row),
         pl.BlockSpec((tm, 1), row)],
        [pl.BlockSpec((1, d), lambda i: (0, 0))], [_sds((1, d), F32)], [dn_a, dn_b, x, r],
        sem=("arbitrary",))[0][0]


def loss_head(name, h, g, target):
    s, d = h.shape
    tm = _tile(s, 512, 8)
    nsteps = s // tm

    def body(h_ref, g_ref, t_ref, loss_ref, dh_ref, dhb_ref, dg_ref, sq_ref):
        i = pl.program_id(0)
        hv = h_ref[...]
        gv = g_ref[...]
        r = lax.rsqrt(jnp.mean(hv * hv, axis=-1, keepdims=True) + EPS)
        xh = hv * r
        err = xh * gv - t_ref[...]
        dy = err * (1.0 / d)
        dxh = dy * gv
        dh = r * (dxh - xh * jnp.mean(dxh * xh, axis=-1, keepdims=True))
        dh_ref[...] = dh
        dhb_ref[...] = dh.astype(BF16)
        dg_part = jnp.sum(dy * xh, axis=0, keepdims=True)
        sq_part = jnp.sum(err * err, axis=0, keepdims=True)

        @pl.when(i == 0)
        def _():
            dg_ref[...] = dg_part
            sq_ref[...] = sq_part

        @pl.when(i > 0)
        def _():
            dg_ref[...] += dg_part
            sq_ref[...] += sq_part

        @pl.when(i == nsteps - 1)
        def _():
            total = jnp.sum(sq_ref[...], axis=-1, keepdims=True) * (0.5 / d)
            loss_ref[...] = jnp.broadcast_to(total, loss_ref.shape)

    row = lambda i: (i, 0)
    fixed = lambda i: (0, 0)
    return _call(
        name, body, (nsteps,),
        [pl.BlockSpec((tm, d), row), pl.BlockSpec((1, d), fixed), pl.BlockSpec((tm, d), row)],
        [pl.BlockSpec((8, 128), fixed), pl.BlockSpec((tm, d), row), pl.BlockSpec((tm, d), row),
         pl.BlockSpec((1, d), fixed)],
        [_sds((8, 128), F32), _sds((s, d), F32), _sds((s, d), BF16), _sds((1, d), F32)], [h, g, target],
        scratch=[pltpu.VMEM((1, d), F32)], sem=("arbitrary",))[0]


def _mm(name, grid, in_arrays, in_specs, out_shapes, out_specs, acc_tile, dot, epilogue, jobs=(), place=None):
    nk = grid[2]
    n_in = len(in_arrays)
    n_out = len(out_shapes)

    def body(*refs):
        ins, outs = refs[:n_in], refs[n_in:n_in + n_out]
        if nk == 1:
            epilogue(dot(*ins), ins, outs)
            return
        acc = refs[n_in + n_out]
        k = pl.program_id(2)

        @pl.when(k == 0)
        def _():
            acc[...] = dot(*ins)

        @pl.when(jnp.logical_and(k > 0, k < nk - 1))
        def _():
            acc[...] += dot(*ins)

        @pl.when(k == nk - 1)
        def _():
            epilogue(acc[...] + dot(*ins), ins, outs)

    scratch = [pltpu.VMEM(acc_tile, F32)] if nk > 1 else []
    outs, job_outs = _call(name, body, grid, in_specs, out_specs, out_shapes, in_arrays, scratch=scratch,
                           sem=("parallel", "parallel", "arbitrary"), jobs=jobs, place=place)
    return _ret(outs, job_outs, jobs)


def _store(scale, dtype):
    def epilogue(acc, ins, outs):
        outs[0][...] = (acc * scale if scale != 1.0 else acc).astype(dtype)
    return epilogue


def mm_nn(name, a, w, out_dtype, tm=1024, tn=1024, tk=2048, jobs=()):
    m, kd = a.shape
    n = w.shape[1]
    tm, tn, tk = _tile(m, tm, 8), _tile(n, tn), _tile(kd, tk)
    return _mm(
        name, (n // tn, m // tm, kd // tk), [a, w],
        [pl.BlockSpec((tm, tk), lambda j, i, k: (i, k)), pl.BlockSpec((tk, tn), lambda j, i, k: (k, j))],
        [_sds((m, n), out_dtype)], [pl.BlockSpec((tm, tn), lambda j, i, k: (i, j))], (tm, tn),
        lambda a_ref, w_ref: _dot_nn(a_ref[...], w_ref[...]), _store(1.0, out_dtype), jobs)


def mm_nn_resid(name, a, w, x, scale, tm=1024, tn=1024, tk=1408, jobs=()):
    m, kd = a.shape
    n = w.shape[1]
    tm, tn, tk = _tile(m, tm, 8), _tile(n, tn), _tile(kd, tk)

    def epilogue(acc, ins, outs):
        outs[0][...] = ins[2][...] + scale * acc

    return _mm(
        name, (n // tn, m // tm, kd // tk), [a, w, x],
        [pl.BlockSpec((tm, tk), lambda j, i, k: (i, k)), pl.BlockSpec((tk, tn), lambda j, i, k: (k, j)),
         pl.BlockSpec((tm, tn), lambda j, i, k: (i, j))],
        [_sds((m, n), F32)], [pl.BlockSpec((tm, tn), lambda j, i, k: (i, j))], (tm, tn),
        lambda a_ref, w_ref, x_ref: _dot_nn(a_ref[...], w_ref[...]), epilogue, jobs)


def mm_nt(name, a, w, out_dtype, scale=1.0, tm=1024, tn=1024, tk=2048, jobs=()):
    m, kd = a.shape
    n = w.shape[0]
    tm, tn, tk = _tile(m, tm, 8), _tile(n, tn), _tile(kd, tk)
    return _mm(
        name, (n // tn, m // tm, kd // tk), [a, w],
        [pl.BlockSpec((tm, tk), lambda j, i, k: (i, k)), pl.BlockSpec((tn, tk), lambda j, i, k: (j, k))],
        [_sds((m, n), out_dtype)], [pl.BlockSpec((tm, tn), lambda j, i, k: (i, j))], (tm, tn),
        lambda a_ref, w_ref: _dot_nt(a_ref[...], w_ref[...]), _store(scale, out_dtype), jobs)


def mm_nt_pair(name, a3, w, out_dtype, tm=1024, tn=1024, tk=2816, jobs=()):
    _, m, f = a3.shape
    n = w.shape[0]
    tm, tn, tk = _tile(m, tm, 8), _tile(n, tn), _tile(f, tk)
    nkf = f // tk
    return _mm(
        name, (n // tn, m // tm, 2 * nkf), [a3, w],
        [pl.BlockSpec((None, tm, tk), lambda j, i, k: (k // nkf, i, k % nkf)),
         pl.BlockSpec((tn, tk), lambda j, i, k: (j, k))],
        [_sds((m, n), out_dtype)], [pl.BlockSpec((tm, tn), lambda j, i, k: (i, j))], (tm, tn),
        lambda a_ref, w_ref: _dot_nt(a_ref[...], w_ref[...]), _store(1.0, out_dtype), jobs)


def mm_tn(name, a, b, out_dtype, scale=1.0, tm=1024, tn=1024, tk=4096, jobs=()):
    kd, m = a.shape
    n = b.shape[1]
    tm, tn, tk = _tile(m, tm), _tile(n, tn), _tile(kd, tk, 16)
    return _mm(
        name, (n // tn, m // tm, kd // tk), [a, b],
        [pl.BlockSpec((tk, tm), lambda j, i, k: (k, i)), pl.BlockSpec((tk, tn), lambda j, i, k: (k, j))],
        [_sds((m, n), out_dtype)], [pl.BlockSpec((tm, tn), lambda j, i, k: (i, j))], (tm, tn),
        lambda a_ref, b_ref: _dot_tn(a_ref[...], b_ref[...]), _store(scale, out_dtype), jobs)


def mm_tn_pair(name, a, b3, out_dtype, tm=1024, tn=512, tk=4096, jobs=()):
    kd, m = a.shape
    f = b3.shape[2]
    tm, tn, tk = _tile(m, tm), _tile(f, tn), _tile(kd, tk, 16)
    nf = f // tn
    return _mm(
        name, (m // tm, 2 * nf, kd // tk), [a, b3],
        [pl.BlockSpec((tk, tm), lambda i, j, k: (k, i)),
         pl.BlockSpec((None, tk, tn), lambda i, j, k: (j // nf, k, j % nf))],
        [_sds((m, 2 * f), out_dtype)], [pl.BlockSpec((tm, tn), lambda i, j, k: (i, j))], (tm, tn),
        lambda a_ref, b_ref: _dot_tn(a_ref[...], b_ref[...]), _store(1.0, out_dtype), jobs)


def mm_tn_pair_half(name, a, b3, out_dtype, place, mine, tm=1024, tn=512, tk=4096, jobs=()):
    kd, m = a.shape
    f = b3.shape[2]
    tm, tn, tk = _tile(m // 2, tm), _tile(f, tn), _tile(kd, tk, 16)
    nf, nbm = f // tn, m // 2 // tm
    which = (lambda p: p[1]) if mine else (lambda p: 1 - p[1])
    return _mm(
        name, (nbm, 2 * nf, kd // tk), [a, b3],
        [pl.BlockSpec((tk, tm), lambda i, j, k, p: (k, i + which(p) * nbm)),
         pl.BlockSpec((None, tk, tn), lambda i, j, k, p: (j // nf, k, j % nf))],
        [_sds((m // 2, 2 * f), out_dtype)], [pl.BlockSpec((tm, tn), lambda i, j, k, p: (i, j))], (tm, tn),
        lambda a_ref, b_ref: _dot_tn(a_ref[...], b_ref[...]), _store(1.0, out_dtype), jobs, place)


def swiglu_fwd(name, n, w_in, tm=1024, tn=512, jobs=(), stride=1, phase=0, prev=None):
    s, d = n.shape
    f = w_in.shape[1] // 2
    tm, tn = _tile(s, tm, 8), _tile(f, tn)
    nf = f // tn
    col = lambda j: j * stride + phase

    def body(n_ref, wg_ref, wu_ref, *rest):
        gu_ref, a_ref = rest[-2:]
        nv = n_ref[...]
        g = _dot_nn(nv, wg_ref[...])
        u = _dot_nn(nv, wu_ref[...])
        gu_ref[0] = g.astype(BF16)
        gu_ref[1] = u.astype(BF16)
        a_ref[...] = (g * jax.nn.sigmoid(g) * u).astype(BF16)

    kept = list(prev) if prev is not None else []
    outs, job_outs = _call(
        name, body, (nf // stride, s // tm),
        [pl.BlockSpec((tm, d), lambda j, i: (i, 0)), pl.BlockSpec((d, tn), lambda j, i: (0, col(j))),
         pl.BlockSpec((d, tn), lambda j, i: (0, col(j) + nf))] + [ANY] * len(kept),
        [pl.BlockSpec((2, tm, tn), lambda j, i: (0, i, col(j))), pl.BlockSpec((tm, tn), lambda j, i: (i, col(j)))],
        [_sds((2, s, f), BF16), _sds((s, f), BF16)], [n, w_in, w_in] + kept, sem=("parallel", "parallel"),
        jobs=jobs, carried={3 + k: k for k in range(len(kept))})
    return _ret(outs, job_outs, jobs, single=False)


def swiglu_bwd(name, dh, w_out, gu, scale, tm=1024, tn=512, jobs=()):
    s, d = dh.shape
    f = w_out.shape[0]
    tm, tn = _tile(s, tm, 8), _tile(f, tn)

    sub = _tile(tm, 256, 8)

    def body(dh_ref, w_ref, gu_ref, out_ref):
        for lo in range(0, tm, sub):
            rows = slice(lo, lo + sub)
            da = (_dot_nt(dh_ref[rows, :], w_ref[...]) * scale).astype(BF16)
            g = gu_ref[0, rows, :]
            u = gu_ref[1, rows, :]
            sg = 0.5 * jnp.tanh(0.5 * g) + 0.5
            t = g * sg
            out_ref[0, rows, :] = da * (u * (sg + t * (1.0 - sg)))
            out_ref[1, rows, :] = da * t

    outs, job_outs = _call(
        name, body, (f // tn, s // tm),
        [pl.BlockSpec((tm, d), lambda j, i: (i, 0)), pl.BlockSpec((tn, d), lambda j, i: (j, 0)),
         pl.BlockSpec((2, tm, tn), lambda j, i: (0, i, j))],
        [pl.BlockSpec((2, tm, tn), lambda j, i: (0, i, j))],
        [_sds((2, s, f), BF16)], [dh, w_out, gu], sem=("parallel", "parallel"), jobs=jobs)
    return _ret(outs, job_outs, jobs)


HALO = 16


def _conv_inputs(z_ref, hgc_ref, hhc_ref, i, cw, tm):
    gc = z_ref[:, cw:2 * cw].astype(F32)
    hc = z_ref[:, 2 * cw:3 * cw].astype(F32)
    cin = gc * hc
    halo = hgc_ref[...].astype(F32) * hhc_ref[...].astype(F32) * (i > 0).astype(F32)
    row = lax.broadcasted_iota(jnp.int32, (tm, cw), 0)
    x1 = jnp.where(row == 0, halo[HALO - 1:HALO], pltpu.roll(cin, 1, 0))
    x2 = jnp.where(row == 0, halo[HALO - 2:HALO - 1], jnp.where(row == 1, halo[HALO - 1:HALO], pltpu.roll(cin, 2, 0)))
    return gc, hc, cin, x1, x2


def _tril(w):
    r = lax.broadcasted_iota(jnp.int32, w.shape, 0)
    c = lax.broadcasted_iota(jnp.int32, w.shape, 1)
    return jnp.where(r >= c, w, jnp.zeros_like(w))


def mixer_fwd(name, z, conv_w, conv_b, g_v, w_s, b_t, tm=256, jobs=()):
    s, zc = z.shape
    cw = conv_w.shape[1]
    gw = g_v.shape[1]
    heads = gw // GROUP
    tm = _tile(s, tm)
    hb = tm // HALO

    def body(z_ref, hgc_ref, hhc_ref, cw_ref, cb_ref, gv_ref, ws_ref, bt_ref, y_ref):
        i = pl.program_id(0)
        _, _, cin, x1, x2 = _conv_inputs(z_ref, hgc_ref, hhc_ref, i, cw, tm)
        cv = cb_ref[...] + cw_ref[2:3, :] * cin + cw_ref[1:2, :] * x1 + cw_ref[0:1, :] * x2
        y_ref[:, 0:cw] = (z_ref[:, 0:cw].astype(F32) * cv).astype(BF16)
        for h in range(heads):
            lo = h * GROUP
            vh = z_ref[:, 3 * cw + gw + lo:3 * cw + gw + lo + GROUP].astype(F32)
            rv = lax.rsqrt(jnp.mean(vh * vh, axis=-1, keepdims=True) + EPS)
            vn = (vh * rv * gv_ref[:, lo:lo + GROUP]).astype(BF16)
            w = _tril(ws_ref[h]).astype(BF16)
            for n in range(tm // GROUP):
                rows = slice(n * GROUP, (n + 1) * GROUP)
                sg = _dot_nn(w, vn[rows]) + bt_ref[:, h:h + 1]
                u = z_ref[rows, 3 * cw + lo:3 * cw + lo + GROUP].astype(F32)
                y_ref[rows, cw + lo:cw + lo + GROUP] = (u * sg).astype(BF16)

    fixed2 = lambda i: (0, 0)
    outs, job_outs = _call(
        name, body, (s // tm,),
        [pl.BlockSpec((tm, zc), lambda i: (i, 0)),
         pl.BlockSpec((HALO, cw), lambda i: (jnp.maximum(i * hb - 1, 0), 1)),
         pl.BlockSpec((HALO, cw), lambda i: (jnp.maximum(i * hb - 1, 0), 2)),
         pl.BlockSpec(conv_w.shape, fixed2), pl.BlockSpec(conv_b.shape, fixed2),
         pl.BlockSpec(g_v.shape, fixed2), pl.BlockSpec(w_s.shape, lambda i: (0, 0, 0)),
         pl.BlockSpec(b_t.shape, fixed2)],
        [pl.BlockSpec((tm, cw + gw), lambda i: (i, 0))], [_sds((s, cw + gw), BF16)],
        [z, z, z, conv_w, conv_b, g_v, w_s, b_t], sem=("arbitrary",), jobs=jobs)
    return _ret(outs, job_outs, jobs)


def mixer_bwd(name, z, dy, conv_w, conv_b, g_v, w_s, b_t, tm=256, jobs=()):
    s, zc = z.shape
    cw = conv_w.shape[1]
    gw = g_v.shape[1]
    heads = gw // GROUP
    tm = _tile(s, tm)
    hb = tm // HALO
    nsteps = s // tm
    last_halo = s // HALO - 1

    def body(z_ref, hgc_ref, hhc_ref, ngb_ref, dy_ref, ndy_ref, cw_ref, cb_ref, gv_ref, ws_ref, bt_ref,
             dz_ref, sm_ref, dws_ref, dbt_ref, dsg_ref):
        i = pl.program_id(0)

        @pl.when(i == 0)
        def _():
            sm_ref[...] = jnp.zeros_like(sm_ref)
            dws_ref[...] = jnp.zeros_like(dws_ref)
            dsg_ref[...] = jnp.zeros_like(dsg_ref)

        gc, hc, cin, x1, x2 = _conv_inputs(z_ref, hgc_ref, hhc_ref, i, cw, tm)
        w0, w1, w2 = cw_ref[0:1, :], cw_ref[1:2, :], cw_ref[2:3, :]
        cv = cb_ref[...] + w2 * cin + w1 * x1 + w0 * x2
        gb = z_ref[:, 0:cw].astype(F32)
        dyc = dy_ref[:, 0:cw].astype(F32)
        dz_ref[:, 0:cw] = (dyc * cv).astype(BF16)
        dcv = dyc * gb
        nxt = ndy_ref[...].astype(F32) * ngb_ref[...].astype(F32) * (i < nsteps - 1).astype(F32)
        row = lax.broadcasted_iota(jnp.int32, (tm, cw), 0)
        d1 = jnp.where(row == tm - 1, nxt[0:1], pltpu.roll(dcv, tm - 1, 0))
        d2 = jnp.where(row == tm - 1, nxt[1:2], jnp.where(row == tm - 2, nxt[0:1], pltpu.roll(dcv, tm - 2, 0)))
        dcin = w2 * dcv + w1 * d1 + w0 * d2
        dz_ref[:, cw:2 * cw] = (dcin * hc).astype(BF16)
        dz_ref[:, 2 * cw:3 * cw] = (dcin * gc).astype(BF16)
        sm_ref[0:1, :] += jnp.sum(dcv * x2, axis=0, keepdims=True)
        sm_ref[1:2, :] += jnp.sum(dcv * x1, axis=0, keepdims=True)
        sm_ref[2:3, :] += jnp.sum(dcv * cin, axis=0, keepdims=True)
        sm_ref[3:4, :] += jnp.sum(dcv, axis=0, keepdims=True)

        for h in range(heads):
            lo = h * GROUP
            vcol = slice(3 * cw + gw + lo, 3 * cw + gw + lo + GROUP)
            ucol = slice(3 * cw + lo, 3 * cw + lo + GROUP)
            vh = z_ref[:, vcol].astype(F32)
            rv = lax.rsqrt(jnp.mean(vh * vh, axis=-1, keepdims=True) + EPS)
            xh = vh * rv
            gvh = gv_ref[:, lo:lo + GROUP]
            vn = (xh * gvh).astype(BF16)
            w = _tril(ws_ref[h]).astype(BF16)
            dgv = jnp.zeros((1, GROUP), F32)
            for n in range(tm // GROUP):
                rows = slice(n * GROUP, (n + 1) * GROUP)
                sg = _dot_nn(w, vn[rows]) + bt_ref[:, h:h + 1]
                dyg = dy_ref[rows, cw + lo:cw + lo + GROUP].astype(F32)
                dsg = dyg * z_ref[rows, ucol].astype(F32)
                dz_ref[rows, ucol] = (dyg * sg).astype(BF16)
                dsgb = dsg.astype(BF16)
                dvn = _dot_tn(w, dsgb)
                dws_ref[h] += _dot_nt(dsgb, vn[rows])
                dsg_ref[:, lo:lo + GROUP] += dsg
                xhc = xh[rows]
                dgv = dgv + jnp.sum(dvn * xhc, axis=0, keepdims=True)
                dxh = dvn * gvh
                dv = rv[rows] * (dxh - xhc * jnp.mean(dxh * xhc, axis=-1, keepdims=True))
                dz_ref[rows, vcol] = dv.astype(BF16)
            sm_ref[4:5, lo:lo + GROUP] += dgv

        @pl.when(i == nsteps - 1)
        def _():
            for h in range(heads):
                dws_ref[h] = _tril(dws_ref[h])
                dbt_ref[:, h:h + 1] = jnp.sum(dsg_ref[:, h * GROUP:(h + 1) * GROUP], axis=-1, keepdims=True)

    fixed2 = lambda i: (0, 0)
    fixed3 = lambda i: (0, 0, 0)
    prev = lambda col: (lambda i: (jnp.maximum(i * hb - 1, 0), col))
    nxt_blk = lambda i: (jnp.minimum((i + 1) * hb, last_halo), 0)
    outs, job_outs = _call(
        name, body, (nsteps,),
        [pl.BlockSpec((tm, zc), lambda i: (i, 0)),
         pl.BlockSpec((HALO, cw), prev(1)), pl.BlockSpec((HALO, cw), prev(2)),
         pl.BlockSpec((HALO, cw), nxt_blk),
         pl.BlockSpec((tm, cw + gw), lambda i: (i, 0)), pl.BlockSpec((HALO, cw), nxt_blk),
         pl.BlockSpec(conv_w.shape, fixed2), pl.BlockSpec(conv_b.shape, fixed2),
         pl.BlockSpec(g_v.shape, fixed2), pl.BlockSpec(w_s.shape, fixed3), pl.BlockSpec(b_t.shape, fixed2)],
        [pl.BlockSpec((tm, zc), lambda i: (i, 0)), pl.BlockSpec((8, cw), fixed2),
         pl.BlockSpec(w_s.shape, fixed3), pl.BlockSpec(b_t.shape, fixed2)],
        [_sds((s, zc), BF16), _sds((8, cw), F32), _sds(w_s.shape, F32), _sds(b_t.shape, F32)],
        [z, z, z, z, dy, dy, conv_w, conv_b, g_v, w_s, b_t],
        scratch=[pltpu.VMEM((GROUP, gw), F32)], sem=("arbitrary",), jobs=jobs)
    return _ret(outs, job_outs, jobs, single=False)


def _softmax_rows(sc):
    e = jnp.exp(sc - jnp.max(sc, axis=-1, keepdims=True))
    return e / jnp.sum(e, axis=-1, keepdims=True)


def attn_fwd(name, q, k, v, tm=512, jobs=()):
    s, d = q.shape
    m = k.shape[0]
    hd = d // XA_HEADS
    scale = hd ** -0.5
    tm = _tile(s, tm, 8)

    def body(q_ref, k_ref, v_ref, o_ref):
        for h in range(XA_HEADS):
            cols = slice(h * hd, (h + 1) * hd)
            p = _softmax_rows(_dot_nt(q_ref[:, cols], k_ref[:, cols]) * scale)
            o_ref[:, cols] = _dot_nn(p.astype(BF16), v_ref[:, cols]).astype(BF16)

    outs, job_outs = _call(
        name, body, (s // tm,),
        [pl.BlockSpec((tm, d), lambda i: (i, 0)), pl.BlockSpec((m, d), lambda i: (0, 0)),
         pl.BlockSpec((m, d), lambda i: (0, 0))],
        [pl.BlockSpec((tm, d), lambda i: (i, 0))], [_sds((s, d), BF16)], [q, k, v], sem=("arbitrary",), jobs=jobs)
    return _ret(outs, job_outs, jobs)


def attn_bwd(name, q, k, v, do, tm=512):
    s, d = q.shape
    m = k.shape[0]
    hd = d // XA_HEADS
    scale = hd ** -0.5
    tm = _tile(s, tm, 8)

    def body(q_ref, k_ref, v_ref, do_ref, dq_ref, dk_ref, dv_ref):
        i = pl.program_id(0)

        @pl.when(i == 0)
        def _():
            dk_ref[...] = jnp.zeros_like(dk_ref)
            dv_ref[...] = jnp.zeros_like(dv_ref)

        for h in range(XA_HEADS):
            cols = slice(h * hd, (h + 1) * hd)
            qh = q_ref[:, cols]
            doh = do_ref[:, cols]
            p = _softmax_rows(_dot_nt(qh, k_ref[:, cols]) * scale)
            dp = _dot_nt(doh, v_ref[:, cols])
            ds = (p * (dp - jnp.sum(dp * p, axis=-1, keepdims=True)) * scale).astype(BF16)
            dq_ref[:, cols] = _dot_nn(ds, k_ref[:, cols]).astype(BF16)
            dk_ref[:, cols] += _dot_tn(ds, qh)
            dv_ref[:, cols] += _dot_tn(p.astype(BF16), doh)

    row = lambda i: (i, 0)
    fixed = lambda i: (0, 0)
    return _call(
        name, body, (s // tm,),
        [pl.BlockSpec((tm, d), row), pl.BlockSpec((m, d), fixed), pl.BlockSpec((m, d), fixed),
         pl.BlockSpec((tm, d), row)],
        [pl.BlockSpec((tm, d), row), pl.BlockSpec((m, d), fixed), pl.BlockSpec((m, d), fixed)],
        [_sds((s, d), BF16), _sds((m, d), F32), _sds((m, d), F32)], [q, k, v, do], sem=("arbitrary",))[0]


def _grid2(rows, cols, row_mult):
    tr, tc = _tile(rows, 512, row_mult), _tile(cols, 2048)
    return tr, tc, rows // tr, cols // tc


def cast_place(name, block, axis, place):
    r, c = block.shape
    tr, tc, nbr, nbc = _grid2(r, c, 16)
    if axis == 1:
        dst = lambda i, j, p: (i, j + p[0] * nbc)
    else:
        dst = lambda i, j, p: (i + p[0] * nbr, j)

    def body(p_ref, w_ref, out_ref):
        out_ref[...] = w_ref[...].astype(BF16)

    return pl.pallas_call(
        body, name=name,
        grid_spec=pltpu.PrefetchScalarGridSpec(
            num_scalar_prefetch=1, grid=(nbr, nbc),
            in_specs=[pl.BlockSpec((tr, tc), lambda i, j, p: (i, j))],
            out_specs=pl.BlockSpec((tr, tc), dst)),
        out_shape=_sds(_full_shape(block.shape, axis), BF16),
        compiler_params=_params(("parallel", "parallel")),
    )(place, block)


def pair_add(name, grad, peer, axis, place):
    hr, hc = peer.shape
    tr, tc, nbr, nbc = _grid2(hr, hc, 16)
    same = lambda i, j, p: (i, j)
    if grad.shape == peer.shape:
        mine = same
    elif axis == 1:
        mine = lambda i, j, p: (i + p[1] * nbr, j)
    else:
        mine = lambda i, j, p: (i, j + p[1] * nbc)

    def body(p_ref, g_ref, q_ref, out_ref):
        out_ref[...] = (g_ref[...].astype(F32) + q_ref[...].astype(F32)).astype(BF16)

    return pl.pallas_call(
        body, name=name,
        grid_spec=pltpu.PrefetchScalarGridSpec(
            num_scalar_prefetch=1, grid=(nbr, nbc),
            in_specs=[pl.BlockSpec((tr, tc), mine), pl.BlockSpec((tr, tc), same)],
            out_specs=pl.BlockSpec((tr, tc), same)),
        out_shape=_sds((hr, hc), BF16),
        compiler_params=_params(("parallel", "parallel")),
    )(place, grad, peer)


def cross_sum(name, part, land, axis, shape, place):
    _, sr, sc = land.shape
    tr, tc, nbr, nbc = _grid2(sr, sc, 16)
    if axis == 1:
        own = lambda i, j, p: (i, j + p[0] * nbc)
        dst = lambda i, j, p: (i + p[1] * nbr, j)
    else:
        own = lambda i, j, p: (i + p[0] * nbr, j)
        dst = lambda i, j, p: (i, j + p[1] * nbc)

    def body(p_ref, own_ref, land_ref, out_ref):
        out_ref[...] = ((own_ref[...].astype(F32) + land_ref[0].astype(F32))
                        + (land_ref[1].astype(F32) + land_ref[2].astype(F32)))

    return pl.pallas_call(
        body, name=name,
        grid_spec=pltpu.PrefetchScalarGridSpec(
            num_scalar_prefetch=1, grid=(nbr, nbc),
            in_specs=[pl.BlockSpec((tr, tc), own), pl.BlockSpec((3, tr, tc), lambda i, j, p: (0, i, j))],
            out_specs=pl.BlockSpec((tr, tc), dst)),
        out_shape=_sds(_block(shape, axis), F32),
        compiler_params=_params(("parallel", "parallel")),
    )(place, part, land)


def _adam_math(w, g, m, v):
    m = ADAM_B1 * m + (1.0 - ADAM_B1) * g
    v = ADAM_B2 * v + (1.0 - ADAM_B2) * (g * g)
    m_hat = m / (1.0 - ADAM_B1 ** ADAM_STEP)
    v_hat = v / (1.0 - ADAM_B2 ** ADAM_STEP)
    delta = -ADAM_LR * (m_hat / (jnp.sqrt(v_hat) + ADAM_EPS) + ADAM_WD * w)
    return delta, m, v


def adamw(name, w, g, m, v, jobs=()):
    r, c = w.shape
    tr, tc = _tile(r, 256, 8), _tile(c, 1408)

    def body(w_ref, g_ref, m_ref, v_ref, g_out, d_out, m_out, v_out):
        d, mm, vv = _adam_math(w_ref[...], g_ref[...], m_ref[...], v_ref[...])
        g_out[...] = g_ref[...]
        d_out[...] = d
        m_out[...] = mm
        v_out[...] = vv

    spec = pl.BlockSpec((tr, tc), lambda i, j: (i, j))
    outs, job_outs = _call(name, body, (r // tr, c // tc), [spec] * 4, [spec] * 4, [_sds((r, c), F32)] * 4,
                           [w, g, m, v], sem=("parallel", "parallel"), jobs=jobs)
    return _ret(outs, job_outs, jobs, single=False)


def small_sum(name, stacks):
    def body(*refs):
        for s_ref, out_ref in zip(refs[:len(stacks)], refs[len(stacks):]):
            acc = s_ref[0]
            for d in range(1, s_ref.shape[0]):
                acc = acc + s_ref[d]
            out_ref[...] = acc

    return pl.pallas_call(body, name=name, out_shape=[_sds(s.shape[1:], F32) for s in stacks])(*stacks)


WEIGHTS = ["g_ffn1", "w_ffn1_in", "w_ffn1_out", "g_mix", "w_mix_in", "conv_w", "conv_b", "g_gm_v", "w_spatial",
           "b_spatial", "w_mix_out", "g_xattn", "g_mem", "w_xq", "w_xk", "w_xv", "w_xo", "g_ffn2", "w_ffn2_in",
           "w_ffn2_out", "g_final"]
BIG = {"w_ffn1_in": 1, "w_ffn1_out": 0, "w_mix_in": 1, "w_mix_out": 0, "w_xq": 0, "w_xk": 0, "w_xv": 0, "w_xo": 0,
       "w_ffn2_in": 1, "w_ffn2_out": 0}
SMALL = [n for n in WEIGHTS if n not in BIG]
LATE_SMALL = ["g_ffn1"]
EARLY_SMALL = [n for n in SMALL if n not in LATE_SMALL]


def _pack(arrays):
    flat = jnp.concatenate([a.reshape(-1) for a in arrays])
    rows = -(-flat.shape[0] // 1024) * 8
    return jnp.pad(flat, (0, rows * 128 - flat.shape[0])).reshape(rows, 128)


def _unpack(buf, shapes):
    flat = buf.reshape(-1)
    out, pos = [], 0
    for shp in shapes:
        n = math.prod(shp)
        out.append(flat[pos:pos + n].reshape(shp))
        pos += n
    return out


def kernel(x, mem, g_ffn1, w_ffn1_in, w_ffn1_out, g_mix, w_mix_in, conv_w, conv_b, g_gm_v, w_spatial, b_spatial, w_mix_out, g_xattn, g_mem, w_xq, w_xk, w_xv, w_xo, g_ffn2, w_ffn2_in, w_ffn2_out, g_final, loss_target, m_g_ffn1, m_w_ffn1_in, m_w_ffn1_out, m_g_mix, m_w_mix_in, m_conv_w, m_conv_b, m_g_gm_v, m_w_spatial, m_b_spatial, m_w_mix_out, m_g_xattn, m_g_mem, m_w_xq, m_w_xk, m_w_xv, m_w_xo, m_g_ffn2, m_w_ffn2_in, m_w_ffn2_out, m_g_final, v_g_ffn1, v_w_ffn1_in, v_w_ffn1_out, v_g_mix, v_w_mix_in, v_conv_w, v_conv_b, v_g_gm_v, v_w_spatial, v_b_spatial, v_w_mix_out, v_g_xattn, v_g_mem, v_w_xq, v_w_xk, v_w_xv, v_w_xo, v_g_ffn2, v_w_ffn2_in, v_w_ffn2_out, v_g_final):
    given = dict(locals())
    wts = {n: given[n] for n in WEIGHTS}
    mom = {n: given["m_" + n] for n in WEIGHTS}
    var = {n: given["v_" + n] for n in WEIGHTS}

    xi, yi, ci = lax.axis_index("x"), lax.axis_index("y"), lax.axis_index("c")
    blk = 2 * xi + yi
    place = jnp.stack([blk, ci]).astype(jnp.int32)

    x2, mem2, tgt = x[0], mem[0], loss_target[0]
    own = {n: cast_place("cast_" + n, wts[n][0], BIG[n], place) for n in BIG}
    shape = {n: own[n].shape for n in BIG}
    w_s, b_t = w_spatial[0], b_spatial[0].T
    gf = g_final[None]

    def gather(*names):
        return gather_job([(own[n], BIG[n], WHOLE, WHOLE) for n in names])

    def gather_part(arr, sub=WHOLE, within=WHOLE):
        return gather_job([(arr, 1, sub, within)])

    full = {}
    left, right = (0, 1, 2), (1, 1, 2)
    half_cols = dict(tm=512, tn=shape["w_ffn1_in"][1] // (2 * N_CHIPS), stride=2)

    (w1in,), (conv_taps,) = comm_only(
        "gather_first", [gather_part(own["w_ffn1_in"], within=left),
                         columns_job(jnp.pad(conv_w[0], ((0, 8 - CONV_K), (0, 0))))])
    n1, r1 = rmsnorm_fwd("norm1", x2, g_ffn1)
    halves, ((w1in,),) = swiglu_fwd("ffn1_in_left", n1, w1in, phase=0, jobs=[gather_part(w1in, within=right)],
                                    **half_cols)
    full["w_ffn1_in"] = w1in
    (gu1, a1), ((full["w_ffn1_out"],),) = swiglu_fwd("ffn1_in_right", n1, w1in, phase=1, prev=halves,
                                                     jobs=[gather("w_ffn1_out")], **half_cols)
    h1, ((full["w_mix_in"], full["w_mix_out"]),) = mm_nn_resid(
        "ffn1_out", a1, full["w_ffn1_out"], x2, 0.5, tm=512, tk=5632, jobs=[gather("w_mix_in", "w_mix_out")])
    n2, r2 = rmsnorm_fwd("norm2", h1, g_mix)
    z, ((full["w_xq"], full["w_xk"], full["w_xv"]),) = mm_nn("mix_in", n2, full["w_mix_in"], BF16,
                                                             jobs=[gather("w_xq", "w_xk", "w_xv")])
    w2in = own["w_ffn2_in"]
    ycat, ((full["w_xo"],),) = mixer_fwd("mixer", z, conv_taps, conv_b, g_gm_v, w_s, b_t, jobs=[gather("w_xo")])
    h2, ((w2in,),) = mm_nn_resid("mix_out", ycat, full["w_mix_out"], h1, 1.0, tk=2048,
                                 jobs=[gather_part(w2in, (0, 2, 8))])
    n3, r3 = rmsnorm_fwd("norm3", h2, g_xattn)
    mn, rm = rmsnorm_fwd("norm_mem", mem2, g_mem)
    q, ((w2in,),) = mm_nn("xq", n3, full["w_xq"], BF16, jobs=[gather_part(w2in, (2, 2, 8))])
    k = mm_nn("xk", mn, full["w_xk"], BF16)
    v = mm_nn("xv", mn, full["w_xv"], BF16)
    o, ((w2in,),) = attn_fwd("attn", q, k, v, jobs=[gather_part(w2in, (4, 1, 8))])
    h3, ((w2in,),) = mm_nn_resid("xo", o, full["w_xo"], h2, 1.0, tk=2048, jobs=[gather_part(w2in, (5, 3, 8))])
    full["w_ffn2_in"] = w2in
    n4, r4 = rmsnorm_fwd("norm4", h3, g_ffn2)
    (gu2, a2), ((full["w_ffn2_out"],),) = swiglu_fwd("ffn2_in", n4, full["w_ffn2_in"], jobs=[gather("w_ffn2_out")])
    h4 = mm_nn_resid("ffn2_out", a2, full["w_ffn2_out"], h3, 0.5, tm=512, tk=5632)
    loss_blk, dh4, dh4b, dg_final = loss_head("loss_head", h4, gf, tgt)

    dw, peer, part, land, half, grads = {}, {}, {}, {}, {}, {}

    def send_pair(*names):
        return pair_job([dw[n] for n in names], [BIG[n] for n in names])

    def take_pair(names, got):
        for n, p in zip(names, got):
            part[n] = pair_add("pair_add_" + n, dw[n], p, BIG[n], place)

    def send_cross(*names, sub=WHOLE):
        return cross_job([(part[n], BIG[n], shape[n], land.get(n), sub) for n in names])

    def take_cross(names, got, last=True):
        for n, l in zip(names, got):
            land[n] = l
            if last:
                half[n] = cross_sum("cross_sum_" + n, part[n], l, BIG[n], shape[n], place)

    def send_final(*names):
        return final_job([half[n] for n in names], [BIG[n] for n in names], [shape[n] for n in names])

    delta, new_m, new_v = {}, {}, {}

    reduced = {}

    def take_final(names, got):
        for n, g in zip(names, got):
            reduced[n] = g

    def update(n, jobs=()):
        res = adamw("adamw_" + n, wts[n][0], reduced[n], mom[n][0], var[n][0], jobs=jobs)
        (grads[n], delta[n], new_m[n], new_v[n]), job_outs = res if jobs else (res, [])
        return job_outs

    dgu2 = swiglu_bwd("ffn2_dact", dh4b, full["w_ffn2_out"], gu2, 0.5)
    dw["w_ffn2_in"] = mm_tn_pair("ffn2_dwin", n4, dgu2, BF16)
    dw["w_ffn2_out"], (got,) = mm_tn("ffn2_dwout", a2, dh4b, BF16, scale=0.5, jobs=[send_pair("w_ffn2_in")])
    take_pair(["w_ffn2_in"], got)
    dn4, (got_c, got_p) = mm_nt_pair("ffn2_dn", dgu2, full["w_ffn2_in"], F32,
                                     jobs=[send_cross("w_ffn2_in", sub=(0, 7, 8)), send_pair("w_ffn2_out")])
    take_cross(["w_ffn2_in"], got_c, last=False)
    take_pair(["w_ffn2_out"], got_p)
    (dh3, dh3b, dg_ffn2), (got_c,) = rmsnorm_bwd("norm4_bwd", dn4, h3, r4, g_ffn2, dh4,
                                                 jobs=[send_cross("w_ffn2_in", sub=(7, 1, 8))])
    take_cross(["w_ffn2_in"], got_c)

    dw["w_xo"], (got_c,) = mm_tn("xo_dw", o, dh3b, BF16, jobs=[send_cross("w_ffn2_out", sub=(0, 2, 8))])
    take_cross(["w_ffn2_out"], got_c, last=False)
    do, (got_c, got_f) = mm_nt("xo_dx", dh3b, full["w_xo"], BF16,
                               jobs=[send_cross("w_ffn2_out", sub=(2, 2, 8)), send_final("w_ffn2_in")])
    take_cross(["w_ffn2_out"], got_c, last=False)
    take_final(["w_ffn2_in"], got_f)
    update("w_ffn2_in")
    dq, dk, dv = attn_bwd("attn_bwd", q, k, v, do)
    dkb, dvb = dk.astype(BF16), dv.astype(BF16)
    dw["w_xq"], (got_c,) = mm_tn("xq_dw", n3, dq, BF16, jobs=[send_cross("w_ffn2_out", sub=(4, 2, 8))])
    take_cross(["w_ffn2_out"], got_c, last=False)
    dn3, (got_c,) = mm_nt("xq_dx", dq, full["w_xq"], F32, jobs=[send_cross("w_ffn2_out", sub=(6, 2, 8))])
    take_cross(["w_ffn2_out"], got_c)
    (dh2, dh2b, dg_xattn), (got_f,) = rmsnorm_bwd("norm3_bwd", dn3, h2, r3, g_xattn, dh3,
                                                  jobs=[send_final("w_ffn2_out")])
    take_final(["w_ffn2_out"], got_f)
    update("w_ffn2_out")
    dw["w_xk"] = mm_tn("xk_dw", mn, dkb, BF16)
    dw["w_xv"] = mm_tn("xv_dw", mn, dvb, BF16)
    dmn_k = mm_nt("xk_dx", dkb, full["w_xk"], F32)
    dmn_v = mm_nt("xv_dx", dvb, full["w_xv"], F32)
    dg_mem = gain_grad("norm_mem_bwd", dmn_k, dmn_v, mem2, rm)

    dw["w_mix_out"] = mm_tn("mix_out_dw", ycat, dh2b, BF16)
    attn_names = ["w_xo", "w_xq", "w_xk", "w_xv", "w_mix_out"]
    dycat, (got_p,) = mm_nt("mix_out_dx", dh2b, full["w_mix_out"], BF16, jobs=[send_pair(*attn_names)])
    take_pair(attn_names, got_p)
    (dz, dsmall, dws, dbt), (got_c,) = mixer_bwd("mixer_bwd", z, dycat, conv_taps, conv_b, g_gm_v, w_s, b_t,
                                                 jobs=[send_cross("w_xo", "w_xq")])
    take_cross(["w_xo", "w_xq"], got_c)
    dw["w_mix_in"], (got_c,) = mm_tn("mix_in_dw", n2, dz, BF16, jobs=[send_cross("w_xk", "w_xv")])
    take_cross(["w_xk", "w_xv"], got_c)
    dn2, (got_c, got_p) = mm_nt("mix_in_dx", dz, full["w_mix_in"], F32, tk=2560,
                                jobs=[send_cross("w_mix_out"), send_pair("w_mix_in")])
    take_cross(["w_mix_out"], got_c)
    take_pair(["w_mix_in"], got_p)
    (dh1, dh1b, dg_mix), (got_f,) = rmsnorm_bwd("norm2_bwd", dn2, h1, r2, g_mix, dh2,
                                                jobs=[send_final(*attn_names)])
    take_final(attn_names, got_f)
    for n in attn_names:
        update(n)

    dw["w_ffn1_out"], (got_c,) = mm_tn("ffn1_dwout", a1, dh1b, BF16, scale=0.5, jobs=[send_cross("w_mix_in")])
    take_cross(["w_mix_in"], got_c)
    early = {"g_mix": dg_mix, "conv_w": dsmall[0:CONV_K], "conv_b": dsmall[3:4], "g_gm_v": dsmall[4:5],
             "w_spatial": dws, "b_spatial": dbt.T, "g_xattn": dg_xattn, "g_mem": dg_mem, "g_ffn2": dg_ffn2,
             "g_final": dg_final}
    dgu1, (got_p, got_f, (early_all,)) = swiglu_bwd(
        "ffn1_dact", dh1b, full["w_ffn1_out"], gu1, 0.5,
        jobs=[send_pair("w_ffn1_out"), send_final("w_mix_in"), stack_job(_pack([early[n] for n in EARLY_SMALL]))])
    take_pair(["w_ffn1_out"], got_p)
    take_final(["w_mix_in"], got_f)
    update("w_mix_in")
    theirs, (got_c,) = mm_tn_pair_half("ffn1_dwin_theirs", n1, dgu1, BF16, place, False,
                                       jobs=[send_cross("w_ffn1_out", sub=(0, 7, 8))])
    take_cross(["w_ffn1_out"], got_c, last=False)
    mine, (got_c, (from_sibling,)) = mm_tn_pair_half(
        "ffn1_dwin_mine", n1, dgu1, BF16, place, True,
        jobs=[send_cross("w_ffn1_out", sub=(7, 1, 8)), pair_job([theirs], [1], is_half=True)])
    take_cross(["w_ffn1_out"], got_c)
    part["w_ffn1_in"] = pair_add("pair_add_w_ffn1_in", mine, from_sibling, 1, place)
    dn1, (got_c, got_f) = mm_nt_pair("ffn1_dn", dgu1, full["w_ffn1_in"], F32,
                                     jobs=[send_cross("w_ffn1_in", sub=(0, 7, 8)), send_final("w_ffn1_out")])
    take_cross(["w_ffn1_in"], got_c, last=False)
    take_final(["w_ffn1_out"], got_f)
    update("w_ffn1_out")
    dx, _, dg_ffn1 = rmsnorm_bwd("norm1_bwd", dn1, x2, r1, g_ffn1, dh1)
    got_c, (late_all,) = comm_only("tail_cross", [send_cross("w_ffn1_in", sub=(7, 1, 8)),
                                                  stack_job(_pack([dg_ffn1]))])
    take_cross(["w_ffn1_in"], got_c)
    (got_f,) = comm_only("tail_final", [send_final("w_ffn1_in")])
    take_final(["w_ffn1_in"], got_f)
    update("w_ffn1_in")

    early_sum, late_sum = small_sum("small_sum", [early_all, late_all])
    for n, g in zip(EARLY_SMALL, _unpack(early_sum, [early[n].shape for n in EARLY_SMALL])):
        grads[n] = g
    grads["g_ffn1"] = _unpack(late_sum, [dg_ffn1.shape])[0]
    taps_cols = conv_w.shape[2]
    grads["conv_w"] = lax.dynamic_slice_in_dim(grads["conv_w"], blk * taps_cols, taps_cols, axis=1)
    packed = [_pack([src[n] for n in SMALL]) for src in (wts, grads, mom, var)]
    own_shapes = [wts[n].shape for n in SMALL]
    for dst, buf in zip((delta, new_m, new_v), adamw("adamw_small", *packed)[1:]):
        for n, a in zip(SMALL, _unpack(buf, own_shapes)):
            dst[n] = a

    loss = lax.psum(loss_blk[0, 0], ("x", "y", "c"))
    outs = [loss, dx[None]]
    for group in (grads, delta, new_m, new_v):
        outs += [group[n].reshape(wts[n].shape) for n in WEIGHTS]
    return tuple(outs)
```

```python
import math

import jax
import jax.numpy as jnp
from jax import lax
from jax.experimental import pallas as pl
from jax.experimental.pallas import tpu as pltpu

F32 = jnp.float32
BF16 = jnp.bfloat16
EPS = 1e-6
GROUP = 128
XA_HEADS = 4
CONV_K = 3
N_CHIPS = 4
VMEM_LIMIT_BYTES = 56 * 1024 * 1024

ADAM_LR = 0.001
ADAM_B1 = 0.9
ADAM_B2 = 0.999
ADAM_EPS = 1e-08
ADAM_WD = 0.01
ADAM_STEP = 10

MESH = pl.DeviceIdType.MESH
ANY = pl.BlockSpec(memory_space=pl.ANY)


def _tile(dim, pref, mult=128):
    if dim <= pref:
        return dim
    t = (pref // mult) * mult
    while t >= mult:
        if dim % t == 0:
            return t
        t -= mult
    raise ValueError(f"no tile for {dim} under {pref}")


def _params(sem):
    return pltpu.CompilerParams(dimension_semantics=sem, vmem_limit_bytes=VMEM_LIMIT_BYTES)


def _sds(shape, dtype):
    return jax.ShapeDtypeStruct(shape, dtype)


def _dot_nn(a, b):
    return jnp.dot(a, b, preferred_element_type=F32)


def _dot_nt(a, b):
    return lax.dot_general(a, b, (((1,), (1,)), ((), ())), preferred_element_type=F32)


def _dot_tn(a, b):
    return lax.dot_general(a, b, (((0,), (0,)), ((), ())), preferred_element_type=F32)


class Job:
    def __init__(self, inputs, out_shapes, aliases, sems, start, middle, finish):
        self.inputs, self.out_shapes, self.aliases, self.sems = inputs, out_shapes, aliases, sems
        self.start, self.middle, self.finish = start, middle, finish


def _place():
    x, y, c = lax.axis_index("x"), lax.axis_index("y"), lax.axis_index("c")
    chips = [(1 - x, y), (x, 1 - y), (1 - x, 1 - y)]
    return x, y, c, chips


def _ds(start, size, lane):
    if not isinstance(start, int):
        start = pl.multiple_of(start, 128 if lane else 16)
    return pl.ds(start, size)


WHOLE = (0, 1, 1)


def _window(ref, axis, shape, blk=None, half=None, sub=WHOLE, within=WHOLE):
    n = shape[axis] // N_CHIPS
    hs = shape[1 - axis] // 2
    idx = [slice(None), slice(None)]
    if blk is not None:
        b_first, b_count, b_pieces = within
        b_ext = n // b_pieces
        idx[axis] = _ds(blk * n + b_first * b_ext, b_count * b_ext, axis == 1)
    first, count, pieces = sub
    ext = hs // pieces
    if half is not None:
        idx[1 - axis] = _ds(half * hs + first * ext, count * ext, axis == 0)
    elif pieces > 1:
        idx[1 - axis] = _ds(first * ext, count * ext, axis == 0)
    return ref.at[tuple(idx)]


def _remote(src, dst, send_sem, recv_sem, dev):
    return pltpu.make_async_remote_copy(src_ref=src, dst_ref=dst, send_sem=send_sem, recv_sem=recv_sem,
                                        device_id=dev, device_id_type=MESH)


def _full_shape(block_shape, axis):
    out = list(block_shape)
    out[axis] *= N_CHIPS
    return tuple(out)


def _half_all(shape, axis):
    out = list(shape)
    out[1 - axis] //= 2
    return tuple(out)


def _block(shape, axis):
    out = list(shape)
    out[axis] //= N_CHIPS
    return tuple(out)


def _half_block(shape, axis):
    return _half_all(_block(shape, axis), axis)


def gather_job(items):
    nw = len(items)
    shapes = [item[0].shape for item in items]
    n_sem = 8

    def parts(sub):
        first, count, pieces = sub
        return (2 * first, count, 2 * pieces), (2 * first + count, count, 2 * pieces)

    def start(pos, ins, outs, sems):
        x, y, c, chips = pos
        for w, (_, ax, sub, within) in enumerate(items):
            mine = _window(outs[w], ax, shapes[w], blk=2 * x + y, half=c, sub=sub, within=within)
            for j in range(2):
                _remote(mine, mine, sems[0].at[n_sem * w + j], sems[1].at[n_sem * w + j], (*chips[j], c)).start()

    def middle(pos, ins, outs, sems):
        x, y, c, chips = pos
        for w, (_, ax, sub, within) in enumerate(items):
            for j in range(2):
                cx, cy = chips[j]
                landed = _window(outs[w], ax, shapes[w], blk=2 * cx + cy, half=c, sub=sub, within=within)
                _remote(landed, landed, sems[0].at[n_sem * w + j], sems[1].at[n_sem * w + j], (cx, cy, c)).wait_recv()
                part = _window(outs[w], ax, shapes[w], blk=2 * cx + cy, half=c, sub=parts(sub)[j], within=within)
                _remote(part, part, sems[0].at[n_sem * w + 2 + j], sems[1].at[n_sem * w + 2 + j],
                        (*chips[1 - j], c)).start()
                _remote(landed, landed, sems[0].at[n_sem * w + 4 + j], sems[1].at[n_sem * w + 4 + j],
                        (x, y, 1 - c)).start()

    def finish(pos, ins, outs, sems):
        x, y, c, chips = pos
        sib = (x, y, 1 - c)
        for w, (_, ax, sub, within) in enumerate(items):
            dx, dy = chips[2]
            for j in range(2):
                part = _window(outs[w], ax, shapes[w], blk=2 * dx + dy, half=c, sub=parts(sub)[j], within=within)
                cp = _remote(part, part, sems[0].at[n_sem * w + 2 + j], sems[1].at[n_sem * w + 2 + j], sib)
                cp.wait_recv()
                cp.wait_send()
            diag = _window(outs[w], ax, shapes[w], blk=2 * dx + dy, half=c, sub=sub, within=within)
            _remote(diag, diag, sems[0].at[n_sem * w + 6], sems[1].at[n_sem * w + 6], sib).start()
        for w, (_, ax, sub, within) in enumerate(items):
            for j, (cx, cy) in enumerate(chips):
                passed = _window(outs[w], ax, shapes[w], blk=2 * cx + cy, half=1 - c, sub=sub, within=within)
                cp = _remote(passed, passed, sems[0].at[n_sem * w + 4 + j], sems[1].at[n_sem * w + 4 + j], sib)
                cp.wait_recv()
                cp.wait_send()
            mine = _window(outs[w], ax, shapes[w], blk=2 * x + y, half=c, sub=sub, within=within)
            for j in range(2):
                _remote(mine, mine, sems[0].at[n_sem * w + j], sems[1].at[n_sem * w + j], sib).wait_send()

    sems = [pltpu.SemaphoreType.DMA((n_sem * nw,)), pltpu.SemaphoreType.DMA((n_sem * nw,))]
    return Job([item[0] for item in items], [_sds(item[0].shape, item[0].dtype) for item in items],
               {w: w for w in range(nw)}, sems, start, middle, finish)


def pair_job(grads, axes, is_half=False):
    nw = len(grads)
    shapes = [g.shape for g in grads]

    def start(pos, ins, outs, sems):
        x, y, c, _ = pos
        for w in range(nw):
            src = ins[w] if is_half else _window(ins[w], axes[w], shapes[w], half=1 - c)
            _remote(src, outs[w], sems[0].at[w], sems[1].at[w], (x, y, 1 - c)).start()

    def finish(pos, ins, outs, sems):
        x, y, c, _ = pos
        for w in range(nw):
            cp = _remote(outs[w], outs[w], sems[0].at[w], sems[1].at[w], (x, y, 1 - c))
            cp.wait_recv()
            cp.wait_send()

    sems = [pltpu.SemaphoreType.DMA((nw,)), pltpu.SemaphoreType.DMA((nw,))]
    out_shapes = [_sds(s if is_half else _half_all(s, a), BF16) for s, a in zip(shapes, axes)]
    return Job(list(grads), out_shapes, {}, sems, start, None, finish)


def cross_job(items):
    nw = len(items)
    inputs, aliases = [], {}
    for w, (part, ax, shape, prev, sub) in enumerate(items):
        inputs.append(part)
        if prev is not None:
            aliases[len(inputs)] = w
            inputs.append(prev)

    def copies(pos, ins, outs, sems):
        x, y, c, chips = pos
        k = 0
        for w, (_, ax, shape, prev, sub) in enumerate(items):
            src = ins[k]
            k += 2 if prev is not None else 1
            for j, (cx, cy) in enumerate(chips):
                slot = _window(outs[w].at[j], ax, shape, sub=sub)
                yield (_remote(_window(src, ax, shape, blk=2 * cx + cy, sub=sub), slot,
                               sems[0].at[3 * w + j], sems[1].at[3 * w + j], (cx, cy, c)),
                       _remote(slot, slot, sems[0].at[3 * w + j], sems[1].at[3 * w + j], (cx, cy, c)))

    def start(pos, ins, outs, sems):
        for send, _ in copies(pos, ins, outs, sems):
            send.start()

    def finish(pos, ins, outs, sems):
        for send, recv in copies(pos, ins, outs, sems):
            recv.wait_recv()
            send.wait_send()

    sems = [pltpu.SemaphoreType.DMA((3 * nw,)), pltpu.SemaphoreType.DMA((3 * nw,))]
    out_shapes = [_sds((3,) + _half_block(shape, ax), BF16) for _, ax, shape, _, _ in items]
    return Job(inputs, out_shapes, aliases, sems, start, None, finish)


def final_job(blocks, axes, shapes):
    nw = len(blocks)

    def start(pos, ins, outs, sems):
        x, y, c, _ = pos
        for w in range(nw):
            mine = _window(outs[w], axes[w], shapes[w], half=c)
            _remote(mine, mine, sems[0].at[w], sems[1].at[w], (x, y, 1 - c)).start()

    def finish(pos, ins, outs, sems):
        x, y, c, _ = pos
        for w in range(nw):
            theirs = _window(outs[w], axes[w], shapes[w], half=1 - c)
            cp = _remote(theirs, theirs, sems[0].at[w], sems[1].at[w], (x, y, 1 - c))
            cp.wait_recv()
            cp.wait_send()

    sems = [pltpu.SemaphoreType.DMA((nw,)), pltpu.SemaphoreType.DMA((nw,))]
    return Job(list(blocks), [_sds(b.shape, b.dtype) for b in blocks], {w: w for w in range(nw)}, sems, start, None,
               finish)


def stack_job(small):
    def peers(pos):
        x, y, c, _ = pos
        for k in range(1, 8):
            yield k - 1, (1 - x if k & 4 else x, 1 - y if k & 2 else y, 1 - c if k & 1 else c)

    def start(pos, ins, outs, sems):
        x, y, c, _ = pos
        mine = outs[0].at[4 * x + 2 * y + c]
        pltpu.make_async_copy(ins[0], mine, sems[2]).start()
        for k, dev in peers(pos):
            _remote(ins[0], mine, sems[0].at[k], sems[1].at[k], dev).start()

    def finish(pos, ins, outs, sems):
        x, y, c, _ = pos
        for k, (px, py, pc) in peers(pos):
            slot = outs[0].at[4 * px + 2 * py + pc]
            cp = _remote(slot, slot, sems[0].at[k], sems[1].at[k], (px, py, pc))
            cp.wait_recv()
            cp.wait_send()
        pltpu.make_async_copy(ins[0], outs[0].at[4 * x + 2 * y + c], sems[2]).wait()

    sems = [pltpu.SemaphoreType.DMA((7,)), pltpu.SemaphoreType.DMA((7,)), pltpu.SemaphoreType.DMA]
    return Job([small], [_sds((8,) + small.shape, small.dtype)], {}, sems, start, None, finish)


def columns_job(block):
    cols = block.shape[1]
    place = lambda out, b: out.at[:, _ds(b * cols, cols, True)]

    def start(pos, ins, outs, sems):
        x, y, c, chips = pos
        pltpu.make_async_copy(ins[0], place(outs[0], 2 * x + y), sems[2]).start()
        for j, (cx, cy) in enumerate(chips):
            _remote(ins[0], place(outs[0], 2 * x + y), sems[0].at[j], sems[1].at[j], (cx, cy, c)).start()

    def finish(pos, ins, outs, sems):
        x, y, c, chips = pos
        for j, (cx, cy) in enumerate(chips):
            got = place(outs[0], 2 * cx + cy)
            cp = _remote(got, got, sems[0].at[j], sems[1].at[j], (cx, cy, c))
            cp.wait_recv()
            cp.wait_send()
        pltpu.make_async_copy(ins[0], place(outs[0], 2 * x + y), sems[2]).wait()

    sems = [pltpu.SemaphoreType.DMA((3,)), pltpu.SemaphoreType.DMA((3,)), pltpu.SemaphoreType.DMA]
    return Job([block], [_sds((block.shape[0], N_CHIPS * cols), block.dtype)], {}, sems, start, None, finish)


def _call(name, body, grid, in_specs, out_specs, out_shape, args, scratch=(), sem=None, jobs=(), place=None,
          carried=None):
    n_in, n_out, n_sc = len(args), len(out_shape), len(scratch)
    carried = dict(carried or {})

    def launch(fn, in_specs, out_specs, out_shape, scratch, aliases, sem, operands):
        if place is None:
            return pl.pallas_call(
                fn, name=name, grid=grid, in_specs=in_specs, out_specs=out_specs, out_shape=out_shape,
                scratch_shapes=scratch, input_output_aliases=aliases, compiler_params=_params(sem))(*operands)
        spec = pltpu.PrefetchScalarGridSpec(num_scalar_prefetch=1, grid=grid, in_specs=in_specs,
                                            out_specs=out_specs, scratch_shapes=scratch)
        return pl.pallas_call(
            lambda p_ref, *refs: fn(*refs), name=name, grid_spec=spec, out_shape=out_shape,
            input_output_aliases={k + 1: v for k, v in aliases.items()}, compiler_params=_params(sem),
        )(place, *operands)

    if not jobs:
        outs = launch(body, list(in_specs), list(out_specs), list(out_shape), list(scratch), carried, sem, args)
        return list(outs), []

    total = math.prod(grid) if grid else 1
    mid = min(total - 1, (2 * total) // 3)

    def split(refs, start, counts):
        out = []
        for n in counts:
            out.append(refs[start:start + n])
            start += n
        return out, start

    def wrapped(*refs):
        c_in = refs[:n_in]
        j_ins, p = split(refs, n_in, [len(j.inputs) for j in jobs])
        c_out = refs[p:p + n_out]
        j_outs, p = split(refs, p + n_out, [len(j.out_shapes) for j in jobs])
        c_sc = refs[p:p + n_sc]
        j_sems, p = split(refs, p + n_sc, [len(j.sems) for j in jobs])
        pos = _place()
        step = 0
        for axis, extent in enumerate(grid):
            step = step * extent + pl.program_id(axis)

        def run(phase):
            for j, ins, outs, sems in zip(jobs, j_ins, j_outs, j_sems):
                fn = getattr(j, phase)
                if fn is not None:
                    fn(pos, ins, outs, sems)

        if total == 1:
            run("start")
            body(*c_in, *c_out, *c_sc)
            run("middle")
            run("finish")
            return
        pl.when(step == 0)(lambda: run("start"))
        body(*c_in, *c_out, *c_sc)
        if any(j.middle is not None for j in jobs):
            pl.when(step == mid)(lambda: run("middle"))
        pl.when(step == total - 1)(lambda: run("finish"))

    aliases, in_at, out_at = carried, n_in, n_out
    for j in jobs:
        for src, dst in j.aliases.items():
            aliases[in_at + src] = out_at + dst
        in_at += len(j.inputs)
        out_at += len(j.out_shapes)
    outs = launch(
        wrapped, list(in_specs) + [ANY] * (in_at - n_in), list(out_specs) + [ANY] * (out_at - n_out),
        list(out_shape) + [s for j in jobs for s in j.out_shapes],
        list(scratch) + [s for j in jobs for s in j.sems], aliases, ("arbitrary",) * len(grid),
        [*args, *[a for j in jobs for a in j.inputs]])
    job_outs, p = split(outs, n_out, [len(j.out_shapes) for j in jobs])
    return list(outs[:n_out]), [list(o) for o in job_outs]


def comm_only(name, jobs):
    def body(dummy_ref, out_ref):
        out_ref[...] = dummy_ref[...]

    dummy = jnp.zeros((8, 128), F32)
    spec = pl.BlockSpec((8, 128), lambda: (0, 0))
    return _call(name, body, (), [spec], [spec], [_sds((8, 128), F32)], [dummy], jobs=jobs)[1]


def _ret(outs, job_outs, jobs, single=True):
    res = outs[0] if single else outs
    return (res, job_outs) if jobs else res


def rmsnorm_fwd(name, x, g):
    s, d = x.shape
    tm = _tile(s, 512, 8)

    def body(x_ref, g_ref, n_ref, r_ref):
        xv = x_ref[...]
        r = lax.rsqrt(jnp.mean(xv * xv, axis=-1, keepdims=True) + EPS)
        n_ref[...] = (xv * r * g_ref[...]).astype(BF16)
        r_ref[...] = r

    row = lambda i: (i, 0)
    return _call(
        name, body, (s // tm,),
        [pl.BlockSpec((tm, d), row), pl.BlockSpec((1, d), lambda i: (0, 0))],
        [pl.BlockSpec((tm, d), row), pl.BlockSpec((tm, 1), row)],
        [_sds((s, d), BF16), _sds((s, 1), F32)], [x, g], sem=("arbitrary",))[0]


def rmsnorm_bwd(name, dn, x, r, g, dh_in, jobs=()):
    s, d = x.shape
    tm = _tile(s, 512, 8)

    def body(dn_ref, x_ref, r_ref, g_ref, dh_ref, out_ref, outb_ref, dg_ref):
        i = pl.program_id(0)
        xh = x_ref[...] * r_ref[...]
        dnv = dn_ref[...]
        dxh = dnv * g_ref[...]
        dx = r_ref[...] * (dxh - xh * jnp.mean(dxh * xh, axis=-1, keepdims=True))
        out = dh_ref[...] + dx
        out_ref[...] = out
        outb_ref[...] = out.astype(BF16)
        part = jnp.sum(dnv * xh, axis=0, keepdims=True)

        @pl.when(i == 0)
        def _():
            dg_ref[...] = part

        @pl.when(i > 0)
        def _():
            dg_ref[...] += part

    row = lambda i: (i, 0)
    fixed = lambda i: (0, 0)
    outs, job_outs = _call(
        name, body, (s // tm,),
        [pl.BlockSpec((tm, d), row), pl.BlockSpec((tm, d), row), pl.BlockSpec((tm, 1), row),
         pl.BlockSpec((1, d), fixed), pl.BlockSpec((tm, d), row)],
        [pl.BlockSpec((tm, d), row), pl.BlockSpec((tm, d), row), pl.BlockSpec((1, d), fixed)],
        [_sds((s, d), F32), _sds((s, d), BF16), _sds((1, d), F32)], [dn, x, r, g, dh_in],
        sem=("arbitrary",), jobs=jobs)
    return _ret(outs, job_outs, jobs, single=False)


def gain_grad(name, dn_a, dn_b, x, r):
    s, d = x.shape
    tm = _tile(s, 512, 8)

    def body(a_ref, b_ref, x_ref, r_ref, dg_ref):
        i = pl.program_id(0)
        part = jnp.sum((a_ref[...] + b_ref[...]) * (x_ref[...] * r_ref[...]), axis=0, keepdims=True)

        @pl.when(i == 0)
        def _():
            dg_ref[...] = part

        @pl.when(i > 0)
        def _():
            dg_ref[...] += part

    row = lambda i: (i, 0)
    return _call(
        name, body, (s // tm,),
        [pl.BlockSpec((tm, d), row), pl.BlockSpec((tm, d), row), pl.BlockSpec((tm, d), row),
         pl.BlockSpec((tm, 1), row)],
        [pl.BlockSpec((1, d), lambda i: (0, 0))], [_sds((1, d), F32)], [dn_a, dn_b, x, r],
        sem=("arbitrary",))[0][0]


def loss_head(name, h, g, target):
    s, d = h.shape
    tm = _tile(s, 512, 8)
    nsteps = s // tm

    def body(h_ref, g_ref, t_ref, loss_ref, dh_ref, dhb_ref, dg_ref, sq_ref):
        i = pl.program_id(0)
        hv = h_ref[...]
        gv = g_ref[...]
        r = lax.rsqrt(jnp.mean(hv * hv, axis=-1, keepdims=True) + EPS)
        xh = hv * r
        err = xh * gv - t_ref[...]
        dy = err * (1.0 / d)
        dxh = dy * gv
        dh = r * (dxh - xh * jnp.mean(dxh * xh, axis=-1, keepdims=True))
        dh_ref[...] = dh
        dhb_ref[...] = dh.astype(BF16)
        dg_part = jnp.sum(dy * xh, axis=0, keepdims=True)
        sq_part = jnp.sum(err * err, axis=0, keepdims=True)

        @pl.when(i == 0)
        def _():
            dg_ref[...] = dg_part
            sq_ref[...] = sq_part

        @pl.when(i > 0)
        def _():
            dg_ref[...] += dg_part
            sq_ref[...] += sq_part

        @pl.when(i == nsteps - 1)
        def _():
            total = jnp.sum(sq_ref[...], axis=-1, keepdims=True) * (0.5 / d)
            loss_ref[...] = jnp.broadcast_to(total, loss_ref.shape)

    row = lambda i: (i, 0)
    fixed = lambda i: (0, 0)
    return _call(
        name, body, (nsteps,),
        [pl.BlockSpec((tm, d), row), pl.BlockSpec((1, d), fixed), pl.BlockSpec((tm, d), row)],
        [pl.BlockSpec((8, 128), fixed), pl.BlockSpec((tm, d), row), pl.BlockSpec((tm, d), row),
         pl.BlockSpec((1, d), fixed)],
        [_sds((8, 128), F32), _sds((s, d), F32), _sds((s, d), BF16), _sds((1, d), F32)], [h, g, target],
        scratch=[pltpu.VMEM((1, d), F32)], sem=("arbitrary",))[0]


def _mm(name, grid, in_arrays, in_specs, out_shapes, out_specs, acc_tile, dot, epilogue, jobs=(), place=None):
    nk = grid[2]
    n_in = len(in_arrays)
    n_out = len(out_shapes)

    def body(*refs):
        ins, outs = refs[:n_in], refs[n_in:n_in + n_out]
        if nk == 1:
            epilogue(dot(*ins), ins, outs)
            return
        acc = refs[n_in + n_out]
        k = pl.program_id(2)

        @pl.when(k == 0)
        def _():
            acc[...] = dot(*ins)

        @pl.when(jnp.logical_and(k > 0, k < nk - 1))
        def _():
            acc[...] += dot(*ins)

        @pl.when(k == nk - 1)
        def _():
            epilogue(acc[...] + dot(*ins), ins, outs)

    scratch = [pltpu.VMEM(acc_tile, F32)] if nk > 1 else []
    outs, job_outs = _call(name, body, grid, in_specs, out_specs, out_shapes, in_arrays, scratch=scratch,
                           sem=("parallel", "parallel", "arbitrary"), jobs=jobs, place=place)
    return _ret(outs, job_outs, jobs)


def _store(scale, dtype):
    def epilogue(acc, ins, outs):
        outs[0][...] = (acc * scale if scale != 1.0 else acc).astype(dtype)
    return epilogue


def mm_nn(name, a, w, out_dtype, tm=1024, tn=1024, tk=2048, jobs=()):
    m, kd = a.shape
    n = w.shape[1]
    tm, tn, tk = _tile(m, tm, 8), _tile(n, tn), _tile(kd, tk)
    return _mm(
        name, (n // tn, m // tm, kd // tk), [a, w],
        [pl.BlockSpec((tm, tk), lambda j, i, k: (i, k)), pl.BlockSpec((tk, tn), lambda j, i, k: (k, j))],
        [_sds((m, n), out_dtype)], [pl.BlockSpec((tm, tn), lambda j, i, k: (i, j))], (tm, tn),
        lambda a_ref, w_ref: _dot_nn(a_ref[...], w_ref[...]), _store(1.0, out_dtype), jobs)


def mm_nn_resid(name, a, w, x, scale, tm=1024, tn=1024, tk=1408, jobs=()):
    m, kd = a.shape
    n = w.shape[1]
    tm, tn, tk = _tile(m, tm, 8), _tile(n, tn), _tile(kd, tk)

    def epilogue(acc, ins, outs):
        outs[0][...] = ins[2][...] + scale * acc

    return _mm(
        name, (n // tn, m // tm, kd // tk), [a, w, x],
        [pl.BlockSpec((tm, tk), lambda j, i, k: (i, k)), pl.BlockSpec((tk, tn), lambda j, i, k: (k, j)),
         pl.BlockSpec((tm, tn), lambda j, i, k: (i, j))],
        [_sds((m, n), F32)], [pl.BlockSpec((tm, tn), lambda j, i, k: (i, j))], (tm, tn),
        lambda a_ref, w_ref, x_ref: _dot_nn(a_ref[...], w_ref[...]), epilogue, jobs)


def mm_nt(name, a, w, out_dtype, scale=1.0, tm=1024, tn=1024, tk=2048, jobs=()):
    m, kd = a.shape
    n = w.shape[0]
    tm, tn, tk = _tile(m, tm, 8), _tile(n, tn), _tile(kd, tk)
    return _mm(
        name, (n // tn, m // tm, kd // tk), [a, w],
        [pl.BlockSpec((tm, tk), lambda j, i, k: (i, k)), pl.BlockSpec((tn, tk), lambda j, i, k: (j, k))],
        [_sds((m, n), out_dtype)], [pl.BlockSpec((tm, tn), lambda j, i, k: (i, j))], (tm, tn),
        lambda a_ref, w_ref: _dot_nt(a_ref[...], w_ref[...]), _store(scale, out_dtype), jobs)


def mm_nt_pair(name, a3, w, out_dtype, tm=1024, tn=1024, tk=2816, jobs=()):
    _, m, f = a3.shape
    n = w.shape[0]
    tm, tn, tk = _tile(m, tm, 8), _tile(n, tn), _tile(f, tk)
    nkf = f // tk
    return _mm(
        name, (n // tn, m // tm, 2 * nkf), [a3, w],
        [pl.BlockSpec((None, tm, tk), lambda j, i, k: (k // nkf, i, k % nkf)),
         pl.BlockSpec((tn, tk), lambda j, i, k: (j, k))],
        [_sds((m, n), out_dtype)], [pl.BlockSpec((tm, tn), lambda j, i, k: (i, j))], (tm, tn),
        lambda a_ref, w_ref: _dot_nt(a_ref[...], w_ref[...]), _store(1.0, out_dtype), jobs)


def mm_nt_norm_bwd(name, a, w, x, r, g, dh_in, tm=512, tk=1408, jobs=()):
    pair = a.ndim == 3
    m, kd = a.shape[-2], a.shape[-1]
    d = w.shape[0]
    tm, tk = _tile(m, tm, 8), _tile(kd, tk)
    nkf = kd // tk
    nk = 2 * nkf if pair else nkf
    if pair:
        a_spec = pl.BlockSpec((None, tm, tk), lambda i, k: (k // nkf, i, k % nkf))
    else:
        a_spec = pl.BlockSpec((tm, tk), lambda i, k: (i, k))
    row = lambda i, k: (i, 0)
    fixed = lambda i, k: (0, 0)

    def body(a_ref, w_ref, x_ref, r_ref, g_ref, dh_ref, out_ref, outb_ref, dg_ref, *acc):
        i, k = pl.program_id(0), pl.program_id(1)
        dot = lambda: _dot_nt(a_ref[...], w_ref[...])

        def finish(dn):
            xh = x_ref[...] * r_ref[...]
            dxh = dn * g_ref[...]
            out = dh_ref[...] + r_ref[...] * (dxh - xh * jnp.mean(dxh * xh, axis=-1, keepdims=True))
            out_ref[...] = out
            outb_ref[...] = out.astype(BF16)
            part = jnp.sum(dn * xh, axis=0, keepdims=True)

            @pl.when(i == 0)
            def _():
                dg_ref[...] = part

            @pl.when(i > 0)
            def _():
                dg_ref[...] += part

        if nk == 1:
            finish(dot())
            return

        @pl.when(k == 0)
        def _():
            acc[0][...] = dot()

        @pl.when(jnp.logical_and(k > 0, k < nk - 1))
        def _():
            acc[0][...] += dot()

        @pl.when(k == nk - 1)
        def _():
            finish(acc[0][...] + dot())

    outs, job_outs = _call(
        name, body, (m // tm, nk),
        [a_spec, pl.BlockSpec((d, tk), lambda i, k: (0, k)), pl.BlockSpec((tm, d), row), pl.BlockSpec((tm, 1), row),
         pl.BlockSpec((1, d), fixed), pl.BlockSpec((tm, d), row)],
        [pl.BlockSpec((tm, d), row), pl.BlockSpec((tm, d), row), pl.BlockSpec((1, d), fixed)],
        [_sds((m, d), F32), _sds((m, d), BF16), _sds((1, d), F32)], [a, w, x, r, g, dh_in],
        scratch=[pltpu.VMEM((tm, d), F32)] if nk > 1 else [], sem=("arbitrary", "arbitrary"), jobs=jobs)
    return _ret(outs, job_outs, jobs, single=False)


def mm_tn(name, a, b, out_dtype, scale=1.0, tm=1024, tn=1024, tk=4096, jobs=()):
    kd, m = a.shape
    n = b.shape[1]
    tm, tn, tk = _tile(m, tm), _tile(n, tn), _tile(kd, tk, 16)
    return _mm(
        name, (n // tn, m // tm, kd // tk), [a, b],
        [pl.BlockSpec((tk, tm), lambda j, i, k: (k, i)), pl.BlockSpec((tk, tn), lambda j, i, k: (k, j))],
        [_sds((m, n), out_dtype)], [pl.BlockSpec((tm, tn), lambda j, i, k: (i, j))], (tm, tn),
        lambda a_ref, b_ref: _dot_tn(a_ref[...], b_ref[...]), _store(scale, out_dtype), jobs)


def mm_tn_pair(name, a, b3, out_dtype, tm=1024, tn=512, tk=4096, jobs=()):
    kd, m = a.shape
    f = b3.shape[2]
    tm, tn, tk = _tile(m, tm), _tile(f, tn), _tile(kd, tk, 16)
    nf = f // tn
    return _mm(
        name, (m // tm, 2 * nf, kd // tk), [a, b3],
        [pl.BlockSpec((tk, tm), lambda i, j, k: (k, i)),
         pl.BlockSpec((None, tk, tn), lambda i, j, k: (j // nf, k, j % nf))],
        [_sds((m, 2 * f), out_dtype)], [pl.BlockSpec((tm, tn), lambda i, j, k: (i, j))], (tm, tn),
        lambda a_ref, b_ref: _dot_tn(a_ref[...], b_ref[...]), _store(1.0, out_dtype), jobs)


def mm_tn_pair_half(name, a, b3, out_dtype, place, mine, tm=1024, tn=512, tk=4096, jobs=()):
    kd, m = a.shape
    f = b3.shape[2]
    tm, tn, tk = _tile(m // 2, tm), _tile(f, tn), _tile(kd, tk, 16)
    nf, nbm = f // tn, m // 2 // tm
    which = (lambda p: p[1]) if mine else (lambda p: 1 - p[1])
    return _mm(
        name, (nbm, 2 * nf, kd // tk), [a, b3],
        [pl.BlockSpec((tk, tm), lambda i, j, k, p: (k, i + which(p) * nbm)),
         pl.BlockSpec((None, tk, tn), lambda i, j, k, p: (j // nf, k, j % nf))],
        [_sds((m // 2, 2 * f), out_dtype)], [pl.BlockSpec((tm, tn), lambda i, j, k, p: (i, j))], (tm, tn),
        lambda a_ref, b_ref: _dot_tn(a_ref[...], b_ref[...]), _store(1.0, out_dtype), jobs, place)


def swiglu_fwd(name, n, w_in, tm=1024, tn=512, jobs=(), stride=1, phase=0, prev=None):
    s, d = n.shape
    f = w_in.shape[1] // 2
    tm, tn = _tile(s, tm, 8), _tile(f, tn)
    nf = f // tn
    col = lambda j: j * stride + phase

    def body(n_ref, wg_ref, wu_ref, *rest):
        gu_ref, a_ref = rest[-2:]
        nv = n_ref[...]
        g = _dot_nn(nv, wg_ref[...])
        u = _dot_nn(nv, wu_ref[...])
        gu_ref[0] = g.astype(BF16)
        gu_ref[1] = u.astype(BF16)
        a_ref[...] = (g * jax.nn.sigmoid(g) * u).astype(BF16)

    kept = list(prev) if prev is not None else []
    outs, job_outs = _call(
        name, body, (nf // stride, s // tm),
        [pl.BlockSpec((tm, d), lambda j, i: (i, 0)), pl.BlockSpec((d, tn), lambda j, i: (0, col(j))),
         pl.BlockSpec((d, tn), lambda j, i: (0, col(j) + nf))] + [ANY] * len(kept),
        [pl.BlockSpec((2, tm, tn), lambda j, i: (0, i, col(j))), pl.BlockSpec((tm, tn), lambda j, i: (i, col(j)))],
        [_sds((2, s, f), BF16), _sds((s, f), BF16)], [n, w_in, w_in] + kept, sem=("parallel", "parallel"),
        jobs=jobs, carried={3 + k: k for k in range(len(kept))})
    return _ret(outs, job_outs, jobs, single=False)


def swiglu_bwd(name, dh, w_out, gu, scale, tm=1024, tn=512, jobs=()):
    s, d = dh.shape
    f = w_out.shape[0]
    tm, tn = _tile(s, tm, 8), _tile(f, tn)

    sub = _tile(tm, 256, 8)

    def body(dh_ref, w_ref, gu_ref, out_ref):
        for lo in range(0, tm, sub):
            rows = slice(lo, lo + sub)
            da = (_dot_nt(dh_ref[rows, :], w_ref[...]) * scale).astype(BF16)
            g = gu_ref[0, rows, :]
            u = gu_ref[1, rows, :]
            sg = 0.5 * jnp.tanh(0.5 * g) + 0.5
            t = g * sg
            out_ref[0, rows, :] = da * (u * (sg + t * (1.0 - sg)))
            out_ref[1, rows, :] = da * t

    outs, job_outs = _call(
        name, body, (f // tn, s // tm),
        [pl.BlockSpec((tm, d), lambda j, i: (i, 0)), pl.BlockSpec((tn, d), lambda j, i: (j, 0)),
         pl.BlockSpec((2, tm, tn), lambda j, i: (0, i, j))],
        [pl.BlockSpec((2, tm, tn), lambda j, i: (0, i, j))],
        [_sds((2, s, f), BF16)], [dh, w_out, gu], sem=("parallel", "parallel"), jobs=jobs)
    return _ret(outs, job_outs, jobs)


HALO = 16


def _conv_inputs(z_ref, hgc_ref, hhc_ref, i, cw, tm):
    gc = z_ref[:, cw:2 * cw].astype(F32)
    hc = z_ref[:, 2 * cw:3 * cw].astype(F32)
    cin = gc * hc
    halo = hgc_ref[...].astype(F32) * hhc_ref[...].astype(F32) * (i > 0).astype(F32)
    row = lax.broadcasted_iota(jnp.int32, (tm, cw), 0)
    x1 = jnp.where(row == 0, halo[HALO - 1:HALO], pltpu.roll(cin, 1, 0))
    x2 = jnp.where(row == 0, halo[HALO - 2:HALO - 1], jnp.where(row == 1, halo[HALO - 1:HALO], pltpu.roll(cin, 2, 0)))
    return gc, hc, cin, x1, x2


def _tril(w):
    r = lax.broadcasted_iota(jnp.int32, w.shape, 0)
    c = lax.broadcasted_iota(jnp.int32, w.shape, 1)
    return jnp.where(r >= c, w, jnp.zeros_like(w))


def mixer_fwd(name, z, conv_w, conv_b, g_v, w_s, b_t, tm=256, jobs=()):
    s, zc = z.shape
    cw = conv_w.shape[1]
    gw = g_v.shape[1]
    heads = gw // GROUP
    tm = _tile(s, tm)
    hb = tm // HALO

    def body(z_ref, hgc_ref, hhc_ref, cw_ref, cb_ref, gv_ref, ws_ref, bt_ref, y_ref):
        i = pl.program_id(0)
        _, _, cin, x1, x2 = _conv_inputs(z_ref, hgc_ref, hhc_ref, i, cw, tm)
        cv = cb_ref[...] + cw_ref[2:3, :] * cin + cw_ref[1:2, :] * x1 + cw_ref[0:1, :] * x2
        y_ref[:, 0:cw] = (z_ref[:, 0:cw].astype(F32) * cv).astype(BF16)
        for h in range(heads):
            lo = h * GROUP
            vh = z_ref[:, 3 * cw + gw + lo:3 * cw + gw + lo + GROUP].astype(F32)
            rv = lax.rsqrt(jnp.mean(vh * vh, axis=-1, keepdims=True) + EPS)
            vn = (vh * rv * gv_ref[:, lo:lo + GROUP]).astype(BF16)
            w = _tril(ws_ref[h]).astype(BF16)
            for n in range(tm // GROUP):
                rows = slice(n * GROUP, (n + 1) * GROUP)
                sg = _dot_nn(w, vn[rows]) + bt_ref[:, h:h + 1]
                u = z_ref[rows, 3 * cw + lo:3 * cw + lo + GROUP].astype(F32)
                y_ref[rows, cw + lo:cw + lo + GROUP] = (u * sg).astype(BF16)

    fixed2 = lambda i: (0, 0)
    outs, job_outs = _call(
        name, body, (s // tm,),
        [pl.BlockSpec((tm, zc), lambda i: (i, 0)),
         pl.BlockSpec((HALO, cw), lambda i: (jnp.maximum(i * hb - 1, 0), 1)),
         pl.BlockSpec((HALO, cw), lambda i: (jnp.maximum(i * hb - 1, 0), 2)),
         pl.BlockSpec(conv_w.shape, fixed2), pl.BlockSpec(conv_b.shape, fixed2),
         pl.BlockSpec(g_v.shape, fixed2), pl.BlockSpec(w_s.shape, lambda i: (0, 0, 0)),
         pl.BlockSpec(b_t.shape, fixed2)],
        [pl.BlockSpec((tm, cw + gw), lambda i: (i, 0))], [_sds((s, cw + gw), BF16)],
        [z, z, z, conv_w, conv_b, g_v, w_s, b_t], sem=("arbitrary",), jobs=jobs)
    return _ret(outs, job_outs, jobs)


def mixer_bwd(name, z, dy, conv_w, conv_b, g_v, w_s, b_t, tm=256, jobs=()):
    s, zc = z.shape
    cw = conv_w.shape[1]
    gw = g_v.shape[1]
    heads = gw // GROUP
    tm = _tile(s, tm)
    hb = tm // HALO
    nsteps = s // tm
    last_halo = s // HALO - 1

    def body(z_ref, hgc_ref, hhc_ref, ngb_ref, dy_ref, ndy_ref, cw_ref, cb_ref, gv_ref, ws_ref, bt_ref,
             dz_ref, sm_ref, dws_ref, dbt_ref, dsg_ref):
        i = pl.program_id(0)

        @pl.when(i == 0)
        def _():
            sm_ref[...] = jnp.zeros_like(sm_ref)
            dws_ref[...] = jnp.zeros_like(dws_ref)
            dsg_ref[...] = jnp.zeros_like(dsg_ref)

        gc, hc, cin, x1, x2 = _conv_inputs(z_ref, hgc_ref, hhc_ref, i, cw, tm)
        w0, w1, w2 = cw_ref[0:1, :], cw_ref[1:2, :], cw_ref[2:3, :]
        cv = cb_ref[...] + w2 * cin + w1 * x1 + w0 * x2
        gb = z_ref[:, 0:cw].astype(F32)
        dyc = dy_ref[:, 0:cw].astype(F32)
        dz_ref[:, 0:cw] = (dyc * cv).astype(BF16)
        dcv = dyc * gb
        nxt = ndy_ref[...].astype(F32) * ngb_ref[...].astype(F32) * (i < nsteps - 1).astype(F32)
        row = lax.broadcasted_iota(jnp.int32, (tm, cw), 0)
        d1 = jnp.where(row == tm - 1, nxt[0:1], pltpu.roll(dcv, tm - 1, 0))
        d2 = jnp.where(row == tm - 1, nxt[1:2], jnp.where(row == tm - 2, nxt[0:1], pltpu.roll(dcv, tm - 2, 0)))
        dcin = w2 * dcv + w1 * d1 + w0 * d2
        dz_ref[:, cw:2 * cw] = (dcin * hc).astype(BF16)
        dz_ref[:, 2 * cw:3 * cw] = (dcin * gc).astype(BF16)
        sm_ref[0:1, :] += jnp.sum(dcv * x2, axis=0, keepdims=True)
        sm_ref[1:2, :] += jnp.sum(dcv * x1, axis=0, keepdims=True)
        sm_ref[2:3, :] += jnp.sum(dcv * cin, axis=0, keepdims=True)
        sm_ref[3:4, :] += jnp.sum(dcv, axis=0, keepdims=True)

        for h in range(heads):
            lo = h * GROUP
            vcol = slice(3 * cw + gw + lo, 3 * cw + gw + lo + GROUP)
            ucol = slice(3 * cw + lo, 3 * cw + lo + GROUP)
            vh = z_ref[:, vcol].astype(F32)
            rv = lax.rsqrt(jnp.mean(vh * vh, axis=-1, keepdims=True) + EPS)
            xh = vh * rv
            gvh = gv_ref[:, lo:lo + GROUP]
            vn = (xh * gvh).astype(BF16)
            w = _tril(ws_ref[h]).astype(BF16)
            dgv = jnp.zeros((1, GROUP), F32)
            for n in range(tm // GROUP):
                rows = slice(n * GROUP, (n + 1) * GROUP)
                sg = _dot_nn(w, vn[rows]) + bt_ref[:, h:h + 1]
                dyg = dy_ref[rows, cw + lo:cw + lo + GROUP].astype(F32)
                dsg = dyg * z_ref[rows, ucol].astype(F32)
                dz_ref[rows, ucol] = (dyg * sg).astype(BF16)
                dsgb = dsg.astype(BF16)
                dvn = _dot_tn(w, dsgb)
                dws_ref[h] += _dot_nt(dsgb, vn[rows])
                dsg_ref[:, lo:lo + GROUP] += dsg
                xhc = xh[rows]
                dgv = dgv + jnp.sum(dvn * xhc, axis=0, keepdims=True)
                dxh = dvn * gvh
                dv = rv[rows] * (dxh - xhc * jnp.mean(dxh * xhc, axis=-1, keepdims=True))
                dz_ref[rows, vcol] = dv.astype(BF16)
            sm_ref[4:5, lo:lo + GROUP] += dgv

        @pl.when(i == nsteps - 1)
        def _():
            for h in range(heads):
                dws_ref[h] = _tril(dws_ref[h])
                dbt_ref[:, h:h + 1] = jnp.sum(dsg_ref[:, h * GROUP:(h + 1) * GROUP], axis=-1, keepdims=True)

    fixed2 = lambda i: (0, 0)
    fixed3 = lambda i: (0, 0, 0)
    prev = lambda col: (lambda i: (jnp.maximum(i * hb - 1, 0), col))
    nxt_blk = lambda i: (jnp.minimum((i + 1) * hb, last_halo), 0)
    outs, job_outs = _call(
        name, body, (nsteps,),
        [pl.BlockSpec((tm, zc), lambda i: (i, 0)),
         pl.BlockSpec((HALO, cw), prev(1)), pl.BlockSpec((HALO, cw), prev(2)),
         pl.BlockSpec((HALO, cw), nxt_blk),
         pl.BlockSpec((tm, cw + gw), lambda i: (i, 0)), pl.BlockSpec((HALO, cw), nxt_blk),
         pl.BlockSpec(conv_w.shape, fixed2), pl.BlockSpec(conv_b.shape, fixed2),
         pl.BlockSpec(g_v.shape, fixed2), pl.BlockSpec(w_s.shape, fixed3), pl.BlockSpec(b_t.shape, fixed2)],
        [pl.BlockSpec((tm, zc), lambda i: (i, 0)), pl.BlockSpec((8, cw), fixed2),
         pl.BlockSpec(w_s.shape, fixed3), pl.BlockSpec(b_t.shape, fixed2)],
        [_sds((s, zc), BF16), _sds((8, cw), F32), _sds(w_s.shape, F32), _sds(b_t.shape, F32)],
        [z, z, z, z, dy, dy, conv_w, conv_b, g_v, w_s, b_t],
        scratch=[pltpu.VMEM((GROUP, gw), F32)], sem=("arbitrary",), jobs=jobs)
    return _ret(outs, job_outs, jobs, single=False)


def _softmax_rows(sc):
    e = jnp.exp(sc - jnp.max(sc, axis=-1, keepdims=True))
    return e / jnp.sum(e, axis=-1, keepdims=True)


def attn_fwd(name, q, k, v, tm=512, jobs=()):
    s, d = q.shape
    m = k.shape[0]
    hd = d // XA_HEADS
    scale = hd ** -0.5
    tm = _tile(s, tm, 8)

    def body(q_ref, k_ref, v_ref, o_ref):
        for h in range(XA_HEADS):
            cols = slice(h * hd, (h + 1) * hd)
            p = _softmax_rows(_dot_nt(q_ref[:, cols], k_ref[:, cols]) * scale)
            o_ref[:, cols] = _dot_nn(p.astype(BF16), v_ref[:, cols]).astype(BF16)

    outs, job_outs = _call(
        name, body, (s // tm,),
        [pl.BlockSpec((tm, d), lambda i: (i, 0)), pl.BlockSpec((m, d), lambda i: (0, 0)),
         pl.BlockSpec((m, d), lambda i: (0, 0))],
        [pl.BlockSpec((tm, d), lambda i: (i, 0))], [_sds((s, d), BF16)], [q, k, v], sem=("arbitrary",), jobs=jobs)
    return _ret(outs, job_outs, jobs)


def attn_bwd(name, q, k, v, do, tm=512):
    s, d = q.shape
    m = k.shape[0]
    hd = d // XA_HEADS
    scale = hd ** -0.5
    tm = _tile(s, tm, 8)

    def body(q_ref, k_ref, v_ref, do_ref, dq_ref, dk_ref, dv_ref):
        i = pl.program_id(0)

        @pl.when(i == 0)
        def _():
            dk_ref[...] = jnp.zeros_like(dk_ref)
            dv_ref[...] = jnp.zeros_like(dv_ref)

        for h in range(XA_HEADS):
            cols = slice(h * hd, (h + 1) * hd)
            qh = q_ref[:, cols]
            doh = do_ref[:, cols]
            p = _softmax_rows(_dot_nt(qh, k_ref[:, cols]) * scale)
            dp = _dot_nt(doh, v_ref[:, cols])
            ds = (p * (dp - jnp.sum(dp * p, axis=-1, keepdims=True)) * scale).astype(BF16)
            dq_ref[:, cols] = _dot_nn(ds, k_ref[:, cols]).astype(BF16)
            dk_ref[:, cols] += _dot_tn(ds, qh)
            dv_ref[:, cols] += _dot_tn(p.astype(BF16), doh)

    row = lambda i: (i, 0)
    fixed = lambda i: (0, 0)
    return _call(
        name, body, (s // tm,),
        [pl.BlockSpec((tm, d), row), pl.BlockSpec((m, d), fixed), pl.BlockSpec((m, d), fixed),
         pl.BlockSpec((tm, d), row)],
        [pl.BlockSpec((tm, d), row), pl.BlockSpec((m, d), fixed), pl.BlockSpec((m, d), fixed)],
        [_sds((s, d), BF16), _sds((m, d), F32), _sds((m, d), F32)], [q, k, v, do], sem=("arbitrary",))[0]


def _grid2(rows, cols, row_mult):
    tr, tc = _tile(rows, 512, row_mult), _tile(cols, 2048)
    return tr, tc, rows // tr, cols // tc


def cast_place(name, block, axis, place):
    r, c = block.shape
    tr, tc, nbr, nbc = _grid2(r, c, 16)
    if axis == 1:
        dst = lambda i, j, p: (i, j + p[0] * nbc)
    else:
        dst = lambda i, j, p: (i + p[0] * nbr, j)

    def body(p_ref, w_ref, out_ref):
        out_ref[...] = w_ref[...].astype(BF16)

    return pl.pallas_call(
        body, name=name,
        grid_spec=pltpu.PrefetchScalarGridSpec(
            num_scalar_prefetch=1, grid=(nbr, nbc),
            in_specs=[pl.BlockSpec((tr, tc), lambda i, j, p: (i, j))],
            out_specs=pl.BlockSpec((tr, tc), dst)),
        out_shape=_sds(_full_shape(block.shape, axis), BF16),
        compiler_params=_params(("parallel", "parallel")),
    )(place, block)


def pair_add(name, grad, peer, axis, place):
    hr, hc = peer.shape
    tr, tc, nbr, nbc = _grid2(hr, hc, 16)
    same = lambda i, j, p: (i, j)
    if grad.shape == peer.shape:
        mine = same
    elif axis == 1:
        mine = lambda i, j, p: (i + p[1] * nbr, j)
    else:
        mine = lambda i, j, p: (i, j + p[1] * nbc)

    def body(p_ref, g_ref, q_ref, out_ref):
        out_ref[...] = (g_ref[...].astype(F32) + q_ref[...].astype(F32)).astype(BF16)

    return pl.pallas_call(
        body, name=name,
        grid_spec=pltpu.PrefetchScalarGridSpec(
            num_scalar_prefetch=1, grid=(nbr, nbc),
            in_specs=[pl.BlockSpec((tr, tc), mine), pl.BlockSpec((tr, tc), same)],
            out_specs=pl.BlockSpec((tr, tc), same)),
        out_shape=_sds((hr, hc), BF16),
        compiler_params=_params(("parallel", "parallel")),
    )(place, grad, peer)


def cross_sum(name, part, land, axis, shape, place):
    _, sr, sc = land.shape
    tr, tc, nbr, nbc = _grid2(sr, sc, 16)
    if axis == 1:
        own = lambda i, j, p: (i, j + p[0] * nbc)
        dst = lambda i, j, p: (i + p[1] * nbr, j)
    else:
        own = lambda i, j, p: (i + p[0] * nbr, j)
        dst = lambda i, j, p: (i, j + p[1] * nbc)

    def body(p_ref, own_ref, land_ref, out_ref):
        out_ref[...] = ((own_ref[...].astype(F32) + land_ref[0].astype(F32))
                        + (land_ref[1].astype(F32) + land_ref[2].astype(F32)))

    return pl.pallas_call(
        body, name=name,
        grid_spec=pltpu.PrefetchScalarGridSpec(
            num_scalar_prefetch=1, grid=(nbr, nbc),
            in_specs=[pl.BlockSpec((tr, tc), own), pl.BlockSpec((3, tr, tc), lambda i, j, p: (0, i, j))],
            out_specs=pl.BlockSpec((tr, tc), dst)),
        out_shape=_sds(_block(shape, axis), F32),
        compiler_params=_params(("parallel", "parallel")),
    )(place, part, land)


def _adam_math(w, g, m, v):
    m = ADAM_B1 * m + (1.0 - ADAM_B1) * g
    v = ADAM_B2 * v + (1.0 - ADAM_B2) * (g * g)
    m_hat = m / (1.0 - ADAM_B1 ** ADAM_STEP)
    v_hat = v / (1.0 - ADAM_B2 ** ADAM_STEP)
    delta = -ADAM_LR * (m_hat / (jnp.sqrt(v_hat) + ADAM_EPS) + ADAM_WD * w)
    return delta, m, v


def adamw(name, w, g, m, v, jobs=()):
    r, c = w.shape
    tr, tc = _tile(r, 256, 8), _tile(c, 1408)

    def body(w_ref, g_ref, m_ref, v_ref, g_out, d_out, m_out, v_out):
        d, mm, vv = _adam_math(w_ref[...], g_ref[...], m_ref[...], v_ref[...])
        g_out[...] = g_ref[...]
        d_out[...] = d
        m_out[...] = mm
        v_out[...] = vv

    spec = pl.BlockSpec((tr, tc), lambda i, j: (i, j))
    outs, job_outs = _call(name, body, (r // tr, c // tc), [spec] * 4, [spec] * 4, [_sds((r, c), F32)] * 4,
                           [w, g, m, v], sem=("parallel", "parallel"), jobs=jobs)
    return _ret(outs, job_outs, jobs, single=False)


def small_sum(name, stacks):
    def body(*refs):
        for s_ref, out_ref in zip(refs[:len(stacks)], refs[len(stacks):]):
            acc = s_ref[0]
            for d in range(1, s_ref.shape[0]):
                acc = acc + s_ref[d]
            out_ref[...] = acc

    return pl.pallas_call(body, name=name, out_shape=[_sds(s.shape[1:], F32) for s in stacks])(*stacks)


WEIGHTS = ["g_ffn1", "w_ffn1_in", "w_ffn1_out", "g_mix", "w_mix_in", "conv_w", "conv_b", "g_gm_v", "w_spatial",
           "b_spatial", "w_mix_out", "g_xattn", "g_mem", "w_xq", "w_xk", "w_xv", "w_xo", "g_ffn2", "w_ffn2_in",
           "w_ffn2_out", "g_final"]
BIG = {"w_ffn1_in": 1, "w_ffn1_out": 0, "w_mix_in": 1, "w_mix_out": 0, "w_xq": 0, "w_xk": 0, "w_xv": 0, "w_xo": 0,
       "w_ffn2_in": 1, "w_ffn2_out": 0}
SMALL = [n for n in WEIGHTS if n not in BIG]
LATE_SMALL = ["g_ffn1"]
EARLY_SMALL = [n for n in SMALL if n not in LATE_SMALL]


def _pack(arrays):
    flat = jnp.concatenate([a.reshape(-1) for a in arrays])
    rows = -(-flat.shape[0] // 1024) * 8
    return jnp.pad(flat, (0, rows * 128 - flat.shape[0])).reshape(rows, 128)


def _unpack(buf, shapes):
    flat = buf.reshape(-1)
    out, pos = [], 0
    for shp in shapes:
        n = math.prod(shp)
        out.append(flat[pos:pos + n].reshape(shp))
        pos += n
    return out


def kernel(x, mem, g_ffn1, w_ffn1_in, w_ffn1_out, g_mix, w_mix_in, conv_w, conv_b, g_gm_v, w_spatial, b_spatial, w_mix_out, g_xattn, g_mem, w_xq, w_xk, w_xv, w_xo, g_ffn2, w_ffn2_in, w_ffn2_out, g_final, loss_target, m_g_ffn1, m_w_ffn1_in, m_w_ffn1_out, m_g_mix, m_w_mix_in, m_conv_w, m_conv_b, m_g_gm_v, m_w_spatial, m_b_spatial, m_w_mix_out, m_g_xattn, m_g_mem, m_w_xq, m_w_xk, m_w_xv, m_w_xo, m_g_ffn2, m_w_ffn2_in, m_w_ffn2_out, m_g_final, v_g_ffn1, v_w_ffn1_in, v_w_ffn1_out, v_g_mix, v_w_mix_in, v_conv_w, v_conv_b, v_g_gm_v, v_w_spatial, v_b_spatial, v_w_mix_out, v_g_xattn, v_g_mem, v_w_xq, v_w_xk, v_w_xv, v_w_xo, v_g_ffn2, v_w_ffn2_in, v_w_ffn2_out, v_g_final):
    given = dict(locals())
    wts = {n: given[n] for n in WEIGHTS}
    mom = {n: given["m_" + n] for n in WEIGHTS}
    var = {n: given["v_" + n] for n in WEIGHTS}

    xi, yi, ci = lax.axis_index("x"), lax.axis_index("y"), lax.axis_index("c")
    blk = 2 * xi + yi
    place = jnp.stack([blk, ci]).astype(jnp.int32)

    x2, mem2, tgt = x[0], mem[0], loss_target[0]
    own = {n: cast_place("cast_" + n, wts[n][0], BIG[n], place) for n in BIG}
    shape = {n: own[n].shape for n in BIG}
    w_s, b_t = w_spatial[0], b_spatial[0].T
    gf = g_final[None]

    def gather(*names):
        return gather_job([(own[n], BIG[n], WHOLE, WHOLE) for n in names])

    def gather_part(arr, sub=WHOLE, within=WHOLE):
        return gather_job([(arr, 1, sub, within)])

    full = {}
    left, right = (0, 1, 2), (1, 1, 2)
    half_cols = dict(tm=512, tn=shape["w_ffn1_in"][1] // (2 * N_CHIPS), stride=2)

    (w1in,), (conv_taps,) = comm_only(
        "gather_first", [gather_part(own["w_ffn1_in"], within=left),
                         columns_job(jnp.pad(conv_w[0], ((0, 8 - CONV_K), (0, 0))))])
    n1, r1 = rmsnorm_fwd("norm1", x2, g_ffn1)
    halves, ((w1in,),) = swiglu_fwd("ffn1_in_left", n1, w1in, phase=0, jobs=[gather_part(w1in, within=right)],
                                    **half_cols)
    full["w_ffn1_in"] = w1in
    (gu1, a1), ((full["w_ffn1_out"],),) = swiglu_fwd("ffn1_in_right", n1, w1in, phase=1, prev=halves,
                                                     jobs=[gather("w_ffn1_out")], **half_cols)
    h1, ((full["w_mix_in"], full["w_mix_out"]),) = mm_nn_resid(
        "ffn1_out", a1, full["w_ffn1_out"], x2, 0.5, tm=512, tk=5632, jobs=[gather("w_mix_in", "w_mix_out")])
    n2, r2 = rmsnorm_fwd("norm2", h1, g_mix)
    z, ((full["w_xq"], full["w_xk"], full["w_xv"]),) = mm_nn("mix_in", n2, full["w_mix_in"], BF16,
                                                             jobs=[gather("w_xq", "w_xk", "w_xv")])
    w2in = own["w_ffn2_in"]
    ycat, ((full["w_xo"],),) = mixer_fwd("mixer", z, conv_taps, conv_b, g_gm_v, w_s, b_t, jobs=[gather("w_xo")])
    h2, ((w2in,),) = mm_nn_resid("mix_out", ycat, full["w_mix_out"], h1, 1.0, tk=2048,
                                 jobs=[gather_part(w2in, (0, 2, 8))])
    n3, r3 = rmsnorm_fwd("norm3", h2, g_xattn)
    mn, rm = rmsnorm_fwd("norm_mem", mem2, g_mem)
    q, ((w2in,),) = mm_nn("xq", n3, full["w_xq"], BF16, jobs=[gather_part(w2in, (2, 2, 8))])
    k = mm_nn("xk", mn, full["w_xk"], BF16)
    v = mm_nn("xv", mn, full["w_xv"], BF16)
    o, ((w2in,),) = attn_fwd("attn", q, k, v, jobs=[gather_part(w2in, (4, 1, 8))])
    h3, ((w2in,),) = mm_nn_resid("xo", o, full["w_xo"], h2, 1.0, tk=2048, jobs=[gather_part(w2in, (5, 3, 8))])
    full["w_ffn2_in"] = w2in
    n4, r4 = rmsnorm_fwd("norm4", h3, g_ffn2)
    (gu2, a2), ((full["w_ffn2_out"],),) = swiglu_fwd("ffn2_in", n4, full["w_ffn2_in"], jobs=[gather("w_ffn2_out")])
    h4 = mm_nn_resid("ffn2_out", a2, full["w_ffn2_out"], h3, 0.5, tm=512, tk=5632)
    loss_blk, dh4, dh4b, dg_final = loss_head("loss_head", h4, gf, tgt)

    dw, peer, part, land, half, grads = {}, {}, {}, {}, {}, {}

    def send_pair(*names):
        return pair_job([dw[n] for n in names], [BIG[n] for n in names])

    def take_pair(names, got):
        for n, p in zip(names, got):
            part[n] = pair_add("pair_add_" + n, dw[n], p, BIG[n], place)

    def send_cross(*names, sub=WHOLE):
        return cross_job([(part[n], BIG[n], shape[n], land.get(n), sub) for n in names])

    def take_cross(names, got, last=True):
        for n, l in zip(names, got):
            land[n] = l
            if last:
                half[n] = cross_sum("cross_sum_" + n, part[n], l, BIG[n], shape[n], place)

    def send_final(*names):
        return final_job([half[n] for n in names], [BIG[n] for n in names], [shape[n] for n in names])

    delta, new_m, new_v = {}, {}, {}

    reduced = {}

    def take_final(names, got):
        for n, g in zip(names, got):
            reduced[n] = g

    def update(n, jobs=()):
        res = adamw("adamw_" + n, wts[n][0], reduced[n], mom[n][0], var[n][0], jobs=jobs)
        (grads[n], delta[n], new_m[n], new_v[n]), job_outs = res if jobs else (res, [])
        return job_outs

    dgu2 = swiglu_bwd("ffn2_dact", dh4b, full["w_ffn2_out"], gu2, 0.5)
    dw["w_ffn2_in"] = mm_tn_pair("ffn2_dwin", n4, dgu2, BF16)
    dw["w_ffn2_out"], (got,) = mm_tn("ffn2_dwout", a2, dh4b, BF16, scale=0.5, jobs=[send_pair("w_ffn2_in")])
    take_pair(["w_ffn2_in"], got)
    (dh3, dh3b, dg_ffn2), (got_c, got_p) = mm_nt_norm_bwd(
        "ffn2_dn", dgu2, full["w_ffn2_in"], h3, r4, g_ffn2, dh4,
        jobs=[send_cross("w_ffn2_in"), send_pair("w_ffn2_out")])
    take_cross(["w_ffn2_in"], got_c)
    take_pair(["w_ffn2_out"], got_p)

    dw["w_xo"], (got_c,) = mm_tn("xo_dw", o, dh3b, BF16, jobs=[send_cross("w_ffn2_out", sub=(0, 2, 8))])
    take_cross(["w_ffn2_out"], got_c, last=False)
    do, (got_c, got_f) = mm_nt("xo_dx", dh3b, full["w_xo"], BF16,
                               jobs=[send_cross("w_ffn2_out", sub=(2, 2, 8)), send_final("w_ffn2_in")])
    take_cross(["w_ffn2_out"], got_c, last=False)
    take_final(["w_ffn2_in"], got_f)
    update("w_ffn2_in")
    dq, dk, dv = attn_bwd("attn_bwd", q, k, v, do)
    dkb, dvb = dk.astype(BF16), dv.astype(BF16)
    dw["w_xq"], (got_c,) = mm_tn("xq_dw", n3, dq, BF16, jobs=[send_cross("w_ffn2_out", sub=(4, 2, 8))])
    take_cross(["w_ffn2_out"], got_c, last=False)
    (dh2, dh2b, dg_xattn), (got_c,) = mm_nt_norm_bwd(
        "xq_dx", dq, full["w_xq"], h2, r3, g_xattn, dh3, tk=1024, jobs=[send_cross("w_ffn2_out", sub=(6, 2, 8))])
    take_cross(["w_ffn2_out"], got_c)
    dw["w_xk"] = mm_tn("xk_dw", mn, dkb, BF16)
    dw["w_xv"] = mm_tn("xv_dw", mn, dvb, BF16)
    dmn_k = mm_nt("xk_dx", dkb, full["w_xk"], F32)
    dmn_v = mm_nt("xv_dx", dvb, full["w_xv"], F32)
    dg_mem = gain_grad("norm_mem_bwd", dmn_k, dmn_v, mem2, rm)

    dw["w_mix_out"], (got_f,) = mm_tn("mix_out_dw", ycat, dh2b, BF16, jobs=[send_final("w_ffn2_out")])
    take_final(["w_ffn2_out"], got_f)
    update("w_ffn2_out")
    attn_names = ["w_xo", "w_xq", "w_xk", "w_xv", "w_mix_out"]
    dycat, (got_p,) = mm_nt("mix_out_dx", dh2b, full["w_mix_out"], BF16, jobs=[send_pair(*attn_names)])
    take_pair(attn_names, got_p)
    (dz, dsmall, dws, dbt), (got_c,) = mixer_bwd("mixer_bwd", z, dycat, conv_taps, conv_b, g_gm_v, w_s, b_t,
                                                 jobs=[send_cross("w_xo", "w_xq")])
    take_cross(["w_xo", "w_xq"], got_c)
    dw["w_mix_in"], (got_c,) = mm_tn("mix_in_dw", n2, dz, BF16, jobs=[send_cross("w_xk", "w_xv")])
    take_cross(["w_xk", "w_xv"], got_c)
    (dh1, dh1b, dg_mix), (got_c, got_p) = mm_nt_norm_bwd(
        "mix_in_dx", dz, full["w_mix_in"], h1, r2, g_mix, dh2, tk=1280,
        jobs=[send_cross("w_mix_out"), send_pair("w_mix_in")])
    take_cross(["w_mix_out"], got_c)
    take_pair(["w_mix_in"], got_p)

    dw["w_ffn1_out"], (got_c, got_f) = mm_tn("ffn1_dwout", a1, dh1b, BF16, scale=0.5,
                                             jobs=[send_cross("w_mix_in"), send_final(*attn_names)])
    take_cross(["w_mix_in"], got_c)
    take_final(attn_names, got_f)
    for n in attn_names:
        update(n)
    early = {"g_mix": dg_mix, "conv_w": dsmall[0:CONV_K], "conv_b": dsmall[3:4], "g_gm_v": dsmall[4:5],
             "w_spatial": dws, "b_spatial": dbt.T, "g_xattn": dg_xattn, "g_mem": dg_mem, "g_ffn2": dg_ffn2,
             "g_final": dg_final}
    dgu1, (got_p, got_f, (early_all,)) = swiglu_bwd(
        "ffn1_dact", dh1b, full["w_ffn1_out"], gu1, 0.5,
        jobs=[send_pair("w_ffn1_out"), send_final("w_mix_in"), stack_job(_pack([early[n] for n in EARLY_SMALL]))])
    take_pair(["w_ffn1_out"], got_p)
    take_final(["w_mix_in"], got_f)
    update("w_mix_in")
    theirs, (got_c,) = mm_tn_pair_half("ffn1_dwin_theirs", n1, dgu1, BF16, place, False,
                                       jobs=[send_cross("w_ffn1_out", sub=(0, 7, 8))])
    take_cross(["w_ffn1_out"], got_c, last=False)
    mine, (got_c, (from_sibling,)) = mm_tn_pair_half(
        "ffn1_dwin_mine", n1, dgu1, BF16, place, True,
        jobs=[send_cross("w_ffn1_out", sub=(7, 1, 8)), pair_job([theirs], [1], is_half=True)])
    take_cross(["w_ffn1_out"], got_c)
    part["w_ffn1_in"] = pair_add("pair_add_w_ffn1_in", mine, from_sibling, 1, place)
    dn1, (got_c, got_f) = mm_nt_pair("ffn1_dn", dgu1, full["w_ffn1_in"], F32,
                                     jobs=[send_cross("w_ffn1_in", sub=(0, 7, 8)), send_final("w_ffn1_out")])
    take_cross(["w_ffn1_in"], got_c, last=False)
    take_final(["w_ffn1_out"], got_f)
    update("w_ffn1_out")
    dx, _, dg_ffn1 = rmsnorm_bwd("norm1_bwd", dn1, x2, r1, g_ffn1, dh1)
    got_c, (late_all,) = comm_only("tail_cross", [send_cross("w_ffn1_in", sub=(7, 1, 8)),
                                                  stack_job(_pack([dg_ffn1]))])
    take_cross(["w_ffn1_in"], got_c)
    (got_f,) = comm_only("tail_final", [send_final("w_ffn1_in")])
    take_final(["w_ffn1_in"], got_f)
    update("w_ffn1_in")

    early_sum, late_sum = small_sum("small_sum", [early_all, late_all])
    for n, g in zip(EARLY_SMALL, _unpack(early_sum, [early[n].shape for n in EARLY_SMALL])):
        grads[n] = g
    grads["g_ffn1"] = _unpack(late_sum, [dg_ffn1.shape])[0]
    taps_cols = conv_w.shape[2]
    grads["conv_w"] = lax.dynamic_slice_in_dim(grads["conv_w"], blk * taps_cols, taps_cols, axis=1)
    packed = [_pack([src[n] for n in SMALL]) for src in (wts, grads, mom, var)]
    own_shapes = [wts[n].shape for n in SMALL]
    for dst, buf in zip((delta, new_m, new_v), adamw("adamw_small", *packed)[1:]):
        for n, a in zip(SMALL, _unpack(buf, own_shapes)):
            dst[n] = a

    loss = lax.psum(loss_blk[0, 0], ("x", "y", "c"))
    outs = [loss, dx[None]]
    for group in (grads, delta, new_m, new_v):
        outs += [group[n].reshape(wts[n].shape) for n in WEIGHTS]
    return tuple(outs)
```

```python
import math

import jax
import jax.numpy as jnp
from jax import lax
from jax.experimental import pallas as pl
from jax.experimental.pallas import tpu as pltpu

F32 = jnp.float32
BF16 = jnp.bfloat16
EPS = 1e-6
GROUP = 128
XA_HEADS = 4
CONV_K = 3
N_CHIPS = 4
VMEM_LIMIT_BYTES = 56 * 1024 * 1024

ADAM_LR = 0.001
ADAM_B1 = 0.9
ADAM_B2 = 0.999
ADAM_EPS = 1e-08
ADAM_WD = 0.01
ADAM_STEP = 10

MESH = pl.DeviceIdType.MESH
ANY = pl.BlockSpec(memory_space=pl.ANY)


def _tile(dim, pref, mult=128):
    if dim <= pref:
        return dim
    t = (pref // mult) * mult
    while t >= mult:
        if dim % t == 0:
            return t
        t -= mult
    raise ValueError(f"no tile for {dim} under {pref}")


def _params(sem):
    return pltpu.CompilerParams(dimension_semantics=sem, vmem_limit_bytes=VMEM_LIMIT_BYTES)


def _sds(shape, dtype):
    return jax.ShapeDtypeStruct(shape, dtype)


def _dot_nn(a, b):
    return jnp.dot(a, b, preferred_element_type=F32)


def _dot_nt(a, b):
    return lax.dot_general(a, b, (((1,), (1,)), ((), ())), preferred_element_type=F32)


def _dot_tn(a, b):
    return lax.dot_general(a, b, (((0,), (0,)), ((), ())), preferred_element_type=F32)


class Job:
    def __init__(self, inputs, out_shapes, aliases, sems, start, middle, finish):
        self.inputs, self.out_shapes, self.aliases, self.sems = inputs, out_shapes, aliases, sems
        self.start, self.middle, self.finish = start, middle, finish


def _place():
    x, y, c = lax.axis_index("x"), lax.axis_index("y"), lax.axis_index("c")
    chips = [(1 - x, y), (x, 1 - y), (1 - x, 1 - y)]
    return x, y, c, chips


def _ds(start, size, lane):
    if not isinstance(start, int):
        start = pl.multiple_of(start, 128 if lane else 16)
    return pl.ds(start, size)


WHOLE = (0, 1, 1)


def _window(ref, axis, shape, blk=None, half=None, sub=WHOLE, within=WHOLE):
    n = shape[axis] // N_CHIPS
    hs = shape[1 - axis] // 2
    idx = [slice(None), slice(None)]
    if blk is not None:
        b_first, b_count, b_pieces = within
        b_ext = n // b_pieces
        idx[axis] = _ds(blk * n + b_first * b_ext, b_count * b_ext, axis == 1)
    first, count, pieces = sub
    ext = hs // pieces
    if half is not None:
        idx[1 - axis] = _ds(half * hs + first * ext, count * ext, axis == 0)
    elif pieces > 1:
        idx[1 - axis] = _ds(first * ext, count * ext, axis == 0)
    return ref.at[tuple(idx)]


def _remote(src, dst, send_sem, recv_sem, dev):
    return pltpu.make_async_remote_copy(src_ref=src, dst_ref=dst, send_sem=send_sem, recv_sem=recv_sem,
                                        device_id=dev, device_id_type=MESH)


def _full_shape(block_shape, axis):
    out = list(block_shape)
    out[axis] *= N_CHIPS
    return tuple(out)


def _half_all(shape, axis):
    out = list(shape)
    out[1 - axis] //= 2
    return tuple(out)


def _block(shape, axis):
    out = list(shape)
    out[axis] //= N_CHIPS
    return tuple(out)


def _half_block(shape, axis):
    return _half_all(_block(shape, axis), axis)


def gather_job(items):
    nw = len(items)
    shapes = [item[0].shape for item in items]
    n_sem = 8

    def parts(sub):
        first, count, pieces = sub
        return (2 * first, count, 2 * pieces), (2 * first + count, count, 2 * pieces)

    def start(pos, ins, outs, sems):
        x, y, c, chips = pos
        for w, (_, ax, sub, within) in enumerate(items):
            mine = _window(outs[w], ax, shapes[w], blk=2 * x + y, half=c, sub=sub, within=within)
            for j in range(2):
                _remote(mine, mine, sems[0].at[n_sem * w + j], sems[1].at[n_sem * w + j], (*chips[j], c)).start()

    def middle(pos, ins, outs, sems):
        x, y, c, chips = pos
        for w, (_, ax, sub, within) in enumerate(items):
            for j in range(2):
                cx, cy = chips[j]
                landed = _window(outs[w], ax, shapes[w], blk=2 * cx + cy, half=c, sub=sub, within=within)
                _remote(landed, landed, sems[0].at[n_sem * w + j], sems[1].at[n_sem * w + j], (cx, cy, c)).wait_recv()
                part = _window(outs[w], ax, shapes[w], blk=2 * cx + cy, half=c, sub=parts(sub)[j], within=within)
                _remote(part, part, sems[0].at[n_sem * w + 2 + j], sems[1].at[n_sem * w + 2 + j],
                        (*chips[1 - j], c)).start()
                _remote(landed, landed, sems[0].at[n_sem * w + 4 + j], sems[1].at[n_sem * w + 4 + j],
                        (x, y, 1 - c)).start()

    def finish(pos, ins, outs, sems):
        x, y, c, chips = pos
        sib = (x, y, 1 - c)
        for w, (_, ax, sub, within) in enumerate(items):
            dx, dy = chips[2]
            for j in range(2):
                part = _window(outs[w], ax, shapes[w], blk=2 * dx + dy, half=c, sub=parts(sub)[j], within=within)
                cp = _remote(part, part, sems[0].at[n_sem * w + 2 + j], sems[1].at[n_sem * w + 2 + j], sib)
                cp.wait_recv()
                cp.wait_send()
            diag = _window(outs[w], ax, shapes[w], blk=2 * dx + dy, half=c, sub=sub, within=within)
            _remote(diag, diag, sems[0].at[n_sem * w + 6], sems[1].at[n_sem * w + 6], sib).start()
        for w, (_, ax, sub, within) in enumerate(items):
            for j, (cx, cy) in enumerate(chips):
                passed = _window(outs[w], ax, shapes[w], blk=2 * cx + cy, half=1 - c, sub=sub, within=within)
                cp = _remote(passed, passed, sems[0].at[n_sem * w + 4 + j], sems[1].at[n_sem * w + 4 + j], sib)
                cp.wait_recv()
                cp.wait_send()
            mine = _window(outs[w], ax, shapes[w], blk=2 * x + y, half=c, sub=sub, within=within)
            for j in range(2):
                _remote(mine, mine, sems[0].at[n_sem * w + j], sems[1].at[n_sem * w + j], sib).wait_send()

    sems = [pltpu.SemaphoreType.DMA((n_sem * nw,)), pltpu.SemaphoreType.DMA((n_sem * nw,))]
    return Job([item[0] for item in items], [_sds(item[0].shape, item[0].dtype) for item in items],
               {w: w for w in range(nw)}, sems, start, middle, finish)


def pair_job(grads, axes, is_half=False):
    nw = len(grads)
    shapes = [g.shape for g in grads]

    def start(pos, ins, outs, sems):
        x, y, c, _ = pos
        for w in range(nw):
            src = ins[w] if is_half else _window(ins[w], axes[w], shapes[w], half=1 - c)
            _remote(src, outs[w], sems[0].at[w], sems[1].at[w], (x, y, 1 - c)).start()

    def finish(pos, ins, outs, sems):
        x, y, c, _ = pos
        for w in range(nw):
            cp = _remote(outs[w], outs[w], sems[0].at[w], sems[1].at[w], (x, y, 1 - c))
            cp.wait_recv()
            cp.wait_send()

    sems = [pltpu.SemaphoreType.DMA((nw,)), pltpu.SemaphoreType.DMA((nw,))]
    out_shapes = [_sds(s if is_half else _half_all(s, a), BF16) for s, a in zip(shapes, axes)]
    return Job(list(grads), out_shapes, {}, sems, start, None, finish)


def cross_job(items):
    nw = len(items)
    inputs, aliases = [], {}
    for w, (part, ax, shape, prev, sub) in enumerate(items):
        inputs.append(part)
        if prev is not None:
            aliases[len(inputs)] = w
            inputs.append(prev)

    def copies(pos, ins, outs, sems):
        x, y, c, chips = pos
        k = 0
        for w, (_, ax, shape, prev, sub) in enumerate(items):
            src = ins[k]
            k += 2 if prev is not None else 1
            for j, (cx, cy) in enumerate(chips):
                slot = _window(outs[w].at[j], ax, shape, sub=sub)
                yield (_remote(_window(src, ax, shape, blk=2 * cx + cy, sub=sub), slot,
                               sems[0].at[3 * w + j], sems[1].at[3 * w + j], (cx, cy, c)),
                       _remote(slot, slot, sems[0].at[3 * w + j], sems[1].at[3 * w + j], (cx, cy, c)))

    def start(pos, ins, outs, sems):
        for send, _ in copies(pos, ins, outs, sems):
            send.start()

    def finish(pos, ins, outs, sems):
        for send, recv in copies(pos, ins, outs, sems):
            recv.wait_recv()
            send.wait_send()

    sems = [pltpu.SemaphoreType.DMA((3 * nw,)), pltpu.SemaphoreType.DMA((3 * nw,))]
    out_shapes = [_sds((3,) + _half_block(shape, ax), BF16) for _, ax, shape, _, _ in items]
    return Job(inputs, out_shapes, aliases, sems, start, None, finish)


def final_job(blocks, axes, shapes):
    nw = len(blocks)

    def start(pos, ins, outs, sems):
        x, y, c, _ = pos
        for w in range(nw):
            mine = _window(outs[w], axes[w], shapes[w], half=c)
            _remote(mine, mine, sems[0].at[w], sems[1].at[w], (x, y, 1 - c)).start()

    def finish(pos, ins, outs, sems):
        x, y, c, _ = pos
        for w in range(nw):
            theirs = _window(outs[w], axes[w], shapes[w], half=1 - c)
            cp = _remote(theirs, theirs, sems[0].at[w], sems[1].at[w], (x, y, 1 - c))
            cp.wait_recv()
            cp.wait_send()

    sems = [pltpu.SemaphoreType.DMA((nw,)), pltpu.SemaphoreType.DMA((nw,))]
    return Job(list(blocks), [_sds(b.shape, b.dtype) for b in blocks], {w: w for w in range(nw)}, sems, start, None,
               finish)


def stack_job(small):
    def peers(pos):
        x, y, c, _ = pos
        for k in range(1, 8):
            yield k - 1, (1 - x if k & 4 else x, 1 - y if k & 2 else y, 1 - c if k & 1 else c)

    def start(pos, ins, outs, sems):
        x, y, c, _ = pos
        mine = outs[0].at[4 * x + 2 * y + c]
        pltpu.make_async_copy(ins[0], mine, sems[2]).start()
        for k, dev in peers(pos):
            _remote(ins[0], mine, sems[0].at[k], sems[1].at[k], dev).start()

    def finish(pos, ins, outs, sems):
        x, y, c, _ = pos
        for k, (px, py, pc) in peers(pos):
            slot = outs[0].at[4 * px + 2 * py + pc]
            cp = _remote(slot, slot, sems[0].at[k], sems[1].at[k], (px, py, pc))
            cp.wait_recv()
            cp.wait_send()
        pltpu.make_async_copy(ins[0], outs[0].at[4 * x + 2 * y + c], sems[2]).wait()

    sems = [pltpu.SemaphoreType.DMA((7,)), pltpu.SemaphoreType.DMA((7,)), pltpu.SemaphoreType.DMA]
    return Job([small], [_sds((8,) + small.shape, small.dtype)], {}, sems, start, None, finish)


def columns_job(block):
    cols = block.shape[1]
    place = lambda out, b: out.at[:, _ds(b * cols, cols, True)]

    def start(pos, ins, outs, sems):
        x, y, c, chips = pos
        pltpu.make_async_copy(ins[0], place(outs[0], 2 * x + y), sems[2]).start()
        for j, (cx, cy) in enumerate(chips):
            _remote(ins[0], place(outs[0], 2 * x + y), sems[0].at[j], sems[1].at[j], (cx, cy, c)).start()

    def finish(pos, ins, outs, sems):
        x, y, c, chips = pos
        for j, (cx, cy) in enumerate(chips):
            got = place(outs[0], 2 * cx + cy)
            cp = _remote(got, got, sems[0].at[j], sems[1].at[j], (cx, cy, c))
            cp.wait_recv()
            cp.wait_send()
        pltpu.make_async_copy(ins[0], place(outs[0], 2 * x + y), sems[2]).wait()

    sems = [pltpu.SemaphoreType.DMA((3,)), pltpu.SemaphoreType.DMA((3,)), pltpu.SemaphoreType.DMA]
    return Job([block], [_sds((block.shape[0], N_CHIPS * cols), block.dtype)], {}, sems, start, None, finish)


def _call(name, body, grid, in_specs, out_specs, out_shape, args, scratch=(), sem=None, jobs=(), place=None,
          carried=None):
    n_in, n_out, n_sc = len(args), len(out_shape), len(scratch)
    carried = dict(carried or {})

    def launch(fn, in_specs, out_specs, out_shape, scratch, aliases, sem, operands):
        if place is None:
            return pl.pallas_call(
                fn, name=name, grid=grid, in_specs=in_specs, out_specs=out_specs, out_shape=out_shape,
                scratch_shapes=scratch, input_output_aliases=aliases, compiler_params=_params(sem))(*operands)
        spec = pltpu.PrefetchScalarGridSpec(num_scalar_prefetch=1, grid=grid, in_specs=in_specs,
                                            out_specs=out_specs, scratch_shapes=scratch)
        return pl.pallas_call(
            lambda p_ref, *refs: fn(*refs), name=name, grid_spec=spec, out_shape=out_shape,
            input_output_aliases={k + 1: v for k, v in aliases.items()}, compiler_params=_params(sem),
        )(place, *operands)

    if not jobs:
        outs = launch(body, list(in_specs), list(out_specs), list(out_shape), list(scratch), carried, sem, args)
        return list(outs), []

    total = math.prod(grid) if grid else 1
    mid = min(total - 1, (2 * total) // 3)

    def split(refs, start, counts):
        out = []
        for n in counts:
            out.append(refs[start:start + n])
            start += n
        return out, start

    def wrapped(*refs):
        c_in = refs[:n_in]
        j_ins, p = split(refs, n_in, [len(j.inputs) for j in jobs])
        c_out = refs[p:p + n_out]
        j_outs, p = split(refs, p + n_out, [len(j.out_shapes) for j in jobs])
        c_sc = refs[p:p + n_sc]
        j_sems, p = split(refs, p + n_sc, [len(j.sems) for j in jobs])
        pos = _place()
        step = 0
        for axis, extent in enumerate(grid):
            step = step * extent + pl.program_id(axis)

        def run(phase):
            for j, ins, outs, sems in zip(jobs, j_ins, j_outs, j_sems):
                fn = getattr(j, phase)
                if fn is not None:
                    fn(pos, ins, outs, sems)

        if total == 1:
            run("start")
            body(*c_in, *c_out, *c_sc)
            run("middle")
            run("finish")
            return
        pl.when(step == 0)(lambda: run("start"))
        body(*c_in, *c_out, *c_sc)
        if any(j.middle is not None for j in jobs):
            pl.when(step == mid)(lambda: run("middle"))
        pl.when(step == total - 1)(lambda: run("finish"))

    aliases, in_at, out_at = carried, n_in, n_out
    for j in jobs:
        for src, dst in j.aliases.items():
            aliases[in_at + src] = out_at + dst
        in_at += len(j.inputs)
        out_at += len(j.out_shapes)
    outs = launch(
        wrapped, list(in_specs) + [ANY] * (in_at - n_in), list(out_specs) + [ANY] * (out_at - n_out),
        list(out_shape) + [s for j in jobs for s in j.out_shapes],
        list(scratch) + [s for j in jobs for s in j.sems], aliases, ("arbitrary",) * len(grid),
        [*args, *[a for j in jobs for a in j.inputs]])
    job_outs, p = split(outs, n_out, [len(j.out_shapes) for j in jobs])
    return list(outs[:n_out]), [list(o) for o in job_outs]


def comm_only(name, jobs):
    def body(dummy_ref, out_ref):
        out_ref[...] = dummy_ref[...]

    dummy = jnp.zeros((8, 128), F32)
    spec = pl.BlockSpec((8, 128), lambda: (0, 0))
    return _call(name, body, (), [spec], [spec], [_sds((8, 128), F32)], [dummy], jobs=jobs)[1]


def _ret(outs, job_outs, jobs, single=True):
    res = outs[0] if single else outs
    return (res, job_outs) if jobs else res


def rmsnorm_fwd(name, x, g, jobs=()):
    s, d = x.shape
    tm = _tile(s, 512, 8)

    def body(x_ref, g_ref, n_ref, r_ref):
        xv = x_ref[...]
        r = lax.rsqrt(jnp.mean(xv * xv, axis=-1, keepdims=True) + EPS)
        n_ref[...] = (xv * r * g_ref[...]).astype(BF16)
        r_ref[...] = r

    row = lambda i: (i, 0)
    outs, job_outs = _call(
        name, body, (s // tm,),
        [pl.BlockSpec((tm, d), row), pl.BlockSpec((1, d), lambda i: (0, 0))],
        [pl.BlockSpec((tm, d), row), pl.BlockSpec((tm, 1), row)],
        [_sds((s, d), BF16), _sds((s, 1), F32)], [x, g], sem=("arbitrary",), jobs=jobs)
    return _ret(outs, job_outs, jobs, single=False)


def rmsnorm_bwd(name, dn, x, r, g, dh_in, jobs=()):
    s, d = x.shape
    tm = _tile(s, 512, 8)

    def body(dn_ref, x_ref, r_ref, g_ref, dh_ref, out_ref, outb_ref, dg_ref):
        i = pl.program_id(0)
        xh = x_ref[...] * r_ref[...]
        dnv = dn_ref[...]
        dxh = dnv * g_ref[...]
        dx = r_ref[...] * (dxh - xh * jnp.mean(dxh * xh, axis=-1, keepdims=True))
        out = dh_ref[...] + dx
        out_ref[...] = out
        outb_ref[...] = out.astype(BF16)
        part = jnp.sum(dnv * xh, axis=0, keepdims=True)

        @pl.when(i == 0)
        def _():
            dg_ref[...] = part

        @pl.when(i > 0)
        def _():
            dg_ref[...] += part

    row = lambda i: (i, 0)
    fixed = lambda i: (0, 0)
    outs, job_outs = _call(
        name, body, (s // tm,),
        [pl.BlockSpec((tm, d), row), pl.BlockSpec((tm, d), row), pl.BlockSpec((tm, 1), row),
         pl.BlockSpec((1, d), fixed), pl.BlockSpec((tm, d), row)],
        [pl.BlockSpec((tm, d), row), pl.BlockSpec((tm, d), row), pl.BlockSpec((1, d), fixed)],
        [_sds((s, d), F32), _sds((s, d), BF16), _sds((1, d), F32)], [dn, x, r, g, dh_in],
        sem=("arbitrary",), jobs=jobs)
    return _ret(outs, job_outs, jobs, single=False)


def gain_grad(name, dn_a, dn_b, x, r):
    s, d = x.shape
    tm = _tile(s, 512, 8)

    def body(a_ref, b_ref, x_ref, r_ref, dg_ref):
        i = pl.program_id(0)
        part = jnp.sum((a_ref[...] + b_ref[...]) * (x_ref[...] * r_ref[...]), axis=0, keepdims=True)

        @pl.when(i == 0)
        def _():
            dg_ref[...] = part

        @pl.when(i > 0)
        def _():
            dg_ref[...] += part

    row = lambda i: (i, 0)
    return _call(
        name, body, (s // tm,),
        [pl.BlockSpec((tm, d), row), pl.BlockSpec((tm, d), row), pl.BlockSpec((tm, d), row),
         pl.BlockSpec((tm, 1), row)],
        [pl.BlockSpec((1, d), lambda i: (0, 0))], [_sds((1, d), F32)], [dn_a, dn_b, x, r],
        sem=("arbitrary",))[0][0]


def loss_head(name, h, g, target):
    s, d = h.shape
    tm = _tile(s, 512, 8)
    nsteps = s // tm

    def body(h_ref, g_ref, t_ref, loss_ref, dh_ref, dhb_ref, dg_ref, sq_ref):
        i = pl.program_id(0)
        hv = h_ref[...]
        gv = g_ref[...]
        r = lax.rsqrt(jnp.mean(hv * hv, axis=-1, keepdims=True) + EPS)
        xh = hv * r
        err = xh * gv - t_ref[...]
        dy = err * (1.0 / d)
        dxh = dy * gv
        dh = r * (dxh - xh * jnp.mean(dxh * xh, axis=-1, keepdims=True))
        dh_ref[...] = dh
        dhb_ref[...] = dh.astype(BF16)
        dg_part = jnp.sum(dy * xh, axis=0, keepdims=True)
        sq_part = jnp.sum(err * err, axis=0, keepdims=True)

        @pl.when(i == 0)
        def _():
            dg_ref[...] = dg_part
            sq_ref[...] = sq_part

        @pl.when(i > 0)
        def _():
            dg_ref[...] += dg_part
            sq_ref[...] += sq_part

        @pl.when(i == nsteps - 1)
        def _():
            total = jnp.sum(sq_ref[...], axis=-1, keepdims=True) * (0.5 / d)
            loss_ref[...] = jnp.broadcast_to(total, loss_ref.shape)

    row = lambda i: (i, 0)
    fixed = lambda i: (0, 0)
    return _call(
        name, body, (nsteps,),
        [pl.BlockSpec((tm, d), row), pl.BlockSpec((1, d), fixed), pl.BlockSpec((tm, d), row)],
        [pl.BlockSpec((8, 128), fixed), pl.BlockSpec((tm, d), row), pl.BlockSpec((tm, d), row),
         pl.BlockSpec((1, d), fixed)],
        [_sds((8, 128), F32), _sds((s, d), F32), _sds((s, d), BF16), _sds((1, d), F32)], [h, g, target],
        scratch=[pltpu.VMEM((1, d), F32)], sem=("arbitrary",))[0]


def _mm(name, grid, in_arrays, in_specs, out_shapes, out_specs, acc_tile, dot, epilogue, jobs=(), place=None):
    nk = grid[2]
    n_in = len(in_arrays)
    n_out = len(out_shapes)

    def body(*refs):
        ins, outs = refs[:n_in], refs[n_in:n_in + n_out]
        if nk == 1:
            epilogue(dot(*ins), ins, outs)
            return
        acc = refs[n_in + n_out]
        k = pl.program_id(2)

        @pl.when(k == 0)
        def _():
            acc[...] = dot(*ins)

        @pl.when(jnp.logical_and(k > 0, k < nk - 1))
        def _():
            acc[...] += dot(*ins)

        @pl.when(k == nk - 1)
        def _():
            epilogue(acc[...] + dot(*ins), ins, outs)

    scratch = [pltpu.VMEM(acc_tile, F32)] if nk > 1 else []
    outs, job_outs = _call(name, body, grid, in_specs, out_specs, out_shapes, in_arrays, scratch=scratch,
                           sem=("parallel", "parallel", "arbitrary"), jobs=jobs, place=place)
    return _ret(outs, job_outs, jobs)


def _store(scale, dtype):
    def epilogue(acc, ins, outs):
        outs[0][...] = (acc * scale if scale != 1.0 else acc).astype(dtype)
    return epilogue


def mm_nn(name, a, w, out_dtype, tm=1024, tn=1024, tk=2048, jobs=()):
    m, kd = a.shape
    n = w.shape[1]
    tm, tn, tk = _tile(m, tm, 8), _tile(n, tn), _tile(kd, tk)
    return _mm(
        name, (n // tn, m // tm, kd // tk), [a, w],
        [pl.BlockSpec((tm, tk), lambda j, i, k: (i, k)), pl.BlockSpec((tk, tn), lambda j, i, k: (k, j))],
        [_sds((m, n), out_dtype)], [pl.BlockSpec((tm, tn), lambda j, i, k: (i, j))], (tm, tn),
        lambda a_ref, w_ref: _dot_nn(a_ref[...], w_ref[...]), _store(1.0, out_dtype), jobs)


def mm_nn_resid(name, a, w, x, scale, tm=1024, tn=1024, tk=1408, jobs=()):
    m, kd = a.shape
    n = w.shape[1]
    tm, tn, tk = _tile(m, tm, 8), _tile(n, tn), _tile(kd, tk)

    def epilogue(acc, ins, outs):
        outs[0][...] = ins[2][...] + scale * acc

    return _mm(
        name, (n // tn, m // tm, kd // tk), [a, w, x],
        [pl.BlockSpec((tm, tk), lambda j, i, k: (i, k)), pl.BlockSpec((tk, tn), lambda j, i, k: (k, j)),
         pl.BlockSpec((tm, tn), lambda j, i, k: (i, j))],
        [_sds((m, n), F32)], [pl.BlockSpec((tm, tn), lambda j, i, k: (i, j))], (tm, tn),
        lambda a_ref, w_ref, x_ref: _dot_nn(a_ref[...], w_ref[...]), epilogue, jobs)


def mm_nt(name, a, w, out_dtype, scale=1.0, tm=1024, tn=1024, tk=2048, jobs=()):
    m, kd = a.shape
    n = w.shape[0]
    tm, tn, tk = _tile(m, tm, 8), _tile(n, tn), _tile(kd, tk)
    return _mm(
        name, (n // tn, m // tm, kd // tk), [a, w],
        [pl.BlockSpec((tm, tk), lambda j, i, k: (i, k)), pl.BlockSpec((tn, tk), lambda j, i, k: (j, k))],
        [_sds((m, n), out_dtype)], [pl.BlockSpec((tm, tn), lambda j, i, k: (i, j))], (tm, tn),
        lambda a_ref, w_ref: _dot_nt(a_ref[...], w_ref[...]), _store(scale, out_dtype), jobs)


def mm_nt_pair(name, a3, w, out_dtype, tm=1024, tn=1024, tk=2816, jobs=()):
    _, m, f = a3.shape
    n = w.shape[0]
    tm, tn, tk = _tile(m, tm, 8), _tile(n, tn), _tile(f, tk)
    nkf = f // tk
    return _mm(
        name, (n // tn, m // tm, 2 * nkf), [a3, w],
        [pl.BlockSpec((None, tm, tk), lambda j, i, k: (k // nkf, i, k % nkf)),
         pl.BlockSpec((tn, tk), lambda j, i, k: (j, k))],
        [_sds((m, n), out_dtype)], [pl.BlockSpec((tm, tn), lambda j, i, k: (i, j))], (tm, tn),
        lambda a_ref, w_ref: _dot_nt(a_ref[...], w_ref[...]), _store(1.0, out_dtype), jobs)


def mm_nt_norm_bwd(name, a, w, x, r, g, dh_in, tm=512, tk=1408, jobs=()):
    pair = a.ndim == 3
    m, kd = a.shape[-2], a.shape[-1]
    d = w.shape[0]
    tm, tk = _tile(m, tm, 8), _tile(kd, tk)
    nkf = kd // tk
    nk = 2 * nkf if pair else nkf
    if pair:
        a_spec = pl.BlockSpec((None, tm, tk), lambda i, k: (k // nkf, i, k % nkf))
    else:
        a_spec = pl.BlockSpec((tm, tk), lambda i, k: (i, k))
    row = lambda i, k: (i, 0)
    fixed = lambda i, k: (0, 0)

    def body(a_ref, w_ref, x_ref, r_ref, g_ref, dh_ref, out_ref, outb_ref, dg_ref, *acc):
        i, k = pl.program_id(0), pl.program_id(1)
        dot = lambda: _dot_nt(a_ref[...], w_ref[...])

        def finish(dn):
            xh = x_ref[...] * r_ref[...]
            dxh = dn * g_ref[...]
            out = dh_ref[...] + r_ref[...] * (dxh - xh * jnp.mean(dxh * xh, axis=-1, keepdims=True))
            out_ref[...] = out
            outb_ref[...] = out.astype(BF16)
            part = jnp.sum(dn * xh, axis=0, keepdims=True)

            @pl.when(i == 0)
            def _():
                dg_ref[...] = part

            @pl.when(i > 0)
            def _():
                dg_ref[...] += part

        if nk == 1:
            finish(dot())
            return

        @pl.when(k == 0)
        def _():
            acc[0][...] = dot()

        @pl.when(jnp.logical_and(k > 0, k < nk - 1))
        def _():
            acc[0][...] += dot()

        @pl.when(k == nk - 1)
        def _():
            finish(acc[0][...] + dot())

    outs, job_outs = _call(
        name, body, (m // tm, nk),
        [a_spec, pl.BlockSpec((d, tk), lambda i, k: (0, k)), pl.BlockSpec((tm, d), row), pl.BlockSpec((tm, 1), row),
         pl.BlockSpec((1, d), fixed), pl.BlockSpec((tm, d), row)],
        [pl.BlockSpec((tm, d), row), pl.BlockSpec((tm, d), row), pl.BlockSpec((1, d), fixed)],
        [_sds((m, d), F32), _sds((m, d), BF16), _sds((1, d), F32)], [a, w, x, r, g, dh_in],
        scratch=[pltpu.VMEM((tm, d), F32)] if nk > 1 else [], sem=("arbitrary", "arbitrary"), jobs=jobs)
    return _ret(outs, job_outs, jobs, single=False)


def mm_tn(name, a, b, out_dtype, scale=1.0, tm=1024, tn=1024, tk=4096, jobs=()):
    kd, m = a.shape
    n = b.shape[1]
    tm, tn, tk = _tile(m, tm), _tile(n, tn), _tile(kd, tk, 16)
    return _mm(
        name, (n // tn, m // tm, kd // tk), [a, b],
        [pl.BlockSpec((tk, tm), lambda j, i, k: (k, i)), pl.BlockSpec((tk, tn), lambda j, i, k: (k, j))],
        [_sds((m, n), out_dtype)], [pl.BlockSpec((tm, tn), lambda j, i, k: (i, j))], (tm, tn),
        lambda a_ref, b_ref: _dot_tn(a_ref[...], b_ref[...]), _store(scale, out_dtype), jobs)


def mm_tn_pair(name, a, b3, out_dtype, tm=1024, tn=512, tk=4096, jobs=()):
    kd, m = a.shape
    f = b3.shape[2]
    tm, tn, tk = _tile(m, tm), _tile(f, tn), _tile(kd, tk, 16)
    nf = f // tn
    return _mm(
        name, (m // tm, 2 * nf, kd // tk), [a, b3],
        [pl.BlockSpec((tk, tm), lambda i, j, k: (k, i)),
         pl.BlockSpec((None, tk, tn), lambda i, j, k: (j // nf, k, j % nf))],
        [_sds((m, 2 * f), out_dtype)], [pl.BlockSpec((tm, tn), lambda i, j, k: (i, j))], (tm, tn),
        lambda a_ref, b_ref: _dot_tn(a_ref[...], b_ref[...]), _store(1.0, out_dtype), jobs)


def mm_tn_pair_half(name, a, b3, out_dtype, place, mine, tm=1024, tn=512, tk=4096, jobs=()):
    kd, m = a.shape
    f = b3.shape[2]
    tm, tn, tk = _tile(m // 2, tm), _tile(f, tn), _tile(kd, tk, 16)
    nf, nbm = f // tn, m // 2 // tm
    which = (lambda p: p[1]) if mine else (lambda p: 1 - p[1])
    return _mm(
        name, (nbm, 2 * nf, kd // tk), [a, b3],
        [pl.BlockSpec((tk, tm), lambda i, j, k, p: (k, i + which(p) * nbm)),
         pl.BlockSpec((None, tk, tn), lambda i, j, k, p: (j // nf, k, j % nf))],
        [_sds((m // 2, 2 * f), out_dtype)], [pl.BlockSpec((tm, tn), lambda i, j, k, p: (i, j))], (tm, tn),
        lambda a_ref, b_ref: _dot_tn(a_ref[...], b_ref[...]), _store(1.0, out_dtype), jobs, place)


def swiglu_fwd(name, n, w_in, tm=1024, tn=512, jobs=(), stride=1, phase=0, prev=None):
    s, d = n.shape
    f = w_in.shape[1] // 2
    tm, tn = _tile(s, tm, 8), _tile(f, tn)
    nf = f // tn
    col = lambda j: j * stride + phase

    def body(n_ref, wg_ref, wu_ref, *rest):
        gu_ref, a_ref = rest[-2:]
        nv = n_ref[...]
        g = _dot_nn(nv, wg_ref[...])
        u = _dot_nn(nv, wu_ref[...])
        gu_ref[0] = g.astype(BF16)
        gu_ref[1] = u.astype(BF16)
        a_ref[...] = (g * jax.nn.sigmoid(g) * u).astype(BF16)

    kept = list(prev) if prev is not None else []
    outs, job_outs = _call(
        name, body, (nf // stride, s // tm),
        [pl.BlockSpec((tm, d), lambda j, i: (i, 0)), pl.BlockSpec((d, tn), lambda j, i: (0, col(j))),
         pl.BlockSpec((d, tn), lambda j, i: (0, col(j) + nf))] + [ANY] * len(kept),
        [pl.BlockSpec((2, tm, tn), lambda j, i: (0, i, col(j))), pl.BlockSpec((tm, tn), lambda j, i: (i, col(j)))],
        [_sds((2, s, f), BF16), _sds((s, f), BF16)], [n, w_in, w_in] + kept, sem=("parallel", "parallel"),
        jobs=jobs, carried={3 + k: k for k in range(len(kept))})
    return _ret(outs, job_outs, jobs, single=False)


def swiglu_bwd(name, dh, w_out, gu, scale, tm=1024, tn=512, jobs=()):
    s, d = dh.shape
    f = w_out.shape[0]
    tm, tn = _tile(s, tm, 8), _tile(f, tn)

    sub = _tile(tm, 256, 8)

    def body(dh_ref, w_ref, gu_ref, out_ref):
        for lo in range(0, tm, sub):
            rows = slice(lo, lo + sub)
            da = (_dot_nt(dh_ref[rows, :], w_ref[...]) * scale).astype(BF16)
            g = gu_ref[0, rows, :]
            u = gu_ref[1, rows, :]
            sg = 0.5 * jnp.tanh(0.5 * g) + 0.5
            t = g * sg
            out_ref[0, rows, :] = da * (u * (sg + t * (1.0 - sg)))
            out_ref[1, rows, :] = da * t

    outs, job_outs = _call(
        name, body, (f // tn, s // tm),
        [pl.BlockSpec((tm, d), lambda j, i: (i, 0)), pl.BlockSpec((tn, d), lambda j, i: (j, 0)),
         pl.BlockSpec((2, tm, tn), lambda j, i: (0, i, j))],
        [pl.BlockSpec((2, tm, tn), lambda j, i: (0, i, j))],
        [_sds((2, s, f), BF16)], [dh, w_out, gu], sem=("parallel", "parallel"), jobs=jobs)
    return _ret(outs, job_outs, jobs)


HALO = 16


def _conv_inputs(z_ref, hgc_ref, hhc_ref, i, cw, tm):
    gc = z_ref[:, cw:2 * cw].astype(F32)
    hc = z_ref[:, 2 * cw:3 * cw].astype(F32)
    cin = gc * hc
    halo = hgc_ref[...].astype(F32) * hhc_ref[...].astype(F32) * (i > 0).astype(F32)
    row = lax.broadcasted_iota(jnp.int32, (tm, cw), 0)
    x1 = jnp.where(row == 0, halo[HALO - 1:HALO], pltpu.roll(cin, 1, 0))
    x2 = jnp.where(row == 0, halo[HALO - 2:HALO - 1], jnp.where(row == 1, halo[HALO - 1:HALO], pltpu.roll(cin, 2, 0)))
    return gc, hc, cin, x1, x2


def _tril(w):
    r = lax.broadcasted_iota(jnp.int32, w.shape, 0)
    c = lax.broadcasted_iota(jnp.int32, w.shape, 1)
    return jnp.where(r >= c, w, jnp.zeros_like(w))


def mixer_fwd(name, z, conv_w, conv_b, g_v, w_s, b_t, tm=256, jobs=()):
    s, zc = z.shape
    cw = conv_w.shape[1]
    gw = g_v.shape[1]
    heads = gw // GROUP
    tm = _tile(s, tm)
    hb = tm // HALO

    def body(z_ref, hgc_ref, hhc_ref, cw_ref, cb_ref, gv_ref, ws_ref, bt_ref, y_ref):
        i = pl.program_id(0)
        _, _, cin, x1, x2 = _conv_inputs(z_ref, hgc_ref, hhc_ref, i, cw, tm)
        cv = cb_ref[...] + cw_ref[2:3, :] * cin + cw_ref[1:2, :] * x1 + cw_ref[0:1, :] * x2
        y_ref[:, 0:cw] = (z_ref[:, 0:cw].astype(F32) * cv).astype(BF16)
        for h in range(heads):
            lo = h * GROUP
            vh = z_ref[:, 3 * cw + gw + lo:3 * cw + gw + lo + GROUP].astype(F32)
            rv = lax.rsqrt(jnp.mean(vh * vh, axis=-1, keepdims=True) + EPS)
            vn = (vh * rv * gv_ref[:, lo:lo + GROUP]).astype(BF16)
            w = _tril(ws_ref[h]).astype(BF16)
            for n in range(tm // GROUP):
                rows = slice(n * GROUP, (n + 1) * GROUP)
                sg = _dot_nn(w, vn[rows]) + bt_ref[:, h:h + 1]
                u = z_ref[rows, 3 * cw + lo:3 * cw + lo + GROUP].astype(F32)
                y_ref[rows, cw + lo:cw + lo + GROUP] = (u * sg).astype(BF16)

    fixed2 = lambda i: (0, 0)
    outs, job_outs = _call(
        name, body, (s // tm,),
        [pl.BlockSpec((tm, zc), lambda i: (i, 0)),
         pl.BlockSpec((HALO, cw), lambda i: (jnp.maximum(i * hb - 1, 0), 1)),
         pl.BlockSpec((HALO, cw), lambda i: (jnp.maximum(i * hb - 1, 0), 2)),
         pl.BlockSpec(conv_w.shape, fixed2), pl.BlockSpec(conv_b.shape, fixed2),
         pl.BlockSpec(g_v.shape, fixed2), pl.BlockSpec(w_s.shape, lambda i: (0, 0, 0)),
         pl.BlockSpec(b_t.shape, fixed2)],
        [pl.BlockSpec((tm, cw + gw), lambda i: (i, 0))], [_sds((s, cw + gw), BF16)],
        [z, z, z, conv_w, conv_b, g_v, w_s, b_t], sem=("arbitrary",), jobs=jobs)
    return _ret(outs, job_outs, jobs)


def mixer_bwd(name, z, dy, conv_w, conv_b, g_v, w_s, b_t, tm=256, jobs=()):
    s, zc = z.shape
    cw = conv_w.shape[1]
    gw = g_v.shape[1]
    heads = gw // GROUP
    tm = _tile(s, tm)
    hb = tm // HALO
    nsteps = s // tm
    last_halo = s // HALO - 1

    def body(z_ref, hgc_ref, hhc_ref, ngb_ref, dy_ref, ndy_ref, cw_ref, cb_ref, gv_ref, ws_ref, bt_ref,
             dz_ref, sm_ref, dws_ref, dbt_ref, dsg_ref):
        i = pl.program_id(0)

        @pl.when(i == 0)
        def _():
            sm_ref[...] = jnp.zeros_like(sm_ref)
            dws_ref[...] = jnp.zeros_like(dws_ref)
            dsg_ref[...] = jnp.zeros_like(dsg_ref)

        gc, hc, cin, x1, x2 = _conv_inputs(z_ref, hgc_ref, hhc_ref, i, cw, tm)
        w0, w1, w2 = cw_ref[0:1, :], cw_ref[1:2, :], cw_ref[2:3, :]
        cv = cb_ref[...] + w2 * cin + w1 * x1 + w0 * x2
        gb = z_ref[:, 0:cw].astype(F32)
        dyc = dy_ref[:, 0:cw].astype(F32)
        dz_ref[:, 0:cw] = (dyc * cv).astype(BF16)
        dcv = dyc * gb
        nxt = ndy_ref[...].astype(F32) * ngb_ref[...].astype(F32) * (i < nsteps - 1).astype(F32)
        row = lax.broadcasted_iota(jnp.int32, (tm, cw), 0)
        d1 = jnp.where(row == tm - 1, nxt[0:1], pltpu.roll(dcv, tm - 1, 0))
        d2 = jnp.where(row == tm - 1, nxt[1:2], jnp.where(row == tm - 2, nxt[0:1], pltpu.roll(dcv, tm - 2, 0)))
        dcin = w2 * dcv + w1 * d1 + w0 * d2
        dz_ref[:, cw:2 * cw] = (dcin * hc).astype(BF16)
        dz_ref[:, 2 * cw:3 * cw] = (dcin * gc).astype(BF16)
        sm_ref[0:1, :] += jnp.sum(dcv * x2, axis=0, keepdims=True)
        sm_ref[1:2, :] += jnp.sum(dcv * x1, axis=0, keepdims=True)
        sm_ref[2:3, :] += jnp.sum(dcv * cin, axis=0, keepdims=True)
        sm_ref[3:4, :] += jnp.sum(dcv, axis=0, keepdims=True)

        for h in range(heads):
            lo = h * GROUP
            vcol = slice(3 * cw + gw + lo, 3 * cw + gw + lo + GROUP)
            ucol = slice(3 * cw + lo, 3 * cw + lo + GROUP)
            vh = z_ref[:, vcol].astype(F32)
            rv = lax.rsqrt(jnp.mean(vh * vh, axis=-1, keepdims=True) + EPS)
            xh = vh * rv
            gvh = gv_ref[:, lo:lo + GROUP]
            vn = (xh * gvh).astype(BF16)
            w = _tril(ws_ref[h]).astype(BF16)
            dgv = jnp.zeros((1, GROUP), F32)
            for n in range(tm // GROUP):
                rows = slice(n * GROUP, (n + 1) * GROUP)
                sg = _dot_nn(w, vn[rows]) + bt_ref[:, h:h + 1]
                dyg = dy_ref[rows, cw + lo:cw + lo + GROUP].astype(F32)
                dsg = dyg * z_ref[rows, ucol].astype(F32)
                dz_ref[rows, ucol] = (dyg * sg).astype(BF16)
                dsgb = dsg.astype(BF16)
                dvn = _dot_tn(w, dsgb)
                dws_ref[h] += _dot_nt(dsgb, vn[rows])
                dsg_ref[:, lo:lo + GROUP] += dsg
                xhc = xh[rows]
                dgv = dgv + jnp.sum(dvn * xhc, axis=0, keepdims=True)
                dxh = dvn * gvh
                dv = rv[rows] * (dxh - xhc * jnp.mean(dxh * xhc, axis=-1, keepdims=True))
                dz_ref[rows, vcol] = dv.astype(BF16)
            sm_ref[4:5, lo:lo + GROUP] += dgv

        @pl.when(i == nsteps - 1)
        def _():
            for h in range(heads):
                dws_ref[h] = _tril(dws_ref[h])
                dbt_ref[:, h:h + 1] = jnp.sum(dsg_ref[:, h * GROUP:(h + 1) * GROUP], axis=-1, keepdims=True)

    fixed2 = lambda i: (0, 0)
    fixed3 = lambda i: (0, 0, 0)
    prev = lambda col: (lambda i: (jnp.maximum(i * hb - 1, 0), col))
    nxt_blk = lambda i: (jnp.minimum((i + 1) * hb, last_halo), 0)
    outs, job_outs = _call(
        name, body, (nsteps,),
        [pl.BlockSpec((tm, zc), lambda i: (i, 0)),
         pl.BlockSpec((HALO, cw), prev(1)), pl.BlockSpec((HALO, cw), prev(2)),
         pl.BlockSpec((HALO, cw), nxt_blk),
         pl.BlockSpec((tm, cw + gw), lambda i: (i, 0)), pl.BlockSpec((HALO, cw), nxt_blk),
         pl.BlockSpec(conv_w.shape, fixed2), pl.BlockSpec(conv_b.shape, fixed2),
         pl.BlockSpec(g_v.shape, fixed2), pl.BlockSpec(w_s.shape, fixed3), pl.BlockSpec(b_t.shape, fixed2)],
        [pl.BlockSpec((tm, zc), lambda i: (i, 0)), pl.BlockSpec((8, cw), fixed2),
         pl.BlockSpec(w_s.shape, fixed3), pl.BlockSpec(b_t.shape, fixed2)],
        [_sds((s, zc), BF16), _sds((8, cw), F32), _sds(w_s.shape, F32), _sds(b_t.shape, F32)],
        [z, z, z, z, dy, dy, conv_w, conv_b, g_v, w_s, b_t],
        scratch=[pltpu.VMEM((GROUP, gw), F32)], sem=("arbitrary",), jobs=jobs)
    return _ret(outs, job_outs, jobs, single=False)


def _softmax_rows(sc):
    e = jnp.exp(sc - jnp.max(sc, axis=-1, keepdims=True))
    return e / jnp.sum(e, axis=-1, keepdims=True)


def attn_fwd(name, q, k, v, tm=512, jobs=()):
    s, d = q.shape
    m = k.shape[0]
    hd = d // XA_HEADS
    scale = hd ** -0.5
    tm = _tile(s, tm, 8)

    def body(q_ref, k_ref, v_ref, o_ref):
        for h in range(XA_HEADS):
            cols = slice(h * hd, (h + 1) * hd)
            p = _softmax_rows(_dot_nt(q_ref[:, cols], k_ref[:, cols]) * scale)
            o_ref[:, cols] = _dot_nn(p.astype(BF16), v_ref[:, cols]).astype(BF16)

    outs, job_outs = _call(
        name, body, (s // tm,),
        [pl.BlockSpec((tm, d), lambda i: (i, 0)), pl.BlockSpec((m, d), lambda i: (0, 0)),
         pl.BlockSpec((m, d), lambda i: (0, 0))],
        [pl.BlockSpec((tm, d), lambda i: (i, 0))], [_sds((s, d), BF16)], [q, k, v], sem=("arbitrary",), jobs=jobs)
    return _ret(outs, job_outs, jobs)


def attn_bwd(name, q, k, v, do, tm=512):
    s, d = q.shape
    m = k.shape[0]
    hd = d // XA_HEADS
    scale = hd ** -0.5
    tm = _tile(s, tm, 8)

    def body(q_ref, k_ref, v_ref, do_ref, dq_ref, dk_ref, dv_ref):
        i = pl.program_id(0)

        @pl.when(i == 0)
        def _():
            dk_ref[...] = jnp.zeros_like(dk_ref)
            dv_ref[...] = jnp.zeros_like(dv_ref)

        for h in range(XA_HEADS):
            cols = slice(h * hd, (h + 1) * hd)
            qh = q_ref[:, cols]
            doh = do_ref[:, cols]
            p = _softmax_rows(_dot_nt(qh, k_ref[:, cols]) * scale)
            dp = _dot_nt(doh, v_ref[:, cols])
            ds = (p * (dp - jnp.sum(dp * p, axis=-1, keepdims=True)) * scale).astype(BF16)
            dq_ref[:, cols] = _dot_nn(ds, k_ref[:, cols]).astype(BF16)
            dk_ref[:, cols] += _dot_tn(ds, qh)
            dv_ref[:, cols] += _dot_tn(p.astype(BF16), doh)

    row = lambda i: (i, 0)
    fixed = lambda i: (0, 0)
    return _call(
        name, body, (s // tm,),
        [pl.BlockSpec((tm, d), row), pl.BlockSpec((m, d), fixed), pl.BlockSpec((m, d), fixed),
         pl.BlockSpec((tm, d), row)],
        [pl.BlockSpec((tm, d), row), pl.BlockSpec((m, d), fixed), pl.BlockSpec((m, d), fixed)],
        [_sds((s, d), BF16), _sds((m, d), F32), _sds((m, d), F32)], [q, k, v, do], sem=("arbitrary",))[0]


def _grid2(rows, cols, row_mult):
    tr, tc = _tile(rows, 512, row_mult), _tile(cols, 2048)
    return tr, tc, rows // tr, cols // tc


def cast_place(name, block, axis, place):
    r, c = block.shape
    tr, tc, nbr, nbc = _grid2(r, c, 16)
    if axis == 1:
        dst = lambda i, j, p: (i, j + p[0] * nbc)
    else:
        dst = lambda i, j, p: (i + p[0] * nbr, j)

    def body(p_ref, w_ref, out_ref):
        out_ref[...] = w_ref[...].astype(BF16)

    return pl.pallas_call(
        body, name=name,
        grid_spec=pltpu.PrefetchScalarGridSpec(
            num_scalar_prefetch=1, grid=(nbr, nbc),
            in_specs=[pl.BlockSpec((tr, tc), lambda i, j, p: (i, j))],
            out_specs=pl.BlockSpec((tr, tc), dst)),
        out_shape=_sds(_full_shape(block.shape, axis), BF16),
        compiler_params=_params(("parallel", "parallel")),
    )(place, block)


def pair_add(name, grad, peer, axis, place):
    hr, hc = peer.shape
    tr, tc, nbr, nbc = _grid2(hr, hc, 16)
    same = lambda i, j, p: (i, j)
    if grad.shape == peer.shape:
        mine = same
    elif axis == 1:
        mine = lambda i, j, p: (i + p[1] * nbr, j)
    else:
        mine = lambda i, j, p: (i, j + p[1] * nbc)

    def body(p_ref, g_ref, q_ref, out_ref):
        out_ref[...] = (g_ref[...].astype(F32) + q_ref[...].astype(F32)).astype(BF16)

    return pl.pallas_call(
        body, name=name,
        grid_spec=pltpu.PrefetchScalarGridSpec(
            num_scalar_prefetch=1, grid=(nbr, nbc),
            in_specs=[pl.BlockSpec((tr, tc), mine), pl.BlockSpec((tr, tc), same)],
            out_specs=pl.BlockSpec((tr, tc), same)),
        out_shape=_sds((hr, hc), BF16),
        compiler_params=_params(("parallel", "parallel")),
    )(place, grad, peer)


def cross_sum(name, part, land, axis, shape, place):
    _, sr, sc = land.shape
    tr, tc, nbr, nbc = _grid2(sr, sc, 16)
    if axis == 1:
        own = lambda i, j, p: (i, j + p[0] * nbc)
        dst = lambda i, j, p: (i + p[1] * nbr, j)
    else:
        own = lambda i, j, p: (i + p[0] * nbr, j)
        dst = lambda i, j, p: (i, j + p[1] * nbc)

    def body(p_ref, own_ref, land_ref, out_ref):
        out_ref[...] = ((own_ref[...].astype(F32) + land_ref[0].astype(F32))
                        + (land_ref[1].astype(F32) + land_ref[2].astype(F32)))

    return pl.pallas_call(
        body, name=name,
        grid_spec=pltpu.PrefetchScalarGridSpec(
            num_scalar_prefetch=1, grid=(nbr, nbc),
            in_specs=[pl.BlockSpec((tr, tc), own), pl.BlockSpec((3, tr, tc), lambda i, j, p: (0, i, j))],
            out_specs=pl.BlockSpec((tr, tc), dst)),
        out_shape=_sds(_block(shape, axis), F32),
        compiler_params=_params(("parallel", "parallel")),
    )(place, part, land)


def _adam_math(w, g, m, v):
    m = ADAM_B1 * m + (1.0 - ADAM_B1) * g
    v = ADAM_B2 * v + (1.0 - ADAM_B2) * (g * g)
    m_hat = m / (1.0 - ADAM_B1 ** ADAM_STEP)
    v_hat = v / (1.0 - ADAM_B2 ** ADAM_STEP)
    delta = -ADAM_LR * (m_hat / (jnp.sqrt(v_hat) + ADAM_EPS) + ADAM_WD * w)
    return delta, m, v


def adamw(name, w, g, m, v, jobs=()):
    r, c = w.shape
    tr, tc = _tile(r, 256, 8), _tile(c, 1408)

    def body(w_ref, g_ref, m_ref, v_ref, g_out, d_out, m_out, v_out):
        d, mm, vv = _adam_math(w_ref[...], g_ref[...], m_ref[...], v_ref[...])
        g_out[...] = g_ref[...]
        d_out[...] = d
        m_out[...] = mm
        v_out[...] = vv

    spec = pl.BlockSpec((tr, tc), lambda i, j: (i, j))
    outs, job_outs = _call(name, body, (r // tr, c // tc), [spec] * 4, [spec] * 4, [_sds((r, c), F32)] * 4,
                           [w, g, m, v], sem=("parallel", "parallel"), jobs=jobs)
    return _ret(outs, job_outs, jobs, single=False)


def small_sum(name, stacks):
    def body(*refs):
        for s_ref, out_ref in zip(refs[:len(stacks)], refs[len(stacks):]):
            acc = s_ref[0]
            for d in range(1, s_ref.shape[0]):
                acc = acc + s_ref[d]
            out_ref[...] = acc

    return pl.pallas_call(body, name=name, out_shape=[_sds(s.shape[1:], F32) for s in stacks])(*stacks)


WEIGHTS = ["g_ffn1", "w_ffn1_in", "w_ffn1_out", "g_mix", "w_mix_in", "conv_w", "conv_b", "g_gm_v", "w_spatial",
           "b_spatial", "w_mix_out", "g_xattn", "g_mem", "w_xq", "w_xk", "w_xv", "w_xo", "g_ffn2", "w_ffn2_in",
           "w_ffn2_out", "g_final"]
BIG = {"w_ffn1_in": 1, "w_ffn1_out": 0, "w_mix_in": 1, "w_mix_out": 0, "w_xq": 0, "w_xk": 0, "w_xv": 0, "w_xo": 0,
       "w_ffn2_in": 1, "w_ffn2_out": 0}
SMALL = [n for n in WEIGHTS if n not in BIG]
LATE_SMALL = ["g_ffn1"]
EARLY_SMALL = [n for n in SMALL if n not in LATE_SMALL]


def _pack(arrays):
    flat = jnp.concatenate([a.reshape(-1) for a in arrays])
    rows = -(-flat.shape[0] // 1024) * 8
    return jnp.pad(flat, (0, rows * 128 - flat.shape[0])).reshape(rows, 128)


def _unpack(buf, shapes):
    flat = buf.reshape(-1)
    out, pos = [], 0
    for shp in shapes:
        n = math.prod(shp)
        out.append(flat[pos:pos + n].reshape(shp))
        pos += n
    return out


def kernel(x, mem, g_ffn1, w_ffn1_in, w_ffn1_out, g_mix, w_mix_in, conv_w, conv_b, g_gm_v, w_spatial, b_spatial, w_mix_out, g_xattn, g_mem, w_xq, w_xk, w_xv, w_xo, g_ffn2, w_ffn2_in, w_ffn2_out, g_final, loss_target, m_g_ffn1, m_w_ffn1_in, m_w_ffn1_out, m_g_mix, m_w_mix_in, m_conv_w, m_conv_b, m_g_gm_v, m_w_spatial, m_b_spatial, m_w_mix_out, m_g_xattn, m_g_mem, m_w_xq, m_w_xk, m_w_xv, m_w_xo, m_g_ffn2, m_w_ffn2_in, m_w_ffn2_out, m_g_final, v_g_ffn1, v_w_ffn1_in, v_w_ffn1_out, v_g_mix, v_w_mix_in, v_conv_w, v_conv_b, v_g_gm_v, v_w_spatial, v_b_spatial, v_w_mix_out, v_g_xattn, v_g_mem, v_w_xq, v_w_xk, v_w_xv, v_w_xo, v_g_ffn2, v_w_ffn2_in, v_w_ffn2_out, v_g_final):
    given = dict(locals())
    wts = {n: given[n] for n in WEIGHTS}
    mom = {n: given["m_" + n] for n in WEIGHTS}
    var = {n: given["v_" + n] for n in WEIGHTS}

    xi, yi, ci = lax.axis_index("x"), lax.axis_index("y"), lax.axis_index("c")
    blk = 2 * xi + yi
    place = jnp.stack([blk, ci]).astype(jnp.int32)

    x2, mem2, tgt = x[0], mem[0], loss_target[0]
    own = {n: cast_place("cast_" + n, wts[n][0], BIG[n], place) for n in BIG}
    shape = {n: own[n].shape for n in BIG}
    w_s, b_t = w_spatial[0], b_spatial[0].T
    gf = g_final[None]

    def gather(*names):
        return gather_job([(own[n], BIG[n], WHOLE, WHOLE) for n in names])

    def gather_part(arr, sub=WHOLE, within=WHOLE):
        return gather_job([(arr, 1, sub, within)])

    full = {}
    left, right = (0, 1, 2), (1, 1, 2)
    half_cols = dict(tm=512, tn=shape["w_ffn1_in"][1] // (2 * N_CHIPS), stride=2)

    (w1in,), (conv_taps,) = comm_only(
        "gather_first", [gather_part(own["w_ffn1_in"], within=left),
                         columns_job(jnp.pad(conv_w[0], ((0, 8 - CONV_K), (0, 0))))])
    n1, r1 = rmsnorm_fwd("norm1", x2, g_ffn1)
    halves, ((w1in,),) = swiglu_fwd("ffn1_in_left", n1, w1in, phase=0, jobs=[gather_part(w1in, within=right)],
                                    **half_cols)
    full["w_ffn1_in"] = w1in
    (gu1, a1), ((full["w_ffn1_out"],),) = swiglu_fwd("ffn1_in_right", n1, w1in, phase=1, prev=halves,
                                                     jobs=[gather("w_ffn1_out")], **half_cols)
    h1, ((full["w_mix_in"], full["w_mix_out"]),) = mm_nn_resid(
        "ffn1_out", a1, full["w_ffn1_out"], x2, 0.5, tm=512, tk=5632, jobs=[gather("w_mix_in", "w_mix_out")])
    (n2, r2), ((w2in,),) = rmsnorm_fwd("norm2", h1, g_mix, jobs=[gather_part(own["w_ffn2_in"], (0, 1, 8))])
    z, ((full["w_xq"], full["w_xk"], full["w_xv"]),) = mm_nn("mix_in", n2, full["w_mix_in"], BF16,
                                                             jobs=[gather("w_xq", "w_xk", "w_xv")])
    ycat, ((full["w_xo"],),) = mixer_fwd("mixer", z, conv_taps, conv_b, g_gm_v, w_s, b_t, jobs=[gather("w_xo")])
    h2, ((w2in,),) = mm_nn_resid("mix_out", ycat, full["w_mix_out"], h1, 1.0, tk=2048,
                                 jobs=[gather_part(w2in, (1, 2, 8))])
    (n3, r3), ((w2in,),) = rmsnorm_fwd("norm3", h2, g_xattn, jobs=[gather_part(w2in, (3, 1, 8))])
    mn, rm = rmsnorm_fwd("norm_mem", mem2, g_mem)
    q, ((w2in,),) = mm_nn("xq", n3, full["w_xq"], BF16, jobs=[gather_part(w2in, (4, 2, 8))])
    k = mm_nn("xk", mn, full["w_xk"], BF16)
    v = mm_nn("xv", mn, full["w_xv"], BF16)
    o, ((w2in,),) = attn_fwd("attn", q, k, v, jobs=[gather_part(w2in, (6, 1, 8))])
    h3, ((w2in,),) = mm_nn_resid("xo", o, full["w_xo"], h2, 1.0, tk=2048, jobs=[gather_part(w2in, (7, 1, 8))])
    full["w_ffn2_in"] = w2in
    n4, r4 = rmsnorm_fwd("norm4", h3, g_ffn2)
    (gu2, a2), ((full["w_ffn2_out"],),) = swiglu_fwd("ffn2_in", n4, full["w_ffn2_in"], jobs=[gather("w_ffn2_out")])
    h4 = mm_nn_resid("ffn2_out", a2, full["w_ffn2_out"], h3, 0.5, tm=512, tk=5632)
    loss_blk, dh4, dh4b, dg_final = loss_head("loss_head", h4, gf, tgt)

    dw, peer, part, land, half, grads = {}, {}, {}, {}, {}, {}

    def send_pair(*names):
        return pair_job([dw[n] for n in names], [BIG[n] for n in names])

    def take_pair(names, got):
        for n, p in zip(names, got):
            part[n] = pair_add("pair_add_" + n, dw[n], p, BIG[n], place)

    def send_cross(*names, sub=WHOLE):
        return cross_job([(part[n], BIG[n], shape[n], land.get(n), sub) for n in names])

    def take_cross(names, got, last=True):
        for n, l in zip(names, got):
            land[n] = l
            if last:
                half[n] = cross_sum("cross_sum_" + n, part[n], l, BIG[n], shape[n], place)

    def send_final(*names):
        return final_job([half[n] for n in names], [BIG[n] for n in names], [shape[n] for n in names])

    delta, new_m, new_v = {}, {}, {}

    reduced = {}

    def take_final(names, got):
        for n, g in zip(names, got):
            reduced[n] = g

    def update(n, jobs=()):
        res = adamw("adamw_" + n, wts[n][0], reduced[n], mom[n][0], var[n][0], jobs=jobs)
        (grads[n], delta[n], new_m[n], new_v[n]), job_outs = res if jobs else (res, [])
        return job_outs

    dgu2 = swiglu_bwd("ffn2_dact", dh4b, full["w_ffn2_out"], gu2, 0.5)
    dw["w_ffn2_in"] = mm_tn_pair("ffn2_dwin", n4, dgu2, BF16)
    dw["w_ffn2_out"], (got,) = mm_tn("ffn2_dwout", a2, dh4b, BF16, scale=0.5, jobs=[send_pair("w_ffn2_in")])
    take_pair(["w_ffn2_in"], got)
    (dh3, dh3b, dg_ffn2), (got_c, got_p) = mm_nt_norm_bwd(
        "ffn2_dn", dgu2, full["w_ffn2_in"], h3, r4, g_ffn2, dh4,
        jobs=[send_cross("w_ffn2_in", sub=(0, 7, 8)), send_pair("w_ffn2_out")])
    take_cross(["w_ffn2_in"], got_c, last=False)
    take_pair(["w_ffn2_out"], got_p)

    dw["w_xo"], (got_c,) = mm_tn("xo_dw", o, dh3b, BF16, jobs=[send_cross("w_ffn2_in", sub=(7, 1, 8))])
    take_cross(["w_ffn2_in"], got_c)
    do, (got_c, got_f) = mm_nt("xo_dx", dh3b, full["w_xo"], BF16,
                               jobs=[send_cross("w_ffn2_out", sub=(0, 2, 8)), send_final("w_ffn2_in")])
    take_cross(["w_ffn2_out"], got_c, last=False)
    take_final(["w_ffn2_in"], got_f)
    update("w_ffn2_in")
    dq, dk, dv = attn_bwd("attn_bwd", q, k, v, do)
    dkb, dvb = dk.astype(BF16), dv.astype(BF16)
    dw["w_xq"], (got_c,) = mm_tn("xq_dw", n3, dq, BF16, jobs=[send_cross("w_ffn2_out", sub=(2, 2, 8))])
    take_cross(["w_ffn2_out"], got_c, last=False)
    (dh2, dh2b, dg_xattn), (got_c,) = mm_nt_norm_bwd(
        "xq_dx", dq, full["w_xq"], h2, r3, g_xattn, dh3, tk=1024, jobs=[send_cross("w_ffn2_out", sub=(4, 4, 8))])
    take_cross(["w_ffn2_out"], got_c)
    dw["w_xk"] = mm_tn("xk_dw", mn, dkb, BF16)
    dw["w_xv"] = mm_tn("xv_dw", mn, dvb, BF16)
    dmn_k = mm_nt("xk_dx", dkb, full["w_xk"], F32)
    dmn_v = mm_nt("xv_dx", dvb, full["w_xv"], F32)
    dg_mem = gain_grad("norm_mem_bwd", dmn_k, dmn_v, mem2, rm)

    dw["w_mix_out"], (got_f,) = mm_tn("mix_out_dw", ycat, dh2b, BF16, jobs=[send_final("w_ffn2_out")])
    take_final(["w_ffn2_out"], got_f)
    update("w_ffn2_out")
    attn_names = ["w_xo", "w_xq", "w_xk", "w_xv", "w_mix_out"]
    dycat, (got_p,) = mm_nt("mix_out_dx", dh2b, full["w_mix_out"], BF16, jobs=[send_pair(*attn_names)])
    take_pair(attn_names, got_p)
    (dz, dsmall, dws, dbt), (got_c,) = mixer_bwd("mixer_bwd", z, dycat, conv_taps, conv_b, g_gm_v, w_s, b_t,
                                                 jobs=[send_cross("w_xo", "w_xq")])
    take_cross(["w_xo", "w_xq"], got_c)
    dw["w_mix_in"], (got_c,) = mm_tn("mix_in_dw", n2, dz, BF16, jobs=[send_cross("w_xk", "w_xv")])
    take_cross(["w_xk", "w_xv"], got_c)
    (dh1, dh1b, dg_mix), (got_c, got_p) = mm_nt_norm_bwd(
        "mix_in_dx", dz, full["w_mix_in"], h1, r2, g_mix, dh2, tk=1280,
        jobs=[send_cross("w_mix_out"), send_pair("w_mix_in")])
    take_cross(["w_mix_out"], got_c)
    take_pair(["w_mix_in"], got_p)

    dw["w_ffn1_out"], (got_c, got_f) = mm_tn("ffn1_dwout", a1, dh1b, BF16, scale=0.5,
                                             jobs=[send_cross("w_mix_in"), send_final(*attn_names)])
    take_cross(["w_mix_in"], got_c)
    take_final(attn_names, got_f)
    for n in attn_names:
        update(n)
    early = {"g_mix": dg_mix, "conv_w": dsmall[0:CONV_K], "conv_b": dsmall[3:4], "g_gm_v": dsmall[4:5],
             "w_spatial": dws, "b_spatial": dbt.T, "g_xattn": dg_xattn, "g_mem": dg_mem, "g_ffn2": dg_ffn2,
             "g_final": dg_final}
    dgu1, (got_p, got_f, (early_all,)) = swiglu_bwd(
        "ffn1_dact", dh1b, full["w_ffn1_out"], gu1, 0.5,
        jobs=[send_pair("w_ffn1_out"), send_final("w_mix_in"), stack_job(_pack([early[n] for n in EARLY_SMALL]))])
    take_pair(["w_ffn1_out"], got_p)
    take_final(["w_mix_in"], got_f)
    update("w_mix_in")
    theirs, (got_c,) = mm_tn_pair_half("ffn1_dwin_theirs", n1, dgu1, BF16, place, False,
                                       jobs=[send_cross("w_ffn1_out", sub=(0, 7, 8))])
    take_cross(["w_ffn1_out"], got_c, last=False)
    mine, (got_c, (from_sibling,)) = mm_tn_pair_half(
        "ffn1_dwin_mine", n1, dgu1, BF16, place, True,
        jobs=[send_cross("w_ffn1_out", sub=(7, 1, 8)), pair_job([theirs], [1], is_half=True)])
    take_cross(["w_ffn1_out"], got_c)
    part["w_ffn1_in"] = pair_add("pair_add_w_ffn1_in", mine, from_sibling, 1, place)
    dn1, (got_c, got_f) = mm_nt_pair("ffn1_dn", dgu1, full["w_ffn1_in"], F32,
                                     jobs=[send_cross("w_ffn1_in", sub=(0, 7, 8)), send_final("w_ffn1_out")])
    take_cross(["w_ffn1_in"], got_c, last=False)
    take_final(["w_ffn1_out"], got_f)
    update("w_ffn1_out")
    dx, _, dg_ffn1 = rmsnorm_bwd("norm1_bwd", dn1, x2, r1, g_ffn1, dh1)
    got_c, (late_all,) = comm_only("tail_cross", [send_cross("w_ffn1_in", sub=(7, 1, 8)),
                                                  stack_job(_pack([dg_ffn1]))])
    take_cross(["w_ffn1_in"], got_c)
    (got_f,) = comm_only("tail_final", [send_final("w_ffn1_in")])
    take_final(["w_ffn1_in"], got_f)
    update("w_ffn1_in")

    early_sum, late_sum = small_sum("small_sum", [early_all, late_all])
    for n, g in zip(EARLY_SMALL, _unpack(early_sum, [early[n].shape for n in EARLY_SMALL])):
        grads[n] = g
    grads["g_ffn1"] = _unpack(late_sum, [dg_ffn1.shape])[0]
    taps_cols = conv_w.shape[2]
    grads["conv_w"] = lax.dynamic_slice_in_dim(grads["conv_w"], blk * taps_cols, taps_cols, axis=1)
    packed = [_pack([src[n] for n in SMALL]) for src in (wts, grads, mom, var)]
    own_shapes = [wts[n].shape for n in SMALL]
    for dst, buf in zip((delta, new_m, new_v), adamw("adamw_small", *packed)[1:]):
        for n, a in zip(SMALL, _unpack(buf, own_shapes)):
            dst[n] = a

    loss = lax.psum(loss_blk[0, 0], ("x", "y", "c"))
    outs = [loss, dx[None]]
    for group in (grads, delta, new_m, new_v):
        outs += [group[n].reshape(wts[n].shape) for n in WEIGHTS]
    return tuple(outs)
```

```python
import math

import jax
import jax.numpy as jnp
from jax import lax
from jax.experimental import pallas as pl
from jax.experimental.pallas import tpu as pltpu
from jax.experimental.pallas import tpu_sc as plsc

F32 = jnp.float32
BF16 = jnp.bfloat16
EPS = 1e-6
GROUP = 128
XA_HEADS = 4
CONV_K = 3
N_CHIPS = 4
VMEM_LIMIT_BYTES = 56 * 1024 * 1024

ADAM_LR = 0.001
ADAM_B1 = 0.9
ADAM_B2 = 0.999
ADAM_EPS = 1e-08
ADAM_WD = 0.01
ADAM_STEP = 10

MESH = pl.DeviceIdType.MESH
ANY = pl.BlockSpec(memory_space=pl.ANY)


def _tile(dim, pref, mult=128):
    if dim <= pref:
        return dim
    t = (pref // mult) * mult
    while t >= mult:
        if dim % t == 0:
            return t
        t -= mult
    raise ValueError(f"no tile for {dim} under {pref}")


def _params(sem):
    return pltpu.CompilerParams(dimension_semantics=sem, vmem_limit_bytes=VMEM_LIMIT_BYTES)


def _sds(shape, dtype):
    return jax.ShapeDtypeStruct(shape, dtype)


def _dot_nn(a, b):
    return jnp.dot(a, b, preferred_element_type=F32)


def _dot_nt(a, b):
    return lax.dot_general(a, b, (((1,), (1,)), ((), ())), preferred_element_type=F32)


def _dot_tn(a, b):
    return lax.dot_general(a, b, (((0,), (0,)), ((), ())), preferred_element_type=F32)


class Job:
    def __init__(self, inputs, out_shapes, aliases, sems, start, middle, finish):
        self.inputs, self.out_shapes, self.aliases, self.sems = inputs, out_shapes, aliases, sems
        self.start, self.middle, self.finish = start, middle, finish


def _place():
    x, y, c = lax.axis_index("x"), lax.axis_index("y"), lax.axis_index("c")
    chips = [(1 - x, y), (x, 1 - y), (1 - x, 1 - y)]
    return x, y, c, chips


def _ds(start, size, lane):
    if not isinstance(start, int):
        start = pl.multiple_of(start, 128 if lane else 16)
    return pl.ds(start, size)


WHOLE = (0, 1, 1)


def _window(ref, axis, shape, blk=None, half=None, sub=WHOLE, within=WHOLE):
    n = shape[axis] // N_CHIPS
    hs = shape[1 - axis] // 2
    idx = [slice(None), slice(None)]
    if blk is not None:
        b_first, b_count, b_pieces = within
        b_ext = n // b_pieces
        idx[axis] = _ds(blk * n + b_first * b_ext, b_count * b_ext, axis == 1)
    first, count, pieces = sub
    ext = hs // pieces
    if half is not None:
        idx[1 - axis] = _ds(half * hs + first * ext, count * ext, axis == 0)
    elif pieces > 1:
        idx[1 - axis] = _ds(first * ext, count * ext, axis == 0)
    return ref.at[tuple(idx)]


def _remote(src, dst, send_sem, recv_sem, dev):
    return pltpu.make_async_remote_copy(src_ref=src, dst_ref=dst, send_sem=send_sem, recv_sem=recv_sem,
                                        device_id=dev, device_id_type=MESH)


def _full_shape(block_shape, axis):
    out = list(block_shape)
    out[axis] *= N_CHIPS
    return tuple(out)


def _half_all(shape, axis):
    out = list(shape)
    out[1 - axis] //= 2
    return tuple(out)


def _block(shape, axis):
    out = list(shape)
    out[axis] //= N_CHIPS
    return tuple(out)


def _half_block(shape, axis):
    return _half_all(_block(shape, axis), axis)


def gather_job(items):
    nw = len(items)
    shapes = [item[0].shape for item in items]
    n_sem = 8

    def parts(sub):
        first, count, pieces = sub
        return (2 * first, count, 2 * pieces), (2 * first + count, count, 2 * pieces)

    def start(pos, ins, outs, sems):
        x, y, c, chips = pos
        for w, (_, ax, sub, within) in enumerate(items):
            mine = _window(outs[w], ax, shapes[w], blk=2 * x + y, half=c, sub=sub, within=within)
            for j in range(2):
                _remote(mine, mine, sems[0].at[n_sem * w + j], sems[1].at[n_sem * w + j], (*chips[j], c)).start()

    def middle(pos, ins, outs, sems):
        x, y, c, chips = pos
        for w, (_, ax, sub, within) in enumerate(items):
            for j in range(2):
                cx, cy = chips[j]
                landed = _window(outs[w], ax, shapes[w], blk=2 * cx + cy, half=c, sub=sub, within=within)
                _remote(landed, landed, sems[0].at[n_sem * w + j], sems[1].at[n_sem * w + j], (cx, cy, c)).wait_recv()
                part = _window(outs[w], ax, shapes[w], blk=2 * cx + cy, half=c, sub=parts(sub)[j], within=within)
                _remote(part, part, sems[0].at[n_sem * w + 2 + j], sems[1].at[n_sem * w + 2 + j],
                        (*chips[1 - j], c)).start()
                _remote(landed, landed, sems[0].at[n_sem * w + 4 + j], sems[1].at[n_sem * w + 4 + j],
                        (x, y, 1 - c)).start()

    def finish(pos, ins, outs, sems):
        x, y, c, chips = pos
        sib = (x, y, 1 - c)
        for w, (_, ax, sub, within) in enumerate(items):
            dx, dy = chips[2]
            for j in range(2):
                part = _window(outs[w], ax, shapes[w], blk=2 * dx + dy, half=c, sub=parts(sub)[j], within=within)
                cp = _remote(part, part, sems[0].at[n_sem * w + 2 + j], sems[1].at[n_sem * w + 2 + j], sib)
                cp.wait_recv()
                cp.wait_send()
            diag = _window(outs[w], ax, shapes[w], blk=2 * dx + dy, half=c, sub=sub, within=within)
            _remote(diag, diag, sems[0].at[n_sem * w + 6], sems[1].at[n_sem * w + 6], sib).start()
        for w, (_, ax, sub, within) in enumerate(items):
            for j, (cx, cy) in enumerate(chips):
                passed = _window(outs[w], ax, shapes[w], blk=2 * cx + cy, half=1 - c, sub=sub, within=within)
                cp = _remote(passed, passed, sems[0].at[n_sem * w + 4 + j], sems[1].at[n_sem * w + 4 + j], sib)
                cp.wait_recv()
                cp.wait_send()
            mine = _window(outs[w], ax, shapes[w], blk=2 * x + y, half=c, sub=sub, within=within)
            for j in range(2):
                _remote(mine, mine, sems[0].at[n_sem * w + j], sems[1].at[n_sem * w + j], sib).wait_send()

    sems = [pltpu.SemaphoreType.DMA((n_sem * nw,)), pltpu.SemaphoreType.DMA((n_sem * nw,))]
    return Job([item[0] for item in items], [_sds(item[0].shape, item[0].dtype) for item in items],
               {w: w for w in range(nw)}, sems, start, middle, finish)


def gather_by_sequencer(name, full, axis, collective_id):
    job = gather_job([(full, axis, WHOLE, WHOLE)])
    ref = jax.new_ref(full, memory_space=pltpu.MemorySpace.HBM)

    @pl.kernel(mesh=plsc.ScalarSubcoreMesh(axis_name="sequencer", num_cores=1), name=name,
               scratch_types=tuple(job.sems), compiler_params=pltpu.CompilerParams(collective_id=collective_id))
    def launch(send_sems, recv_sems):
        pos = _place()
        x, y, c, chips = pos
        barrier = pltpu.get_barrier_semaphore()
        for dev in ((*chips[0], c), (*chips[1], c), (x, y, 1 - c)):
            pl.semaphore_signal(barrier, inc=1, device_id=dev, device_id_type=MESH)
        pl.semaphore_wait(barrier, 3)
        job.start(pos, [ref], [ref], [send_sems, recv_sems])
        job.middle(pos, [ref], [ref], [send_sems, recv_sems])
        job.finish(pos, [ref], [ref], [send_sems, recv_sems])

    launch()
    return ref[...]


def pair_job(grads, axes, is_half=False):
    nw = len(grads)
    shapes = [g.shape for g in grads]

    def start(pos, ins, outs, sems):
        x, y, c, _ = pos
        for w in range(nw):
            src = ins[w] if is_half else _window(ins[w], axes[w], shapes[w], half=1 - c)
            _remote(src, outs[w], sems[0].at[w], sems[1].at[w], (x, y, 1 - c)).start()

    def finish(pos, ins, outs, sems):
        x, y, c, _ = pos
        for w in range(nw):
            cp = _remote(outs[w], outs[w], sems[0].at[w], sems[1].at[w], (x, y, 1 - c))
            cp.wait_recv()
            cp.wait_send()

    sems = [pltpu.SemaphoreType.DMA((nw,)), pltpu.SemaphoreType.DMA((nw,))]
    out_shapes = [_sds(s if is_half else _half_all(s, a), BF16) for s, a in zip(shapes, axes)]
    return Job(list(grads), out_shapes, {}, sems, start, None, finish)


def cross_job(items):
    nw = len(items)
    inputs, aliases = [], {}
    for w, (part, ax, shape, prev, sub) in enumerate(items):
        inputs.append(part)
        if prev is not None:
            aliases[len(inputs)] = w
            inputs.append(prev)

    def copies(pos, ins, outs, sems):
        x, y, c, chips = pos
        k = 0
        for w, (_, ax, shape, prev, sub) in enumerate(items):
            src = ins[k]
            k += 2 if prev is not None else 1
            for j, (cx, cy) in enumerate(chips):
                slot = _window(outs[w].at[j], ax, shape, sub=sub)
                yield (_remote(_window(src, ax, shape, blk=2 * cx + cy, sub=sub), slot,
                               sems[0].at[3 * w + j], sems[1].at[3 * w + j], (cx, cy, c)),
                       _remote(slot, slot, sems[0].at[3 * w + j], sems[1].at[3 * w + j], (cx, cy, c)))

    def start(pos, ins, outs, sems):
        for send, _ in copies(pos, ins, outs, sems):
            send.start()

    def finish(pos, ins, outs, sems):
        for send, recv in copies(pos, ins, outs, sems):
            recv.wait_recv()
            send.wait_send()

    sems = [pltpu.SemaphoreType.DMA((3 * nw,)), pltpu.SemaphoreType.DMA((3 * nw,))]
    out_shapes = [_sds((3,) + _half_block(shape, ax), BF16) for _, ax, shape, _, _ in items]
    return Job(inputs, out_shapes, aliases, sems, start, None, finish)


def final_job(blocks, axes, shapes):
    nw = len(blocks)

    def start(pos, ins, outs, sems):
        x, y, c, _ = pos
        for w in range(nw):
            mine = _window(outs[w], axes[w], shapes[w], half=c)
            _remote(mine, mine, sems[0].at[w], sems[1].at[w], (x, y, 1 - c)).start()

    def finish(pos, ins, outs, sems):
        x, y, c, _ = pos
        for w in range(nw):
            theirs = _window(outs[w], axes[w], shapes[w], half=1 - c)
            cp = _remote(theirs, theirs, sems[0].at[w], sems[1].at[w], (x, y, 1 - c))
            cp.wait_recv()
            cp.wait_send()

    sems = [pltpu.SemaphoreType.DMA((nw,)), pltpu.SemaphoreType.DMA((nw,))]
    return Job(list(blocks), [_sds(b.shape, b.dtype) for b in blocks], {w: w for w in range(nw)}, sems, start, None,
               finish)


def stack_job(small):
    def peers(pos):
        x, y, c, _ = pos
        for k in range(1, 8):
            yield k - 1, (1 - x if k & 4 else x, 1 - y if k & 2 else y, 1 - c if k & 1 else c)

    def start(pos, ins, outs, sems):
        x, y, c, _ = pos
        mine = outs[0].at[4 * x + 2 * y + c]
        pltpu.make_async_copy(ins[0], mine, sems[2]).start()
        for k, dev in peers(pos):
            _remote(ins[0], mine, sems[0].at[k], sems[1].at[k], dev).start()

    def finish(pos, ins, outs, sems):
        x, y, c, _ = pos
        for k, (px, py, pc) in peers(pos):
            slot = outs[0].at[4 * px + 2 * py + pc]
            cp = _remote(slot, slot, sems[0].at[k], sems[1].at[k], (px, py, pc))
            cp.wait_recv()
            cp.wait_send()
        pltpu.make_async_copy(ins[0], outs[0].at[4 * x + 2 * y + c], sems[2]).wait()

    sems = [pltpu.SemaphoreType.DMA((7,)), pltpu.SemaphoreType.DMA((7,)), pltpu.SemaphoreType.DMA]
    return Job([small], [_sds((8,) + small.shape, small.dtype)], {}, sems, start, None, finish)


def columns_job(block):
    cols = block.shape[1]
    place = lambda out, b: out.at[:, _ds(b * cols, cols, True)]

    def start(pos, ins, outs, sems):
        x, y, c, chips = pos
        pltpu.make_async_copy(ins[0], place(outs[0], 2 * x + y), sems[2]).start()
        for j, (cx, cy) in enumerate(chips):
            _remote(ins[0], place(outs[0], 2 * x + y), sems[0].at[j], sems[1].at[j], (cx, cy, c)).start()

    def finish(pos, ins, outs, sems):
        x, y, c, chips = pos
        for j, (cx, cy) in enumerate(chips):
            got = place(outs[0], 2 * cx + cy)
            cp = _remote(got, got, sems[0].at[j], sems[1].at[j], (cx, cy, c))
            cp.wait_recv()
            cp.wait_send()
        pltpu.make_async_copy(ins[0], place(outs[0], 2 * x + y), sems[2]).wait()

    sems = [pltpu.SemaphoreType.DMA((3,)), pltpu.SemaphoreType.DMA((3,)), pltpu.SemaphoreType.DMA]
    return Job([block], [_sds((block.shape[0], N_CHIPS * cols), block.dtype)], {}, sems, start, None, finish)


def _call(name, body, grid, in_specs, out_specs, out_shape, args, scratch=(), sem=None, jobs=(), place=None,
          carried=None):
    n_in, n_out, n_sc = len(args), len(out_shape), len(scratch)
    carried = dict(carried or {})

    def launch(fn, in_specs, out_specs, out_shape, scratch, aliases, sem, operands):
        if place is None:
            return pl.pallas_call(
                fn, name=name, grid=grid, in_specs=in_specs, out_specs=out_specs, out_shape=out_shape,
                scratch_shapes=scratch, input_output_aliases=aliases, compiler_params=_params(sem))(*operands)
        spec = pltpu.PrefetchScalarGridSpec(num_scalar_prefetch=1, grid=grid, in_specs=in_specs,
                                            out_specs=out_specs, scratch_shapes=scratch)
        return pl.pallas_call(
            lambda p_ref, *refs: fn(*refs), name=name, grid_spec=spec, out_shape=out_shape,
            input_output_aliases={k + 1: v for k, v in aliases.items()}, compiler_params=_params(sem),
        )(place, *operands)

    if not jobs:
        outs = launch(body, list(in_specs), list(out_specs), list(out_shape), list(scratch), carried, sem, args)
        return list(outs), []

    total = math.prod(grid) if grid else 1
    mid = min(total - 1, (2 * total) // 3)

    def split(refs, start, counts):
        out = []
        for n in counts:
            out.append(refs[start:start + n])
            start += n
        return out, start

    def wrapped(*refs):
        c_in = refs[:n_in]
        j_ins, p = split(refs, n_in, [len(j.inputs) for j in jobs])
        c_out = refs[p:p + n_out]
        j_outs, p = split(refs, p + n_out, [len(j.out_shapes) for j in jobs])
        c_sc = refs[p:p + n_sc]
        j_sems, p = split(refs, p + n_sc, [len(j.sems) for j in jobs])
        pos = _place()
        step = 0
        for axis, extent in enumerate(grid):
            step = step * extent + pl.program_id(axis)

        def run(phase):
            for j, ins, outs, sems in zip(jobs, j_ins, j_outs, j_sems):
                fn = getattr(j, phase)
                if fn is not None:
                    fn(pos, ins, outs, sems)

        if total == 1:
            run("start")
            body(*c_in, *c_out, *c_sc)
            run("middle")
            run("finish")
            return
        pl.when(step == 0)(lambda: run("start"))
        body(*c_in, *c_out, *c_sc)
        if any(j.middle is not None for j in jobs):
            pl.when(step == mid)(lambda: run("middle"))
        pl.when(step == total - 1)(lambda: run("finish"))

    aliases, in_at, out_at = carried, n_in, n_out
    for j in jobs:
        for src, dst in j.aliases.items():
            aliases[in_at + src] = out_at + dst
        in_at += len(j.inputs)
        out_at += len(j.out_shapes)
    outs = launch(
        wrapped, list(in_specs) + [ANY] * (in_at - n_in), list(out_specs) + [ANY] * (out_at - n_out),
        list(out_shape) + [s for j in jobs for s in j.out_shapes],
        list(scratch) + [s for j in jobs for s in j.sems], aliases, ("arbitrary",) * len(grid),
        [*args, *[a for j in jobs for a in j.inputs]])
    job_outs, p = split(outs, n_out, [len(j.out_shapes) for j in jobs])
    return list(outs[:n_out]), [list(o) for o in job_outs]


def comm_only(name, jobs):
    def body(dummy_ref, out_ref):
        out_ref[...] = dummy_ref[...]

    dummy = jnp.zeros((8, 128), F32)
    spec = pl.BlockSpec((8, 128), lambda: (0, 0))
    return _call(name, body, (), [spec], [spec], [_sds((8, 128), F32)], [dummy], jobs=jobs)[1]


def _ret(outs, job_outs, jobs, single=True):
    res = outs[0] if single else outs
    return (res, job_outs) if jobs else res


def rmsnorm_fwd(name, x, g, jobs=()):
    s, d = x.shape
    tm = _tile(s, 512, 8)

    def body(x_ref, g_ref, n_ref, r_ref):
        xv = x_ref[...]
        r = lax.rsqrt(jnp.mean(xv * xv, axis=-1, keepdims=True) + EPS)
        n_ref[...] = (xv * r * g_ref[...]).astype(BF16)
        r_ref[...] = r

    row = lambda i: (i, 0)
    outs, job_outs = _call(
        name, body, (s // tm,),
        [pl.BlockSpec((tm, d), row), pl.BlockSpec((1, d), lambda i: (0, 0))],
        [pl.BlockSpec((tm, d), row), pl.BlockSpec((tm, 1), row)],
        [_sds((s, d), BF16), _sds((s, 1), F32)], [x, g], sem=("arbitrary",), jobs=jobs)
    return _ret(outs, job_outs, jobs, single=False)


def rmsnorm_bwd(name, dn, x, r, g, dh_in, jobs=()):
    s, d = x.shape
    tm = _tile(s, 512, 8)

    def body(dn_ref, x_ref, r_ref, g_ref, dh_ref, out_ref, outb_ref, dg_ref):
        i = pl.program_id(0)
        xh = x_ref[...] * r_ref[...]
        dnv = dn_ref[...]
        dxh = dnv * g_ref[...]
        dx = r_ref[...] * (dxh - xh * jnp.mean(dxh * xh, axis=-1, keepdims=True))
        out = dh_ref[...] + dx
        out_ref[...] = out
        outb_ref[...] = out.astype(BF16)
        part = jnp.sum(dnv * xh, axis=0, keepdims=True)

        @pl.when(i == 0)
        def _():
            dg_ref[...] = part

        @pl.when(i > 0)
        def _():
            dg_ref[...] += part

    row = lambda i: (i, 0)
    fixed = lambda i: (0, 0)
    outs, job_outs = _call(
        name, body, (s // tm,),
        [pl.BlockSpec((tm, d), row), pl.BlockSpec((tm, d), row), pl.BlockSpec((tm, 1), row),
         pl.BlockSpec((1, d), fixed), pl.BlockSpec((tm, d), row)],
        [pl.BlockSpec((tm, d), row), pl.BlockSpec((tm, d), row), pl.BlockSpec((1, d), fixed)],
        [_sds((s, d), F32), _sds((s, d), BF16), _sds((1, d), F32)], [dn, x, r, g, dh_in],
        sem=("arbitrary",), jobs=jobs)
    return _ret(outs, job_outs, jobs, single=False)


def gain_grad(name, dn_a, dn_b, x, r):
    s, d = x.shape
    tm = _tile(s, 512, 8)

    def body(a_ref, b_ref, x_ref, r_ref, dg_ref):
        i = pl.program_id(0)
        part = jnp.sum((a_ref[...] + b_ref[...]) * (x_ref[...] * r_ref[...]), axis=0, keepdims=True)

        @pl.when(i == 0)
        def _():
            dg_ref[...] = part

        @pl.when(i > 0)
        def _():
            dg_ref[...] += part

    row = lambda i: (i, 0)
    return _call(
        name, body, (s // tm,),
        [pl.BlockSpec((tm, d), row), pl.BlockSpec((tm, d), row), pl.BlockSpec((tm, d), row),
         pl.BlockSpec((tm, 1), row)],
        [pl.BlockSpec((1, d), lambda i: (0, 0))], [_sds((1, d), F32)], [dn_a, dn_b, x, r],
        sem=("arbitrary",))[0][0]


def loss_head(name, h, g, target):
    s, d = h.shape
    tm = _tile(s, 512, 8)
    nsteps = s // tm

    def body(h_ref, g_ref, t_ref, loss_ref, dh_ref, dhb_ref, dg_ref, sq_ref):
        i = pl.program_id(0)
        hv = h_ref[...]
        gv = g_ref[...]
        r = lax.rsqrt(jnp.mean(hv * hv, axis=-1, keepdims=True) + EPS)
        xh = hv * r
        err = xh * gv - t_ref[...]
        dy = err * (1.0 / d)
        dxh = dy * gv
        dh = r * (dxh - xh * jnp.mean(dxh * xh, axis=-1, keepdims=True))
        dh_ref[...] = dh
        dhb_ref[...] = dh.astype(BF16)
        dg_part = jnp.sum(dy * xh, axis=0, keepdims=True)
        sq_part = jnp.sum(err * err, axis=0, keepdims=True)

        @pl.when(i == 0)
        def _():
            dg_ref[...] = dg_part
            sq_ref[...] = sq_part

        @pl.when(i > 0)
        def _():
            dg_ref[...] += dg_part
            sq_ref[...] += sq_part

        @pl.when(i == nsteps - 1)
        def _():
            total = jnp.sum(sq_ref[...], axis=-1, keepdims=True) * (0.5 / d)
            loss_ref[...] = jnp.broadcast_to(total, loss_ref.shape)

    row = lambda i: (i, 0)
    fixed = lambda i: (0, 0)
    return _call(
        name, body, (nsteps,),
        [pl.BlockSpec((tm, d), row), pl.BlockSpec((1, d), fixed), pl.BlockSpec((tm, d), row)],
        [pl.BlockSpec((8, 128), fixed), pl.BlockSpec((tm, d), row), pl.BlockSpec((tm, d), row),
         pl.BlockSpec((1, d), fixed)],
        [_sds((8, 128), F32), _sds((s, d), F32), _sds((s, d), BF16), _sds((1, d), F32)], [h, g, target],
        scratch=[pltpu.VMEM((1, d), F32)], sem=("arbitrary",))[0]


def _mm(name, grid, in_arrays, in_specs, out_shapes, out_specs, acc_tile, dot, epilogue, jobs=(), place=None):
    nk = grid[2]
    n_in = len(in_arrays)
    n_out = len(out_shapes)

    def body(*refs):
        ins, outs = refs[:n_in], refs[n_in:n_in + n_out]
        if nk == 1:
            epilogue(dot(*ins), ins, outs)
            return
        acc = refs[n_in + n_out]
        k = pl.program_id(2)

        @pl.when(k == 0)
        def _():
            acc[...] = dot(*ins)

        @pl.when(jnp.logical_and(k > 0, k < nk - 1))
        def _():
            acc[...] += dot(*ins)

        @pl.when(k == nk - 1)
        def _():
            epilogue(acc[...] + dot(*ins), ins, outs)

    scratch = [pltpu.VMEM(acc_tile, F32)] if nk > 1 else []
    outs, job_outs = _call(name, body, grid, in_specs, out_specs, out_shapes, in_arrays, scratch=scratch,
                           sem=("parallel", "parallel", "arbitrary"), jobs=jobs, place=place)
    return _ret(outs, job_outs, jobs)


def _store(scale, dtype):
    def epilogue(acc, ins, outs):
        outs[0][...] = (acc * scale if scale != 1.0 else acc).astype(dtype)
    return epilogue


def mm_nn(name, a, w, out_dtype, tm=1024, tn=1024, tk=2048, jobs=()):
    m, kd = a.shape
    n = w.shape[1]
    tm, tn, tk = _tile(m, tm, 8), _tile(n, tn), _tile(kd, tk)
    return _mm(
        name, (n // tn, m // tm, kd // tk), [a, w],
        [pl.BlockSpec((tm, tk), lambda j, i, k: (i, k)), pl.BlockSpec((tk, tn), lambda j, i, k: (k, j))],
        [_sds((m, n), out_dtype)], [pl.BlockSpec((tm, tn), lambda j, i, k: (i, j))], (tm, tn),
        lambda a_ref, w_ref: _dot_nn(a_ref[...], w_ref[...]), _store(1.0, out_dtype), jobs)


def mm_nn_resid(name, a, w, x, scale, tm=1024, tn=1024, tk=1408, jobs=()):
    m, kd = a.shape
    n = w.shape[1]
    tm, tn, tk = _tile(m, tm, 8), _tile(n, tn), _tile(kd, tk)

    def epilogue(acc, ins, outs):
        outs[0][...] = ins[2][...] + scale * acc

    return _mm(
        name, (n // tn, m // tm, kd // tk), [a, w, x],
        [pl.BlockSpec((tm, tk), lambda j, i, k: (i, k)), pl.BlockSpec((tk, tn), lambda j, i, k: (k, j)),
         pl.BlockSpec((tm, tn), lambda j, i, k: (i, j))],
        [_sds((m, n), F32)], [pl.BlockSpec((tm, tn), lambda j, i, k: (i, j))], (tm, tn),
        lambda a_ref, w_ref, x_ref: _dot_nn(a_ref[...], w_ref[...]), epilogue, jobs)


def mm_nt(name, a, w, out_dtype, scale=1.0, tm=1024, tn=1024, tk=2048, jobs=()):
    m, kd = a.shape
    n = w.shape[0]
    tm, tn, tk = _tile(m, tm, 8), _tile(n, tn), _tile(kd, tk)
    return _mm(
        name, (n // tn, m // tm, kd // tk), [a, w],
        [pl.BlockSpec((tm, tk), lambda j, i, k: (i, k)), pl.BlockSpec((tn, tk), lambda j, i, k: (j, k))],
        [_sds((m, n), out_dtype)], [pl.BlockSpec((tm, tn), lambda j, i, k: (i, j))], (tm, tn),
        lambda a_ref, w_ref: _dot_nt(a_ref[...], w_ref[...]), _store(scale, out_dtype), jobs)


def mm_nt_pair(name, a3, w, out_dtype, tm=1024, tn=1024, tk=2816, jobs=()):
    _, m, f = a3.shape
    n = w.shape[0]
    tm, tn, tk = _tile(m, tm, 8), _tile(n, tn), _tile(f, tk)
    nkf = f // tk
    return _mm(
        name, (n // tn, m // tm, 2 * nkf), [a3, w],
        [pl.BlockSpec((None, tm, tk), lambda j, i, k: (k // nkf, i, k % nkf)),
         pl.BlockSpec((tn, tk), lambda j, i, k: (j, k))],
        [_sds((m, n), out_dtype)], [pl.BlockSpec((tm, tn), lambda j, i, k: (i, j))], (tm, tn),
        lambda a_ref, w_ref: _dot_nt(a_ref[...], w_ref[...]), _store(1.0, out_dtype), jobs)


def mm_nt_norm_bwd(name, a, w, x, r, g, dh_in, tm=512, tk=1408, jobs=()):
    pair = a.ndim == 3
    m, kd = a.shape[-2], a.shape[-1]
    d = w.shape[0]
    tm, tk = _tile(m, tm, 8), _tile(kd, tk)
    nkf = kd // tk
    nk = 2 * nkf if pair else nkf
    if pair:
        a_spec = pl.BlockSpec((None, tm, tk), lambda i, k: (k // nkf, i, k % nkf))
    else:
        a_spec = pl.BlockSpec((tm, tk), lambda i, k: (i, k))
    row = lambda i, k: (i, 0)
    fixed = lambda i, k: (0, 0)

    def body(a_ref, w_ref, x_ref, r_ref, g_ref, dh_ref, out_ref, outb_ref, dg_ref, *acc):
        i, k = pl.program_id(0), pl.program_id(1)
        dot = lambda: _dot_nt(a_ref[...], w_ref[...])

        def finish(dn):
            xh = x_ref[...] * r_ref[...]
            dxh = dn * g_ref[...]
            out = dh_ref[...] + r_ref[...] * (dxh - xh * jnp.mean(dxh * xh, axis=-1, keepdims=True))
            out_ref[...] = out
            outb_ref[...] = out.astype(BF16)
            part = jnp.sum(dn * xh, axis=0, keepdims=True)

            @pl.when(i == 0)
            def _():
                dg_ref[...] = part

            @pl.when(i > 0)
            def _():
                dg_ref[...] += part

        if nk == 1:
            finish(dot())
            return

        @pl.when(k == 0)
        def _():
            acc[0][...] = dot()

        @pl.when(jnp.logical_and(k > 0, k < nk - 1))
        def _():
            acc[0][...] += dot()

        @pl.when(k == nk - 1)
        def _():
            finish(acc[0][...] + dot())

    outs, job_outs = _call(
        name, body, (m // tm, nk),
        [a_spec, pl.BlockSpec((d, tk), lambda i, k: (0, k)), pl.BlockSpec((tm, d), row), pl.BlockSpec((tm, 1), row),
         pl.BlockSpec((1, d), fixed), pl.BlockSpec((tm, d), row)],
        [pl.BlockSpec((tm, d), row), pl.BlockSpec((tm, d), row), pl.BlockSpec((1, d), fixed)],
        [_sds((m, d), F32), _sds((m, d), BF16), _sds((1, d), F32)], [a, w, x, r, g, dh_in],
        scratch=[pltpu.VMEM((tm, d), F32)] if nk > 1 else [], sem=("arbitrary", "arbitrary"), jobs=jobs)
    return _ret(outs, job_outs, jobs, single=False)


def mm_tn(name, a, b, out_dtype, scale=1.0, tm=1024, tn=1024, tk=4096, jobs=()):
    kd, m = a.shape
    n = b.shape[1]
    tm, tn, tk = _tile(m, tm), _tile(n, tn), _tile(kd, tk, 16)
    return _mm(
        name, (n // tn, m // tm, kd // tk), [a, b],
        [pl.BlockSpec((tk, tm), lambda j, i, k: (k, i)), pl.BlockSpec((tk, tn), lambda j, i, k: (k, j))],
        [_sds((m, n), out_dtype)], [pl.BlockSpec((tm, tn), lambda j, i, k: (i, j))], (tm, tn),
        lambda a_ref, b_ref: _dot_tn(a_ref[...], b_ref[...]), _store(scale, out_dtype), jobs)


def mm_tn_pair(name, a, b3, out_dtype, tm=1024, tn=512, tk=4096, jobs=()):
    kd, m = a.shape
    f = b3.shape[2]
    tm, tn, tk = _tile(m, tm), _tile(f, tn), _tile(kd, tk, 16)
    nf = f // tn
    return _mm(
        name, (m // tm, 2 * nf, kd // tk), [a, b3],
        [pl.BlockSpec((tk, tm), lambda i, j, k: (k, i)),
         pl.BlockSpec((None, tk, tn), lambda i, j, k: (j // nf, k, j % nf))],
        [_sds((m, 2 * f), out_dtype)], [pl.BlockSpec((tm, tn), lambda i, j, k: (i, j))], (tm, tn),
        lambda a_ref, b_ref: _dot_tn(a_ref[...], b_ref[...]), _store(1.0, out_dtype), jobs)


def mm_tn_pair_half(name, a, b3, out_dtype, place, mine, tm=1024, tn=512, tk=4096, jobs=()):
    kd, m = a.shape
    f = b3.shape[2]
    tm, tn, tk = _tile(m // 2, tm), _tile(f, tn), _tile(kd, tk, 16)
    nf, nbm = f // tn, m // 2 // tm
    which = (lambda p: p[1]) if mine else (lambda p: 1 - p[1])
    return _mm(
        name, (nbm, 2 * nf, kd // tk), [a, b3],
        [pl.BlockSpec((tk, tm), lambda i, j, k, p: (k, i + which(p) * nbm)),
         pl.BlockSpec((None, tk, tn), lambda i, j, k, p: (j // nf, k, j % nf))],
        [_sds((m // 2, 2 * f), out_dtype)], [pl.BlockSpec((tm, tn), lambda i, j, k, p: (i, j))], (tm, tn),
        lambda a_ref, b_ref: _dot_tn(a_ref[...], b_ref[...]), _store(1.0, out_dtype), jobs, place)


def swiglu_fwd(name, n, w_in, tm=1024, tn=512, jobs=(), stride=1, phase=0, prev=None):
    s, d = n.shape
    f = w_in.shape[1] // 2
    tm, tn = _tile(s, tm, 8), _tile(f, tn)
    nf = f // tn
    col = lambda j: j * stride + phase

    def body(n_ref, wg_ref, wu_ref, *rest):
        gu_ref, a_ref = rest[-2:]
        nv = n_ref[...]
        g = _dot_nn(nv, wg_ref[...])
        u = _dot_nn(nv, wu_ref[...])
        gu_ref[0] = g.astype(BF16)
        gu_ref[1] = u.astype(BF16)
        a_ref[...] = (g * jax.nn.sigmoid(g) * u).astype(BF16)

    kept = list(prev) if prev is not None else []
    outs, job_outs = _call(
        name, body, (nf // stride, s // tm),
        [pl.BlockSpec((tm, d), lambda j, i: (i, 0)), pl.BlockSpec((d, tn), lambda j, i: (0, col(j))),
         pl.BlockSpec((d, tn), lambda j, i: (0, col(j) + nf))] + [ANY] * len(kept),
        [pl.BlockSpec((2, tm, tn), lambda j, i: (0, i, col(j))), pl.BlockSpec((tm, tn), lambda j, i: (i, col(j)))],
        [_sds((2, s, f), BF16), _sds((s, f), BF16)], [n, w_in, w_in] + kept, sem=("parallel", "parallel"),
        jobs=jobs, carried={3 + k: k for k in range(len(kept))})
    return _ret(outs, job_outs, jobs, single=False)


def swiglu_bwd(name, dh, w_out, gu, scale, tm=1024, tn=512, jobs=()):
    s, d = dh.shape
    f = w_out.shape[0]
    tm, tn = _tile(s, tm, 8), _tile(f, tn)

    sub = _tile(tm, 256, 8)

    def body(dh_ref, w_ref, gu_ref, out_ref):
        for lo in range(0, tm, sub):
            rows = slice(lo, lo + sub)
            da = (_dot_nt(dh_ref[rows, :], w_ref[...]) * scale).astype(BF16)
            g = gu_ref[0, rows, :]
            u = gu_ref[1, rows, :]
            sg = 0.5 * jnp.tanh(0.5 * g) + 0.5
            t = g * sg
            out_ref[0, rows, :] = da * (u * (sg + t * (1.0 - sg)))
            out_ref[1, rows, :] = da * t

    outs, job_outs = _call(
        name, body, (f // tn, s // tm),
        [pl.BlockSpec((tm, d), lambda j, i: (i, 0)), pl.BlockSpec((tn, d), lambda j, i: (j, 0)),
         pl.BlockSpec((2, tm, tn), lambda j, i: (0, i, j))],
        [pl.BlockSpec((2, tm, tn), lambda j, i: (0, i, j))],
        [_sds((2, s, f), BF16)], [dh, w_out, gu], sem=("parallel", "parallel"), jobs=jobs)
    return _ret(outs, job_outs, jobs)


HALO = 16


def _conv_inputs(z_ref, hgc_ref, hhc_ref, i, cw, tm):
    gc = z_ref[:, cw:2 * cw].astype(F32)
    hc = z_ref[:, 2 * cw:3 * cw].astype(F32)
    cin = gc * hc
    halo = hgc_ref[...].astype(F32) * hhc_ref[...].astype(F32) * (i > 0).astype(F32)
    row = lax.broadcasted_iota(jnp.int32, (tm, cw), 0)
    x1 = jnp.where(row == 0, halo[HALO - 1:HALO], pltpu.roll(cin, 1, 0))
    x2 = jnp.where(row == 0, halo[HALO - 2:HALO - 1], jnp.where(row == 1, halo[HALO - 1:HALO], pltpu.roll(cin, 2, 0)))
    return gc, hc, cin, x1, x2


def _tril(w):
    r = lax.broadcasted_iota(jnp.int32, w.shape, 0)
    c = lax.broadcasted_iota(jnp.int32, w.shape, 1)
    return jnp.where(r >= c, w, jnp.zeros_like(w))


def mixer_fwd(name, z, conv_w, conv_b, g_v, w_s, b_t, tm=256, jobs=()):
    s, zc = z.shape
    cw = conv_w.shape[1]
    gw = g_v.shape[1]
    heads = gw // GROUP
    tm = _tile(s, tm)
    hb = tm // HALO

    def body(z_ref, hgc_ref, hhc_ref, cw_ref, cb_ref, gv_ref, ws_ref, bt_ref, y_ref):
        i = pl.program_id(0)
        _, _, cin, x1, x2 = _conv_inputs(z_ref, hgc_ref, hhc_ref, i, cw, tm)
        cv = cb_ref[...] + cw_ref[2:3, :] * cin + cw_ref[1:2, :] * x1 + cw_ref[0:1, :] * x2
        y_ref[:, 0:cw] = (z_ref[:, 0:cw].astype(F32) * cv).astype(BF16)
        for h in range(heads):
            lo = h * GROUP
            vh = z_ref[:, 3 * cw + gw + lo:3 * cw + gw + lo + GROUP].astype(F32)
            rv = lax.rsqrt(jnp.mean(vh * vh, axis=-1, keepdims=True) + EPS)
            vn = (vh * rv * gv_ref[:, lo:lo + GROUP]).astype(BF16)
            w = _tril(ws_ref[h]).astype(BF16)
            for n in range(tm // GROUP):
                rows = slice(n * GROUP, (n + 1) * GROUP)
                sg = _dot_nn(w, vn[rows]) + bt_ref[:, h:h + 1]
                u = z_ref[rows, 3 * cw + lo:3 * cw + lo + GROUP].astype(F32)
                y_ref[rows, cw + lo:cw + lo + GROUP] = (u * sg).astype(BF16)

    fixed2 = lambda i: (0, 0)
    outs, job_outs = _call(
        name, body, (s // tm,),
        [pl.BlockSpec((tm, zc), lambda i: (i, 0)),
         pl.BlockSpec((HALO, cw), lambda i: (jnp.maximum(i * hb - 1, 0), 1)),
         pl.BlockSpec((HALO, cw), lambda i: (jnp.maximum(i * hb - 1, 0), 2)),
         pl.BlockSpec(conv_w.shape, fixed2), pl.BlockSpec(conv_b.shape, fixed2),
         pl.BlockSpec(g_v.shape, fixed2), pl.BlockSpec(w_s.shape, lambda i: (0, 0, 0)),
         pl.BlockSpec(b_t.shape, fixed2)],
        [pl.BlockSpec((tm, cw + gw), lambda i: (i, 0))], [_sds((s, cw + gw), BF16)],
        [z, z, z, conv_w, conv_b, g_v, w_s, b_t], sem=("arbitrary",), jobs=jobs)
    return _ret(outs, job_outs, jobs)


def mixer_bwd(name, z, dy, conv_w, conv_b, g_v, w_s, b_t, tm=256, jobs=()):
    s, zc = z.shape
    cw = conv_w.shape[1]
    gw = g_v.shape[1]
    heads = gw // GROUP
    tm = _tile(s, tm)
    hb = tm // HALO
    nsteps = s // tm
    last_halo = s // HALO - 1

    def body(z_ref, hgc_ref, hhc_ref, ngb_ref, dy_ref, ndy_ref, cw_ref, cb_ref, gv_ref, ws_ref, bt_ref,
             dz_ref, sm_ref, dws_ref, dbt_ref, dsg_ref):
        i = pl.program_id(0)

        @pl.when(i == 0)
        def _():
            sm_ref[...] = jnp.zeros_like(sm_ref)
            dws_ref[...] = jnp.zeros_like(dws_ref)
            dsg_ref[...] = jnp.zeros_like(dsg_ref)

        gc, hc, cin, x1, x2 = _conv_inputs(z_ref, hgc_ref, hhc_ref, i, cw, tm)
        w0, w1, w2 = cw_ref[0:1, :], cw_ref[1:2, :], cw_ref[2:3, :]
        cv = cb_ref[...] + w2 * cin + w1 * x1 + w0 * x2
        gb = z_ref[:, 0:cw].astype(F32)
        dyc = dy_ref[:, 0:cw].astype(F32)
        dz_ref[:, 0:cw] = (dyc * cv).astype(BF16)
        dcv = dyc * gb
        nxt = ndy_ref[...].astype(F32) * ngb_ref[...].astype(F32) * (i < nsteps - 1).astype(F32)
        row = lax.broadcasted_iota(jnp.int32, (tm, cw), 0)
        d1 = jnp.where(row == tm - 1, nxt[0:1], pltpu.roll(dcv, tm - 1, 0))
        d2 = jnp.where(row == tm - 1, nxt[1:2], jnp.where(row == tm - 2, nxt[0:1], pltpu.roll(dcv, tm - 2, 0)))
        dcin = w2 * dcv + w1 * d1 + w0 * d2
        dz_ref[:, cw:2 * cw] = (dcin * hc).astype(BF16)
        dz_ref[:, 2 * cw:3 * cw] = (dcin * gc).astype(BF16)
        sm_ref[0:1, :] += jnp.sum(dcv * x2, axis=0, keepdims=True)
        sm_ref[1:2, :] += jnp.sum(dcv * x1, axis=0, keepdims=True)
        sm_ref[2:3, :] += jnp.sum(dcv * cin, axis=0, keepdims=True)
        sm_ref[3:4, :] += jnp.sum(dcv, axis=0, keepdims=True)

        for h in range(heads):
            lo = h * GROUP
            vcol = slice(3 * cw + gw + lo, 3 * cw + gw + lo + GROUP)
            ucol = slice(3 * cw + lo, 3 * cw + lo + GROUP)
            vh = z_ref[:, vcol].astype(F32)
            rv = lax.rsqrt(jnp.mean(vh * vh, axis=-1, keepdims=True) + EPS)
            xh = vh * rv
            gvh = gv_ref[:, lo:lo + GROUP]
            vn = (xh * gvh).astype(BF16)
            w = _tril(ws_ref[h]).astype(BF16)
            dgv = jnp.zeros((1, GROUP), F32)
            for n in range(tm // GROUP):
                rows = slice(n * GROUP, (n + 1) * GROUP)
                sg = _dot_nn(w, vn[rows]) + bt_ref[:, h:h + 1]
                dyg = dy_ref[rows, cw + lo:cw + lo + GROUP].astype(F32)
                dsg = dyg * z_ref[rows, ucol].astype(F32)
                dz_ref[rows, ucol] = (dyg * sg).astype(BF16)
                dsgb = dsg.astype(BF16)
                dvn = _dot_tn(w, dsgb)
                dws_ref[h] += _dot_nt(dsgb, vn[rows])
                dsg_ref[:, lo:lo + GROUP] += dsg
                xhc = xh[rows]
                dgv = dgv + jnp.sum(dvn * xhc, axis=0, keepdims=True)
                dxh = dvn * gvh
                dv = rv[rows] * (dxh - xhc * jnp.mean(dxh * xhc, axis=-1, keepdims=True))
                dz_ref[rows, vcol] = dv.astype(BF16)
            sm_ref[4:5, lo:lo + GROUP] += dgv

        @pl.when(i == nsteps - 1)
        def _():
            for h in range(heads):
                dws_ref[h] = _tril(dws_ref[h])
                dbt_ref[:, h:h + 1] = jnp.sum(dsg_ref[:, h * GROUP:(h + 1) * GROUP], axis=-1, keepdims=True)

    fixed2 = lambda i: (0, 0)
    fixed3 = lambda i: (0, 0, 0)
    prev = lambda col: (lambda i: (jnp.maximum(i * hb - 1, 0), col))
    nxt_blk = lambda i: (jnp.minimum((i + 1) * hb, last_halo), 0)
    outs, job_outs = _call(
        name, body, (nsteps,),
        [pl.BlockSpec((tm, zc), lambda i: (i, 0)),
         pl.BlockSpec((HALO, cw), prev(1)), pl.BlockSpec((HALO, cw), prev(2)),
         pl.BlockSpec((HALO, cw), nxt_blk),
         pl.BlockSpec((tm, cw + gw), lambda i: (i, 0)), pl.BlockSpec((HALO, cw), nxt_blk),
         pl.BlockSpec(conv_w.shape, fixed2), pl.BlockSpec(conv_b.shape, fixed2),
         pl.BlockSpec(g_v.shape, fixed2), pl.BlockSpec(w_s.shape, fixed3), pl.BlockSpec(b_t.shape, fixed2)],
        [pl.BlockSpec((tm, zc), lambda i: (i, 0)), pl.BlockSpec((8, cw), fixed2),
         pl.BlockSpec(w_s.shape, fixed3), pl.BlockSpec(b_t.shape, fixed2)],
        [_sds((s, zc), BF16), _sds((8, cw), F32), _sds(w_s.shape, F32), _sds(b_t.shape, F32)],
        [z, z, z, z, dy, dy, conv_w, conv_b, g_v, w_s, b_t],
        scratch=[pltpu.VMEM((GROUP, gw), F32)], sem=("arbitrary",), jobs=jobs)
    return _ret(outs, job_outs, jobs, single=False)


def _softmax_rows(sc):
    e = jnp.exp(sc - jnp.max(sc, axis=-1, keepdims=True))
    return e / jnp.sum(e, axis=-1, keepdims=True)


def attn_fwd(name, q, k, v, tm=512, jobs=()):
    s, d = q.shape
    m = k.shape[0]
    hd = d // XA_HEADS
    scale = hd ** -0.5
    tm = _tile(s, tm, 8)

    def body(q_ref, k_ref, v_ref, o_ref):
        for h in range(XA_HEADS):
            cols = slice(h * hd, (h + 1) * hd)
            p = _softmax_rows(_dot_nt(q_ref[:, cols], k_ref[:, cols]) * scale)
            o_ref[:, cols] = _dot_nn(p.astype(BF16), v_ref[:, cols]).astype(BF16)

    outs, job_outs = _call(
        name, body, (s // tm,),
        [pl.BlockSpec((tm, d), lambda i: (i, 0)), pl.BlockSpec((m, d), lambda i: (0, 0)),
         pl.BlockSpec((m, d), lambda i: (0, 0))],
        [pl.BlockSpec((tm, d), lambda i: (i, 0))], [_sds((s, d), BF16)], [q, k, v], sem=("arbitrary",), jobs=jobs)
    return _ret(outs, job_outs, jobs)


def attn_bwd(name, q, k, v, do, tm=512):
    s, d = q.shape
    m = k.shape[0]
    hd = d // XA_HEADS
    scale = hd ** -0.5
    tm = _tile(s, tm, 8)

    def body(q_ref, k_ref, v_ref, do_ref, dq_ref, dk_ref, dv_ref):
        i = pl.program_id(0)

        @pl.when(i == 0)
        def _():
            dk_ref[...] = jnp.zeros_like(dk_ref)
            dv_ref[...] = jnp.zeros_like(dv_ref)

        for h in range(XA_HEADS):
            cols = slice(h * hd, (h + 1) * hd)
            qh = q_ref[:, cols]
            doh = do_ref[:, cols]
            p = _softmax_rows(_dot_nt(qh, k_ref[:, cols]) * scale)
            dp = _dot_nt(doh, v_ref[:, cols])
            ds = (p * (dp - jnp.sum(dp * p, axis=-1, keepdims=True)) * scale).astype(BF16)
            dq_ref[:, cols] = _dot_nn(ds, k_ref[:, cols]).astype(BF16)
            dk_ref[:, cols] += _dot_tn(ds, qh)
            dv_ref[:, cols] += _dot_tn(p.astype(BF16), doh)

    row = lambda i: (i, 0)
    fixed = lambda i: (0, 0)
    return _call(
        name, body, (s // tm,),
        [pl.BlockSpec((tm, d), row), pl.BlockSpec((m, d), fixed), pl.BlockSpec((m, d), fixed),
         pl.BlockSpec((tm, d), row)],
        [pl.BlockSpec((tm, d), row), pl.BlockSpec((m, d), fixed), pl.BlockSpec((m, d), fixed)],
        [_sds((s, d), BF16), _sds((m, d), F32), _sds((m, d), F32)], [q, k, v, do], sem=("arbitrary",))[0]


def _grid2(rows, cols, row_mult):
    tr, tc = _tile(rows, 512, row_mult), _tile(cols, 2048)
    return tr, tc, rows // tr, cols // tc


def cast_place(name, block, axis, place):
    r, c = block.shape
    tr, tc, nbr, nbc = _grid2(r, c, 16)
    if axis == 1:
        dst = lambda i, j, p: (i, j + p[0] * nbc)
    else:
        dst = lambda i, j, p: (i + p[0] * nbr, j)

    def body(p_ref, w_ref, out_ref):
        out_ref[...] = w_ref[...].astype(BF16)

    return pl.pallas_call(
        body, name=name,
        grid_spec=pltpu.PrefetchScalarGridSpec(
            num_scalar_prefetch=1, grid=(nbr, nbc),
            in_specs=[pl.BlockSpec((tr, tc), lambda i, j, p: (i, j))],
            out_specs=pl.BlockSpec((tr, tc), dst)),
        out_shape=_sds(_full_shape(block.shape, axis), BF16),
        compiler_params=_params(("parallel", "parallel")),
    )(place, block)


def pair_add(name, grad, peer, axis, place):
    hr, hc = peer.shape
    tr, tc, nbr, nbc = _grid2(hr, hc, 16)
    same = lambda i, j, p: (i, j)
    if grad.shape == peer.shape:
        mine = same
    elif axis == 1:
        mine = lambda i, j, p: (i + p[1] * nbr, j)
    else:
        mine = lambda i, j, p: (i, j + p[1] * nbc)

    def body(p_ref, g_ref, q_ref, out_ref):
        out_ref[...] = (g_ref[...].astype(F32) + q_ref[...].astype(F32)).astype(BF16)

    return pl.pallas_call(
        body, name=name,
        grid_spec=pltpu.PrefetchScalarGridSpec(
            num_scalar_prefetch=1, grid=(nbr, nbc),
            in_specs=[pl.BlockSpec((tr, tc), mine), pl.BlockSpec((tr, tc), same)],
            out_specs=pl.BlockSpec((tr, tc), same)),
        out_shape=_sds((hr, hc), BF16),
        compiler_params=_params(("parallel", "parallel")),
    )(place, grad, peer)


def cross_sum(name, part, land, axis, shape, place):
    _, sr, sc = land.shape
    tr, tc, nbr, nbc = _grid2(sr, sc, 16)
    if axis == 1:
        own = lambda i, j, p: (i, j + p[0] * nbc)
        dst = lambda i, j, p: (i + p[1] * nbr, j)
    else:
        own = lambda i, j, p: (i + p[0] * nbr, j)
        dst = lambda i, j, p: (i, j + p[1] * nbc)

    def body(p_ref, own_ref, land_ref, out_ref):
        out_ref[...] = ((own_ref[...].astype(F32) + land_ref[0].astype(F32))
                        + (land_ref[1].astype(F32) + land_ref[2].astype(F32)))

    return pl.pallas_call(
        body, name=name,
        grid_spec=pltpu.PrefetchScalarGridSpec(
            num_scalar_prefetch=1, grid=(nbr, nbc),
            in_specs=[pl.BlockSpec((tr, tc), own), pl.BlockSpec((3, tr, tc), lambda i, j, p: (0, i, j))],
            out_specs=pl.BlockSpec((tr, tc), dst)),
        out_shape=_sds(_block(shape, axis), F32),
        compiler_params=_params(("parallel", "parallel")),
    )(place, part, land)


def _adam_math(w, g, m, v):
    m = ADAM_B1 * m + (1.0 - ADAM_B1) * g
    v = ADAM_B2 * v + (1.0 - ADAM_B2) * (g * g)
    m_hat = m / (1.0 - ADAM_B1 ** ADAM_STEP)
    v_hat = v / (1.0 - ADAM_B2 ** ADAM_STEP)
    delta = -ADAM_LR * (m_hat / (jnp.sqrt(v_hat) + ADAM_EPS) + ADAM_WD * w)
    return delta, m, v


def adamw(name, w, g, m, v, jobs=()):
    r, c = w.shape
    tr, tc = _tile(r, 256, 8), _tile(c, 1408)

    def body(w_ref, g_ref, m_ref, v_ref, g_out, d_out, m_out, v_out):
        d, mm, vv = _adam_math(w_ref[...], g_ref[...], m_ref[...], v_ref[...])
        g_out[...] = g_ref[...]
        d_out[...] = d
        m_out[...] = mm
        v_out[...] = vv

    spec = pl.BlockSpec((tr, tc), lambda i, j: (i, j))
    outs, job_outs = _call(name, body, (r // tr, c // tc), [spec] * 4, [spec] * 4, [_sds((r, c), F32)] * 4,
                           [w, g, m, v], sem=("parallel", "parallel"), jobs=jobs)
    return _ret(outs, job_outs, jobs, single=False)


def small_sum(name, stacks):
    def body(*refs):
        for s_ref, out_ref in zip(refs[:len(stacks)], refs[len(stacks):]):
            acc = s_ref[0]
            for d in range(1, s_ref.shape[0]):
                acc = acc + s_ref[d]
            out_ref[...] = acc

    return pl.pallas_call(body, name=name, out_shape=[_sds(s.shape[1:], F32) for s in stacks])(*stacks)


WEIGHTS = ["g_ffn1", "w_ffn1_in", "w_ffn1_out", "g_mix", "w_mix_in", "conv_w", "conv_b", "g_gm_v", "w_spatial",
           "b_spatial", "w_mix_out", "g_xattn", "g_mem", "w_xq", "w_xk", "w_xv", "w_xo", "g_ffn2", "w_ffn2_in",
           "w_ffn2_out", "g_final"]
BIG = {"w_ffn1_in": 1, "w_ffn1_out": 0, "w_mix_in": 1, "w_mix_out": 0, "w_xq": 0, "w_xk": 0, "w_xv": 0, "w_xo": 0,
       "w_ffn2_in": 1, "w_ffn2_out": 0}
SMALL = [n for n in WEIGHTS if n not in BIG]
LATE_SMALL = ["g_ffn1"]
EARLY_SMALL = [n for n in SMALL if n not in LATE_SMALL]


def _pack(arrays):
    flat = jnp.concatenate([a.reshape(-1) for a in arrays])
    rows = -(-flat.shape[0] // 1024) * 8
    return jnp.pad(flat, (0, rows * 128 - flat.shape[0])).reshape(rows, 128)


def _unpack(buf, shapes):
    flat = buf.reshape(-1)
    out, pos = [], 0
    for shp in shapes:
        n = math.prod(shp)
        out.append(flat[pos:pos + n].reshape(shp))
        pos += n
    return out


def kernel(x, mem, g_ffn1, w_ffn1_in, w_ffn1_out, g_mix, w_mix_in, conv_w, conv_b, g_gm_v, w_spatial, b_spatial, w_mix_out, g_xattn, g_mem, w_xq, w_xk, w_xv, w_xo, g_ffn2, w_ffn2_in, w_ffn2_out, g_final, loss_target, m_g_ffn1, m_w_ffn1_in, m_w_ffn1_out, m_g_mix, m_w_mix_in, m_conv_w, m_conv_b, m_g_gm_v, m_w_spatial, m_b_spatial, m_w_mix_out, m_g_xattn, m_g_mem, m_w_xq, m_w_xk, m_w_xv, m_w_xo, m_g_ffn2, m_w_ffn2_in, m_w_ffn2_out, m_g_final, v_g_ffn1, v_w_ffn1_in, v_w_ffn1_out, v_g_mix, v_w_mix_in, v_conv_w, v_conv_b, v_g_gm_v, v_w_spatial, v_b_spatial, v_w_mix_out, v_g_xattn, v_g_mem, v_w_xq, v_w_xk, v_w_xv, v_w_xo, v_g_ffn2, v_w_ffn2_in, v_w_ffn2_out, v_g_final):
    given = dict(locals())
    wts = {n: given[n] for n in WEIGHTS}
    mom = {n: given["m_" + n] for n in WEIGHTS}
    var = {n: given["v_" + n] for n in WEIGHTS}

    xi, yi, ci = lax.axis_index("x"), lax.axis_index("y"), lax.axis_index("c")
    blk = 2 * xi + yi
    place = jnp.stack([blk, ci]).astype(jnp.int32)

    x2, mem2, tgt = x[0], mem[0], loss_target[0]
    own = {n: cast_place("cast_" + n, wts[n][0], BIG[n], place) for n in BIG}
    shape = {n: own[n].shape for n in BIG}
    w_s, b_t = w_spatial[0], b_spatial[0].T
    gf = g_final[None]

    def gather(*names):
        return gather_job([(own[n], BIG[n], WHOLE, WHOLE) for n in names])

    def gather_part(arr, sub=WHOLE, within=WHOLE):
        return gather_job([(arr, 1, sub, within)])

    full = {}
    left, right = (0, 1, 2), (1, 1, 2)
    half_cols = dict(tm=512, tn=shape["w_ffn1_in"][1] // (2 * N_CHIPS), stride=2)

    (w1in,), (conv_taps,) = comm_only(
        "gather_first", [gather_part(own["w_ffn1_in"], within=left),
                         columns_job(jnp.pad(conv_w[0], ((0, 8 - CONV_K), (0, 0))))])
    n1, r1 = rmsnorm_fwd("norm1", x2, g_ffn1)
    halves, ((w1in,),) = swiglu_fwd("ffn1_in_left", n1, w1in, phase=0, jobs=[gather_part(w1in, within=right)],
                                    **half_cols)
    full["w_ffn1_in"] = w1in
    (gu1, a1), ((full["w_ffn1_out"],),) = swiglu_fwd("ffn1_in_right", n1, w1in, phase=1, prev=halves,
                                                     jobs=[gather("w_ffn1_out")], **half_cols)
    h1, ((full["w_mix_in"], full["w_mix_out"]),) = mm_nn_resid(
        "ffn1_out", a1, full["w_ffn1_out"], x2, 0.5, tm=512, tk=5632, jobs=[gather("w_mix_in", "w_mix_out")])
    mine, h1 = lax.optimization_barrier((own["w_ffn2_in"], h1))
    full["w_ffn2_in"] = gather_by_sequencer("gather_w_ffn2_in", mine, 1, 1)
    n2, r2 = rmsnorm_fwd("norm2", h1, g_mix)
    z, ((full["w_xq"], full["w_xk"], full["w_xv"]),) = mm_nn("mix_in", n2, full["w_mix_in"], BF16,
                                                             jobs=[gather("w_xq", "w_xk", "w_xv")])
    ycat, ((full["w_xo"],),) = mixer_fwd("mixer", z, conv_taps, conv_b, g_gm_v, w_s, b_t, jobs=[gather("w_xo")])
    h2 = mm_nn_resid("mix_out", ycat, full["w_mix_out"], h1, 1.0, tk=2048)
    n3, r3 = rmsnorm_fwd("norm3", h2, g_xattn)
    mn, rm = rmsnorm_fwd("norm_mem", mem2, g_mem)
    q = mm_nn("xq", n3, full["w_xq"], BF16)
    k = mm_nn("xk", mn, full["w_xk"], BF16)
    v = mm_nn("xv", mn, full["w_xv"], BF16)
    o = attn_fwd("attn", q, k, v)
    h3 = mm_nn_resid("xo", o, full["w_xo"], h2, 1.0, tk=2048)
    n4, r4 = rmsnorm_fwd("norm4", h3, g_ffn2)
    (gu2, a2), ((full["w_ffn2_out"],),) = swiglu_fwd("ffn2_in", n4, full["w_ffn2_in"], jobs=[gather("w_ffn2_out")])
    h4 = mm_nn_resid("ffn2_out", a2, full["w_ffn2_out"], h3, 0.5, tm=512, tk=5632)
    loss_blk, dh4, dh4b, dg_final = loss_head("loss_head", h4, gf, tgt)

    dw, peer, part, land, half, grads = {}, {}, {}, {}, {}, {}

    def send_pair(*names):
        return pair_job([dw[n] for n in names], [BIG[n] for n in names])

    def take_pair(names, got):
        for n, p in zip(names, got):
            part[n] = pair_add("pair_add_" + n, dw[n], p, BIG[n], place)

    def send_cross(*names, sub=WHOLE):
        return cross_job([(part[n], BIG[n], shape[n], land.get(n), sub) for n in names])

    def take_cross(names, got, last=True):
        for n, l in zip(names, got):
            land[n] = l
            if last:
                half[n] = cross_sum("cross_sum_" + n, part[n], l, BIG[n], shape[n], place)

    def send_final(*names):
        return final_job([half[n] for n in names], [BIG[n] for n in names], [shape[n] for n in names])

    delta, new_m, new_v = {}, {}, {}

    reduced = {}

    def take_final(names, got):
        for n, g in zip(names, got):
            reduced[n] = g

    def update(n, jobs=()):
        res = adamw("adamw_" + n, wts[n][0], reduced[n], mom[n][0], var[n][0], jobs=jobs)
        (grads[n], delta[n], new_m[n], new_v[n]), job_outs = res if jobs else (res, [])
        return job_outs

    dgu2 = swiglu_bwd("ffn2_dact", dh4b, full["w_ffn2_out"], gu2, 0.5)
    dw["w_ffn2_in"] = mm_tn_pair("ffn2_dwin", n4, dgu2, BF16)
    dw["w_ffn2_out"], (got,) = mm_tn("ffn2_dwout", a2, dh4b, BF16, scale=0.5, jobs=[send_pair("w_ffn2_in")])
    take_pair(["w_ffn2_in"], got)
    (dh3, dh3b, dg_ffn2), (got_c, got_p) = mm_nt_norm_bwd(
        "ffn2_dn", dgu2, full["w_ffn2_in"], h3, r4, g_ffn2, dh4,
        jobs=[send_cross("w_ffn2_in", sub=(0, 7, 8)), send_pair("w_ffn2_out")])
    take_cross(["w_ffn2_in"], got_c, last=False)
    take_pair(["w_ffn2_out"], got_p)

    dw["w_xo"], (got_c,) = mm_tn("xo_dw", o, dh3b, BF16, jobs=[send_cross("w_ffn2_in", sub=(7, 1, 8))])
    take_cross(["w_ffn2_in"], got_c)
    do, (got_c, got_f) = mm_nt("xo_dx", dh3b, full["w_xo"], BF16,
                               jobs=[send_cross("w_ffn2_out", sub=(0, 2, 8)), send_final("w_ffn2_in")])
    take_cross(["w_ffn2_out"], got_c, last=False)
    take_final(["w_ffn2_in"], got_f)
    update("w_ffn2_in")
    dq, dk, dv = attn_bwd("attn_bwd", q, k, v, do)
    dkb, dvb = dk.astype(BF16), dv.astype(BF16)
    dw["w_xq"], (got_c,) = mm_tn("xq_dw", n3, dq, BF16, jobs=[send_cross("w_ffn2_out", sub=(2, 2, 8))])
    take_cross(["w_ffn2_out"], got_c, last=False)
    (dh2, dh2b, dg_xattn), (got_c,) = mm_nt_norm_bwd(
        "xq_dx", dq, full["w_xq"], h2, r3, g_xattn, dh3, tk=1024, jobs=[send_cross("w_ffn2_out", sub=(4, 4, 8))])
    take_cross(["w_ffn2_out"], got_c)
    dw["w_xk"] = mm_tn("xk_dw", mn, dkb, BF16)
    dw["w_xv"] = mm_tn("xv_dw", mn, dvb, BF16)
    dmn_k = mm_nt("xk_dx", dkb, full["w_xk"], F32)
    dmn_v = mm_nt("xv_dx", dvb, full["w_xv"], F32)
    dg_mem = gain_grad("norm_mem_bwd", dmn_k, dmn_v, mem2, rm)

    dw["w_mix_out"], (got_f,) = mm_tn("mix_out_dw", ycat, dh2b, BF16, jobs=[send_final("w_ffn2_out")])
    take_final(["w_ffn2_out"], got_f)
    update("w_ffn2_out")
    attn_names = ["w_xo", "w_xq", "w_xk", "w_xv", "w_mix_out"]
    dycat, (got_p,) = mm_nt("mix_out_dx", dh2b, full["w_mix_out"], BF16, jobs=[send_pair(*attn_names)])
    take_pair(attn_names, got_p)
    (dz, dsmall, dws, dbt), (got_c,) = mixer_bwd("mixer_bwd", z, dycat, conv_taps, conv_b, g_gm_v, w_s, b_t,
                                                 jobs=[send_cross("w_xo", "w_xq")])
    take_cross(["w_xo", "w_xq"], got_c)
    dw["w_mix_in"], (got_c,) = mm_tn("mix_in_dw", n2, dz, BF16, jobs=[send_cross("w_xk", "w_xv")])
    take_cross(["w_xk", "w_xv"], got_c)
    (dh1, dh1b, dg_mix), (got_c, got_p) = mm_nt_norm_bwd(
        "mix_in_dx", dz, full["w_mix_in"], h1, r2, g_mix, dh2, tk=1280,
        jobs=[send_cross("w_mix_out"), send_pair("w_mix_in")])
    take_cross(["w_mix_out"], got_c)
    take_pair(["w_mix_in"], got_p)

    dw["w_ffn1_out"], (got_c, got_f) = mm_tn("ffn1_dwout", a1, dh1b, BF16, scale=0.5,
                                             jobs=[send_cross("w_mix_in"), send_final(*attn_names)])
    take_cross(["w_mix_in"], got_c)
    take_final(attn_names, got_f)
    for n in attn_names:
        update(n)
    early = {"g_mix": dg_mix, "conv_w": dsmall[0:CONV_K], "conv_b": dsmall[3:4], "g_gm_v": dsmall[4:5],
             "w_spatial": dws, "b_spatial": dbt.T, "g_xattn": dg_xattn, "g_mem": dg_mem, "g_ffn2": dg_ffn2,
             "g_final": dg_final}
    dgu1, (got_p, got_f, (early_all,)) = swiglu_bwd(
        "ffn1_dact", dh1b, full["w_ffn1_out"], gu1, 0.5,
        jobs=[send_pair("w_ffn1_out"), send_final("w_mix_in"), stack_job(_pack([early[n] for n in EARLY_SMALL]))])
    take_pair(["w_ffn1_out"], got_p)
    take_final(["w_mix_in"], got_f)
    update("w_mix_in")
    theirs, (got_c,) = mm_tn_pair_half("ffn1_dwin_theirs", n1, dgu1, BF16, place, False,
                                       jobs=[send_cross("w_ffn1_out", sub=(0, 7, 8))])
    take_cross(["w_ffn1_out"], got_c, last=False)
    mine, (got_c, (from_sibling,)) = mm_tn_pair_half(
        "ffn1_dwin_mine", n1, dgu1, BF16, place, True,
        jobs=[send_cross("w_ffn1_out", sub=(7, 1, 8)), pair_job([theirs], [1], is_half=True)])
    take_cross(["w_ffn1_out"], got_c)
    part["w_ffn1_in"] = pair_add("pair_add_w_ffn1_in", mine, from_sibling, 1, place)
    dn1, (got_c, got_f) = mm_nt_pair("ffn1_dn", dgu1, full["w_ffn1_in"], F32,
                                     jobs=[send_cross("w_ffn1_in", sub=(0, 7, 8)), send_final("w_ffn1_out")])
    take_cross(["w_ffn1_in"], got_c, last=False)
    take_final(["w_ffn1_out"], got_f)
    update("w_ffn1_out")
    dx, _, dg_ffn1 = rmsnorm_bwd("norm1_bwd", dn1, x2, r1, g_ffn1, dh1)
    got_c, (late_all,) = comm_only("tail_cross", [send_cross("w_ffn1_in", sub=(7, 1, 8)),
                                                  stack_job(_pack([dg_ffn1]))])
    take_cross(["w_ffn1_in"], got_c)
    (got_f,) = comm_only("tail_final", [send_final("w_ffn1_in")])
    take_final(["w_ffn1_in"], got_f)
    update("w_ffn1_in")

    early_sum, late_sum = small_sum("small_sum", [early_all, late_all])
    for n, g in zip(EARLY_SMALL, _unpack(early_sum, [early[n].shape for n in EARLY_SMALL])):
        grads[n] = g
    grads["g_ffn1"] = _unpack(late_sum, [dg_ffn1.shape])[0]
    taps_cols = conv_w.shape[2]
    grads["conv_w"] = lax.dynamic_slice_in_dim(grads["conv_w"], blk * taps_cols, taps_cols, axis=1)
    packed = [_pack([src[n] for n in SMALL]) for src in (wts, grads, mom, var)]
    own_shapes = [wts[n].shape for n in SMALL]
    for dst, buf in zip((delta, new_m, new_v), adamw("adamw_small", *packed)[1:]):
        for n, a in zip(SMALL, _unpack(buf, own_shapes)):
            dst[n] = a

    loss = lax.psum(loss_blk[0, 0], ("x", "y", "c"))
    outs = [loss, dx[None]]
    for group in (grads, delta, new_m, new_v):
        outs += [group[n].reshape(wts[n].shape) for n in WEIGHTS]
    return tuple(outs)
```

```python
import math

import jax
import jax.numpy as jnp
from jax import lax
from jax.experimental import pallas as pl
from jax.experimental.pallas import tpu as pltpu
from jax.experimental.pallas import tpu_sc as plsc

F32 = jnp.float32
BF16 = jnp.bfloat16
EPS = 1e-6
GROUP = 128
XA_HEADS = 4
CONV_K = 3
N_CHIPS = 4
VMEM_LIMIT_BYTES = 56 * 1024 * 1024

ADAM_LR = 0.001
ADAM_B1 = 0.9
ADAM_B2 = 0.999
ADAM_EPS = 1e-08
ADAM_WD = 0.01
ADAM_STEP = 10

MESH = pl.DeviceIdType.MESH
ANY = pl.BlockSpec(memory_space=pl.ANY)


def _tile(dim, pref, mult=128):
    if dim <= pref:
        return dim
    t = (pref // mult) * mult
    while t >= mult:
        if dim % t == 0:
            return t
        t -= mult
    raise ValueError(f"no tile for {dim} under {pref}")


def _params(sem):
    return pltpu.CompilerParams(dimension_semantics=sem, vmem_limit_bytes=VMEM_LIMIT_BYTES)


def _sds(shape, dtype):
    return jax.ShapeDtypeStruct(shape, dtype)


def _dot_nn(a, b):
    return jnp.dot(a, b, preferred_element_type=F32)


def _dot_nt(a, b):
    return lax.dot_general(a, b, (((1,), (1,)), ((), ())), preferred_element_type=F32)


def _dot_tn(a, b):
    return lax.dot_general(a, b, (((0,), (0,)), ((), ())), preferred_element_type=F32)


class Job:
    def __init__(self, inputs, out_shapes, aliases, sems, start, middle, finish):
        self.inputs, self.out_shapes, self.aliases, self.sems = inputs, out_shapes, aliases, sems
        self.start, self.middle, self.finish = start, middle, finish


def _place():
    x, y, c = lax.axis_index("x"), lax.axis_index("y"), lax.axis_index("c")
    chips = [(1 - x, y), (x, 1 - y), (1 - x, 1 - y)]
    return x, y, c, chips


def _ds(start, size, lane):
    if not isinstance(start, int):
        start = pl.multiple_of(start, 128 if lane else 16)
    return pl.ds(start, size)


WHOLE = (0, 1, 1)


def _window(ref, axis, shape, blk=None, half=None, sub=WHOLE, within=WHOLE):
    n = shape[axis] // N_CHIPS
    hs = shape[1 - axis] // 2
    idx = [slice(None), slice(None)]
    if blk is not None:
        b_first, b_count, b_pieces = within
        b_ext = n // b_pieces
        idx[axis] = _ds(blk * n + b_first * b_ext, b_count * b_ext, axis == 1)
    first, count, pieces = sub
    ext = hs // pieces
    if half is not None:
        idx[1 - axis] = _ds(half * hs + first * ext, count * ext, axis == 0)
    elif pieces > 1:
        idx[1 - axis] = _ds(first * ext, count * ext, axis == 0)
    return ref.at[tuple(idx)]


def _remote(src, dst, send_sem, recv_sem, dev):
    return pltpu.make_async_remote_copy(src_ref=src, dst_ref=dst, send_sem=send_sem, recv_sem=recv_sem,
                                        device_id=dev, device_id_type=MESH)


def _full_shape(block_shape, axis):
    out = list(block_shape)
    out[axis] *= N_CHIPS
    return tuple(out)


def _half_all(shape, axis):
    out = list(shape)
    out[1 - axis] //= 2
    return tuple(out)


def _block(shape, axis):
    out = list(shape)
    out[axis] //= N_CHIPS
    return tuple(out)


def _half_block(shape, axis):
    return _half_all(_block(shape, axis), axis)


def gather_job(items):
    nw = len(items)
    shapes = [item[0].shape for item in items]
    n_sem = 8

    def parts(sub):
        first, count, pieces = sub
        return (2 * first, count, 2 * pieces), (2 * first + count, count, 2 * pieces)

    def start(pos, ins, outs, sems):
        x, y, c, chips = pos
        for w, (_, ax, sub, within) in enumerate(items):
            mine = _window(outs[w], ax, shapes[w], blk=2 * x + y, half=c, sub=sub, within=within)
            for j in range(2):
                _remote(mine, mine, sems[0].at[n_sem * w + j], sems[1].at[n_sem * w + j], (*chips[j], c)).start()

    def middle(pos, ins, outs, sems):
        x, y, c, chips = pos
        for w, (_, ax, sub, within) in enumerate(items):
            for j in range(2):
                cx, cy = chips[j]
                landed = _window(outs[w], ax, shapes[w], blk=2 * cx + cy, half=c, sub=sub, within=within)
                _remote(landed, landed, sems[0].at[n_sem * w + j], sems[1].at[n_sem * w + j], (cx, cy, c)).wait_recv()
                part = _window(outs[w], ax, shapes[w], blk=2 * cx + cy, half=c, sub=parts(sub)[j], within=within)
                _remote(part, part, sems[0].at[n_sem * w + 2 + j], sems[1].at[n_sem * w + 2 + j],
                        (*chips[1 - j], c)).start()
                _remote(landed, landed, sems[0].at[n_sem * w + 4 + j], sems[1].at[n_sem * w + 4 + j],
                        (x, y, 1 - c)).start()

    def finish(pos, ins, outs, sems):
        x, y, c, chips = pos
        sib = (x, y, 1 - c)
        for w, (_, ax, sub, within) in enumerate(items):
            dx, dy = chips[2]
            for j in range(2):
                part = _window(outs[w], ax, shapes[w], blk=2 * dx + dy, half=c, sub=parts(sub)[j], within=within)
                cp = _remote(part, part, sems[0].at[n_sem * w + 2 + j], sems[1].at[n_sem * w + 2 + j], sib)
                cp.wait_recv()
                cp.wait_send()
            diag = _window(outs[w], ax, shapes[w], blk=2 * dx + dy, half=c, sub=sub, within=within)
            _remote(diag, diag, sems[0].at[n_sem * w + 6], sems[1].at[n_sem * w + 6], sib).start()
        for w, (_, ax, sub, within) in enumerate(items):
            for j, (cx, cy) in enumerate(chips):
                passed = _window(outs[w], ax, shapes[w], blk=2 * cx + cy, half=1 - c, sub=sub, within=within)
                cp = _remote(passed, passed, sems[0].at[n_sem * w + 4 + j], sems[1].at[n_sem * w + 4 + j], sib)
                cp.wait_recv()
                cp.wait_send()
            mine = _window(outs[w], ax, shapes[w], blk=2 * x + y, half=c, sub=sub, within=within)
            for j in range(2):
                _remote(mine, mine, sems[0].at[n_sem * w + j], sems[1].at[n_sem * w + j], sib).wait_send()

    sems = [pltpu.SemaphoreType.DMA((n_sem * nw,)), pltpu.SemaphoreType.DMA((n_sem * nw,))]
    return Job([item[0] for item in items], [_sds(item[0].shape, item[0].dtype) for item in items],
               {w: w for w in range(nw)}, sems, start, middle, finish)


def gather_by_sequencer(name, fulls, axes, collective_id):
    job = gather_job([(full, axis, WHOLE, WHOLE) for full, axis in zip(fulls, axes)])
    refs = [jax.new_ref(full, memory_space=pltpu.MemorySpace.HBM) for full in fulls]

    @pl.kernel(mesh=plsc.ScalarSubcoreMesh(axis_name="sequencer", num_cores=1), name=name,
               scratch_types=tuple(job.sems), compiler_params=pltpu.CompilerParams(collective_id=collective_id))
    def launch(send_sems, recv_sems):
        pos = _place()
        x, y, c, chips = pos
        barrier = pltpu.get_barrier_semaphore()
        for dev in ((*chips[0], c), (*chips[1], c), (x, y, 1 - c)):
            pl.semaphore_signal(barrier, inc=1, device_id=dev, device_id_type=MESH)
        pl.semaphore_wait(barrier, 3)
        job.start(pos, refs, refs, [send_sems, recv_sems])
        job.middle(pos, refs, refs, [send_sems, recv_sems])
        job.finish(pos, refs, refs, [send_sems, recv_sems])

    launch()
    return [ref[...] for ref in refs]


def pair_job(grads, axes, is_half=False):
    nw = len(grads)
    shapes = [g.shape for g in grads]

    def start(pos, ins, outs, sems):
        x, y, c, _ = pos
        for w in range(nw):
            src = ins[w] if is_half else _window(ins[w], axes[w], shapes[w], half=1 - c)
            _remote(src, outs[w], sems[0].at[w], sems[1].at[w], (x, y, 1 - c)).start()

    def finish(pos, ins, outs, sems):
        x, y, c, _ = pos
        for w in range(nw):
            cp = _remote(outs[w], outs[w], sems[0].at[w], sems[1].at[w], (x, y, 1 - c))
            cp.wait_recv()
            cp.wait_send()

    sems = [pltpu.SemaphoreType.DMA((nw,)), pltpu.SemaphoreType.DMA((nw,))]
    out_shapes = [_sds(s if is_half else _half_all(s, a), BF16) for s, a in zip(shapes, axes)]
    return Job(list(grads), out_shapes, {}, sems, start, None, finish)


def cross_job(items):
    nw = len(items)
    inputs, aliases = [], {}
    for w, (part, ax, shape, prev, sub) in enumerate(items):
        inputs.append(part)
        if prev is not None:
            aliases[len(inputs)] = w
            inputs.append(prev)

    def copies(pos, ins, outs, sems):
        x, y, c, chips = pos
        k = 0
        for w, (_, ax, shape, prev, sub) in enumerate(items):
            src = ins[k]
            k += 2 if prev is not None else 1
            for j, (cx, cy) in enumerate(chips):
                slot = _window(outs[w].at[j], ax, shape, sub=sub)
                yield (_remote(_window(src, ax, shape, blk=2 * cx + cy, sub=sub), slot,
                               sems[0].at[3 * w + j], sems[1].at[3 * w + j], (cx, cy, c)),
                       _remote(slot, slot, sems[0].at[3 * w + j], sems[1].at[3 * w + j], (cx, cy, c)))

    def start(pos, ins, outs, sems):
        for send, _ in copies(pos, ins, outs, sems):
            send.start()

    def finish(pos, ins, outs, sems):
        for send, recv in copies(pos, ins, outs, sems):
            recv.wait_recv()
            send.wait_send()

    sems = [pltpu.SemaphoreType.DMA((3 * nw,)), pltpu.SemaphoreType.DMA((3 * nw,))]
    out_shapes = [_sds((3,) + _half_block(shape, ax), BF16) for _, ax, shape, _, _ in items]
    return Job(inputs, out_shapes, aliases, sems, start, None, finish)


def final_job(blocks, axes, shapes):
    nw = len(blocks)

    def start(pos, ins, outs, sems):
        x, y, c, _ = pos
        for w in range(nw):
            mine = _window(outs[w], axes[w], shapes[w], half=c)
            _remote(mine, mine, sems[0].at[w], sems[1].at[w], (x, y, 1 - c)).start()

    def finish(pos, ins, outs, sems):
        x, y, c, _ = pos
        for w in range(nw):
            theirs = _window(outs[w], axes[w], shapes[w], half=1 - c)
            cp = _remote(theirs, theirs, sems[0].at[w], sems[1].at[w], (x, y, 1 - c))
            cp.wait_recv()
            cp.wait_send()

    sems = [pltpu.SemaphoreType.DMA((nw,)), pltpu.SemaphoreType.DMA((nw,))]
    return Job(list(blocks), [_sds(b.shape, b.dtype) for b in blocks], {w: w for w in range(nw)}, sems, start, None,
               finish)


def stack_job(small):
    def peers(pos):
        x, y, c, _ = pos
        for k in range(1, 8):
            yield k - 1, (1 - x if k & 4 else x, 1 - y if k & 2 else y, 1 - c if k & 1 else c)

    def start(pos, ins, outs, sems):
        x, y, c, _ = pos
        mine = outs[0].at[4 * x + 2 * y + c]
        pltpu.make_async_copy(ins[0], mine, sems[2]).start()
        for k, dev in peers(pos):
            _remote(ins[0], mine, sems[0].at[k], sems[1].at[k], dev).start()

    def finish(pos, ins, outs, sems):
        x, y, c, _ = pos
        for k, (px, py, pc) in peers(pos):
            slot = outs[0].at[4 * px + 2 * py + pc]
            cp = _remote(slot, slot, sems[0].at[k], sems[1].at[k], (px, py, pc))
            cp.wait_recv()
            cp.wait_send()
        pltpu.make_async_copy(ins[0], outs[0].at[4 * x + 2 * y + c], sems[2]).wait()

    sems = [pltpu.SemaphoreType.DMA((7,)), pltpu.SemaphoreType.DMA((7,)), pltpu.SemaphoreType.DMA]
    return Job([small], [_sds((8,) + small.shape, small.dtype)], {}, sems, start, None, finish)


def columns_job(block):
    cols = block.shape[1]
    place = lambda out, b: out.at[:, _ds(b * cols, cols, True)]

    def start(pos, ins, outs, sems):
        x, y, c, chips = pos
        pltpu.make_async_copy(ins[0], place(outs[0], 2 * x + y), sems[2]).start()
        for j, (cx, cy) in enumerate(chips):
            _remote(ins[0], place(outs[0], 2 * x + y), sems[0].at[j], sems[1].at[j], (cx, cy, c)).start()

    def finish(pos, ins, outs, sems):
        x, y, c, chips = pos
        for j, (cx, cy) in enumerate(chips):
            got = place(outs[0], 2 * cx + cy)
            cp = _remote(got, got, sems[0].at[j], sems[1].at[j], (cx, cy, c))
            cp.wait_recv()
            cp.wait_send()
        pltpu.make_async_copy(ins[0], place(outs[0], 2 * x + y), sems[2]).wait()

    sems = [pltpu.SemaphoreType.DMA((3,)), pltpu.SemaphoreType.DMA((3,)), pltpu.SemaphoreType.DMA]
    return Job([block], [_sds((block.shape[0], N_CHIPS * cols), block.dtype)], {}, sems, start, None, finish)


def _call(name, body, grid, in_specs, out_specs, out_shape, args, scratch=(), sem=None, jobs=(), place=None,
          carried=None):
    n_in, n_out, n_sc = len(args), len(out_shape), len(scratch)
    carried = dict(carried or {})

    def launch(fn, in_specs, out_specs, out_shape, scratch, aliases, sem, operands):
        if place is None:
            return pl.pallas_call(
                fn, name=name, grid=grid, in_specs=in_specs, out_specs=out_specs, out_shape=out_shape,
                scratch_shapes=scratch, input_output_aliases=aliases, compiler_params=_params(sem))(*operands)
        spec = pltpu.PrefetchScalarGridSpec(num_scalar_prefetch=1, grid=grid, in_specs=in_specs,
                                            out_specs=out_specs, scratch_shapes=scratch)
        return pl.pallas_call(
            lambda p_ref, *refs: fn(*refs), name=name, grid_spec=spec, out_shape=out_shape,
            input_output_aliases={k + 1: v for k, v in aliases.items()}, compiler_params=_params(sem),
        )(place, *operands)

    if not jobs:
        outs = launch(body, list(in_specs), list(out_specs), list(out_shape), list(scratch), carried, sem, args)
        return list(outs), []

    total = math.prod(grid) if grid else 1
    mid = min(total - 1, (2 * total) // 3)

    def split(refs, start, counts):
        out = []
        for n in counts:
            out.append(refs[start:start + n])
            start += n
        return out, start

    def wrapped(*refs):
        c_in = refs[:n_in]
        j_ins, p = split(refs, n_in, [len(j.inputs) for j in jobs])
        c_out = refs[p:p + n_out]
        j_outs, p = split(refs, p + n_out, [len(j.out_shapes) for j in jobs])
        c_sc = refs[p:p + n_sc]
        j_sems, p = split(refs, p + n_sc, [len(j.sems) for j in jobs])
        pos = _place()
        step = 0
        for axis, extent in enumerate(grid):
            step = step * extent + pl.program_id(axis)

        def run(phase):
            for j, ins, outs, sems in zip(jobs, j_ins, j_outs, j_sems):
                fn = getattr(j, phase)
                if fn is not None:
                    fn(pos, ins, outs, sems)

        if total == 1:
            run("start")
            body(*c_in, *c_out, *c_sc)
            run("middle")
            run("finish")
            return
        pl.when(step == 0)(lambda: run("start"))
        body(*c_in, *c_out, *c_sc)
        if any(j.middle is not None for j in jobs):
            pl.when(step == mid)(lambda: run("middle"))
        pl.when(step == total - 1)(lambda: run("finish"))

    aliases, in_at, out_at = carried, n_in, n_out
    for j in jobs:
        for src, dst in j.aliases.items():
            aliases[in_at + src] = out_at + dst
        in_at += len(j.inputs)
        out_at += len(j.out_shapes)
    outs = launch(
        wrapped, list(in_specs) + [ANY] * (in_at - n_in), list(out_specs) + [ANY] * (out_at - n_out),
        list(out_shape) + [s for j in jobs for s in j.out_shapes],
        list(scratch) + [s for j in jobs for s in j.sems], aliases, ("arbitrary",) * len(grid),
        [*args, *[a for j in jobs for a in j.inputs]])
    job_outs, p = split(outs, n_out, [len(j.out_shapes) for j in jobs])
    return list(outs[:n_out]), [list(o) for o in job_outs]


def comm_only(name, jobs):
    def body(dummy_ref, out_ref):
        out_ref[...] = dummy_ref[...]

    dummy = jnp.zeros((8, 128), F32)
    spec = pl.BlockSpec((8, 128), lambda: (0, 0))
    return _call(name, body, (), [spec], [spec], [_sds((8, 128), F32)], [dummy], jobs=jobs)[1]


def _ret(outs, job_outs, jobs, single=True):
    res = outs[0] if single else outs
    return (res, job_outs) if jobs else res


def rmsnorm_fwd(name, x, g, jobs=()):
    s, d = x.shape
    tm = _tile(s, 512, 8)

    def body(x_ref, g_ref, n_ref, r_ref):
        xv = x_ref[...]
        r = lax.rsqrt(jnp.mean(xv * xv, axis=-1, keepdims=True) + EPS)
        n_ref[...] = (xv * r * g_ref[...]).astype(BF16)
        r_ref[...] = r

    row = lambda i: (i, 0)
    outs, job_outs = _call(
        name, body, (s // tm,),
        [pl.BlockSpec((tm, d), row), pl.BlockSpec((1, d), lambda i: (0, 0))],
        [pl.BlockSpec((tm, d), row), pl.BlockSpec((tm, 1), row)],
        [_sds((s, d), BF16), _sds((s, 1), F32)], [x, g], sem=("arbitrary",), jobs=jobs)
    return _ret(outs, job_outs, jobs, single=False)


def rmsnorm_bwd(name, dn, x, r, g, dh_in, jobs=()):
    s, d = x.shape
    tm = _tile(s, 512, 8)

    def body(dn_ref, x_ref, r_ref, g_ref, dh_ref, out_ref, outb_ref, dg_ref):
        i = pl.program_id(0)
        xh = x_ref[...] * r_ref[...]
        dnv = dn_ref[...]
        dxh = dnv * g_ref[...]
        dx = r_ref[...] * (dxh - xh * jnp.mean(dxh * xh, axis=-1, keepdims=True))
        out = dh_ref[...] + dx
        out_ref[...] = out
        outb_ref[...] = out.astype(BF16)
        part = jnp.sum(dnv * xh, axis=0, keepdims=True)

        @pl.when(i == 0)
        def _():
            dg_ref[...] = part

        @pl.when(i > 0)
        def _():
            dg_ref[...] += part

    row = lambda i: (i, 0)
    fixed = lambda i: (0, 0)
    outs, job_outs = _call(
        name, body, (s // tm,),
        [pl.BlockSpec((tm, d), row), pl.BlockSpec((tm, d), row), pl.BlockSpec((tm, 1), row),
         pl.BlockSpec((1, d), fixed), pl.BlockSpec((tm, d), row)],
        [pl.BlockSpec((tm, d), row), pl.BlockSpec((tm, d), row), pl.BlockSpec((1, d), fixed)],
        [_sds((s, d), F32), _sds((s, d), BF16), _sds((1, d), F32)], [dn, x, r, g, dh_in],
        sem=("arbitrary",), jobs=jobs)
    return _ret(outs, job_outs, jobs, single=False)


def gain_grad(name, dn_a, dn_b, x, r):
    s, d = x.shape
    tm = _tile(s, 512, 8)

    def body(a_ref, b_ref, x_ref, r_ref, dg_ref):
        i = pl.program_id(0)
        part = jnp.sum((a_ref[...] + b_ref[...]) * (x_ref[...] * r_ref[...]), axis=0, keepdims=True)

        @pl.when(i == 0)
        def _():
            dg_ref[...] = part

        @pl.when(i > 0)
        def _():
            dg_ref[...] += part

    row = lambda i: (i, 0)
    return _call(
        name, body, (s // tm,),
        [pl.BlockSpec((tm, d), row), pl.BlockSpec((tm, d), row), pl.BlockSpec((tm, d), row),
         pl.BlockSpec((tm, 1), row)],
        [pl.BlockSpec((1, d), lambda i: (0, 0))], [_sds((1, d), F32)], [dn_a, dn_b, x, r],
        sem=("arbitrary",))[0][0]


def loss_head(name, h, g, target):
    s, d = h.shape
    tm = _tile(s, 512, 8)
    nsteps = s // tm

    def body(h_ref, g_ref, t_ref, loss_ref, dh_ref, dhb_ref, dg_ref, sq_ref):
        i = pl.program_id(0)
        hv = h_ref[...]
        gv = g_ref[...]
        r = lax.rsqrt(jnp.mean(hv * hv, axis=-1, keepdims=True) + EPS)
        xh = hv * r
        err = xh * gv - t_ref[...]
        dy = err * (1.0 / d)
        dxh = dy * gv
        dh = r * (dxh - xh * jnp.mean(dxh * xh, axis=-1, keepdims=True))
        dh_ref[...] = dh
        dhb_ref[...] = dh.astype(BF16)
        dg_part = jnp.sum(dy * xh, axis=0, keepdims=True)
        sq_part = jnp.sum(err * err, axis=0, keepdims=True)

        @pl.when(i == 0)
        def _():
            dg_ref[...] = dg_part
            sq_ref[...] = sq_part

        @pl.when(i > 0)
        def _():
            dg_ref[...] += dg_part
            sq_ref[...] += sq_part

        @pl.when(i == nsteps - 1)
        def _():
            total = jnp.sum(sq_ref[...], axis=-1, keepdims=True) * (0.5 / d)
            loss_ref[...] = jnp.broadcast_to(total, loss_ref.shape)

    row = lambda i: (i, 0)
    fixed = lambda i: (0, 0)
    return _call(
        name, body, (nsteps,),
        [pl.BlockSpec((tm, d), row), pl.BlockSpec((1, d), fixed), pl.BlockSpec((tm, d), row)],
        [pl.BlockSpec((8, 128), fixed), pl.BlockSpec((tm, d), row), pl.BlockSpec((tm, d), row),
         pl.BlockSpec((1, d), fixed)],
        [_sds((8, 128), F32), _sds((s, d), F32), _sds((s, d), BF16), _sds((1, d), F32)], [h, g, target],
        scratch=[pltpu.VMEM((1, d), F32)], sem=("arbitrary",))[0]


def _mm(name, grid, in_arrays, in_specs, out_shapes, out_specs, acc_tile, dot, epilogue, jobs=(), place=None):
    nk = grid[2]
    n_in = len(in_arrays)
    n_out = len(out_shapes)

    def body(*refs):
        ins, outs = refs[:n_in], refs[n_in:n_in + n_out]
        if nk == 1:
            epilogue(dot(*ins), ins, outs)
            return
        acc = refs[n_in + n_out]
        k = pl.program_id(2)

        @pl.when(k == 0)
        def _():
            acc[...] = dot(*ins)

        @pl.when(jnp.logical_and(k > 0, k < nk - 1))
        def _():
            acc[...] += dot(*ins)

        @pl.when(k == nk - 1)
        def _():
            epilogue(acc[...] + dot(*ins), ins, outs)

    scratch = [pltpu.VMEM(acc_tile, F32)] if nk > 1 else []
    outs, job_outs = _call(name, body, grid, in_specs, out_specs, out_shapes, in_arrays, scratch=scratch,
                           sem=("parallel", "parallel", "arbitrary"), jobs=jobs, place=place)
    return _ret(outs, job_outs, jobs)


def _store(scale, dtype):
    def epilogue(acc, ins, outs):
        outs[0][...] = (acc * scale if scale != 1.0 else acc).astype(dtype)
    return epilogue


def mm_nn(name, a, w, out_dtype, tm=1024, tn=1024, tk=2048, jobs=()):
    m, kd = a.shape
    n = w.shape[1]
    tm, tn, tk = _tile(m, tm, 8), _tile(n, tn), _tile(kd, tk)
    return _mm(
        name, (n // tn, m // tm, kd // tk), [a, w],
        [pl.BlockSpec((tm, tk), lambda j, i, k: (i, k)), pl.BlockSpec((tk, tn), lambda j, i, k: (k, j))],
        [_sds((m, n), out_dtype)], [pl.BlockSpec((tm, tn), lambda j, i, k: (i, j))], (tm, tn),
        lambda a_ref, w_ref: _dot_nn(a_ref[...], w_ref[...]), _store(1.0, out_dtype), jobs)


def mm_nn_resid(name, a, w, x, scale, tm=1024, tn=1024, tk=1408, jobs=()):
    m, kd = a.shape
    n = w.shape[1]
    tm, tn, tk = _tile(m, tm, 8), _tile(n, tn), _tile(kd, tk)

    def epilogue(acc, ins, outs):
        outs[0][...] = ins[2][...] + scale * acc

    return _mm(
        name, (n // tn, m // tm, kd // tk), [a, w, x],
        [pl.BlockSpec((tm, tk), lambda j, i, k: (i, k)), pl.BlockSpec((tk, tn), lambda j, i, k: (k, j)),
         pl.BlockSpec((tm, tn), lambda j, i, k: (i, j))],
        [_sds((m, n), F32)], [pl.BlockSpec((tm, tn), lambda j, i, k: (i, j))], (tm, tn),
        lambda a_ref, w_ref, x_ref: _dot_nn(a_ref[...], w_ref[...]), epilogue, jobs)


def mm_nt(name, a, w, out_dtype, scale=1.0, tm=1024, tn=1024, tk=2048, jobs=()):
    m, kd = a.shape
    n = w.shape[0]
    tm, tn, tk = _tile(m, tm, 8), _tile(n, tn), _tile(kd, tk)
    return _mm(
        name, (n // tn, m // tm, kd // tk), [a, w],
        [pl.BlockSpec((tm, tk), lambda j, i, k: (i, k)), pl.BlockSpec((tn, tk), lambda j, i, k: (j, k))],
        [_sds((m, n), out_dtype)], [pl.BlockSpec((tm, tn), lambda j, i, k: (i, j))], (tm, tn),
        lambda a_ref, w_ref: _dot_nt(a_ref[...], w_ref[...]), _store(scale, out_dtype), jobs)


def mm_nt_pair(name, a3, w, out_dtype, tm=1024, tn=1024, tk=2816, jobs=()):
    _, m, f = a3.shape
    n = w.shape[0]
    tm, tn, tk = _tile(m, tm, 8), _tile(n, tn), _tile(f, tk)
    nkf = f // tk
    return _mm(
        name, (n // tn, m // tm, 2 * nkf), [a3, w],
        [pl.BlockSpec((None, tm, tk), lambda j, i, k: (k // nkf, i, k % nkf)),
         pl.BlockSpec((tn, tk), lambda j, i, k: (j, k))],
        [_sds((m, n), out_dtype)], [pl.BlockSpec((tm, tn), lambda j, i, k: (i, j))], (tm, tn),
        lambda a_ref, w_ref: _dot_nt(a_ref[...], w_ref[...]), _store(1.0, out_dtype), jobs)


def mm_nt_norm_bwd(name, a, w, x, r, g, dh_in, tm=512, tk=1408, jobs=()):
    pair = a.ndim == 3
    m, kd = a.shape[-2], a.shape[-1]
    d = w.shape[0]
    tm, tk = _tile(m, tm, 8), _tile(kd, tk)
    nkf = kd // tk
    nk = 2 * nkf if pair else nkf
    if pair:
        a_spec = pl.BlockSpec((None, tm, tk), lambda i, k: (k // nkf, i, k % nkf))
    else:
        a_spec = pl.BlockSpec((tm, tk), lambda i, k: (i, k))
    row = lambda i, k: (i, 0)
    fixed = lambda i, k: (0, 0)

    def body(a_ref, w_ref, x_ref, r_ref, g_ref, dh_ref, out_ref, outb_ref, dg_ref, *acc):
        i, k = pl.program_id(0), pl.program_id(1)
        dot = lambda: _dot_nt(a_ref[...], w_ref[...])

        def finish(dn):
            xh = x_ref[...] * r_ref[...]
            dxh = dn * g_ref[...]
            out = dh_ref[...] + r_ref[...] * (dxh - xh * jnp.mean(dxh * xh, axis=-1, keepdims=True))
            out_ref[...] = out
            outb_ref[...] = out.astype(BF16)
            part = jnp.sum(dn * xh, axis=0, keepdims=True)

            @pl.when(i == 0)
            def _():
                dg_ref[...] = part

            @pl.when(i > 0)
            def _():
                dg_ref[...] += part

        if nk == 1:
            finish(dot())
            return

        @pl.when(k == 0)
        def _():
            acc[0][...] = dot()

        @pl.when(jnp.logical_and(k > 0, k < nk - 1))
        def _():
            acc[0][...] += dot()

        @pl.when(k == nk - 1)
        def _():
            finish(acc[0][...] + dot())

    outs, job_outs = _call(
        name, body, (m // tm, nk),
        [a_spec, pl.BlockSpec((d, tk), lambda i, k: (0, k)), pl.BlockSpec((tm, d), row), pl.BlockSpec((tm, 1), row),
         pl.BlockSpec((1, d), fixed), pl.BlockSpec((tm, d), row)],
        [pl.BlockSpec((tm, d), row), pl.BlockSpec((tm, d), row), pl.BlockSpec((1, d), fixed)],
        [_sds((m, d), F32), _sds((m, d), BF16), _sds((1, d), F32)], [a, w, x, r, g, dh_in],
        scratch=[pltpu.VMEM((tm, d), F32)] if nk > 1 else [], sem=("arbitrary", "arbitrary"), jobs=jobs)
    return _ret(outs, job_outs, jobs, single=False)


def mm_tn(name, a, b, out_dtype, scale=1.0, tm=1024, tn=1024, tk=4096, jobs=()):
    kd, m = a.shape
    n = b.shape[1]
    tm, tn, tk = _tile(m, tm), _tile(n, tn), _tile(kd, tk, 16)
    return _mm(
        name, (n // tn, m // tm, kd // tk), [a, b],
        [pl.BlockSpec((tk, tm), lambda j, i, k: (k, i)), pl.BlockSpec((tk, tn), lambda j, i, k: (k, j))],
        [_sds((m, n), out_dtype)], [pl.BlockSpec((tm, tn), lambda j, i, k: (i, j))], (tm, tn),
        lambda a_ref, b_ref: _dot_tn(a_ref[...], b_ref[...]), _store(scale, out_dtype), jobs)


def mm_tn_pair(name, a, b3, out_dtype, tm=1024, tn=512, tk=4096, jobs=()):
    kd, m = a.shape
    f = b3.shape[2]
    tm, tn, tk = _tile(m, tm), _tile(f, tn), _tile(kd, tk, 16)
    nf = f // tn
    return _mm(
        name, (m // tm, 2 * nf, kd // tk), [a, b3],
        [pl.BlockSpec((tk, tm), lambda i, j, k: (k, i)),
         pl.BlockSpec((None, tk, tn), lambda i, j, k: (j // nf, k, j % nf))],
        [_sds((m, 2 * f), out_dtype)], [pl.BlockSpec((tm, tn), lambda i, j, k: (i, j))], (tm, tn),
        lambda a_ref, b_ref: _dot_tn(a_ref[...], b_ref[...]), _store(1.0, out_dtype), jobs)


def mm_tn_pair_half(name, a, b3, out_dtype, place, mine, tm=1024, tn=512, tk=4096, jobs=()):
    kd, m = a.shape
    f = b3.shape[2]
    tm, tn, tk = _tile(m // 2, tm), _tile(f, tn), _tile(kd, tk, 16)
    nf, nbm = f // tn, m // 2 // tm
    which = (lambda p: p[1]) if mine else (lambda p: 1 - p[1])
    return _mm(
        name, (nbm, 2 * nf, kd // tk), [a, b3],
        [pl.BlockSpec((tk, tm), lambda i, j, k, p: (k, i + which(p) * nbm)),
         pl.BlockSpec((None, tk, tn), lambda i, j, k, p: (j // nf, k, j % nf))],
        [_sds((m // 2, 2 * f), out_dtype)], [pl.BlockSpec((tm, tn), lambda i, j, k, p: (i, j))], (tm, tn),
        lambda a_ref, b_ref: _dot_tn(a_ref[...], b_ref[...]), _store(1.0, out_dtype), jobs, place)


def swiglu_fwd(name, n, w_in, tm=1024, tn=512, jobs=(), stride=1, phase=0, prev=None, compact=False):
    s, d = n.shape
    f = w_in.shape[1] // 2 * (stride if compact else 1)
    tm, tn = _tile(s, tm, 8), _tile(f, tn)
    nf = f // tn
    col = lambda j: j * stride + phase
    w_gate = (lambda j: j) if compact else col
    w_up = (lambda j: j + nf // stride) if compact else (lambda j: col(j) + nf)

    def body(n_ref, wg_ref, wu_ref, *rest):
        gu_ref, a_ref = rest[-2:]
        nv = n_ref[...]
        g = _dot_nn(nv, wg_ref[...])
        u = _dot_nn(nv, wu_ref[...])
        gu_ref[0] = g.astype(BF16)
        gu_ref[1] = u.astype(BF16)
        a_ref[...] = (g * jax.nn.sigmoid(g) * u).astype(BF16)

    kept = list(prev) if prev is not None else []
    outs, job_outs = _call(
        name, body, (nf // stride, s // tm),
        [pl.BlockSpec((tm, d), lambda j, i: (i, 0)), pl.BlockSpec((d, tn), lambda j, i: (0, w_gate(j))),
         pl.BlockSpec((d, tn), lambda j, i: (0, w_up(j)))] + [ANY] * len(kept),
        [pl.BlockSpec((2, tm, tn), lambda j, i: (0, i, col(j))), pl.BlockSpec((tm, tn), lambda j, i: (i, col(j)))],
        [_sds((2, s, f), BF16), _sds((s, f), BF16)], [n, w_in, w_in] + kept, sem=("parallel", "parallel"),
        jobs=jobs, carried={3 + k: k for k in range(len(kept))})
    return _ret(outs, job_outs, jobs, single=False)


def swiglu_bwd(name, dh, w_out, gu, scale, tm=1024, tn=512, jobs=()):
    s, d = dh.shape
    f = w_out.shape[0]
    tm, tn = _tile(s, tm, 8), _tile(f, tn)

    sub = _tile(tm, 256, 8)

    def body(dh_ref, w_ref, gu_ref, out_ref):
        for lo in range(0, tm, sub):
            rows = slice(lo, lo + sub)
            da = (_dot_nt(dh_ref[rows, :], w_ref[...]) * scale).astype(BF16)
            g = gu_ref[0, rows, :]
            u = gu_ref[1, rows, :]
            sg = 0.5 * jnp.tanh(0.5 * g) + 0.5
            t = g * sg
            out_ref[0, rows, :] = da * (u * (sg + t * (1.0 - sg)))
            out_ref[1, rows, :] = da * t

    outs, job_outs = _call(
        name, body, (f // tn, s // tm),
        [pl.BlockSpec((tm, d), lambda j, i: (i, 0)), pl.BlockSpec((tn, d), lambda j, i: (j, 0)),
         pl.BlockSpec((2, tm, tn), lambda j, i: (0, i, j))],
        [pl.BlockSpec((2, tm, tn), lambda j, i: (0, i, j))],
        [_sds((2, s, f), BF16)], [dh, w_out, gu], sem=("parallel", "parallel"), jobs=jobs)
    return _ret(outs, job_outs, jobs)


HALO = 16


def _conv_inputs(z_ref, hgc_ref, hhc_ref, i, cw, tm):
    gc = z_ref[:, cw:2 * cw].astype(F32)
    hc = z_ref[:, 2 * cw:3 * cw].astype(F32)
    cin = gc * hc
    halo = hgc_ref[...].astype(F32) * hhc_ref[...].astype(F32) * (i > 0).astype(F32)
    row = lax.broadcasted_iota(jnp.int32, (tm, cw), 0)
    x1 = jnp.where(row == 0, halo[HALO - 1:HALO], pltpu.roll(cin, 1, 0))
    x2 = jnp.where(row == 0, halo[HALO - 2:HALO - 1], jnp.where(row == 1, halo[HALO - 1:HALO], pltpu.roll(cin, 2, 0)))
    return gc, hc, cin, x1, x2


def _tril(w):
    r = lax.broadcasted_iota(jnp.int32, w.shape, 0)
    c = lax.broadcasted_iota(jnp.int32, w.shape, 1)
    return jnp.where(r >= c, w, jnp.zeros_like(w))


def mixer_fwd(name, z, conv_w, conv_b, g_v, w_s, b_t, tm=256, jobs=()):
    s, zc = z.shape
    cw = conv_w.shape[1]
    gw = g_v.shape[1]
    heads = gw // GROUP
    tm = _tile(s, tm)
    hb = tm // HALO

    def body(z_ref, hgc_ref, hhc_ref, cw_ref, cb_ref, gv_ref, ws_ref, bt_ref, y_ref):
        i = pl.program_id(0)
        _, _, cin, x1, x2 = _conv_inputs(z_ref, hgc_ref, hhc_ref, i, cw, tm)
        cv = cb_ref[...] + cw_ref[2:3, :] * cin + cw_ref[1:2, :] * x1 + cw_ref[0:1, :] * x2
        y_ref[:, 0:cw] = (z_ref[:, 0:cw].astype(F32) * cv).astype(BF16)
        for h in range(heads):
            lo = h * GROUP
            vh = z_ref[:, 3 * cw + gw + lo:3 * cw + gw + lo + GROUP].astype(F32)
            rv = lax.rsqrt(jnp.mean(vh * vh, axis=-1, keepdims=True) + EPS)
            vn = (vh * rv * gv_ref[:, lo:lo + GROUP]).astype(BF16)
            w = _tril(ws_ref[h]).astype(BF16)
            for n in range(tm // GROUP):
                rows = slice(n * GROUP, (n + 1) * GROUP)
                sg = _dot_nn(w, vn[rows]) + bt_ref[:, h:h + 1]
                u = z_ref[rows, 3 * cw + lo:3 * cw + lo + GROUP].astype(F32)
                y_ref[rows, cw + lo:cw + lo + GROUP] = (u * sg).astype(BF16)

    fixed2 = lambda i: (0, 0)
    outs, job_outs = _call(
        name, body, (s // tm,),
        [pl.BlockSpec((tm, zc), lambda i: (i, 0)),
         pl.BlockSpec((HALO, cw), lambda i: (jnp.maximum(i * hb - 1, 0), 1)),
         pl.BlockSpec((HALO, cw), lambda i: (jnp.maximum(i * hb - 1, 0), 2)),
         pl.BlockSpec(conv_w.shape, fixed2), pl.BlockSpec(conv_b.shape, fixed2),
         pl.BlockSpec(g_v.shape, fixed2), pl.BlockSpec(w_s.shape, lambda i: (0, 0, 0)),
         pl.BlockSpec(b_t.shape, fixed2)],
        [pl.BlockSpec((tm, cw + gw), lambda i: (i, 0))], [_sds((s, cw + gw), BF16)],
        [z, z, z, conv_w, conv_b, g_v, w_s, b_t], sem=("arbitrary",), jobs=jobs)
    return _ret(outs, job_outs, jobs)


def mixer_bwd(name, z, dy, conv_w, conv_b, g_v, w_s, b_t, tm=256, jobs=()):
    s, zc = z.shape
    cw = conv_w.shape[1]
    gw = g_v.shape[1]
    heads = gw // GROUP
    tm = _tile(s, tm)
    hb = tm // HALO
    nsteps = s // tm
    last_halo = s // HALO - 1

    def body(z_ref, hgc_ref, hhc_ref, ngb_ref, dy_ref, ndy_ref, cw_ref, cb_ref, gv_ref, ws_ref, bt_ref,
             dz_ref, sm_ref, dws_ref, dbt_ref, dsg_ref):
        i = pl.program_id(0)

        @pl.when(i == 0)
        def _():
            sm_ref[...] = jnp.zeros_like(sm_ref)
            dws_ref[...] = jnp.zeros_like(dws_ref)
            dsg_ref[...] = jnp.zeros_like(dsg_ref)

        gc, hc, cin, x1, x2 = _conv_inputs(z_ref, hgc_ref, hhc_ref, i, cw, tm)
        w0, w1, w2 = cw_ref[0:1, :], cw_ref[1:2, :], cw_ref[2:3, :]
        cv = cb_ref[...] + w2 * cin + w1 * x1 + w0 * x2
        gb = z_ref[:, 0:cw].astype(F32)
        dyc = dy_ref[:, 0:cw].astype(F32)
        dz_ref[:, 0:cw] = (dyc * cv).astype(BF16)
        dcv = dyc * gb
        nxt = ndy_ref[...].astype(F32) * ngb_ref[...].astype(F32) * (i < nsteps - 1).astype(F32)
        row = lax.broadcasted_iota(jnp.int32, (tm, cw), 0)
        d1 = jnp.where(row == tm - 1, nxt[0:1], pltpu.roll(dcv, tm - 1, 0))
        d2 = jnp.where(row == tm - 1, nxt[1:2], jnp.where(row == tm - 2, nxt[0:1], pltpu.roll(dcv, tm - 2, 0)))
        dcin = w2 * dcv + w1 * d1 + w0 * d2
        dz_ref[:, cw:2 * cw] = (dcin * hc).astype(BF16)
        dz_ref[:, 2 * cw:3 * cw] = (dcin * gc).astype(BF16)
        sm_ref[0:1, :] += jnp.sum(dcv * x2, axis=0, keepdims=True)
        sm_ref[1:2, :] += jnp.sum(dcv * x1, axis=0, keepdims=True)
        sm_ref[2:3, :] += jnp.sum(dcv * cin, axis=0, keepdims=True)
        sm_ref[3:4, :] += jnp.sum(dcv, axis=0, keepdims=True)

        for h in range(heads):
            lo = h * GROUP
            vcol = slice(3 * cw + gw + lo, 3 * cw + gw + lo + GROUP)
            ucol = slice(3 * cw + lo, 3 * cw + lo + GROUP)
            vh = z_ref[:, vcol].astype(F32)
            rv = lax.rsqrt(jnp.mean(vh * vh, axis=-1, keepdims=True) + EPS)
            xh = vh * rv
            gvh = gv_ref[:, lo:lo + GROUP]
            vn = (xh * gvh).astype(BF16)
            w = _tril(ws_ref[h]).astype(BF16)
            dgv = jnp.zeros((1, GROUP), F32)
            for n in range(tm // GROUP):
                rows = slice(n * GROUP, (n + 1) * GROUP)
                sg = _dot_nn(w, vn[rows]) + bt_ref[:, h:h + 1]
                dyg = dy_ref[rows, cw + lo:cw + lo + GROUP].astype(F32)
                dsg = dyg * z_ref[rows, ucol].astype(F32)
                dz_ref[rows, ucol] = (dyg * sg).astype(BF16)
                dsgb = dsg.astype(BF16)
                dvn = _dot_tn(w, dsgb)
                dws_ref[h] += _dot_nt(dsgb, vn[rows])
                dsg_ref[:, lo:lo + GROUP] += dsg
                xhc = xh[rows]
                dgv = dgv + jnp.sum(dvn * xhc, axis=0, keepdims=True)
                dxh = dvn * gvh
                dv = rv[rows] * (dxh - xhc * jnp.mean(dxh * xhc, axis=-1, keepdims=True))
                dz_ref[rows, vcol] = dv.astype(BF16)
            sm_ref[4:5, lo:lo + GROUP] += dgv

        @pl.when(i == nsteps - 1)
        def _():
            for h in range(heads):
                dws_ref[h] = _tril(dws_ref[h])
                dbt_ref[:, h:h + 1] = jnp.sum(dsg_ref[:, h * GROUP:(h + 1) * GROUP], axis=-1, keepdims=True)

    fixed2 = lambda i: (0, 0)
    fixed3 = lambda i: (0, 0, 0)
    prev = lambda col: (lambda i: (jnp.maximum(i * hb - 1, 0), col))
    nxt_blk = lambda i: (jnp.minimum((i + 1) * hb, last_halo), 0)
    outs, job_outs = _call(
        name, body, (nsteps,),
        [pl.BlockSpec((tm, zc), lambda i: (i, 0)),
         pl.BlockSpec((HALO, cw), prev(1)), pl.BlockSpec((HALO, cw), prev(2)),
         pl.BlockSpec((HALO, cw), nxt_blk),
         pl.BlockSpec((tm, cw + gw), lambda i: (i, 0)), pl.BlockSpec((HALO, cw), nxt_blk),
         pl.BlockSpec(conv_w.shape, fixed2), pl.BlockSpec(conv_b.shape, fixed2),
         pl.BlockSpec(g_v.shape, fixed2), pl.BlockSpec(w_s.shape, fixed3), pl.BlockSpec(b_t.shape, fixed2)],
        [pl.BlockSpec((tm, zc), lambda i: (i, 0)), pl.BlockSpec((8, cw), fixed2),
         pl.BlockSpec(w_s.shape, fixed3), pl.BlockSpec(b_t.shape, fixed2)],
        [_sds((s, zc), BF16), _sds((8, cw), F32), _sds(w_s.shape, F32), _sds(b_t.shape, F32)],
        [z, z, z, z, dy, dy, conv_w, conv_b, g_v, w_s, b_t],
        scratch=[pltpu.VMEM((GROUP, gw), F32)], sem=("arbitrary",), jobs=jobs)
    return _ret(outs, job_outs, jobs, single=False)


def _softmax_rows(sc):
    e = jnp.exp(sc - jnp.max(sc, axis=-1, keepdims=True))
    return e / jnp.sum(e, axis=-1, keepdims=True)


def attn_fwd(name, q, k, v, tm=512, jobs=()):
    s, d = q.shape
    m = k.shape[0]
    hd = d // XA_HEADS
    scale = hd ** -0.5
    tm = _tile(s, tm, 8)

    def body(q_ref, k_ref, v_ref, o_ref):
        for h in range(XA_HEADS):
            cols = slice(h * hd, (h + 1) * hd)
            p = _softmax_rows(_dot_nt(q_ref[:, cols], k_ref[:, cols]) * scale)
            o_ref[:, cols] = _dot_nn(p.astype(BF16), v_ref[:, cols]).astype(BF16)

    outs, job_outs = _call(
        name, body, (s // tm,),
        [pl.BlockSpec((tm, d), lambda i: (i, 0)), pl.BlockSpec((m, d), lambda i: (0, 0)),
         pl.BlockSpec((m, d), lambda i: (0, 0))],
        [pl.BlockSpec((tm, d), lambda i: (i, 0))], [_sds((s, d), BF16)], [q, k, v], sem=("arbitrary",), jobs=jobs)
    return _ret(outs, job_outs, jobs)


def attn_bwd(name, q, k, v, do, tm=512):
    s, d = q.shape
    m = k.shape[0]
    hd = d // XA_HEADS
    scale = hd ** -0.5
    tm = _tile(s, tm, 8)

    def body(q_ref, k_ref, v_ref, do_ref, dq_ref, dk_ref, dv_ref):
        i = pl.program_id(0)

        @pl.when(i == 0)
        def _():
            dk_ref[...] = jnp.zeros_like(dk_ref)
            dv_ref[...] = jnp.zeros_like(dv_ref)

        for h in range(XA_HEADS):
            cols = slice(h * hd, (h + 1) * hd)
            qh = q_ref[:, cols]
            doh = do_ref[:, cols]
            p = _softmax_rows(_dot_nt(qh, k_ref[:, cols]) * scale)
            dp = _dot_nt(doh, v_ref[:, cols])
            ds = (p * (dp - jnp.sum(dp * p, axis=-1, keepdims=True)) * scale).astype(BF16)
            dq_ref[:, cols] = _dot_nn(ds, k_ref[:, cols]).astype(BF16)
            dk_ref[:, cols] += _dot_tn(ds, qh)
            dv_ref[:, cols] += _dot_tn(p.astype(BF16), doh)

    row = lambda i: (i, 0)
    fixed = lambda i: (0, 0)
    return _call(
        name, body, (s // tm,),
        [pl.BlockSpec((tm, d), row), pl.BlockSpec((m, d), fixed), pl.BlockSpec((m, d), fixed),
         pl.BlockSpec((tm, d), row)],
        [pl.BlockSpec((tm, d), row), pl.BlockSpec((m, d), fixed), pl.BlockSpec((m, d), fixed)],
        [_sds((s, d), BF16), _sds((m, d), F32), _sds((m, d), F32)], [q, k, v, do], sem=("arbitrary",))[0]


def _grid2(rows, cols, row_mult):
    tr, tc = _tile(rows, 512, row_mult), _tile(cols, 2048)
    return tr, tc, rows // tr, cols // tc


def cast_place(name, block, axis, place, column_half=None):
    r, c = block.shape
    if column_half is not None:
        c //= 2
    tr, tc, nbr, nbc = _grid2(r, c, 16)
    first = 0 if column_half is None else column_half * nbc
    if axis == 1:
        dst = lambda i, j, p: (i, j + p[0] * nbc)
    else:
        dst = lambda i, j, p: (i + p[0] * nbr, j)

    def body(p_ref, w_ref, out_ref):
        out_ref[...] = w_ref[...].astype(BF16)

    return pl.pallas_call(
        body, name=name,
        grid_spec=pltpu.PrefetchScalarGridSpec(
            num_scalar_prefetch=1, grid=(nbr, nbc),
            in_specs=[pl.BlockSpec((tr, tc), lambda i, j, p: (i, j + first))],
            out_specs=pl.BlockSpec((tr, tc), dst)),
        out_shape=_sds(_full_shape((r, c), axis), BF16),
        compiler_params=_params(("parallel", "parallel")),
    )(place, block)


def merge_column_halves(name, left, right):
    r, c = left.shape
    w = c // N_CHIPS
    tr = _tile(r, 512, 16)

    def body(l_ref, r_ref, out_ref):
        side = pl.program_id(1) % 2

        @pl.when(side == 0)
        def _():
            out_ref[...] = l_ref[...]

        @pl.when(side == 1)
        def _():
            out_ref[...] = r_ref[...]

    half = pl.BlockSpec((tr, w), lambda i, j: (i, j // 2))
    return _call(name, body, (r // tr, 2 * N_CHIPS), [half, half], [pl.BlockSpec((tr, w), lambda i, j: (i, j))],
                 [_sds((r, 2 * c), left.dtype)], [left, right], sem=("parallel", "arbitrary"))[0][0]


def pair_add(name, grad, peer, axis, place):
    hr, hc = peer.shape
    tr, tc, nbr, nbc = _grid2(hr, hc, 16)
    same = lambda i, j, p: (i, j)
    if grad.shape == peer.shape:
        mine = same
    elif axis == 1:
        mine = lambda i, j, p: (i + p[1] * nbr, j)
    else:
        mine = lambda i, j, p: (i, j + p[1] * nbc)

    def body(p_ref, g_ref, q_ref, out_ref):
        out_ref[...] = (g_ref[...].astype(F32) + q_ref[...].astype(F32)).astype(BF16)

    return pl.pallas_call(
        body, name=name,
        grid_spec=pltpu.PrefetchScalarGridSpec(
            num_scalar_prefetch=1, grid=(nbr, nbc),
            in_specs=[pl.BlockSpec((tr, tc), mine), pl.BlockSpec((tr, tc), same)],
            out_specs=pl.BlockSpec((tr, tc), same)),
        out_shape=_sds((hr, hc), BF16),
        compiler_params=_params(("parallel", "parallel")),
    )(place, grad, peer)


def cross_sum(name, part, land, axis, shape, place):
    _, sr, sc = land.shape
    tr, tc, nbr, nbc = _grid2(sr, sc, 16)
    if axis == 1:
        own = lambda i, j, p: (i, j + p[0] * nbc)
        dst = lambda i, j, p: (i + p[1] * nbr, j)
    else:
        own = lambda i, j, p: (i + p[0] * nbr, j)
        dst = lambda i, j, p: (i, j + p[1] * nbc)

    def body(p_ref, own_ref, land_ref, out_ref):
        out_ref[...] = ((own_ref[...].astype(F32) + land_ref[0].astype(F32))
                        + (land_ref[1].astype(F32) + land_ref[2].astype(F32)))

    return pl.pallas_call(
        body, name=name,
        grid_spec=pltpu.PrefetchScalarGridSpec(
            num_scalar_prefetch=1, grid=(nbr, nbc),
            in_specs=[pl.BlockSpec((tr, tc), own), pl.BlockSpec((3, tr, tc), lambda i, j, p: (0, i, j))],
            out_specs=pl.BlockSpec((tr, tc), dst)),
        out_shape=_sds(_block(shape, axis), F32),
        compiler_params=_params(("parallel", "parallel")),
    )(place, part, land)


def _adam_math(w, g, m, v):
    m = ADAM_B1 * m + (1.0 - ADAM_B1) * g
    v = ADAM_B2 * v + (1.0 - ADAM_B2) * (g * g)
    m_hat = m / (1.0 - ADAM_B1 ** ADAM_STEP)
    v_hat = v / (1.0 - ADAM_B2 ** ADAM_STEP)
    delta = -ADAM_LR * (m_hat / (jnp.sqrt(v_hat) + ADAM_EPS) + ADAM_WD * w)
    return delta, m, v


def adamw(name, w, g, m, v, jobs=()):
    r, c = w.shape
    tr, tc = _tile(r, 256, 8), _tile(c, 1408)

    def body(w_ref, g_ref, m_ref, v_ref, g_out, d_out, m_out, v_out):
        d, mm, vv = _adam_math(w_ref[...], g_ref[...], m_ref[...], v_ref[...])
        g_out[...] = g_ref[...]
        d_out[...] = d
        m_out[...] = mm
        v_out[...] = vv

    spec = pl.BlockSpec((tr, tc), lambda i, j: (i, j))
    outs, job_outs = _call(name, body, (r // tr, c // tc), [spec] * 4, [spec] * 4, [_sds((r, c), F32)] * 4,
                           [w, g, m, v], sem=("parallel", "parallel"), jobs=jobs)
    return _ret(outs, job_outs, jobs, single=False)


def small_sum(name, stacks):
    def body(*refs):
        for s_ref, out_ref in zip(refs[:len(stacks)], refs[len(stacks):]):
            acc = s_ref[0]
            for d in range(1, s_ref.shape[0]):
                acc = acc + s_ref[d]
            out_ref[...] = acc

    return pl.pallas_call(body, name=name, out_shape=[_sds(s.shape[1:], F32) for s in stacks])(*stacks)


WEIGHTS = ["g_ffn1", "w_ffn1_in", "w_ffn1_out", "g_mix", "w_mix_in", "conv_w", "conv_b", "g_gm_v", "w_spatial",
           "b_spatial", "w_mix_out", "g_xattn", "g_mem", "w_xq", "w_xk", "w_xv", "w_xo", "g_ffn2", "w_ffn2_in",
           "w_ffn2_out", "g_final"]
BIG = {"w_ffn1_in": 1, "w_ffn1_out": 0, "w_mix_in": 1, "w_mix_out": 0, "w_xq": 0, "w_xk": 0, "w_xv": 0, "w_xo": 0,
       "w_ffn2_in": 1, "w_ffn2_out": 0}
SMALL = [n for n in WEIGHTS if n not in BIG]
LATE_SMALL = ["g_ffn1"]
EARLY_SMALL = [n for n in SMALL if n not in LATE_SMALL]


def _pack(arrays):
    flat = jnp.concatenate([a.reshape(-1) for a in arrays])
    rows = -(-flat.shape[0] // 1024) * 8
    return jnp.pad(flat, (0, rows * 128 - flat.shape[0])).reshape(rows, 128)


def _unpack(buf, shapes):
    flat = buf.reshape(-1)
    out, pos = [], 0
    for shp in shapes:
        n = math.prod(shp)
        out.append(flat[pos:pos + n].reshape(shp))
        pos += n
    return out


def kernel(x, mem, g_ffn1, w_ffn1_in, w_ffn1_out, g_mix, w_mix_in, conv_w, conv_b, g_gm_v, w_spatial, b_spatial, w_mix_out, g_xattn, g_mem, w_xq, w_xk, w_xv, w_xo, g_ffn2, w_ffn2_in, w_ffn2_out, g_final, loss_target, m_g_ffn1, m_w_ffn1_in, m_w_ffn1_out, m_g_mix, m_w_mix_in, m_conv_w, m_conv_b, m_g_gm_v, m_w_spatial, m_b_spatial, m_w_mix_out, m_g_xattn, m_g_mem, m_w_xq, m_w_xk, m_w_xv, m_w_xo, m_g_ffn2, m_w_ffn2_in, m_w_ffn2_out, m_g_final, v_g_ffn1, v_w_ffn1_in, v_w_ffn1_out, v_g_mix, v_w_mix_in, v_conv_w, v_conv_b, v_g_gm_v, v_w_spatial, v_b_spatial, v_w_mix_out, v_g_xattn, v_g_mem, v_w_xq, v_w_xk, v_w_xv, v_w_xo, v_g_ffn2, v_w_ffn2_in, v_w_ffn2_out, v_g_final):
    given = dict(locals())
    wts = {n: given[n] for n in WEIGHTS}
    mom = {n: given["m_" + n] for n in WEIGHTS}
    var = {n: given["v_" + n] for n in WEIGHTS}

    xi, yi, ci = lax.axis_index("x"), lax.axis_index("y"), lax.axis_index("c")
    blk = 2 * xi + yi
    place = jnp.stack([blk, ci]).astype(jnp.int32)

    x2, mem2, tgt = x[0], mem[0], loss_target[0]
    w_s, b_t = w_spatial[0], b_spatial[0].T
    gf = g_final[None]

    rest = [n for n in BIG if n != "w_ffn1_in"]
    own = {n: cast_place("cast_" + n, wts[n][0], BIG[n], place) for n in rest}
    own_left = cast_place("cast_w_ffn1_in_left", wts["w_ffn1_in"][0], 1, place, column_half=0)
    own_right = cast_place("cast_w_ffn1_in_right", wts["w_ffn1_in"][0], 1, place, column_half=1)
    shape = {n: own[n].shape for n in rest}
    shape["w_ffn1_in"] = _full_shape(wts["w_ffn1_in"][0].shape, 1)
    full = {}
    (w1_left,) = gather_by_sequencer("gather_w_ffn1_in_left", [own_left], [1], 1)
    (w1_right,) = gather_by_sequencer("gather_w_ffn1_in_right", [own_right], [1], 2)
    groups = [["w_ffn1_out"], ["w_mix_in", "w_mix_out"], ["w_xq", "w_xk", "w_xv", "w_xo"], ["w_ffn2_in"],
              ["w_ffn2_out"]]
    for g, names in enumerate(groups):
        got = gather_by_sequencer("gather_" + "_".join(names), [own[n] for n in names], [BIG[n] for n in names],
                                  3 + g)
        full.update(zip(names, got))
    ((conv_taps,),) = comm_only("gather_conv_taps", [columns_job(jnp.pad(conv_w[0], ((0, 8 - CONV_K), (0, 0))))])

    half_cols = dict(tm=512, tn=shape["w_ffn1_in"][1] // (2 * N_CHIPS), stride=2, compact=True)
    n1, r1 = rmsnorm_fwd("norm1", x2, g_ffn1)
    halves = swiglu_fwd("ffn1_in_left", n1, w1_left, phase=0, **half_cols)
    gu1, a1 = swiglu_fwd("ffn1_in_right", n1, w1_right, phase=1, prev=halves, **half_cols)
    h1 = mm_nn_resid("ffn1_out", a1, full["w_ffn1_out"], x2, 0.5, tm=512, tk=5632)
    n2, r2 = rmsnorm_fwd("norm2", h1, g_mix)
    z = mm_nn("mix_in", n2, full["w_mix_in"], BF16)
    ycat = mixer_fwd("mixer", z, conv_taps, conv_b, g_gm_v, w_s, b_t)
    h2 = mm_nn_resid("mix_out", ycat, full["w_mix_out"], h1, 1.0, tk=2048)
    n3, r3 = rmsnorm_fwd("norm3", h2, g_xattn)
    mn, rm = rmsnorm_fwd("norm_mem", mem2, g_mem)
    q = mm_nn("xq", n3, full["w_xq"], BF16)
    k = mm_nn("xk", mn, full["w_xk"], BF16)
    v = mm_nn("xv", mn, full["w_xv"], BF16)
    o = attn_fwd("attn", q, k, v)
    h3 = mm_nn_resid("xo", o, full["w_xo"], h2, 1.0, tk=2048)
    n4, r4 = rmsnorm_fwd("norm4", h3, g_ffn2)
    gu2, a2 = swiglu_fwd("ffn2_in", n4, full["w_ffn2_in"])
    h4 = mm_nn_resid("ffn2_out", a2, full["w_ffn2_out"], h3, 0.5, tm=512, tk=5632)
    loss_blk, dh4, dh4b, dg_final = loss_head("loss_head", h4, gf, tgt)
    full["w_ffn1_in"] = merge_column_halves("merge_w_ffn1_in", w1_left, w1_right)

    dw, peer, part, land, half, grads = {}, {}, {}, {}, {}, {}

    def send_pair(*names):
        return pair_job([dw[n] for n in names], [BIG[n] for n in names])

    def take_pair(names, got):
        for n, p in zip(names, got):
            part[n] = pair_add("pair_add_" + n, dw[n], p, BIG[n], place)

    def send_cross(*names, sub=WHOLE):
        return cross_job([(part[n], BIG[n], shape[n], land.get(n), sub) for n in names])

    def take_cross(names, got, last=True):
        for n, l in zip(names, got):
            land[n] = l
            if last:
                half[n] = cross_sum("cross_sum_" + n, part[n], l, BIG[n], shape[n], place)

    def send_final(*names):
        return final_job([half[n] for n in names], [BIG[n] for n in names], [shape[n] for n in names])

    delta, new_m, new_v = {}, {}, {}

    reduced = {}

    def take_final(names, got):
        for n, g in zip(names, got):
            reduced[n] = g

    def update(n, jobs=()):
        res = adamw("adamw_" + n, wts[n][0], reduced[n], mom[n][0], var[n][0], jobs=jobs)
        (grads[n], delta[n], new_m[n], new_v[n]), job_outs = res if jobs else (res, [])
        return job_outs

    dgu2 = swiglu_bwd("ffn2_dact", dh4b, full["w_ffn2_out"], gu2, 0.5)
    dw["w_ffn2_in"] = mm_tn_pair("ffn2_dwin", n4, dgu2, BF16)
    dw["w_ffn2_out"], (got,) = mm_tn("ffn2_dwout", a2, dh4b, BF16, scale=0.5, jobs=[send_pair("w_ffn2_in")])
    take_pair(["w_ffn2_in"], got)
    (dh3, dh3b, dg_ffn2), (got_c, got_p) = mm_nt_norm_bwd(
        "ffn2_dn", dgu2, full["w_ffn2_in"], h3, r4, g_ffn2, dh4,
        jobs=[send_cross("w_ffn2_in", sub=(0, 7, 8)), send_pair("w_ffn2_out")])
    take_cross(["w_ffn2_in"], got_c, last=False)
    take_pair(["w_ffn2_out"], got_p)

    dw["w_xo"], (got_c,) = mm_tn("xo_dw", o, dh3b, BF16, jobs=[send_cross("w_ffn2_in", sub=(7, 1, 8))])
    take_cross(["w_ffn2_in"], got_c)
    do, (got_c, got_f) = mm_nt("xo_dx", dh3b, full["w_xo"], BF16,
                               jobs=[send_cross("w_ffn2_out", sub=(0, 2, 8)), send_final("w_ffn2_in")])
    take_cross(["w_ffn2_out"], got_c, last=False)
    take_final(["w_ffn2_in"], got_f)
    update("w_ffn2_in")
    dq, dk, dv = attn_bwd("attn_bwd", q, k, v, do)
    dkb, dvb = dk.astype(BF16), dv.astype(BF16)
    dw["w_xq"], (got_c,) = mm_tn("xq_dw", n3, dq, BF16, jobs=[send_cross("w_ffn2_out", sub=(2, 2, 8))])
    take_cross(["w_ffn2_out"], got_c, last=False)
    (dh2, dh2b, dg_xattn), (got_c,) = mm_nt_norm_bwd(
        "xq_dx", dq, full["w_xq"], h2, r3, g_xattn, dh3, tk=1024, jobs=[send_cross("w_ffn2_out", sub=(4, 4, 8))])
    take_cross(["w_ffn2_out"], got_c)
    dw["w_xk"] = mm_tn("xk_dw", mn, dkb, BF16)
    dw["w_xv"] = mm_tn("xv_dw", mn, dvb, BF16)
    dmn_k = mm_nt("xk_dx", dkb, full["w_xk"], F32)
    dmn_v = mm_nt("xv_dx", dvb, full["w_xv"], F32)
    dg_mem = gain_grad("norm_mem_bwd", dmn_k, dmn_v, mem2, rm)

    dw["w_mix_out"], (got_f,) = mm_tn("mix_out_dw", ycat, dh2b, BF16, jobs=[send_final("w_ffn2_out")])
    take_final(["w_ffn2_out"], got_f)
    update("w_ffn2_out")
    attn_names = ["w_xo", "w_xq", "w_xk", "w_xv", "w_mix_out"]
    dycat, (got_p,) = mm_nt("mix_out_dx", dh2b, full["w_mix_out"], BF16, jobs=[send_pair(*attn_names)])
    take_pair(attn_names, got_p)
    (dz, dsmall, dws, dbt), (got_c,) = mixer_bwd("mixer_bwd", z, dycat, conv_taps, conv_b, g_gm_v, w_s, b_t,
                                                 jobs=[send_cross("w_xo", "w_xq")])
    take_cross(["w_xo", "w_xq"], got_c)
    dw["w_mix_in"], (got_c,) = mm_tn("mix_in_dw", n2, dz, BF16, jobs=[send_cross("w_xk", "w_xv")])
    take_cross(["w_xk", "w_xv"], got_c)
    (dh1, dh1b, dg_mix), (got_c, got_p) = mm_nt_norm_bwd(
        "mix_in_dx", dz, full["w_mix_in"], h1, r2, g_mix, dh2, tk=1280,
        jobs=[send_cross("w_mix_out"), send_pair("w_mix_in")])
    take_cross(["w_mix_out"], got_c)
    take_pair(["w_mix_in"], got_p)

    dw["w_ffn1_out"], (got_c, got_f) = mm_tn("ffn1_dwout", a1, dh1b, BF16, scale=0.5,
                                             jobs=[send_cross("w_mix_in"), send_final(*attn_names)])
    take_cross(["w_mix_in"], got_c)
    take_final(attn_names, got_f)
    for n in attn_names:
        update(n)
    early = {"g_mix": dg_mix, "conv_w": dsmall[0:CONV_K], "conv_b": dsmall[3:4], "g_gm_v": dsmall[4:5],
             "w_spatial": dws, "b_spatial": dbt.T, "g_xattn": dg_xattn, "g_mem": dg_mem, "g_ffn2": dg_ffn2,
             "g_final": dg_final}
    dgu1, (got_p, got_f, (early_all,)) = swiglu_bwd(
        "ffn1_dact", dh1b, full["w_ffn1_out"], gu1, 0.5,
        jobs=[send_pair("w_ffn1_out"), send_final("w_mix_in"), stack_job(_pack([early[n] for n in EARLY_SMALL]))])
    take_pair(["w_ffn1_out"], got_p)
    take_final(["w_mix_in"], got_f)
    update("w_mix_in")
    theirs, (got_c,) = mm_tn_pair_half("ffn1_dwin_theirs", n1, dgu1, BF16, place, False,
                                       jobs=[send_cross("w_ffn1_out", sub=(0, 7, 8))])
    take_cross(["w_ffn1_out"], got_c, last=False)
    mine, (got_c, (from_sibling,)) = mm_tn_pair_half(
        "ffn1_dwin_mine", n1, dgu1, BF16, place, True,
        jobs=[send_cross("w_ffn1_out", sub=(7, 1, 8)), pair_job([theirs], [1], is_half=True)])
    take_cross(["w_ffn1_out"], got_c)
    part["w_ffn1_in"] = pair_add("pair_add_w_ffn1_in", mine, from_sibling, 1, place)
    dn1, (got_c, got_f) = mm_nt_pair("ffn1_dn", dgu1, full["w_ffn1_in"], F32,
                                     jobs=[send_cross("w_ffn1_in", sub=(0, 7, 8)), send_final("w_ffn1_out")])
    take_cross(["w_ffn1_in"], got_c, last=False)
    take_final(["w_ffn1_out"], got_f)
    update("w_ffn1_out")
    dx, _, dg_ffn1 = rmsnorm_bwd("norm1_bwd", dn1, x2, r1, g_ffn1, dh1)
    got_c, (late_all,) = comm_only("tail_cross", [send_cross("w_ffn1_in", sub=(7, 1, 8)),
                                                  stack_job(_pack([dg_ffn1]))])
    take_cross(["w_ffn1_in"], got_c)
    (got_f,) = comm_only("tail_final", [send_final("w_ffn1_in")])
    take_final(["w_ffn1_in"], got_f)
    update("w_ffn1_in")

    early_sum, late_sum = small_sum("small_sum", [early_all, late_all])
    for n, g in zip(EARLY_SMALL, _unpack(early_sum, [early[n].shape for n in EARLY_SMALL])):
        grads[n] = g
    grads["g_ffn1"] = _unpack(late_sum, [dg_ffn1.shape])[0]
    taps_cols = conv_w.shape[2]
    grads["conv_w"] = lax.dynamic_slice_in_dim(grads["conv_w"], blk * taps_cols, taps_cols, axis=1)
    packed = [_pack([src[n] for n in SMALL]) for src in (wts, grads, mom, var)]
    own_shapes = [wts[n].shape for n in SMALL]
    for dst, buf in zip((delta, new_m, new_v), adamw("adamw_small", *packed)[1:]):
        for n, a in zip(SMALL, _unpack(buf, own_shapes)):
            dst[n] = a

    loss = lax.psum(loss_blk[0, 0], ("x", "y", "c"))
    outs = [loss, dx[None]]
    for group in (grads, delta, new_m, new_v):
        outs += [group[n].reshape(wts[n].shape) for n in WEIGHTS]
    return tuple(outs)
```

```python
import math

import jax
import jax.numpy as jnp
from jax import lax
from jax.experimental import pallas as pl
from jax.experimental.pallas import tpu as pltpu
from jax.experimental.pallas import tpu_sc as plsc

F32 = jnp.float32
BF16 = jnp.bfloat16
EPS = 1e-6
GROUP = 128
XA_HEADS = 4
CONV_K = 3
N_CHIPS = 4
VMEM_LIMIT_BYTES = 56 * 1024 * 1024

ADAM_LR = 0.001
ADAM_B1 = 0.9
ADAM_B2 = 0.999
ADAM_EPS = 1e-08
ADAM_WD = 0.01
ADAM_STEP = 10

MESH = pl.DeviceIdType.MESH
ANY = pl.BlockSpec(memory_space=pl.ANY)


def _tile(dim, pref, mult=128):
    if dim <= pref:
        return dim
    t = (pref // mult) * mult
    while t >= mult:
        if dim % t == 0:
            return t
        t -= mult
    raise ValueError(f"no tile for {dim} under {pref}")


def _params(sem):
    return pltpu.CompilerParams(dimension_semantics=sem, vmem_limit_bytes=VMEM_LIMIT_BYTES)


def _sds(shape, dtype):
    return jax.ShapeDtypeStruct(shape, dtype)


def _dot_nn(a, b):
    return jnp.dot(a, b, preferred_element_type=F32)


def _dot_nt(a, b):
    return lax.dot_general(a, b, (((1,), (1,)), ((), ())), preferred_element_type=F32)


def _dot_tn(a, b):
    return lax.dot_general(a, b, (((0,), (0,)), ((), ())), preferred_element_type=F32)


class Job:
    def __init__(self, inputs, out_shapes, aliases, sems, start, middle, finish):
        self.inputs, self.out_shapes, self.aliases, self.sems = inputs, out_shapes, aliases, sems
        self.start, self.middle, self.finish = start, middle, finish


def _place():
    x, y, c = lax.axis_index("x"), lax.axis_index("y"), lax.axis_index("c")
    chips = [(1 - x, y), (x, 1 - y), (1 - x, 1 - y)]
    return x, y, c, chips


def _ds(start, size, lane):
    if not isinstance(start, int):
        start = pl.multiple_of(start, 128 if lane else 16)
    return pl.ds(start, size)


WHOLE = (0, 1, 1)


def _window(ref, axis, shape, blk=None, half=None, sub=WHOLE, within=WHOLE):
    n = shape[axis] // N_CHIPS
    hs = shape[1 - axis] // 2
    idx = [slice(None), slice(None)]
    if blk is not None:
        b_first, b_count, b_pieces = within
        b_ext = n // b_pieces
        idx[axis] = _ds(blk * n + b_first * b_ext, b_count * b_ext, axis == 1)
    first, count, pieces = sub
    ext = hs // pieces
    if half is not None:
        idx[1 - axis] = _ds(half * hs + first * ext, count * ext, axis == 0)
    elif pieces > 1:
        idx[1 - axis] = _ds(first * ext, count * ext, axis == 0)
    return ref.at[tuple(idx)]


def _remote(src, dst, send_sem, recv_sem, dev):
    return pltpu.make_async_remote_copy(src_ref=src, dst_ref=dst, send_sem=send_sem, recv_sem=recv_sem,
                                        device_id=dev, device_id_type=MESH)


def _full_shape(block_shape, axis):
    out = list(block_shape)
    out[axis] *= N_CHIPS
    return tuple(out)


def _half_all(shape, axis):
    out = list(shape)
    out[1 - axis] //= 2
    return tuple(out)


def _block(shape, axis):
    out = list(shape)
    out[axis] //= N_CHIPS
    return tuple(out)


def _half_block(shape, axis):
    return _half_all(_block(shape, axis), axis)


def gather_job(items):
    nw = len(items)
    shapes = [item[0].shape for item in items]
    n_sem = 8

    def parts(sub):
        first, count, pieces = sub
        return (2 * first, count, 2 * pieces), (2 * first + count, count, 2 * pieces)

    def start(pos, ins, outs, sems):
        x, y, c, chips = pos
        for w, (_, ax, sub, within) in enumerate(items):
            mine = _window(outs[w], ax, shapes[w], blk=2 * x + y, half=c, sub=sub, within=within)
            for j in range(2):
                _remote(mine, mine, sems[0].at[n_sem * w + j], sems[1].at[n_sem * w + j], (*chips[j], c)).start()

    def middle(pos, ins, outs, sems):
        x, y, c, chips = pos
        for w, (_, ax, sub, within) in enumerate(items):
            for j in range(2):
                cx, cy = chips[j]
                landed = _window(outs[w], ax, shapes[w], blk=2 * cx + cy, half=c, sub=sub, within=within)
                _remote(landed, landed, sems[0].at[n_sem * w + j], sems[1].at[n_sem * w + j], (cx, cy, c)).wait_recv()
                part = _window(outs[w], ax, shapes[w], blk=2 * cx + cy, half=c, sub=parts(sub)[j], within=within)
                _remote(part, part, sems[0].at[n_sem * w + 2 + j], sems[1].at[n_sem * w + 2 + j],
                        (*chips[1 - j], c)).start()
                _remote(landed, landed, sems[0].at[n_sem * w + 4 + j], sems[1].at[n_sem * w + 4 + j],
                        (x, y, 1 - c)).start()

    def finish(pos, ins, outs, sems):
        x, y, c, chips = pos
        sib = (x, y, 1 - c)
        for w, (_, ax, sub, within) in enumerate(items):
            dx, dy = chips[2]
            for j in range(2):
                part = _window(outs[w], ax, shapes[w], blk=2 * dx + dy, half=c, sub=parts(sub)[j], within=within)
                cp = _remote(part, part, sems[0].at[n_sem * w + 2 + j], sems[1].at[n_sem * w + 2 + j], sib)
                cp.wait_recv()
                cp.wait_send()
            diag = _window(outs[w], ax, shapes[w], blk=2 * dx + dy, half=c, sub=sub, within=within)
            _remote(diag, diag, sems[0].at[n_sem * w + 6], sems[1].at[n_sem * w + 6], sib).start()
        for w, (_, ax, sub, within) in enumerate(items):
            for j, (cx, cy) in enumerate(chips):
                passed = _window(outs[w], ax, shapes[w], blk=2 * cx + cy, half=1 - c, sub=sub, within=within)
                cp = _remote(passed, passed, sems[0].at[n_sem * w + 4 + j], sems[1].at[n_sem * w + 4 + j], sib)
                cp.wait_recv()
                cp.wait_send()
            mine = _window(outs[w], ax, shapes[w], blk=2 * x + y, half=c, sub=sub, within=within)
            for j in range(2):
                _remote(mine, mine, sems[0].at[n_sem * w + j], sems[1].at[n_sem * w + j], sib).wait_send()

    sems = [pltpu.SemaphoreType.DMA((n_sem * nw,)), pltpu.SemaphoreType.DMA((n_sem * nw,))]
    return Job([item[0] for item in items], [_sds(item[0].shape, item[0].dtype) for item in items],
               {w: w for w in range(nw)}, sems, start, middle, finish)


def gather_by_sequencer(name, fulls, axes, collective_id):
    job = gather_job([(full, axis, WHOLE, WHOLE) for full, axis in zip(fulls, axes)])
    refs = [jax.new_ref(full, memory_space=pltpu.MemorySpace.HBM) for full in fulls]

    @pl.kernel(mesh=plsc.ScalarSubcoreMesh(axis_name="sequencer", num_cores=1), name=name,
               scratch_types=tuple(job.sems), compiler_params=pltpu.CompilerParams(collective_id=collective_id))
    def launch(send_sems, recv_sems):
        pos = _place()
        x, y, c, chips = pos
        barrier = pltpu.get_barrier_semaphore()
        for dev in ((*chips[0], c), (*chips[1], c), (x, y, 1 - c)):
            pl.semaphore_signal(barrier, inc=1, device_id=dev, device_id_type=MESH)
        pl.semaphore_wait(barrier, 3)
        job.start(pos, refs, refs, [send_sems, recv_sems])
        job.middle(pos, refs, refs, [send_sems, recv_sems])
        job.finish(pos, refs, refs, [send_sems, recv_sems])

    launch()
    return [ref[...] for ref in refs]


def cross_by_sequencer(name, items, collective_id):
    job = cross_job([(part, axis, shape, None, WHOLE) for part, axis, shape in items])
    srcs = [jax.new_ref(part, memory_space=pltpu.MemorySpace.HBM) for part, _, _ in items]
    lands = [jax.empty_ref(s, memory_space=pltpu.MemorySpace.HBM) for s in job.out_shapes]

    @pl.kernel(mesh=plsc.ScalarSubcoreMesh(axis_name="sequencer", num_cores=1), name=name,
               scratch_types=tuple(job.sems), compiler_params=pltpu.CompilerParams(collective_id=collective_id))
    def launch(send_sems, recv_sems):
        pos = _place()
        x, y, c, chips = pos
        barrier = pltpu.get_barrier_semaphore()
        for cx, cy in chips:
            pl.semaphore_signal(barrier, inc=1, device_id=(cx, cy, c), device_id_type=MESH)
        pl.semaphore_wait(barrier, 3)
        job.start(pos, srcs, lands, [send_sems, recv_sems])
        job.finish(pos, srcs, lands, [send_sems, recv_sems])

    launch()
    return [r[...] for r in srcs], [r[...] for r in lands]


def pair_job(grads, axes, is_half=False):
    nw = len(grads)
    shapes = [g.shape for g in grads]

    def start(pos, ins, outs, sems):
        x, y, c, _ = pos
        for w in range(nw):
            src = ins[w] if is_half else _window(ins[w], axes[w], shapes[w], half=1 - c)
            _remote(src, outs[w], sems[0].at[w], sems[1].at[w], (x, y, 1 - c)).start()

    def finish(pos, ins, outs, sems):
        x, y, c, _ = pos
        for w in range(nw):
            cp = _remote(outs[w], outs[w], sems[0].at[w], sems[1].at[w], (x, y, 1 - c))
            cp.wait_recv()
            cp.wait_send()

    sems = [pltpu.SemaphoreType.DMA((nw,)), pltpu.SemaphoreType.DMA((nw,))]
    out_shapes = [_sds(s if is_half else _half_all(s, a), BF16) for s, a in zip(shapes, axes)]
    return Job(list(grads), out_shapes, {}, sems, start, None, finish)


def cross_job(items):
    nw = len(items)
    inputs, aliases = [], {}
    for w, (part, ax, shape, prev, sub) in enumerate(items):
        inputs.append(part)
        if prev is not None:
            aliases[len(inputs)] = w
            inputs.append(prev)

    def copies(pos, ins, outs, sems):
        x, y, c, chips = pos
        k = 0
        for w, (_, ax, shape, prev, sub) in enumerate(items):
            src = ins[k]
            k += 2 if prev is not None else 1
            for j, (cx, cy) in enumerate(chips):
                slot = _window(outs[w].at[j], ax, shape, sub=sub)
                yield (_remote(_window(src, ax, shape, blk=2 * cx + cy, sub=sub), slot,
                               sems[0].at[3 * w + j], sems[1].at[3 * w + j], (cx, cy, c)),
                       _remote(slot, slot, sems[0].at[3 * w + j], sems[1].at[3 * w + j], (cx, cy, c)))

    def start(pos, ins, outs, sems):
        for send, _ in copies(pos, ins, outs, sems):
            send.start()

    def finish(pos, ins, outs, sems):
        for send, recv in copies(pos, ins, outs, sems):
            recv.wait_recv()
            send.wait_send()

    sems = [pltpu.SemaphoreType.DMA((3 * nw,)), pltpu.SemaphoreType.DMA((3 * nw,))]
    out_shapes = [_sds((3,) + _half_block(shape, ax), BF16) for _, ax, shape, _, _ in items]
    return Job(inputs, out_shapes, aliases, sems, start, None, finish)


def final_job(blocks, axes, shapes):
    nw = len(blocks)

    def start(pos, ins, outs, sems):
        x, y, c, _ = pos
        for w in range(nw):
            mine = _window(outs[w], axes[w], shapes[w], half=c)
            _remote(mine, mine, sems[0].at[w], sems[1].at[w], (x, y, 1 - c)).start()

    def finish(pos, ins, outs, sems):
        x, y, c, _ = pos
        for w in range(nw):
            theirs = _window(outs[w], axes[w], shapes[w], half=1 - c)
            cp = _remote(theirs, theirs, sems[0].at[w], sems[1].at[w], (x, y, 1 - c))
            cp.wait_recv()
            cp.wait_send()

    sems = [pltpu.SemaphoreType.DMA((nw,)), pltpu.SemaphoreType.DMA((nw,))]
    return Job(list(blocks), [_sds(b.shape, b.dtype) for b in blocks], {w: w for w in range(nw)}, sems, start, None,
               finish)


def stack_job(small):
    def peers(pos):
        x, y, c, _ = pos
        for k in range(1, 8):
            yield k - 1, (1 - x if k & 4 else x, 1 - y if k & 2 else y, 1 - c if k & 1 else c)

    def start(pos, ins, outs, sems):
        x, y, c, _ = pos
        mine = outs[0].at[4 * x + 2 * y + c]
        pltpu.make_async_copy(ins[0], mine, sems[2]).start()
        for k, dev in peers(pos):
            _remote(ins[0], mine, sems[0].at[k], sems[1].at[k], dev).start()

    def finish(pos, ins, outs, sems):
        x, y, c, _ = pos
        for k, (px, py, pc) in peers(pos):
            slot = outs[0].at[4 * px + 2 * py + pc]
            cp = _remote(slot, slot, sems[0].at[k], sems[1].at[k], (px, py, pc))
            cp.wait_recv()
            cp.wait_send()
        pltpu.make_async_copy(ins[0], outs[0].at[4 * x + 2 * y + c], sems[2]).wait()

    sems = [pltpu.SemaphoreType.DMA((7,)), pltpu.SemaphoreType.DMA((7,)), pltpu.SemaphoreType.DMA]
    return Job([small], [_sds((8,) + small.shape, small.dtype)], {}, sems, start, None, finish)


def columns_job(block):
    cols = block.shape[1]
    place = lambda out, b: out.at[:, _ds(b * cols, cols, True)]

    def start(pos, ins, outs, sems):
        x, y, c, chips = pos
        pltpu.make_async_copy(ins[0], place(outs[0], 2 * x + y), sems[2]).start()
        for j, (cx, cy) in enumerate(chips):
            _remote(ins[0], place(outs[0], 2 * x + y), sems[0].at[j], sems[1].at[j], (cx, cy, c)).start()

    def finish(pos, ins, outs, sems):
        x, y, c, chips = pos
        for j, (cx, cy) in enumerate(chips):
            got = place(outs[0], 2 * cx + cy)
            cp = _remote(got, got, sems[0].at[j], sems[1].at[j], (cx, cy, c))
            cp.wait_recv()
            cp.wait_send()
        pltpu.make_async_copy(ins[0], place(outs[0], 2 * x + y), sems[2]).wait()

    sems = [pltpu.SemaphoreType.DMA((3,)), pltpu.SemaphoreType.DMA((3,)), pltpu.SemaphoreType.DMA]
    return Job([block], [_sds((block.shape[0], N_CHIPS * cols), block.dtype)], {}, sems, start, None, finish)


def _call(name, body, grid, in_specs, out_specs, out_shape, args, scratch=(), sem=None, jobs=(), place=None,
          carried=None):
    n_in, n_out, n_sc = len(args), len(out_shape), len(scratch)
    carried = dict(carried or {})

    def launch(fn, in_specs, out_specs, out_shape, scratch, aliases, sem, operands):
        if place is None:
            return pl.pallas_call(
                fn, name=name, grid=grid, in_specs=in_specs, out_specs=out_specs, out_shape=out_shape,
                scratch_shapes=scratch, input_output_aliases=aliases, compiler_params=_params(sem))(*operands)
        spec = pltpu.PrefetchScalarGridSpec(num_scalar_prefetch=1, grid=grid, in_specs=in_specs,
                                            out_specs=out_specs, scratch_shapes=scratch)
        return pl.pallas_call(
            lambda p_ref, *refs: fn(*refs), name=name, grid_spec=spec, out_shape=out_shape,
            input_output_aliases={k + 1: v for k, v in aliases.items()}, compiler_params=_params(sem),
        )(place, *operands)

    if not jobs:
        outs = launch(body, list(in_specs), list(out_specs), list(out_shape), list(scratch), carried, sem, args)
        return list(outs), []

    total = math.prod(grid) if grid else 1
    mid = min(total - 1, (2 * total) // 3)

    def split(refs, start, counts):
        out = []
        for n in counts:
            out.append(refs[start:start + n])
            start += n
        return out, start

    def wrapped(*refs):
        c_in = refs[:n_in]
        j_ins, p = split(refs, n_in, [len(j.inputs) for j in jobs])
        c_out = refs[p:p + n_out]
        j_outs, p = split(refs, p + n_out, [len(j.out_shapes) for j in jobs])
        c_sc = refs[p:p + n_sc]
        j_sems, p = split(refs, p + n_sc, [len(j.sems) for j in jobs])
        pos = _place()
        step = 0
        for axis, extent in enumerate(grid):
            step = step * extent + pl.program_id(axis)

        def run(phase):
            for j, ins, outs, sems in zip(jobs, j_ins, j_outs, j_sems):
                fn = getattr(j, phase)
                if fn is not None:
                    fn(pos, ins, outs, sems)

        if total == 1:
            run("start")
            body(*c_in, *c_out, *c_sc)
            run("middle")
            run("finish")
            return
        pl.when(step == 0)(lambda: run("start"))
        body(*c_in, *c_out, *c_sc)
        if any(j.middle is not None for j in jobs):
            pl.when(step == mid)(lambda: run("middle"))
        pl.when(step == total - 1)(lambda: run("finish"))

    aliases, in_at, out_at = carried, n_in, n_out
    for j in jobs:
        for src, dst in j.aliases.items():
            aliases[in_at + src] = out_at + dst
        in_at += len(j.inputs)
        out_at += len(j.out_shapes)
    outs = launch(
        wrapped, list(in_specs) + [ANY] * (in_at - n_in), list(out_specs) + [ANY] * (out_at - n_out),
        list(out_shape) + [s for j in jobs for s in j.out_shapes],
        list(scratch) + [s for j in jobs for s in j.sems], aliases, ("arbitrary",) * len(grid),
        [*args, *[a for j in jobs for a in j.inputs]])
    job_outs, p = split(outs, n_out, [len(j.out_shapes) for j in jobs])
    return list(outs[:n_out]), [list(o) for o in job_outs]


def comm_only(name, jobs):
    def body(dummy_ref, out_ref):
        out_ref[...] = dummy_ref[...]

    dummy = jnp.zeros((8, 128), F32)
    spec = pl.BlockSpec((8, 128), lambda: (0, 0))
    return _call(name, body, (), [spec], [spec], [_sds((8, 128), F32)], [dummy], jobs=jobs)[1]


def _ret(outs, job_outs, jobs, single=True):
    res = outs[0] if single else outs
    return (res, job_outs) if jobs else res


def rmsnorm_fwd(name, x, g, jobs=()):
    s, d = x.shape
    tm = _tile(s, 512, 8)

    def body(x_ref, g_ref, n_ref, r_ref):
        xv = x_ref[...]
        r = lax.rsqrt(jnp.mean(xv * xv, axis=-1, keepdims=True) + EPS)
        n_ref[...] = (xv * r * g_ref[...]).astype(BF16)
        r_ref[...] = r

    row = lambda i: (i, 0)
    outs, job_outs = _call(
        name, body, (s // tm,),
        [pl.BlockSpec((tm, d), row), pl.BlockSpec((1, d), lambda i: (0, 0))],
        [pl.BlockSpec((tm, d), row), pl.BlockSpec((tm, 1), row)],
        [_sds((s, d), BF16), _sds((s, 1), F32)], [x, g], sem=("arbitrary",), jobs=jobs)
    return _ret(outs, job_outs, jobs, single=False)


def rmsnorm_bwd(name, dn, x, r, g, dh_in, jobs=()):
    s, d = x.shape
    tm = _tile(s, 512, 8)

    def body(dn_ref, x_ref, r_ref, g_ref, dh_ref, out_ref, outb_ref, dg_ref):
        i = pl.program_id(0)
        xh = x_ref[...] * r_ref[...]
        dnv = dn_ref[...]
        dxh = dnv * g_ref[...]
        dx = r_ref[...] * (dxh - xh * jnp.mean(dxh * xh, axis=-1, keepdims=True))
        out = dh_ref[...] + dx
        out_ref[...] = out
        outb_ref[...] = out.astype(BF16)
        part = jnp.sum(dnv * xh, axis=0, keepdims=True)

        @pl.when(i == 0)
        def _():
            dg_ref[...] = part

        @pl.when(i > 0)
        def _():
            dg_ref[...] += part

    row = lambda i: (i, 0)
    fixed = lambda i: (0, 0)
    outs, job_outs = _call(
        name, body, (s // tm,),
        [pl.BlockSpec((tm, d), row), pl.BlockSpec((tm, d), row), pl.BlockSpec((tm, 1), row),
         pl.BlockSpec((1, d), fixed), pl.BlockSpec((tm, d), row)],
        [pl.BlockSpec((tm, d), row), pl.BlockSpec((tm, d), row), pl.BlockSpec((1, d), fixed)],
        [_sds((s, d), F32), _sds((s, d), BF16), _sds((1, d), F32)], [dn, x, r, g, dh_in],
        sem=("arbitrary",), jobs=jobs)
    return _ret(outs, job_outs, jobs, single=False)


def gain_grad(name, dn_a, dn_b, x, r):
    s, d = x.shape
    tm = _tile(s, 512, 8)

    def body(a_ref, b_ref, x_ref, r_ref, dg_ref):
        i = pl.program_id(0)
        part = jnp.sum((a_ref[...] + b_ref[...]) * (x_ref[...] * r_ref[...]), axis=0, keepdims=True)

        @pl.when(i == 0)
        def _():
            dg_ref[...] = part

        @pl.when(i > 0)
        def _():
            dg_ref[...] += part

    row = lambda i: (i, 0)
    return _call(
        name, body, (s // tm,),
        [pl.BlockSpec((tm, d), row), pl.BlockSpec((tm, d), row), pl.BlockSpec((tm, d), row),
         pl.BlockSpec((tm, 1), row)],
        [pl.BlockSpec((1, d), lambda i: (0, 0))], [_sds((1, d), F32)], [dn_a, dn_b, x, r],
        sem=("arbitrary",))[0][0]


def loss_head(name, h, g, target):
    s, d = h.shape
    tm = _tile(s, 512, 8)
    nsteps = s // tm

    def body(h_ref, g_ref, t_ref, loss_ref, dh_ref, dhb_ref, dg_ref, sq_ref):
        i = pl.program_id(0)
        hv = h_ref[...]
        gv = g_ref[...]
        r = lax.rsqrt(jnp.mean(hv * hv, axis=-1, keepdims=True) + EPS)
        xh = hv * r
        err = xh * gv - t_ref[...]
        dy = err * (1.0 / d)
        dxh = dy * gv
        dh = r * (dxh - xh * jnp.mean(dxh * xh, axis=-1, keepdims=True))
        dh_ref[...] = dh
        dhb_ref[...] = dh.astype(BF16)
        dg_part = jnp.sum(dy * xh, axis=0, keepdims=True)
        sq_part = jnp.sum(err * err, axis=0, keepdims=True)

        @pl.when(i == 0)
        def _():
            dg_ref[...] = dg_part
            sq_ref[...] = sq_part

        @pl.when(i > 0)
        def _():
            dg_ref[...] += dg_part
            sq_ref[...] += sq_part

        @pl.when(i == nsteps - 1)
        def _():
            total = jnp.sum(sq_ref[...], axis=-1, keepdims=True) * (0.5 / d)
            loss_ref[...] = jnp.broadcast_to(total, loss_ref.shape)

    row = lambda i: (i, 0)
    fixed = lambda i: (0, 0)
    return _call(
        name, body, (nsteps,),
        [pl.BlockSpec((tm, d), row), pl.BlockSpec((1, d), fixed), pl.BlockSpec((tm, d), row)],
        [pl.BlockSpec((8, 128), fixed), pl.BlockSpec((tm, d), row), pl.BlockSpec((tm, d), row),
         pl.BlockSpec((1, d), fixed)],
        [_sds((8, 128), F32), _sds((s, d), F32), _sds((s, d), BF16), _sds((1, d), F32)], [h, g, target],
        scratch=[pltpu.VMEM((1, d), F32)], sem=("arbitrary",))[0]


def _mm(name, grid, in_arrays, in_specs, out_shapes, out_specs, acc_tile, dot, epilogue, jobs=(), place=None):
    nk = grid[2]
    n_in = len(in_arrays)
    n_out = len(out_shapes)

    def body(*refs):
        ins, outs = refs[:n_in], refs[n_in:n_in + n_out]
        if nk == 1:
            epilogue(dot(*ins), ins, outs)
            return
        acc = refs[n_in + n_out]
        k = pl.program_id(2)

        @pl.when(k == 0)
        def _():
            acc[...] = dot(*ins)

        @pl.when(jnp.logical_and(k > 0, k < nk - 1))
        def _():
            acc[...] += dot(*ins)

        @pl.when(k == nk - 1)
        def _():
            epilogue(acc[...] + dot(*ins), ins, outs)

    scratch = [pltpu.VMEM(acc_tile, F32)] if nk > 1 else []
    outs, job_outs = _call(name, body, grid, in_specs, out_specs, out_shapes, in_arrays, scratch=scratch,
                           sem=("parallel", "parallel", "arbitrary"), jobs=jobs, place=place)
    return _ret(outs, job_outs, jobs)


def _store(scale, dtype):
    def epilogue(acc, ins, outs):
        outs[0][...] = (acc * scale if scale != 1.0 else acc).astype(dtype)
    return epilogue


def mm_nn(name, a, w, out_dtype, tm=1024, tn=1024, tk=2048, jobs=()):
    m, kd = a.shape
    n = w.shape[1]
    tm, tn, tk = _tile(m, tm, 8), _tile(n, tn), _tile(kd, tk)
    return _mm(
        name, (n // tn, m // tm, kd // tk), [a, w],
        [pl.BlockSpec((tm, tk), lambda j, i, k: (i, k)), pl.BlockSpec((tk, tn), lambda j, i, k: (k, j))],
        [_sds((m, n), out_dtype)], [pl.BlockSpec((tm, tn), lambda j, i, k: (i, j))], (tm, tn),
        lambda a_ref, w_ref: _dot_nn(a_ref[...], w_ref[...]), _store(1.0, out_dtype), jobs)


def mm_nn_resid(name, a, w, x, scale, tm=1024, tn=1024, tk=1408, jobs=()):
    m, kd = a.shape
    n = w.shape[1]
    tm, tn, tk = _tile(m, tm, 8), _tile(n, tn), _tile(kd, tk)

    def epilogue(acc, ins, outs):
        outs[0][...] = ins[2][...] + scale * acc

    return _mm(
        name, (n // tn, m // tm, kd // tk), [a, w, x],
        [pl.BlockSpec((tm, tk), lambda j, i, k: (i, k)), pl.BlockSpec((tk, tn), lambda j, i, k: (k, j)),
         pl.BlockSpec((tm, tn), lambda j, i, k: (i, j))],
        [_sds((m, n), F32)], [pl.BlockSpec((tm, tn), lambda j, i, k: (i, j))], (tm, tn),
        lambda a_ref, w_ref, x_ref: _dot_nn(a_ref[...], w_ref[...]), epilogue, jobs)


def mm_nt(name, a, w, out_dtype, scale=1.0, tm=1024, tn=1024, tk=2048, jobs=()):
    m, kd = a.shape
    n = w.shape[0]
    tm, tn, tk = _tile(m, tm, 8), _tile(n, tn), _tile(kd, tk)
    return _mm(
        name, (n // tn, m // tm, kd // tk), [a, w],
        [pl.BlockSpec((tm, tk), lambda j, i, k: (i, k)), pl.BlockSpec((tn, tk), lambda j, i, k: (j, k))],
        [_sds((m, n), out_dtype)], [pl.BlockSpec((tm, tn), lambda j, i, k: (i, j))], (tm, tn),
        lambda a_ref, w_ref: _dot_nt(a_ref[...], w_ref[...]), _store(scale, out_dtype), jobs)


def mm_nt_pair(name, a3, w, out_dtype, tm=1024, tn=1024, tk=2816, jobs=()):
    _, m, f = a3.shape
    n = w.shape[0]
    tm, tn, tk = _tile(m, tm, 8), _tile(n, tn), _tile(f, tk)
    nkf = f // tk
    return _mm(
        name, (n // tn, m // tm, 2 * nkf), [a3, w],
        [pl.BlockSpec((None, tm, tk), lambda j, i, k: (k // nkf, i, k % nkf)),
         pl.BlockSpec((tn, tk), lambda j, i, k: (j, k))],
        [_sds((m, n), out_dtype)], [pl.BlockSpec((tm, tn), lambda j, i, k: (i, j))], (tm, tn),
        lambda a_ref, w_ref: _dot_nt(a_ref[...], w_ref[...]), _store(1.0, out_dtype), jobs)


def mm_nt_norm_bwd(name, a, w, x, r, g, dh_in, tm=512, tk=1408, jobs=()):
    pair = a.ndim == 3
    m, kd = a.shape[-2], a.shape[-1]
    d = w.shape[0]
    tm, tk = _tile(m, tm, 8), _tile(kd, tk)
    nkf = kd // tk
    nk = 2 * nkf if pair else nkf
    if pair:
        a_spec = pl.BlockSpec((None, tm, tk), lambda i, k: (k // nkf, i, k % nkf))
    else:
        a_spec = pl.BlockSpec((tm, tk), lambda i, k: (i, k))
    row = lambda i, k: (i, 0)
    fixed = lambda i, k: (0, 0)

    def body(a_ref, w_ref, x_ref, r_ref, g_ref, dh_ref, out_ref, outb_ref, dg_ref, *acc):
        i, k = pl.program_id(0), pl.program_id(1)
        dot = lambda: _dot_nt(a_ref[...], w_ref[...])

        def finish(dn):
            xh = x_ref[...] * r_ref[...]
            dxh = dn * g_ref[...]
            out = dh_ref[...] + r_ref[...] * (dxh - xh * jnp.mean(dxh * xh, axis=-1, keepdims=True))
            out_ref[...] = out
            outb_ref[...] = out.astype(BF16)
            part = jnp.sum(dn * xh, axis=0, keepdims=True)

            @pl.when(i == 0)
            def _():
                dg_ref[...] = part

            @pl.when(i > 0)
            def _():
                dg_ref[...] += part

        if nk == 1:
            finish(dot())
            return

        @pl.when(k == 0)
        def _():
            acc[0][...] = dot()

        @pl.when(jnp.logical_and(k > 0, k < nk - 1))
        def _():
            acc[0][...] += dot()

        @pl.when(k == nk - 1)
        def _():
            finish(acc[0][...] + dot())

    outs, job_outs = _call(
        name, body, (m // tm, nk),
        [a_spec, pl.BlockSpec((d, tk), lambda i, k: (0, k)), pl.BlockSpec((tm, d), row), pl.BlockSpec((tm, 1), row),
         pl.BlockSpec((1, d), fixed), pl.BlockSpec((tm, d), row)],
        [pl.BlockSpec((tm, d), row), pl.BlockSpec((tm, d), row), pl.BlockSpec((1, d), fixed)],
        [_sds((m, d), F32), _sds((m, d), BF16), _sds((1, d), F32)], [a, w, x, r, g, dh_in],
        scratch=[pltpu.VMEM((tm, d), F32)] if nk > 1 else [], sem=("arbitrary", "arbitrary"), jobs=jobs)
    return _ret(outs, job_outs, jobs, single=False)


def mm_tn(name, a, b, out_dtype, scale=1.0, tm=1024, tn=1024, tk=4096, jobs=()):
    kd, m = a.shape
    n = b.shape[1]
    tm, tn, tk = _tile(m, tm), _tile(n, tn), _tile(kd, tk, 16)
    return _mm(
        name, (n // tn, m // tm, kd // tk), [a, b],
        [pl.BlockSpec((tk, tm), lambda j, i, k: (k, i)), pl.BlockSpec((tk, tn), lambda j, i, k: (k, j))],
        [_sds((m, n), out_dtype)], [pl.BlockSpec((tm, tn), lambda j, i, k: (i, j))], (tm, tn),
        lambda a_ref, b_ref: _dot_tn(a_ref[...], b_ref[...]), _store(scale, out_dtype), jobs)


def mm_tn_pair(name, a, b3, out_dtype, tm=1024, tn=512, tk=4096, jobs=()):
    kd, m = a.shape
    f = b3.shape[2]
    tm, tn, tk = _tile(m, tm), _tile(f, tn), _tile(kd, tk, 16)
    nf = f // tn
    return _mm(
        name, (m // tm, 2 * nf, kd // tk), [a, b3],
        [pl.BlockSpec((tk, tm), lambda i, j, k: (k, i)),
         pl.BlockSpec((None, tk, tn), lambda i, j, k: (j // nf, k, j % nf))],
        [_sds((m, 2 * f), out_dtype)], [pl.BlockSpec((tm, tn), lambda i, j, k: (i, j))], (tm, tn),
        lambda a_ref, b_ref: _dot_tn(a_ref[...], b_ref[...]), _store(1.0, out_dtype), jobs)


def mm_tn_pair_half(name, a, b3, out_dtype, place, mine, tm=1024, tn=512, tk=4096, jobs=()):
    kd, m = a.shape
    f = b3.shape[2]
    tm, tn, tk = _tile(m // 2, tm), _tile(f, tn), _tile(kd, tk, 16)
    nf, nbm = f // tn, m // 2 // tm
    which = (lambda p: p[1]) if mine else (lambda p: 1 - p[1])
    return _mm(
        name, (nbm, 2 * nf, kd // tk), [a, b3],
        [pl.BlockSpec((tk, tm), lambda i, j, k, p: (k, i + which(p) * nbm)),
         pl.BlockSpec((None, tk, tn), lambda i, j, k, p: (j // nf, k, j % nf))],
        [_sds((m // 2, 2 * f), out_dtype)], [pl.BlockSpec((tm, tn), lambda i, j, k, p: (i, j))], (tm, tn),
        lambda a_ref, b_ref: _dot_tn(a_ref[...], b_ref[...]), _store(1.0, out_dtype), jobs, place)


def swiglu_fwd(name, n, w_in, tm=1024, tn=512, jobs=(), stride=1, phase=0, prev=None, compact=False):
    s, d = n.shape
    f = w_in.shape[1] // 2 * (stride if compact else 1)
    tm, tn = _tile(s, tm, 8), _tile(f, tn)
    nf = f // tn
    col = lambda j: j * stride + phase
    w_gate = (lambda j: j) if compact else col
    w_up = (lambda j: j + nf // stride) if compact else (lambda j: col(j) + nf)

    def body(n_ref, wg_ref, wu_ref, *rest):
        gu_ref, a_ref = rest[-2:]
        nv = n_ref[...]
        g = _dot_nn(nv, wg_ref[...])
        u = _dot_nn(nv, wu_ref[...])
        gu_ref[0] = g.astype(BF16)
        gu_ref[1] = u.astype(BF16)
        a_ref[...] = (g * jax.nn.sigmoid(g) * u).astype(BF16)

    kept = list(prev) if prev is not None else []
    outs, job_outs = _call(
        name, body, (nf // stride, s // tm),
        [pl.BlockSpec((tm, d), lambda j, i: (i, 0)), pl.BlockSpec((d, tn), lambda j, i: (0, w_gate(j))),
         pl.BlockSpec((d, tn), lambda j, i: (0, w_up(j)))] + [ANY] * len(kept),
        [pl.BlockSpec((2, tm, tn), lambda j, i: (0, i, col(j))), pl.BlockSpec((tm, tn), lambda j, i: (i, col(j)))],
        [_sds((2, s, f), BF16), _sds((s, f), BF16)], [n, w_in, w_in] + kept, sem=("parallel", "parallel"),
        jobs=jobs, carried={3 + k: k for k in range(len(kept))})
    return _ret(outs, job_outs, jobs, single=False)


def swiglu_bwd(name, dh, w_out, gu, scale, tm=1024, tn=512, jobs=()):
    s, d = dh.shape
    f = w_out.shape[0]
    tm, tn = _tile(s, tm, 8), _tile(f, tn)

    sub = _tile(tm, 256, 8)

    def body(dh_ref, w_ref, gu_ref, out_ref):
        for lo in range(0, tm, sub):
            rows = slice(lo, lo + sub)
            da = (_dot_nt(dh_ref[rows, :], w_ref[...]) * scale).astype(BF16)
            g = gu_ref[0, rows, :]
            u = gu_ref[1, rows, :]
            sg = 0.5 * jnp.tanh(0.5 * g) + 0.5
            t = g * sg
            out_ref[0, rows, :] = da * (u * (sg + t * (1.0 - sg)))
            out_ref[1, rows, :] = da * t

    outs, job_outs = _call(
        name, body, (f // tn, s // tm),
        [pl.BlockSpec((tm, d), lambda j, i: (i, 0)), pl.BlockSpec((tn, d), lambda j, i: (j, 0)),
         pl.BlockSpec((2, tm, tn), lambda j, i: (0, i, j))],
        [pl.BlockSpec((2, tm, tn), lambda j, i: (0, i, j))],
        [_sds((2, s, f), BF16)], [dh, w_out, gu], sem=("parallel", "parallel"), jobs=jobs)
    return _ret(outs, job_outs, jobs)


HALO = 16


def _conv_inputs(z_ref, hgc_ref, hhc_ref, i, cw, tm):
    gc = z_ref[:, cw:2 * cw].astype(F32)
    hc = z_ref[:, 2 * cw:3 * cw].astype(F32)
    cin = gc * hc
    halo = hgc_ref[...].astype(F32) * hhc_ref[...].astype(F32) * (i > 0).astype(F32)
    row = lax.broadcasted_iota(jnp.int32, (tm, cw), 0)
    x1 = jnp.where(row == 0, halo[HALO - 1:HALO], pltpu.roll(cin, 1, 0))
    x2 = jnp.where(row == 0, halo[HALO - 2:HALO - 1], jnp.where(row == 1, halo[HALO - 1:HALO], pltpu.roll(cin, 2, 0)))
    return gc, hc, cin, x1, x2


def _tril(w):
    r = lax.broadcasted_iota(jnp.int32, w.shape, 0)
    c = lax.broadcasted_iota(jnp.int32, w.shape, 1)
    return jnp.where(r >= c, w, jnp.zeros_like(w))


def mixer_fwd(name, z, conv_w, conv_b, g_v, w_s, b_t, tm=256, jobs=()):
    s, zc = z.shape
    cw = conv_w.shape[1]
    gw = g_v.shape[1]
    heads = gw // GROUP
    tm = _tile(s, tm)
    hb = tm // HALO

    def body(z_ref, hgc_ref, hhc_ref, cw_ref, cb_ref, gv_ref, ws_ref, bt_ref, y_ref):
        i = pl.program_id(0)
        _, _, cin, x1, x2 = _conv_inputs(z_ref, hgc_ref, hhc_ref, i, cw, tm)
        cv = cb_ref[...] + cw_ref[2:3, :] * cin + cw_ref[1:2, :] * x1 + cw_ref[0:1, :] * x2
        y_ref[:, 0:cw] = (z_ref[:, 0:cw].astype(F32) * cv).astype(BF16)
        for h in range(heads):
            lo = h * GROUP
            vh = z_ref[:, 3 * cw + gw + lo:3 * cw + gw + lo + GROUP].astype(F32)
            rv = lax.rsqrt(jnp.mean(vh * vh, axis=-1, keepdims=True) + EPS)
            vn = (vh * rv * gv_ref[:, lo:lo + GROUP]).astype(BF16)
            w = _tril(ws_ref[h]).astype(BF16)
            for n in range(tm // GROUP):
                rows = slice(n * GROUP, (n + 1) * GROUP)
                sg = _dot_nn(w, vn[rows]) + bt_ref[:, h:h + 1]
                u = z_ref[rows, 3 * cw + lo:3 * cw + lo + GROUP].astype(F32)
                y_ref[rows, cw + lo:cw + lo + GROUP] = (u * sg).astype(BF16)

    fixed2 = lambda i: (0, 0)
    outs, job_outs = _call(
        name, body, (s // tm,),
        [pl.BlockSpec((tm, zc), lambda i: (i, 0)),
         pl.BlockSpec((HALO, cw), lambda i: (jnp.maximum(i * hb - 1, 0), 1)),
         pl.BlockSpec((HALO, cw), lambda i: (jnp.maximum(i * hb - 1, 0), 2)),
         pl.BlockSpec(conv_w.shape, fixed2), pl.BlockSpec(conv_b.shape, fixed2),
         pl.BlockSpec(g_v.shape, fixed2), pl.BlockSpec(w_s.shape, lambda i: (0, 0, 0)),
         pl.BlockSpec(b_t.shape, fixed2)],
        [pl.BlockSpec((tm, cw + gw), lambda i: (i, 0))], [_sds((s, cw + gw), BF16)],
        [z, z, z, conv_w, conv_b, g_v, w_s, b_t], sem=("arbitrary",), jobs=jobs)
    return _ret(outs, job_outs, jobs)


def mixer_bwd(name, z, dy, conv_w, conv_b, g_v, w_s, b_t, tm=256, jobs=()):
    s, zc = z.shape
    cw = conv_w.shape[1]
    gw = g_v.shape[1]
    heads = gw // GROUP
    tm = _tile(s, tm)
    hb = tm // HALO
    nsteps = s // tm
    last_halo = s // HALO - 1

    def body(z_ref, hgc_ref, hhc_ref, ngb_ref, dy_ref, ndy_ref, cw_ref, cb_ref, gv_ref, ws_ref, bt_ref,
             dz_ref, sm_ref, dws_ref, dbt_ref, dsg_ref):
        i = pl.program_id(0)

        @pl.when(i == 0)
        def _():
            sm_ref[...] = jnp.zeros_like(sm_ref)
            dws_ref[...] = jnp.zeros_like(dws_ref)
            dsg_ref[...] = jnp.zeros_like(dsg_ref)

        gc, hc, cin, x1, x2 = _conv_inputs(z_ref, hgc_ref, hhc_ref, i, cw, tm)
        w0, w1, w2 = cw_ref[0:1, :], cw_ref[1:2, :], cw_ref[2:3, :]
        cv = cb_ref[...] + w2 * cin + w1 * x1 + w0 * x2
        gb = z_ref[:, 0:cw].astype(F32)
        dyc = dy_ref[:, 0:cw].astype(F32)
        dz_ref[:, 0:cw] = (dyc * cv).astype(BF16)
        dcv = dyc * gb
        nxt = ndy_ref[...].astype(F32) * ngb_ref[...].astype(F32) * (i < nsteps - 1).astype(F32)
        row = lax.broadcasted_iota(jnp.int32, (tm, cw), 0)
        d1 = jnp.where(row == tm - 1, nxt[0:1], pltpu.roll(dcv, tm - 1, 0))
        d2 = jnp.where(row == tm - 1, nxt[1:2], jnp.where(row == tm - 2, nxt[0:1], pltpu.roll(dcv, tm - 2, 0)))
        dcin = w2 * dcv + w1 * d1 + w0 * d2
        dz_ref[:, cw:2 * cw] = (dcin * hc).astype(BF16)
        dz_ref[:, 2 * cw:3 * cw] = (dcin * gc).astype(BF16)
        sm_ref[0:1, :] += jnp.sum(dcv * x2, axis=0, keepdims=True)
        sm_ref[1:2, :] += jnp.sum(dcv * x1, axis=0, keepdims=True)
        sm_ref[2:3, :] += jnp.sum(dcv * cin, axis=0, keepdims=True)
        sm_ref[3:4, :] += jnp.sum(dcv, axis=0, keepdims=True)

        for h in range(heads):
            lo = h * GROUP
            vcol = slice(3 * cw + gw + lo, 3 * cw + gw + lo + GROUP)
            ucol = slice(3 * cw + lo, 3 * cw + lo + GROUP)
            vh = z_ref[:, vcol].astype(F32)
            rv = lax.rsqrt(jnp.mean(vh * vh, axis=-1, keepdims=True) + EPS)
            xh = vh * rv
            gvh = gv_ref[:, lo:lo + GROUP]
            vn = (xh * gvh).astype(BF16)
            w = _tril(ws_ref[h]).astype(BF16)
            dgv = jnp.zeros((1, GROUP), F32)
            for n in range(tm // GROUP):
                rows = slice(n * GROUP, (n + 1) * GROUP)
                sg = _dot_nn(w, vn[rows]) + bt_ref[:, h:h + 1]
                dyg = dy_ref[rows, cw + lo:cw + lo + GROUP].astype(F32)
                dsg = dyg * z_ref[rows, ucol].astype(F32)
                dz_ref[rows, ucol] = (dyg * sg).astype(BF16)
                dsgb = dsg.astype(BF16)
                dvn = _dot_tn(w, dsgb)
                dws_ref[h] += _dot_nt(dsgb, vn[rows])
                dsg_ref[:, lo:lo + GROUP] += dsg
                xhc = xh[rows]
                dgv = dgv + jnp.sum(dvn * xhc, axis=0, keepdims=True)
                dxh = dvn * gvh
                dv = rv[rows] * (dxh - xhc * jnp.mean(dxh * xhc, axis=-1, keepdims=True))
                dz_ref[rows, vcol] = dv.astype(BF16)
            sm_ref[4:5, lo:lo + GROUP] += dgv

        @pl.when(i == nsteps - 1)
        def _():
            for h in range(heads):
                dws_ref[h] = _tril(dws_ref[h])
                dbt_ref[:, h:h + 1] = jnp.sum(dsg_ref[:, h * GROUP:(h + 1) * GROUP], axis=-1, keepdims=True)

    fixed2 = lambda i: (0, 0)
    fixed3 = lambda i: (0, 0, 0)
    prev = lambda col: (lambda i: (jnp.maximum(i * hb - 1, 0), col))
    nxt_blk = lambda i: (jnp.minimum((i + 1) * hb, last_halo), 0)
    outs, job_outs = _call(
        name, body, (nsteps,),
        [pl.BlockSpec((tm, zc), lambda i: (i, 0)),
         pl.BlockSpec((HALO, cw), prev(1)), pl.BlockSpec((HALO, cw), prev(2)),
         pl.BlockSpec((HALO, cw), nxt_blk),
         pl.BlockSpec((tm, cw + gw), lambda i: (i, 0)), pl.BlockSpec((HALO, cw), nxt_blk),
         pl.BlockSpec(conv_w.shape, fixed2), pl.BlockSpec(conv_b.shape, fixed2),
         pl.BlockSpec(g_v.shape, fixed2), pl.BlockSpec(w_s.shape, fixed3), pl.BlockSpec(b_t.shape, fixed2)],
        [pl.BlockSpec((tm, zc), lambda i: (i, 0)), pl.BlockSpec((8, cw), fixed2),
         pl.BlockSpec(w_s.shape, fixed3), pl.BlockSpec(b_t.shape, fixed2)],
        [_sds((s, zc), BF16), _sds((8, cw), F32), _sds(w_s.shape, F32), _sds(b_t.shape, F32)],
        [z, z, z, z, dy, dy, conv_w, conv_b, g_v, w_s, b_t],
        scratch=[pltpu.VMEM((GROUP, gw), F32)], sem=("arbitrary",), jobs=jobs)
    return _ret(outs, job_outs, jobs, single=False)


def _softmax_rows(sc):
    e = jnp.exp(sc - jnp.max(sc, axis=-1, keepdims=True))
    return e / jnp.sum(e, axis=-1, keepdims=True)


def attn_fwd(name, q, k, v, tm=512, jobs=()):
    s, d = q.shape
    m = k.shape[0]
    hd = d // XA_HEADS
    scale = hd ** -0.5
    tm = _tile(s, tm, 8)

    def body(q_ref, k_ref, v_ref, o_ref):
        for h in range(XA_HEADS):
            cols = slice(h * hd, (h + 1) * hd)
            p = _softmax_rows(_dot_nt(q_ref[:, cols], k_ref[:, cols]) * scale)
            o_ref[:, cols] = _dot_nn(p.astype(BF16), v_ref[:, cols]).astype(BF16)

    outs, job_outs = _call(
        name, body, (s // tm,),
        [pl.BlockSpec((tm, d), lambda i: (i, 0)), pl.BlockSpec((m, d), lambda i: (0, 0)),
         pl.BlockSpec((m, d), lambda i: (0, 0))],
        [pl.BlockSpec((tm, d), lambda i: (i, 0))], [_sds((s, d), BF16)], [q, k, v], sem=("arbitrary",), jobs=jobs)
    return _ret(outs, job_outs, jobs)


def attn_bwd(name, q, k, v, do, tm=512):
    s, d = q.shape
    m = k.shape[0]
    hd = d // XA_HEADS
    scale = hd ** -0.5
    tm = _tile(s, tm, 8)

    def body(q_ref, k_ref, v_ref, do_ref, dq_ref, dk_ref, dv_ref):
        i = pl.program_id(0)

        @pl.when(i == 0)
        def _():
            dk_ref[...] = jnp.zeros_like(dk_ref)
            dv_ref[...] = jnp.zeros_like(dv_ref)

        for h in range(XA_HEADS):
            cols = slice(h * hd, (h + 1) * hd)
            qh = q_ref[:, cols]
            doh = do_ref[:, cols]
            p = _softmax_rows(_dot_nt(qh, k_ref[:, cols]) * scale)
            dp = _dot_nt(doh, v_ref[:, cols])
            ds = (p * (dp - jnp.sum(dp * p, axis=-1, keepdims=True)) * scale).astype(BF16)
            dq_ref[:, cols] = _dot_nn(ds, k_ref[:, cols]).astype(BF16)
            dk_ref[:, cols] += _dot_tn(ds, qh)
            dv_ref[:, cols] += _dot_tn(p.astype(BF16), doh)

    row = lambda i: (i, 0)
    fixed = lambda i: (0, 0)
    return _call(
        name, body, (s // tm,),
        [pl.BlockSpec((tm, d), row), pl.BlockSpec((m, d), fixed), pl.BlockSpec((m, d), fixed),
         pl.BlockSpec((tm, d), row)],
        [pl.BlockSpec((tm, d), row), pl.BlockSpec((m, d), fixed), pl.BlockSpec((m, d), fixed)],
        [_sds((s, d), BF16), _sds((m, d), F32), _sds((m, d), F32)], [q, k, v, do], sem=("arbitrary",))[0]


def _grid2(rows, cols, row_mult):
    tr, tc = _tile(rows, 512, row_mult), _tile(cols, 2048)
    return tr, tc, rows // tr, cols // tc


def cast_place(name, block, axis, place, column_half=None):
    r, c = block.shape
    if column_half is not None:
        c //= 2
    tr, tc, nbr, nbc = _grid2(r, c, 16)
    first = 0 if column_half is None else column_half * nbc
    if axis == 1:
        dst = lambda i, j, p: (i, j + p[0] * nbc)
    else:
        dst = lambda i, j, p: (i + p[0] * nbr, j)

    def body(p_ref, w_ref, out_ref):
        out_ref[...] = w_ref[...].astype(BF16)

    return pl.pallas_call(
        body, name=name,
        grid_spec=pltpu.PrefetchScalarGridSpec(
            num_scalar_prefetch=1, grid=(nbr, nbc),
            in_specs=[pl.BlockSpec((tr, tc), lambda i, j, p: (i, j + first))],
            out_specs=pl.BlockSpec((tr, tc), dst)),
        out_shape=_sds(_full_shape((r, c), axis), BF16),
        compiler_params=_params(("parallel", "parallel")),
    )(place, block)


def merge_column_halves(name, left, right):
    r, c = left.shape
    w = c // N_CHIPS
    tr = _tile(r, 512, 16)

    def body(l_ref, r_ref, out_ref):
        side = pl.program_id(1) % 2

        @pl.when(side == 0)
        def _():
            out_ref[...] = l_ref[...]

        @pl.when(side == 1)
        def _():
            out_ref[...] = r_ref[...]

    half = pl.BlockSpec((tr, w), lambda i, j: (i, j // 2))
    return _call(name, body, (r // tr, 2 * N_CHIPS), [half, half], [pl.BlockSpec((tr, w), lambda i, j: (i, j))],
                 [_sds((r, 2 * c), left.dtype)], [left, right], sem=("parallel", "arbitrary"))[0][0]


def pair_add(name, grad, peer, axis, place):
    hr, hc = peer.shape
    tr, tc, nbr, nbc = _grid2(hr, hc, 16)
    same = lambda i, j, p: (i, j)
    if grad.shape == peer.shape:
        mine = same
    elif axis == 1:
        mine = lambda i, j, p: (i + p[1] * nbr, j)
    else:
        mine = lambda i, j, p: (i, j + p[1] * nbc)

    def body(p_ref, g_ref, q_ref, out_ref):
        out_ref[...] = (g_ref[...].astype(F32) + q_ref[...].astype(F32)).astype(BF16)

    return pl.pallas_call(
        body, name=name,
        grid_spec=pltpu.PrefetchScalarGridSpec(
            num_scalar_prefetch=1, grid=(nbr, nbc),
            in_specs=[pl.BlockSpec((tr, tc), mine), pl.BlockSpec((tr, tc), same)],
            out_specs=pl.BlockSpec((tr, tc), same)),
        out_shape=_sds((hr, hc), BF16),
        compiler_params=_params(("parallel", "parallel")),
    )(place, grad, peer)


def cross_sum(name, part, land, axis, shape, place):
    _, sr, sc = land.shape
    tr, tc, nbr, nbc = _grid2(sr, sc, 16)
    if axis == 1:
        own = lambda i, j, p: (i, j + p[0] * nbc)
        dst = lambda i, j, p: (i + p[1] * nbr, j)
    else:
        own = lambda i, j, p: (i + p[0] * nbr, j)
        dst = lambda i, j, p: (i, j + p[1] * nbc)

    def body(p_ref, own_ref, land_ref, out_ref):
        out_ref[...] = ((own_ref[...].astype(F32) + land_ref[0].astype(F32))
                        + (land_ref[1].astype(F32) + land_ref[2].astype(F32)))

    return pl.pallas_call(
        body, name=name,
        grid_spec=pltpu.PrefetchScalarGridSpec(
            num_scalar_prefetch=1, grid=(nbr, nbc),
            in_specs=[pl.BlockSpec((tr, tc), own), pl.BlockSpec((3, tr, tc), lambda i, j, p: (0, i, j))],
            out_specs=pl.BlockSpec((tr, tc), dst)),
        out_shape=_sds(_block(shape, axis), F32),
        compiler_params=_params(("parallel", "parallel")),
    )(place, part, land)


def _adam_math(w, g, m, v):
    m = ADAM_B1 * m + (1.0 - ADAM_B1) * g
    v = ADAM_B2 * v + (1.0 - ADAM_B2) * (g * g)
    m_hat = m / (1.0 - ADAM_B1 ** ADAM_STEP)
    v_hat = v / (1.0 - ADAM_B2 ** ADAM_STEP)
    delta = -ADAM_LR * (m_hat / (jnp.sqrt(v_hat) + ADAM_EPS) + ADAM_WD * w)
    return delta, m, v


def adamw(name, w, g, m, v, jobs=()):
    r, c = w.shape
    tr, tc = _tile(r, 256, 8), _tile(c, 1408)

    def body(w_ref, g_ref, m_ref, v_ref, g_out, d_out, m_out, v_out):
        d, mm, vv = _adam_math(w_ref[...], g_ref[...], m_ref[...], v_ref[...])
        g_out[...] = g_ref[...]
        d_out[...] = d
        m_out[...] = mm
        v_out[...] = vv

    spec = pl.BlockSpec((tr, tc), lambda i, j: (i, j))
    outs, job_outs = _call(name, body, (r // tr, c // tc), [spec] * 4, [spec] * 4, [_sds((r, c), F32)] * 4,
                           [w, g, m, v], sem=("parallel", "parallel"), jobs=jobs)
    return _ret(outs, job_outs, jobs, single=False)


def small_sum(name, stacks):
    def body(*refs):
        for s_ref, out_ref in zip(refs[:len(stacks)], refs[len(stacks):]):
            acc = s_ref[0]
            for d in range(1, s_ref.shape[0]):
                acc = acc + s_ref[d]
            out_ref[...] = acc

    return pl.pallas_call(body, name=name, out_shape=[_sds(s.shape[1:], F32) for s in stacks])(*stacks)


WEIGHTS = ["g_ffn1", "w_ffn1_in", "w_ffn1_out", "g_mix", "w_mix_in", "conv_w", "conv_b", "g_gm_v", "w_spatial",
           "b_spatial", "w_mix_out", "g_xattn", "g_mem", "w_xq", "w_xk", "w_xv", "w_xo", "g_ffn2", "w_ffn2_in",
           "w_ffn2_out", "g_final"]
BIG = {"w_ffn1_in": 1, "w_ffn1_out": 0, "w_mix_in": 1, "w_mix_out": 0, "w_xq": 0, "w_xk": 0, "w_xv": 0, "w_xo": 0,
       "w_ffn2_in": 1, "w_ffn2_out": 0}
SMALL = [n for n in WEIGHTS if n not in BIG]
LATE_SMALL = ["g_ffn1"]
EARLY_SMALL = [n for n in SMALL if n not in LATE_SMALL]


def _pack(arrays):
    flat = jnp.concatenate([a.reshape(-1) for a in arrays])
    rows = -(-flat.shape[0] // 1024) * 8
    return jnp.pad(flat, (0, rows * 128 - flat.shape[0])).reshape(rows, 128)


def _unpack(buf, shapes):
    flat = buf.reshape(-1)
    out, pos = [], 0
    for shp in shapes:
        n = math.prod(shp)
        out.append(flat[pos:pos + n].reshape(shp))
        pos += n
    return out


def kernel(x, mem, g_ffn1, w_ffn1_in, w_ffn1_out, g_mix, w_mix_in, conv_w, conv_b, g_gm_v, w_spatial, b_spatial, w_mix_out, g_xattn, g_mem, w_xq, w_xk, w_xv, w_xo, g_ffn2, w_ffn2_in, w_ffn2_out, g_final, loss_target, m_g_ffn1, m_w_ffn1_in, m_w_ffn1_out, m_g_mix, m_w_mix_in, m_conv_w, m_conv_b, m_g_gm_v, m_w_spatial, m_b_spatial, m_w_mix_out, m_g_xattn, m_g_mem, m_w_xq, m_w_xk, m_w_xv, m_w_xo, m_g_ffn2, m_w_ffn2_in, m_w_ffn2_out, m_g_final, v_g_ffn1, v_w_ffn1_in, v_w_ffn1_out, v_g_mix, v_w_mix_in, v_conv_w, v_conv_b, v_g_gm_v, v_w_spatial, v_b_spatial, v_w_mix_out, v_g_xattn, v_g_mem, v_w_xq, v_w_xk, v_w_xv, v_w_xo, v_g_ffn2, v_w_ffn2_in, v_w_ffn2_out, v_g_final):
    given = dict(locals())
    wts = {n: given[n] for n in WEIGHTS}
    mom = {n: given["m_" + n] for n in WEIGHTS}
    var = {n: given["v_" + n] for n in WEIGHTS}

    xi, yi, ci = lax.axis_index("x"), lax.axis_index("y"), lax.axis_index("c")
    blk = 2 * xi + yi
    place = jnp.stack([blk, ci]).astype(jnp.int32)

    x2, mem2, tgt = x[0], mem[0], loss_target[0]
    w_s, b_t = w_spatial[0], b_spatial[0].T
    gf = g_final[None]

    rest = [n for n in BIG if n != "w_ffn1_in"]
    own = {n: cast_place("cast_" + n, wts[n][0], BIG[n], place) for n in rest}
    own_left = cast_place("cast_w_ffn1_in_left", wts["w_ffn1_in"][0], 1, place, column_half=0)
    own_right = cast_place("cast_w_ffn1_in_right", wts["w_ffn1_in"][0], 1, place, column_half=1)
    shape = {n: own[n].shape for n in rest}
    shape["w_ffn1_in"] = _full_shape(wts["w_ffn1_in"][0].shape, 1)
    full = {}
    (w1_left,) = gather_by_sequencer("gather_w_ffn1_in_left", [own_left], [1], 1)
    (w1_right,) = gather_by_sequencer("gather_w_ffn1_in_right", [own_right], [1], 2)
    groups = [["w_ffn1_out"], ["w_mix_in", "w_mix_out"], ["w_xq", "w_xk", "w_xv", "w_xo"], ["w_ffn2_in"],
              ["w_ffn2_out"]]
    for g, names in enumerate(groups):
        got = gather_by_sequencer("gather_" + "_".join(names), [own[n] for n in names], [BIG[n] for n in names],
                                  3 + g)
        full.update(zip(names, got))
    ((conv_taps,),) = comm_only("gather_conv_taps", [columns_job(jnp.pad(conv_w[0], ((0, 8 - CONV_K), (0, 0))))])

    half_cols = dict(tm=512, tn=shape["w_ffn1_in"][1] // (2 * N_CHIPS), stride=2, compact=True)
    n1, r1 = rmsnorm_fwd("norm1", x2, g_ffn1)
    halves = swiglu_fwd("ffn1_in_left", n1, w1_left, phase=0, **half_cols)
    gu1, a1 = swiglu_fwd("ffn1_in_right", n1, w1_right, phase=1, prev=halves, **half_cols)
    h1 = mm_nn_resid("ffn1_out", a1, full["w_ffn1_out"], x2, 0.5, tm=512, tk=5632)
    n2, r2 = rmsnorm_fwd("norm2", h1, g_mix)
    z = mm_nn("mix_in", n2, full["w_mix_in"], BF16)
    ycat = mixer_fwd("mixer", z, conv_taps, conv_b, g_gm_v, w_s, b_t)
    h2 = mm_nn_resid("mix_out", ycat, full["w_mix_out"], h1, 1.0, tk=2048)
    n3, r3 = rmsnorm_fwd("norm3", h2, g_xattn)
    mn, rm = rmsnorm_fwd("norm_mem", mem2, g_mem)
    q = mm_nn("xq", n3, full["w_xq"], BF16)
    k = mm_nn("xk", mn, full["w_xk"], BF16)
    v = mm_nn("xv", mn, full["w_xv"], BF16)
    o = attn_fwd("attn", q, k, v)
    h3 = mm_nn_resid("xo", o, full["w_xo"], h2, 1.0, tk=2048)
    n4, r4 = rmsnorm_fwd("norm4", h3, g_ffn2)
    gu2, a2 = swiglu_fwd("ffn2_in", n4, full["w_ffn2_in"])
    h4 = mm_nn_resid("ffn2_out", a2, full["w_ffn2_out"], h3, 0.5, tm=512, tk=5632)
    loss_blk, dh4, dh4b, dg_final = loss_head("loss_head", h4, gf, tgt)
    full["w_ffn1_in"] = merge_column_halves("merge_w_ffn1_in", w1_left, w1_right)

    dw, part, land, half, grads = {}, {}, {}, {}, {}
    launched = []

    def send_pair(*names):
        return pair_job([dw[n] for n in names], [BIG[n] for n in names])

    def take_pair(names, got):
        for n, p in zip(names, got):
            part[n] = pair_add("pair_add_" + n, dw[n], p, BIG[n], place)

    def start_cross(*names):
        kept, landed = cross_by_sequencer("cross_" + "_".join(names), [(part[n], BIG[n], shape[n]) for n in names],
                                          8 + len(launched) % 2)
        launched.append(names)
        for n, p, l in zip(names, kept, landed):
            part[n], land[n] = p, l

    def finish_cross(*names):
        for n in names:
            half[n] = cross_sum("cross_sum_" + n, part[n], land[n], BIG[n], shape[n], place)

    def send_final(*names):
        return final_job([half[n] for n in names], [BIG[n] for n in names], [shape[n] for n in names])

    delta, new_m, new_v = {}, {}, {}

    def take_final(names, got):
        for n, g in zip(names, got):
            grads[n], delta[n], new_m[n], new_v[n] = adamw("adamw_" + n, wts[n][0], g, mom[n][0], var[n][0])

    dgu2 = swiglu_bwd("ffn2_dact", dh4b, full["w_ffn2_out"], gu2, 0.5)
    dw["w_ffn2_in"] = mm_tn_pair("ffn2_dwin", n4, dgu2, BF16)
    dw["w_ffn2_out"], (got,) = mm_tn("ffn2_dwout", a2, dh4b, BF16, scale=0.5, jobs=[send_pair("w_ffn2_in")])
    take_pair(["w_ffn2_in"], got)
    start_cross("w_ffn2_in")
    (dh3, dh3b, dg_ffn2), (got,) = mm_nt_norm_bwd("ffn2_dn", dgu2, full["w_ffn2_in"], h3, r4, g_ffn2, dh4,
                                                  jobs=[send_pair("w_ffn2_out")])
    take_pair(["w_ffn2_out"], got)
    start_cross("w_ffn2_out")

    dw["w_xo"] = mm_tn("xo_dw", o, dh3b, BF16)
    finish_cross("w_ffn2_in")
    do, (got,) = mm_nt("xo_dx", dh3b, full["w_xo"], BF16, jobs=[send_final("w_ffn2_in")])
    take_final(["w_ffn2_in"], got)
    dq, dk, dv = attn_bwd("attn_bwd", q, k, v, do)
    dkb, dvb = dk.astype(BF16), dv.astype(BF16)
    dw["w_xq"] = mm_tn("xq_dw", n3, dq, BF16)
    dh2, dh2b, dg_xattn = mm_nt_norm_bwd("xq_dx", dq, full["w_xq"], h2, r3, g_xattn, dh3, tk=1024)
    dw["w_xk"] = mm_tn("xk_dw", mn, dkb, BF16)
    dw["w_xv"] = mm_tn("xv_dw", mn, dvb, BF16)
    dmn_k = mm_nt("xk_dx", dkb, full["w_xk"], F32)
    dmn_v = mm_nt("xv_dx", dvb, full["w_xv"], F32)
    dg_mem = gain_grad("norm_mem_bwd", dmn_k, dmn_v, mem2, rm)

    finish_cross("w_ffn2_out")
    dw["w_mix_out"], (got,) = mm_tn("mix_out_dw", ycat, dh2b, BF16, jobs=[send_final("w_ffn2_out")])
    take_final(["w_ffn2_out"], got)
    attn_names = ["w_xo", "w_xq", "w_xk", "w_xv", "w_mix_out"]
    dycat, (got,) = mm_nt("mix_out_dx", dh2b, full["w_mix_out"], BF16, jobs=[send_pair(*attn_names)])
    take_pair(attn_names, got)
    start_cross(*attn_names)
    dz, dsmall, dws, dbt = mixer_bwd("mixer_bwd", z, dycat, conv_taps, conv_b, g_gm_v, w_s, b_t)
    dw["w_mix_in"] = mm_tn("mix_in_dw", n2, dz, BF16)
    (dh1, dh1b, dg_mix), (got,) = mm_nt_norm_bwd("mix_in_dx", dz, full["w_mix_in"], h1, r2, g_mix, dh2, tk=1280,
                                                 jobs=[send_pair("w_mix_in")])
    take_pair(["w_mix_in"], got)
    start_cross("w_mix_in")

    finish_cross(*attn_names)
    dw["w_ffn1_out"], (got,) = mm_tn("ffn1_dwout", a1, dh1b, BF16, scale=0.5, jobs=[send_final(*attn_names)])
    take_final(attn_names, got)
    early = {"g_mix": dg_mix, "conv_w": dsmall[0:CONV_K], "conv_b": dsmall[3:4], "g_gm_v": dsmall[4:5],
             "w_spatial": dws, "b_spatial": dbt.T, "g_xattn": dg_xattn, "g_mem": dg_mem, "g_ffn2": dg_ffn2,
             "g_final": dg_final}
    finish_cross("w_mix_in")
    dgu1, (got_p, got_f, (early_all,)) = swiglu_bwd(
        "ffn1_dact", dh1b, full["w_ffn1_out"], gu1, 0.5,
        jobs=[send_pair("w_ffn1_out"), send_final("w_mix_in"), stack_job(_pack([early[n] for n in EARLY_SMALL]))])
    take_pair(["w_ffn1_out"], got_p)
    start_cross("w_ffn1_out")
    take_final(["w_mix_in"], got_f)
    theirs = mm_tn_pair_half("ffn1_dwin_theirs", n1, dgu1, BF16, place, False)
    mine, ((from_sibling,),) = mm_tn_pair_half("ffn1_dwin_mine", n1, dgu1, BF16, place, True,
                                               jobs=[pair_job([theirs], [1], is_half=True)])
    part["w_ffn1_in"] = pair_add("pair_add_w_ffn1_in", mine, from_sibling, 1, place)
    start_cross("w_ffn1_in")
    finish_cross("w_ffn1_out")
    dn1, (got,) = mm_nt_pair("ffn1_dn", dgu1, full["w_ffn1_in"], F32, jobs=[send_final("w_ffn1_out")])
    take_final(["w_ffn1_out"], got)
    dx, _, dg_ffn1 = rmsnorm_bwd("norm1_bwd", dn1, x2, r1, g_ffn1, dh1)
    finish_cross("w_ffn1_in")
    got, (late_all,) = comm_only("tail_final", [send_final("w_ffn1_in"), stack_job(_pack([dg_ffn1]))])
    take_final(["w_ffn1_in"], got)

    early_sum, late_sum = small_sum("small_sum", [early_all, late_all])
    for n, g in zip(EARLY_SMALL, _unpack(early_sum, [early[n].shape for n in EARLY_SMALL])):
        grads[n] = g
    grads["g_ffn1"] = _unpack(late_sum, [dg_ffn1.shape])[0]
    taps_cols = conv_w.shape[2]
    grads["conv_w"] = lax.dynamic_slice_in_dim(grads["conv_w"], blk * taps_cols, taps_cols, axis=1)
    packed = [_pack([src[n] for n in SMALL]) for src in (wts, grads, mom, var)]
    own_shapes = [wts[n].shape for n in SMALL]
    for dst, buf in zip((delta, new_m, new_v), adamw("adamw_small", *packed)[1:]):
        for n, a in zip(SMALL, _unpack(buf, own_shapes)):
            dst[n] = a

    loss = lax.psum(loss_blk[0, 0], ("x", "y", "c"))
    outs = [loss, dx[None]]
    for group in (grads, delta, new_m, new_v):
        outs += [group[n].reshape(wts[n].shape) for n in WEIGHTS]
    return tuple(outs)
```

```python
import math

import jax
import jax.numpy as jnp
from jax import lax
from jax.experimental import pallas as pl
from jax.experimental.pallas import tpu as pltpu
from jax.experimental.pallas import tpu_sc as plsc

F32 = jnp.float32
BF16 = jnp.bfloat16
EPS = 1e-6
GROUP = 128
XA_HEADS = 4
CONV_K = 3
N_CHIPS = 4
VMEM_LIMIT_BYTES = 56 * 1024 * 1024

ADAM_LR = 0.001
ADAM_B1 = 0.9
ADAM_B2 = 0.999
ADAM_EPS = 1e-08
ADAM_WD = 0.01
ADAM_STEP = 10

MESH = pl.DeviceIdType.MESH
ANY = pl.BlockSpec(memory_space=pl.ANY)


def _tile(dim, pref, mult=128):
    if dim <= pref:
        return dim
    t = (pref // mult) * mult
    while t >= mult:
        if dim % t == 0:
            return t
        t -= mult
    raise ValueError(f"no tile for {dim} under {pref}")


def _params(sem):
    return pltpu.CompilerParams(dimension_semantics=sem, vmem_limit_bytes=VMEM_LIMIT_BYTES)


def _sds(shape, dtype):
    return jax.ShapeDtypeStruct(shape, dtype)


def _dot_nn(a, b):
    return jnp.dot(a, b, preferred_element_type=F32)


def _dot_nt(a, b):
    return lax.dot_general(a, b, (((1,), (1,)), ((), ())), preferred_element_type=F32)


def _dot_tn(a, b):
    return lax.dot_general(a, b, (((0,), (0,)), ((), ())), preferred_element_type=F32)


class Job:
    def __init__(self, inputs, out_shapes, aliases, sems, start, middle, finish):
        self.inputs, self.out_shapes, self.aliases, self.sems = inputs, out_shapes, aliases, sems
        self.start, self.middle, self.finish = start, middle, finish


def _place():
    x, y, c = lax.axis_index("x"), lax.axis_index("y"), lax.axis_index("c")
    chips = [(1 - x, y), (x, 1 - y), (1 - x, 1 - y)]
    return x, y, c, chips


def _ds(start, size, lane):
    if not isinstance(start, int):
        start = pl.multiple_of(start, 128 if lane else 16)
    return pl.ds(start, size)


WHOLE = (0, 1, 1)


def _window(ref, axis, shape, blk=None, half=None, sub=WHOLE, within=WHOLE):
    n = shape[axis] // N_CHIPS
    hs = shape[1 - axis] // 2
    idx = [slice(None), slice(None)]
    if blk is not None:
        b_first, b_count, b_pieces = within
        b_ext = n // b_pieces
        idx[axis] = _ds(blk * n + b_first * b_ext, b_count * b_ext, axis == 1)
    first, count, pieces = sub
    ext = hs // pieces
    if half is not None:
        idx[1 - axis] = _ds(half * hs + first * ext, count * ext, axis == 0)
    elif pieces > 1:
        idx[1 - axis] = _ds(first * ext, count * ext, axis == 0)
    return ref.at[tuple(idx)]


def _remote(src, dst, send_sem, recv_sem, dev):
    return pltpu.make_async_remote_copy(src_ref=src, dst_ref=dst, send_sem=send_sem, recv_sem=recv_sem,
                                        device_id=dev, device_id_type=MESH)


def _full_shape(block_shape, axis):
    out = list(block_shape)
    out[axis] *= N_CHIPS
    return tuple(out)


def _half_all(shape, axis):
    out = list(shape)
    out[1 - axis] //= 2
    return tuple(out)


def _block(shape, axis):
    out = list(shape)
    out[axis] //= N_CHIPS
    return tuple(out)


def _half_block(shape, axis):
    return _half_all(_block(shape, axis), axis)


def gather_job(items):
    nw = len(items)
    shapes = [item[0].shape for item in items]
    n_sem = 8

    def parts(sub):
        first, count, pieces = sub
        return (2 * first, count, 2 * pieces), (2 * first + count, count, 2 * pieces)

    def start(pos, ins, outs, sems):
        x, y, c, chips = pos
        for w, (_, ax, sub, within) in enumerate(items):
            mine = _window(outs[w], ax, shapes[w], blk=2 * x + y, half=c, sub=sub, within=within)
            for j in range(2):
                _remote(mine, mine, sems[0].at[n_sem * w + j], sems[1].at[n_sem * w + j], (*chips[j], c)).start()

    def middle(pos, ins, outs, sems):
        x, y, c, chips = pos
        for w, (_, ax, sub, within) in enumerate(items):
            for j in range(2):
                cx, cy = chips[j]
                landed = _window(outs[w], ax, shapes[w], blk=2 * cx + cy, half=c, sub=sub, within=within)
                _remote(landed, landed, sems[0].at[n_sem * w + j], sems[1].at[n_sem * w + j], (cx, cy, c)).wait_recv()
                part = _window(outs[w], ax, shapes[w], blk=2 * cx + cy, half=c, sub=parts(sub)[j], within=within)
                _remote(part, part, sems[0].at[n_sem * w + 2 + j], sems[1].at[n_sem * w + 2 + j],
                        (*chips[1 - j], c)).start()
                _remote(landed, landed, sems[0].at[n_sem * w + 4 + j], sems[1].at[n_sem * w + 4 + j],
                        (x, y, 1 - c)).start()

    def finish(pos, ins, outs, sems):
        x, y, c, chips = pos
        sib = (x, y, 1 - c)
        for w, (_, ax, sub, within) in enumerate(items):
            dx, dy = chips[2]
            for j in range(2):
                part = _window(outs[w], ax, shapes[w], blk=2 * dx + dy, half=c, sub=parts(sub)[j], within=within)
                cp = _remote(part, part, sems[0].at[n_sem * w + 2 + j], sems[1].at[n_sem * w + 2 + j], sib)
                cp.wait_recv()
                cp.wait_send()
            diag = _window(outs[w], ax, shapes[w], blk=2 * dx + dy, half=c, sub=sub, within=within)
            _remote(diag, diag, sems[0].at[n_sem * w + 6], sems[1].at[n_sem * w + 6], sib).start()
        for w, (_, ax, sub, within) in enumerate(items):
            for j, (cx, cy) in enumerate(chips):
                passed = _window(outs[w], ax, shapes[w], blk=2 * cx + cy, half=1 - c, sub=sub, within=within)
                cp = _remote(passed, passed, sems[0].at[n_sem * w + 4 + j], sems[1].at[n_sem * w + 4 + j], sib)
                cp.wait_recv()
                cp.wait_send()
            mine = _window(outs[w], ax, shapes[w], blk=2 * x + y, half=c, sub=sub, within=within)
            for j in range(2):
                _remote(mine, mine, sems[0].at[n_sem * w + j], sems[1].at[n_sem * w + j], sib).wait_send()

    sems = [pltpu.SemaphoreType.DMA((n_sem * nw,)), pltpu.SemaphoreType.DMA((n_sem * nw,))]
    return Job([item[0] for item in items], [_sds(item[0].shape, item[0].dtype) for item in items],
               {w: w for w in range(nw)}, sems, start, middle, finish)


def gather_by_sequencer(name, fulls, axes, collective_id):
    job = gather_job([(full, axis, WHOLE, WHOLE) for full, axis in zip(fulls, axes)])
    refs = [jax.new_ref(full, memory_space=pltpu.MemorySpace.HBM) for full in fulls]

    @pl.kernel(mesh=plsc.ScalarSubcoreMesh(axis_name="sequencer", num_cores=1), name=name,
               scratch_types=tuple(job.sems), compiler_params=pltpu.CompilerParams(collective_id=collective_id))
    def launch(send_sems, recv_sems):
        pos = _place()
        x, y, c, chips = pos
        barrier = pltpu.get_barrier_semaphore()
        for dev in ((*chips[0], c), (*chips[1], c), (x, y, 1 - c)):
            pl.semaphore_signal(barrier, inc=1, device_id=dev, device_id_type=MESH)
        pl.semaphore_wait(barrier, 3)
        job.start(pos, refs, refs, [send_sems, recv_sems])
        job.middle(pos, refs, refs, [send_sems, recv_sems])
        job.finish(pos, refs, refs, [send_sems, recv_sems])

    launch()
    return [ref[...] for ref in refs]


def cross_by_sequencer(name, items, collective_id):
    job = cross_job([(part, axis, shape, None, WHOLE) for part, axis, shape in items])
    srcs = [jax.new_ref(part, memory_space=pltpu.MemorySpace.HBM) for part, _, _ in items]
    lands = [jax.empty_ref(s, memory_space=pltpu.MemorySpace.HBM) for s in job.out_shapes]

    @pl.kernel(mesh=plsc.ScalarSubcoreMesh(axis_name="sequencer", num_cores=1), name=name,
               scratch_types=tuple(job.sems), compiler_params=pltpu.CompilerParams(collective_id=collective_id))
    def launch(send_sems, recv_sems):
        pos = _place()
        x, y, c, chips = pos
        barrier = pltpu.get_barrier_semaphore()
        for cx, cy in chips:
            pl.semaphore_signal(barrier, inc=1, device_id=(cx, cy, c), device_id_type=MESH)
        pl.semaphore_wait(barrier, 3)
        job.start(pos, srcs, lands, [send_sems, recv_sems])
        job.finish(pos, srcs, lands, [send_sems, recv_sems])

    launch()
    return [r[...] for r in srcs], [r[...] for r in lands]


def pair_job(grads, axes, is_half=False):
    nw = len(grads)
    shapes = [g.shape for g in grads]

    def start(pos, ins, outs, sems):
        x, y, c, _ = pos
        for w in range(nw):
            src = ins[w] if is_half else _window(ins[w], axes[w], shapes[w], half=1 - c)
            _remote(src, outs[w], sems[0].at[w], sems[1].at[w], (x, y, 1 - c)).start()

    def finish(pos, ins, outs, sems):
        x, y, c, _ = pos
        for w in range(nw):
            cp = _remote(outs[w], outs[w], sems[0].at[w], sems[1].at[w], (x, y, 1 - c))
            cp.wait_recv()
            cp.wait_send()

    sems = [pltpu.SemaphoreType.DMA((nw,)), pltpu.SemaphoreType.DMA((nw,))]
    out_shapes = [_sds(s if is_half else _half_all(s, a), BF16) for s, a in zip(shapes, axes)]
    return Job(list(grads), out_shapes, {}, sems, start, None, finish)


def cross_job(items):
    nw = len(items)
    inputs, aliases = [], {}
    for w, (part, ax, shape, prev, sub) in enumerate(items):
        inputs.append(part)
        if prev is not None:
            aliases[len(inputs)] = w
            inputs.append(prev)

    def copies(pos, ins, outs, sems):
        x, y, c, chips = pos
        k = 0
        for w, (_, ax, shape, prev, sub) in enumerate(items):
            src = ins[k]
            k += 2 if prev is not None else 1
            for j, (cx, cy) in enumerate(chips):
                slot = _window(outs[w].at[j], ax, shape, sub=sub)
                yield (_remote(_window(src, ax, shape, blk=2 * cx + cy, sub=sub), slot,
                               sems[0].at[3 * w + j], sems[1].at[3 * w + j], (cx, cy, c)),
                       _remote(slot, slot, sems[0].at[3 * w + j], sems[1].at[3 * w + j], (cx, cy, c)))

    def start(pos, ins, outs, sems):
        for send, _ in copies(pos, ins, outs, sems):
            send.start()

    def finish(pos, ins, outs, sems):
        for send, recv in copies(pos, ins, outs, sems):
            recv.wait_recv()
            send.wait_send()

    sems = [pltpu.SemaphoreType.DMA((3 * nw,)), pltpu.SemaphoreType.DMA((3 * nw,))]
    out_shapes = [_sds((3,) + _half_block(shape, ax), BF16) for _, ax, shape, _, _ in items]
    return Job(inputs, out_shapes, aliases, sems, start, None, finish)


def final_job(blocks, axes, shapes):
    nw = len(blocks)

    def start(pos, ins, outs, sems):
        x, y, c, _ = pos
        for w in range(nw):
            mine = _window(outs[w], axes[w], shapes[w], half=c)
            _remote(mine, mine, sems[0].at[w], sems[1].at[w], (x, y, 1 - c)).start()

    def finish(pos, ins, outs, sems):
        x, y, c, _ = pos
        for w in range(nw):
            theirs = _window(outs[w], axes[w], shapes[w], half=1 - c)
            cp = _remote(theirs, theirs, sems[0].at[w], sems[1].at[w], (x, y, 1 - c))
            cp.wait_recv()
            cp.wait_send()

    sems = [pltpu.SemaphoreType.DMA((nw,)), pltpu.SemaphoreType.DMA((nw,))]
    return Job(list(blocks), [_sds(b.shape, b.dtype) for b in blocks], {w: w for w in range(nw)}, sems, start, None,
               finish)


def stack_job(small):
    def peers(pos):
        x, y, c, _ = pos
        for k in range(1, 8):
            yield k - 1, (1 - x if k & 4 else x, 1 - y if k & 2 else y, 1 - c if k & 1 else c)

    def start(pos, ins, outs, sems):
        x, y, c, _ = pos
        mine = outs[0].at[4 * x + 2 * y + c]
        pltpu.make_async_copy(ins[0], mine, sems[2]).start()
        for k, dev in peers(pos):
            _remote(ins[0], mine, sems[0].at[k], sems[1].at[k], dev).start()

    def finish(pos, ins, outs, sems):
        x, y, c, _ = pos
        for k, (px, py, pc) in peers(pos):
            slot = outs[0].at[4 * px + 2 * py + pc]
            cp = _remote(slot, slot, sems[0].at[k], sems[1].at[k], (px, py, pc))
            cp.wait_recv()
            cp.wait_send()
        pltpu.make_async_copy(ins[0], outs[0].at[4 * x + 2 * y + c], sems[2]).wait()

    sems = [pltpu.SemaphoreType.DMA((7,)), pltpu.SemaphoreType.DMA((7,)), pltpu.SemaphoreType.DMA]
    return Job([small], [_sds((8,) + small.shape, small.dtype)], {}, sems, start, None, finish)


def columns_job(block):
    cols = block.shape[1]
    place = lambda out, b: out.at[:, _ds(b * cols, cols, True)]

    def start(pos, ins, outs, sems):
        x, y, c, chips = pos
        pltpu.make_async_copy(ins[0], place(outs[0], 2 * x + y), sems[2]).start()
        for j, (cx, cy) in enumerate(chips):
            _remote(ins[0], place(outs[0], 2 * x + y), sems[0].at[j], sems[1].at[j], (cx, cy, c)).start()

    def finish(pos, ins, outs, sems):
        x, y, c, chips = pos
        for j, (cx, cy) in enumerate(chips):
            got = place(outs[0], 2 * cx + cy)
            cp = _remote(got, got, sems[0].at[j], sems[1].at[j], (cx, cy, c))
            cp.wait_recv()
            cp.wait_send()
        pltpu.make_async_copy(ins[0], place(outs[0], 2 * x + y), sems[2]).wait()

    sems = [pltpu.SemaphoreType.DMA((3,)), pltpu.SemaphoreType.DMA((3,)), pltpu.SemaphoreType.DMA]
    return Job([block], [_sds((block.shape[0], N_CHIPS * cols), block.dtype)], {}, sems, start, None, finish)


def _call(name, body, grid, in_specs, out_specs, out_shape, args, scratch=(), sem=None, jobs=(), place=None,
          carried=None):
    n_in, n_out, n_sc = len(args), len(out_shape), len(scratch)
    carried = dict(carried or {})

    def launch(fn, in_specs, out_specs, out_shape, scratch, aliases, sem, operands):
        if place is None:
            return pl.pallas_call(
                fn, name=name, grid=grid, in_specs=in_specs, out_specs=out_specs, out_shape=out_shape,
                scratch_shapes=scratch, input_output_aliases=aliases, compiler_params=_params(sem))(*operands)
        spec = pltpu.PrefetchScalarGridSpec(num_scalar_prefetch=1, grid=grid, in_specs=in_specs,
                                            out_specs=out_specs, scratch_shapes=scratch)
        return pl.pallas_call(
            lambda p_ref, *refs: fn(*refs), name=name, grid_spec=spec, out_shape=out_shape,
            input_output_aliases={k + 1: v for k, v in aliases.items()}, compiler_params=_params(sem),
        )(place, *operands)

    if not jobs:
        outs = launch(body, list(in_specs), list(out_specs), list(out_shape), list(scratch), carried, sem, args)
        return list(outs), []

    total = math.prod(grid) if grid else 1
    mid = min(total - 1, (2 * total) // 3)

    def split(refs, start, counts):
        out = []
        for n in counts:
            out.append(refs[start:start + n])
            start += n
        return out, start

    def wrapped(*refs):
        c_in = refs[:n_in]
        j_ins, p = split(refs, n_in, [len(j.inputs) for j in jobs])
        c_out = refs[p:p + n_out]
        j_outs, p = split(refs, p + n_out, [len(j.out_shapes) for j in jobs])
        c_sc = refs[p:p + n_sc]
        j_sems, p = split(refs, p + n_sc, [len(j.sems) for j in jobs])
        pos = _place()
        step = 0
        for axis, extent in enumerate(grid):
            step = step * extent + pl.program_id(axis)

        def run(phase):
            for j, ins, outs, sems in zip(jobs, j_ins, j_outs, j_sems):
                fn = getattr(j, phase)
                if fn is not None:
                    fn(pos, ins, outs, sems)

        if total == 1:
            run("start")
            body(*c_in, *c_out, *c_sc)
            run("middle")
            run("finish")
            return
        pl.when(step == 0)(lambda: run("start"))
        body(*c_in, *c_out, *c_sc)
        if any(j.middle is not None for j in jobs):
            pl.when(step == mid)(lambda: run("middle"))
        pl.when(step == total - 1)(lambda: run("finish"))

    aliases, in_at, out_at = carried, n_in, n_out
    for j in jobs:
        for src, dst in j.aliases.items():
            aliases[in_at + src] = out_at + dst
        in_at += len(j.inputs)
        out_at += len(j.out_shapes)
    outs = launch(
        wrapped, list(in_specs) + [ANY] * (in_at - n_in), list(out_specs) + [ANY] * (out_at - n_out),
        list(out_shape) + [s for j in jobs for s in j.out_shapes],
        list(scratch) + [s for j in jobs for s in j.sems], aliases, ("arbitrary",) * len(grid),
        [*args, *[a for j in jobs for a in j.inputs]])
    job_outs, p = split(outs, n_out, [len(j.out_shapes) for j in jobs])
    return list(outs[:n_out]), [list(o) for o in job_outs]


def comm_only(name, jobs):
    def body(dummy_ref, out_ref):
        out_ref[...] = dummy_ref[...]

    dummy = jnp.zeros((8, 128), F32)
    spec = pl.BlockSpec((8, 128), lambda: (0, 0))
    return _call(name, body, (), [spec], [spec], [_sds((8, 128), F32)], [dummy], jobs=jobs)[1]


def _ret(outs, job_outs, jobs, single=True):
    res = outs[0] if single else outs
    return (res, job_outs) if jobs else res


def rmsnorm_fwd(name, x, g, jobs=()):
    s, d = x.shape
    tm = _tile(s, 512, 8)

    def body(x_ref, g_ref, n_ref, r_ref):
        xv = x_ref[...]
        r = lax.rsqrt(jnp.mean(xv * xv, axis=-1, keepdims=True) + EPS)
        n_ref[...] = (xv * r * g_ref[...]).astype(BF16)
        r_ref[...] = r

    row = lambda i: (i, 0)
    outs, job_outs = _call(
        name, body, (s // tm,),
        [pl.BlockSpec((tm, d), row), pl.BlockSpec((1, d), lambda i: (0, 0))],
        [pl.BlockSpec((tm, d), row), pl.BlockSpec((tm, 1), row)],
        [_sds((s, d), BF16), _sds((s, 1), F32)], [x, g], sem=("arbitrary",), jobs=jobs)
    return _ret(outs, job_outs, jobs, single=False)


def rmsnorm_bwd(name, dn, x, r, g, dh_in, jobs=()):
    s, d = x.shape
    tm = _tile(s, 512, 8)

    def body(dn_ref, x_ref, r_ref, g_ref, dh_ref, out_ref, outb_ref, dg_ref):
        i = pl.program_id(0)
        xh = x_ref[...] * r_ref[...]
        dnv = dn_ref[...]
        dxh = dnv * g_ref[...]
        dx = r_ref[...] * (dxh - xh * jnp.mean(dxh * xh, axis=-1, keepdims=True))
        out = dh_ref[...] + dx
        out_ref[...] = out
        outb_ref[...] = out.astype(BF16)
        part = jnp.sum(dnv * xh, axis=0, keepdims=True)

        @pl.when(i == 0)
        def _():
            dg_ref[...] = part

        @pl.when(i > 0)
        def _():
            dg_ref[...] += part

    row = lambda i: (i, 0)
    fixed = lambda i: (0, 0)
    outs, job_outs = _call(
        name, body, (s // tm,),
        [pl.BlockSpec((tm, d), row), pl.BlockSpec((tm, d), row), pl.BlockSpec((tm, 1), row),
         pl.BlockSpec((1, d), fixed), pl.BlockSpec((tm, d), row)],
        [pl.BlockSpec((tm, d), row), pl.BlockSpec((tm, d), row), pl.BlockSpec((1, d), fixed)],
        [_sds((s, d), F32), _sds((s, d), BF16), _sds((1, d), F32)], [dn, x, r, g, dh_in],
        sem=("arbitrary",), jobs=jobs)
    return _ret(outs, job_outs, jobs, single=False)


def gain_grad(name, dn_a, dn_b, x, r):
    s, d = x.shape
    tm = _tile(s, 512, 8)

    def body(a_ref, b_ref, x_ref, r_ref, dg_ref):
        i = pl.program_id(0)
        part = jnp.sum((a_ref[...] + b_ref[...]) * (x_ref[...] * r_ref[...]), axis=0, keepdims=True)

        @pl.when(i == 0)
        def _():
            dg_ref[...] = part

        @pl.when(i > 0)
        def _():
            dg_ref[...] += part

    row = lambda i: (i, 0)
    return _call(
        name, body, (s // tm,),
        [pl.BlockSpec((tm, d), row), pl.BlockSpec((tm, d), row), pl.BlockSpec((tm, d), row),
         pl.BlockSpec((tm, 1), row)],
        [pl.BlockSpec((1, d), lambda i: (0, 0))], [_sds((1, d), F32)], [dn_a, dn_b, x, r],
        sem=("arbitrary",))[0][0]


def loss_head(name, h, g, target):
    s, d = h.shape
    tm = _tile(s, 512, 8)
    nsteps = s // tm

    def body(h_ref, g_ref, t_ref, loss_ref, dh_ref, dhb_ref, dg_ref, sq_ref):
        i = pl.program_id(0)
        hv = h_ref[...]
        gv = g_ref[...]
        r = lax.rsqrt(jnp.mean(hv * hv, axis=-1, keepdims=True) + EPS)
        xh = hv * r
        err = xh * gv - t_ref[...]
        dy = err * (1.0 / d)
        dxh = dy * gv
        dh = r * (dxh - xh * jnp.mean(dxh * xh, axis=-1, keepdims=True))
        dh_ref[...] = dh
        dhb_ref[...] = dh.astype(BF16)
        dg_part = jnp.sum(dy * xh, axis=0, keepdims=True)
        sq_part = jnp.sum(err * err, axis=0, keepdims=True)

        @pl.when(i == 0)
        def _():
            dg_ref[...] = dg_part
            sq_ref[...] = sq_part

        @pl.when(i > 0)
        def _():
            dg_ref[...] += dg_part
            sq_ref[...] += sq_part

        @pl.when(i == nsteps - 1)
        def _():
            total = jnp.sum(sq_ref[...], axis=-1, keepdims=True) * (0.5 / d)
            loss_ref[...] = jnp.broadcast_to(total, loss_ref.shape)

    row = lambda i: (i, 0)
    fixed = lambda i: (0, 0)
    return _call(
        name, body, (nsteps,),
        [pl.BlockSpec((tm, d), row), pl.BlockSpec((1, d), fixed), pl.BlockSpec((tm, d), row)],
        [pl.BlockSpec((8, 128), fixed), pl.BlockSpec((tm, d), row), pl.BlockSpec((tm, d), row),
         pl.BlockSpec((1, d), fixed)],
        [_sds((8, 128), F32), _sds((s, d), F32), _sds((s, d), BF16), _sds((1, d), F32)], [h, g, target],
        scratch=[pltpu.VMEM((1, d), F32)], sem=("arbitrary",))[0]


def _mm(name, grid, in_arrays, in_specs, out_shapes, out_specs, acc_tile, dot, epilogue, jobs=(), place=None):
    nk = grid[2]
    n_in = len(in_arrays)
    n_out = len(out_shapes)

    def body(*refs):
        ins, outs = refs[:n_in], refs[n_in:n_in + n_out]
        if nk == 1:
            epilogue(dot(*ins), ins, outs)
            return
        acc = refs[n_in + n_out]
        k = pl.program_id(2)

        @pl.when(k == 0)
        def _():
            acc[...] = dot(*ins)

        @pl.when(jnp.logical_and(k > 0, k < nk - 1))
        def _():
            acc[...] += dot(*ins)

        @pl.when(k == nk - 1)
        def _():
            epilogue(acc[...] + dot(*ins), ins, outs)

    scratch = [pltpu.VMEM(acc_tile, F32)] if nk > 1 else []
    outs, job_outs = _call(name, body, grid, in_specs, out_specs, out_shapes, in_arrays, scratch=scratch,
                           sem=("parallel", "parallel", "arbitrary"), jobs=jobs, place=place)
    return _ret(outs, job_outs, jobs)


def _store(scale, dtype):
    def epilogue(acc, ins, outs):
        outs[0][...] = (acc * scale if scale != 1.0 else acc).astype(dtype)
    return epilogue


def mm_nn(name, a, w, out_dtype, tm=1024, tn=1024, tk=2048, jobs=()):
    m, kd = a.shape
    n = w.shape[1]
    tm, tn, tk = _tile(m, tm, 8), _tile(n, tn), _tile(kd, tk)
    return _mm(
        name, (n // tn, m // tm, kd // tk), [a, w],
        [pl.BlockSpec((tm, tk), lambda j, i, k: (i, k)), pl.BlockSpec((tk, tn), lambda j, i, k: (k, j))],
        [_sds((m, n), out_dtype)], [pl.BlockSpec((tm, tn), lambda j, i, k: (i, j))], (tm, tn),
        lambda a_ref, w_ref: _dot_nn(a_ref[...], w_ref[...]), _store(1.0, out_dtype), jobs)


def mm_nn_resid(name, a, w, x, scale, tm=1024, tn=1024, tk=1408, jobs=()):
    m, kd = a.shape
    n = w.shape[1]
    tm, tn, tk = _tile(m, tm, 8), _tile(n, tn), _tile(kd, tk)

    def epilogue(acc, ins, outs):
        outs[0][...] = ins[2][...] + scale * acc

    return _mm(
        name, (n // tn, m // tm, kd // tk), [a, w, x],
        [pl.BlockSpec((tm, tk), lambda j, i, k: (i, k)), pl.BlockSpec((tk, tn), lambda j, i, k: (k, j)),
         pl.BlockSpec((tm, tn), lambda j, i, k: (i, j))],
        [_sds((m, n), F32)], [pl.BlockSpec((tm, tn), lambda j, i, k: (i, j))], (tm, tn),
        lambda a_ref, w_ref, x_ref: _dot_nn(a_ref[...], w_ref[...]), epilogue, jobs)


def mm_nt(name, a, w, out_dtype, scale=1.0, tm=1024, tn=1024, tk=2048, jobs=()):
    m, kd = a.shape
    n = w.shape[0]
    tm, tn, tk = _tile(m, tm, 8), _tile(n, tn), _tile(kd, tk)
    return _mm(
        name, (n // tn, m // tm, kd // tk), [a, w],
        [pl.BlockSpec((tm, tk), lambda j, i, k: (i, k)), pl.BlockSpec((tn, tk), lambda j, i, k: (j, k))],
        [_sds((m, n), out_dtype)], [pl.BlockSpec((tm, tn), lambda j, i, k: (i, j))], (tm, tn),
        lambda a_ref, w_ref: _dot_nt(a_ref[...], w_ref[...]), _store(scale, out_dtype), jobs)


def mm_nt_pair(name, a3, w, out_dtype, tm=1024, tn=1024, tk=2816, jobs=()):
    _, m, f = a3.shape
    n = w.shape[0]
    tm, tn, tk = _tile(m, tm, 8), _tile(n, tn), _tile(f, tk)
    nkf = f // tk
    return _mm(
        name, (n // tn, m // tm, 2 * nkf), [a3, w],
        [pl.BlockSpec((None, tm, tk), lambda j, i, k: (k // nkf, i, k % nkf)),
         pl.BlockSpec((tn, tk), lambda j, i, k: (j, k))],
        [_sds((m, n), out_dtype)], [pl.BlockSpec((tm, tn), lambda j, i, k: (i, j))], (tm, tn),
        lambda a_ref, w_ref: _dot_nt(a_ref[...], w_ref[...]), _store(1.0, out_dtype), jobs)


def mm_nt_norm_bwd(name, a, w, x, r, g, dh_in, tm=512, tk=1408, jobs=()):
    pair = a.ndim == 3
    m, kd = a.shape[-2], a.shape[-1]
    d = w.shape[0]
    tm, tk = _tile(m, tm, 8), _tile(kd, tk)
    nkf = kd // tk
    nk = 2 * nkf if pair else nkf
    if pair:
        a_spec = pl.BlockSpec((None, tm, tk), lambda i, k: (k // nkf, i, k % nkf))
    else:
        a_spec = pl.BlockSpec((tm, tk), lambda i, k: (i, k))
    row = lambda i, k: (i, 0)
    fixed = lambda i, k: (0, 0)

    def body(a_ref, w_ref, x_ref, r_ref, g_ref, dh_ref, out_ref, outb_ref, dg_ref, *acc):
        i, k = pl.program_id(0), pl.program_id(1)
        dot = lambda: _dot_nt(a_ref[...], w_ref[...])

        def finish(dn):
            xh = x_ref[...] * r_ref[...]
            dxh = dn * g_ref[...]
            out = dh_ref[...] + r_ref[...] * (dxh - xh * jnp.mean(dxh * xh, axis=-1, keepdims=True))
            out_ref[...] = out
            outb_ref[...] = out.astype(BF16)
            part = jnp.sum(dn * xh, axis=0, keepdims=True)

            @pl.when(i == 0)
            def _():
                dg_ref[...] = part

            @pl.when(i > 0)
            def _():
                dg_ref[...] += part

        if nk == 1:
            finish(dot())
            return

        @pl.when(k == 0)
        def _():
            acc[0][...] = dot()

        @pl.when(jnp.logical_and(k > 0, k < nk - 1))
        def _():
            acc[0][...] += dot()

        @pl.when(k == nk - 1)
        def _():
            finish(acc[0][...] + dot())

    outs, job_outs = _call(
        name, body, (m // tm, nk),
        [a_spec, pl.BlockSpec((d, tk), lambda i, k: (0, k)), pl.BlockSpec((tm, d), row), pl.BlockSpec((tm, 1), row),
         pl.BlockSpec((1, d), fixed), pl.BlockSpec((tm, d), row)],
        [pl.BlockSpec((tm, d), row), pl.BlockSpec((tm, d), row), pl.BlockSpec((1, d), fixed)],
        [_sds((m, d), F32), _sds((m, d), BF16), _sds((1, d), F32)], [a, w, x, r, g, dh_in],
        scratch=[pltpu.VMEM((tm, d), F32)] if nk > 1 else [], sem=("arbitrary", "arbitrary"), jobs=jobs)
    return _ret(outs, job_outs, jobs, single=False)


def mm_tn(name, a, b, out_dtype, scale=1.0, tm=1024, tn=1024, tk=4096, jobs=()):
    kd, m = a.shape
    n = b.shape[1]
    tm, tn, tk = _tile(m, tm), _tile(n, tn), _tile(kd, tk, 16)
    return _mm(
        name, (n // tn, m // tm, kd // tk), [a, b],
        [pl.BlockSpec((tk, tm), lambda j, i, k: (k, i)), pl.BlockSpec((tk, tn), lambda j, i, k: (k, j))],
        [_sds((m, n), out_dtype)], [pl.BlockSpec((tm, tn), lambda j, i, k: (i, j))], (tm, tn),
        lambda a_ref, b_ref: _dot_tn(a_ref[...], b_ref[...]), _store(scale, out_dtype), jobs)


def mm_tn_pair(name, a, b3, out_dtype, tm=1024, tn=512, tk=4096, jobs=()):
    kd, m = a.shape
    f = b3.shape[2]
    tm, tn, tk = _tile(m, tm), _tile(f, tn), _tile(kd, tk, 16)
    nf = f // tn
    return _mm(
        name, (m // tm, 2 * nf, kd // tk), [a, b3],
        [pl.BlockSpec((tk, tm), lambda i, j, k: (k, i)),
         pl.BlockSpec((None, tk, tn), lambda i, j, k: (j // nf, k, j % nf))],
        [_sds((m, 2 * f), out_dtype)], [pl.BlockSpec((tm, tn), lambda i, j, k: (i, j))], (tm, tn),
        lambda a_ref, b_ref: _dot_tn(a_ref[...], b_ref[...]), _store(1.0, out_dtype), jobs)


def mm_tn_pair_half(name, a, b3, out_dtype, place, mine, tm=1024, tn=512, tk=4096, jobs=()):
    kd, m = a.shape
    f = b3.shape[2]
    tm, tn, tk = _tile(m // 2, tm), _tile(f, tn), _tile(kd, tk, 16)
    nf, nbm = f // tn, m // 2 // tm
    which = (lambda p: p[1]) if mine else (lambda p: 1 - p[1])
    return _mm(
        name, (nbm, 2 * nf, kd // tk), [a, b3],
        [pl.BlockSpec((tk, tm), lambda i, j, k, p: (k, i + which(p) * nbm)),
         pl.BlockSpec((None, tk, tn), lambda i, j, k, p: (j // nf, k, j % nf))],
        [_sds((m // 2, 2 * f), out_dtype)], [pl.BlockSpec((tm, tn), lambda i, j, k, p: (i, j))], (tm, tn),
        lambda a_ref, b_ref: _dot_tn(a_ref[...], b_ref[...]), _store(1.0, out_dtype), jobs, place)


def swiglu_fwd(name, n, w_in, tm=1024, tn=512, jobs=(), stride=1, phase=0, prev=None, compact=False):
    s, d = n.shape
    f = w_in.shape[1] // 2 * (stride if compact else 1)
    tm, tn = _tile(s, tm, 8), _tile(f, tn)
    nf = f // tn
    col = lambda j: j * stride + phase
    w_gate = (lambda j: j) if compact else col
    w_up = (lambda j: j + nf // stride) if compact else (lambda j: col(j) + nf)

    def body(n_ref, wg_ref, wu_ref, *rest):
        gu_ref, a_ref = rest[-2:]
        nv = n_ref[...]
        g = _dot_nn(nv, wg_ref[...])
        u = _dot_nn(nv, wu_ref[...])
        gu_ref[0] = g.astype(BF16)
        gu_ref[1] = u.astype(BF16)
        a_ref[...] = (g * jax.nn.sigmoid(g) * u).astype(BF16)

    kept = list(prev) if prev is not None else []
    outs, job_outs = _call(
        name, body, (nf // stride, s // tm),
        [pl.BlockSpec((tm, d), lambda j, i: (i, 0)), pl.BlockSpec((d, tn), lambda j, i: (0, w_gate(j))),
         pl.BlockSpec((d, tn), lambda j, i: (0, w_up(j)))] + [ANY] * len(kept),
        [pl.BlockSpec((2, tm, tn), lambda j, i: (0, i, col(j))), pl.BlockSpec((tm, tn), lambda j, i: (i, col(j)))],
        [_sds((2, s, f), BF16), _sds((s, f), BF16)], [n, w_in, w_in] + kept, sem=("parallel", "parallel"),
        jobs=jobs, carried={3 + k: k for k in range(len(kept))})
    return _ret(outs, job_outs, jobs, single=False)


def swiglu_bwd(name, dh, w_out, gu, scale, tm=1024, tn=512, jobs=()):
    s, d = dh.shape
    f = w_out.shape[0]
    tm, tn = _tile(s, tm, 8), _tile(f, tn)

    sub = _tile(tm, 256, 8)

    def body(dh_ref, w_ref, gu_ref, out_ref):
        for lo in range(0, tm, sub):
            rows = slice(lo, lo + sub)
            da = (_dot_nt(dh_ref[rows, :], w_ref[...]) * scale).astype(BF16)
            g = gu_ref[0, rows, :]
            u = gu_ref[1, rows, :]
            sg = 0.5 * jnp.tanh(0.5 * g) + 0.5
            t = g * sg
            out_ref[0, rows, :] = da * (u * (sg + t * (1.0 - sg)))
            out_ref[1, rows, :] = da * t

    outs, job_outs = _call(
        name, body, (f // tn, s // tm),
        [pl.BlockSpec((tm, d), lambda j, i: (i, 0)), pl.BlockSpec((tn, d), lambda j, i: (j, 0)),
         pl.BlockSpec((2, tm, tn), lambda j, i: (0, i, j))],
        [pl.BlockSpec((2, tm, tn), lambda j, i: (0, i, j))],
        [_sds((2, s, f), BF16)], [dh, w_out, gu], sem=("parallel", "parallel"), jobs=jobs)
    return _ret(outs, job_outs, jobs)


HALO = 16


def _conv_inputs(z_ref, hgc_ref, hhc_ref, i, cw, tm):
    gc = z_ref[:, cw:2 * cw].astype(F32)
    hc = z_ref[:, 2 * cw:3 * cw].astype(F32)
    cin = gc * hc
    halo = hgc_ref[...].astype(F32) * hhc_ref[...].astype(F32) * (i > 0).astype(F32)
    row = lax.broadcasted_iota(jnp.int32, (tm, cw), 0)
    x1 = jnp.where(row == 0, halo[HALO - 1:HALO], pltpu.roll(cin, 1, 0))
    x2 = jnp.where(row == 0, halo[HALO - 2:HALO - 1], jnp.where(row == 1, halo[HALO - 1:HALO], pltpu.roll(cin, 2, 0)))
    return gc, hc, cin, x1, x2


def _tril(w):
    r = lax.broadcasted_iota(jnp.int32, w.shape, 0)
    c = lax.broadcasted_iota(jnp.int32, w.shape, 1)
    return jnp.where(r >= c, w, jnp.zeros_like(w))


def mixer_fwd(name, z, conv_w, conv_b, g_v, w_s, b_t, tm=256, jobs=()):
    s, zc = z.shape
    cw = conv_w.shape[1]
    gw = g_v.shape[1]
    heads = gw // GROUP
    tm = _tile(s, tm)
    hb = tm // HALO

    def body(z_ref, hgc_ref, hhc_ref, cw_ref, cb_ref, gv_ref, ws_ref, bt_ref, y_ref):
        i = pl.program_id(0)
        _, _, cin, x1, x2 = _conv_inputs(z_ref, hgc_ref, hhc_ref, i, cw, tm)
        cv = cb_ref[...] + cw_ref[2:3, :] * cin + cw_ref[1:2, :] * x1 + cw_ref[0:1, :] * x2
        y_ref[:, 0:cw] = (z_ref[:, 0:cw].astype(F32) * cv).astype(BF16)
        for h in range(heads):
            lo = h * GROUP
            vh = z_ref[:, 3 * cw + gw + lo:3 * cw + gw + lo + GROUP].astype(F32)
            rv = lax.rsqrt(jnp.mean(vh * vh, axis=-1, keepdims=True) + EPS)
            vn = (vh * rv * gv_ref[:, lo:lo + GROUP]).astype(BF16)
            w = _tril(ws_ref[h]).astype(BF16)
            for n in range(tm // GROUP):
                rows = slice(n * GROUP, (n + 1) * GROUP)
                sg = _dot_nn(w, vn[rows]) + bt_ref[:, h:h + 1]
                u = z_ref[rows, 3 * cw + lo:3 * cw + lo + GROUP].astype(F32)
                y_ref[rows, cw + lo:cw + lo + GROUP] = (u * sg).astype(BF16)

    fixed2 = lambda i: (0, 0)
    outs, job_outs = _call(
        name, body, (s // tm,),
        [pl.BlockSpec((tm, zc), lambda i: (i, 0)),
         pl.BlockSpec((HALO, cw), lambda i: (jnp.maximum(i * hb - 1, 0), 1)),
         pl.BlockSpec((HALO, cw), lambda i: (jnp.maximum(i * hb - 1, 0), 2)),
         pl.BlockSpec(conv_w.shape, fixed2), pl.BlockSpec(conv_b.shape, fixed2),
         pl.BlockSpec(g_v.shape, fixed2), pl.BlockSpec(w_s.shape, lambda i: (0, 0, 0)),
         pl.BlockSpec(b_t.shape, fixed2)],
        [pl.BlockSpec((tm, cw + gw), lambda i: (i, 0))], [_sds((s, cw + gw), BF16)],
        [z, z, z, conv_w, conv_b, g_v, w_s, b_t], sem=("arbitrary",), jobs=jobs)
    return _ret(outs, job_outs, jobs)


def mixer_bwd(name, z, dy, conv_w, conv_b, g_v, w_s, b_t, tm=256, jobs=()):
    s, zc = z.shape
    cw = conv_w.shape[1]
    gw = g_v.shape[1]
    heads = gw // GROUP
    tm = _tile(s, tm)
    hb = tm // HALO
    nsteps = s // tm
    last_halo = s // HALO - 1

    def body(z_ref, hgc_ref, hhc_ref, ngb_ref, dy_ref, ndy_ref, cw_ref, cb_ref, gv_ref, ws_ref, bt_ref,
             dz_ref, sm_ref, dws_ref, dbt_ref, dsg_ref):
        i = pl.program_id(0)

        @pl.when(i == 0)
        def _():
            sm_ref[...] = jnp.zeros_like(sm_ref)
            dws_ref[...] = jnp.zeros_like(dws_ref)
            dsg_ref[...] = jnp.zeros_like(dsg_ref)

        gc, hc, cin, x1, x2 = _conv_inputs(z_ref, hgc_ref, hhc_ref, i, cw, tm)
        w0, w1, w2 = cw_ref[0:1, :], cw_ref[1:2, :], cw_ref[2:3, :]
        cv = cb_ref[...] + w2 * cin + w1 * x1 + w0 * x2
        gb = z_ref[:, 0:cw].astype(F32)
        dyc = dy_ref[:, 0:cw].astype(F32)
        dz_ref[:, 0:cw] = (dyc * cv).astype(BF16)
        dcv = dyc * gb
        nxt = ndy_ref[...].astype(F32) * ngb_ref[...].astype(F32) * (i < nsteps - 1).astype(F32)
        row = lax.broadcasted_iota(jnp.int32, (tm, cw), 0)
        d1 = jnp.where(row == tm - 1, nxt[0:1], pltpu.roll(dcv, tm - 1, 0))
        d2 = jnp.where(row == tm - 1, nxt[1:2], jnp.where(row == tm - 2, nxt[0:1], pltpu.roll(dcv, tm - 2, 0)))
        dcin = w2 * dcv + w1 * d1 + w0 * d2
        dz_ref[:, cw:2 * cw] = (dcin * hc).astype(BF16)
        dz_ref[:, 2 * cw:3 * cw] = (dcin * gc).astype(BF16)
        sm_ref[0:1, :] += jnp.sum(dcv * x2, axis=0, keepdims=True)
        sm_ref[1:2, :] += jnp.sum(dcv * x1, axis=0, keepdims=True)
        sm_ref[2:3, :] += jnp.sum(dcv * cin, axis=0, keepdims=True)
        sm_ref[3:4, :] += jnp.sum(dcv, axis=0, keepdims=True)

        for h in range(heads):
            lo = h * GROUP
            vcol = slice(3 * cw + gw + lo, 3 * cw + gw + lo + GROUP)
            ucol = slice(3 * cw + lo, 3 * cw + lo + GROUP)
            vh = z_ref[:, vcol].astype(F32)
            rv = lax.rsqrt(jnp.mean(vh * vh, axis=-1, keepdims=True) + EPS)
            xh = vh * rv
            gvh = gv_ref[:, lo:lo + GROUP]
            vn = (xh * gvh).astype(BF16)
            w = _tril(ws_ref[h]).astype(BF16)
            dgv = jnp.zeros((1, GROUP), F32)
            for n in range(tm // GROUP):
                rows = slice(n * GROUP, (n + 1) * GROUP)
                sg = _dot_nn(w, vn[rows]) + bt_ref[:, h:h + 1]
                dyg = dy_ref[rows, cw + lo:cw + lo + GROUP].astype(F32)
                dsg = dyg * z_ref[rows, ucol].astype(F32)
                dz_ref[rows, ucol] = (dyg * sg).astype(BF16)
                dsgb = dsg.astype(BF16)
                dvn = _dot_tn(w, dsgb)
                dws_ref[h] += _dot_nt(dsgb, vn[rows])
                dsg_ref[:, lo:lo + GROUP] += dsg
                xhc = xh[rows]
                dgv = dgv + jnp.sum(dvn * xhc, axis=0, keepdims=True)
                dxh = dvn * gvh
                dv = rv[rows] * (dxh - xhc * jnp.mean(dxh * xhc, axis=-1, keepdims=True))
                dz_ref[rows, vcol] = dv.astype(BF16)
            sm_ref[4:5, lo:lo + GROUP] += dgv

        @pl.when(i == nsteps - 1)
        def _():
            for h in range(heads):
                dws_ref[h] = _tril(dws_ref[h])
                dbt_ref[:, h:h + 1] = jnp.sum(dsg_ref[:, h * GROUP:(h + 1) * GROUP], axis=-1, keepdims=True)

    fixed2 = lambda i: (0, 0)
    fixed3 = lambda i: (0, 0, 0)
    prev = lambda col: (lambda i: (jnp.maximum(i * hb - 1, 0), col))
    nxt_blk = lambda i: (jnp.minimum((i + 1) * hb, last_halo), 0)
    outs, job_outs = _call(
        name, body, (nsteps,),
        [pl.BlockSpec((tm, zc), lambda i: (i, 0)),
         pl.BlockSpec((HALO, cw), prev(1)), pl.BlockSpec((HALO, cw), prev(2)),
         pl.BlockSpec((HALO, cw), nxt_blk),
         pl.BlockSpec((tm, cw + gw), lambda i: (i, 0)), pl.BlockSpec((HALO, cw), nxt_blk),
         pl.BlockSpec(conv_w.shape, fixed2), pl.BlockSpec(conv_b.shape, fixed2),
         pl.BlockSpec(g_v.shape, fixed2), pl.BlockSpec(w_s.shape, fixed3), pl.BlockSpec(b_t.shape, fixed2)],
        [pl.BlockSpec((tm, zc), lambda i: (i, 0)), pl.BlockSpec((8, cw), fixed2),
         pl.BlockSpec(w_s.shape, fixed3), pl.BlockSpec(b_t.shape, fixed2)],
        [_sds((s, zc), BF16), _sds((8, cw), F32), _sds(w_s.shape, F32), _sds(b_t.shape, F32)],
        [z, z, z, z, dy, dy, conv_w, conv_b, g_v, w_s, b_t],
        scratch=[pltpu.VMEM((GROUP, gw), F32)], sem=("arbitrary",), jobs=jobs)
    return _ret(outs, job_outs, jobs, single=False)


def _softmax_rows(sc):
    e = jnp.exp(sc - jnp.max(sc, axis=-1, keepdims=True))
    return e / jnp.sum(e, axis=-1, keepdims=True)


def attn_fwd(name, q, k, v, tm=512, jobs=()):
    s, d = q.shape
    m = k.shape[0]
    hd = d // XA_HEADS
    scale = hd ** -0.5
    tm = _tile(s, tm, 8)

    def body(q_ref, k_ref, v_ref, o_ref):
        for h in range(XA_HEADS):
            cols = slice(h * hd, (h + 1) * hd)
            p = _softmax_rows(_dot_nt(q_ref[:, cols], k_ref[:, cols]) * scale)
            o_ref[:, cols] = _dot_nn(p.astype(BF16), v_ref[:, cols]).astype(BF16)

    outs, job_outs = _call(
        name, body, (s // tm,),
        [pl.BlockSpec((tm, d), lambda i: (i, 0)), pl.BlockSpec((m, d), lambda i: (0, 0)),
         pl.BlockSpec((m, d), lambda i: (0, 0))],
        [pl.BlockSpec((tm, d), lambda i: (i, 0))], [_sds((s, d), BF16)], [q, k, v], sem=("arbitrary",), jobs=jobs)
    return _ret(outs, job_outs, jobs)


def attn_bwd(name, q, k, v, do, tm=512):
    s, d = q.shape
    m = k.shape[0]
    hd = d // XA_HEADS
    scale = hd ** -0.5
    tm = _tile(s, tm, 8)

    def body(q_ref, k_ref, v_ref, do_ref, dq_ref, dk_ref, dv_ref):
        i = pl.program_id(0)

        @pl.when(i == 0)
        def _():
            dk_ref[...] = jnp.zeros_like(dk_ref)
            dv_ref[...] = jnp.zeros_like(dv_ref)

        for h in range(XA_HEADS):
            cols = slice(h * hd, (h + 1) * hd)
            qh = q_ref[:, cols]
            doh = do_ref[:, cols]
            p = _softmax_rows(_dot_nt(qh, k_ref[:, cols]) * scale)
            dp = _dot_nt(doh, v_ref[:, cols])
            ds = (p * (dp - jnp.sum(dp * p, axis=-1, keepdims=True)) * scale).astype(BF16)
            dq_ref[:, cols] = _dot_nn(ds, k_ref[:, cols]).astype(BF16)
            dk_ref[:, cols] += _dot_tn(ds, qh)
            dv_ref[:, cols] += _dot_tn(p.astype(BF16), doh)

    row = lambda i: (i, 0)
    fixed = lambda i: (0, 0)
    return _call(
        name, body, (s // tm,),
        [pl.BlockSpec((tm, d), row), pl.BlockSpec((m, d), fixed), pl.BlockSpec((m, d), fixed),
         pl.BlockSpec((tm, d), row)],
        [pl.BlockSpec((tm, d), row), pl.BlockSpec((m, d), fixed), pl.BlockSpec((m, d), fixed)],
        [_sds((s, d), BF16), _sds((m, d), F32), _sds((m, d), F32)], [q, k, v, do], sem=("arbitrary",))[0]


def _grid2(rows, cols, row_mult):
    tr, tc = _tile(rows, 512, row_mult), _tile(cols, 2048)
    return tr, tc, rows // tr, cols // tc


def cast_place(name, block, axis, place, column_half=None):
    r, c = block.shape
    if column_half is not None:
        c //= 2
    tr, tc, nbr, nbc = _grid2(r, c, 16)
    first = 0 if column_half is None else column_half * nbc
    if axis == 1:
        dst = lambda i, j, p: (i, j + p[0] * nbc)
    else:
        dst = lambda i, j, p: (i + p[0] * nbr, j)

    def body(p_ref, w_ref, out_ref):
        out_ref[...] = w_ref[...].astype(BF16)

    return pl.pallas_call(
        body, name=name,
        grid_spec=pltpu.PrefetchScalarGridSpec(
            num_scalar_prefetch=1, grid=(nbr, nbc),
            in_specs=[pl.BlockSpec((tr, tc), lambda i, j, p: (i, j + first))],
            out_specs=pl.BlockSpec((tr, tc), dst)),
        out_shape=_sds(_full_shape((r, c), axis), BF16),
        compiler_params=_params(("parallel", "parallel")),
    )(place, block)


def merge_column_halves(name, left, right):
    r, c = left.shape
    w = c // N_CHIPS
    tr = _tile(r, 512, 16)

    def body(l_ref, r_ref, out_ref):
        side = pl.program_id(1) % 2

        @pl.when(side == 0)
        def _():
            out_ref[...] = l_ref[...]

        @pl.when(side == 1)
        def _():
            out_ref[...] = r_ref[...]

    half = pl.BlockSpec((tr, w), lambda i, j: (i, j // 2))
    return _call(name, body, (r // tr, 2 * N_CHIPS), [half, half], [pl.BlockSpec((tr, w), lambda i, j: (i, j))],
                 [_sds((r, 2 * c), left.dtype)], [left, right], sem=("parallel", "arbitrary"))[0][0]


def pair_add(name, grad, peer, axis, place):
    hr, hc = peer.shape
    tr, tc, nbr, nbc = _grid2(hr, hc, 16)
    same = lambda i, j, p: (i, j)
    if grad.shape == peer.shape:
        mine = same
    elif axis == 1:
        mine = lambda i, j, p: (i + p[1] * nbr, j)
    else:
        mine = lambda i, j, p: (i, j + p[1] * nbc)

    def body(p_ref, g_ref, q_ref, out_ref):
        out_ref[...] = (g_ref[...].astype(F32) + q_ref[...].astype(F32)).astype(BF16)

    return pl.pallas_call(
        body, name=name,
        grid_spec=pltpu.PrefetchScalarGridSpec(
            num_scalar_prefetch=1, grid=(nbr, nbc),
            in_specs=[pl.BlockSpec((tr, tc), mine), pl.BlockSpec((tr, tc), same)],
            out_specs=pl.BlockSpec((tr, tc), same)),
        out_shape=_sds((hr, hc), BF16),
        compiler_params=_params(("parallel", "parallel")),
    )(place, grad, peer)


def cross_sum(name, part, land, axis, shape, place):
    _, sr, sc = land.shape
    tr, tc, nbr, nbc = _grid2(sr, sc, 16)
    if axis == 1:
        own = lambda i, j, p: (i, j + p[0] * nbc)
        dst = lambda i, j, p: (i + p[1] * nbr, j)
    else:
        own = lambda i, j, p: (i + p[0] * nbr, j)
        dst = lambda i, j, p: (i, j + p[1] * nbc)

    def body(p_ref, own_ref, land_ref, out_ref):
        out_ref[...] = ((own_ref[...].astype(F32) + land_ref[0].astype(F32))
                        + (land_ref[1].astype(F32) + land_ref[2].astype(F32)))

    return pl.pallas_call(
        body, name=name,
        grid_spec=pltpu.PrefetchScalarGridSpec(
            num_scalar_prefetch=1, grid=(nbr, nbc),
            in_specs=[pl.BlockSpec((tr, tc), own), pl.BlockSpec((3, tr, tc), lambda i, j, p: (0, i, j))],
            out_specs=pl.BlockSpec((tr, tc), dst)),
        out_shape=_sds(_block(shape, axis), F32),
        compiler_params=_params(("parallel", "parallel")),
    )(place, part, land)


def _adam_math(w, g, m, v):
    m = ADAM_B1 * m + (1.0 - ADAM_B1) * g
    v = ADAM_B2 * v + (1.0 - ADAM_B2) * (g * g)
    m_hat = m / (1.0 - ADAM_B1 ** ADAM_STEP)
    v_hat = v / (1.0 - ADAM_B2 ** ADAM_STEP)
    delta = -ADAM_LR * (m_hat / (jnp.sqrt(v_hat) + ADAM_EPS) + ADAM_WD * w)
    return delta, m, v


def adamw(name, w, g, m, v, jobs=()):
    r, c = w.shape
    tr, tc = _tile(r, 256, 8), _tile(c, 1408)

    def body(w_ref, g_ref, m_ref, v_ref, g_out, d_out, m_out, v_out):
        d, mm, vv = _adam_math(w_ref[...], g_ref[...], m_ref[...], v_ref[...])
        g_out[...] = g_ref[...]
        d_out[...] = d
        m_out[...] = mm
        v_out[...] = vv

    spec = pl.BlockSpec((tr, tc), lambda i, j: (i, j))
    outs, job_outs = _call(name, body, (r // tr, c // tc), [spec] * 4, [spec] * 4, [_sds((r, c), F32)] * 4,
                           [w, g, m, v], sem=("parallel", "parallel"), jobs=jobs)
    return _ret(outs, job_outs, jobs, single=False)


def small_sum(name, stacks):
    def body(*refs):
        for s_ref, out_ref in zip(refs[:len(stacks)], refs[len(stacks):]):
            acc = s_ref[0]
            for d in range(1, s_ref.shape[0]):
                acc = acc + s_ref[d]
            out_ref[...] = acc

    return pl.pallas_call(body, name=name, out_shape=[_sds(s.shape[1:], F32) for s in stacks])(*stacks)


WEIGHTS = ["g_ffn1", "w_ffn1_in", "w_ffn1_out", "g_mix", "w_mix_in", "conv_w", "conv_b", "g_gm_v", "w_spatial",
           "b_spatial", "w_mix_out", "g_xattn", "g_mem", "w_xq", "w_xk", "w_xv", "w_xo", "g_ffn2", "w_ffn2_in",
           "w_ffn2_out", "g_final"]
BIG = {"w_ffn1_in": 1, "w_ffn1_out": 0, "w_mix_in": 1, "w_mix_out": 0, "w_xq": 0, "w_xk": 0, "w_xv": 0, "w_xo": 0,
       "w_ffn2_in": 1, "w_ffn2_out": 0}
SMALL = [n for n in WEIGHTS if n not in BIG]
LATE_SMALL = ["g_ffn1"]
EARLY_SMALL = [n for n in SMALL if n not in LATE_SMALL]


def _pack(arrays):
    flat = jnp.concatenate([a.reshape(-1) for a in arrays])
    rows = -(-flat.shape[0] // 1024) * 8
    return jnp.pad(flat, (0, rows * 128 - flat.shape[0])).reshape(rows, 128)


def _unpack(buf, shapes):
    flat = buf.reshape(-1)
    out, pos = [], 0
    for shp in shapes:
        n = math.prod(shp)
        out.append(flat[pos:pos + n].reshape(shp))
        pos += n
    return out


def kernel(x, mem, g_ffn1, w_ffn1_in, w_ffn1_out, g_mix, w_mix_in, conv_w, conv_b, g_gm_v, w_spatial, b_spatial, w_mix_out, g_xattn, g_mem, w_xq, w_xk, w_xv, w_xo, g_ffn2, w_ffn2_in, w_ffn2_out, g_final, loss_target, m_g_ffn1, m_w_ffn1_in, m_w_ffn1_out, m_g_mix, m_w_mix_in, m_conv_w, m_conv_b, m_g_gm_v, m_w_spatial, m_b_spatial, m_w_mix_out, m_g_xattn, m_g_mem, m_w_xq, m_w_xk, m_w_xv, m_w_xo, m_g_ffn2, m_w_ffn2_in, m_w_ffn2_out, m_g_final, v_g_ffn1, v_w_ffn1_in, v_w_ffn1_out, v_g_mix, v_w_mix_in, v_conv_w, v_conv_b, v_g_gm_v, v_w_spatial, v_b_spatial, v_w_mix_out, v_g_xattn, v_g_mem, v_w_xq, v_w_xk, v_w_xv, v_w_xo, v_g_ffn2, v_w_ffn2_in, v_w_ffn2_out, v_g_final):
    given = dict(locals())
    wts = {n: given[n] for n in WEIGHTS}
    mom = {n: given["m_" + n] for n in WEIGHTS}
    var = {n: given["v_" + n] for n in WEIGHTS}

    xi, yi, ci = lax.axis_index("x"), lax.axis_index("y"), lax.axis_index("c")
    blk = 2 * xi + yi
    place = jnp.stack([blk, ci]).astype(jnp.int32)

    x2, mem2, tgt = x[0], mem[0], loss_target[0]
    w_s, b_t = w_spatial[0], b_spatial[0].T
    gf = g_final[None]

    rest = [n for n in BIG if n != "w_ffn1_in"]
    own = {n: cast_place("cast_" + n, wts[n][0], BIG[n], place) for n in rest}
    own_left = cast_place("cast_w_ffn1_in_left", wts["w_ffn1_in"][0], 1, place, column_half=0)
    own_right = cast_place("cast_w_ffn1_in_right", wts["w_ffn1_in"][0], 1, place, column_half=1)
    shape = {n: own[n].shape for n in rest}
    shape["w_ffn1_in"] = _full_shape(wts["w_ffn1_in"][0].shape, 1)
    full = {}
    (w1_left,) = gather_by_sequencer("gather_w_ffn1_in_left", [own_left], [1], 1)
    (w1_right,) = gather_by_sequencer("gather_w_ffn1_in_right", [own_right], [1], 2)
    groups = [["w_ffn1_out"], ["w_mix_in", "w_mix_out"], ["w_xq", "w_xk", "w_xv", "w_xo"], ["w_ffn2_in"],
              ["w_ffn2_out"]]
    for g, names in enumerate(groups):
        got = gather_by_sequencer("gather_" + "_".join(names), [own[n] for n in names], [BIG[n] for n in names],
                                  3 + g)
        full.update(zip(names, got))
    ((conv_taps,),) = comm_only("gather_conv_taps", [columns_job(jnp.pad(conv_w[0], ((0, 8 - CONV_K), (0, 0))))])

    half_cols = dict(tm=512, tn=shape["w_ffn1_in"][1] // (2 * N_CHIPS), stride=2, compact=True)
    n1, r1 = rmsnorm_fwd("norm1", x2, g_ffn1)
    halves = swiglu_fwd("ffn1_in_left", n1, w1_left, phase=0, **half_cols)
    gu1, a1 = swiglu_fwd("ffn1_in_right", n1, w1_right, phase=1, prev=halves, **half_cols)
    h1 = mm_nn_resid("ffn1_out", a1, full["w_ffn1_out"], x2, 0.5, tm=512, tk=5632)
    n2, r2 = rmsnorm_fwd("norm2", h1, g_mix)
    z = mm_nn("mix_in", n2, full["w_mix_in"], BF16)
    ycat = mixer_fwd("mixer", z, conv_taps, conv_b, g_gm_v, w_s, b_t)
    h2 = mm_nn_resid("mix_out", ycat, full["w_mix_out"], h1, 1.0, tk=2048)
    n3, r3 = rmsnorm_fwd("norm3", h2, g_xattn)
    mem2, h2 = lax.optimization_barrier((mem2, h2))
    mn, rm = rmsnorm_fwd("norm_mem", mem2, g_mem)
    q = mm_nn("xq", n3, full["w_xq"], BF16)
    k = mm_nn("xk", mn, full["w_xk"], BF16)
    v = mm_nn("xv", mn, full["w_xv"], BF16)
    o = attn_fwd("attn", q, k, v)
    h3 = mm_nn_resid("xo", o, full["w_xo"], h2, 1.0, tk=2048)
    n4, r4 = rmsnorm_fwd("norm4", h3, g_ffn2)
    gu2, a2 = swiglu_fwd("ffn2_in", n4, full["w_ffn2_in"])
    h4 = mm_nn_resid("ffn2_out", a2, full["w_ffn2_out"], h3, 0.5, tm=512, tk=5632)
    loss_blk, dh4, dh4b, dg_final = loss_head("loss_head", h4, gf, tgt)
    full["w_ffn1_in"] = merge_column_halves("merge_w_ffn1_in", w1_left, w1_right)

    dw, part, land, half, grads = {}, {}, {}, {}, {}
    launched = []

    def send_pair(*names):
        return pair_job([dw[n] for n in names], [BIG[n] for n in names])

    def take_pair(names, got):
        for n, p in zip(names, got):
            part[n] = pair_add("pair_add_" + n, dw[n], p, BIG[n], place)

    def start_cross(*names):
        kept, landed = cross_by_sequencer("cross_" + "_".join(names), [(part[n], BIG[n], shape[n]) for n in names],
                                          8 + len(launched) % 2)
        launched.append(names)
        for n, p, l in zip(names, kept, landed):
            part[n], land[n] = p, l

    def finish_cross(*names):
        for n in names:
            half[n] = cross_sum("cross_sum_" + n, part[n], land[n], BIG[n], shape[n], place)

    def send_final(*names):
        return final_job([half[n] for n in names], [BIG[n] for n in names], [shape[n] for n in names])

    delta, new_m, new_v = {}, {}, {}

    def take_final(names, got):
        for n, g in zip(names, got):
            grads[n], delta[n], new_m[n], new_v[n] = adamw("adamw_" + n, wts[n][0], g, mom[n][0], var[n][0])

    dgu2 = swiglu_bwd("ffn2_dact", dh4b, full["w_ffn2_out"], gu2, 0.5)
    dw["w_ffn2_in"] = mm_tn_pair("ffn2_dwin", n4, dgu2, BF16)
    dw["w_ffn2_out"], (got,) = mm_tn("ffn2_dwout", a2, dh4b, BF16, scale=0.5, jobs=[send_pair("w_ffn2_in")])
    take_pair(["w_ffn2_in"], got)
    start_cross("w_ffn2_in")
    (dh3, dh3b, dg_ffn2), (got,) = mm_nt_norm_bwd("ffn2_dn", dgu2, full["w_ffn2_in"], h3, r4, g_ffn2, dh4,
                                                  jobs=[send_pair("w_ffn2_out")])
    take_pair(["w_ffn2_out"], got)
    start_cross("w_ffn2_out")

    dw["w_xo"] = mm_tn("xo_dw", o, dh3b, BF16)
    finish_cross("w_ffn2_in")
    do, (got,) = mm_nt("xo_dx", dh3b, full["w_xo"], BF16, jobs=[send_final("w_ffn2_in")])
    take_final(["w_ffn2_in"], got)
    dq, dk, dv = attn_bwd("attn_bwd", q, k, v, do)
    dkb, dvb = dk.astype(BF16), dv.astype(BF16)
    dw["w_xq"] = mm_tn("xq_dw", n3, dq, BF16)
    dh2, dh2b, dg_xattn = mm_nt_norm_bwd("xq_dx", dq, full["w_xq"], h2, r3, g_xattn, dh3, tk=1024)
    dw["w_xk"] = mm_tn("xk_dw", mn, dkb, BF16)
    dw["w_xv"] = mm_tn("xv_dw", mn, dvb, BF16)
    dmn_k = mm_nt("xk_dx", dkb, full["w_xk"], F32)
    dmn_v = mm_nt("xv_dx", dvb, full["w_xv"], F32)
    dg_mem = gain_grad("norm_mem_bwd", dmn_k, dmn_v, mem2, rm)

    finish_cross("w_ffn2_out")
    dw["w_mix_out"], (got,) = mm_tn("mix_out_dw", ycat, dh2b, BF16, jobs=[send_final("w_ffn2_out")])
    take_final(["w_ffn2_out"], got)
    attn_names = ["w_xo", "w_xq", "w_xk", "w_xv", "w_mix_out"]
    dycat, (got,) = mm_nt("mix_out_dx", dh2b, full["w_mix_out"], BF16, jobs=[send_pair(*attn_names)])
    take_pair(attn_names, got)
    start_cross(*attn_names)
    dz, dsmall, dws, dbt = mixer_bwd("mixer_bwd", z, dycat, conv_taps, conv_b, g_gm_v, w_s, b_t)
    dw["w_mix_in"] = mm_tn("mix_in_dw", n2, dz, BF16)
    (dh1, dh1b, dg_mix), (got,) = mm_nt_norm_bwd("mix_in_dx", dz, full["w_mix_in"], h1, r2, g_mix, dh2, tk=1280,
                                                 jobs=[send_pair("w_mix_in")])
    take_pair(["w_mix_in"], got)
    start_cross("w_mix_in")

    finish_cross(*attn_names)
    dw["w_ffn1_out"], (got,) = mm_tn("ffn1_dwout", a1, dh1b, BF16, scale=0.5, jobs=[send_final(*attn_names)])
    take_final(attn_names, got)
    early = {"g_mix": dg_mix, "conv_w": dsmall[0:CONV_K], "conv_b": dsmall[3:4], "g_gm_v": dsmall[4:5],
             "w_spatial": dws, "b_spatial": dbt.T, "g_xattn": dg_xattn, "g_mem": dg_mem, "g_ffn2": dg_ffn2,
             "g_final": dg_final}
    finish_cross("w_mix_in")
    dgu1, (got_p, got_f, (early_all,)) = swiglu_bwd(
        "ffn1_dact", dh1b, full["w_ffn1_out"], gu1, 0.5,
        jobs=[send_pair("w_ffn1_out"), send_final("w_mix_in"), stack_job(_pack([early[n] for n in EARLY_SMALL]))])
    take_pair(["w_ffn1_out"], got_p)
    start_cross("w_ffn1_out")
    take_final(["w_mix_in"], got_f)
    theirs = mm_tn_pair_half("ffn1_dwin_theirs", n1, dgu1, BF16, place, False)
    mine, ((from_sibling,),) = mm_tn_pair_half("ffn1_dwin_mine", n1, dgu1, BF16, place, True,
                                               jobs=[pair_job([theirs], [1], is_half=True)])
    part["w_ffn1_in"] = pair_add("pair_add_w_ffn1_in", mine, from_sibling, 1, place)
    start_cross("w_ffn1_in")
    finish_cross("w_ffn1_out")
    dn1, (got,) = mm_nt_pair("ffn1_dn", dgu1, full["w_ffn1_in"], F32, jobs=[send_final("w_ffn1_out")])
    take_final(["w_ffn1_out"], got)
    dx, _, dg_ffn1 = rmsnorm_bwd("norm1_bwd", dn1, x2, r1, g_ffn1, dh1)
    finish_cross("w_ffn1_in")
    got, (late_all,) = comm_only("tail_final", [send_final("w_ffn1_in"), stack_job(_pack([dg_ffn1]))])
    take_final(["w_ffn1_in"], got)

    early_sum, late_sum = small_sum("small_sum", [early_all, late_all])
    for n, g in zip(EARLY_SMALL, _unpack(early_sum, [early[n].shape for n in EARLY_SMALL])):
        grads[n] = g
    grads["g_ffn1"] = _unpack(late_sum, [dg_ffn1.shape])[0]
    taps_cols = conv_w.shape[2]
    grads["conv_w"] = lax.dynamic_slice_in_dim(grads["conv_w"], blk * taps_cols, taps_cols, axis=1)
    packed = [_pack([src[n] for n in SMALL]) for src in (wts, grads, mom, var)]
    own_shapes = [wts[n].shape for n in SMALL]
    for dst, buf in zip((delta, new_m, new_v), adamw("adamw_small", *packed)[1:]):
        for n, a in zip(SMALL, _unpack(buf, own_shapes)):
            dst[n] = a

    loss = lax.psum(loss_blk[0, 0], ("x", "y", "c"))
    outs = [loss, dx[None]]
    for group in (grads, delta, new_m, new_v):
        outs += [group[n].reshape(wts[n].shape) for n in WEIGHTS]
    return tuple(outs)
```

```python
import math

import jax
import jax.numpy as jnp
from jax import lax
from jax.experimental import pallas as pl
from jax.experimental.pallas import tpu as pltpu
from jax.experimental.pallas import tpu_sc as plsc

F32 = jnp.float32
BF16 = jnp.bfloat16
EPS = 1e-6
GROUP = 128
XA_HEADS = 4
CONV_K = 3
N_CHIPS = 4
VMEM_LIMIT_BYTES = 56 * 1024 * 1024

ADAM_LR = 0.001
ADAM_B1 = 0.9
ADAM_B2 = 0.999
ADAM_EPS = 1e-08
ADAM_WD = 0.01
ADAM_STEP = 10

MESH = pl.DeviceIdType.MESH
ANY = pl.BlockSpec(memory_space=pl.ANY)


def _tile(dim, pref, mult=128):
    if dim <= pref:
        return dim
    t = (pref // mult) * mult
    while t >= mult:
        if dim % t == 0:
            return t
        t -= mult
    raise ValueError(f"no tile for {dim} under {pref}")


def _params(sem):
    return pltpu.CompilerParams(dimension_semantics=sem, vmem_limit_bytes=VMEM_LIMIT_BYTES)


def _sds(shape, dtype):
    return jax.ShapeDtypeStruct(shape, dtype)


def _dot_nn(a, b):
    return jnp.dot(a, b, preferred_element_type=F32)


def _dot_nt(a, b):
    return lax.dot_general(a, b, (((1,), (1,)), ((), ())), preferred_element_type=F32)


def _dot_tn(a, b):
    return lax.dot_general(a, b, (((0,), (0,)), ((), ())), preferred_element_type=F32)


class Job:
    def __init__(self, inputs, out_shapes, aliases, sems, start, middle, finish):
        self.inputs, self.out_shapes, self.aliases, self.sems = inputs, out_shapes, aliases, sems
        self.start, self.middle, self.finish = start, middle, finish


def _place():
    x, y, c = lax.axis_index("x"), lax.axis_index("y"), lax.axis_index("c")
    chips = [(1 - x, y), (x, 1 - y), (1 - x, 1 - y)]
    return x, y, c, chips


def _ds(start, size, lane):
    if not isinstance(start, int):
        start = pl.multiple_of(start, 128 if lane else 16)
    return pl.ds(start, size)


WHOLE = (0, 1, 1)


def _window(ref, axis, shape, blk=None, half=None, sub=WHOLE, within=WHOLE):
    n = shape[axis] // N_CHIPS
    hs = shape[1 - axis] // 2
    idx = [slice(None), slice(None)]
    if blk is not None:
        b_first, b_count, b_pieces = within
        b_ext = n // b_pieces
        idx[axis] = _ds(blk * n + b_first * b_ext, b_count * b_ext, axis == 1)
    first, count, pieces = sub
    ext = hs // pieces
    if half is not None:
        idx[1 - axis] = _ds(half * hs + first * ext, count * ext, axis == 0)
    elif pieces > 1:
        idx[1 - axis] = _ds(first * ext, count * ext, axis == 0)
    return ref.at[tuple(idx)]


def _remote(src, dst, send_sem, recv_sem, dev):
    return pltpu.make_async_remote_copy(src_ref=src, dst_ref=dst, send_sem=send_sem, recv_sem=recv_sem,
                                        device_id=dev, device_id_type=MESH)


def _full_shape(block_shape, axis):
    out = list(block_shape)
    out[axis] *= N_CHIPS
    return tuple(out)


def _half_all(shape, axis):
    out = list(shape)
    out[1 - axis] //= 2
    return tuple(out)


def _block(shape, axis):
    out = list(shape)
    out[axis] //= N_CHIPS
    return tuple(out)


def _half_block(shape, axis):
    return _half_all(_block(shape, axis), axis)


def gather_job(items):
    nw = len(items)
    shapes = [item[0].shape for item in items]
    n_sem = 8

    def parts(sub):
        first, count, pieces = sub
        return (2 * first, count, 2 * pieces), (2 * first + count, count, 2 * pieces)

    def start(pos, ins, outs, sems):
        x, y, c, chips = pos
        for w, (_, ax, sub, within) in enumerate(items):
            mine = _window(outs[w], ax, shapes[w], blk=2 * x + y, half=c, sub=sub, within=within)
            for j in range(2):
                _remote(mine, mine, sems[0].at[n_sem * w + j], sems[1].at[n_sem * w + j], (*chips[j], c)).start()

    def middle(pos, ins, outs, sems):
        x, y, c, chips = pos
        for w, (_, ax, sub, within) in enumerate(items):
            for j in range(2):
                cx, cy = chips[j]
                landed = _window(outs[w], ax, shapes[w], blk=2 * cx + cy, half=c, sub=sub, within=within)
                _remote(landed, landed, sems[0].at[n_sem * w + j], sems[1].at[n_sem * w + j], (cx, cy, c)).wait_recv()
                part = _window(outs[w], ax, shapes[w], blk=2 * cx + cy, half=c, sub=parts(sub)[j], within=within)
                _remote(part, part, sems[0].at[n_sem * w + 2 + j], sems[1].at[n_sem * w + 2 + j],
                        (*chips[1 - j], c)).start()
                _remote(landed, landed, sems[0].at[n_sem * w + 4 + j], sems[1].at[n_sem * w + 4 + j],
                        (x, y, 1 - c)).start()

    def finish(pos, ins, outs, sems):
        x, y, c, chips = pos
        sib = (x, y, 1 - c)
        for w, (_, ax, sub, within) in enumerate(items):
            dx, dy = chips[2]
            for j in range(2):
                part = _window(outs[w], ax, shapes[w], blk=2 * dx + dy, half=c, sub=parts(sub)[j], within=within)
                cp = _remote(part, part, sems[0].at[n_sem * w + 2 + j], sems[1].at[n_sem * w + 2 + j], sib)
                cp.wait_recv()
                cp.wait_send()
            diag = _window(outs[w], ax, shapes[w], blk=2 * dx + dy, half=c, sub=sub, within=within)
            _remote(diag, diag, sems[0].at[n_sem * w + 6], sems[1].at[n_sem * w + 6], sib).start()
        for w, (_, ax, sub, within) in enumerate(items):
            for j, (cx, cy) in enumerate(chips):
                passed = _window(outs[w], ax, shapes[w], blk=2 * cx + cy, half=1 - c, sub=sub, within=within)
                cp = _remote(passed, passed, sems[0].at[n_sem * w + 4 + j], sems[1].at[n_sem * w + 4 + j], sib)
                cp.wait_recv()
                cp.wait_send()
            mine = _window(outs[w], ax, shapes[w], blk=2 * x + y, half=c, sub=sub, within=within)
            for j in range(2):
                _remote(mine, mine, sems[0].at[n_sem * w + j], sems[1].at[n_sem * w + j], sib).wait_send()

    sems = [pltpu.SemaphoreType.DMA((n_sem * nw,)), pltpu.SemaphoreType.DMA((n_sem * nw,))]
    return Job([item[0] for item in items], [_sds(item[0].shape, item[0].dtype) for item in items],
               {w: w for w in range(nw)}, sems, start, middle, finish)


def by_sequencer(name, job, peers, collective_id):
    ins = [jax.new_ref(a, memory_space=pltpu.MemorySpace.HBM) for a in job.inputs]
    from_input = {o: i for i, o in job.aliases.items()}
    outs = [ins[from_input[k]] if k in from_input else jax.empty_ref(s, memory_space=pltpu.MemorySpace.HBM)
            for k, s in enumerate(job.out_shapes)]

    @pl.kernel(mesh=plsc.ScalarSubcoreMesh(axis_name="sequencer", num_cores=1), name=name,
               scratch_types=tuple(job.sems), compiler_params=pltpu.CompilerParams(collective_id=collective_id))
    def launch(*sems):
        pos = _place()
        x, y, c, chips = pos
        devs = {"sibling": [(x, y, 1 - c)],
                "chips": [(cx, cy, c) for cx, cy in chips],
                "gather": [(*chips[0], c), (*chips[1], c), (x, y, 1 - c)],
                "all": [(px, py, pc) for px in (x, 1 - x) for py in (y, 1 - y) for pc in (c, 1 - c)][1:]}[peers]
        barrier = pltpu.get_barrier_semaphore()
        for dev in devs:
            pl.semaphore_signal(barrier, inc=1, device_id=dev, device_id_type=MESH)
        pl.semaphore_wait(barrier, len(devs))
        for phase in (job.start, job.middle, job.finish):
            if phase is not None:
                phase(pos, ins, outs, list(sems))

    launch()
    kept = [r[...] for r in ins]
    return kept, [kept[from_input[k]] if k in from_input else r[...] for k, r in enumerate(outs)]


def pair_job(grads, axes, is_half=False):
    nw = len(grads)
    shapes = [g.shape for g in grads]

    def start(pos, ins, outs, sems):
        x, y, c, _ = pos
        for w in range(nw):
            src = ins[w] if is_half else _window(ins[w], axes[w], shapes[w], half=1 - c)
            _remote(src, outs[w], sems[0].at[w], sems[1].at[w], (x, y, 1 - c)).start()

    def finish(pos, ins, outs, sems):
        x, y, c, _ = pos
        for w in range(nw):
            cp = _remote(outs[w], outs[w], sems[0].at[w], sems[1].at[w], (x, y, 1 - c))
            cp.wait_recv()
            cp.wait_send()

    sems = [pltpu.SemaphoreType.DMA((nw,)), pltpu.SemaphoreType.DMA((nw,))]
    out_shapes = [_sds(s if is_half else _half_all(s, a), BF16) for s, a in zip(shapes, axes)]
    return Job(list(grads), out_shapes, {}, sems, start, None, finish)


def cross_job(items):
    nw = len(items)
    inputs, aliases = [], {}
    for w, (part, ax, shape, prev, sub) in enumerate(items):
        inputs.append(part)
        if prev is not None:
            aliases[len(inputs)] = w
            inputs.append(prev)

    def copies(pos, ins, outs, sems):
        x, y, c, chips = pos
        k = 0
        for w, (_, ax, shape, prev, sub) in enumerate(items):
            src = ins[k]
            k += 2 if prev is not None else 1
            for j, (cx, cy) in enumerate(chips):
                slot = _window(outs[w].at[j], ax, shape, sub=sub)
                yield (_remote(_window(src, ax, shape, blk=2 * cx + cy, sub=sub), slot,
                               sems[0].at[3 * w + j], sems[1].at[3 * w + j], (cx, cy, c)),
                       _remote(slot, slot, sems[0].at[3 * w + j], sems[1].at[3 * w + j], (cx, cy, c)))

    def start(pos, ins, outs, sems):
        for send, _ in copies(pos, ins, outs, sems):
            send.start()

    def finish(pos, ins, outs, sems):
        for send, recv in copies(pos, ins, outs, sems):
            recv.wait_recv()
            send.wait_send()

    sems = [pltpu.SemaphoreType.DMA((3 * nw,)), pltpu.SemaphoreType.DMA((3 * nw,))]
    out_shapes = [_sds((3,) + _half_block(shape, ax), BF16) for _, ax, shape, _, _ in items]
    return Job(inputs, out_shapes, aliases, sems, start, None, finish)


def final_job(blocks, axes, shapes):
    nw = len(blocks)

    def start(pos, ins, outs, sems):
        x, y, c, _ = pos
        for w in range(nw):
            mine = _window(outs[w], axes[w], shapes[w], half=c)
            _remote(mine, mine, sems[0].at[w], sems[1].at[w], (x, y, 1 - c)).start()

    def finish(pos, ins, outs, sems):
        x, y, c, _ = pos
        for w in range(nw):
            theirs = _window(outs[w], axes[w], shapes[w], half=1 - c)
            cp = _remote(theirs, theirs, sems[0].at[w], sems[1].at[w], (x, y, 1 - c))
            cp.wait_recv()
            cp.wait_send()

    sems = [pltpu.SemaphoreType.DMA((nw,)), pltpu.SemaphoreType.DMA((nw,))]
    return Job(list(blocks), [_sds(b.shape, b.dtype) for b in blocks], {w: w for w in range(nw)}, sems, start, None,
               finish)


def stack_job(small):
    def peers(pos):
        x, y, c, _ = pos
        for k in range(1, 8):
            yield k - 1, (1 - x if k & 4 else x, 1 - y if k & 2 else y, 1 - c if k & 1 else c)

    def start(pos, ins, outs, sems):
        x, y, c, _ = pos
        mine = outs[0].at[4 * x + 2 * y + c]
        pltpu.make_async_copy(ins[0], mine, sems[2]).start()
        for k, dev in peers(pos):
            _remote(ins[0], mine, sems[0].at[k], sems[1].at[k], dev).start()

    def finish(pos, ins, outs, sems):
        x, y, c, _ = pos
        for k, (px, py, pc) in peers(pos):
            slot = outs[0].at[4 * px + 2 * py + pc]
            cp = _remote(slot, slot, sems[0].at[k], sems[1].at[k], (px, py, pc))
            cp.wait_recv()
            cp.wait_send()
        pltpu.make_async_copy(ins[0], outs[0].at[4 * x + 2 * y + c], sems[2]).wait()

    sems = [pltpu.SemaphoreType.DMA((7,)), pltpu.SemaphoreType.DMA((7,)), pltpu.SemaphoreType.DMA]
    return Job([small], [_sds((8,) + small.shape, small.dtype)], {}, sems, start, None, finish)


def columns_job(block):
    cols = block.shape[1]
    place = lambda out, b: out.at[:, _ds(b * cols, cols, True)]

    def start(pos, ins, outs, sems):
        x, y, c, chips = pos
        pltpu.make_async_copy(ins[0], place(outs[0], 2 * x + y), sems[2]).start()
        for j, (cx, cy) in enumerate(chips):
            _remote(ins[0], place(outs[0], 2 * x + y), sems[0].at[j], sems[1].at[j], (cx, cy, c)).start()

    def finish(pos, ins, outs, sems):
        x, y, c, chips = pos
        for j, (cx, cy) in enumerate(chips):
            got = place(outs[0], 2 * cx + cy)
            cp = _remote(got, got, sems[0].at[j], sems[1].at[j], (cx, cy, c))
            cp.wait_recv()
            cp.wait_send()
        pltpu.make_async_copy(ins[0], place(outs[0], 2 * x + y), sems[2]).wait()

    sems = [pltpu.SemaphoreType.DMA((3,)), pltpu.SemaphoreType.DMA((3,)), pltpu.SemaphoreType.DMA]
    return Job([block], [_sds((block.shape[0], N_CHIPS * cols), block.dtype)], {}, sems, start, None, finish)


def _call(name, body, grid, in_specs, out_specs, out_shape, args, scratch=(), sem=None, jobs=(), place=None,
          carried=None):
    n_in, n_out, n_sc = len(args), len(out_shape), len(scratch)
    carried = dict(carried or {})

    def launch(fn, in_specs, out_specs, out_shape, scratch, aliases, sem, operands):
        if place is None:
            return pl.pallas_call(
                fn, name=name, grid=grid, in_specs=in_specs, out_specs=out_specs, out_shape=out_shape,
                scratch_shapes=scratch, input_output_aliases=aliases, compiler_params=_params(sem))(*operands)
        spec = pltpu.PrefetchScalarGridSpec(num_scalar_prefetch=1, grid=grid, in_specs=in_specs,
                                            out_specs=out_specs, scratch_shapes=scratch)
        return pl.pallas_call(
            lambda p_ref, *refs: fn(*refs), name=name, grid_spec=spec, out_shape=out_shape,
            input_output_aliases={k + 1: v for k, v in aliases.items()}, compiler_params=_params(sem),
        )(place, *operands)

    if not jobs:
        outs = launch(body, list(in_specs), list(out_specs), list(out_shape), list(scratch), carried, sem, args)
        return list(outs), []

    total = math.prod(grid) if grid else 1
    mid = min(total - 1, (2 * total) // 3)

    def split(refs, start, counts):
        out = []
        for n in counts:
            out.append(refs[start:start + n])
            start += n
        return out, start

    def wrapped(*refs):
        c_in = refs[:n_in]
        j_ins, p = split(refs, n_in, [len(j.inputs) for j in jobs])
        c_out = refs[p:p + n_out]
        j_outs, p = split(refs, p + n_out, [len(j.out_shapes) for j in jobs])
        c_sc = refs[p:p + n_sc]
        j_sems, p = split(refs, p + n_sc, [len(j.sems) for j in jobs])
        pos = _place()
        step = 0
        for axis, extent in enumerate(grid):
            step = step * extent + pl.program_id(axis)

        def run(phase):
            for j, ins, outs, sems in zip(jobs, j_ins, j_outs, j_sems):
                fn = getattr(j, phase)
                if fn is not None:
                    fn(pos, ins, outs, sems)

        if total == 1:
            run("start")
            body(*c_in, *c_out, *c_sc)
            run("middle")
            run("finish")
            return
        pl.when(step == 0)(lambda: run("start"))
        body(*c_in, *c_out, *c_sc)
        if any(j.middle is not None for j in jobs):
            pl.when(step == mid)(lambda: run("middle"))
        pl.when(step == total - 1)(lambda: run("finish"))

    aliases, in_at, out_at = carried, n_in, n_out
    for j in jobs:
        for src, dst in j.aliases.items():
            aliases[in_at + src] = out_at + dst
        in_at += len(j.inputs)
        out_at += len(j.out_shapes)
    outs = launch(
        wrapped, list(in_specs) + [ANY] * (in_at - n_in), list(out_specs) + [ANY] * (out_at - n_out),
        list(out_shape) + [s for j in jobs for s in j.out_shapes],
        list(scratch) + [s for j in jobs for s in j.sems], aliases, ("arbitrary",) * len(grid),
        [*args, *[a for j in jobs for a in j.inputs]])
    job_outs, p = split(outs, n_out, [len(j.out_shapes) for j in jobs])
    return list(outs[:n_out]), [list(o) for o in job_outs]


def comm_only(name, jobs):
    def body(dummy_ref, out_ref):
        out_ref[...] = dummy_ref[...]

    dummy = jnp.zeros((8, 128), F32)
    spec = pl.BlockSpec((8, 128), lambda: (0, 0))
    return _call(name, body, (), [spec], [spec], [_sds((8, 128), F32)], [dummy], jobs=jobs)[1]


def _ret(outs, job_outs, jobs, single=True):
    res = outs[0] if single else outs
    return (res, job_outs) if jobs else res


def rmsnorm_fwd(name, x, g, jobs=()):
    s, d = x.shape
    tm = _tile(s, 512, 8)

    def body(x_ref, g_ref, n_ref, r_ref):
        xv = x_ref[...]
        r = lax.rsqrt(jnp.mean(xv * xv, axis=-1, keepdims=True) + EPS)
        n_ref[...] = (xv * r * g_ref[...]).astype(BF16)
        r_ref[...] = r

    row = lambda i: (i, 0)
    outs, job_outs = _call(
        name, body, (s // tm,),
        [pl.BlockSpec((tm, d), row), pl.BlockSpec((1, d), lambda i: (0, 0))],
        [pl.BlockSpec((tm, d), row), pl.BlockSpec((tm, 1), row)],
        [_sds((s, d), BF16), _sds((s, 1), F32)], [x, g], sem=("arbitrary",), jobs=jobs)
    return _ret(outs, job_outs, jobs, single=False)


def rmsnorm_bwd(name, dn, x, r, g, dh_in, jobs=()):
    s, d = x.shape
    tm = _tile(s, 512, 8)

    def body(dn_ref, x_ref, r_ref, g_ref, dh_ref, out_ref, outb_ref, dg_ref):
        i = pl.program_id(0)
        xh = x_ref[...] * r_ref[...]
        dnv = dn_ref[...]
        dxh = dnv * g_ref[...]
        dx = r_ref[...] * (dxh - xh * jnp.mean(dxh * xh, axis=-1, keepdims=True))
        out = dh_ref[...] + dx
        out_ref[...] = out
        outb_ref[...] = out.astype(BF16)
        part = jnp.sum(dnv * xh, axis=0, keepdims=True)

        @pl.when(i == 0)
        def _():
            dg_ref[...] = part

        @pl.when(i > 0)
        def _():
            dg_ref[...] += part

    row = lambda i: (i, 0)
    fixed = lambda i: (0, 0)
    outs, job_outs = _call(
        name, body, (s // tm,),
        [pl.BlockSpec((tm, d), row), pl.BlockSpec((tm, d), row), pl.BlockSpec((tm, 1), row),
         pl.BlockSpec((1, d), fixed), pl.BlockSpec((tm, d), row)],
        [pl.BlockSpec((tm, d), row), pl.BlockSpec((tm, d), row), pl.BlockSpec((1, d), fixed)],
        [_sds((s, d), F32), _sds((s, d), BF16), _sds((1, d), F32)], [dn, x, r, g, dh_in],
        sem=("arbitrary",), jobs=jobs)
    return _ret(outs, job_outs, jobs, single=False)


def gain_grad(name, dn_a, dn_b, x, r):
    s, d = x.shape
    tm = _tile(s, 512, 8)

    def body(a_ref, b_ref, x_ref, r_ref, dg_ref):
        i = pl.program_id(0)
        part = jnp.sum((a_ref[...] + b_ref[...]) * (x_ref[...] * r_ref[...]), axis=0, keepdims=True)

        @pl.when(i == 0)
        def _():
            dg_ref[...] = part

        @pl.when(i > 0)
        def _():
            dg_ref[...] += part

    row = lambda i: (i, 0)
    return _call(
        name, body, (s // tm,),
        [pl.BlockSpec((tm, d), row), pl.BlockSpec((tm, d), row), pl.BlockSpec((tm, d), row),
         pl.BlockSpec((tm, 1), row)],
        [pl.BlockSpec((1, d), lambda i: (0, 0))], [_sds((1, d), F32)], [dn_a, dn_b, x, r],
        sem=("arbitrary",))[0][0]


def loss_head(name, h, g, target):
    s, d = h.shape
    tm = _tile(s, 512, 8)
    nsteps = s // tm

    def body(h_ref, g_ref, t_ref, loss_ref, dh_ref, dhb_ref, dg_ref, sq_ref):
        i = pl.program_id(0)
        hv = h_ref[...]
        gv = g_ref[...]
        r = lax.rsqrt(jnp.mean(hv * hv, axis=-1, keepdims=True) + EPS)
        xh = hv * r
        err = xh * gv - t_ref[...]
        dy = err * (1.0 / d)
        dxh = dy * gv
        dh = r * (dxh - xh * jnp.mean(dxh * xh, axis=-1, keepdims=True))
        dh_ref[...] = dh
        dhb_ref[...] = dh.astype(BF16)
        dg_part = jnp.sum(dy * xh, axis=0, keepdims=True)
        sq_part = jnp.sum(err * err, axis=0, keepdims=True)

        @pl.when(i == 0)
        def _():
            dg_ref[...] = dg_part
            sq_ref[...] = sq_part

        @pl.when(i > 0)
        def _():
            dg_ref[...] += dg_part
            sq_ref[...] += sq_part

        @pl.when(i == nsteps - 1)
        def _():
            total = jnp.sum(sq_ref[...], axis=-1, keepdims=True) * (0.5 / d)
            loss_ref[...] = jnp.broadcast_to(total, loss_ref.shape)

    row = lambda i: (i, 0)
    fixed = lambda i: (0, 0)
    return _call(
        name, body, (nsteps,),
        [pl.BlockSpec((tm, d), row), pl.BlockSpec((1, d), fixed), pl.BlockSpec((tm, d), row)],
        [pl.BlockSpec((8, 128), fixed), pl.BlockSpec((tm, d), row), pl.BlockSpec((tm, d), row),
         pl.BlockSpec((1, d), fixed)],
        [_sds((8, 128), F32), _sds((s, d), F32), _sds((s, d), BF16), _sds((1, d), F32)], [h, g, target],
        scratch=[pltpu.VMEM((1, d), F32)], sem=("arbitrary",))[0]


def _mm(name, grid, in_arrays, in_specs, out_shapes, out_specs, acc_tile, dot, epilogue, jobs=(), place=None):
    nk = grid[2]
    n_in = len(in_arrays)
    n_out = len(out_shapes)

    def body(*refs):
        ins, outs = refs[:n_in], refs[n_in:n_in + n_out]
        if nk == 1:
            epilogue(dot(*ins), ins, outs)
            return
        acc = refs[n_in + n_out]
        k = pl.program_id(2)

        @pl.when(k == 0)
        def _():
            acc[...] = dot(*ins)

        @pl.when(jnp.logical_and(k > 0, k < nk - 1))
        def _():
            acc[...] += dot(*ins)

        @pl.when(k == nk - 1)
        def _():
            epilogue(acc[...] + dot(*ins), ins, outs)

    scratch = [pltpu.VMEM(acc_tile, F32)] if nk > 1 else []
    outs, job_outs = _call(name, body, grid, in_specs, out_specs, out_shapes, in_arrays, scratch=scratch,
                           sem=("parallel", "parallel", "arbitrary"), jobs=jobs, place=place)
    return _ret(outs, job_outs, jobs)


def _store(scale, dtype):
    def epilogue(acc, ins, outs):
        outs[0][...] = (acc * scale if scale != 1.0 else acc).astype(dtype)
    return epilogue


def mm_nn(name, a, w, out_dtype, tm=1024, tn=1024, tk=2048, jobs=()):
    m, kd = a.shape
    n = w.shape[1]
    tm, tn, tk = _tile(m, tm, 8), _tile(n, tn), _tile(kd, tk)
    return _mm(
        name, (n // tn, m // tm, kd // tk), [a, w],
        [pl.BlockSpec((tm, tk), lambda j, i, k: (i, k)), pl.BlockSpec((tk, tn), lambda j, i, k: (k, j))],
        [_sds((m, n), out_dtype)], [pl.BlockSpec((tm, tn), lambda j, i, k: (i, j))], (tm, tn),
        lambda a_ref, w_ref: _dot_nn(a_ref[...], w_ref[...]), _store(1.0, out_dtype), jobs)


def mm_nn_resid(name, a, w, x, scale, tm=1024, tn=1024, tk=1408, jobs=()):
    m, kd = a.shape
    n = w.shape[1]
    tm, tn, tk = _tile(m, tm, 8), _tile(n, tn), _tile(kd, tk)

    def epilogue(acc, ins, outs):
        outs[0][...] = ins[2][...] + scale * acc

    return _mm(
        name, (n // tn, m // tm, kd // tk), [a, w, x],
        [pl.BlockSpec((tm, tk), lambda j, i, k: (i, k)), pl.BlockSpec((tk, tn), lambda j, i, k: (k, j)),
         pl.BlockSpec((tm, tn), lambda j, i, k: (i, j))],
        [_sds((m, n), F32)], [pl.BlockSpec((tm, tn), lambda j, i, k: (i, j))], (tm, tn),
        lambda a_ref, w_ref, x_ref: _dot_nn(a_ref[...], w_ref[...]), epilogue, jobs)


def mm_nt(name, a, w, out_dtype, scale=1.0, tm=1024, tn=1024, tk=2048, jobs=()):
    m, kd = a.shape
    n = w.shape[0]
    tm, tn, tk = _tile(m, tm, 8), _tile(n, tn), _tile(kd, tk)
    return _mm(
        name, (n // tn, m // tm, kd // tk), [a, w],
        [pl.BlockSpec((tm, tk), lambda j, i, k: (i, k)), pl.BlockSpec((tn, tk), lambda j, i, k: (j, k))],
        [_sds((m, n), out_dtype)], [pl.BlockSpec((tm, tn), lambda j, i, k: (i, j))], (tm, tn),
        lambda a_ref, w_ref: _dot_nt(a_ref[...], w_ref[...]), _store(scale, out_dtype), jobs)


def mm_nt_pair(name, a3, w, out_dtype, tm=1024, tn=1024, tk=2816, jobs=()):
    _, m, f = a3.shape
    n = w.shape[0]
    tm, tn, tk = _tile(m, tm, 8), _tile(n, tn), _tile(f, tk)
    nkf = f // tk
    return _mm(
        name, (n // tn, m // tm, 2 * nkf), [a3, w],
        [pl.BlockSpec((None, tm, tk), lambda j, i, k: (k // nkf, i, k % nkf)),
         pl.BlockSpec((tn, tk), lambda j, i, k: (j, k))],
        [_sds((m, n), out_dtype)], [pl.BlockSpec((tm, tn), lambda j, i, k: (i, j))], (tm, tn),
        lambda a_ref, w_ref: _dot_nt(a_ref[...], w_ref[...]), _store(1.0, out_dtype), jobs)


def mm_nt_norm_bwd(name, a, w, x, r, g, dh_in, tm=512, tk=1408, jobs=()):
    pair = a.ndim == 3
    m, kd = a.shape[-2], a.shape[-1]
    d = w.shape[0]
    tm, tk = _tile(m, tm, 8), _tile(kd, tk)
    nkf = kd // tk
    nk = 2 * nkf if pair else nkf
    if pair:
        a_spec = pl.BlockSpec((None, tm, tk), lambda i, k: (k // nkf, i, k % nkf))
    else:
        a_spec = pl.BlockSpec((tm, tk), lambda i, k: (i, k))
    row = lambda i, k: (i, 0)
    fixed = lambda i, k: (0, 0)

    def body(a_ref, w_ref, x_ref, r_ref, g_ref, dh_ref, out_ref, outb_ref, dg_ref, *acc):
        i, k = pl.program_id(0), pl.program_id(1)
        dot = lambda: _dot_nt(a_ref[...], w_ref[...])

        def finish(dn):
            xh = x_ref[...] * r_ref[...]
            dxh = dn * g_ref[...]
            out = dh_ref[...] + r_ref[...] * (dxh - xh * jnp.mean(dxh * xh, axis=-1, keepdims=True))
            out_ref[...] = out
            outb_ref[...] = out.astype(BF16)
            part = jnp.sum(dn * xh, axis=0, keepdims=True)

            @pl.when(i == 0)
            def _():
                dg_ref[...] = part

            @pl.when(i > 0)
            def _():
                dg_ref[...] += part

        if nk == 1:
            finish(dot())
            return

        @pl.when(k == 0)
        def _():
            acc[0][...] = dot()

        @pl.when(jnp.logical_and(k > 0, k < nk - 1))
        def _():
            acc[0][...] += dot()

        @pl.when(k == nk - 1)
        def _():
            finish(acc[0][...] + dot())

    outs, job_outs = _call(
        name, body, (m // tm, nk),
        [a_spec, pl.BlockSpec((d, tk), lambda i, k: (0, k)), pl.BlockSpec((tm, d), row), pl.BlockSpec((tm, 1), row),
         pl.BlockSpec((1, d), fixed), pl.BlockSpec((tm, d), row)],
        [pl.BlockSpec((tm, d), row), pl.BlockSpec((tm, d), row), pl.BlockSpec((1, d), fixed)],
        [_sds((m, d), F32), _sds((m, d), BF16), _sds((1, d), F32)], [a, w, x, r, g, dh_in],
        scratch=[pltpu.VMEM((tm, d), F32)] if nk > 1 else [], sem=("arbitrary", "arbitrary"), jobs=jobs)
    return _ret(outs, job_outs, jobs, single=False)


def mm_tn(name, a, b, out_dtype, scale=1.0, tm=1024, tn=1024, tk=4096, jobs=()):
    kd, m = a.shape
    n = b.shape[1]
    tm, tn, tk = _tile(m, tm), _tile(n, tn), _tile(kd, tk, 16)
    return _mm(
        name, (n // tn, m // tm, kd // tk), [a, b],
        [pl.BlockSpec((tk, tm), lambda j, i, k: (k, i)), pl.BlockSpec((tk, tn), lambda j, i, k: (k, j))],
        [_sds((m, n), out_dtype)], [pl.BlockSpec((tm, tn), lambda j, i, k: (i, j))], (tm, tn),
        lambda a_ref, b_ref: _dot_tn(a_ref[...], b_ref[...]), _store(scale, out_dtype), jobs)


def mm_tn_pair(name, a, b3, out_dtype, tm=1024, tn=512, tk=4096, jobs=()):
    kd, m = a.shape
    f = b3.shape[2]
    tm, tn, tk = _tile(m, tm), _tile(f, tn), _tile(kd, tk, 16)
    nf = f // tn
    return _mm(
        name, (m // tm, 2 * nf, kd // tk), [a, b3],
        [pl.BlockSpec((tk, tm), lambda i, j, k: (k, i)),
         pl.BlockSpec((None, tk, tn), lambda i, j, k: (j // nf, k, j % nf))],
        [_sds((m, 2 * f), out_dtype)], [pl.BlockSpec((tm, tn), lambda i, j, k: (i, j))], (tm, tn),
        lambda a_ref, b_ref: _dot_tn(a_ref[...], b_ref[...]), _store(1.0, out_dtype), jobs)


def mm_tn_pair_half(name, a, b3, out_dtype, place, mine, tm=1024, tn=512, tk=4096, jobs=()):
    kd, m = a.shape
    f = b3.shape[2]
    tm, tn, tk = _tile(m // 2, tm), _tile(f, tn), _tile(kd, tk, 16)
    nf, nbm = f // tn, m // 2 // tm
    which = (lambda p: p[1]) if mine else (lambda p: 1 - p[1])
    return _mm(
        name, (nbm, 2 * nf, kd // tk), [a, b3],
        [pl.BlockSpec((tk, tm), lambda i, j, k, p: (k, i + which(p) * nbm)),
         pl.BlockSpec((None, tk, tn), lambda i, j, k, p: (j // nf, k, j % nf))],
        [_sds((m // 2, 2 * f), out_dtype)], [pl.BlockSpec((tm, tn), lambda i, j, k, p: (i, j))], (tm, tn),
        lambda a_ref, b_ref: _dot_tn(a_ref[...], b_ref[...]), _store(1.0, out_dtype), jobs, place)


def swiglu_fwd(name, n, w_in, tm=1024, tn=512, jobs=(), stride=1, phase=0, prev=None, compact=False):
    s, d = n.shape
    f = w_in.shape[1] // 2 * (stride if compact else 1)
    tm, tn = _tile(s, tm, 8), _tile(f, tn)
    nf = f // tn
    col = lambda j: j * stride + phase
    w_gate = (lambda j: j) if compact else col
    w_up = (lambda j: j + nf // stride) if compact else (lambda j: col(j) + nf)

    def body(n_ref, wg_ref, wu_ref, *rest):
        gu_ref, a_ref = rest[-2:]
        nv = n_ref[...]
        g = _dot_nn(nv, wg_ref[...])
        u = _dot_nn(nv, wu_ref[...])
        gu_ref[0] = g.astype(BF16)
        gu_ref[1] = u.astype(BF16)
        a_ref[...] = (g * jax.nn.sigmoid(g) * u).astype(BF16)

    kept = list(prev) if prev is not None else []
    outs, job_outs = _call(
        name, body, (nf // stride, s // tm),
        [pl.BlockSpec((tm, d), lambda j, i: (i, 0)), pl.BlockSpec((d, tn), lambda j, i: (0, w_gate(j))),
         pl.BlockSpec((d, tn), lambda j, i: (0, w_up(j)))] + [ANY] * len(kept),
        [pl.BlockSpec((2, tm, tn), lambda j, i: (0, i, col(j))), pl.BlockSpec((tm, tn), lambda j, i: (i, col(j)))],
        [_sds((2, s, f), BF16), _sds((s, f), BF16)], [n, w_in, w_in] + kept, sem=("parallel", "parallel"),
        jobs=jobs, carried={3 + k: k for k in range(len(kept))})
    return _ret(outs, job_outs, jobs, single=False)


def swiglu_bwd(name, dh, w_out, gu, scale, tm=1024, tn=512, jobs=()):
    s, d = dh.shape
    f = w_out.shape[0]
    tm, tn = _tile(s, tm, 8), _tile(f, tn)

    sub = _tile(tm, 256, 8)

    def body(dh_ref, w_ref, gu_ref, out_ref):
        for lo in range(0, tm, sub):
            rows = slice(lo, lo + sub)
            da = (_dot_nt(dh_ref[rows, :], w_ref[...]) * scale).astype(BF16)
            g = gu_ref[0, rows, :]
            u = gu_ref[1, rows, :]
            sg = 0.5 * jnp.tanh(0.5 * g) + 0.5
            t = g * sg
            out_ref[0, rows, :] = da * (u * (sg + t * (1.0 - sg)))
            out_ref[1, rows, :] = da * t

    outs, job_outs = _call(
        name, body, (f // tn, s // tm),
        [pl.BlockSpec((tm, d), lambda j, i: (i, 0)), pl.BlockSpec((tn, d), lambda j, i: (j, 0)),
         pl.BlockSpec((2, tm, tn), lambda j, i: (0, i, j))],
        [pl.BlockSpec((2, tm, tn), lambda j, i: (0, i, j))],
        [_sds((2, s, f), BF16)], [dh, w_out, gu], sem=("parallel", "parallel"), jobs=jobs)
    return _ret(outs, job_outs, jobs)


HALO = 16


def _conv_inputs(z_ref, hgc_ref, hhc_ref, i, cw, tm):
    gc = z_ref[:, cw:2 * cw].astype(F32)
    hc = z_ref[:, 2 * cw:3 * cw].astype(F32)
    cin = gc * hc
    halo = hgc_ref[...].astype(F32) * hhc_ref[...].astype(F32) * (i > 0).astype(F32)
    row = lax.broadcasted_iota(jnp.int32, (tm, cw), 0)
    x1 = jnp.where(row == 0, halo[HALO - 1:HALO], pltpu.roll(cin, 1, 0))
    x2 = jnp.where(row == 0, halo[HALO - 2:HALO - 1], jnp.where(row == 1, halo[HALO - 1:HALO], pltpu.roll(cin, 2, 0)))
    return gc, hc, cin, x1, x2


def _tril(w):
    r = lax.broadcasted_iota(jnp.int32, w.shape, 0)
    c = lax.broadcasted_iota(jnp.int32, w.shape, 1)
    return jnp.where(r >= c, w, jnp.zeros_like(w))


def mixer_fwd(name, z, conv_w, conv_b, g_v, w_s, b_t, tm=256, jobs=()):
    s, zc = z.shape
    cw = conv_w.shape[1]
    gw = g_v.shape[1]
    heads = gw // GROUP
    tm = _tile(s, tm)
    hb = tm // HALO

    def body(z_ref, hgc_ref, hhc_ref, cw_ref, cb_ref, gv_ref, ws_ref, bt_ref, y_ref):
        i = pl.program_id(0)
        _, _, cin, x1, x2 = _conv_inputs(z_ref, hgc_ref, hhc_ref, i, cw, tm)
        cv = cb_ref[...] + cw_ref[2:3, :] * cin + cw_ref[1:2, :] * x1 + cw_ref[0:1, :] * x2
        y_ref[:, 0:cw] = (z_ref[:, 0:cw].astype(F32) * cv).astype(BF16)
        for h in range(heads):
            lo = h * GROUP
            vh = z_ref[:, 3 * cw + gw + lo:3 * cw + gw + lo + GROUP].astype(F32)
            rv = lax.rsqrt(jnp.mean(vh * vh, axis=-1, keepdims=True) + EPS)
            vn = (vh * rv * gv_ref[:, lo:lo + GROUP]).astype(BF16)
            w = _tril(ws_ref[h]).astype(BF16)
            for n in range(tm // GROUP):
                rows = slice(n * GROUP, (n + 1) * GROUP)
                sg = _dot_nn(w, vn[rows]) + bt_ref[:, h:h + 1]
                u = z_ref[rows, 3 * cw + lo:3 * cw + lo + GROUP].astype(F32)
                y_ref[rows, cw + lo:cw + lo + GROUP] = (u * sg).astype(BF16)

    fixed2 = lambda i: (0, 0)
    outs, job_outs = _call(
        name, body, (s // tm,),
        [pl.BlockSpec((tm, zc), lambda i: (i, 0)),
         pl.BlockSpec((HALO, cw), lambda i: (jnp.maximum(i * hb - 1, 0), 1)),
         pl.BlockSpec((HALO, cw), lambda i: (jnp.maximum(i * hb - 1, 0), 2)),
         pl.BlockSpec(conv_w.shape, fixed2), pl.BlockSpec(conv_b.shape, fixed2),
         pl.BlockSpec(g_v.shape, fixed2), pl.BlockSpec(w_s.shape, lambda i: (0, 0, 0)),
         pl.BlockSpec(b_t.shape, fixed2)],
        [pl.BlockSpec((tm, cw + gw), lambda i: (i, 0))], [_sds((s, cw + gw), BF16)],
        [z, z, z, conv_w, conv_b, g_v, w_s, b_t], sem=("arbitrary",), jobs=jobs)
    return _ret(outs, job_outs, jobs)


def mixer_bwd(name, z, dy, conv_w, conv_b, g_v, w_s, b_t, tm=256, jobs=()):
    s, zc = z.shape
    cw = conv_w.shape[1]
    gw = g_v.shape[1]
    heads = gw // GROUP
    tm = _tile(s, tm)
    hb = tm // HALO
    nsteps = s // tm
    last_halo = s // HALO - 1

    def body(z_ref, hgc_ref, hhc_ref, ngb_ref, dy_ref, ndy_ref, cw_ref, cb_ref, gv_ref, ws_ref, bt_ref,
             dz_ref, sm_ref, dws_ref, dbt_ref, dsg_ref):
        i = pl.program_id(0)

        @pl.when(i == 0)
        def _():
            sm_ref[...] = jnp.zeros_like(sm_ref)
            dws_ref[...] = jnp.zeros_like(dws_ref)
            dsg_ref[...] = jnp.zeros_like(dsg_ref)

        gc, hc, cin, x1, x2 = _conv_inputs(z_ref, hgc_ref, hhc_ref, i, cw, tm)
        w0, w1, w2 = cw_ref[0:1, :], cw_ref[1:2, :], cw_ref[2:3, :]
        cv = cb_ref[...] + w2 * cin + w1 * x1 + w0 * x2
        gb = z_ref[:, 0:cw].astype(F32)
        dyc = dy_ref[:, 0:cw].astype(F32)
        dz_ref[:, 0:cw] = (dyc * cv).astype(BF16)
        dcv = dyc * gb
        nxt = ndy_ref[...].astype(F32) * ngb_ref[...].astype(F32) * (i < nsteps - 1).astype(F32)
        row = lax.broadcasted_iota(jnp.int32, (tm, cw), 0)
        d1 = jnp.where(row == tm - 1, nxt[0:1], pltpu.roll(dcv, tm - 1, 0))
        d2 = jnp.where(row == tm - 1, nxt[1:2], jnp.where(row == tm - 2, nxt[0:1], pltpu.roll(dcv, tm - 2, 0)))
        dcin = w2 * dcv + w1 * d1 + w0 * d2
        dz_ref[:, cw:2 * cw] = (dcin * hc).astype(BF16)
        dz_ref[:, 2 * cw:3 * cw] = (dcin * gc).astype(BF16)
        sm_ref[0:1, :] += jnp.sum(dcv * x2, axis=0, keepdims=True)
        sm_ref[1:2, :] += jnp.sum(dcv * x1, axis=0, keepdims=True)
        sm_ref[2:3, :] += jnp.sum(dcv * cin, axis=0, keepdims=True)
        sm_ref[3:4, :] += jnp.sum(dcv, axis=0, keepdims=True)

        for h in range(heads):
            lo = h * GROUP
            vcol = slice(3 * cw + gw + lo, 3 * cw + gw + lo + GROUP)
            ucol = slice(3 * cw + lo, 3 * cw + lo + GROUP)
            vh = z_ref[:, vcol].astype(F32)
            rv = lax.rsqrt(jnp.mean(vh * vh, axis=-1, keepdims=True) + EPS)
            xh = vh * rv
            gvh = gv_ref[:, lo:lo + GROUP]
            vn = (xh * gvh).astype(BF16)
            w = _tril(ws_ref[h]).astype(BF16)
            dgv = jnp.zeros((1, GROUP), F32)
            for n in range(tm // GROUP):
                rows = slice(n * GROUP, (n + 1) * GROUP)
                sg = _dot_nn(w, vn[rows]) + bt_ref[:, h:h + 1]
                dyg = dy_ref[rows, cw + lo:cw + lo + GROUP].astype(F32)
                dsg = dyg * z_ref[rows, ucol].astype(F32)
                dz_ref[rows, ucol] = (dyg * sg).astype(BF16)
                dsgb = dsg.astype(BF16)
                dvn = _dot_tn(w, dsgb)
                dws_ref[h] += _dot_nt(dsgb, vn[rows])
                dsg_ref[:, lo:lo + GROUP] += dsg
                xhc = xh[rows]
                dgv = dgv + jnp.sum(dvn * xhc, axis=0, keepdims=True)
                dxh = dvn * gvh
                dv = rv[rows] * (dxh - xhc * jnp.mean(dxh * xhc, axis=-1, keepdims=True))
                dz_ref[rows, vcol] = dv.astype(BF16)
            sm_ref[4:5, lo:lo + GROUP] += dgv

        @pl.when(i == nsteps - 1)
        def _():
            for h in range(heads):
                dws_ref[h] = _tril(dws_ref[h])
                dbt_ref[:, h:h + 1] = jnp.sum(dsg_ref[:, h * GROUP:(h + 1) * GROUP], axis=-1, keepdims=True)

    fixed2 = lambda i: (0, 0)
    fixed3 = lambda i: (0, 0, 0)
    prev = lambda col: (lambda i: (jnp.maximum(i * hb - 1, 0), col))
    nxt_blk = lambda i: (jnp.minimum((i + 1) * hb, last_halo), 0)
    outs, job_outs = _call(
        name, body, (nsteps,),
        [pl.BlockSpec((tm, zc), lambda i: (i, 0)),
         pl.BlockSpec((HALO, cw), prev(1)), pl.BlockSpec((HALO, cw), prev(2)),
         pl.BlockSpec((HALO, cw), nxt_blk),
         pl.BlockSpec((tm, cw + gw), lambda i: (i, 0)), pl.BlockSpec((HALO, cw), nxt_blk),
         pl.BlockSpec(conv_w.shape, fixed2), pl.BlockSpec(conv_b.shape, fixed2),
         pl.BlockSpec(g_v.shape, fixed2), pl.BlockSpec(w_s.shape, fixed3), pl.BlockSpec(b_t.shape, fixed2)],
        [pl.BlockSpec((tm, zc), lambda i: (i, 0)), pl.BlockSpec((8, cw), fixed2),
         pl.BlockSpec(w_s.shape, fixed3), pl.BlockSpec(b_t.shape, fixed2)],
        [_sds((s, zc), BF16), _sds((8, cw), F32), _sds(w_s.shape, F32), _sds(b_t.shape, F32)],
        [z, z, z, z, dy, dy, conv_w, conv_b, g_v, w_s, b_t],
        scratch=[pltpu.VMEM((GROUP, gw), F32)], sem=("arbitrary",), jobs=jobs)
    return _ret(outs, job_outs, jobs, single=False)


def _softmax_rows(sc):
    e = jnp.exp(sc - jnp.max(sc, axis=-1, keepdims=True))
    return e / jnp.sum(e, axis=-1, keepdims=True)


def attn_fwd(name, q, k, v, tm=512, jobs=()):
    s, d = q.shape
    m = k.shape[0]
    hd = d // XA_HEADS
    scale = hd ** -0.5
    tm = _tile(s, tm, 8)

    def body(q_ref, k_ref, v_ref, o_ref):
        for h in range(XA_HEADS):
            cols = slice(h * hd, (h + 1) * hd)
            p = _softmax_rows(_dot_nt(q_ref[:, cols], k_ref[:, cols]) * scale)
            o_ref[:, cols] = _dot_nn(p.astype(BF16), v_ref[:, cols]).astype(BF16)

    outs, job_outs = _call(
        name, body, (s // tm,),
        [pl.BlockSpec((tm, d), lambda i: (i, 0)), pl.BlockSpec((m, d), lambda i: (0, 0)),
         pl.BlockSpec((m, d), lambda i: (0, 0))],
        [pl.BlockSpec((tm, d), lambda i: (i, 0))], [_sds((s, d), BF16)], [q, k, v], sem=("arbitrary",), jobs=jobs)
    return _ret(outs, job_outs, jobs)


def attn_bwd(name, q, k, v, do, tm=512):
    s, d = q.shape
    m = k.shape[0]
    hd = d // XA_HEADS
    scale = hd ** -0.5
    tm = _tile(s, tm, 8)

    def body(q_ref, k_ref, v_ref, do_ref, dq_ref, dk_ref, dv_ref):
        i = pl.program_id(0)

        @pl.when(i == 0)
        def _():
            dk_ref[...] = jnp.zeros_like(dk_ref)
            dv_ref[...] = jnp.zeros_like(dv_ref)

        for h in range(XA_HEADS):
            cols = slice(h * hd, (h + 1) * hd)
            qh = q_ref[:, cols]
            doh = do_ref[:, cols]
            p = _softmax_rows(_dot_nt(qh, k_ref[:, cols]) * scale)
            dp = _dot_nt(doh, v_ref[:, cols])
            ds = (p * (dp - jnp.sum(dp * p, axis=-1, keepdims=True)) * scale).astype(BF16)
            dq_ref[:, cols] = _dot_nn(ds, k_ref[:, cols]).astype(BF16)
            dk_ref[:, cols] += _dot_tn(ds, qh)
            dv_ref[:, cols] += _dot_tn(p.astype(BF16), doh)

    row = lambda i: (i, 0)
    fixed = lambda i: (0, 0)
    return _call(
        name, body, (s // tm,),
        [pl.BlockSpec((tm, d), row), pl.BlockSpec((m, d), fixed), pl.BlockSpec((m, d), fixed),
         pl.BlockSpec((tm, d), row)],
        [pl.BlockSpec((tm, d), row), pl.BlockSpec((m, d), fixed), pl.BlockSpec((m, d), fixed)],
        [_sds((s, d), BF16), _sds((m, d), F32), _sds((m, d), F32)], [q, k, v, do], sem=("arbitrary",))[0]


def _grid2(rows, cols, row_mult):
    tr, tc = _tile(rows, 512, row_mult), _tile(cols, 2048)
    return tr, tc, rows // tr, cols // tc


def cast_place(name, block, axis, place, column_half=None):
    r, c = block.shape
    if column_half is not None:
        c //= 2
    tr, tc, nbr, nbc = _grid2(r, c, 16)
    first = 0 if column_half is None else column_half * nbc
    if axis == 1:
        dst = lambda i, j, p: (i, j + p[0] * nbc)
    else:
        dst = lambda i, j, p: (i + p[0] * nbr, j)

    def body(p_ref, w_ref, out_ref):
        out_ref[...] = w_ref[...].astype(BF16)

    return pl.pallas_call(
        body, name=name,
        grid_spec=pltpu.PrefetchScalarGridSpec(
            num_scalar_prefetch=1, grid=(nbr, nbc),
            in_specs=[pl.BlockSpec((tr, tc), lambda i, j, p: (i, j + first))],
            out_specs=pl.BlockSpec((tr, tc), dst)),
        out_shape=_sds(_full_shape((r, c), axis), BF16),
        compiler_params=_params(("parallel", "parallel")),
    )(place, block)


def merge_column_halves(name, left, right):
    r, c = left.shape
    w = c // N_CHIPS
    tr = _tile(r, 512, 16)

    def body(l_ref, r_ref, out_ref):
        side = pl.program_id(1) % 2

        @pl.when(side == 0)
        def _():
            out_ref[...] = l_ref[...]

        @pl.when(side == 1)
        def _():
            out_ref[...] = r_ref[...]

    half = pl.BlockSpec((tr, w), lambda i, j: (i, j // 2))
    return _call(name, body, (r // tr, 2 * N_CHIPS), [half, half], [pl.BlockSpec((tr, w), lambda i, j: (i, j))],
                 [_sds((r, 2 * c), left.dtype)], [left, right], sem=("parallel", "arbitrary"))[0][0]


def pair_add(name, grad, peer, axis, place):
    hr, hc = peer.shape
    tr, tc, nbr, nbc = _grid2(hr, hc, 16)
    same = lambda i, j, p: (i, j)
    if grad.shape == peer.shape:
        mine = same
    elif axis == 1:
        mine = lambda i, j, p: (i + p[1] * nbr, j)
    else:
        mine = lambda i, j, p: (i, j + p[1] * nbc)

    def body(p_ref, g_ref, q_ref, out_ref):
        out_ref[...] = (g_ref[...].astype(F32) + q_ref[...].astype(F32)).astype(BF16)

    return pl.pallas_call(
        body, name=name,
        grid_spec=pltpu.PrefetchScalarGridSpec(
            num_scalar_prefetch=1, grid=(nbr, nbc),
            in_specs=[pl.BlockSpec((tr, tc), mine), pl.BlockSpec((tr, tc), same)],
            out_specs=pl.BlockSpec((tr, tc), same)),
        out_shape=_sds((hr, hc), BF16),
        compiler_params=_params(("parallel", "parallel")),
    )(place, grad, peer)


def cross_sum(name, part, land, axis, shape, place):
    _, sr, sc = land.shape
    tr, tc, nbr, nbc = _grid2(sr, sc, 16)
    if axis == 1:
        own = lambda i, j, p: (i, j + p[0] * nbc)
        dst = lambda i, j, p: (i + p[1] * nbr, j)
    else:
        own = lambda i, j, p: (i + p[0] * nbr, j)
        dst = lambda i, j, p: (i, j + p[1] * nbc)

    def body(p_ref, own_ref, land_ref, out_ref):
        out_ref[...] = ((own_ref[...].astype(F32) + land_ref[0].astype(F32))
                        + (land_ref[1].astype(F32) + land_ref[2].astype(F32)))

    return pl.pallas_call(
        body, name=name,
        grid_spec=pltpu.PrefetchScalarGridSpec(
            num_scalar_prefetch=1, grid=(nbr, nbc),
            in_specs=[pl.BlockSpec((tr, tc), own), pl.BlockSpec((3, tr, tc), lambda i, j, p: (0, i, j))],
            out_specs=pl.BlockSpec((tr, tc), dst)),
        out_shape=_sds(_block(shape, axis), F32),
        compiler_params=_params(("parallel", "parallel")),
    )(place, part, land)


def _adam_math(w, g, m, v):
    m = ADAM_B1 * m + (1.0 - ADAM_B1) * g
    v = ADAM_B2 * v + (1.0 - ADAM_B2) * (g * g)
    m_hat = m / (1.0 - ADAM_B1 ** ADAM_STEP)
    v_hat = v / (1.0 - ADAM_B2 ** ADAM_STEP)
    delta = -ADAM_LR * (m_hat / (jnp.sqrt(v_hat) + ADAM_EPS) + ADAM_WD * w)
    return delta, m, v


def adamw(name, w, g, m, v, jobs=()):
    r, c = w.shape
    tr, tc = _tile(r, 256, 8), _tile(c, 1408)

    def body(w_ref, g_ref, m_ref, v_ref, g_out, d_out, m_out, v_out):
        d, mm, vv = _adam_math(w_ref[...], g_ref[...], m_ref[...], v_ref[...])
        g_out[...] = g_ref[...]
        d_out[...] = d
        m_out[...] = mm
        v_out[...] = vv

    spec = pl.BlockSpec((tr, tc), lambda i, j: (i, j))
    outs, job_outs = _call(name, body, (r // tr, c // tc), [spec] * 4, [spec] * 4, [_sds((r, c), F32)] * 4,
                           [w, g, m, v], sem=("parallel", "parallel"), jobs=jobs)
    return _ret(outs, job_outs, jobs, single=False)


def small_sum(name, stacks):
    def body(*refs):
        for s_ref, out_ref in zip(refs[:len(stacks)], refs[len(stacks):]):
            acc = s_ref[0]
            for d in range(1, s_ref.shape[0]):
                acc = acc + s_ref[d]
            out_ref[...] = acc

    return pl.pallas_call(body, name=name, out_shape=[_sds(s.shape[1:], F32) for s in stacks])(*stacks)


WEIGHTS = ["g_ffn1", "w_ffn1_in", "w_ffn1_out", "g_mix", "w_mix_in", "conv_w", "conv_b", "g_gm_v", "w_spatial",
           "b_spatial", "w_mix_out", "g_xattn", "g_mem", "w_xq", "w_xk", "w_xv", "w_xo", "g_ffn2", "w_ffn2_in",
           "w_ffn2_out", "g_final"]
BIG = {"w_ffn1_in": 1, "w_ffn1_out": 0, "w_mix_in": 1, "w_mix_out": 0, "w_xq": 0, "w_xk": 0, "w_xv": 0, "w_xo": 0,
       "w_ffn2_in": 1, "w_ffn2_out": 0}
SMALL = [n for n in WEIGHTS if n not in BIG]
LATE_SMALL = ["g_ffn1"]
EARLY_SMALL = [n for n in SMALL if n not in LATE_SMALL]


def _pack(arrays):
    flat = jnp.concatenate([a.reshape(-1) for a in arrays])
    rows = -(-flat.shape[0] // 1024) * 8
    return jnp.pad(flat, (0, rows * 128 - flat.shape[0])).reshape(rows, 128)


def _unpack(buf, shapes):
    flat = buf.reshape(-1)
    out, pos = [], 0
    for shp in shapes:
        n = math.prod(shp)
        out.append(flat[pos:pos + n].reshape(shp))
        pos += n
    return out


def kernel(x, mem, g_ffn1, w_ffn1_in, w_ffn1_out, g_mix, w_mix_in, conv_w, conv_b, g_gm_v, w_spatial, b_spatial, w_mix_out, g_xattn, g_mem, w_xq, w_xk, w_xv, w_xo, g_ffn2, w_ffn2_in, w_ffn2_out, g_final, loss_target, m_g_ffn1, m_w_ffn1_in, m_w_ffn1_out, m_g_mix, m_w_mix_in, m_conv_w, m_conv_b, m_g_gm_v, m_w_spatial, m_b_spatial, m_w_mix_out, m_g_xattn, m_g_mem, m_w_xq, m_w_xk, m_w_xv, m_w_xo, m_g_ffn2, m_w_ffn2_in, m_w_ffn2_out, m_g_final, v_g_ffn1, v_w_ffn1_in, v_w_ffn1_out, v_g_mix, v_w_mix_in, v_conv_w, v_conv_b, v_g_gm_v, v_w_spatial, v_b_spatial, v_w_mix_out, v_g_xattn, v_g_mem, v_w_xq, v_w_xk, v_w_xv, v_w_xo, v_g_ffn2, v_w_ffn2_in, v_w_ffn2_out, v_g_final):
    given = dict(locals())
    wts = {n: given[n] for n in WEIGHTS}
    mom = {n: given["m_" + n] for n in WEIGHTS}
    var = {n: given["v_" + n] for n in WEIGHTS}

    xi, yi, ci = lax.axis_index("x"), lax.axis_index("y"), lax.axis_index("c")
    blk = 2 * xi + yi
    place = jnp.stack([blk, ci]).astype(jnp.int32)

    x2, mem2, tgt = x[0], mem[0], loss_target[0]
    w_s, b_t = w_spatial[0], b_spatial[0].T
    gf = g_final[None]

    rest = [n for n in BIG if n != "w_ffn1_in"]
    own = {n: cast_place("cast_" + n, wts[n][0], BIG[n], place) for n in rest}
    own_left = cast_place("cast_w_ffn1_in_left", wts["w_ffn1_in"][0], 1, place, column_half=0)
    own_right = cast_place("cast_w_ffn1_in_right", wts["w_ffn1_in"][0], 1, place, column_half=1)
    shape = {n: own[n].shape for n in rest}
    shape["w_ffn1_in"] = _full_shape(wts["w_ffn1_in"][0].shape, 1)
    full = {}

    def gather_now(name, arrays, axes, collective_id):
        job = gather_job([(a, ax, WHOLE, WHOLE) for a, ax in zip(arrays, axes)])
        return by_sequencer(name, job, "gather", collective_id)[1]

    _, (conv_taps,) = by_sequencer("gather_conv_taps", columns_job(jnp.pad(conv_w[0], ((0, 8 - CONV_K), (0, 0)))),
                                   "chips", 14)
    (w1_left,) = gather_now("gather_w_ffn1_in_left", [own_left], [1], 1)
    (w1_right,) = gather_now("gather_w_ffn1_in_right", [own_right], [1], 2)
    groups = [["w_ffn1_out"], ["w_mix_in", "w_mix_out"], ["w_xq", "w_xk", "w_xv", "w_xo"], ["w_ffn2_in"],
              ["w_ffn2_out"]]
    for g, names in enumerate(groups):
        got = gather_now("gather_" + "_".join(names), [own[n] for n in names], [BIG[n] for n in names], 3 + g)
        full.update(zip(names, got))

    half_cols = dict(tm=512, tn=shape["w_ffn1_in"][1] // (2 * N_CHIPS), stride=2, compact=True)
    n1, r1 = rmsnorm_fwd("norm1", x2, g_ffn1)
    halves = swiglu_fwd("ffn1_in_left", n1, w1_left, phase=0, **half_cols)
    gu1, a1 = swiglu_fwd("ffn1_in_right", n1, w1_right, phase=1, prev=halves, **half_cols)
    h1 = mm_nn_resid("ffn1_out", a1, full["w_ffn1_out"], x2, 0.5, tm=512, tk=5632)
    n2, r2 = rmsnorm_fwd("norm2", h1, g_mix)
    z = mm_nn("mix_in", n2, full["w_mix_in"], BF16)
    ycat = mixer_fwd("mixer", z, conv_taps, conv_b, g_gm_v, w_s, b_t)
    h2 = mm_nn_resid("mix_out", ycat, full["w_mix_out"], h1, 1.0, tk=2048)
    n3, r3 = rmsnorm_fwd("norm3", h2, g_xattn)
    mem2, h2 = lax.optimization_barrier((mem2, h2))
    mn, rm = rmsnorm_fwd("norm_mem", mem2, g_mem)
    q = mm_nn("xq", n3, full["w_xq"], BF16)
    k = mm_nn("xk", mn, full["w_xk"], BF16)
    v = mm_nn("xv", mn, full["w_xv"], BF16)
    o = attn_fwd("attn", q, k, v)
    h3 = mm_nn_resid("xo", o, full["w_xo"], h2, 1.0, tk=2048)
    n4, r4 = rmsnorm_fwd("norm4", h3, g_ffn2)
    gu2, a2 = swiglu_fwd("ffn2_in", n4, full["w_ffn2_in"])
    h4 = mm_nn_resid("ffn2_out", a2, full["w_ffn2_out"], h3, 0.5, tm=512, tk=5632)
    late, h4 = lax.optimization_barrier((wts["w_ffn1_in"][0], h4))
    (full["w_ffn1_in"],) = gather_now("gather_w_ffn1_in", [cast_place("cast_w_ffn1_in", late, 1, place)], [1], 15)
    loss_blk, dh4, dh4b, dg_final = loss_head("loss_head", h4, gf, tgt)

    dw, peer, part, land, half, reduced, grads = {}, {}, {}, {}, {}, {}, {}
    uses = {"sibling": 0, "chips": 0}

    def on_sequencer(name, job, peers):
        uses[peers] += 1
        return by_sequencer(name, job, peers, {"sibling": 10, "chips": 8}[peers] + uses[peers] % 2)

    def start_pair(*names):
        kept, got = on_sequencer("pair_" + "_".join(names), pair_job([dw[n] for n in names], [BIG[n] for n in names]),
                                 "sibling")
        for n, k, p in zip(names, kept, got):
            dw[n], peer[n] = k, p

    def finish_pair(*names):
        for n in names:
            part[n] = pair_add("pair_add_" + n, dw[n], peer[n], BIG[n], place)
        start_cross(*names)

    def start_cross(*names):
        kept, got = on_sequencer("cross_" + "_".join(names),
                                 cross_job([(part[n], BIG[n], shape[n], None, WHOLE) for n in names]), "chips")
        for n, k, l in zip(names, kept, got):
            part[n], land[n] = k, l

    def finish_cross(*names):
        for n in names:
            half[n] = cross_sum("cross_sum_" + n, part[n], land[n], BIG[n], shape[n], place)
        _, got = on_sequencer("final_" + "_".join(names),
                              final_job([half[n] for n in names], [BIG[n] for n in names], [shape[n] for n in names]),
                              "sibling")
        reduced.update(zip(names, got))

    delta, new_m, new_v = {}, {}, {}

    def update(*names):
        for n in names:
            grads[n], delta[n], new_m[n], new_v[n] = adamw("adamw_" + n, wts[n][0], reduced[n], mom[n][0], var[n][0])

    dgu2 = swiglu_bwd("ffn2_dact", dh4b, full["w_ffn2_out"], gu2, 0.5)
    dw["w_ffn2_in"] = mm_tn_pair("ffn2_dwin", n4, dgu2, BF16)
    start_pair("w_ffn2_in")
    dw["w_ffn2_out"] = mm_tn("ffn2_dwout", a2, dh4b, BF16, scale=0.5)
    finish_pair("w_ffn2_in")
    start_pair("w_ffn2_out")
    dh3, dh3b, dg_ffn2 = mm_nt_norm_bwd("ffn2_dn", dgu2, full["w_ffn2_in"], h3, r4, g_ffn2, dh4)
    finish_pair("w_ffn2_out")

    dw["w_xo"] = mm_tn("xo_dw", o, dh3b, BF16)
    finish_cross("w_ffn2_in")
    do = mm_nt("xo_dx", dh3b, full["w_xo"], BF16)
    dq, dk, dv = attn_bwd("attn_bwd", q, k, v, do)
    dkb, dvb = dk.astype(BF16), dv.astype(BF16)
    dw["w_xq"] = mm_tn("xq_dw", n3, dq, BF16)
    update("w_ffn2_in")
    dh2, dh2b, dg_xattn = mm_nt_norm_bwd("xq_dx", dq, full["w_xq"], h2, r3, g_xattn, dh3, tk=1024)
    dw["w_xk"] = mm_tn("xk_dw", mn, dkb, BF16)
    dw["w_xv"] = mm_tn("xv_dw", mn, dvb, BF16)
    dmn_k = mm_nt("xk_dx", dkb, full["w_xk"], F32)
    dmn_v = mm_nt("xv_dx", dvb, full["w_xv"], F32)
    dg_mem = gain_grad("norm_mem_bwd", dmn_k, dmn_v, mem2, rm)

    finish_cross("w_ffn2_out")
    dw["w_mix_out"] = mm_tn("mix_out_dw", ycat, dh2b, BF16)
    attn_names = ["w_xo", "w_xq", "w_xk", "w_xv", "w_mix_out"]
    start_pair(*attn_names)
    dycat = mm_nt("mix_out_dx", dh2b, full["w_mix_out"], BF16)
    finish_pair(*attn_names)
    update("w_ffn2_out")
    dz, dsmall, dws, dbt = mixer_bwd("mixer_bwd", z, dycat, conv_taps, conv_b, g_gm_v, w_s, b_t)
    dw["w_mix_in"] = mm_tn("mix_in_dw", n2, dz, BF16)
    start_pair("w_mix_in")
    dh1, dh1b, dg_mix = mm_nt_norm_bwd("mix_in_dx", dz, full["w_mix_in"], h1, r2, g_mix, dh2, tk=1280)
    finish_pair("w_mix_in")
    early = {"g_mix": dg_mix, "conv_w": dsmall[0:CONV_K], "conv_b": dsmall[3:4], "g_gm_v": dsmall[4:5],
             "w_spatial": dws, "b_spatial": dbt.T, "g_xattn": dg_xattn, "g_mem": dg_mem, "g_ffn2": dg_ffn2,
             "g_final": dg_final}
    _, (early_all,) = by_sequencer("stack_early", stack_job(_pack([early[n] for n in EARLY_SMALL])), "all", 12)

    finish_cross(*attn_names)
    dw["w_ffn1_out"] = mm_tn("ffn1_dwout", a1, dh1b, BF16, scale=0.5)
    start_pair("w_ffn1_out")
    finish_cross("w_mix_in")
    dgu1 = swiglu_bwd("ffn1_dact", dh1b, full["w_ffn1_out"], gu1, 0.5)
    finish_pair("w_ffn1_out")
    update(*attn_names)
    theirs = mm_tn_pair_half("ffn1_dwin_theirs", n1, dgu1, BF16, place, False)
    (theirs,), (from_sibling,) = on_sequencer("pair_w_ffn1_in", pair_job([theirs], [1], is_half=True), "sibling")
    mine = mm_tn_pair_half("ffn1_dwin_mine", n1, dgu1, BF16, place, True)
    part["w_ffn1_in"] = pair_add("pair_add_w_ffn1_in", mine, from_sibling, 1, place)
    start_cross("w_ffn1_in")
    update("w_mix_in")
    finish_cross("w_ffn1_out")
    dn1 = mm_nt_pair("ffn1_dn", dgu1, full["w_ffn1_in"], F32)
    dx, _, dg_ffn1 = rmsnorm_bwd("norm1_bwd", dn1, x2, r1, g_ffn1, dh1)
    _, (late_all,) = by_sequencer("stack_late", stack_job(_pack([dg_ffn1])), "all", 13)
    finish_cross("w_ffn1_in")
    update("w_ffn1_out", "w_ffn1_in")

    early_sum, late_sum = small_sum("small_sum", [early_all, late_all])
    for n, g in zip(EARLY_SMALL, _unpack(early_sum, [early[n].shape for n in EARLY_SMALL])):
        grads[n] = g
    grads["g_ffn1"] = _unpack(late_sum, [dg_ffn1.shape])[0]
    taps_cols = conv_w.shape[2]
    grads["conv_w"] = lax.dynamic_slice_in_dim(grads["conv_w"], blk * taps_cols, taps_cols, axis=1)
    packed = [_pack([src[n] for n in SMALL]) for src in (wts, grads, mom, var)]
    own_shapes = [wts[n].shape for n in SMALL]
    for dst, buf in zip((delta, new_m, new_v), adamw("adamw_small", *packed)[1:]):
        for n, a in zip(SMALL, _unpack(buf, own_shapes)):
            dst[n] = a

    loss = lax.psum(loss_blk[0, 0], ("x", "y", "c"))
    outs = [loss, dx[None]]
    for group in (grads, delta, new_m, new_v):
        outs += [group[n].reshape(wts[n].shape) for n in WEIGHTS]
    return tuple(outs)
```

```python
import math

import jax
import jax.numpy as jnp
from jax import lax
from jax.experimental import pallas as pl
from jax.experimental.pallas import tpu as pltpu
from jax.experimental.pallas import tpu_sc as plsc

F32 = jnp.float32
BF16 = jnp.bfloat16
EPS = 1e-6
GROUP = 128
XA_HEADS = 4
CONV_K = 3
N_CHIPS = 4
VMEM_LIMIT_BYTES = 56 * 1024 * 1024

ADAM_LR = 0.001
ADAM_B1 = 0.9
ADAM_B2 = 0.999
ADAM_EPS = 1e-08
ADAM_WD = 0.01
ADAM_STEP = 10

MESH = pl.DeviceIdType.MESH
ANY = pl.BlockSpec(memory_space=pl.ANY)


def _tile(dim, pref, mult=128):
    if dim <= pref:
        return dim
    t = (pref // mult) * mult
    while t >= mult:
        if dim % t == 0:
            return t
        t -= mult
    raise ValueError(f"no tile for {dim} under {pref}")


def _params(sem):
    return pltpu.CompilerParams(dimension_semantics=sem, vmem_limit_bytes=VMEM_LIMIT_BYTES)


def _sds(shape, dtype):
    return jax.ShapeDtypeStruct(shape, dtype)


def _dot_nn(a, b):
    return jnp.dot(a, b, preferred_element_type=F32)


def _dot_nt(a, b):
    return lax.dot_general(a, b, (((1,), (1,)), ((), ())), preferred_element_type=F32)


def _dot_tn(a, b):
    return lax.dot_general(a, b, (((0,), (0,)), ((), ())), preferred_element_type=F32)


class Job:
    def __init__(self, inputs, out_shapes, aliases, sems, start, middle, finish):
        self.inputs, self.out_shapes, self.aliases, self.sems = inputs, out_shapes, aliases, sems
        self.start, self.middle, self.finish = start, middle, finish


def _place():
    x, y, c = lax.axis_index("x"), lax.axis_index("y"), lax.axis_index("c")
    chips = [(1 - x, y), (x, 1 - y), (1 - x, 1 - y)]
    return x, y, c, chips


def _ds(start, size, lane):
    if not isinstance(start, int):
        start = pl.multiple_of(start, 128 if lane else 16)
    return pl.ds(start, size)


WHOLE = (0, 1, 1)


def _window(ref, axis, shape, blk=None, half=None, sub=WHOLE, within=WHOLE):
    n = shape[axis] // N_CHIPS
    hs = shape[1 - axis] // 2
    idx = [slice(None), slice(None)]
    if blk is not None:
        b_first, b_count, b_pieces = within
        b_ext = n // b_pieces
        idx[axis] = _ds(blk * n + b_first * b_ext, b_count * b_ext, axis == 1)
    first, count, pieces = sub
    ext = hs // pieces
    if half is not None:
        idx[1 - axis] = _ds(half * hs + first * ext, count * ext, axis == 0)
    elif pieces > 1:
        idx[1 - axis] = _ds(first * ext, count * ext, axis == 0)
    return ref.at[tuple(idx)]


def _remote(src, dst, send_sem, recv_sem, dev):
    return pltpu.make_async_remote_copy(src_ref=src, dst_ref=dst, send_sem=send_sem, recv_sem=recv_sem,
                                        device_id=dev, device_id_type=MESH)


def _full_shape(block_shape, axis):
    out = list(block_shape)
    out[axis] *= N_CHIPS
    return tuple(out)


def _half_all(shape, axis):
    out = list(shape)
    out[1 - axis] //= 2
    return tuple(out)


def _block(shape, axis):
    out = list(shape)
    out[axis] //= N_CHIPS
    return tuple(out)


def _half_block(shape, axis):
    return _half_all(_block(shape, axis), axis)


def gather_job(items):
    nw = len(items)
    shapes = [item[0].shape for item in items]
    n_sem = 8

    def parts(sub):
        first, count, pieces = sub
        return (2 * first, count, 2 * pieces), (2 * first + count, count, 2 * pieces)

    def start(pos, ins, outs, sems):
        x, y, c, chips = pos
        for w, (_, ax, sub, within) in enumerate(items):
            mine = _window(outs[w], ax, shapes[w], blk=2 * x + y, half=c, sub=sub, within=within)
            for j in range(2):
                _remote(mine, mine, sems[0].at[n_sem * w + j], sems[1].at[n_sem * w + j], (*chips[j], c)).start()

    def middle(pos, ins, outs, sems):
        x, y, c, chips = pos
        for w, (_, ax, sub, within) in enumerate(items):
            for j in range(2):
                cx, cy = chips[j]
                landed = _window(outs[w], ax, shapes[w], blk=2 * cx + cy, half=c, sub=sub, within=within)
                _remote(landed, landed, sems[0].at[n_sem * w + j], sems[1].at[n_sem * w + j], (cx, cy, c)).wait_recv()
                part = _window(outs[w], ax, shapes[w], blk=2 * cx + cy, half=c, sub=parts(sub)[j], within=within)
                _remote(part, part, sems[0].at[n_sem * w + 2 + j], sems[1].at[n_sem * w + 2 + j],
                        (*chips[1 - j], c)).start()
                _remote(landed, landed, sems[0].at[n_sem * w + 4 + j], sems[1].at[n_sem * w + 4 + j],
                        (x, y, 1 - c)).start()

    def finish(pos, ins, outs, sems):
        x, y, c, chips = pos
        sib = (x, y, 1 - c)
        for w, (_, ax, sub, within) in enumerate(items):
            dx, dy = chips[2]
            for j in range(2):
                part = _window(outs[w], ax, shapes[w], blk=2 * dx + dy, half=c, sub=parts(sub)[j], within=within)
                cp = _remote(part, part, sems[0].at[n_sem * w + 2 + j], sems[1].at[n_sem * w + 2 + j], sib)
                cp.wait_recv()
                cp.wait_send()
            diag = _window(outs[w], ax, shapes[w], blk=2 * dx + dy, half=c, sub=sub, within=within)
            _remote(diag, diag, sems[0].at[n_sem * w + 6], sems[1].at[n_sem * w + 6], sib).start()
        for w, (_, ax, sub, within) in enumerate(items):
            for j, (cx, cy) in enumerate(chips):
                passed = _window(outs[w], ax, shapes[w], blk=2 * cx + cy, half=1 - c, sub=sub, within=within)
                cp = _remote(passed, passed, sems[0].at[n_sem * w + 4 + j], sems[1].at[n_sem * w + 4 + j], sib)
                cp.wait_recv()
                cp.wait_send()
            mine = _window(outs[w], ax, shapes[w], blk=2 * x + y, half=c, sub=sub, within=within)
            for j in range(2):
                _remote(mine, mine, sems[0].at[n_sem * w + j], sems[1].at[n_sem * w + j], sib).wait_send()

    sems = [pltpu.SemaphoreType.DMA((n_sem * nw,)), pltpu.SemaphoreType.DMA((n_sem * nw,))]
    return Job([item[0] for item in items], [_sds(item[0].shape, item[0].dtype) for item in items],
               {w: w for w in range(nw)}, sems, start, middle, finish)


def by_sequencer(name, job, peers, collective_id):
    ins = [jax.new_ref(a, memory_space=pltpu.MemorySpace.HBM) for a in job.inputs]
    from_input = {o: i for i, o in job.aliases.items()}
    outs = [ins[from_input[k]] if k in from_input else jax.empty_ref(s, memory_space=pltpu.MemorySpace.HBM)
            for k, s in enumerate(job.out_shapes)]

    @pl.kernel(mesh=plsc.ScalarSubcoreMesh(axis_name="sequencer", num_cores=1), name=name,
               scratch_types=tuple(job.sems), compiler_params=pltpu.CompilerParams(collective_id=collective_id))
    def launch(*sems):
        pos = _place()
        x, y, c, chips = pos
        devs = {"sibling": [(x, y, 1 - c)],
                "chips": [(cx, cy, c) for cx, cy in chips],
                "gather": [(*chips[0], c), (*chips[1], c), (x, y, 1 - c)],
                "all": [(px, py, pc) for px in (x, 1 - x) for py in (y, 1 - y) for pc in (c, 1 - c)][1:]}[peers]
        barrier = pltpu.get_barrier_semaphore()
        for dev in devs:
            pl.semaphore_signal(barrier, inc=1, device_id=dev, device_id_type=MESH)
        pl.semaphore_wait(barrier, len(devs))
        for phase in (job.start, job.middle, job.finish):
            if phase is not None:
                phase(pos, ins, outs, list(sems))

    launch()
    kept = [r[...] for r in ins]
    return kept, [kept[from_input[k]] if k in from_input else r[...] for k, r in enumerate(outs)]


def pair_job(grads, axes, is_half=False):
    nw = len(grads)
    shapes = [g.shape for g in grads]

    def start(pos, ins, outs, sems):
        x, y, c, _ = pos
        for w in range(nw):
            src = ins[w] if is_half else _window(ins[w], axes[w], shapes[w], half=1 - c)
            _remote(src, outs[w], sems[0].at[w], sems[1].at[w], (x, y, 1 - c)).start()

    def finish(pos, ins, outs, sems):
        x, y, c, _ = pos
        for w in range(nw):
            cp = _remote(outs[w], outs[w], sems[0].at[w], sems[1].at[w], (x, y, 1 - c))
            cp.wait_recv()
            cp.wait_send()

    sems = [pltpu.SemaphoreType.DMA((nw,)), pltpu.SemaphoreType.DMA((nw,))]
    out_shapes = [_sds(s if is_half else _half_all(s, a), BF16) for s, a in zip(shapes, axes)]
    return Job(list(grads), out_shapes, {}, sems, start, None, finish)


def cross_job(items):
    nw = len(items)
    inputs, aliases = [], {}
    for w, (part, ax, shape, prev, sub) in enumerate(items):
        inputs.append(part)
        if prev is not None:
            aliases[len(inputs)] = w
            inputs.append(prev)

    def copies(pos, ins, outs, sems):
        x, y, c, chips = pos
        k = 0
        for w, (_, ax, shape, prev, sub) in enumerate(items):
            src = ins[k]
            k += 2 if prev is not None else 1
            for j, (cx, cy) in enumerate(chips):
                slot = _window(outs[w].at[j], ax, shape, sub=sub)
                yield (_remote(_window(src, ax, shape, blk=2 * cx + cy, sub=sub), slot,
                               sems[0].at[3 * w + j], sems[1].at[3 * w + j], (cx, cy, c)),
                       _remote(slot, slot, sems[0].at[3 * w + j], sems[1].at[3 * w + j], (cx, cy, c)))

    def start(pos, ins, outs, sems):
        for send, _ in copies(pos, ins, outs, sems):
            send.start()

    def finish(pos, ins, outs, sems):
        for send, recv in copies(pos, ins, outs, sems):
            recv.wait_recv()
            send.wait_send()

    sems = [pltpu.SemaphoreType.DMA((3 * nw,)), pltpu.SemaphoreType.DMA((3 * nw,))]
    out_shapes = [_sds((3,) + _half_block(shape, ax), BF16) for _, ax, shape, _, _ in items]
    return Job(inputs, out_shapes, aliases, sems, start, None, finish)


def final_job(blocks, axes, shapes):
    nw = len(blocks)

    def start(pos, ins, outs, sems):
        x, y, c, _ = pos
        for w in range(nw):
            mine = _window(outs[w], axes[w], shapes[w], half=c)
            _remote(mine, mine, sems[0].at[w], sems[1].at[w], (x, y, 1 - c)).start()

    def finish(pos, ins, outs, sems):
        x, y, c, _ = pos
        for w in range(nw):
            theirs = _window(outs[w], axes[w], shapes[w], half=1 - c)
            cp = _remote(theirs, theirs, sems[0].at[w], sems[1].at[w], (x, y, 1 - c))
            cp.wait_recv()
            cp.wait_send()

    sems = [pltpu.SemaphoreType.DMA((nw,)), pltpu.SemaphoreType.DMA((nw,))]
    return Job(list(blocks), [_sds(b.shape, b.dtype) for b in blocks], {w: w for w in range(nw)}, sems, start, None,
               finish)


def stack_job(small):
    def peers(pos):
        x, y, c, _ = pos
        for k in range(1, 8):
            yield k - 1, (1 - x if k & 4 else x, 1 - y if k & 2 else y, 1 - c if k & 1 else c)

    def start(pos, ins, outs, sems):
        x, y, c, _ = pos
        mine = outs[0].at[4 * x + 2 * y + c]
        pltpu.make_async_copy(ins[0], mine, sems[2]).start()
        for k, dev in peers(pos):
            _remote(ins[0], mine, sems[0].at[k], sems[1].at[k], dev).start()

    def finish(pos, ins, outs, sems):
        x, y, c, _ = pos
        for k, (px, py, pc) in peers(pos):
            slot = outs[0].at[4 * px + 2 * py + pc]
            cp = _remote(slot, slot, sems[0].at[k], sems[1].at[k], (px, py, pc))
            cp.wait_recv()
            cp.wait_send()
        pltpu.make_async_copy(ins[0], outs[0].at[4 * x + 2 * y + c], sems[2]).wait()

    sems = [pltpu.SemaphoreType.DMA((7,)), pltpu.SemaphoreType.DMA((7,)), pltpu.SemaphoreType.DMA]
    return Job([small], [_sds((8,) + small.shape, small.dtype)], {}, sems, start, None, finish)


def columns_job(block):
    cols = block.shape[1]
    place = lambda out, b: out.at[:, _ds(b * cols, cols, True)]

    def start(pos, ins, outs, sems):
        x, y, c, chips = pos
        pltpu.make_async_copy(ins[0], place(outs[0], 2 * x + y), sems[2]).start()
        for j, (cx, cy) in enumerate(chips):
            _remote(ins[0], place(outs[0], 2 * x + y), sems[0].at[j], sems[1].at[j], (cx, cy, c)).start()

    def finish(pos, ins, outs, sems):
        x, y, c, chips = pos
        for j, (cx, cy) in enumerate(chips):
            got = place(outs[0], 2 * cx + cy)
            cp = _remote(got, got, sems[0].at[j], sems[1].at[j], (cx, cy, c))
            cp.wait_recv()
            cp.wait_send()
        pltpu.make_async_copy(ins[0], place(outs[0], 2 * x + y), sems[2]).wait()

    sems = [pltpu.SemaphoreType.DMA((3,)), pltpu.SemaphoreType.DMA((3,)), pltpu.SemaphoreType.DMA]
    return Job([block], [_sds((block.shape[0], N_CHIPS * cols), block.dtype)], {}, sems, start, None, finish)


def _call(name, body, grid, in_specs, out_specs, out_shape, args, scratch=(), sem=None, jobs=(), place=None,
          carried=None):
    n_in, n_out, n_sc = len(args), len(out_shape), len(scratch)
    carried = dict(carried or {})

    def launch(fn, in_specs, out_specs, out_shape, scratch, aliases, sem, operands):
        if place is None:
            return pl.pallas_call(
                fn, name=name, grid=grid, in_specs=in_specs, out_specs=out_specs, out_shape=out_shape,
                scratch_shapes=scratch, input_output_aliases=aliases, compiler_params=_params(sem))(*operands)
        spec = pltpu.PrefetchScalarGridSpec(num_scalar_prefetch=1, grid=grid, in_specs=in_specs,
                                            out_specs=out_specs, scratch_shapes=scratch)
        return pl.pallas_call(
            lambda p_ref, *refs: fn(*refs), name=name, grid_spec=spec, out_shape=out_shape,
            input_output_aliases={k + 1: v for k, v in aliases.items()}, compiler_params=_params(sem),
        )(place, *operands)

    if not jobs:
        outs = launch(body, list(in_specs), list(out_specs), list(out_shape), list(scratch), carried, sem, args)
        return list(outs), []

    total = math.prod(grid) if grid else 1
    mid = min(total - 1, (2 * total) // 3)

    def split(refs, start, counts):
        out = []
        for n in counts:
            out.append(refs[start:start + n])
            start += n
        return out, start

    def wrapped(*refs):
        c_in = refs[:n_in]
        j_ins, p = split(refs, n_in, [len(j.inputs) for j in jobs])
        c_out = refs[p:p + n_out]
        j_outs, p = split(refs, p + n_out, [len(j.out_shapes) for j in jobs])
        c_sc = refs[p:p + n_sc]
        j_sems, p = split(refs, p + n_sc, [len(j.sems) for j in jobs])
        pos = _place()
        step = 0
        for axis, extent in enumerate(grid):
            step = step * extent + pl.program_id(axis)

        def run(phase):
            for j, ins, outs, sems in zip(jobs, j_ins, j_outs, j_sems):
                fn = getattr(j, phase)
                if fn is not None:
                    fn(pos, ins, outs, sems)

        if total == 1:
            run("start")
            body(*c_in, *c_out, *c_sc)
            run("middle")
            run("finish")
            return
        pl.when(step == 0)(lambda: run("start"))
        body(*c_in, *c_out, *c_sc)
        if any(j.middle is not None for j in jobs):
            pl.when(step == mid)(lambda: run("middle"))
        pl.when(step == total - 1)(lambda: run("finish"))

    aliases, in_at, out_at = carried, n_in, n_out
    for j in jobs:
        for src, dst in j.aliases.items():
            aliases[in_at + src] = out_at + dst
        in_at += len(j.inputs)
        out_at += len(j.out_shapes)
    outs = launch(
        wrapped, list(in_specs) + [ANY] * (in_at - n_in), list(out_specs) + [ANY] * (out_at - n_out),
        list(out_shape) + [s for j in jobs for s in j.out_shapes],
        list(scratch) + [s for j in jobs for s in j.sems], aliases, ("arbitrary",) * len(grid),
        [*args, *[a for j in jobs for a in j.inputs]])
    job_outs, p = split(outs, n_out, [len(j.out_shapes) for j in jobs])
    return list(outs[:n_out]), [list(o) for o in job_outs]


def comm_only(name, jobs):
    def body(dummy_ref, out_ref):
        out_ref[...] = dummy_ref[...]

    dummy = jnp.zeros((8, 128), F32)
    spec = pl.BlockSpec((8, 128), lambda: (0, 0))
    return _call(name, body, (), [spec], [spec], [_sds((8, 128), F32)], [dummy], jobs=jobs)[1]


def _ret(outs, job_outs, jobs, single=True):
    res = outs[0] if single else outs
    return (res, job_outs) if jobs else res


def rmsnorm_fwd(name, x, g, jobs=()):
    s, d = x.shape
    tm = _tile(s, 512, 8)

    def body(x_ref, g_ref, n_ref, r_ref):
        xv = x_ref[...]
        r = lax.rsqrt(jnp.mean(xv * xv, axis=-1, keepdims=True) + EPS)
        n_ref[...] = (xv * r * g_ref[...]).astype(BF16)
        r_ref[...] = r

    row = lambda i: (i, 0)
    outs, job_outs = _call(
        name, body, (s // tm,),
        [pl.BlockSpec((tm, d), row), pl.BlockSpec((1, d), lambda i: (0, 0))],
        [pl.BlockSpec((tm, d), row), pl.BlockSpec((tm, 1), row)],
        [_sds((s, d), BF16), _sds((s, 1), F32)], [x, g], sem=("arbitrary",), jobs=jobs)
    return _ret(outs, job_outs, jobs, single=False)


def rmsnorm_bwd(name, dn, x, r, g, dh_in, jobs=()):
    s, d = x.shape
    tm = _tile(s, 512, 8)

    def body(dn_ref, x_ref, r_ref, g_ref, dh_ref, out_ref, outb_ref, dg_ref):
        i = pl.program_id(0)
        xh = x_ref[...] * r_ref[...]
        dnv = dn_ref[...]
        dxh = dnv * g_ref[...]
        dx = r_ref[...] * (dxh - xh * jnp.mean(dxh * xh, axis=-1, keepdims=True))
        out = dh_ref[...] + dx
        out_ref[...] = out
        outb_ref[...] = out.astype(BF16)
        part = jnp.sum(dnv * xh, axis=0, keepdims=True)

        @pl.when(i == 0)
        def _():
            dg_ref[...] = part

        @pl.when(i > 0)
        def _():
            dg_ref[...] += part

    row = lambda i: (i, 0)
    fixed = lambda i: (0, 0)
    outs, job_outs = _call(
        name, body, (s // tm,),
        [pl.BlockSpec((tm, d), row), pl.BlockSpec((tm, d), row), pl.BlockSpec((tm, 1), row),
         pl.BlockSpec((1, d), fixed), pl.BlockSpec((tm, d), row)],
        [pl.BlockSpec((tm, d), row), pl.BlockSpec((tm, d), row), pl.BlockSpec((1, d), fixed)],
        [_sds((s, d), F32), _sds((s, d), BF16), _sds((1, d), F32)], [dn, x, r, g, dh_in],
        sem=("arbitrary",), jobs=jobs)
    return _ret(outs, job_outs, jobs, single=False)


def gain_grad(name, dn_a, dn_b, x, r):
    s, d = x.shape
    tm = _tile(s, 512, 8)

    def body(a_ref, b_ref, x_ref, r_ref, dg_ref):
        i = pl.program_id(0)
        part = jnp.sum((a_ref[...] + b_ref[...]) * (x_ref[...] * r_ref[...]), axis=0, keepdims=True)

        @pl.when(i == 0)
        def _():
            dg_ref[...] = part

        @pl.when(i > 0)
        def _():
            dg_ref[...] += part

    row = lambda i: (i, 0)
    return _call(
        name, body, (s // tm,),
        [pl.BlockSpec((tm, d), row), pl.BlockSpec((tm, d), row), pl.BlockSpec((tm, d), row),
         pl.BlockSpec((tm, 1), row)],
        [pl.BlockSpec((1, d), lambda i: (0, 0))], [_sds((1, d), F32)], [dn_a, dn_b, x, r],
        sem=("arbitrary",))[0][0]


def loss_head(name, h, g, target):
    s, d = h.shape
    tm = _tile(s, 512, 8)
    nsteps = s // tm

    def body(h_ref, g_ref, t_ref, loss_ref, dh_ref, dhb_ref, dg_ref, sq_ref):
        i = pl.program_id(0)
        hv = h_ref[...]
        gv = g_ref[...]
        r = lax.rsqrt(jnp.mean(hv * hv, axis=-1, keepdims=True) + EPS)
        xh = hv * r
        err = xh * gv - t_ref[...]
        dy = err * (1.0 / d)
        dxh = dy * gv
        dh = r * (dxh - xh * jnp.mean(dxh * xh, axis=-1, keepdims=True))
        dh_ref[...] = dh
        dhb_ref[...] = dh.astype(BF16)
        dg_part = jnp.sum(dy * xh, axis=0, keepdims=True)
        sq_part = jnp.sum(err * err, axis=0, keepdims=True)

        @pl.when(i == 0)
        def _():
            dg_ref[...] = dg_part
            sq_ref[...] = sq_part

        @pl.when(i > 0)
        def _():
            dg_ref[...] += dg_part
            sq_ref[...] += sq_part

        @pl.when(i == nsteps - 1)
        def _():
            total = jnp.sum(sq_ref[...], axis=-1, keepdims=True) * (0.5 / d)
            loss_ref[...] = jnp.broadcast_to(total, loss_ref.shape)

    row = lambda i: (i, 0)
    fixed = lambda i: (0, 0)
    return _call(
        name, body, (nsteps,),
        [pl.BlockSpec((tm, d), row), pl.BlockSpec((1, d), fixed), pl.BlockSpec((tm, d), row)],
        [pl.BlockSpec((8, 128), fixed), pl.BlockSpec((tm, d), row), pl.BlockSpec((tm, d), row),
         pl.BlockSpec((1, d), fixed)],
        [_sds((8, 128), F32), _sds((s, d), F32), _sds((s, d), BF16), _sds((1, d), F32)], [h, g, target],
        scratch=[pltpu.VMEM((1, d), F32)], sem=("arbitrary",))[0]


def _mm(name, grid, in_arrays, in_specs, out_shapes, out_specs, acc_tile, dot, epilogue, jobs=(), place=None):
    nk = grid[2]
    n_in = len(in_arrays)
    n_out = len(out_shapes)

    def body(*refs):
        ins, outs = refs[:n_in], refs[n_in:n_in + n_out]
        if nk == 1:
            epilogue(dot(*ins), ins, outs)
            return
        acc = refs[n_in + n_out]
        k = pl.program_id(2)

        @pl.when(k == 0)
        def _():
            acc[...] = dot(*ins)

        @pl.when(jnp.logical_and(k > 0, k < nk - 1))
        def _():
            acc[...] += dot(*ins)

        @pl.when(k == nk - 1)
        def _():
            epilogue(acc[...] + dot(*ins), ins, outs)

    scratch = [pltpu.VMEM(acc_tile, F32)] if nk > 1 else []
    outs, job_outs = _call(name, body, grid, in_specs, out_specs, out_shapes, in_arrays, scratch=scratch,
                           sem=("parallel", "parallel", "arbitrary"), jobs=jobs, place=place)
    return _ret(outs, job_outs, jobs)


def _store(scale, dtype):
    def epilogue(acc, ins, outs):
        outs[0][...] = (acc * scale if scale != 1.0 else acc).astype(dtype)
    return epilogue


def mm_nn(name, a, w, out_dtype, tm=1024, tn=1024, tk=2048, jobs=()):
    m, kd = a.shape
    n = w.shape[1]
    tm, tn, tk = _tile(m, tm, 8), _tile(n, tn), _tile(kd, tk)
    return _mm(
        name, (n // tn, m // tm, kd // tk), [a, w],
        [pl.BlockSpec((tm, tk), lambda j, i, k: (i, k)), pl.BlockSpec((tk, tn), lambda j, i, k: (k, j))],
        [_sds((m, n), out_dtype)], [pl.BlockSpec((tm, tn), lambda j, i, k: (i, j))], (tm, tn),
        lambda a_ref, w_ref: _dot_nn(a_ref[...], w_ref[...]), _store(1.0, out_dtype), jobs)


def mm_nn_resid(name, a, w, x, scale, tm=1024, tn=1024, tk=1408, jobs=()):
    m, kd = a.shape
    n = w.shape[1]
    tm, tn, tk = _tile(m, tm, 8), _tile(n, tn), _tile(kd, tk)

    def epilogue(acc, ins, outs):
        outs[0][...] = ins[2][...] + scale * acc

    return _mm(
        name, (n // tn, m // tm, kd // tk), [a, w, x],
        [pl.BlockSpec((tm, tk), lambda j, i, k: (i, k)), pl.BlockSpec((tk, tn), lambda j, i, k: (k, j)),
         pl.BlockSpec((tm, tn), lambda j, i, k: (i, j))],
        [_sds((m, n), F32)], [pl.BlockSpec((tm, tn), lambda j, i, k: (i, j))], (tm, tn),
        lambda a_ref, w_ref, x_ref: _dot_nn(a_ref[...], w_ref[...]), epilogue, jobs)


def mm_nt(name, a, w, out_dtype, scale=1.0, tm=1024, tn=1024, tk=2048, jobs=()):
    m, kd = a.shape
    n = w.shape[0]
    tm, tn, tk = _tile(m, tm, 8), _tile(n, tn), _tile(kd, tk)
    return _mm(
        name, (n // tn, m // tm, kd // tk), [a, w],
        [pl.BlockSpec((tm, tk), lambda j, i, k: (i, k)), pl.BlockSpec((tn, tk), lambda j, i, k: (j, k))],
        [_sds((m, n), out_dtype)], [pl.BlockSpec((tm, tn), lambda j, i, k: (i, j))], (tm, tn),
        lambda a_ref, w_ref: _dot_nt(a_ref[...], w_ref[...]), _store(scale, out_dtype), jobs)


def mm_nt_pair(name, a3, w, out_dtype, tm=1024, tn=1024, tk=2816, jobs=()):
    _, m, f = a3.shape
    n = w.shape[0]
    tm, tn, tk = _tile(m, tm, 8), _tile(n, tn), _tile(f, tk)
    nkf = f // tk
    return _mm(
        name, (n // tn, m // tm, 2 * nkf), [a3, w],
        [pl.BlockSpec((None, tm, tk), lambda j, i, k: (k // nkf, i, k % nkf)),
         pl.BlockSpec((tn, tk), lambda j, i, k: (j, k))],
        [_sds((m, n), out_dtype)], [pl.BlockSpec((tm, tn), lambda j, i, k: (i, j))], (tm, tn),
        lambda a_ref, w_ref: _dot_nt(a_ref[...], w_ref[...]), _store(1.0, out_dtype), jobs)


def mm_nt_pair_halves(name, a3, w_left, w_right, out_dtype, tm=1024, tn=1024, jobs=()):
    _, m, f = a3.shape
    n = w_left.shape[0]
    w = w_left.shape[1] // N_CHIPS
    tm, tn = _tile(m, tm, 8), _tile(n, tn)
    per_half = f // w

    def dot(a_ref, l_ref, r_ref):
        right = pl.program_id(2) % 2 == 1
        return _dot_nt(a_ref[...], jnp.where(right, r_ref[...], l_ref[...]))

    half = pl.BlockSpec((tn, w), lambda j, i, k: (j, k // 2))
    return _mm(
        name, (n // tn, m // tm, 2 * N_CHIPS), [a3, w_left, w_right],
        [pl.BlockSpec((None, tm, w), lambda j, i, k: (k // per_half, i, k % per_half)), half, half],
        [_sds((m, n), out_dtype)], [pl.BlockSpec((tm, tn), lambda j, i, k: (i, j))], (tm, tn),
        dot, _store(1.0, out_dtype), jobs)


def mm_nt_norm_bwd(name, a, w, x, r, g, dh_in, tm=512, tk=1408, jobs=()):
    pair = a.ndim == 3
    m, kd = a.shape[-2], a.shape[-1]
    d = w.shape[0]
    tm, tk = _tile(m, tm, 8), _tile(kd, tk)
    nkf = kd // tk
    nk = 2 * nkf if pair else nkf
    if pair:
        a_spec = pl.BlockSpec((None, tm, tk), lambda i, k: (k // nkf, i, k % nkf))
    else:
        a_spec = pl.BlockSpec((tm, tk), lambda i, k: (i, k))
    row = lambda i, k: (i, 0)
    fixed = lambda i, k: (0, 0)

    def body(a_ref, w_ref, x_ref, r_ref, g_ref, dh_ref, out_ref, outb_ref, dg_ref, *acc):
        i, k = pl.program_id(0), pl.program_id(1)
        dot = lambda: _dot_nt(a_ref[...], w_ref[...])

        def finish(dn):
            xh = x_ref[...] * r_ref[...]
            dxh = dn * g_ref[...]
            out = dh_ref[...] + r_ref[...] * (dxh - xh * jnp.mean(dxh * xh, axis=-1, keepdims=True))
            out_ref[...] = out
            outb_ref[...] = out.astype(BF16)
            part = jnp.sum(dn * xh, axis=0, keepdims=True)

            @pl.when(i == 0)
            def _():
                dg_ref[...] = part

            @pl.when(i > 0)
            def _():
                dg_ref[...] += part

        if nk == 1:
            finish(dot())
            return

        @pl.when(k == 0)
        def _():
            acc[0][...] = dot()

        @pl.when(jnp.logical_and(k > 0, k < nk - 1))
        def _():
            acc[0][...] += dot()

        @pl.when(k == nk - 1)
        def _():
            finish(acc[0][...] + dot())

    outs, job_outs = _call(
        name, body, (m // tm, nk),
        [a_spec, pl.BlockSpec((d, tk), lambda i, k: (0, k)), pl.BlockSpec((tm, d), row), pl.BlockSpec((tm, 1), row),
         pl.BlockSpec((1, d), fixed), pl.BlockSpec((tm, d), row)],
        [pl.BlockSpec((tm, d), row), pl.BlockSpec((tm, d), row), pl.BlockSpec((1, d), fixed)],
        [_sds((m, d), F32), _sds((m, d), BF16), _sds((1, d), F32)], [a, w, x, r, g, dh_in],
        scratch=[pltpu.VMEM((tm, d), F32)] if nk > 1 else [], sem=("arbitrary", "arbitrary"), jobs=jobs)
    return _ret(outs, job_outs, jobs, single=False)


def mm_tn(name, a, b, out_dtype, scale=1.0, tm=1024, tn=1024, tk=4096, jobs=()):
    kd, m = a.shape
    n = b.shape[1]
    tm, tn, tk = _tile(m, tm), _tile(n, tn), _tile(kd, tk, 16)
    return _mm(
        name, (n // tn, m // tm, kd // tk), [a, b],
        [pl.BlockSpec((tk, tm), lambda j, i, k: (k, i)), pl.BlockSpec((tk, tn), lambda j, i, k: (k, j))],
        [_sds((m, n), out_dtype)], [pl.BlockSpec((tm, tn), lambda j, i, k: (i, j))], (tm, tn),
        lambda a_ref, b_ref: _dot_tn(a_ref[...], b_ref[...]), _store(scale, out_dtype), jobs)


def mm_tn_pair(name, a, b3, out_dtype, tm=1024, tn=512, tk=4096, jobs=()):
    kd, m = a.shape
    f = b3.shape[2]
    tm, tn, tk = _tile(m, tm), _tile(f, tn), _tile(kd, tk, 16)
    nf = f // tn
    return _mm(
        name, (m // tm, 2 * nf, kd // tk), [a, b3],
        [pl.BlockSpec((tk, tm), lambda i, j, k: (k, i)),
         pl.BlockSpec((None, tk, tn), lambda i, j, k: (j // nf, k, j % nf))],
        [_sds((m, 2 * f), out_dtype)], [pl.BlockSpec((tm, tn), lambda i, j, k: (i, j))], (tm, tn),
        lambda a_ref, b_ref: _dot_tn(a_ref[...], b_ref[...]), _store(1.0, out_dtype), jobs)


def mm_tn_pair_half(name, a, b3, out_dtype, place, mine, tm=1024, tn=512, tk=4096, jobs=()):
    kd, m = a.shape
    f = b3.shape[2]
    tm, tn, tk = _tile(m // 2, tm), _tile(f, tn), _tile(kd, tk, 16)
    nf, nbm = f // tn, m // 2 // tm
    which = (lambda p: p[1]) if mine else (lambda p: 1 - p[1])
    return _mm(
        name, (nbm, 2 * nf, kd // tk), [a, b3],
        [pl.BlockSpec((tk, tm), lambda i, j, k, p: (k, i + which(p) * nbm)),
         pl.BlockSpec((None, tk, tn), lambda i, j, k, p: (j // nf, k, j % nf))],
        [_sds((m // 2, 2 * f), out_dtype)], [pl.BlockSpec((tm, tn), lambda i, j, k, p: (i, j))], (tm, tn),
        lambda a_ref, b_ref: _dot_tn(a_ref[...], b_ref[...]), _store(1.0, out_dtype), jobs, place)


def swiglu_fwd(name, n, w_in, tm=1024, tn=512, jobs=(), stride=1, phase=0, prev=None, compact=False):
    s, d = n.shape
    f = w_in.shape[1] // 2 * (stride if compact else 1)
    tm, tn = _tile(s, tm, 8), _tile(f, tn)
    nf = f // tn
    col = lambda j: j * stride + phase
    w_gate = (lambda j: j) if compact else col
    w_up = (lambda j: j + nf // stride) if compact else (lambda j: col(j) + nf)

    def body(n_ref, wg_ref, wu_ref, *rest):
        gu_ref, a_ref = rest[-2:]
        nv = n_ref[...]
        g = _dot_nn(nv, wg_ref[...])
        u = _dot_nn(nv, wu_ref[...])
        gu_ref[0] = g.astype(BF16)
        gu_ref[1] = u.astype(BF16)
        a_ref[...] = (g * jax.nn.sigmoid(g) * u).astype(BF16)

    kept = list(prev) if prev is not None else []
    outs, job_outs = _call(
        name, body, (nf // stride, s // tm),
        [pl.BlockSpec((tm, d), lambda j, i: (i, 0)), pl.BlockSpec((d, tn), lambda j, i: (0, w_gate(j))),
         pl.BlockSpec((d, tn), lambda j, i: (0, w_up(j)))] + [ANY] * len(kept),
        [pl.BlockSpec((2, tm, tn), lambda j, i: (0, i, col(j))), pl.BlockSpec((tm, tn), lambda j, i: (i, col(j)))],
        [_sds((2, s, f), BF16), _sds((s, f), BF16)], [n, w_in, w_in] + kept, sem=("parallel", "parallel"),
        jobs=jobs, carried={3 + k: k for k in range(len(kept))})
    return _ret(outs, job_outs, jobs, single=False)


def swiglu_bwd(name, dh, w_out, gu, scale, tm=1024, tn=512, jobs=()):
    s, d = dh.shape
    f = w_out.shape[0]
    tm, tn = _tile(s, tm, 8), _tile(f, tn)

    sub = _tile(tm, 256, 8)

    def body(dh_ref, w_ref, gu_ref, out_ref):
        for lo in range(0, tm, sub):
            rows = slice(lo, lo + sub)
            da = (_dot_nt(dh_ref[rows, :], w_ref[...]) * scale).astype(BF16)
            g = gu_ref[0, rows, :]
            u = gu_ref[1, rows, :]
            sg = 0.5 * jnp.tanh(0.5 * g) + 0.5
            t = g * sg
            out_ref[0, rows, :] = da * (u * (sg + t * (1.0 - sg)))
            out_ref[1, rows, :] = da * t

    outs, job_outs = _call(
        name, body, (f // tn, s // tm),
        [pl.BlockSpec((tm, d), lambda j, i: (i, 0)), pl.BlockSpec((tn, d), lambda j, i: (j, 0)),
         pl.BlockSpec((2, tm, tn), lambda j, i: (0, i, j))],
        [pl.BlockSpec((2, tm, tn), lambda j, i: (0, i, j))],
        [_sds((2, s, f), BF16)], [dh, w_out, gu], sem=("parallel", "parallel"), jobs=jobs)
    return _ret(outs, job_outs, jobs)


HALO = 16


def _conv_inputs(z_ref, hgc_ref, hhc_ref, i, cw, tm):
    gc = z_ref[:, cw:2 * cw].astype(F32)
    hc = z_ref[:, 2 * cw:3 * cw].astype(F32)
    cin = gc * hc
    halo = hgc_ref[...].astype(F32) * hhc_ref[...].astype(F32) * (i > 0).astype(F32)
    row = lax.broadcasted_iota(jnp.int32, (tm, cw), 0)
    x1 = jnp.where(row == 0, halo[HALO - 1:HALO], pltpu.roll(cin, 1, 0))
    x2 = jnp.where(row == 0, halo[HALO - 2:HALO - 1], jnp.where(row == 1, halo[HALO - 1:HALO], pltpu.roll(cin, 2, 0)))
    return gc, hc, cin, x1, x2


def _tril(w):
    r = lax.broadcasted_iota(jnp.int32, w.shape, 0)
    c = lax.broadcasted_iota(jnp.int32, w.shape, 1)
    return jnp.where(r >= c, w, jnp.zeros_like(w))


def mixer_fwd(name, z, conv_w, conv_b, g_v, w_s, b_t, tm=256, jobs=()):
    s, zc = z.shape
    cw = conv_w.shape[1]
    gw = g_v.shape[1]
    heads = gw // GROUP
    tm = _tile(s, tm)
    hb = tm // HALO

    def body(z_ref, hgc_ref, hhc_ref, cw_ref, cb_ref, gv_ref, ws_ref, bt_ref, y_ref):
        i = pl.program_id(0)
        _, _, cin, x1, x2 = _conv_inputs(z_ref, hgc_ref, hhc_ref, i, cw, tm)
        cv = cb_ref[...] + cw_ref[2:3, :] * cin + cw_ref[1:2, :] * x1 + cw_ref[0:1, :] * x2
        y_ref[:, 0:cw] = (z_ref[:, 0:cw].astype(F32) * cv).astype(BF16)
        for h in range(heads):
            lo = h * GROUP
            vh = z_ref[:, 3 * cw + gw + lo:3 * cw + gw + lo + GROUP].astype(F32)
            rv = lax.rsqrt(jnp.mean(vh * vh, axis=-1, keepdims=True) + EPS)
            vn = (vh * rv * gv_ref[:, lo:lo + GROUP]).astype(BF16)
            w = _tril(ws_ref[h]).astype(BF16)
            for n in range(tm // GROUP):
                rows = slice(n * GROUP, (n + 1) * GROUP)
                sg = _dot_nn(w, vn[rows]) + bt_ref[:, h:h + 1]
                u = z_ref[rows, 3 * cw + lo:3 * cw + lo + GROUP].astype(F32)
                y_ref[rows, cw + lo:cw + lo + GROUP] = (u * sg).astype(BF16)

    fixed2 = lambda i: (0, 0)
    outs, job_outs = _call(
        name, body, (s // tm,),
        [pl.BlockSpec((tm, zc), lambda i: (i, 0)),
         pl.BlockSpec((HALO, cw), lambda i: (jnp.maximum(i * hb - 1, 0), 1)),
         pl.BlockSpec((HALO, cw), lambda i: (jnp.maximum(i * hb - 1, 0), 2)),
         pl.BlockSpec(conv_w.shape, fixed2), pl.BlockSpec(conv_b.shape, fixed2),
         pl.BlockSpec(g_v.shape, fixed2), pl.BlockSpec(w_s.shape, lambda i: (0, 0, 0)),
         pl.BlockSpec(b_t.shape, fixed2)],
        [pl.BlockSpec((tm, cw + gw), lambda i: (i, 0))], [_sds((s, cw + gw), BF16)],
        [z, z, z, conv_w, conv_b, g_v, w_s, b_t], sem=("arbitrary",), jobs=jobs)
    return _ret(outs, job_outs, jobs)


def mixer_bwd(name, z, dy, conv_w, conv_b, g_v, w_s, b_t, tm=256, jobs=()):
    s, zc = z.shape
    cw = conv_w.shape[1]
    gw = g_v.shape[1]
    heads = gw // GROUP
    tm = _tile(s, tm)
    hb = tm // HALO
    nsteps = s // tm
    last_halo = s // HALO - 1

    def body(z_ref, hgc_ref, hhc_ref, ngb_ref, dy_ref, ndy_ref, cw_ref, cb_ref, gv_ref, ws_ref, bt_ref,
             dz_ref, sm_ref, dws_ref, dbt_ref, dsg_ref):
        i = pl.program_id(0)

        @pl.when(i == 0)
        def _():
            sm_ref[...] = jnp.zeros_like(sm_ref)
            dws_ref[...] = jnp.zeros_like(dws_ref)
            dsg_ref[...] = jnp.zeros_like(dsg_ref)

        gc, hc, cin, x1, x2 = _conv_inputs(z_ref, hgc_ref, hhc_ref, i, cw, tm)
        w0, w1, w2 = cw_ref[0:1, :], cw_ref[1:2, :], cw_ref[2:3, :]
        cv = cb_ref[...] + w2 * cin + w1 * x1 + w0 * x2
        gb = z_ref[:, 0:cw].astype(F32)
        dyc = dy_ref[:, 0:cw].astype(F32)
        dz_ref[:, 0:cw] = (dyc * cv).astype(BF16)
        dcv = dyc * gb
        nxt = ndy_ref[...].astype(F32) * ngb_ref[...].astype(F32) * (i < nsteps - 1).astype(F32)
        row = lax.broadcasted_iota(jnp.int32, (tm, cw), 0)
        d1 = jnp.where(row == tm - 1, nxt[0:1], pltpu.roll(dcv, tm - 1, 0))
        d2 = jnp.where(row == tm - 1, nxt[1:2], jnp.where(row == tm - 2, nxt[0:1], pltpu.roll(dcv, tm - 2, 0)))
        dcin = w2 * dcv + w1 * d1 + w0 * d2
        dz_ref[:, cw:2 * cw] = (dcin * hc).astype(BF16)
        dz_ref[:, 2 * cw:3 * cw] = (dcin * gc).astype(BF16)
        sm_ref[0:1, :] += jnp.sum(dcv * x2, axis=0, keepdims=True)
        sm_ref[1:2, :] += jnp.sum(dcv * x1, axis=0, keepdims=True)
        sm_ref[2:3, :] += jnp.sum(dcv * cin, axis=0, keepdims=True)
        sm_ref[3:4, :] += jnp.sum(dcv, axis=0, keepdims=True)

        for h in range(heads):
            lo = h * GROUP
            vcol = slice(3 * cw + gw + lo, 3 * cw + gw + lo + GROUP)
            ucol = slice(3 * cw + lo, 3 * cw + lo + GROUP)
            vh = z_ref[:, vcol].astype(F32)
            rv = lax.rsqrt(jnp.mean(vh * vh, axis=-1, keepdims=True) + EPS)
            xh = vh * rv
            gvh = gv_ref[:, lo:lo + GROUP]
            vn = (xh * gvh).astype(BF16)
            w = _tril(ws_ref[h]).astype(BF16)
            dgv = jnp.zeros((1, GROUP), F32)
            for n in range(tm // GROUP):
                rows = slice(n * GROUP, (n + 1) * GROUP)
                sg = _dot_nn(w, vn[rows]) + bt_ref[:, h:h + 1]
                dyg = dy_ref[rows, cw + lo:cw + lo + GROUP].astype(F32)
                dsg = dyg * z_ref[rows, ucol].astype(F32)
                dz_ref[rows, ucol] = (dyg * sg).astype(BF16)
                dsgb = dsg.astype(BF16)
                dvn = _dot_tn(w, dsgb)
                dws_ref[h] += _dot_nt(dsgb, vn[rows])
                dsg_ref[:, lo:lo + GROUP] += dsg
                xhc = xh[rows]
                dgv = dgv + jnp.sum(dvn * xhc, axis=0, keepdims=True)
                dxh = dvn * gvh
                dv = rv[rows] * (dxh - xhc * jnp.mean(dxh * xhc, axis=-1, keepdims=True))
                dz_ref[rows, vcol] = dv.astype(BF16)
            sm_ref[4:5, lo:lo + GROUP] += dgv

        @pl.when(i == nsteps - 1)
        def _():
            for h in range(heads):
                dws_ref[h] = _tril(dws_ref[h])
                dbt_ref[:, h:h + 1] = jnp.sum(dsg_ref[:, h * GROUP:(h + 1) * GROUP], axis=-1, keepdims=True)

    fixed2 = lambda i: (0, 0)
    fixed3 = lambda i: (0, 0, 0)
    prev = lambda col: (lambda i: (jnp.maximum(i * hb - 1, 0), col))
    nxt_blk = lambda i: (jnp.minimum((i + 1) * hb, last_halo), 0)
    outs, job_outs = _call(
        name, body, (nsteps,),
        [pl.BlockSpec((tm, zc), lambda i: (i, 0)),
         pl.BlockSpec((HALO, cw), prev(1)), pl.BlockSpec((HALO, cw), prev(2)),
         pl.BlockSpec((HALO, cw), nxt_blk),
         pl.BlockSpec((tm, cw + gw), lambda i: (i, 0)), pl.BlockSpec((HALO, cw), nxt_blk),
         pl.BlockSpec(conv_w.shape, fixed2), pl.BlockSpec(conv_b.shape, fixed2),
         pl.BlockSpec(g_v.shape, fixed2), pl.BlockSpec(w_s.shape, fixed3), pl.BlockSpec(b_t.shape, fixed2)],
        [pl.BlockSpec((tm, zc), lambda i: (i, 0)), pl.BlockSpec((8, cw), fixed2),
         pl.BlockSpec(w_s.shape, fixed3), pl.BlockSpec(b_t.shape, fixed2)],
        [_sds((s, zc), BF16), _sds((8, cw), F32), _sds(w_s.shape, F32), _sds(b_t.shape, F32)],
        [z, z, z, z, dy, dy, conv_w, conv_b, g_v, w_s, b_t],
        scratch=[pltpu.VMEM((GROUP, gw), F32)], sem=("arbitrary",), jobs=jobs)
    return _ret(outs, job_outs, jobs, single=False)


def _softmax_rows(sc):
    e = jnp.exp(sc - jnp.max(sc, axis=-1, keepdims=True))
    return e / jnp.sum(e, axis=-1, keepdims=True)


def attn_fwd(name, q, k, v, tm=512, jobs=()):
    s, d = q.shape
    m = k.shape[0]
    hd = d // XA_HEADS
    scale = hd ** -0.5
    tm = _tile(s, tm, 8)

    def body(q_ref, k_ref, v_ref, o_ref):
        for h in range(XA_HEADS):
            cols = slice(h * hd, (h + 1) * hd)
            p = _softmax_rows(_dot_nt(q_ref[:, cols], k_ref[:, cols]) * scale)
            o_ref[:, cols] = _dot_nn(p.astype(BF16), v_ref[:, cols]).astype(BF16)

    outs, job_outs = _call(
        name, body, (s // tm,),
        [pl.BlockSpec((tm, d), lambda i: (i, 0)), pl.BlockSpec((m, d), lambda i: (0, 0)),
         pl.BlockSpec((m, d), lambda i: (0, 0))],
        [pl.BlockSpec((tm, d), lambda i: (i, 0))], [_sds((s, d), BF16)], [q, k, v], sem=("arbitrary",), jobs=jobs)
    return _ret(outs, job_outs, jobs)


def attn_bwd(name, q, k, v, do, tm=512):
    s, d = q.shape
    m = k.shape[0]
    hd = d // XA_HEADS
    scale = hd ** -0.5
    tm = _tile(s, tm, 8)

    def body(q_ref, k_ref, v_ref, do_ref, dq_ref, dk_ref, dv_ref):
        i = pl.program_id(0)

        @pl.when(i == 0)
        def _():
            dk_ref[...] = jnp.zeros_like(dk_ref)
            dv_ref[...] = jnp.zeros_like(dv_ref)

        for h in range(XA_HEADS):
            cols = slice(h * hd, (h + 1) * hd)
            qh = q_ref[:, cols]
            doh = do_ref[:, cols]
            p = _softmax_rows(_dot_nt(qh, k_ref[:, cols]) * scale)
            dp = _dot_nt(doh, v_ref[:, cols])
            ds = (p * (dp - jnp.sum(dp * p, axis=-1, keepdims=True)) * scale).astype(BF16)
            dq_ref[:, cols] = _dot_nn(ds, k_ref[:, cols]).astype(BF16)
            dk_ref[:, cols] += _dot_tn(ds, qh)
            dv_ref[:, cols] += _dot_tn(p.astype(BF16), doh)

    row = lambda i: (i, 0)
    fixed = lambda i: (0, 0)
    return _call(
        name, body, (s // tm,),
        [pl.BlockSpec((tm, d), row), pl.BlockSpec((m, d), fixed), pl.BlockSpec((m, d), fixed),
         pl.BlockSpec((tm, d), row)],
        [pl.BlockSpec((tm, d), row), pl.BlockSpec((m, d), fixed), pl.BlockSpec((m, d), fixed)],
        [_sds((s, d), BF16), _sds((m, d), F32), _sds((m, d), F32)], [q, k, v, do], sem=("arbitrary",))[0]


def _grid2(rows, cols, row_mult):
    tr, tc = _tile(rows, 512, row_mult), _tile(cols, 2048)
    return tr, tc, rows // tr, cols // tc


def cast_place(name, block, axis, place, column_half=None):
    r, c = block.shape
    if column_half is not None:
        c //= 2
    tr, tc, nbr, nbc = _grid2(r, c, 16)
    first = 0 if column_half is None else column_half * nbc
    if axis == 1:
        dst = lambda i, j, p: (i, j + p[0] * nbc)
    else:
        dst = lambda i, j, p: (i + p[0] * nbr, j)

    def body(p_ref, w_ref, out_ref):
        out_ref[...] = w_ref[...].astype(BF16)

    return pl.pallas_call(
        body, name=name,
        grid_spec=pltpu.PrefetchScalarGridSpec(
            num_scalar_prefetch=1, grid=(nbr, nbc),
            in_specs=[pl.BlockSpec((tr, tc), lambda i, j, p: (i, j + first))],
            out_specs=pl.BlockSpec((tr, tc), dst)),
        out_shape=_sds(_full_shape((r, c), axis), BF16),
        compiler_params=_params(("parallel", "parallel")),
    )(place, block)


def merge_column_halves(name, left, right):
    r, c = left.shape
    w = c // N_CHIPS
    tr = _tile(r, 512, 16)

    def body(l_ref, r_ref, out_ref):
        side = pl.program_id(1) % 2

        @pl.when(side == 0)
        def _():
            out_ref[...] = l_ref[...]

        @pl.when(side == 1)
        def _():
            out_ref[...] = r_ref[...]

    half = pl.BlockSpec((tr, w), lambda i, j: (i, j // 2))
    return _call(name, body, (r // tr, 2 * N_CHIPS), [half, half], [pl.BlockSpec((tr, w), lambda i, j: (i, j))],
                 [_sds((r, 2 * c), left.dtype)], [left, right], sem=("parallel", "arbitrary"))[0][0]


def pair_add(name, grad, peer, axis, place):
    hr, hc = peer.shape
    tr, tc, nbr, nbc = _grid2(hr, hc, 16)
    same = lambda i, j, p: (i, j)
    if grad.shape == peer.shape:
        mine = same
    elif axis == 1:
        mine = lambda i, j, p: (i + p[1] * nbr, j)
    else:
        mine = lambda i, j, p: (i, j + p[1] * nbc)

    def body(p_ref, g_ref, q_ref, out_ref):
        out_ref[...] = (g_ref[...].astype(F32) + q_ref[...].astype(F32)).astype(BF16)

    return pl.pallas_call(
        body, name=name,
        grid_spec=pltpu.PrefetchScalarGridSpec(
            num_scalar_prefetch=1, grid=(nbr, nbc),
            in_specs=[pl.BlockSpec((tr, tc), mine), pl.BlockSpec((tr, tc), same)],
            out_specs=pl.BlockSpec((tr, tc), same)),
        out_shape=_sds((hr, hc), BF16),
        compiler_params=_params(("parallel", "parallel")),
    )(place, grad, peer)


def cross_sum(name, part, land, axis, shape, place):
    _, sr, sc = land.shape
    tr, tc, nbr, nbc = _grid2(sr, sc, 16)
    if axis == 1:
        own = lambda i, j, p: (i, j + p[0] * nbc)
        dst = lambda i, j, p: (i + p[1] * nbr, j)
    else:
        own = lambda i, j, p: (i + p[0] * nbr, j)
        dst = lambda i, j, p: (i, j + p[1] * nbc)

    def body(p_ref, own_ref, land_ref, out_ref):
        out_ref[...] = ((own_ref[...].astype(F32) + land_ref[0].astype(F32))
                        + (land_ref[1].astype(F32) + land_ref[2].astype(F32)))

    return pl.pallas_call(
        body, name=name,
        grid_spec=pltpu.PrefetchScalarGridSpec(
            num_scalar_prefetch=1, grid=(nbr, nbc),
            in_specs=[pl.BlockSpec((tr, tc), own), pl.BlockSpec((3, tr, tc), lambda i, j, p: (0, i, j))],
            out_specs=pl.BlockSpec((tr, tc), dst)),
        out_shape=_sds(_block(shape, axis), F32),
        compiler_params=_params(("parallel", "parallel")),
    )(place, part, land)


def _adam_math(w, g, m, v):
    m = ADAM_B1 * m + (1.0 - ADAM_B1) * g
    v = ADAM_B2 * v + (1.0 - ADAM_B2) * (g * g)
    m_hat = m / (1.0 - ADAM_B1 ** ADAM_STEP)
    v_hat = v / (1.0 - ADAM_B2 ** ADAM_STEP)
    delta = -ADAM_LR * (m_hat / (jnp.sqrt(v_hat) + ADAM_EPS) + ADAM_WD * w)
    return delta, m, v


def adamw(name, w, g, m, v, jobs=()):
    r, c = w.shape
    tr, tc = _tile(r, 256, 8), _tile(c, 1408)

    def body(w_ref, g_ref, m_ref, v_ref, g_out, d_out, m_out, v_out):
        d, mm, vv = _adam_math(w_ref[...], g_ref[...], m_ref[...], v_ref[...])
        g_out[...] = g_ref[...]
        d_out[...] = d
        m_out[...] = mm
        v_out[...] = vv

    spec = pl.BlockSpec((tr, tc), lambda i, j: (i, j))
    outs, job_outs = _call(name, body, (r // tr, c // tc), [spec] * 4, [spec] * 4, [_sds((r, c), F32)] * 4,
                           [w, g, m, v], sem=("parallel", "parallel"), jobs=jobs)
    return _ret(outs, job_outs, jobs, single=False)


def small_sum(name, stacks):
    def body(*refs):
        for s_ref, out_ref in zip(refs[:len(stacks)], refs[len(stacks):]):
            acc = s_ref[0]
            for d in range(1, s_ref.shape[0]):
                acc = acc + s_ref[d]
            out_ref[...] = acc

    return pl.pallas_call(body, name=name, out_shape=[_sds(s.shape[1:], F32) for s in stacks])(*stacks)


WEIGHTS = ["g_ffn1", "w_ffn1_in", "w_ffn1_out", "g_mix", "w_mix_in", "conv_w", "conv_b", "g_gm_v", "w_spatial",
           "b_spatial", "w_mix_out", "g_xattn", "g_mem", "w_xq", "w_xk", "w_xv", "w_xo", "g_ffn2", "w_ffn2_in",
           "w_ffn2_out", "g_final"]
BIG = {"w_ffn1_in": 1, "w_ffn1_out": 0, "w_mix_in": 1, "w_mix_out": 0, "w_xq": 0, "w_xk": 0, "w_xv": 0, "w_xo": 0,
       "w_ffn2_in": 1, "w_ffn2_out": 0}
SMALL = [n for n in WEIGHTS if n not in BIG]
LATE_SMALL = ["g_ffn1"]
EARLY_SMALL = [n for n in SMALL if n not in LATE_SMALL]


def _pack(arrays):
    flat = jnp.concatenate([a.reshape(-1) for a in arrays])
    rows = -(-flat.shape[0] // 1024) * 8
    return jnp.pad(flat, (0, rows * 128 - flat.shape[0])).reshape(rows, 128)


def _unpack(buf, shapes):
    flat = buf.reshape(-1)
    out, pos = [], 0
    for shp in shapes:
        n = math.prod(shp)
        out.append(flat[pos:pos + n].reshape(shp))
        pos += n
    return out


def kernel(x, mem, g_ffn1, w_ffn1_in, w_ffn1_out, g_mix, w_mix_in, conv_w, conv_b, g_gm_v, w_spatial, b_spatial, w_mix_out, g_xattn, g_mem, w_xq, w_xk, w_xv, w_xo, g_ffn2, w_ffn2_in, w_ffn2_out, g_final, loss_target, m_g_ffn1, m_w_ffn1_in, m_w_ffn1_out, m_g_mix, m_w_mix_in, m_conv_w, m_conv_b, m_g_gm_v, m_w_spatial, m_b_spatial, m_w_mix_out, m_g_xattn, m_g_mem, m_w_xq, m_w_xk, m_w_xv, m_w_xo, m_g_ffn2, m_w_ffn2_in, m_w_ffn2_out, m_g_final, v_g_ffn1, v_w_ffn1_in, v_w_ffn1_out, v_g_mix, v_w_mix_in, v_conv_w, v_conv_b, v_g_gm_v, v_w_spatial, v_b_spatial, v_w_mix_out, v_g_xattn, v_g_mem, v_w_xq, v_w_xk, v_w_xv, v_w_xo, v_g_ffn2, v_w_ffn2_in, v_w_ffn2_out, v_g_final):
    given = dict(locals())
    wts = {n: given[n] for n in WEIGHTS}
    mom = {n: given["m_" + n] for n in WEIGHTS}
    var = {n: given["v_" + n] for n in WEIGHTS}

    xi, yi, ci = lax.axis_index("x"), lax.axis_index("y"), lax.axis_index("c")
    blk = 2 * xi + yi
    place = jnp.stack([blk, ci]).astype(jnp.int32)

    x2, mem2, tgt = x[0], mem[0], loss_target[0]
    w_s, b_t = w_spatial[0], b_spatial[0].T
    gf = g_final[None]

    rest = [n for n in BIG if n != "w_ffn1_in"]
    own = {n: cast_place("cast_" + n, wts[n][0], BIG[n], place) for n in rest}
    own_left = cast_place("cast_w_ffn1_in_left", wts["w_ffn1_in"][0], 1, place, column_half=0)
    own_right = cast_place("cast_w_ffn1_in_right", wts["w_ffn1_in"][0], 1, place, column_half=1)
    shape = {n: own[n].shape for n in rest}
    shape["w_ffn1_in"] = _full_shape(wts["w_ffn1_in"][0].shape, 1)
    full = {}

    def gather_now(name, arrays, axes, collective_id):
        job = gather_job([(a, ax, WHOLE, WHOLE) for a, ax in zip(arrays, axes)])
        return by_sequencer(name, job, "gather", collective_id)[1]

    _, (conv_taps,) = by_sequencer("gather_conv_taps", columns_job(jnp.pad(conv_w[0], ((0, 8 - CONV_K), (0, 0)))),
                                   "chips", 14)
    (w1_left,) = gather_now("gather_w_ffn1_in_left", [own_left], [1], 1)
    (w1_right,) = gather_now("gather_w_ffn1_in_right", [own_right], [1], 2)
    groups = [["w_ffn1_out"], ["w_mix_in", "w_mix_out"], ["w_xq", "w_xk", "w_xv", "w_xo"], ["w_ffn2_in"],
              ["w_ffn2_out"]]
    for g, names in enumerate(groups):
        got = gather_now("gather_" + "_".join(names), [own[n] for n in names], [BIG[n] for n in names], 3 + g)
        full.update(zip(names, got))

    half_cols = dict(tm=512, tn=shape["w_ffn1_in"][1] // (2 * N_CHIPS), stride=2, compact=True)
    n1, r1 = rmsnorm_fwd("norm1", x2, g_ffn1)
    halves = swiglu_fwd("ffn1_in_left", n1, w1_left, phase=0, **half_cols)
    gu1, a1 = swiglu_fwd("ffn1_in_right", n1, w1_right, phase=1, prev=halves, **half_cols)
    h1 = mm_nn_resid("ffn1_out", a1, full["w_ffn1_out"], x2, 0.5, tm=512, tk=5632)
    n2, r2 = rmsnorm_fwd("norm2", h1, g_mix)
    z = mm_nn("mix_in", n2, full["w_mix_in"], BF16)
    ycat = mixer_fwd("mixer", z, conv_taps, conv_b, g_gm_v, w_s, b_t)
    h2 = mm_nn_resid("mix_out", ycat, full["w_mix_out"], h1, 1.0, tk=2048)
    n3, r3 = rmsnorm_fwd("norm3", h2, g_xattn)
    mem2, h2 = lax.optimization_barrier((mem2, h2))
    mn, rm = rmsnorm_fwd("norm_mem", mem2, g_mem)
    q = mm_nn("xq", n3, full["w_xq"], BF16)
    k = mm_nn("xk", mn, full["w_xk"], BF16)
    v = mm_nn("xv", mn, full["w_xv"], BF16)
    o = attn_fwd("attn", q, k, v)
    h3 = mm_nn_resid("xo", o, full["w_xo"], h2, 1.0, tk=2048)
    n4, r4 = rmsnorm_fwd("norm4", h3, g_ffn2)
    gu2, a2 = swiglu_fwd("ffn2_in", n4, full["w_ffn2_in"])
    h4 = mm_nn_resid("ffn2_out", a2, full["w_ffn2_out"], h3, 0.5, tm=512, tk=5632)
    loss_blk, dh4, dh4b, dg_final = loss_head("loss_head", h4, gf, tgt)

    dw, part, land, half, grads = {}, {}, {}, {}, {}
    launched = []

    def send_pair(*names):
        return pair_job([dw[n] for n in names], [BIG[n] for n in names])

    def take_pair(names, got):
        for n, p in zip(names, got):
            part[n] = pair_add("pair_add_" + n, dw[n], p, BIG[n], place)

    def start_cross(*names):
        job = cross_job([(part[n], BIG[n], shape[n], None, WHOLE) for n in names])
        kept, landed = by_sequencer("cross_" + "_".join(names), job, "chips", 8 + len(launched) % 2)
        launched.append(names)
        for n, p, l in zip(names, kept, landed):
            part[n], land[n] = p, l

    def finish_cross(*names):
        for n in names:
            half[n] = cross_sum("cross_sum_" + n, part[n], land[n], BIG[n], shape[n], place)

    def send_final(*names):
        return final_job([half[n] for n in names], [BIG[n] for n in names], [shape[n] for n in names])

    delta, new_m, new_v = {}, {}, {}

    def take_final(names, got):
        for n, g in zip(names, got):
            grads[n], delta[n], new_m[n], new_v[n] = adamw("adamw_" + n, wts[n][0], g, mom[n][0], var[n][0])

    dgu2 = swiglu_bwd("ffn2_dact", dh4b, full["w_ffn2_out"], gu2, 0.5)
    dw["w_ffn2_in"] = mm_tn_pair("ffn2_dwin", n4, dgu2, BF16)
    dw["w_ffn2_out"], (got,) = mm_tn("ffn2_dwout", a2, dh4b, BF16, scale=0.5, jobs=[send_pair("w_ffn2_in")])
    take_pair(["w_ffn2_in"], got)
    start_cross("w_ffn2_in")
    (dh3, dh3b, dg_ffn2), (got,) = mm_nt_norm_bwd("ffn2_dn", dgu2, full["w_ffn2_in"], h3, r4, g_ffn2, dh4,
                                                  jobs=[send_pair("w_ffn2_out")])
    take_pair(["w_ffn2_out"], got)
    start_cross("w_ffn2_out")

    dw["w_xo"] = mm_tn("xo_dw", o, dh3b, BF16)
    finish_cross("w_ffn2_in")
    do, (got,) = mm_nt("xo_dx", dh3b, full["w_xo"], BF16, jobs=[send_final("w_ffn2_in")])
    take_final(["w_ffn2_in"], got)
    dq, dk, dv = attn_bwd("attn_bwd", q, k, v, do)
    dkb, dvb = dk.astype(BF16), dv.astype(BF16)
    dw["w_xq"] = mm_tn("xq_dw", n3, dq, BF16)
    dh2, dh2b, dg_xattn = mm_nt_norm_bwd("xq_dx", dq, full["w_xq"], h2, r3, g_xattn, dh3, tk=1024)
    dw["w_xk"] = mm_tn("xk_dw", mn, dkb, BF16)
    dw["w_xv"] = mm_tn("xv_dw", mn, dvb, BF16)
    dmn_k = mm_nt("xk_dx", dkb, full["w_xk"], F32)
    dmn_v = mm_nt("xv_dx", dvb, full["w_xv"], F32)
    dg_mem = gain_grad("norm_mem_bwd", dmn_k, dmn_v, mem2, rm)

    finish_cross("w_ffn2_out")
    dw["w_mix_out"], (got,) = mm_tn("mix_out_dw", ycat, dh2b, BF16, jobs=[send_final("w_ffn2_out")])
    take_final(["w_ffn2_out"], got)
    attn_names = ["w_xo", "w_xq", "w_xk", "w_xv", "w_mix_out"]
    dycat, (got,) = mm_nt("mix_out_dx", dh2b, full["w_mix_out"], BF16, jobs=[send_pair(*attn_names)])
    take_pair(attn_names, got)
    start_cross(*attn_names)
    dz, dsmall, dws, dbt = mixer_bwd("mixer_bwd", z, dycat, conv_taps, conv_b, g_gm_v, w_s, b_t)
    dw["w_mix_in"] = mm_tn("mix_in_dw", n2, dz, BF16)
    (dh1, dh1b, dg_mix), (got,) = mm_nt_norm_bwd("mix_in_dx", dz, full["w_mix_in"], h1, r2, g_mix, dh2, tk=1280,
                                                 jobs=[send_pair("w_mix_in")])
    take_pair(["w_mix_in"], got)
    start_cross("w_mix_in")

    finish_cross(*attn_names)
    dw["w_ffn1_out"], (got,) = mm_tn("ffn1_dwout", a1, dh1b, BF16, scale=0.5, jobs=[send_final(*attn_names)])
    take_final(attn_names, got)
    early = {"g_mix": dg_mix, "conv_w": dsmall[0:CONV_K], "conv_b": dsmall[3:4], "g_gm_v": dsmall[4:5],
             "w_spatial": dws, "b_spatial": dbt.T, "g_xattn": dg_xattn, "g_mem": dg_mem, "g_ffn2": dg_ffn2,
             "g_final": dg_final}
    finish_cross("w_mix_in")
    dgu1, (got_p, got_f, (early_all,)) = swiglu_bwd(
        "ffn1_dact", dh1b, full["w_ffn1_out"], gu1, 0.5,
        jobs=[send_pair("w_ffn1_out"), send_final("w_mix_in"), stack_job(_pack([early[n] for n in EARLY_SMALL]))])
    take_pair(["w_ffn1_out"], got_p)
    start_cross("w_ffn1_out")
    take_final(["w_mix_in"], got_f)
    theirs = mm_tn_pair_half("ffn1_dwin_theirs", n1, dgu1, BF16, place, False)
    mine, ((from_sibling,),) = mm_tn_pair_half("ffn1_dwin_mine", n1, dgu1, BF16, place, True,
                                               jobs=[pair_job([theirs], [1], is_half=True)])
    part["w_ffn1_in"] = pair_add("pair_add_w_ffn1_in", mine, from_sibling, 1, place)
    start_cross("w_ffn1_in")
    finish_cross("w_ffn1_out")
    dn1, (got,) = mm_nt_pair_halves("ffn1_dn", dgu1, w1_left, w1_right, F32, jobs=[send_final("w_ffn1_out")])
    take_final(["w_ffn1_out"], got)
    dx, _, dg_ffn1 = rmsnorm_bwd("norm1_bwd", dn1, x2, r1, g_ffn1, dh1)
    finish_cross("w_ffn1_in")
    got, (late_all,) = comm_only("tail_final", [send_final("w_ffn1_in"), stack_job(_pack([dg_ffn1]))])
    take_final(["w_ffn1_in"], got)

    early_sum, late_sum = small_sum("small_sum", [early_all, late_all])
    for n, g in zip(EARLY_SMALL, _unpack(early_sum, [early[n].shape for n in EARLY_SMALL])):
        grads[n] = g
    grads["g_ffn1"] = _unpack(late_sum, [dg_ffn1.shape])[0]
    taps_cols = conv_w.shape[2]
    grads["conv_w"] = lax.dynamic_slice_in_dim(grads["conv_w"], blk * taps_cols, taps_cols, axis=1)
    packed = [_pack([src[n] for n in SMALL]) for src in (wts, grads, mom, var)]
    own_shapes = [wts[n].shape for n in SMALL]
    for dst, buf in zip((delta, new_m, new_v), adamw("adamw_small", *packed)[1:]):
        for n, a in zip(SMALL, _unpack(buf, own_shapes)):
            dst[n] = a

    loss = lax.psum(loss_blk[0, 0], ("x", "y", "c"))
    outs = [loss, dx[None]]
    for group in (grads, delta, new_m, new_v):
        outs += [group[n].reshape(wts[n].shape) for n in WEIGHTS]
    return tuple(outs)
```

```python
import math

import jax
import jax.numpy as jnp
from jax import lax
from jax.experimental import pallas as pl
from jax.experimental.pallas import tpu as pltpu
from jax.experimental.pallas import tpu_sc as plsc

F32 = jnp.float32
BF16 = jnp.bfloat16
EPS = 1e-6
GROUP = 128
XA_HEADS = 4
CONV_K = 3
N_CHIPS = 4
VMEM_LIMIT_BYTES = 56 * 1024 * 1024

ADAM_LR = 0.001
ADAM_B1 = 0.9
ADAM_B2 = 0.999
ADAM_EPS = 1e-08
ADAM_WD = 0.01
ADAM_STEP = 10

MESH = pl.DeviceIdType.MESH
ANY = pl.BlockSpec(memory_space=pl.ANY)


def _tile(dim, pref, mult=128):
    if dim <= pref:
        return dim
    t = (pref // mult) * mult
    while t >= mult:
        if dim % t == 0:
            return t
        t -= mult
    raise ValueError(f"no tile for {dim} under {pref}")


def _params(sem):
    return pltpu.CompilerParams(dimension_semantics=sem, vmem_limit_bytes=VMEM_LIMIT_BYTES)


def _sds(shape, dtype):
    return jax.ShapeDtypeStruct(shape, dtype)


def _dot_nn(a, b):
    return jnp.dot(a, b, preferred_element_type=F32)


def _dot_nt(a, b):
    return lax.dot_general(a, b, (((1,), (1,)), ((), ())), preferred_element_type=F32)


def _dot_tn(a, b):
    return lax.dot_general(a, b, (((0,), (0,)), ((), ())), preferred_element_type=F32)


class Job:
    def __init__(self, inputs, out_shapes, aliases, sems, start, middle, finish):
        self.inputs, self.out_shapes, self.aliases, self.sems = inputs, out_shapes, aliases, sems
        self.start, self.middle, self.finish = start, middle, finish


def _place():
    x, y, c = lax.axis_index("x"), lax.axis_index("y"), lax.axis_index("c")
    chips = [(1 - x, y), (x, 1 - y), (1 - x, 1 - y)]
    return x, y, c, chips


def _ds(start, size, lane):
    if not isinstance(start, int):
        start = pl.multiple_of(start, 128 if lane else 16)
    return pl.ds(start, size)


WHOLE = (0, 1, 1)


def _window(ref, axis, shape, blk=None, half=None, sub=WHOLE, within=WHOLE):
    n = shape[axis] // N_CHIPS
    hs = shape[1 - axis] // 2
    idx = [slice(None), slice(None)]
    if blk is not None:
        b_first, b_count, b_pieces = within
        b_ext = n // b_pieces
        idx[axis] = _ds(blk * n + b_first * b_ext, b_count * b_ext, axis == 1)
    first, count, pieces = sub
    ext = hs // pieces
    if half is not None:
        idx[1 - axis] = _ds(half * hs + first * ext, count * ext, axis == 0)
    elif pieces > 1:
        idx[1 - axis] = _ds(first * ext, count * ext, axis == 0)
    return ref.at[tuple(idx)]


def _remote(src, dst, send_sem, recv_sem, dev):
    return pltpu.make_async_remote_copy(src_ref=src, dst_ref=dst, send_sem=send_sem, recv_sem=recv_sem,
                                        device_id=dev, device_id_type=MESH)


def _full_shape(block_shape, axis):
    out = list(block_shape)
    out[axis] *= N_CHIPS
    return tuple(out)


def _half_all(shape, axis):
    out = list(shape)
    out[1 - axis] //= 2
    return tuple(out)


def _block(shape, axis):
    out = list(shape)
    out[axis] //= N_CHIPS
    return tuple(out)


def _half_block(shape, axis):
    return _half_all(_block(shape, axis), axis)


def gather_job(items):
    nw = len(items)
    shapes = [item[0].shape for item in items]
    n_sem = 8

    def parts(sub):
        first, count, pieces = sub
        return (2 * first, count, 2 * pieces), (2 * first + count, count, 2 * pieces)

    def start(pos, ins, outs, sems):
        x, y, c, chips = pos
        for w, (_, ax, sub, within) in enumerate(items):
            mine = _window(outs[w], ax, shapes[w], blk=2 * x + y, half=c, sub=sub, within=within)
            for j in range(2):
                _remote(mine, mine, sems[0].at[n_sem * w + j], sems[1].at[n_sem * w + j], (*chips[j], c)).start()

    def middle(pos, ins, outs, sems):
        x, y, c, chips = pos
        for w, (_, ax, sub, within) in enumerate(items):
            for j in range(2):
                cx, cy = chips[j]
                landed = _window(outs[w], ax, shapes[w], blk=2 * cx + cy, half=c, sub=sub, within=within)
                _remote(landed, landed, sems[0].at[n_sem * w + j], sems[1].at[n_sem * w + j], (cx, cy, c)).wait_recv()
                part = _window(outs[w], ax, shapes[w], blk=2 * cx + cy, half=c, sub=parts(sub)[j], within=within)
                _remote(part, part, sems[0].at[n_sem * w + 2 + j], sems[1].at[n_sem * w + 2 + j],
                        (*chips[1 - j], c)).start()
                _remote(landed, landed, sems[0].at[n_sem * w + 4 + j], sems[1].at[n_sem * w + 4 + j],
                        (x, y, 1 - c)).start()

    def finish(pos, ins, outs, sems):
        x, y, c, chips = pos
        sib = (x, y, 1 - c)
        for w, (_, ax, sub, within) in enumerate(items):
            dx, dy = chips[2]
            for j in range(2):
                part = _window(outs[w], ax, shapes[w], blk=2 * dx + dy, half=c, sub=parts(sub)[j], within=within)
                cp = _remote(part, part, sems[0].at[n_sem * w + 2 + j], sems[1].at[n_sem * w + 2 + j], sib)
                cp.wait_recv()
                cp.wait_send()
            diag = _window(outs[w], ax, shapes[w], blk=2 * dx + dy, half=c, sub=sub, within=within)
            _remote(diag, diag, sems[0].at[n_sem * w + 6], sems[1].at[n_sem * w + 6], sib).start()
        for w, (_, ax, sub, within) in enumerate(items):
            for j, (cx, cy) in enumerate(chips):
                passed = _window(outs[w], ax, shapes[w], blk=2 * cx + cy, half=1 - c, sub=sub, within=within)
                cp = _remote(passed, passed, sems[0].at[n_sem * w + 4 + j], sems[1].at[n_sem * w + 4 + j], sib)
                cp.wait_recv()
                cp.wait_send()
            mine = _window(outs[w], ax, shapes[w], blk=2 * x + y, half=c, sub=sub, within=within)
            for j in range(2):
                _remote(mine, mine, sems[0].at[n_sem * w + j], sems[1].at[n_sem * w + j], sib).wait_send()

    sems = [pltpu.SemaphoreType.DMA((n_sem * nw,)), pltpu.SemaphoreType.DMA((n_sem * nw,))]
    return Job([item[0] for item in items], [_sds(item[0].shape, item[0].dtype) for item in items],
               {w: w for w in range(nw)}, sems, start, middle, finish)


def by_sequencer(name, job, peers, collective_id):
    ins = [jax.new_ref(a, memory_space=pltpu.MemorySpace.HBM) for a in job.inputs]
    from_input = {o: i for i, o in job.aliases.items()}
    outs = [ins[from_input[k]] if k in from_input else jax.empty_ref(s, memory_space=pltpu.MemorySpace.HBM)
            for k, s in enumerate(job.out_shapes)]

    @pl.kernel(mesh=plsc.ScalarSubcoreMesh(axis_name="sequencer", num_cores=1), name=name,
               scratch_types=tuple(job.sems), compiler_params=pltpu.CompilerParams(collective_id=collective_id))
    def launch(*sems):
        pos = _place()
        x, y, c, chips = pos
        devs = {"sibling": [(x, y, 1 - c)],
                "chips": [(cx, cy, c) for cx, cy in chips],
                "gather": [(*chips[0], c), (*chips[1], c), (x, y, 1 - c)],
                "all": [(px, py, pc) for px in (x, 1 - x) for py in (y, 1 - y) for pc in (c, 1 - c)][1:]}[peers]
        barrier = pltpu.get_barrier_semaphore()
        for dev in devs:
            pl.semaphore_signal(barrier, inc=1, device_id=dev, device_id_type=MESH)
        pl.semaphore_wait(barrier, len(devs))
        for phase in (job.start, job.middle, job.finish):
            if phase is not None:
                phase(pos, ins, outs, list(sems))

    launch()
    kept = [r[...] for r in ins]
    return kept, [kept[from_input[k]] if k in from_input else r[...] for k, r in enumerate(outs)]


def pair_job(grads, axes, is_half=False):
    nw = len(grads)
    shapes = [g.shape for g in grads]

    def start(pos, ins, outs, sems):
        x, y, c, _ = pos
        for w in range(nw):
            src = ins[w] if is_half else _window(ins[w], axes[w], shapes[w], half=1 - c)
            _remote(src, outs[w], sems[0].at[w], sems[1].at[w], (x, y, 1 - c)).start()

    def finish(pos, ins, outs, sems):
        x, y, c, _ = pos
        for w in range(nw):
            cp = _remote(outs[w], outs[w], sems[0].at[w], sems[1].at[w], (x, y, 1 - c))
            cp.wait_recv()
            cp.wait_send()

    sems = [pltpu.SemaphoreType.DMA((nw,)), pltpu.SemaphoreType.DMA((nw,))]
    out_shapes = [_sds(s if is_half else _half_all(s, a), BF16) for s, a in zip(shapes, axes)]
    return Job(list(grads), out_shapes, {}, sems, start, None, finish)


def cross_job(items):
    nw = len(items)
    inputs, aliases = [], {}
    for w, (part, ax, shape, prev, sub) in enumerate(items):
        inputs.append(part)
        if prev is not None:
            aliases[len(inputs)] = w
            inputs.append(prev)

    def copies(pos, ins, outs, sems):
        x, y, c, chips = pos
        k = 0
        for w, (_, ax, shape, prev, sub) in enumerate(items):
            src = ins[k]
            k += 2 if prev is not None else 1
            for j, (cx, cy) in enumerate(chips):
                slot = _window(outs[w].at[j], ax, shape, sub=sub)
                yield (_remote(_window(src, ax, shape, blk=2 * cx + cy, sub=sub), slot,
                               sems[0].at[3 * w + j], sems[1].at[3 * w + j], (cx, cy, c)),
                       _remote(slot, slot, sems[0].at[3 * w + j], sems[1].at[3 * w + j], (cx, cy, c)))

    def start(pos, ins, outs, sems):
        for send, _ in copies(pos, ins, outs, sems):
            send.start()

    def finish(pos, ins, outs, sems):
        for send, recv in copies(pos, ins, outs, sems):
            recv.wait_recv()
            send.wait_send()

    sems = [pltpu.SemaphoreType.DMA((3 * nw,)), pltpu.SemaphoreType.DMA((3 * nw,))]
    out_shapes = [_sds((3,) + _half_block(shape, ax), BF16) for _, ax, shape, _, _ in items]
    return Job(inputs, out_shapes, aliases, sems, start, None, finish)


def final_job(blocks, axes, shapes):
    nw = len(blocks)

    def start(pos, ins, outs, sems):
        x, y, c, _ = pos
        for w in range(nw):
            mine = _window(outs[w], axes[w], shapes[w], half=c)
            _remote(mine, mine, sems[0].at[w], sems[1].at[w], (x, y, 1 - c)).start()

    def finish(pos, ins, outs, sems):
        x, y, c, _ = pos
        for w in range(nw):
            theirs = _window(outs[w], axes[w], shapes[w], half=1 - c)
            cp = _remote(theirs, theirs, sems[0].at[w], sems[1].at[w], (x, y, 1 - c))
            cp.wait_recv()
            cp.wait_send()

    sems = [pltpu.SemaphoreType.DMA((nw,)), pltpu.SemaphoreType.DMA((nw,))]
    return Job(list(blocks), [_sds(b.shape, b.dtype) for b in blocks], {w: w for w in range(nw)}, sems, start, None,
               finish)


def stack_job(small):
    def peers(pos):
        x, y, c, _ = pos
        for k in range(1, 8):
            yield k - 1, (1 - x if k & 4 else x, 1 - y if k & 2 else y, 1 - c if k & 1 else c)

    def start(pos, ins, outs, sems):
        x, y, c, _ = pos
        mine = outs[0].at[4 * x + 2 * y + c]
        pltpu.make_async_copy(ins[0], mine, sems[2]).start()
        for k, dev in peers(pos):
            _remote(ins[0], mine, sems[0].at[k], sems[1].at[k], dev).start()

    def finish(pos, ins, outs, sems):
        x, y, c, _ = pos
        for k, (px, py, pc) in peers(pos):
            slot = outs[0].at[4 * px + 2 * py + pc]
            cp = _remote(slot, slot, sems[0].at[k], sems[1].at[k], (px, py, pc))
            cp.wait_recv()
            cp.wait_send()
        pltpu.make_async_copy(ins[0], outs[0].at[4 * x + 2 * y + c], sems[2]).wait()

    sems = [pltpu.SemaphoreType.DMA((7,)), pltpu.SemaphoreType.DMA((7,)), pltpu.SemaphoreType.DMA]
    return Job([small], [_sds((8,) + small.shape, small.dtype)], {}, sems, start, None, finish)


def columns_job(block):
    cols = block.shape[1]
    place = lambda out, b: out.at[:, _ds(b * cols, cols, True)]

    def start(pos, ins, outs, sems):
        x, y, c, chips = pos
        pltpu.make_async_copy(ins[0], place(outs[0], 2 * x + y), sems[2]).start()
        for j, (cx, cy) in enumerate(chips):
            _remote(ins[0], place(outs[0], 2 * x + y), sems[0].at[j], sems[1].at[j], (cx, cy, c)).start()

    def finish(pos, ins, outs, sems):
        x, y, c, chips = pos
        for j, (cx, cy) in enumerate(chips):
            got = place(outs[0], 2 * cx + cy)
            cp = _remote(got, got, sems[0].at[j], sems[1].at[j], (cx, cy, c))
            cp.wait_recv()
            cp.wait_send()
        pltpu.make_async_copy(ins[0], place(outs[0], 2 * x + y), sems[2]).wait()

    sems = [pltpu.SemaphoreType.DMA((3,)), pltpu.SemaphoreType.DMA((3,)), pltpu.SemaphoreType.DMA]
    return Job([block], [_sds((block.shape[0], N_CHIPS * cols), block.dtype)], {}, sems, start, None, finish)


def _call(name, body, grid, in_specs, out_specs, out_shape, args, scratch=(), sem=None, jobs=(), place=None,
          carried=None):
    n_in, n_out, n_sc = len(args), len(out_shape), len(scratch)
    carried = dict(carried or {})

    def launch(fn, in_specs, out_specs, out_shape, scratch, aliases, sem, operands):
        if place is None:
            return pl.pallas_call(
                fn, name=name, grid=grid, in_specs=in_specs, out_specs=out_specs, out_shape=out_shape,
                scratch_shapes=scratch, input_output_aliases=aliases, compiler_params=_params(sem))(*operands)
        spec = pltpu.PrefetchScalarGridSpec(num_scalar_prefetch=1, grid=grid, in_specs=in_specs,
                                            out_specs=out_specs, scratch_shapes=scratch)
        return pl.pallas_call(
            lambda p_ref, *refs: fn(*refs), name=name, grid_spec=spec, out_shape=out_shape,
            input_output_aliases={k + 1: v for k, v in aliases.items()}, compiler_params=_params(sem),
        )(place, *operands)

    if not jobs:
        outs = launch(body, list(in_specs), list(out_specs), list(out_shape), list(scratch), carried, sem, args)
        return list(outs), []

    total = math.prod(grid) if grid else 1
    mid = min(total - 1, (2 * total) // 3)

    def split(refs, start, counts):
        out = []
        for n in counts:
            out.append(refs[start:start + n])
            start += n
        return out, start

    def wrapped(*refs):
        c_in = refs[:n_in]
        j_ins, p = split(refs, n_in, [len(j.inputs) for j in jobs])
        c_out = refs[p:p + n_out]
        j_outs, p = split(refs, p + n_out, [len(j.out_shapes) for j in jobs])
        c_sc = refs[p:p + n_sc]
        j_sems, p = split(refs, p + n_sc, [len(j.sems) for j in jobs])
        pos = _place()
        step = 0
        for axis, extent in enumerate(grid):
            step = step * extent + pl.program_id(axis)

        def run(phase):
            for j, ins, outs, sems in zip(jobs, j_ins, j_outs, j_sems):
                fn = getattr(j, phase)
                if fn is not None:
                    fn(pos, ins, outs, sems)

        if total == 1:
            run("start")
            body(*c_in, *c_out, *c_sc)
            run("middle")
            run("finish")
            return
        pl.when(step == 0)(lambda: run("start"))
        body(*c_in, *c_out, *c_sc)
        if any(j.middle is not None for j in jobs):
            pl.when(step == mid)(lambda: run("middle"))
        pl.when(step == total - 1)(lambda: run("finish"))

    aliases, in_at, out_at = carried, n_in, n_out
    for j in jobs:
        for src, dst in j.aliases.items():
            aliases[in_at + src] = out_at + dst
        in_at += len(j.inputs)
        out_at += len(j.out_shapes)
    outs = launch(
        wrapped, list(in_specs) + [ANY] * (in_at - n_in), list(out_specs) + [ANY] * (out_at - n_out),
        list(out_shape) + [s for j in jobs for s in j.out_shapes],
        list(scratch) + [s for j in jobs for s in j.sems], aliases, ("arbitrary",) * len(grid),
        [*args, *[a for j in jobs for a in j.inputs]])
    job_outs, p = split(outs, n_out, [len(j.out_shapes) for j in jobs])
    return list(outs[:n_out]), [list(o) for o in job_outs]


def comm_only(name, jobs):
    def body(dummy_ref, out_ref):
        out_ref[...] = dummy_ref[...]

    dummy = jnp.zeros((8, 128), F32)
    spec = pl.BlockSpec((8, 128), lambda: (0, 0))
    return _call(name, body, (), [spec], [spec], [_sds((8, 128), F32)], [dummy], jobs=jobs)[1]


def _ret(outs, job_outs, jobs, single=True):
    res = outs[0] if single else outs
    return (res, job_outs) if jobs else res


def rmsnorm_fwd(name, x, g, jobs=()):
    s, d = x.shape
    tm = _tile(s, 512, 8)

    def body(x_ref, g_ref, n_ref, r_ref):
        xv = x_ref[...]
        r = lax.rsqrt(jnp.mean(xv * xv, axis=-1, keepdims=True) + EPS)
        n_ref[...] = (xv * r * g_ref[...]).astype(BF16)
        r_ref[...] = r

    row = lambda i: (i, 0)
    outs, job_outs = _call(
        name, body, (s // tm,),
        [pl.BlockSpec((tm, d), row), pl.BlockSpec((1, d), lambda i: (0, 0))],
        [pl.BlockSpec((tm, d), row), pl.BlockSpec((tm, 1), row)],
        [_sds((s, d), BF16), _sds((s, 1), F32)], [x, g], sem=("arbitrary",), jobs=jobs)
    return _ret(outs, job_outs, jobs, single=False)


def rmsnorm_bwd(name, dn, x, r, g, dh_in, jobs=()):
    s, d = x.shape
    tm = _tile(s, 512, 8)

    def body(dn_ref, x_ref, r_ref, g_ref, dh_ref, out_ref, outb_ref, dg_ref):
        i = pl.program_id(0)
        xh = x_ref[...] * r_ref[...]
        dnv = dn_ref[...]
        dxh = dnv * g_ref[...]
        dx = r_ref[...] * (dxh - xh * jnp.mean(dxh * xh, axis=-1, keepdims=True))
        out = dh_ref[...] + dx
        out_ref[...] = out
        outb_ref[...] = out.astype(BF16)
        part = jnp.sum(dnv * xh, axis=0, keepdims=True)

        @pl.when(i == 0)
        def _():
            dg_ref[...] = part

        @pl.when(i > 0)
        def _():
            dg_ref[...] += part

    row = lambda i: (i, 0)
    fixed = lambda i: (0, 0)
    outs, job_outs = _call(
        name, body, (s // tm,),
        [pl.BlockSpec((tm, d), row), pl.BlockSpec((tm, d), row), pl.BlockSpec((tm, 1), row),
         pl.BlockSpec((1, d), fixed), pl.BlockSpec((tm, d), row)],
        [pl.BlockSpec((tm, d), row), pl.BlockSpec((tm, d), row), pl.BlockSpec((1, d), fixed)],
        [_sds((s, d), F32), _sds((s, d), BF16), _sds((1, d), F32)], [dn, x, r, g, dh_in],
        sem=("arbitrary",), jobs=jobs)
    return _ret(outs, job_outs, jobs, single=False)


def gain_grad(name, dn_a, dn_b, x, r):
    s, d = x.shape
    tm = _tile(s, 512, 8)

    def body(a_ref, b_ref, x_ref, r_ref, dg_ref):
        i = pl.program_id(0)
        part = jnp.sum((a_ref[...] + b_ref[...]) * (x_ref[...] * r_ref[...]), axis=0, keepdims=True)

        @pl.when(i == 0)
        def _():
            dg_ref[...] = part

        @pl.when(i > 0)
        def _():
            dg_ref[...] += part

    row = lambda i: (i, 0)
    return _call(
        name, body, (s // tm,),
        [pl.BlockSpec((tm, d), row), pl.BlockSpec((tm, d), row), pl.BlockSpec((tm, d), row),
         pl.BlockSpec((tm, 1), row)],
        [pl.BlockSpec((1, d), lambda i: (0, 0))], [_sds((1, d), F32)], [dn_a, dn_b, x, r],
        sem=("arbitrary",))[0][0]


def loss_head(name, h, g, target):
    s, d = h.shape
    tm = _tile(s, 512, 8)
    nsteps = s // tm

    def body(h_ref, g_ref, t_ref, loss_ref, dh_ref, dhb_ref, dg_ref, sq_ref):
        i = pl.program_id(0)
        hv = h_ref[...]
        gv = g_ref[...]
        r = lax.rsqrt(jnp.mean(hv * hv, axis=-1, keepdims=True) + EPS)
        xh = hv * r
        err = xh * gv - t_ref[...]
        dy = err * (1.0 / d)
        dxh = dy * gv
        dh = r * (dxh - xh * jnp.mean(dxh * xh, axis=-1, keepdims=True))
        dh_ref[...] = dh
        dhb_ref[...] = dh.astype(BF16)
        dg_part = jnp.sum(dy * xh, axis=0, keepdims=True)
        sq_part = jnp.sum(err * err, axis=0, keepdims=True)

        @pl.when(i == 0)
        def _():
            dg_ref[...] = dg_part
            sq_ref[...] = sq_part

        @pl.when(i > 0)
        def _():
            dg_ref[...] += dg_part
            sq_ref[...] += sq_part

        @pl.when(i == nsteps - 1)
        def _():
            total = jnp.sum(sq_ref[...], axis=-1, keepdims=True) * (0.5 / d)
            loss_ref[...] = jnp.broadcast_to(total, loss_ref.shape)

    row = lambda i: (i, 0)
    fixed = lambda i: (0, 0)
    return _call(
        name, body, (nsteps,),
        [pl.BlockSpec((tm, d), row), pl.BlockSpec((1, d), fixed), pl.BlockSpec((tm, d), row)],
        [pl.BlockSpec((8, 128), fixed), pl.BlockSpec((tm, d), row), pl.BlockSpec((tm, d), row),
         pl.BlockSpec((1, d), fixed)],
        [_sds((8, 128), F32), _sds((s, d), F32), _sds((s, d), BF16), _sds((1, d), F32)], [h, g, target],
        scratch=[pltpu.VMEM((1, d), F32)], sem=("arbitrary",))[0]


def _mm(name, grid, in_arrays, in_specs, out_shapes, out_specs, acc_tile, dot, epilogue, jobs=(), place=None):
    nk = grid[2]
    n_in = len(in_arrays)
    n_out = len(out_shapes)

    def body(*refs):
        ins, outs = refs[:n_in], refs[n_in:n_in + n_out]
        if nk == 1:
            epilogue(dot(*ins), ins, outs)
            return
        acc = refs[n_in + n_out]
        k = pl.program_id(2)

        @pl.when(k == 0)
        def _():
            acc[...] = dot(*ins)

        @pl.when(jnp.logical_and(k > 0, k < nk - 1))
        def _():
            acc[...] += dot(*ins)

        @pl.when(k == nk - 1)
        def _():
            epilogue(acc[...] + dot(*ins), ins, outs)

    scratch = [pltpu.VMEM(acc_tile, F32)] if nk > 1 else []
    outs, job_outs = _call(name, body, grid, in_specs, out_specs, out_shapes, in_arrays, scratch=scratch,
                           sem=("parallel", "parallel", "arbitrary"), jobs=jobs, place=place)
    return _ret(outs, job_outs, jobs)


def _store(scale, dtype):
    def epilogue(acc, ins, outs):
        outs[0][...] = (acc * scale if scale != 1.0 else acc).astype(dtype)
    return epilogue


def mm_nn(name, a, w, out_dtype, tm=1024, tn=1024, tk=2048, jobs=()):
    m, kd = a.shape
    n = w.shape[1]
    tm, tn, tk = _tile(m, tm, 8), _tile(n, tn), _tile(kd, tk)
    return _mm(
        name, (n // tn, m // tm, kd // tk), [a, w],
        [pl.BlockSpec((tm, tk), lambda j, i, k: (i, k)), pl.BlockSpec((tk, tn), lambda j, i, k: (k, j))],
        [_sds((m, n), out_dtype)], [pl.BlockSpec((tm, tn), lambda j, i, k: (i, j))], (tm, tn),
        lambda a_ref, w_ref: _dot_nn(a_ref[...], w_ref[...]), _store(1.0, out_dtype), jobs)


def mm_nn_resid(name, a, w, x, scale, tm=1024, tn=1024, tk=1408, jobs=()):
    m, kd = a.shape
    n = w.shape[1]
    tm, tn, tk = _tile(m, tm, 8), _tile(n, tn), _tile(kd, tk)

    def epilogue(acc, ins, outs):
        outs[0][...] = ins[2][...] + scale * acc

    return _mm(
        name, (n // tn, m // tm, kd // tk), [a, w, x],
        [pl.BlockSpec((tm, tk), lambda j, i, k: (i, k)), pl.BlockSpec((tk, tn), lambda j, i, k: (k, j)),
         pl.BlockSpec((tm, tn), lambda j, i, k: (i, j))],
        [_sds((m, n), F32)], [pl.BlockSpec((tm, tn), lambda j, i, k: (i, j))], (tm, tn),
        lambda a_ref, w_ref, x_ref: _dot_nn(a_ref[...], w_ref[...]), epilogue, jobs)


def mm_nt(name, a, w, out_dtype, scale=1.0, tm=1024, tn=1024, tk=2048, jobs=()):
    m, kd = a.shape
    n = w.shape[0]
    tm, tn, tk = _tile(m, tm, 8), _tile(n, tn), _tile(kd, tk)
    return _mm(
        name, (n // tn, m // tm, kd // tk), [a, w],
        [pl.BlockSpec((tm, tk), lambda j, i, k: (i, k)), pl.BlockSpec((tn, tk), lambda j, i, k: (j, k))],
        [_sds((m, n), out_dtype)], [pl.BlockSpec((tm, tn), lambda j, i, k: (i, j))], (tm, tn),
        lambda a_ref, w_ref: _dot_nt(a_ref[...], w_ref[...]), _store(scale, out_dtype), jobs)


def mm_nt_pair(name, a3, w, out_dtype, tm=1024, tn=1024, tk=2816, jobs=()):
    _, m, f = a3.shape
    n = w.shape[0]
    tm, tn, tk = _tile(m, tm, 8), _tile(n, tn), _tile(f, tk)
    nkf = f // tk
    return _mm(
        name, (n // tn, m // tm, 2 * nkf), [a3, w],
        [pl.BlockSpec((None, tm, tk), lambda j, i, k: (k // nkf, i, k % nkf)),
         pl.BlockSpec((tn, tk), lambda j, i, k: (j, k))],
        [_sds((m, n), out_dtype)], [pl.BlockSpec((tm, tn), lambda j, i, k: (i, j))], (tm, tn),
        lambda a_ref, w_ref: _dot_nt(a_ref[...], w_ref[...]), _store(1.0, out_dtype), jobs)


def mm_nt_pair_halves(name, a3, w_left, w_right, out_dtype, tm=1024, tn=1024, jobs=()):
    _, m, f = a3.shape
    n = w_left.shape[0]
    w = w_left.shape[1] // N_CHIPS
    tm, tn = _tile(m, tm, 8), _tile(n, tn)
    per_half = f // (2 * w)

    def dot(a_ref, l_ref, r_ref):
        return _dot_nt(a_ref[:, 0:w], l_ref[...]) + _dot_nt(a_ref[:, w:2 * w], r_ref[...])

    half = pl.BlockSpec((tn, w), lambda j, i, k: (j, k))
    return _mm(
        name, (n // tn, m // tm, N_CHIPS), [a3, w_left, w_right],
        [pl.BlockSpec((None, tm, 2 * w), lambda j, i, k: (k // per_half, i, k % per_half)), half, half],
        [_sds((m, n), out_dtype)], [pl.BlockSpec((tm, tn), lambda j, i, k: (i, j))], (tm, tn),
        dot, _store(1.0, out_dtype), jobs)


def mm_nt_norm_bwd(name, a, w, x, r, g, dh_in, tm=512, tk=1408, jobs=()):
    pair = a.ndim == 3
    m, kd = a.shape[-2], a.shape[-1]
    d = w.shape[0]
    tm, tk = _tile(m, tm, 8), _tile(kd, tk)
    nkf = kd // tk
    nk = 2 * nkf if pair else nkf
    if pair:
        a_spec = pl.BlockSpec((None, tm, tk), lambda i, k: (k // nkf, i, k % nkf))
    else:
        a_spec = pl.BlockSpec((tm, tk), lambda i, k: (i, k))
    row = lambda i, k: (i, 0)
    fixed = lambda i, k: (0, 0)

    def body(a_ref, w_ref, x_ref, r_ref, g_ref, dh_ref, out_ref, outb_ref, dg_ref, *acc):
        i, k = pl.program_id(0), pl.program_id(1)
        dot = lambda: _dot_nt(a_ref[...], w_ref[...])

        def finish(dn):
            xh = x_ref[...] * r_ref[...]
            dxh = dn * g_ref[...]
            out = dh_ref[...] + r_ref[...] * (dxh - xh * jnp.mean(dxh * xh, axis=-1, keepdims=True))
            out_ref[...] = out
            outb_ref[...] = out.astype(BF16)
            part = jnp.sum(dn * xh, axis=0, keepdims=True)

            @pl.when(i == 0)
            def _():
                dg_ref[...] = part

            @pl.when(i > 0)
            def _():
                dg_ref[...] += part

        if nk == 1:
            finish(dot())
            return

        @pl.when(k == 0)
        def _():
            acc[0][...] = dot()

        @pl.when(jnp.logical_and(k > 0, k < nk - 1))
        def _():
            acc[0][...] += dot()

        @pl.when(k == nk - 1)
        def _():
            finish(acc[0][...] + dot())

    outs, job_outs = _call(
        name, body, (m // tm, nk),
        [a_spec, pl.BlockSpec((d, tk), lambda i, k: (0, k)), pl.BlockSpec((tm, d), row), pl.BlockSpec((tm, 1), row),
         pl.BlockSpec((1, d), fixed), pl.BlockSpec((tm, d), row)],
        [pl.BlockSpec((tm, d), row), pl.BlockSpec((tm, d), row), pl.BlockSpec((1, d), fixed)],
        [_sds((m, d), F32), _sds((m, d), BF16), _sds((1, d), F32)], [a, w, x, r, g, dh_in],
        scratch=[pltpu.VMEM((tm, d), F32)] if nk > 1 else [], sem=("arbitrary", "arbitrary"), jobs=jobs)
    return _ret(outs, job_outs, jobs, single=False)


def mm_tn(name, a, b, out_dtype, scale=1.0, tm=1024, tn=1024, tk=4096, jobs=()):
    kd, m = a.shape
    n = b.shape[1]
    tm, tn, tk = _tile(m, tm), _tile(n, tn), _tile(kd, tk, 16)
    return _mm(
        name, (n // tn, m // tm, kd // tk), [a, b],
        [pl.BlockSpec((tk, tm), lambda j, i, k: (k, i)), pl.BlockSpec((tk, tn), lambda j, i, k: (k, j))],
        [_sds((m, n), out_dtype)], [pl.BlockSpec((tm, tn), lambda j, i, k: (i, j))], (tm, tn),
        lambda a_ref, b_ref: _dot_tn(a_ref[...], b_ref[...]), _store(scale, out_dtype), jobs)


def mm_tn_pair(name, a, b3, out_dtype, tm=1024, tn=512, tk=4096, jobs=()):
    kd, m = a.shape
    f = b3.shape[2]
    tm, tn, tk = _tile(m, tm), _tile(f, tn), _tile(kd, tk, 16)
    nf = f // tn
    return _mm(
        name, (m // tm, 2 * nf, kd // tk), [a, b3],
        [pl.BlockSpec((tk, tm), lambda i, j, k: (k, i)),
         pl.BlockSpec((None, tk, tn), lambda i, j, k: (j // nf, k, j % nf))],
        [_sds((m, 2 * f), out_dtype)], [pl.BlockSpec((tm, tn), lambda i, j, k: (i, j))], (tm, tn),
        lambda a_ref, b_ref: _dot_tn(a_ref[...], b_ref[...]), _store(1.0, out_dtype), jobs)


def mm_tn_pair_half(name, a, b3, out_dtype, place, mine, tm=1024, tn=512, tk=4096, jobs=()):
    kd, m = a.shape
    f = b3.shape[2]
    tm, tn, tk = _tile(m // 2, tm), _tile(f, tn), _tile(kd, tk, 16)
    nf, nbm = f // tn, m // 2 // tm
    which = (lambda p: p[1]) if mine else (lambda p: 1 - p[1])
    return _mm(
        name, (nbm, 2 * nf, kd // tk), [a, b3],
        [pl.BlockSpec((tk, tm), lambda i, j, k, p: (k, i + which(p) * nbm)),
         pl.BlockSpec((None, tk, tn), lambda i, j, k, p: (j // nf, k, j % nf))],
        [_sds((m // 2, 2 * f), out_dtype)], [pl.BlockSpec((tm, tn), lambda i, j, k, p: (i, j))], (tm, tn),
        lambda a_ref, b_ref: _dot_tn(a_ref[...], b_ref[...]), _store(1.0, out_dtype), jobs, place)


def swiglu_fwd(name, n, w_in, tm=1024, tn=512, jobs=(), stride=1, phase=0, prev=None, compact=False):
    s, d = n.shape
    f = w_in.shape[1] // 2 * (stride if compact else 1)
    tm, tn = _tile(s, tm, 8), _tile(f, tn)
    nf = f // tn
    col = lambda j: j * stride + phase
    w_gate = (lambda j: j) if compact else col
    w_up = (lambda j: j + nf // stride) if compact else (lambda j: col(j) + nf)

    def body(n_ref, wg_ref, wu_ref, *rest):
        gu_ref, a_ref = rest[-2:]
        nv = n_ref[...]
        g = _dot_nn(nv, wg_ref[...])
        u = _dot_nn(nv, wu_ref[...])
        gu_ref[0] = g.astype(BF16)
        gu_ref[1] = u.astype(BF16)
        a_ref[...] = (g * jax.nn.sigmoid(g) * u).astype(BF16)

    kept = list(prev) if prev is not None else []
    outs, job_outs = _call(
        name, body, (nf // stride, s // tm),
        [pl.BlockSpec((tm, d), lambda j, i: (i, 0)), pl.BlockSpec((d, tn), lambda j, i: (0, w_gate(j))),
         pl.BlockSpec((d, tn), lambda j, i: (0, w_up(j)))] + [ANY] * len(kept),
        [pl.BlockSpec((2, tm, tn), lambda j, i: (0, i, col(j))), pl.BlockSpec((tm, tn), lambda j, i: (i, col(j)))],
        [_sds((2, s, f), BF16), _sds((s, f), BF16)], [n, w_in, w_in] + kept, sem=("parallel", "parallel"),
        jobs=jobs, carried={3 + k: k for k in range(len(kept))})
    return _ret(outs, job_outs, jobs, single=False)


def swiglu_bwd(name, dh, w_out, gu, scale, tm=1024, tn=512, jobs=()):
    s, d = dh.shape
    f = w_out.shape[0]
    tm, tn = _tile(s, tm, 8), _tile(f, tn)

    sub = _tile(tm, 256, 8)

    def body(dh_ref, w_ref, gu_ref, out_ref):
        for lo in range(0, tm, sub):
            rows = slice(lo, lo + sub)
            da = (_dot_nt(dh_ref[rows, :], w_ref[...]) * scale).astype(BF16)
            g = gu_ref[0, rows, :]
            u = gu_ref[1, rows, :]
            sg = 0.5 * jnp.tanh(0.5 * g) + 0.5
            t = g * sg
            out_ref[0, rows, :] = da * (u * (sg + t * (1.0 - sg)))
            out_ref[1, rows, :] = da * t

    outs, job_outs = _call(
        name, body, (f // tn, s // tm),
        [pl.BlockSpec((tm, d), lambda j, i: (i, 0)), pl.BlockSpec((tn, d), lambda j, i: (j, 0)),
         pl.BlockSpec((2, tm, tn), lambda j, i: (0, i, j))],
        [pl.BlockSpec((2, tm, tn), lambda j, i: (0, i, j))],
        [_sds((2, s, f), BF16)], [dh, w_out, gu], sem=("parallel", "parallel"), jobs=jobs)
    return _ret(outs, job_outs, jobs)


HALO = 16


def _conv_inputs(z_ref, hgc_ref, hhc_ref, i, cw, tm):
    gc = z_ref[:, cw:2 * cw].astype(F32)
    hc = z_ref[:, 2 * cw:3 * cw].astype(F32)
    cin = gc * hc
    halo = hgc_ref[...].astype(F32) * hhc_ref[...].astype(F32) * (i > 0).astype(F32)
    row = lax.broadcasted_iota(jnp.int32, (tm, cw), 0)
    x1 = jnp.where(row == 0, halo[HALO - 1:HALO], pltpu.roll(cin, 1, 0))
    x2 = jnp.where(row == 0, halo[HALO - 2:HALO - 1], jnp.where(row == 1, halo[HALO - 1:HALO], pltpu.roll(cin, 2, 0)))
    return gc, hc, cin, x1, x2


def _tril(w):
    r = lax.broadcasted_iota(jnp.int32, w.shape, 0)
    c = lax.broadcasted_iota(jnp.int32, w.shape, 1)
    return jnp.where(r >= c, w, jnp.zeros_like(w))


def mixer_fwd(name, z, conv_w, conv_b, g_v, w_s, b_t, tm=256, jobs=()):
    s, zc = z.shape
    cw = conv_w.shape[1]
    gw = g_v.shape[1]
    heads = gw // GROUP
    tm = _tile(s, tm)
    hb = tm // HALO

    def body(z_ref, hgc_ref, hhc_ref, cw_ref, cb_ref, gv_ref, ws_ref, bt_ref, y_ref):
        i = pl.program_id(0)
        _, _, cin, x1, x2 = _conv_inputs(z_ref, hgc_ref, hhc_ref, i, cw, tm)
        cv = cb_ref[...] + cw_ref[2:3, :] * cin + cw_ref[1:2, :] * x1 + cw_ref[0:1, :] * x2
        y_ref[:, 0:cw] = (z_ref[:, 0:cw].astype(F32) * cv).astype(BF16)
        for h in range(heads):
            lo = h * GROUP
            vh = z_ref[:, 3 * cw + gw + lo:3 * cw + gw + lo + GROUP].astype(F32)
            rv = lax.rsqrt(jnp.mean(vh * vh, axis=-1, keepdims=True) + EPS)
            vn = (vh * rv * gv_ref[:, lo:lo + GROUP]).astype(BF16)
            w = _tril(ws_ref[h]).astype(BF16)
            for n in range(tm // GROUP):
                rows = slice(n * GROUP, (n + 1) * GROUP)
                sg = _dot_nn(w, vn[rows]) + bt_ref[:, h:h + 1]
                u = z_ref[rows, 3 * cw + lo:3 * cw + lo + GROUP].astype(F32)
                y_ref[rows, cw + lo:cw + lo + GROUP] = (u * sg).astype(BF16)

    fixed2 = lambda i: (0, 0)
    outs, job_outs = _call(
        name, body, (s // tm,),
        [pl.BlockSpec((tm, zc), lambda i: (i, 0)),
         pl.BlockSpec((HALO, cw), lambda i: (jnp.maximum(i * hb - 1, 0), 1)),
         pl.BlockSpec((HALO, cw), lambda i: (jnp.maximum(i * hb - 1, 0), 2)),
         pl.BlockSpec(conv_w.shape, fixed2), pl.BlockSpec(conv_b.shape, fixed2),
         pl.BlockSpec(g_v.shape, fixed2), pl.BlockSpec(w_s.shape, lambda i: (0, 0, 0)),
         pl.BlockSpec(b_t.shape, fixed2)],
        [pl.BlockSpec((tm, cw + gw), lambda i: (i, 0))], [_sds((s, cw + gw), BF16)],
        [z, z, z, conv_w, conv_b, g_v, w_s, b_t], sem=("arbitrary",), jobs=jobs)
    return _ret(outs, job_outs, jobs)


def mixer_bwd(name, z, dy, conv_w, conv_b, g_v, w_s, b_t, tm=256, jobs=()):
    s, zc = z.shape
    cw = conv_w.shape[1]
    gw = g_v.shape[1]
    heads = gw // GROUP
    tm = _tile(s, tm)
    hb = tm // HALO
    nsteps = s // tm
    last_halo = s // HALO - 1

    def body(z_ref, hgc_ref, hhc_ref, ngb_ref, dy_ref, ndy_ref, cw_ref, cb_ref, gv_ref, ws_ref, bt_ref,
             dz_ref, sm_ref, dws_ref, dbt_ref, dsg_ref):
        i = pl.program_id(0)

        @pl.when(i == 0)
        def _():
            sm_ref[...] = jnp.zeros_like(sm_ref)
            dws_ref[...] = jnp.zeros_like(dws_ref)
            dsg_ref[...] = jnp.zeros_like(dsg_ref)

        gc, hc, cin, x1, x2 = _conv_inputs(z_ref, hgc_ref, hhc_ref, i, cw, tm)
        w0, w1, w2 = cw_ref[0:1, :], cw_ref[1:2, :], cw_ref[2:3, :]
        cv = cb_ref[...] + w2 * cin + w1 * x1 + w0 * x2
        gb = z_ref[:, 0:cw].astype(F32)
        dyc = dy_ref[:, 0:cw].astype(F32)
        dz_ref[:, 0:cw] = (dyc * cv).astype(BF16)
        dcv = dyc * gb
        nxt = ndy_ref[...].astype(F32) * ngb_ref[...].astype(F32) * (i < nsteps - 1).astype(F32)
        row = lax.broadcasted_iota(jnp.int32, (tm, cw), 0)
        d1 = jnp.where(row == tm - 1, nxt[0:1], pltpu.roll(dcv, tm - 1, 0))
        d2 = jnp.where(row == tm - 1, nxt[1:2], jnp.where(row == tm - 2, nxt[0:1], pltpu.roll(dcv, tm - 2, 0)))
        dcin = w2 * dcv + w1 * d1 + w0 * d2
        dz_ref[:, cw:2 * cw] = (dcin * hc).astype(BF16)
        dz_ref[:, 2 * cw:3 * cw] = (dcin * gc).astype(BF16)
        sm_ref[0:1, :] += jnp.sum(dcv * x2, axis=0, keepdims=True)
        sm_ref[1:2, :] += jnp.sum(dcv * x1, axis=0, keepdims=True)
        sm_ref[2:3, :] += jnp.sum(dcv * cin, axis=0, keepdims=True)
        sm_ref[3:4, :] += jnp.sum(dcv, axis=0, keepdims=True)

        for h in range(heads):
            lo = h * GROUP
            vcol = slice(3 * cw + gw + lo, 3 * cw + gw + lo + GROUP)
            ucol = slice(3 * cw + lo, 3 * cw + lo + GROUP)
            vh = z_ref[:, vcol].astype(F32)
            rv = lax.rsqrt(jnp.mean(vh * vh, axis=-1, keepdims=True) + EPS)
            xh = vh * rv
            gvh = gv_ref[:, lo:lo + GROUP]
            vn = (xh * gvh).astype(BF16)
            w = _tril(ws_ref[h]).astype(BF16)
            dgv = jnp.zeros((1, GROUP), F32)
            for n in range(tm // GROUP):
                rows = slice(n * GROUP, (n + 1) * GROUP)
                sg = _dot_nn(w, vn[rows]) + bt_ref[:, h:h + 1]
                dyg = dy_ref[rows, cw + lo:cw + lo + GROUP].astype(F32)
                dsg = dyg * z_ref[rows, ucol].astype(F32)
                dz_ref[rows, ucol] = (dyg * sg).astype(BF16)
                dsgb = dsg.astype(BF16)
                dvn = _dot_tn(w, dsgb)
                dws_ref[h] += _dot_nt(dsgb, vn[rows])
                dsg_ref[:, lo:lo + GROUP] += dsg
                xhc = xh[rows]
                dgv = dgv + jnp.sum(dvn * xhc, axis=0, keepdims=True)
                dxh = dvn * gvh
                dv = rv[rows] * (dxh - xhc * jnp.mean(dxh * xhc, axis=-1, keepdims=True))
                dz_ref[rows, vcol] = dv.astype(BF16)
            sm_ref[4:5, lo:lo + GROUP] += dgv

        @pl.when(i == nsteps - 1)
        def _():
            for h in range(heads):
                dws_ref[h] = _tril(dws_ref[h])
                dbt_ref[:, h:h + 1] = jnp.sum(dsg_ref[:, h * GROUP:(h + 1) * GROUP], axis=-1, keepdims=True)

    fixed2 = lambda i: (0, 0)
    fixed3 = lambda i: (0, 0, 0)
    prev = lambda col: (lambda i: (jnp.maximum(i * hb - 1, 0), col))
    nxt_blk = lambda i: (jnp.minimum((i + 1) * hb, last_halo), 0)
    outs, job_outs = _call(
        name, body, (nsteps,),
        [pl.BlockSpec((tm, zc), lambda i: (i, 0)),
         pl.BlockSpec((HALO, cw), prev(1)), pl.BlockSpec((HALO, cw), prev(2)),
         pl.BlockSpec((HALO, cw), nxt_blk),
         pl.BlockSpec((tm, cw + gw), lambda i: (i, 0)), pl.BlockSpec((HALO, cw), nxt_blk),
         pl.BlockSpec(conv_w.shape, fixed2), pl.BlockSpec(conv_b.shape, fixed2),
         pl.BlockSpec(g_v.shape, fixed2), pl.BlockSpec(w_s.shape, fixed3), pl.BlockSpec(b_t.shape, fixed2)],
        [pl.BlockSpec((tm, zc), lambda i: (i, 0)), pl.BlockSpec((8, cw), fixed2),
         pl.BlockSpec(w_s.shape, fixed3), pl.BlockSpec(b_t.shape, fixed2)],
        [_sds((s, zc), BF16), _sds((8, cw), F32), _sds(w_s.shape, F32), _sds(b_t.shape, F32)],
        [z, z, z, z, dy, dy, conv_w, conv_b, g_v, w_s, b_t],
        scratch=[pltpu.VMEM((GROUP, gw), F32)], sem=("arbitrary",), jobs=jobs)
    return _ret(outs, job_outs, jobs, single=False)


def _softmax_rows(sc):
    e = jnp.exp(sc - jnp.max(sc, axis=-1, keepdims=True))
    return e / jnp.sum(e, axis=-1, keepdims=True)


def attn_fwd(name, q, k, v, tm=512, jobs=()):
    s, d = q.shape
    m = k.shape[0]
    hd = d // XA_HEADS
    scale = hd ** -0.5
    tm = _tile(s, tm, 8)

    def body(q_ref, k_ref, v_ref, o_ref):
        for h in range(XA_HEADS):
            cols = slice(h * hd, (h + 1) * hd)
            p = _softmax_rows(_dot_nt(q_ref[:, cols], k_ref[:, cols]) * scale)
            o_ref[:, cols] = _dot_nn(p.astype(BF16), v_ref[:, cols]).astype(BF16)

    outs, job_outs = _call(
        name, body, (s // tm,),
        [pl.BlockSpec((tm, d), lambda i: (i, 0)), pl.BlockSpec((m, d), lambda i: (0, 0)),
         pl.BlockSpec((m, d), lambda i: (0, 0))],
        [pl.BlockSpec((tm, d), lambda i: (i, 0))], [_sds((s, d), BF16)], [q, k, v], sem=("arbitrary",), jobs=jobs)
    return _ret(outs, job_outs, jobs)


def attn_bwd(name, q, k, v, do, tm=512):
    s, d = q.shape
    m = k.shape[0]
    hd = d // XA_HEADS
    scale = hd ** -0.5
    tm = _tile(s, tm, 8)

    def body(q_ref, k_ref, v_ref, do_ref, dq_ref, dk_ref, dv_ref):
        i = pl.program_id(0)

        @pl.when(i == 0)
        def _():
            dk_ref[...] = jnp.zeros_like(dk_ref)
            dv_ref[...] = jnp.zeros_like(dv_ref)

        for h in range(XA_HEADS):
            cols = slice(h * hd, (h + 1) * hd)
            qh = q_ref[:, cols]
            doh = do_ref[:, cols]
            p = _softmax_rows(_dot_nt(qh, k_ref[:, cols]) * scale)
            dp = _dot_nt(doh, v_ref[:, cols])
            ds = (p * (dp - jnp.sum(dp * p, axis=-1, keepdims=True)) * scale).astype(BF16)
            dq_ref[:, cols] = _dot_nn(ds, k_ref[:, cols]).astype(BF16)
            dk_ref[:, cols] += _dot_tn(ds, qh)
            dv_ref[:, cols] += _dot_tn(p.astype(BF16), doh)

    row = lambda i: (i, 0)
    fixed = lambda i: (0, 0)
    return _call(
        name, body, (s // tm,),
        [pl.BlockSpec((tm, d), row), pl.BlockSpec((m, d), fixed), pl.BlockSpec((m, d), fixed),
         pl.BlockSpec((tm, d), row)],
        [pl.BlockSpec((tm, d), row), pl.BlockSpec((m, d), fixed), pl.BlockSpec((m, d), fixed)],
        [_sds((s, d), BF16), _sds((m, d), F32), _sds((m, d), F32)], [q, k, v, do], sem=("arbitrary",))[0]


def _grid2(rows, cols, row_mult):
    tr, tc = _tile(rows, 512, row_mult), _tile(cols, 2048)
    return tr, tc, rows // tr, cols // tc


def cast_place(name, block, axis, place, column_half=None):
    r, c = block.shape
    if column_half is not None:
        c //= 2
    tr, tc, nbr, nbc = _grid2(r, c, 16)
    first = 0 if column_half is None else column_half * nbc
    if axis == 1:
        dst = lambda i, j, p: (i, j + p[0] * nbc)
    else:
        dst = lambda i, j, p: (i + p[0] * nbr, j)

    def body(p_ref, w_ref, out_ref):
        out_ref[...] = w_ref[...].astype(BF16)

    return pl.pallas_call(
        body, name=name,
        grid_spec=pltpu.PrefetchScalarGridSpec(
            num_scalar_prefetch=1, grid=(nbr, nbc),
            in_specs=[pl.BlockSpec((tr, tc), lambda i, j, p: (i, j + first))],
            out_specs=pl.BlockSpec((tr, tc), dst)),
        out_shape=_sds(_full_shape((r, c), axis), BF16),
        compiler_params=_params(("parallel", "parallel")),
    )(place, block)


def merge_column_halves(name, left, right):
    r, c = left.shape
    w = c // N_CHIPS
    tr = _tile(r, 512, 16)

    def body(l_ref, r_ref, out_ref):
        side = pl.program_id(1) % 2

        @pl.when(side == 0)
        def _():
            out_ref[...] = l_ref[...]

        @pl.when(side == 1)
        def _():
            out_ref[...] = r_ref[...]

    half = pl.BlockSpec((tr, w), lambda i, j: (i, j // 2))
    return _call(name, body, (r // tr, 2 * N_CHIPS), [half, half], [pl.BlockSpec((tr, w), lambda i, j: (i, j))],
                 [_sds((r, 2 * c), left.dtype)], [left, right], sem=("parallel", "arbitrary"))[0][0]


def pair_add(name, grad, peer, axis, place):
    hr, hc = peer.shape
    tr, tc, nbr, nbc = _grid2(hr, hc, 16)
    same = lambda i, j, p: (i, j)
    if grad.shape == peer.shape:
        mine = same
    elif axis == 1:
        mine = lambda i, j, p: (i + p[1] * nbr, j)
    else:
        mine = lambda i, j, p: (i, j + p[1] * nbc)

    def body(p_ref, g_ref, q_ref, out_ref):
        out_ref[...] = (g_ref[...].astype(F32) + q_ref[...].astype(F32)).astype(BF16)

    return pl.pallas_call(
        body, name=name,
        grid_spec=pltpu.PrefetchScalarGridSpec(
            num_scalar_prefetch=1, grid=(nbr, nbc),
            in_specs=[pl.BlockSpec((tr, tc), mine), pl.BlockSpec((tr, tc), same)],
            out_specs=pl.BlockSpec((tr, tc), same)),
        out_shape=_sds((hr, hc), BF16),
        compiler_params=_params(("parallel", "parallel")),
    )(place, grad, peer)


def cross_sum(name, part, land, axis, shape, place):
    _, sr, sc = land.shape
    tr, tc, nbr, nbc = _grid2(sr, sc, 16)
    if axis == 1:
        own = lambda i, j, p: (i, j + p[0] * nbc)
        dst = lambda i, j, p: (i + p[1] * nbr, j)
    else:
        own = lambda i, j, p: (i + p[0] * nbr, j)
        dst = lambda i, j, p: (i, j + p[1] * nbc)

    def body(p_ref, own_ref, land_ref, out_ref):
        out_ref[...] = ((own_ref[...].astype(F32) + land_ref[0].astype(F32))
                        + (land_ref[1].astype(F32) + land_ref[2].astype(F32)))

    return pl.pallas_call(
        body, name=name,
        grid_spec=pltpu.PrefetchScalarGridSpec(
            num_scalar_prefetch=1, grid=(nbr, nbc),
            in_specs=[pl.BlockSpec((tr, tc), own), pl.BlockSpec((3, tr, tc), lambda i, j, p: (0, i, j))],
            out_specs=pl.BlockSpec((tr, tc), dst)),
        out_shape=_sds(_block(shape, axis), F32),
        compiler_params=_params(("parallel", "parallel")),
    )(place, part, land)


def _adam_math(w, g, m, v):
    m = ADAM_B1 * m + (1.0 - ADAM_B1) * g
    v = ADAM_B2 * v + (1.0 - ADAM_B2) * (g * g)
    m_hat = m / (1.0 - ADAM_B1 ** ADAM_STEP)
    v_hat = v / (1.0 - ADAM_B2 ** ADAM_STEP)
    delta = -ADAM_LR * (m_hat / (jnp.sqrt(v_hat) + ADAM_EPS) + ADAM_WD * w)
    return delta, m, v


def adamw(name, w, g, m, v, jobs=()):
    r, c = w.shape
    tr, tc = _tile(r, 256, 8), _tile(c, 1408)

    def body(w_ref, g_ref, m_ref, v_ref, g_out, d_out, m_out, v_out):
        d, mm, vv = _adam_math(w_ref[...], g_ref[...], m_ref[...], v_ref[...])
        g_out[...] = g_ref[...]
        d_out[...] = d
        m_out[...] = mm
        v_out[...] = vv

    spec = pl.BlockSpec((tr, tc), lambda i, j: (i, j))
    outs, job_outs = _call(name, body, (r // tr, c // tc), [spec] * 4, [spec] * 4, [_sds((r, c), F32)] * 4,
                           [w, g, m, v], sem=("parallel", "parallel"), jobs=jobs)
    return _ret(outs, job_outs, jobs, single=False)


def small_sum(name, stacks):
    def body(*refs):
        for s_ref, out_ref in zip(refs[:len(stacks)], refs[len(stacks):]):
            acc = s_ref[0]
            for d in range(1, s_ref.shape[0]):
                acc = acc + s_ref[d]
            out_ref[...] = acc

    return pl.pallas_call(body, name=name, out_shape=[_sds(s.shape[1:], F32) for s in stacks])(*stacks)


WEIGHTS = ["g_ffn1", "w_ffn1_in", "w_ffn1_out", "g_mix", "w_mix_in", "conv_w", "conv_b", "g_gm_v", "w_spatial",
           "b_spatial", "w_mix_out", "g_xattn", "g_mem", "w_xq", "w_xk", "w_xv", "w_xo", "g_ffn2", "w_ffn2_in",
           "w_ffn2_out", "g_final"]
BIG = {"w_ffn1_in": 1, "w_ffn1_out": 0, "w_mix_in": 1, "w_mix_out": 0, "w_xq": 0, "w_xk": 0, "w_xv": 0, "w_xo": 0,
       "w_ffn2_in": 1, "w_ffn2_out": 0}
SMALL = [n for n in WEIGHTS if n not in BIG]
LATE_SMALL = ["g_ffn1"]
EARLY_SMALL = [n for n in SMALL if n not in LATE_SMALL]


def _pack(arrays):
    flat = jnp.concatenate([a.reshape(-1) for a in arrays])
    rows = -(-flat.shape[0] // 1024) * 8
    return jnp.pad(flat, (0, rows * 128 - flat.shape[0])).reshape(rows, 128)


def _unpack(buf, shapes):
    flat = buf.reshape(-1)
    out, pos = [], 0
    for shp in shapes:
        n = math.prod(shp)
        out.append(flat[pos:pos + n].reshape(shp))
        pos += n
    return out


def kernel(x, mem, g_ffn1, w_ffn1_in, w_ffn1_out, g_mix, w_mix_in, conv_w, conv_b, g_gm_v, w_spatial, b_spatial, w_mix_out, g_xattn, g_mem, w_xq, w_xk, w_xv, w_xo, g_ffn2, w_ffn2_in, w_ffn2_out, g_final, loss_target, m_g_ffn1, m_w_ffn1_in, m_w_ffn1_out, m_g_mix, m_w_mix_in, m_conv_w, m_conv_b, m_g_gm_v, m_w_spatial, m_b_spatial, m_w_mix_out, m_g_xattn, m_g_mem, m_w_xq, m_w_xk, m_w_xv, m_w_xo, m_g_ffn2, m_w_ffn2_in, m_w_ffn2_out, m_g_final, v_g_ffn1, v_w_ffn1_in, v_w_ffn1_out, v_g_mix, v_w_mix_in, v_conv_w, v_conv_b, v_g_gm_v, v_w_spatial, v_b_spatial, v_w_mix_out, v_g_xattn, v_g_mem, v_w_xq, v_w_xk, v_w_xv, v_w_xo, v_g_ffn2, v_w_ffn2_in, v_w_ffn2_out, v_g_final):
    given = dict(locals())
    wts = {n: given[n] for n in WEIGHTS}
    mom = {n: given["m_" + n] for n in WEIGHTS}
    var = {n: given["v_" + n] for n in WEIGHTS}

    xi, yi, ci = lax.axis_index("x"), lax.axis_index("y"), lax.axis_index("c")
    blk = 2 * xi + yi
    place = jnp.stack([blk, ci]).astype(jnp.int32)

    x2, mem2, tgt = x[0], mem[0], loss_target[0]
    w_s, b_t = w_spatial[0], b_spatial[0].T
    gf = g_final[None]

    rest = [n for n in BIG if n != "w_ffn1_in"]
    own = {n: cast_place("cast_" + n, wts[n][0], BIG[n], place) for n in rest}
    own_left = cast_place("cast_w_ffn1_in_left", wts["w_ffn1_in"][0], 1, place, column_half=0)
    own_right = cast_place("cast_w_ffn1_in_right", wts["w_ffn1_in"][0], 1, place, column_half=1)
    shape = {n: own[n].shape for n in rest}
    shape["w_ffn1_in"] = _full_shape(wts["w_ffn1_in"][0].shape, 1)
    full = {}

    def gather_now(name, arrays, axes, collective_id):
        job = gather_job([(a, ax, WHOLE, WHOLE) for a, ax in zip(arrays, axes)])
        return by_sequencer(name, job, "gather", collective_id)[1]

    _, (conv_taps,) = by_sequencer("gather_conv_taps", columns_job(jnp.pad(conv_w[0], ((0, 8 - CONV_K), (0, 0)))),
                                   "chips", 14)
    (w1_left,) = gather_now("gather_w_ffn1_in_left", [own_left], [1], 1)
    (w1_right,) = gather_now("gather_w_ffn1_in_right", [own_right], [1], 2)
    groups = [["w_ffn1_out"], ["w_mix_in", "w_mix_out"], ["w_xq", "w_xk", "w_xv", "w_xo"], ["w_ffn2_in"],
              ["w_ffn2_out"]]
    for g, names in enumerate(groups):
        got = gather_now("gather_" + "_".join(names), [own[n] for n in names], [BIG[n] for n in names], 3 + g)
        full.update(zip(names, got))

    half_cols = dict(tm=512, tn=shape["w_ffn1_in"][1] // (2 * N_CHIPS), stride=2, compact=True)
    n1, r1 = rmsnorm_fwd("norm1", x2, g_ffn1)
    halves = swiglu_fwd("ffn1_in_left", n1, w1_left, phase=0, **half_cols)
    gu1, a1 = swiglu_fwd("ffn1_in_right", n1, w1_right, phase=1, prev=halves, **half_cols)
    h1 = mm_nn_resid("ffn1_out", a1, full["w_ffn1_out"], x2, 0.5, tm=512, tk=5632)
    n2, r2 = rmsnorm_fwd("norm2", h1, g_mix)
    z = mm_nn("mix_in", n2, full["w_mix_in"], BF16)
    ycat = mixer_fwd("mixer", z, conv_taps, conv_b, g_gm_v, w_s, b_t)
    h2 = mm_nn_resid("mix_out", ycat, full["w_mix_out"], h1, 1.0, tk=2048)
    n3, r3 = rmsnorm_fwd("norm3", h2, g_xattn)
    mem2, h2 = lax.optimization_barrier((mem2, h2))
    mn, rm = rmsnorm_fwd("norm_mem", mem2, g_mem)
    q = mm_nn("xq", n3, full["w_xq"], BF16)
    k = mm_nn("xk", mn, full["w_xk"], BF16)
    v = mm_nn("xv", mn, full["w_xv"], BF16)
    o = attn_fwd("attn", q, k, v)
    h3 = mm_nn_resid("xo", o, full["w_xo"], h2, 1.0, tk=2048)
    n4, r4 = rmsnorm_fwd("norm4", h3, g_ffn2)
    gu2, a2 = swiglu_fwd("ffn2_in", n4, full["w_ffn2_in"])
    h4 = mm_nn_resid("ffn2_out", a2, full["w_ffn2_out"], h3, 0.5, tm=512, tk=5632)
    loss_blk, dh4, dh4b, dg_final = loss_head("loss_head", h4, gf, tgt)

    dw, part, land, half, grads = {}, {}, {}, {}, {}
    launched = []

    def send_pair(*names):
        return pair_job([dw[n] for n in names], [BIG[n] for n in names])

    def take_pair(names, got):
        for n, p in zip(names, got):
            part[n] = pair_add("pair_add_" + n, dw[n], p, BIG[n], place)

    def start_cross(*names):
        job = cross_job([(part[n], BIG[n], shape[n], None, WHOLE) for n in names])
        kept, landed = by_sequencer("cross_" + "_".join(names), job, "chips", 8 + len(launched) % 2)
        launched.append(names)
        for n, p, l in zip(names, kept, landed):
            part[n], land[n] = p, l

    def finish_cross(*names):
        for n in names:
            half[n] = cross_sum("cross_sum_" + n, part[n], land[n], BIG[n], shape[n], place)

    def send_final(*names):
        return final_job([half[n] for n in names], [BIG[n] for n in names], [shape[n] for n in names])

    delta, new_m, new_v = {}, {}, {}

    def take_final(names, got):
        for n, g in zip(names, got):
            grads[n], delta[n], new_m[n], new_v[n] = adamw("adamw_" + n, wts[n][0], g, mom[n][0], var[n][0])

    dgu2 = swiglu_bwd("ffn2_dact", dh4b, full["w_ffn2_out"], gu2, 0.5)
    dw["w_ffn2_in"] = mm_tn_pair("ffn2_dwin", n4, dgu2, BF16)
    dw["w_ffn2_out"], (got,) = mm_tn("ffn2_dwout", a2, dh4b, BF16, scale=0.5, jobs=[send_pair("w_ffn2_in")])
    take_pair(["w_ffn2_in"], got)
    start_cross("w_ffn2_in")
    (dh3, dh3b, dg_ffn2), (got,) = mm_nt_norm_bwd("ffn2_dn", dgu2, full["w_ffn2_in"], h3, r4, g_ffn2, dh4,
                                                  jobs=[send_pair("w_ffn2_out")])
    take_pair(["w_ffn2_out"], got)
    start_cross("w_ffn2_out")

    dw["w_xo"] = mm_tn("xo_dw", o, dh3b, BF16)
    finish_cross("w_ffn2_in")
    do, (got,) = mm_nt("xo_dx", dh3b, full["w_xo"], BF16, jobs=[send_final("w_ffn2_in")])
    take_final(["w_ffn2_in"], got)
    dq, dk, dv = attn_bwd("attn_bwd", q, k, v, do)
    dkb, dvb = dk.astype(BF16), dv.astype(BF16)
    dw["w_xq"] = mm_tn("xq_dw", n3, dq, BF16)
    dh2, dh2b, dg_xattn = mm_nt_norm_bwd("xq_dx", dq, full["w_xq"], h2, r3, g_xattn, dh3, tk=1024)
    dw["w_xk"] = mm_tn("xk_dw", mn, dkb, BF16)
    dw["w_xv"] = mm_tn("xv_dw", mn, dvb, BF16)
    dmn_k = mm_nt("xk_dx", dkb, full["w_xk"], F32)
    dmn_v = mm_nt("xv_dx", dvb, full["w_xv"], F32)
    dg_mem = gain_grad("norm_mem_bwd", dmn_k, dmn_v, mem2, rm)

    finish_cross("w_ffn2_out")
    dw["w_mix_out"], (got,) = mm_tn("mix_out_dw", ycat, dh2b, BF16, jobs=[send_final("w_ffn2_out")])
    take_final(["w_ffn2_out"], got)
    attn_names = ["w_xo", "w_xq", "w_xk", "w_xv", "w_mix_out"]
    dycat, (got,) = mm_nt("mix_out_dx", dh2b, full["w_mix_out"], BF16, jobs=[send_pair(*attn_names)])
    take_pair(attn_names, got)
    start_cross(*attn_names)
    dz, dsmall, dws, dbt = mixer_bwd("mixer_bwd", z, dycat, conv_taps, conv_b, g_gm_v, w_s, b_t)
    dw["w_mix_in"] = mm_tn("mix_in_dw", n2, dz, BF16)
    (dh1, dh1b, dg_mix), (got,) = mm_nt_norm_bwd("mix_in_dx", dz, full["w_mix_in"], h1, r2, g_mix, dh2, tk=1280,
                                                 jobs=[send_pair("w_mix_in")])
    take_pair(["w_mix_in"], got)
    start_cross("w_mix_in")

    finish_cross(*attn_names)
    dw["w_ffn1_out"], (got,) = mm_tn("ffn1_dwout", a1, dh1b, BF16, scale=0.5, jobs=[send_final(*attn_names)])
    take_final(attn_names, got)
    early = {"g_mix": dg_mix, "conv_w": dsmall[0:CONV_K], "conv_b": dsmall[3:4], "g_gm_v": dsmall[4:5],
             "w_spatial": dws, "b_spatial": dbt.T, "g_xattn": dg_xattn, "g_mem": dg_mem, "g_ffn2": dg_ffn2,
             "g_final": dg_final}
    finish_cross("w_mix_in")
    dgu1, (got_p, got_f, (early_all,)) = swiglu_bwd(
        "ffn1_dact", dh1b, full["w_ffn1_out"], gu1, 0.5,
        jobs=[send_pair("w_ffn1_out"), send_final("w_mix_in"), stack_job(_pack([early[n] for n in EARLY_SMALL]))])
    take_pair(["w_ffn1_out"], got_p)
    start_cross("w_ffn1_out")
    take_final(["w_mix_in"], got_f)
    theirs = mm_tn_pair_half("ffn1_dwin_theirs", n1, dgu1, BF16, place, False)
    mine, ((from_sibling,),) = mm_tn_pair_half("ffn1_dwin_mine", n1, dgu1, BF16, place, True,
                                               jobs=[pair_job([theirs], [1], is_half=True)])
    part["w_ffn1_in"] = pair_add("pair_add_w_ffn1_in", mine, from_sibling, 1, place)
    start_cross("w_ffn1_in")
    finish_cross("w_ffn1_out")
    dn1, (got,) = mm_nt_pair_halves("ffn1_dn", dgu1, w1_left, w1_right, F32, jobs=[send_final("w_ffn1_out")])
    take_final(["w_ffn1_out"], got)
    dx, _, dg_ffn1 = rmsnorm_bwd("norm1_bwd", dn1, x2, r1, g_ffn1, dh1)
    finish_cross("w_ffn1_in")
    got, (late_all,) = comm_only("tail_final", [send_final("w_ffn1_in"), stack_job(_pack([dg_ffn1]))])
    take_final(["w_ffn1_in"], got)

    early_sum, late_sum = small_sum("small_sum", [early_all, late_all])
    for n, g in zip(EARLY_SMALL, _unpack(early_sum, [early[n].shape for n in EARLY_SMALL])):
        grads[n] = g
    grads["g_ffn1"] = _unpack(late_sum, [dg_ffn1.shape])[0]
    taps_cols = conv_w.shape[2]
    grads["conv_w"] = lax.dynamic_slice_in_dim(grads["conv_w"], blk * taps_cols, taps_cols, axis=1)
    packed = [_pack([src[n] for n in SMALL]) for src in (wts, grads, mom, var)]
    own_shapes = [wts[n].shape for n in SMALL]
    for dst, buf in zip((delta, new_m, new_v), adamw("adamw_small", *packed)[1:]):
        for n, a in zip(SMALL, _unpack(buf, own_shapes)):
            dst[n] = a

    loss = lax.psum(loss_blk[0, 0], ("x", "y", "c"))
    outs = [loss, dx[None]]
    for group in (grads, delta, new_m, new_v):
        outs += [group[n].reshape(wts[n].shape) for n in WEIGHTS]
    return tuple(outs)
```

```python
import math

import jax
import jax.numpy as jnp
from jax import lax
from jax.experimental import pallas as pl
from jax.experimental.pallas import tpu as pltpu
from jax.experimental.pallas import tpu_sc as plsc

F32 = jnp.float32
BF16 = jnp.bfloat16
EPS = 1e-6
GROUP = 128
XA_HEADS = 4
CONV_K = 3
N_CHIPS = 4
VMEM_LIMIT_BYTES = 56 * 1024 * 1024

ADAM_LR = 0.001
ADAM_B1 = 0.9
ADAM_B2 = 0.999
ADAM_EPS = 1e-08
ADAM_WD = 0.01
ADAM_STEP = 10

MESH = pl.DeviceIdType.MESH
ANY = pl.BlockSpec(memory_space=pl.ANY)

GATHER_IDS = tuple(range(1, 8))
CROSS_IDS = (8, 9)
SIBLING_IDS = (10, 11)
STACK_IDS = (12, 13)
TAPS_ID = 14


def _tile(dim, pref, mult=128):
    if dim <= pref:
        return dim
    t = (pref // mult) * mult
    while t >= mult:
        if dim % t == 0:
            return t
        t -= mult
    raise ValueError(f"no tile for {dim} under {pref}")


def _params(sem):
    return pltpu.CompilerParams(dimension_semantics=sem, vmem_limit_bytes=VMEM_LIMIT_BYTES)


def _sds(shape, dtype):
    return jax.ShapeDtypeStruct(shape, dtype)


def _dot_nn(a, b):
    return jnp.dot(a, b, preferred_element_type=F32)


def _dot_nt(a, b):
    return lax.dot_general(a, b, (((1,), (1,)), ((), ())), preferred_element_type=F32)


def _dot_tn(a, b):
    return lax.dot_general(a, b, (((0,), (0,)), ((), ())), preferred_element_type=F32)


class Job:
    def __init__(self, inputs, out_shapes, aliases, sems, start, middle, finish):
        self.inputs, self.out_shapes, self.aliases, self.sems = inputs, out_shapes, aliases, sems
        self.start, self.middle, self.finish = start, middle, finish


def _place():
    x, y, c = lax.axis_index("x"), lax.axis_index("y"), lax.axis_index("c")
    chips = [(1 - x, y), (x, 1 - y), (1 - x, 1 - y)]
    return x, y, c, chips


def _ds(start, size, lane):
    if not isinstance(start, int):
        start = pl.multiple_of(start, 128 if lane else 16)
    return pl.ds(start, size)


WHOLE = (0, 1, 1)


def _window(ref, axis, shape, blk=None, half=None, sub=WHOLE, within=WHOLE):
    n = shape[axis] // N_CHIPS
    hs = shape[1 - axis] // 2
    idx = [slice(None), slice(None)]
    if blk is not None:
        b_first, b_count, b_pieces = within
        b_ext = n // b_pieces
        idx[axis] = _ds(blk * n + b_first * b_ext, b_count * b_ext, axis == 1)
    first, count, pieces = sub
    ext = hs // pieces
    if half is not None:
        idx[1 - axis] = _ds(half * hs + first * ext, count * ext, axis == 0)
    elif pieces > 1:
        idx[1 - axis] = _ds(first * ext, count * ext, axis == 0)
    return ref.at[tuple(idx)]


def _remote(src, dst, send_sem, recv_sem, dev):
    return pltpu.make_async_remote_copy(src_ref=src, dst_ref=dst, send_sem=send_sem, recv_sem=recv_sem,
                                        device_id=dev, device_id_type=MESH)


def _full_shape(block_shape, axis):
    out = list(block_shape)
    out[axis] *= N_CHIPS
    return tuple(out)


def _half_all(shape, axis):
    out = list(shape)
    out[1 - axis] //= 2
    return tuple(out)


def _block(shape, axis):
    out = list(shape)
    out[axis] //= N_CHIPS
    return tuple(out)


def _half_block(shape, axis):
    return _half_all(_block(shape, axis), axis)


def gather_job(items):
    nw = len(items)
    shapes = [item[0].shape for item in items]
    n_sem = 8

    def parts(sub):
        first, count, pieces = sub
        return (2 * first, count, 2 * pieces), (2 * first + count, count, 2 * pieces)

    def start(pos, ins, outs, sems):
        x, y, c, chips = pos
        for w, (_, ax, sub, within) in enumerate(items):
            mine = _window(outs[w], ax, shapes[w], blk=2 * x + y, half=c, sub=sub, within=within)
            for j in range(2):
                _remote(mine, mine, sems[0].at[n_sem * w + j], sems[1].at[n_sem * w + j], (*chips[j], c)).start()

    def middle(pos, ins, outs, sems):
        x, y, c, chips = pos
        for w, (_, ax, sub, within) in enumerate(items):
            for j in range(2):
                cx, cy = chips[j]
                landed = _window(outs[w], ax, shapes[w], blk=2 * cx + cy, half=c, sub=sub, within=within)
                _remote(landed, landed, sems[0].at[n_sem * w + j], sems[1].at[n_sem * w + j], (cx, cy, c)).wait_recv()
                part = _window(outs[w], ax, shapes[w], blk=2 * cx + cy, half=c, sub=parts(sub)[j], within=within)
                _remote(part, part, sems[0].at[n_sem * w + 2 + j], sems[1].at[n_sem * w + 2 + j],
                        (*chips[1 - j], c)).start()
                _remote(landed, landed, sems[0].at[n_sem * w + 4 + j], sems[1].at[n_sem * w + 4 + j],
                        (x, y, 1 - c)).start()

    def finish(pos, ins, outs, sems):
        x, y, c, chips = pos
        sib = (x, y, 1 - c)
        for w, (_, ax, sub, within) in enumerate(items):
            dx, dy = chips[2]
            for j in range(2):
                part = _window(outs[w], ax, shapes[w], blk=2 * dx + dy, half=c, sub=parts(sub)[j], within=within)
                cp = _remote(part, part, sems[0].at[n_sem * w + 2 + j], sems[1].at[n_sem * w + 2 + j], sib)
                cp.wait_recv()
                cp.wait_send()
            diag = _window(outs[w], ax, shapes[w], blk=2 * dx + dy, half=c, sub=sub, within=within)
            _remote(diag, diag, sems[0].at[n_sem * w + 6], sems[1].at[n_sem * w + 6], sib).start()
        for w, (_, ax, sub, within) in enumerate(items):
            for j, (cx, cy) in enumerate(chips):
                passed = _window(outs[w], ax, shapes[w], blk=2 * cx + cy, half=1 - c, sub=sub, within=within)
                cp = _remote(passed, passed, sems[0].at[n_sem * w + 4 + j], sems[1].at[n_sem * w + 4 + j], sib)
                cp.wait_recv()
                cp.wait_send()
            mine = _window(outs[w], ax, shapes[w], blk=2 * x + y, half=c, sub=sub, within=within)
            for j in range(2):
                _remote(mine, mine, sems[0].at[n_sem * w + j], sems[1].at[n_sem * w + j], sib).wait_send()

    sems = [pltpu.SemaphoreType.DMA((n_sem * nw,)), pltpu.SemaphoreType.DMA((n_sem * nw,))]
    return Job([item[0] for item in items], [_sds(item[0].shape, item[0].dtype) for item in items],
               {w: w for w in range(nw)}, sems, start, middle, finish)


def by_sequencer(name, job, peers, collective_id):
    ins = [jax.new_ref(a, memory_space=pltpu.MemorySpace.HBM) for a in job.inputs]
    from_input = {o: i for i, o in job.aliases.items()}
    outs = [ins[from_input[k]] if k in from_input else jax.empty_ref(s, memory_space=pltpu.MemorySpace.HBM)
            for k, s in enumerate(job.out_shapes)]

    @pl.kernel(mesh=plsc.ScalarSubcoreMesh(axis_name="sequencer", num_cores=1), name=name,
               scratch_types=tuple(job.sems), compiler_params=pltpu.CompilerParams(collective_id=collective_id))
    def launch(*sems):
        pos = _place()
        x, y, c, chips = pos
        devs = {"sibling": [(x, y, 1 - c)],
                "chips": [(cx, cy, c) for cx, cy in chips],
                "gather": [(*chips[0], c), (*chips[1], c), (x, y, 1 - c)],
                "all": [(px, py, pc) for px in (x, 1 - x) for py in (y, 1 - y) for pc in (c, 1 - c)][1:]}[peers]
        barrier = pltpu.get_barrier_semaphore()
        for dev in devs:
            pl.semaphore_signal(barrier, inc=1, device_id=dev, device_id_type=MESH)
        pl.semaphore_wait(barrier, len(devs))
        for phase in (job.start, job.middle, job.finish):
            if phase is not None:
                phase(pos, ins, outs, list(sems))

    launch()
    kept = [r[...] for r in ins]
    return kept, [kept[from_input[k]] if k in from_input else r[...] for k, r in enumerate(outs)]


def pair_job(grads, axes, is_half=False):
    nw = len(grads)
    shapes = [g.shape for g in grads]

    def start(pos, ins, outs, sems):
        x, y, c, _ = pos
        for w in range(nw):
            src = ins[w] if is_half else _window(ins[w], axes[w], shapes[w], half=1 - c)
            _remote(src, outs[w], sems[0].at[w], sems[1].at[w], (x, y, 1 - c)).start()

    def finish(pos, ins, outs, sems):
        x, y, c, _ = pos
        for w in range(nw):
            cp = _remote(outs[w], outs[w], sems[0].at[w], sems[1].at[w], (x, y, 1 - c))
            cp.wait_recv()
            cp.wait_send()

    sems = [pltpu.SemaphoreType.DMA((nw,)), pltpu.SemaphoreType.DMA((nw,))]
    out_shapes = [_sds(s if is_half else _half_all(s, a), BF16) for s, a in zip(shapes, axes)]
    return Job(list(grads), out_shapes, {}, sems, start, None, finish)


def cross_job(items):
    nw = len(items)
    inputs, aliases = [], {}
    for w, (part, ax, shape, prev, sub) in enumerate(items):
        inputs.append(part)
        if prev is not None:
            aliases[len(inputs)] = w
            inputs.append(prev)

    def copies(pos, ins, outs, sems):
        x, y, c, chips = pos
        k = 0
        for w, (_, ax, shape, prev, sub) in enumerate(items):
            src = ins[k]
            k += 2 if prev is not None else 1
            for j, (cx, cy) in enumerate(chips):
                slot = _window(outs[w].at[j], ax, shape, sub=sub)
                yield (_remote(_window(src, ax, shape, blk=2 * cx + cy, sub=sub), slot,
                               sems[0].at[3 * w + j], sems[1].at[3 * w + j], (cx, cy, c)),
                       _remote(slot, slot, sems[0].at[3 * w + j], sems[1].at[3 * w + j], (cx, cy, c)))

    def start(pos, ins, outs, sems):
        for send, _ in copies(pos, ins, outs, sems):
            send.start()

    def finish(pos, ins, outs, sems):
        for send, recv in copies(pos, ins, outs, sems):
            recv.wait_recv()
            send.wait_send()

    sems = [pltpu.SemaphoreType.DMA((3 * nw,)), pltpu.SemaphoreType.DMA((3 * nw,))]
    out_shapes = [_sds((3,) + _half_block(shape, ax), BF16) for _, ax, shape, _, _ in items]
    return Job(inputs, out_shapes, aliases, sems, start, None, finish)


def final_job(blocks, axes, shapes):
    nw = len(blocks)

    def start(pos, ins, outs, sems):
        x, y, c, _ = pos
        for w in range(nw):
            mine = _window(outs[w], axes[w], shapes[w], half=c)
            _remote(mine, mine, sems[0].at[w], sems[1].at[w], (x, y, 1 - c)).start()

    def finish(pos, ins, outs, sems):
        x, y, c, _ = pos
        for w in range(nw):
            theirs = _window(outs[w], axes[w], shapes[w], half=1 - c)
            cp = _remote(theirs, theirs, sems[0].at[w], sems[1].at[w], (x, y, 1 - c))
            cp.wait_recv()
            cp.wait_send()

    sems = [pltpu.SemaphoreType.DMA((nw,)), pltpu.SemaphoreType.DMA((nw,))]
    return Job(list(blocks), [_sds(b.shape, b.dtype) for b in blocks], {w: w for w in range(nw)}, sems, start, None,
               finish)


def stack_job(small):
    def peers(pos):
        x, y, c, _ = pos
        for k in range(1, 8):
            yield k - 1, (1 - x if k & 4 else x, 1 - y if k & 2 else y, 1 - c if k & 1 else c)

    def start(pos, ins, outs, sems):
        x, y, c, _ = pos
        mine = outs[0].at[4 * x + 2 * y + c]
        pltpu.make_async_copy(ins[0], mine, sems[2]).start()
        for k, dev in peers(pos):
            _remote(ins[0], mine, sems[0].at[k], sems[1].at[k], dev).start()

    def finish(pos, ins, outs, sems):
        x, y, c, _ = pos
        for k, (px, py, pc) in peers(pos):
            slot = outs[0].at[4 * px + 2 * py + pc]
            cp = _remote(slot, slot, sems[0].at[k], sems[1].at[k], (px, py, pc))
            cp.wait_recv()
            cp.wait_send()
        pltpu.make_async_copy(ins[0], outs[0].at[4 * x + 2 * y + c], sems[2]).wait()

    sems = [pltpu.SemaphoreType.DMA((7,)), pltpu.SemaphoreType.DMA((7,)), pltpu.SemaphoreType.DMA]
    return Job([small], [_sds((8,) + small.shape, small.dtype)], {}, sems, start, None, finish)


def columns_job(block):
    cols = block.shape[1]
    place = lambda out, b: out.at[:, _ds(b * cols, cols, True)]

    def start(pos, ins, outs, sems):
        x, y, c, chips = pos
        pltpu.make_async_copy(ins[0], place(outs[0], 2 * x + y), sems[2]).start()
        for j, (cx, cy) in enumerate(chips):
            _remote(ins[0], place(outs[0], 2 * x + y), sems[0].at[j], sems[1].at[j], (cx, cy, c)).start()

    def finish(pos, ins, outs, sems):
        x, y, c, chips = pos
        for j, (cx, cy) in enumerate(chips):
            got = place(outs[0], 2 * cx + cy)
            cp = _remote(got, got, sems[0].at[j], sems[1].at[j], (cx, cy, c))
            cp.wait_recv()
            cp.wait_send()
        pltpu.make_async_copy(ins[0], place(outs[0], 2 * x + y), sems[2]).wait()

    sems = [pltpu.SemaphoreType.DMA((3,)), pltpu.SemaphoreType.DMA((3,)), pltpu.SemaphoreType.DMA]
    return Job([block], [_sds((block.shape[0], N_CHIPS * cols), block.dtype)], {}, sems, start, None, finish)


def _call(name, body, grid, in_specs, out_specs, out_shape, args, scratch=(), sem=None, jobs=(), place=None,
          carried=None):
    n_in, n_out, n_sc = len(args), len(out_shape), len(scratch)
    carried = dict(carried or {})

    def launch(fn, in_specs, out_specs, out_shape, scratch, aliases, sem, operands):
        if place is None:
            return pl.pallas_call(
                fn, name=name, grid=grid, in_specs=in_specs, out_specs=out_specs, out_shape=out_shape,
                scratch_shapes=scratch, input_output_aliases=aliases, compiler_params=_params(sem))(*operands)
        spec = pltpu.PrefetchScalarGridSpec(num_scalar_prefetch=1, grid=grid, in_specs=in_specs,
                                            out_specs=out_specs, scratch_shapes=scratch)
        return pl.pallas_call(
            lambda p_ref, *refs: fn(*refs), name=name, grid_spec=spec, out_shape=out_shape,
            input_output_aliases={k + 1: v for k, v in aliases.items()}, compiler_params=_params(sem),
        )(place, *operands)

    if not jobs:
        outs = launch(body, list(in_specs), list(out_specs), list(out_shape), list(scratch), carried, sem, args)
        return list(outs), []

    total = math.prod(grid) if grid else 1
    mid = min(total - 1, (2 * total) // 3)

    def split(refs, start, counts):
        out = []
        for n in counts:
            out.append(refs[start:start + n])
            start += n
        return out, start

    def wrapped(*refs):
        c_in = refs[:n_in]
        j_ins, p = split(refs, n_in, [len(j.inputs) for j in jobs])
        c_out = refs[p:p + n_out]
        j_outs, p = split(refs, p + n_out, [len(j.out_shapes) for j in jobs])
        c_sc = refs[p:p + n_sc]
        j_sems, p = split(refs, p + n_sc, [len(j.sems) for j in jobs])
        pos = _place()
        step = 0
        for axis, extent in enumerate(grid):
            step = step * extent + pl.program_id(axis)

        def run(phase):
            for j, ins, outs, sems in zip(jobs, j_ins, j_outs, j_sems):
                fn = getattr(j, phase)
                if fn is not None:
                    fn(pos, ins, outs, sems)

        if total == 1:
            run("start")
            body(*c_in, *c_out, *c_sc)
            run("middle")
            run("finish")
            return
        pl.when(step == 0)(lambda: run("start"))
        body(*c_in, *c_out, *c_sc)
        if any(j.middle is not None for j in jobs):
            pl.when(step == mid)(lambda: run("middle"))
        pl.when(step == total - 1)(lambda: run("finish"))

    aliases, in_at, out_at = carried, n_in, n_out
    for j in jobs:
        for src, dst in j.aliases.items():
            aliases[in_at + src] = out_at + dst
        in_at += len(j.inputs)
        out_at += len(j.out_shapes)
    outs = launch(
        wrapped, list(in_specs) + [ANY] * (in_at - n_in), list(out_specs) + [ANY] * (out_at - n_out),
        list(out_shape) + [s for j in jobs for s in j.out_shapes],
        list(scratch) + [s for j in jobs for s in j.sems], aliases, ("arbitrary",) * len(grid),
        [*args, *[a for j in jobs for a in j.inputs]])
    job_outs, p = split(outs, n_out, [len(j.out_shapes) for j in jobs])
    return list(outs[:n_out]), [list(o) for o in job_outs]


def comm_only(name, jobs):
    def body(dummy_ref, out_ref):
        out_ref[...] = dummy_ref[...]

    dummy = jnp.zeros((8, 128), F32)
    spec = pl.BlockSpec((8, 128), lambda: (0, 0))
    return _call(name, body, (), [spec], [spec], [_sds((8, 128), F32)], [dummy], jobs=jobs)[1]


def _ret(outs, job_outs, jobs, single=True):
    res = outs[0] if single else outs
    return (res, job_outs) if jobs else res


def rmsnorm_fwd(name, x, g, jobs=()):
    s, d = x.shape
    tm = _tile(s, 512, 8)

    def body(x_ref, g_ref, n_ref, r_ref):
        xv = x_ref[...]
        r = lax.rsqrt(jnp.mean(xv * xv, axis=-1, keepdims=True) + EPS)
        n_ref[...] = (xv * r * g_ref[...]).astype(BF16)
        r_ref[...] = r

    row = lambda i: (i, 0)
    outs, job_outs = _call(
        name, body, (s // tm,),
        [pl.BlockSpec((tm, d), row), pl.BlockSpec((1, d), lambda i: (0, 0))],
        [pl.BlockSpec((tm, d), row), pl.BlockSpec((tm, 1), row)],
        [_sds((s, d), BF16), _sds((s, 1), F32)], [x, g], sem=("arbitrary",), jobs=jobs)
    return _ret(outs, job_outs, jobs, single=False)


def rmsnorm_bwd(name, dn, x, r, g, dh_in, jobs=()):
    s, d = x.shape
    tm = _tile(s, 512, 8)

    def body(dn_ref, x_ref, r_ref, g_ref, dh_ref, out_ref, outb_ref, dg_ref):
        i = pl.program_id(0)
        xh = x_ref[...] * r_ref[...]
        dnv = dn_ref[...]
        dxh = dnv * g_ref[...]
        dx = r_ref[...] * (dxh - xh * jnp.mean(dxh * xh, axis=-1, keepdims=True))
        out = dh_ref[...] + dx
        out_ref[...] = out
        outb_ref[...] = out.astype(BF16)
        part = jnp.sum(dnv * xh, axis=0, keepdims=True)

        @pl.when(i == 0)
        def _():
            dg_ref[...] = part

        @pl.when(i > 0)
        def _():
            dg_ref[...] += part

    row = lambda i: (i, 0)
    fixed = lambda i: (0, 0)
    outs, job_outs = _call(
        name, body, (s // tm,),
        [pl.BlockSpec((tm, d), row), pl.BlockSpec((tm, d), row), pl.BlockSpec((tm, 1), row),
         pl.BlockSpec((1, d), fixed), pl.BlockSpec((tm, d), row)],
        [pl.BlockSpec((tm, d), row), pl.BlockSpec((tm, d), row), pl.BlockSpec((1, d), fixed)],
        [_sds((s, d), F32), _sds((s, d), BF16), _sds((1, d), F32)], [dn, x, r, g, dh_in],
        sem=("arbitrary",), jobs=jobs)
    return _ret(outs, job_outs, jobs, single=False)


def gain_grad(name, dn_a, dn_b, x, r):
    s, d = x.shape
    tm = _tile(s, 512, 8)

    def body(a_ref, b_ref, x_ref, r_ref, dg_ref):
        i = pl.program_id(0)
        part = jnp.sum((a_ref[...] + b_ref[...]) * (x_ref[...] * r_ref[...]), axis=0, keepdims=True)

        @pl.when(i == 0)
        def _():
            dg_ref[...] = part

        @pl.when(i > 0)
        def _():
            dg_ref[...] += part

    row = lambda i: (i, 0)
    return _call(
        name, body, (s // tm,),
        [pl.BlockSpec((tm, d), row), pl.BlockSpec((tm, d), row), pl.BlockSpec((tm, d), row),
         pl.BlockSpec((tm, 1), row)],
        [pl.BlockSpec((1, d), lambda i: (0, 0))], [_sds((1, d), F32)], [dn_a, dn_b, x, r],
        sem=("arbitrary",))[0][0]


def loss_head(name, h, g, target):
    s, d = h.shape
    tm = _tile(s, 512, 8)
    nsteps = s // tm

    def body(h_ref, g_ref, t_ref, loss_ref, dh_ref, dhb_ref, dg_ref, sq_ref):
        i = pl.program_id(0)
        hv = h_ref[...]
        gv = g_ref[...]
        r = lax.rsqrt(jnp.mean(hv * hv, axis=-1, keepdims=True) + EPS)
        xh = hv * r
        err = xh * gv - t_ref[...]
        dy = err * (1.0 / d)
        dxh = dy * gv
        dh = r * (dxh - xh * jnp.mean(dxh * xh, axis=-1, keepdims=True))
        dh_ref[...] = dh
        dhb_ref[...] = dh.astype(BF16)
        dg_part = jnp.sum(dy * xh, axis=0, keepdims=True)
        sq_part = jnp.sum(err * err, axis=0, keepdims=True)

        @pl.when(i == 0)
        def _():
            dg_ref[...] = dg_part
            sq_ref[...] = sq_part

        @pl.when(i > 0)
        def _():
            dg_ref[...] += dg_part
            sq_ref[...] += sq_part

        @pl.when(i == nsteps - 1)
        def _():
            total = jnp.sum(sq_ref[...], axis=-1, keepdims=True) * (0.5 / d)
            loss_ref[...] = jnp.broadcast_to(total, loss_ref.shape)

    row = lambda i: (i, 0)
    fixed = lambda i: (0, 0)
    return _call(
        name, body, (nsteps,),
        [pl.BlockSpec((tm, d), row), pl.BlockSpec((1, d), fixed), pl.BlockSpec((tm, d), row)],
        [pl.BlockSpec((8, 128), fixed), pl.BlockSpec((tm, d), row), pl.BlockSpec((tm, d), row),
         pl.BlockSpec((1, d), fixed)],
        [_sds((8, 128), F32), _sds((s, d), F32), _sds((s, d), BF16), _sds((1, d), F32)], [h, g, target],
        scratch=[pltpu.VMEM((1, d), F32)], sem=("arbitrary",))[0]


def _mm(name, grid, in_arrays, in_specs, out_shapes, out_specs, acc_tile, dot, epilogue, jobs=(), place=None):
    nk = grid[2]
    n_in = len(in_arrays)
    n_out = len(out_shapes)

    def body(*refs):
        ins, outs = refs[:n_in], refs[n_in:n_in + n_out]
        if nk == 1:
            epilogue(dot(*ins), ins, outs)
            return
        acc = refs[n_in + n_out]
        k = pl.program_id(2)

        @pl.when(k == 0)
        def _():
            acc[...] = dot(*ins)

        @pl.when(jnp.logical_and(k > 0, k < nk - 1))
        def _():
            acc[...] += dot(*ins)

        @pl.when(k == nk - 1)
        def _():
            epilogue(acc[...] + dot(*ins), ins, outs)

    scratch = [pltpu.VMEM(acc_tile, F32)] if nk > 1 else []
    outs, job_outs = _call(name, body, grid, in_specs, out_specs, out_shapes, in_arrays, scratch=scratch,
                           sem=("parallel", "parallel", "arbitrary"), jobs=jobs, place=place)
    return _ret(outs, job_outs, jobs)


def _store(scale, dtype):
    def epilogue(acc, ins, outs):
        outs[0][...] = (acc * scale if scale != 1.0 else acc).astype(dtype)
    return epilogue


def mm_nn(name, a, w, out_dtype, tm=1024, tn=1024, tk=2048, jobs=()):
    m, kd = a.shape
    n = w.shape[1]
    tm, tn, tk = _tile(m, tm, 8), _tile(n, tn), _tile(kd, tk)
    return _mm(
        name, (n // tn, m // tm, kd // tk), [a, w],
        [pl.BlockSpec((tm, tk), lambda j, i, k: (i, k)), pl.BlockSpec((tk, tn), lambda j, i, k: (k, j))],
        [_sds((m, n), out_dtype)], [pl.BlockSpec((tm, tn), lambda j, i, k: (i, j))], (tm, tn),
        lambda a_ref, w_ref: _dot_nn(a_ref[...], w_ref[...]), _store(1.0, out_dtype), jobs)


def mm_nn_resid(name, a, w, x, scale, tm=1024, tn=1024, tk=1408, jobs=()):
    m, kd = a.shape
    n = w.shape[1]
    tm, tn, tk = _tile(m, tm, 8), _tile(n, tn), _tile(kd, tk)

    def epilogue(acc, ins, outs):
        outs[0][...] = ins[2][...] + scale * acc

    return _mm(
        name, (n // tn, m // tm, kd // tk), [a, w, x],
        [pl.BlockSpec((tm, tk), lambda j, i, k: (i, k)), pl.BlockSpec((tk, tn), lambda j, i, k: (k, j)),
         pl.BlockSpec((tm, tn), lambda j, i, k: (i, j))],
        [_sds((m, n), F32)], [pl.BlockSpec((tm, tn), lambda j, i, k: (i, j))], (tm, tn),
        lambda a_ref, w_ref, x_ref: _dot_nn(a_ref[...], w_ref[...]), epilogue, jobs)


def mm_nt(name, a, w, out_dtype, scale=1.0, tm=1024, tn=1024, tk=2048, jobs=()):
    m, kd = a.shape
    n = w.shape[0]
    tm, tn, tk = _tile(m, tm, 8), _tile(n, tn), _tile(kd, tk)
    return _mm(
        name, (n // tn, m // tm, kd // tk), [a, w],
        [pl.BlockSpec((tm, tk), lambda j, i, k: (i, k)), pl.BlockSpec((tn, tk), lambda j, i, k: (j, k))],
        [_sds((m, n), out_dtype)], [pl.BlockSpec((tm, tn), lambda j, i, k: (i, j))], (tm, tn),
        lambda a_ref, w_ref: _dot_nt(a_ref[...], w_ref[...]), _store(scale, out_dtype), jobs)


def mm_nt_pair_halves(name, a3, w_left, w_right, out_dtype, tm=1024, tn=1024, jobs=()):
    _, m, f = a3.shape
    n = w_left.shape[0]
    w = w_left.shape[1] // N_CHIPS
    tm, tn = _tile(m, tm, 8), _tile(n, tn)
    per_half = f // w

    def dot(a_ref, l_ref, r_ref):
        right = pl.program_id(2) % 2 == 1
        return _dot_nt(a_ref[...], jnp.where(right, r_ref[...], l_ref[...]))

    half = pl.BlockSpec((tn, w), lambda j, i, k: (j, k // 2))
    return _mm(
        name, (n // tn, m // tm, 2 * N_CHIPS), [a3, w_left, w_right],
        [pl.BlockSpec((None, tm, w), lambda j, i, k: (k // per_half, i, k % per_half)), half, half],
        [_sds((m, n), out_dtype)], [pl.BlockSpec((tm, tn), lambda j, i, k: (i, j))], (tm, tn),
        dot, _store(1.0, out_dtype), jobs)


def mm_nt_norm_bwd(name, a, w, x, r, g, dh_in, tm=512, tk=1408, jobs=()):
    pair = a.ndim == 3
    m, kd = a.shape[-2], a.shape[-1]
    d = w.shape[0]
    tm, tk = _tile(m, tm, 8), _tile(kd, tk)
    nkf = kd // tk
    nk = 2 * nkf if pair else nkf
    if pair:
        a_spec = pl.BlockSpec((None, tm, tk), lambda i, k: (k // nkf, i, k % nkf))
    else:
        a_spec = pl.BlockSpec((tm, tk), lambda i, k: (i, k))
    row = lambda i, k: (i, 0)
    fixed = lambda i, k: (0, 0)

    def body(a_ref, w_ref, x_ref, r_ref, g_ref, dh_ref, out_ref, outb_ref, dg_ref, *acc):
        i, k = pl.program_id(0), pl.program_id(1)
        dot = lambda: _dot_nt(a_ref[...], w_ref[...])

        def finish(dn):
            xh = x_ref[...] * r_ref[...]
            dxh = dn * g_ref[...]
            out = dh_ref[...] + r_ref[...] * (dxh - xh * jnp.mean(dxh * xh, axis=-1, keepdims=True))
            out_ref[...] = out
            outb_ref[...] = out.astype(BF16)
            part = jnp.sum(dn * xh, axis=0, keepdims=True)

            @pl.when(i == 0)
            def _():
                dg_ref[...] = part

            @pl.when(i > 0)
            def _():
                dg_ref[...] += part

        if nk == 1:
            finish(dot())
            return

        @pl.when(k == 0)
        def _():
            acc[0][...] = dot()

        @pl.when(jnp.logical_and(k > 0, k < nk - 1))
        def _():
            acc[0][...] += dot()

        @pl.when(k == nk - 1)
        def _():
            finish(acc[0][...] + dot())

    outs, job_outs = _call(
        name, body, (m // tm, nk),
        [a_spec, pl.BlockSpec((d, tk), lambda i, k: (0, k)), pl.BlockSpec((tm, d), row), pl.BlockSpec((tm, 1), row),
         pl.BlockSpec((1, d), fixed), pl.BlockSpec((tm, d), row)],
        [pl.BlockSpec((tm, d), row), pl.BlockSpec((tm, d), row), pl.BlockSpec((1, d), fixed)],
        [_sds((m, d), F32), _sds((m, d), BF16), _sds((1, d), F32)], [a, w, x, r, g, dh_in],
        scratch=[pltpu.VMEM((tm, d), F32)] if nk > 1 else [], sem=("arbitrary", "arbitrary"), jobs=jobs)
    return _ret(outs, job_outs, jobs, single=False)


def mm_tn(name, a, b, out_dtype, scale=1.0, tm=1024, tn=1024, tk=4096, jobs=()):
    kd, m = a.shape
    n = b.shape[1]
    tm, tn, tk = _tile(m, tm), _tile(n, tn), _tile(kd, tk, 16)
    return _mm(
        name, (n // tn, m // tm, kd // tk), [a, b],
        [pl.BlockSpec((tk, tm), lambda j, i, k: (k, i)), pl.BlockSpec((tk, tn), lambda j, i, k: (k, j))],
        [_sds((m, n), out_dtype)], [pl.BlockSpec((tm, tn), lambda j, i, k: (i, j))], (tm, tn),
        lambda a_ref, b_ref: _dot_tn(a_ref[...], b_ref[...]), _store(scale, out_dtype), jobs)


def mm_tn_pair(name, a, b3, out_dtype, tm=1024, tn=512, tk=4096, jobs=()):
    kd, m = a.shape
    f = b3.shape[2]
    tm, tn, tk = _tile(m, tm), _tile(f, tn), _tile(kd, tk, 16)
    nf = f // tn
    return _mm(
        name, (m // tm, 2 * nf, kd // tk), [a, b3],
        [pl.BlockSpec((tk, tm), lambda i, j, k: (k, i)),
         pl.BlockSpec((None, tk, tn), lambda i, j, k: (j // nf, k, j % nf))],
        [_sds((m, 2 * f), out_dtype)], [pl.BlockSpec((tm, tn), lambda i, j, k: (i, j))], (tm, tn),
        lambda a_ref, b_ref: _dot_tn(a_ref[...], b_ref[...]), _store(1.0, out_dtype), jobs)


def mm_tn_pair_half(name, a, b3, out_dtype, place, mine, tm=1024, tn=512, tk=4096, jobs=()):
    kd, m = a.shape
    f = b3.shape[2]
    tm, tn, tk = _tile(m // 2, tm), _tile(f, tn), _tile(kd, tk, 16)
    nf, nbm = f // tn, m // 2 // tm
    which = (lambda p: p[1]) if mine else (lambda p: 1 - p[1])
    return _mm(
        name, (nbm, 2 * nf, kd // tk), [a, b3],
        [pl.BlockSpec((tk, tm), lambda i, j, k, p: (k, i + which(p) * nbm)),
         pl.BlockSpec((None, tk, tn), lambda i, j, k, p: (j // nf, k, j % nf))],
        [_sds((m // 2, 2 * f), out_dtype)], [pl.BlockSpec((tm, tn), lambda i, j, k, p: (i, j))], (tm, tn),
        lambda a_ref, b_ref: _dot_tn(a_ref[...], b_ref[...]), _store(1.0, out_dtype), jobs, place)


def swiglu_fwd(name, n, w_in, tm=1024, tn=512, jobs=(), stride=1, phase=0, prev=None, compact=False):
    s, d = n.shape
    f = w_in.shape[1] // 2 * (stride if compact else 1)
    tm, tn = _tile(s, tm, 8), _tile(f, tn)
    nf = f // tn
    col = lambda j: j * stride + phase
    w_gate = (lambda j: j) if compact else col
    w_up = (lambda j: j + nf // stride) if compact else (lambda j: col(j) + nf)

    def body(n_ref, wg_ref, wu_ref, *rest):
        gu_ref, a_ref = rest[-2:]
        nv = n_ref[...]
        g = _dot_nn(nv, wg_ref[...])
        u = _dot_nn(nv, wu_ref[...])
        gu_ref[0] = g.astype(BF16)
        gu_ref[1] = u.astype(BF16)
        a_ref[...] = (g * jax.nn.sigmoid(g) * u).astype(BF16)

    kept = list(prev) if prev is not None else []
    outs, job_outs = _call(
        name, body, (nf // stride, s // tm),
        [pl.BlockSpec((tm, d), lambda j, i: (i, 0)), pl.BlockSpec((d, tn), lambda j, i: (0, w_gate(j))),
         pl.BlockSpec((d, tn), lambda j, i: (0, w_up(j)))] + [ANY] * len(kept),
        [pl.BlockSpec((2, tm, tn), lambda j, i: (0, i, col(j))), pl.BlockSpec((tm, tn), lambda j, i: (i, col(j)))],
        [_sds((2, s, f), BF16), _sds((s, f), BF16)], [n, w_in, w_in] + kept, sem=("parallel", "parallel"),
        jobs=jobs, carried={3 + k: k for k in range(len(kept))})
    return _ret(outs, job_outs, jobs, single=False)


def swiglu_bwd(name, dh, w_out, gu, scale, tm=1024, tn=512, jobs=()):
    s, d = dh.shape
    f = w_out.shape[0]
    tm, tn = _tile(s, tm, 8), _tile(f, tn)

    sub = _tile(tm, 256, 8)

    def body(dh_ref, w_ref, gu_ref, out_ref):
        for lo in range(0, tm, sub):
            rows = slice(lo, lo + sub)
            da = (_dot_nt(dh_ref[rows, :], w_ref[...]) * scale).astype(BF16)
            g = gu_ref[0, rows, :]
            u = gu_ref[1, rows, :]
            sg = 0.5 * jnp.tanh(0.5 * g) + 0.5
            t = g * sg
            out_ref[0, rows, :] = da * (u * (sg + t * (1.0 - sg)))
            out_ref[1, rows, :] = da * t

    outs, job_outs = _call(
        name, body, (f // tn, s // tm),
        [pl.BlockSpec((tm, d), lambda j, i: (i, 0)), pl.BlockSpec((tn, d), lambda j, i: (j, 0)),
         pl.BlockSpec((2, tm, tn), lambda j, i: (0, i, j))],
        [pl.BlockSpec((2, tm, tn), lambda j, i: (0, i, j))],
        [_sds((2, s, f), BF16)], [dh, w_out, gu], sem=("parallel", "parallel"), jobs=jobs)
    return _ret(outs, job_outs, jobs)


HALO = 16


def _conv_inputs(z_ref, hgc_ref, hhc_ref, i, cw, tm):
    gc = z_ref[:, cw:2 * cw].astype(F32)
    hc = z_ref[:, 2 * cw:3 * cw].astype(F32)
    cin = gc * hc
    halo = hgc_ref[...].astype(F32) * hhc_ref[...].astype(F32) * (i > 0).astype(F32)
    row = lax.broadcasted_iota(jnp.int32, (tm, cw), 0)
    x1 = jnp.where(row == 0, halo[HALO - 1:HALO], pltpu.roll(cin, 1, 0))
    x2 = jnp.where(row == 0, halo[HALO - 2:HALO - 1], jnp.where(row == 1, halo[HALO - 1:HALO], pltpu.roll(cin, 2, 0)))
    return gc, hc, cin, x1, x2


def _tril(w):
    r = lax.broadcasted_iota(jnp.int32, w.shape, 0)
    c = lax.broadcasted_iota(jnp.int32, w.shape, 1)
    return jnp.where(r >= c, w, jnp.zeros_like(w))


def mixer_fwd(name, z, conv_w, conv_b, g_v, w_s, b_t, tm=256, jobs=()):
    s, zc = z.shape
    cw = conv_w.shape[1]
    gw = g_v.shape[1]
    heads = gw // GROUP
    tm = _tile(s, tm)
    hb = tm // HALO

    def body(z_ref, hgc_ref, hhc_ref, cw_ref, cb_ref, gv_ref, ws_ref, bt_ref, y_ref):
        i = pl.program_id(0)
        _, _, cin, x1, x2 = _conv_inputs(z_ref, hgc_ref, hhc_ref, i, cw, tm)
        cv = cb_ref[...] + cw_ref[2:3, :] * cin + cw_ref[1:2, :] * x1 + cw_ref[0:1, :] * x2
        y_ref[:, 0:cw] = (z_ref[:, 0:cw].astype(F32) * cv).astype(BF16)
        for h in range(heads):
            lo = h * GROUP
            vh = z_ref[:, 3 * cw + gw + lo:3 * cw + gw + lo + GROUP].astype(F32)
            rv = lax.rsqrt(jnp.mean(vh * vh, axis=-1, keepdims=True) + EPS)
            vn = (vh * rv * gv_ref[:, lo:lo + GROUP]).astype(BF16)
            w = _tril(ws_ref[h]).astype(BF16)
            for n in range(tm // GROUP):
                rows = slice(n * GROUP, (n + 1) * GROUP)
                sg = _dot_nn(w, vn[rows]) + bt_ref[:, h:h + 1]
                u = z_ref[rows, 3 * cw + lo:3 * cw + lo + GROUP].astype(F32)
                y_ref[rows, cw + lo:cw + lo + GROUP] = (u * sg).astype(BF16)

    fixed2 = lambda i: (0, 0)
    outs, job_outs = _call(
        name, body, (s // tm,),
        [pl.BlockSpec((tm, zc), lambda i: (i, 0)),
         pl.BlockSpec((HALO, cw), lambda i: (jnp.maximum(i * hb - 1, 0), 1)),
         pl.BlockSpec((HALO, cw), lambda i: (jnp.maximum(i * hb - 1, 0), 2)),
         pl.BlockSpec(conv_w.shape, fixed2), pl.BlockSpec(conv_b.shape, fixed2),
         pl.BlockSpec(g_v.shape, fixed2), pl.BlockSpec(w_s.shape, lambda i: (0, 0, 0)),
         pl.BlockSpec(b_t.shape, fixed2)],
        [pl.BlockSpec((tm, cw + gw), lambda i: (i, 0))], [_sds((s, cw + gw), BF16)],
        [z, z, z, conv_w, conv_b, g_v, w_s, b_t], sem=("arbitrary",), jobs=jobs)
    return _ret(outs, job_outs, jobs)


def mixer_bwd(name, z, dy, conv_w, conv_b, g_v, w_s, b_t, tm=256, jobs=()):
    s, zc = z.shape
    cw = conv_w.shape[1]
    gw = g_v.shape[1]
    heads = gw // GROUP
    tm = _tile(s, tm)
    hb = tm // HALO
    nsteps = s // tm
    last_halo = s // HALO - 1

    def body(z_ref, hgc_ref, hhc_ref, ngb_ref, dy_ref, ndy_ref, cw_ref, cb_ref, gv_ref, ws_ref, bt_ref,
             dz_ref, sm_ref, dws_ref, dbt_ref, dsg_ref):
        i = pl.program_id(0)

        @pl.when(i == 0)
        def _():
            sm_ref[...] = jnp.zeros_like(sm_ref)
            dws_ref[...] = jnp.zeros_like(dws_ref)
            dsg_ref[...] = jnp.zeros_like(dsg_ref)

        gc, hc, cin, x1, x2 = _conv_inputs(z_ref, hgc_ref, hhc_ref, i, cw, tm)
        w0, w1, w2 = cw_ref[0:1, :], cw_ref[1:2, :], cw_ref[2:3, :]
        cv = cb_ref[...] + w2 * cin + w1 * x1 + w0 * x2
        gb = z_ref[:, 0:cw].astype(F32)
        dyc = dy_ref[:, 0:cw].astype(F32)
        dz_ref[:, 0:cw] = (dyc * cv).astype(BF16)
        dcv = dyc * gb
        nxt = ndy_ref[...].astype(F32) * ngb_ref[...].astype(F32) * (i < nsteps - 1).astype(F32)
        row = lax.broadcasted_iota(jnp.int32, (tm, cw), 0)
        d1 = jnp.where(row == tm - 1, nxt[0:1], pltpu.roll(dcv, tm - 1, 0))
        d2 = jnp.where(row == tm - 1, nxt[1:2], jnp.where(row == tm - 2, nxt[0:1], pltpu.roll(dcv, tm - 2, 0)))
        dcin = w2 * dcv + w1 * d1 + w0 * d2
        dz_ref[:, cw:2 * cw] = (dcin * hc).astype(BF16)
        dz_ref[:, 2 * cw:3 * cw] = (dcin * gc).astype(BF16)
        sm_ref[0:1, :] += jnp.sum(dcv * x2, axis=0, keepdims=True)
        sm_ref[1:2, :] += jnp.sum(dcv * x1, axis=0, keepdims=True)
        sm_ref[2:3, :] += jnp.sum(dcv * cin, axis=0, keepdims=True)
        sm_ref[3:4, :] += jnp.sum(dcv, axis=0, keepdims=True)

        for h in range(heads):
            lo = h * GROUP
            vcol = slice(3 * cw + gw + lo, 3 * cw + gw + lo + GROUP)
            ucol = slice(3 * cw + lo, 3 * cw + lo + GROUP)
            vh = z_ref[:, vcol].astype(F32)
            rv = lax.rsqrt(jnp.mean(vh * vh, axis=-1, keepdims=True) + EPS)
            xh = vh * rv
            gvh = gv_ref[:, lo:lo + GROUP]
            vn = (xh * gvh).astype(BF16)
            w = _tril(ws_ref[h]).astype(BF16)
            dgv = jnp.zeros((1, GROUP), F32)
            for n in range(tm // GROUP):
                rows = slice(n * GROUP, (n + 1) * GROUP)
                sg = _dot_nn(w, vn[rows]) + bt_ref[:, h:h + 1]
                dyg = dy_ref[rows, cw + lo:cw + lo + GROUP].astype(F32)
                dsg = dyg * z_ref[rows, ucol].astype(F32)
                dz_ref[rows, ucol] = (dyg * sg).astype(BF16)
                dsgb = dsg.astype(BF16)
                dvn = _dot_tn(w, dsgb)
                dws_ref[h] += _dot_nt(dsgb, vn[rows])
                dsg_ref[:, lo:lo + GROUP] += dsg
                xhc = xh[rows]
                dgv = dgv + jnp.sum(dvn * xhc, axis=0, keepdims=True)
                dxh = dvn * gvh
                dv = rv[rows] * (dxh - xhc * jnp.mean(dxh * xhc, axis=-1, keepdims=True))
                dz_ref[rows, vcol] = dv.astype(BF16)
            sm_ref[4:5, lo:lo + GROUP] += dgv

        @pl.when(i == nsteps - 1)
        def _():
            for h in range(heads):
                dws_ref[h] = _tril(dws_ref[h])
                dbt_ref[:, h:h + 1] = jnp.sum(dsg_ref[:, h * GROUP:(h + 1) * GROUP], axis=-1, keepdims=True)

    fixed2 = lambda i: (0, 0)
    fixed3 = lambda i: (0, 0, 0)
    prev = lambda col: (lambda i: (jnp.maximum(i * hb - 1, 0), col))
    nxt_blk = lambda i: (jnp.minimum((i + 1) * hb, last_halo), 0)
    outs, job_outs = _call(
        name, body, (nsteps,),
        [pl.BlockSpec((tm, zc), lambda i: (i, 0)),
         pl.BlockSpec((HALO, cw), prev(1)), pl.BlockSpec((HALO, cw), prev(2)),
         pl.BlockSpec((HALO, cw), nxt_blk),
         pl.BlockSpec((tm, cw + gw), lambda i: (i, 0)), pl.BlockSpec((HALO, cw), nxt_blk),
         pl.BlockSpec(conv_w.shape, fixed2), pl.BlockSpec(conv_b.shape, fixed2),
         pl.BlockSpec(g_v.shape, fixed2), pl.BlockSpec(w_s.shape, fixed3), pl.BlockSpec(b_t.shape, fixed2)],
        [pl.BlockSpec((tm, zc), lambda i: (i, 0)), pl.BlockSpec((8, cw), fixed2),
         pl.BlockSpec(w_s.shape, fixed3), pl.BlockSpec(b_t.shape, fixed2)],
        [_sds((s, zc), BF16), _sds((8, cw), F32), _sds(w_s.shape, F32), _sds(b_t.shape, F32)],
        [z, z, z, z, dy, dy, conv_w, conv_b, g_v, w_s, b_t],
        scratch=[pltpu.VMEM((GROUP, gw), F32)], sem=("arbitrary",), jobs=jobs)
    return _ret(outs, job_outs, jobs, single=False)


def _softmax_rows(sc):
    e = jnp.exp(sc - jnp.max(sc, axis=-1, keepdims=True))
    return e / jnp.sum(e, axis=-1, keepdims=True)


def attn_fwd(name, q, k, v, tm=512, jobs=()):
    s, d = q.shape
    m = k.shape[0]
    hd = d // XA_HEADS
    scale = hd ** -0.5
    tm = _tile(s, tm, 8)

    def body(q_ref, k_ref, v_ref, o_ref):
        for h in range(XA_HEADS):
            cols = slice(h * hd, (h + 1) * hd)
            p = _softmax_rows(_dot_nt(q_ref[:, cols], k_ref[:, cols]) * scale)
            o_ref[:, cols] = _dot_nn(p.astype(BF16), v_ref[:, cols]).astype(BF16)

    outs, job_outs = _call(
        name, body, (s // tm,),
        [pl.BlockSpec((tm, d), lambda i: (i, 0)), pl.BlockSpec((m, d), lambda i: (0, 0)),
         pl.BlockSpec((m, d), lambda i: (0, 0))],
        [pl.BlockSpec((tm, d), lambda i: (i, 0))], [_sds((s, d), BF16)], [q, k, v], sem=("arbitrary",), jobs=jobs)
    return _ret(outs, job_outs, jobs)


def attn_bwd(name, q, k, v, do, tm=512):
    s, d = q.shape
    m = k.shape[0]
    hd = d // XA_HEADS
    scale = hd ** -0.5
    tm = _tile(s, tm, 8)

    def body(q_ref, k_ref, v_ref, do_ref, dq_ref, dk_ref, dv_ref):
        i = pl.program_id(0)

        @pl.when(i == 0)
        def _():
            dk_ref[...] = jnp.zeros_like(dk_ref)
            dv_ref[...] = jnp.zeros_like(dv_ref)

        for h in range(XA_HEADS):
            cols = slice(h * hd, (h + 1) * hd)
            qh = q_ref[:, cols]
            doh = do_ref[:, cols]
            p = _softmax_rows(_dot_nt(qh, k_ref[:, cols]) * scale)
            dp = _dot_nt(doh, v_ref[:, cols])
            ds = (p * (dp - jnp.sum(dp * p, axis=-1, keepdims=True)) * scale).astype(BF16)
            dq_ref[:, cols] = _dot_nn(ds, k_ref[:, cols]).astype(BF16)
            dk_ref[:, cols] += _dot_tn(ds, qh)
            dv_ref[:, cols] += _dot_tn(p.astype(BF16), doh)

    row = lambda i: (i, 0)
    fixed = lambda i: (0, 0)
    return _call(
        name, body, (s // tm,),
        [pl.BlockSpec((tm, d), row), pl.BlockSpec((m, d), fixed), pl.BlockSpec((m, d), fixed),
         pl.BlockSpec((tm, d), row)],
        [pl.BlockSpec((tm, d), row), pl.BlockSpec((m, d), fixed), pl.BlockSpec((m, d), fixed)],
        [_sds((s, d), BF16), _sds((m, d), F32), _sds((m, d), F32)], [q, k, v, do], sem=("arbitrary",))[0]


def _grid2(rows, cols, row_mult):
    tr, tc = _tile(rows, 512, row_mult), _tile(cols, 2048)
    return tr, tc, rows // tr, cols // tc


def cast_place(name, block, axis, place, column_half=None):
    r, c = block.shape
    if column_half is not None:
        c //= 2
    tr, tc, nbr, nbc = _grid2(r, c, 16)
    first = 0 if column_half is None else column_half * nbc
    if axis == 1:
        dst = lambda i, j, p: (i, j + p[0] * nbc)
    else:
        dst = lambda i, j, p: (i + p[0] * nbr, j)

    def body(p_ref, w_ref, out_ref):
        out_ref[...] = w_ref[...].astype(BF16)

    return pl.pallas_call(
        body, name=name,
        grid_spec=pltpu.PrefetchScalarGridSpec(
            num_scalar_prefetch=1, grid=(nbr, nbc),
            in_specs=[pl.BlockSpec((tr, tc), lambda i, j, p: (i, j + first))],
            out_specs=pl.BlockSpec((tr, tc), dst)),
        out_shape=_sds(_full_shape((r, c), axis), BF16),
        compiler_params=_params(("parallel", "parallel")),
    )(place, block)


def pair_add(name, grad, peer, axis, place):
    hr, hc = peer.shape
    tr, tc, nbr, nbc = _grid2(hr, hc, 16)
    same = lambda i, j, p: (i, j)
    if grad.shape == peer.shape:
        mine = same
    elif axis == 1:
        mine = lambda i, j, p: (i + p[1] * nbr, j)
    else:
        mine = lambda i, j, p: (i, j + p[1] * nbc)

    def body(p_ref, g_ref, q_ref, out_ref):
        out_ref[...] = (g_ref[...].astype(F32) + q_ref[...].astype(F32)).astype(BF16)

    return pl.pallas_call(
        body, name=name,
        grid_spec=pltpu.PrefetchScalarGridSpec(
            num_scalar_prefetch=1, grid=(nbr, nbc),
            in_specs=[pl.BlockSpec((tr, tc), mine), pl.BlockSpec((tr, tc), same)],
            out_specs=pl.BlockSpec((tr, tc), same)),
        out_shape=_sds((hr, hc), BF16),
        compiler_params=_params(("parallel", "parallel")),
    )(place, grad, peer)


def cross_sum(name, part, land, axis, shape, place):
    _, sr, sc = land.shape
    tr, tc, nbr, nbc = _grid2(sr, sc, 16)
    if axis == 1:
        own = lambda i, j, p: (i, j + p[0] * nbc)
        dst = lambda i, j, p: (i + p[1] * nbr, j)
    else:
        own = lambda i, j, p: (i + p[0] * nbr, j)
        dst = lambda i, j, p: (i, j + p[1] * nbc)

    def body(p_ref, own_ref, land_ref, out_ref):
        out_ref[...] = ((own_ref[...].astype(F32) + land_ref[0].astype(F32))
                        + (land_ref[1].astype(F32) + land_ref[2].astype(F32)))

    return pl.pallas_call(
        body, name=name,
        grid_spec=pltpu.PrefetchScalarGridSpec(
            num_scalar_prefetch=1, grid=(nbr, nbc),
            in_specs=[pl.BlockSpec((tr, tc), own), pl.BlockSpec((3, tr, tc), lambda i, j, p: (0, i, j))],
            out_specs=pl.BlockSpec((tr, tc), dst)),
        out_shape=_sds(_block(shape, axis), F32),
        compiler_params=_params(("parallel", "parallel")),
    )(place, part, land)


def _adam_math(w, g, m, v):
    m = ADAM_B1 * m + (1.0 - ADAM_B1) * g
    v = ADAM_B2 * v + (1.0 - ADAM_B2) * (g * g)
    m_hat = m / (1.0 - ADAM_B1 ** ADAM_STEP)
    v_hat = v / (1.0 - ADAM_B2 ** ADAM_STEP)
    delta = -ADAM_LR * (m_hat / (jnp.sqrt(v_hat) + ADAM_EPS) + ADAM_WD * w)
    return delta, m, v


def adamw(name, w, g, m, v, jobs=()):
    r, c = w.shape
    tr, tc = _tile(r, 256, 8), _tile(c, 1408)

    def body(w_ref, g_ref, m_ref, v_ref, g_out, d_out, m_out, v_out):
        d, mm, vv = _adam_math(w_ref[...], g_ref[...], m_ref[...], v_ref[...])
        g_out[...] = g_ref[...]
        d_out[...] = d
        m_out[...] = mm
        v_out[...] = vv

    spec = pl.BlockSpec((tr, tc), lambda i, j: (i, j))
    outs, job_outs = _call(name, body, (r // tr, c // tc), [spec] * 4, [spec] * 4, [_sds((r, c), F32)] * 4,
                           [w, g, m, v], sem=("parallel", "parallel"), jobs=jobs)
    return _ret(outs, job_outs, jobs, single=False)


def small_sum(name, stacks):
    def body(*refs):
        for s_ref, out_ref in zip(refs[:len(stacks)], refs[len(stacks):]):
            acc = s_ref[0]
            for d in range(1, s_ref.shape[0]):
                acc = acc + s_ref[d]
            out_ref[...] = acc

    return pl.pallas_call(body, name=name, out_shape=[_sds(s.shape[1:], F32) for s in stacks])(*stacks)


WEIGHTS = ["g_ffn1", "w_ffn1_in", "w_ffn1_out", "g_mix", "w_mix_in", "conv_w", "conv_b", "g_gm_v", "w_spatial",
           "b_spatial", "w_mix_out", "g_xattn", "g_mem", "w_xq", "w_xk", "w_xv", "w_xo", "g_ffn2", "w_ffn2_in",
           "w_ffn2_out", "g_final"]
BIG = {"w_ffn1_in": 1, "w_ffn1_out": 0, "w_mix_in": 1, "w_mix_out": 0, "w_xq": 0, "w_xk": 0, "w_xv": 0, "w_xo": 0,
       "w_ffn2_in": 1, "w_ffn2_out": 0}
SMALL = [n for n in WEIGHTS if n not in BIG]
LATE_SMALL = ["g_ffn1"]
EARLY_SMALL = [n for n in SMALL if n not in LATE_SMALL]


def _pack(arrays):
    flat = jnp.concatenate([a.reshape(-1) for a in arrays])
    rows = -(-flat.shape[0] // 1024) * 8
    return jnp.pad(flat, (0, rows * 128 - flat.shape[0])).reshape(rows, 128)


def _unpack(buf, shapes):
    flat = buf.reshape(-1)
    out, pos = [], 0
    for shp in shapes:
        n = math.prod(shp)
        out.append(flat[pos:pos + n].reshape(shp))
        pos += n
    return out


def kernel(x, mem, g_ffn1, w_ffn1_in, w_ffn1_out, g_mix, w_mix_in, conv_w, conv_b, g_gm_v, w_spatial, b_spatial, w_mix_out, g_xattn, g_mem, w_xq, w_xk, w_xv, w_xo, g_ffn2, w_ffn2_in, w_ffn2_out, g_final, loss_target, m_g_ffn1, m_w_ffn1_in, m_w_ffn1_out, m_g_mix, m_w_mix_in, m_conv_w, m_conv_b, m_g_gm_v, m_w_spatial, m_b_spatial, m_w_mix_out, m_g_xattn, m_g_mem, m_w_xq, m_w_xk, m_w_xv, m_w_xo, m_g_ffn2, m_w_ffn2_in, m_w_ffn2_out, m_g_final, v_g_ffn1, v_w_ffn1_in, v_w_ffn1_out, v_g_mix, v_w_mix_in, v_conv_w, v_conv_b, v_g_gm_v, v_w_spatial, v_b_spatial, v_w_mix_out, v_g_xattn, v_g_mem, v_w_xq, v_w_xk, v_w_xv, v_w_xo, v_g_ffn2, v_w_ffn2_in, v_w_ffn2_out, v_g_final):
    given = dict(locals())
    wts = {n: given[n] for n in WEIGHTS}
    mom = {n: given["m_" + n] for n in WEIGHTS}
    var = {n: given["v_" + n] for n in WEIGHTS}

    xi, yi, ci = lax.axis_index("x"), lax.axis_index("y"), lax.axis_index("c")
    blk = 2 * xi + yi
    place = jnp.stack([blk, ci]).astype(jnp.int32)

    x2, mem2, tgt = x[0], mem[0], loss_target[0]
    w_s, b_t = w_spatial[0], b_spatial[0].T
    gf = g_final[None]

    rest = [n for n in BIG if n != "w_ffn1_in"]
    own = {n: cast_place("cast_" + n, wts[n][0], BIG[n], place) for n in rest}
    own_left = cast_place("cast_w_ffn1_in_left", wts["w_ffn1_in"][0], 1, place, column_half=0)
    own_right = cast_place("cast_w_ffn1_in_right", wts["w_ffn1_in"][0], 1, place, column_half=1)
    shape = {n: own[n].shape for n in rest}
    shape["w_ffn1_in"] = _full_shape(wts["w_ffn1_in"][0].shape, 1)
    full = {}

    def gather_now(name, arrays, axes, collective_id):
        job = gather_job([(a, ax, WHOLE, WHOLE) for a, ax in zip(arrays, axes)])
        return by_sequencer(name, job, "gather", collective_id)[1]

    _, (conv_taps,) = by_sequencer("gather_conv_taps", columns_job(jnp.pad(conv_w[0], ((0, 8 - CONV_K), (0, 0)))),
                                   "chips", TAPS_ID)
    (w1_left,) = gather_now("gather_w_ffn1_in_left", [own_left], [1], GATHER_IDS[0])
    (w1_right,) = gather_now("gather_w_ffn1_in_right", [own_right], [1], GATHER_IDS[1])
    groups = [["w_ffn1_out"], ["w_mix_in", "w_mix_out"], ["w_xq", "w_xk", "w_xv", "w_xo"], ["w_ffn2_in"],
              ["w_ffn2_out"]]
    for g, names in enumerate(groups):
        got = gather_now("gather_" + "_".join(names), [own[n] for n in names], [BIG[n] for n in names],
                         GATHER_IDS[2 + g])
        full.update(zip(names, got))

    half_cols = dict(tm=512, tn=shape["w_ffn1_in"][1] // (2 * N_CHIPS), stride=2, compact=True)
    n1, r1 = rmsnorm_fwd("norm1", x2, g_ffn1)
    halves = swiglu_fwd("ffn1_in_left", n1, w1_left, phase=0, **half_cols)
    gu1, a1 = swiglu_fwd("ffn1_in_right", n1, w1_right, phase=1, prev=halves, **half_cols)
    h1 = mm_nn_resid("ffn1_out", a1, full["w_ffn1_out"], x2, 0.5, tm=512, tk=5632)
    n2, r2 = rmsnorm_fwd("norm2", h1, g_mix)
    z = mm_nn("mix_in", n2, full["w_mix_in"], BF16)
    ycat = mixer_fwd("mixer", z, conv_taps, conv_b, g_gm_v, w_s, b_t)
    h2 = mm_nn_resid("mix_out", ycat, full["w_mix_out"], h1, 1.0, tk=2048)
    n3, r3 = rmsnorm_fwd("norm3", h2, g_xattn)
    mem2, h2 = lax.optimization_barrier((mem2, h2))
    mn, rm = rmsnorm_fwd("norm_mem", mem2, g_mem)
    q = mm_nn("xq", n3, full["w_xq"], BF16)
    k = mm_nn("xk", mn, full["w_xk"], BF16)
    v = mm_nn("xv", mn, full["w_xv"], BF16)
    o = attn_fwd("attn", q, k, v)
    h3 = mm_nn_resid("xo", o, full["w_xo"], h2, 1.0, tk=2048)
    n4, r4 = rmsnorm_fwd("norm4", h3, g_ffn2)
    gu2, a2 = swiglu_fwd("ffn2_in", n4, full["w_ffn2_in"])
    h4 = mm_nn_resid("ffn2_out", a2, full["w_ffn2_out"], h3, 0.5, tm=512, tk=5632)
    loss_blk, dh4, dh4b, dg_final = loss_head("loss_head", h4, gf, tgt)

    dw, peer, part, land, half, reduced, grads = {}, {}, {}, {}, {}, {}, {}
    uses = {"sibling": 0, "chips": 0}

    def tie(first, then):
        return lax.optimization_barrier((first, then))

    def on_sequencer(name, job, peers):
        uses[peers] += 1
        return by_sequencer(name, job, peers, {"sibling": SIBLING_IDS, "chips": CROSS_IDS}[peers][uses[peers] % 2])

    def start_pair(*names):
        kept, got = on_sequencer("pair_" + "_".join(names), pair_job([dw[n] for n in names], [BIG[n] for n in names]),
                                 "sibling")
        for n, k, p in zip(names, kept, got):
            dw[n], peer[n] = k, p

    def start_cross(*names):
        kept, got = on_sequencer("cross_" + "_".join(names),
                                 cross_job([(part[n], BIG[n], shape[n], None, WHOLE) for n in names]), "chips")
        for n, k, l in zip(names, kept, got):
            part[n], land[n] = k, l

    def finish_pair(chain, *names):
        for n in names:
            part[n], chain = tie(pair_add("pair_add_" + n, dw[n], peer[n], BIG[n], place), chain)
        start_cross(*names)
        return chain

    def finish_cross(chain, *names):
        for n in names:
            half[n], chain = tie(cross_sum("cross_sum_" + n, part[n], land[n], BIG[n], shape[n], place), chain)
        _, got = on_sequencer("final_" + "_".join(names),
                              final_job([half[n] for n in names], [BIG[n] for n in names], [shape[n] for n in names]),
                              "sibling")
        reduced.update(zip(names, got))
        return chain

    delta, new_m, new_v = {}, {}, {}

    def update(*names):
        for n in names:
            grads[n], delta[n], new_m[n], new_v[n] = adamw("adamw_" + n, wts[n][0], reduced[n], mom[n][0], var[n][0])

    dgu2 = swiglu_bwd("ffn2_dact", dh4b, full["w_ffn2_out"], gu2, 0.5)
    dw["w_ffn2_in"], dh4b = tie(mm_tn_pair("ffn2_dwin", n4, dgu2, BF16), dh4b)
    start_pair("w_ffn2_in")
    dw["w_ffn2_out"], dgu2 = tie(mm_tn("ffn2_dwout", a2, dh4b, BF16, scale=0.5), dgu2)
    dgu2 = finish_pair(dgu2, "w_ffn2_in")
    start_pair("w_ffn2_out")
    dh3, dh3b, dg_ffn2 = mm_nt_norm_bwd("ffn2_dn", dgu2, full["w_ffn2_in"], h3, r4, g_ffn2, dh4)
    dh3b = finish_pair(dh3b, "w_ffn2_out")

    dw["w_xo"], dh3b = tie(mm_tn("xo_dw", o, dh3b, BF16), dh3b)
    dh3b = finish_cross(dh3b, "w_ffn2_in")
    do = mm_nt("xo_dx", dh3b, full["w_xo"], BF16)
    dq, dk, dv = attn_bwd("attn_bwd", q, k, v, do)
    dkb, dvb = dk.astype(BF16), dv.astype(BF16)
    dw["w_xq"], dq = tie(mm_tn("xq_dw", n3, dq, BF16), dq)
    dh2, dh2b, dg_xattn = mm_nt_norm_bwd("xq_dx", dq, full["w_xq"], h2, r3, g_xattn, dh3, tk=1024)
    dw["w_xk"] = mm_tn("xk_dw", mn, dkb, BF16)
    dw["w_xv"] = mm_tn("xv_dw", mn, dvb, BF16)
    dmn_k = mm_nt("xk_dx", dkb, full["w_xk"], F32)
    dmn_v = mm_nt("xv_dx", dvb, full["w_xv"], F32)
    dg_mem = gain_grad("norm_mem_bwd", dmn_k, dmn_v, mem2, rm)

    dh2b = finish_cross(dh2b, "w_ffn2_out")
    dw["w_mix_out"], dh2b = tie(mm_tn("mix_out_dw", ycat, dh2b, BF16), dh2b)
    attn_names = ["w_xo", "w_xq", "w_xk", "w_xv", "w_mix_out"]
    start_pair(*attn_names)
    dycat = mm_nt("mix_out_dx", dh2b, full["w_mix_out"], BF16)
    dycat = finish_pair(dycat, *attn_names)
    dz, dsmall, dws, dbt = mixer_bwd("mixer_bwd", z, dycat, conv_taps, conv_b, g_gm_v, w_s, b_t)
    dw["w_mix_in"], dz = tie(mm_tn("mix_in_dw", n2, dz, BF16), dz)
    start_pair("w_mix_in")
    dh1, dh1b, dg_mix = mm_nt_norm_bwd("mix_in_dx", dz, full["w_mix_in"], h1, r2, g_mix, dh2, tk=1280)
    dh1b = finish_pair(dh1b, "w_mix_in")
    early = {"g_mix": dg_mix, "conv_w": dsmall[0:CONV_K], "conv_b": dsmall[3:4], "g_gm_v": dsmall[4:5],
             "w_spatial": dws, "b_spatial": dbt.T, "g_xattn": dg_xattn, "g_mem": dg_mem, "g_ffn2": dg_ffn2,
             "g_final": dg_final}
    _, (early_all,) = by_sequencer("stack_early", stack_job(_pack([early[n] for n in EARLY_SMALL])), "all",
                                   STACK_IDS[0])

    dh1b = finish_cross(dh1b, *attn_names)
    dw["w_ffn1_out"], dh1b = tie(mm_tn("ffn1_dwout", a1, dh1b, BF16, scale=0.5), dh1b)
    start_pair("w_ffn1_out")
    dh1b = finish_cross(dh1b, "w_mix_in")
    dgu1 = swiglu_bwd("ffn1_dact", dh1b, full["w_ffn1_out"], gu1, 0.5)
    dgu1 = finish_pair(dgu1, "w_ffn1_out")
    theirs, dgu1 = tie(mm_tn_pair_half("ffn1_dwin_theirs", n1, dgu1, BF16, place, False), dgu1)
    _, (from_sibling,) = on_sequencer("pair_w_ffn1_in", pair_job([theirs], [1], is_half=True), "sibling")
    mine, dgu1 = tie(mm_tn_pair_half("ffn1_dwin_mine", n1, dgu1, BF16, place, True), dgu1)
    part["w_ffn1_in"], dgu1 = tie(pair_add("pair_add_w_ffn1_in", mine, from_sibling, 1, place), dgu1)
    start_cross("w_ffn1_in")
    dgu1 = finish_cross(dgu1, "w_ffn1_out")
    dn1 = mm_nt_pair_halves("ffn1_dn", dgu1, w1_left, w1_right, F32)
    dx, _, dg_ffn1 = rmsnorm_bwd("norm1_bwd", dn1, x2, r1, g_ffn1, dh1)
    _, (late_all,) = by_sequencer("stack_late", stack_job(_pack([dg_ffn1])), "all", STACK_IDS[1])
    finish_cross(dx, "w_ffn1_in")
    update(*BIG)

    early_sum, late_sum = small_sum("small_sum", [early_all, late_all])
    for n, g in zip(EARLY_SMALL, _unpack(early_sum, [early[n].shape for n in EARLY_SMALL])):
        grads[n] = g
    grads["g_ffn1"] = _unpack(late_sum, [dg_ffn1.shape])[0]
    taps_cols = conv_w.shape[2]
    grads["conv_w"] = lax.dynamic_slice_in_dim(grads["conv_w"], blk * taps_cols, taps_cols, axis=1)
    packed = [_pack([src[n] for n in SMALL]) for src in (wts, grads, mom, var)]
    own_shapes = [wts[n].shape for n in SMALL]
    for dst, buf in zip((delta, new_m, new_v), adamw("adamw_small", *packed)[1:]):
        for n, a in zip(SMALL, _unpack(buf, own_shapes)):
            dst[n] = a

    loss = lax.psum(loss_blk[0, 0], ("x", "y", "c"))
    outs = [loss, dx[None]]
    for group in (grads, delta, new_m, new_v):
        outs += [group[n].reshape(wts[n].shape) for n in WEIGHTS]
    return tuple(outs)
```

```python
import math

import jax
import jax.numpy as jnp
from jax import lax
from jax.experimental import pallas as pl
from jax.experimental.pallas import tpu as pltpu
from jax.experimental.pallas import tpu_sc as plsc

F32 = jnp.float32
BF16 = jnp.bfloat16
EPS = 1e-6
GROUP = 128
XA_HEADS = 4
CONV_K = 3
N_CHIPS = 4
VMEM_LIMIT_BYTES = 56 * 1024 * 1024

ADAM_LR = 0.001
ADAM_B1 = 0.9
ADAM_B2 = 0.999
ADAM_EPS = 1e-08
ADAM_WD = 0.01
ADAM_STEP = 10

MESH = pl.DeviceIdType.MESH
ANY = pl.BlockSpec(memory_space=pl.ANY)

GATHER_IDS = tuple(range(1, 8))
CROSS_IDS = (8, 9)
SIBLING_IDS = (10, 11)
STACK_IDS = (12, 13)
TAPS_ID = 14


def _tile(dim, pref, mult=128):
    if dim <= pref:
        return dim
    t = (pref // mult) * mult
    while t >= mult:
        if dim % t == 0:
            return t
        t -= mult
    raise ValueError(f"no tile for {dim} under {pref}")


def _params(sem):
    return pltpu.CompilerParams(dimension_semantics=sem, vmem_limit_bytes=VMEM_LIMIT_BYTES)


def _sds(shape, dtype):
    return jax.ShapeDtypeStruct(shape, dtype)


def _dot_nn(a, b):
    return jnp.dot(a, b, preferred_element_type=F32)


def _dot_nt(a, b):
    return lax.dot_general(a, b, (((1,), (1,)), ((), ())), preferred_element_type=F32)


def _dot_tn(a, b):
    return lax.dot_general(a, b, (((0,), (0,)), ((), ())), preferred_element_type=F32)


class Job:
    def __init__(self, inputs, out_shapes, aliases, sems, start, middle, finish):
        self.inputs, self.out_shapes, self.aliases, self.sems = inputs, out_shapes, aliases, sems
        self.start, self.middle, self.finish = start, middle, finish


def _place():
    x, y, c = lax.axis_index("x"), lax.axis_index("y"), lax.axis_index("c")
    chips = [(1 - x, y), (x, 1 - y), (1 - x, 1 - y)]
    return x, y, c, chips


def _ds(start, size, lane):
    if not isinstance(start, int):
        start = pl.multiple_of(start, 128 if lane else 16)
    return pl.ds(start, size)


WHOLE = (0, 1, 1)


def _window(ref, axis, shape, blk=None, half=None, sub=WHOLE, within=WHOLE):
    n = shape[axis] // N_CHIPS
    hs = shape[1 - axis] // 2
    idx = [slice(None), slice(None)]
    if blk is not None:
        b_first, b_count, b_pieces = within
        b_ext = n // b_pieces
        idx[axis] = _ds(blk * n + b_first * b_ext, b_count * b_ext, axis == 1)
    first, count, pieces = sub
    ext = hs // pieces
    if half is not None:
        idx[1 - axis] = _ds(half * hs + first * ext, count * ext, axis == 0)
    elif pieces > 1:
        idx[1 - axis] = _ds(first * ext, count * ext, axis == 0)
    return ref.at[tuple(idx)]


def _remote(src, dst, send_sem, recv_sem, dev):
    return pltpu.make_async_remote_copy(src_ref=src, dst_ref=dst, send_sem=send_sem, recv_sem=recv_sem,
                                        device_id=dev, device_id_type=MESH)


def _full_shape(block_shape, axis):
    out = list(block_shape)
    out[axis] *= N_CHIPS
    return tuple(out)


def _half_all(shape, axis):
    out = list(shape)
    out[1 - axis] //= 2
    return tuple(out)


def _block(shape, axis):
    out = list(shape)
    out[axis] //= N_CHIPS
    return tuple(out)


def _half_block(shape, axis):
    return _half_all(_block(shape, axis), axis)


def gather_job(items):
    nw = len(items)
    shapes = [item[0].shape for item in items]
    n_sem = 8

    def parts(sub):
        first, count, pieces = sub
        return (2 * first, count, 2 * pieces), (2 * first + count, count, 2 * pieces)

    def start(pos, ins, outs, sems):
        x, y, c, chips = pos
        for w, (_, ax, sub, within) in enumerate(items):
            mine = _window(outs[w], ax, shapes[w], blk=2 * x + y, half=c, sub=sub, within=within)
            for j in range(2):
                _remote(mine, mine, sems[0].at[n_sem * w + j], sems[1].at[n_sem * w + j], (*chips[j], c)).start()

    def middle(pos, ins, outs, sems):
        x, y, c, chips = pos
        for w, (_, ax, sub, within) in enumerate(items):
            for j in range(2):
                cx, cy = chips[j]
                landed = _window(outs[w], ax, shapes[w], blk=2 * cx + cy, half=c, sub=sub, within=within)
                _remote(landed, landed, sems[0].at[n_sem * w + j], sems[1].at[n_sem * w + j], (cx, cy, c)).wait_recv()
                part = _window(outs[w], ax, shapes[w], blk=2 * cx + cy, half=c, sub=parts(sub)[j], within=within)
                _remote(part, part, sems[0].at[n_sem * w + 2 + j], sems[1].at[n_sem * w + 2 + j],
                        (*chips[1 - j], c)).start()
                _remote(landed, landed, sems[0].at[n_sem * w + 4 + j], sems[1].at[n_sem * w + 4 + j],
                        (x, y, 1 - c)).start()

    def finish(pos, ins, outs, sems):
        x, y, c, chips = pos
        sib = (x, y, 1 - c)
        for w, (_, ax, sub, within) in enumerate(items):
            dx, dy = chips[2]
            for j in range(2):
                part = _window(outs[w], ax, shapes[w], blk=2 * dx + dy, half=c, sub=parts(sub)[j], within=within)
                cp = _remote(part, part, sems[0].at[n_sem * w + 2 + j], sems[1].at[n_sem * w + 2 + j], sib)
                cp.wait_recv()
                cp.wait_send()
            diag = _window(outs[w], ax, shapes[w], blk=2 * dx + dy, half=c, sub=sub, within=within)
            _remote(diag, diag, sems[0].at[n_sem * w + 6], sems[1].at[n_sem * w + 6], sib).start()
        for w, (_, ax, sub, within) in enumerate(items):
            for j, (cx, cy) in enumerate(chips):
                passed = _window(outs[w], ax, shapes[w], blk=2 * cx + cy, half=1 - c, sub=sub, within=within)
                cp = _remote(passed, passed, sems[0].at[n_sem * w + 4 + j], sems[1].at[n_sem * w + 4 + j], sib)
                cp.wait_recv()
                cp.wait_send()
            mine = _window(outs[w], ax, shapes[w], blk=2 * x + y, half=c, sub=sub, within=within)
            for j in range(2):
                _remote(mine, mine, sems[0].at[n_sem * w + j], sems[1].at[n_sem * w + j], sib).wait_send()

    sems = [pltpu.SemaphoreType.DMA((n_sem * nw,)), pltpu.SemaphoreType.DMA((n_sem * nw,))]
    return Job([item[0] for item in items], [_sds(item[0].shape, item[0].dtype) for item in items],
               {w: w for w in range(nw)}, sems, start, middle, finish)


def by_sequencer(name, job, peers, collective_id):
    ins = [jax.new_ref(a, memory_space=pltpu.MemorySpace.HBM) for a in job.inputs]
    from_input = {o: i for i, o in job.aliases.items()}
    outs = [ins[from_input[k]] if k in from_input else jax.empty_ref(s, memory_space=pltpu.MemorySpace.HBM)
            for k, s in enumerate(job.out_shapes)]

    @pl.kernel(mesh=plsc.ScalarSubcoreMesh(axis_name="sequencer", num_cores=1), name=name,
               scratch_types=tuple(job.sems), compiler_params=pltpu.CompilerParams(collective_id=collective_id))
    def launch(*sems):
        pos = _place()
        x, y, c, chips = pos
        devs = {"sibling": [(x, y, 1 - c)],
                "chips": [(cx, cy, c) for cx, cy in chips],
                "gather": [(*chips[0], c), (*chips[1], c), (x, y, 1 - c)],
                "all": [(px, py, pc) for px in (x, 1 - x) for py in (y, 1 - y) for pc in (c, 1 - c)][1:]}[peers]
        barrier = pltpu.get_barrier_semaphore()
        for dev in devs:
            pl.semaphore_signal(barrier, inc=1, device_id=dev, device_id_type=MESH)
        pl.semaphore_wait(barrier, len(devs))
        for phase in (job.start, job.middle, job.finish):
            if phase is not None:
                phase(pos, ins, outs, list(sems))

    launch()
    kept = [r[...] for r in ins]
    return kept, [kept[from_input[k]] if k in from_input else r[...] for k, r in enumerate(outs)]


def pair_job(grads, axes, is_half=False):
    nw = len(grads)
    shapes = [g.shape for g in grads]

    def start(pos, ins, outs, sems):
        x, y, c, _ = pos
        for w in range(nw):
            src = ins[w] if is_half else _window(ins[w], axes[w], shapes[w], half=1 - c)
            _remote(src, outs[w], sems[0].at[w], sems[1].at[w], (x, y, 1 - c)).start()

    def finish(pos, ins, outs, sems):
        x, y, c, _ = pos
        for w in range(nw):
            cp = _remote(outs[w], outs[w], sems[0].at[w], sems[1].at[w], (x, y, 1 - c))
            cp.wait_recv()
            cp.wait_send()

    sems = [pltpu.SemaphoreType.DMA((nw,)), pltpu.SemaphoreType.DMA((nw,))]
    out_shapes = [_sds(s if is_half else _half_all(s, a), BF16) for s, a in zip(shapes, axes)]
    return Job(list(grads), out_shapes, {}, sems, start, None, finish)


def cross_job(items):
    nw = len(items)
    inputs, aliases = [], {}
    for w, (part, ax, shape, prev, sub) in enumerate(items):
        inputs.append(part)
        if prev is not None:
            aliases[len(inputs)] = w
            inputs.append(prev)

    def copies(pos, ins, outs, sems):
        x, y, c, chips = pos
        k = 0
        for w, (_, ax, shape, prev, sub) in enumerate(items):
            src = ins[k]
            k += 2 if prev is not None else 1
            for j, (cx, cy) in enumerate(chips):
                slot = _window(outs[w].at[j], ax, shape, sub=sub)
                yield (_remote(_window(src, ax, shape, blk=2 * cx + cy, sub=sub), slot,
                               sems[0].at[3 * w + j], sems[1].at[3 * w + j], (cx, cy, c)),
                       _remote(slot, slot, sems[0].at[3 * w + j], sems[1].at[3 * w + j], (cx, cy, c)))

    def start(pos, ins, outs, sems):
        for send, _ in copies(pos, ins, outs, sems):
            send.start()

    def finish(pos, ins, outs, sems):
        for send, recv in copies(pos, ins, outs, sems):
            recv.wait_recv()
            send.wait_send()

    sems = [pltpu.SemaphoreType.DMA((3 * nw,)), pltpu.SemaphoreType.DMA((3 * nw,))]
    out_shapes = [_sds((3,) + _half_block(shape, ax), BF16) for _, ax, shape, _, _ in items]
    return Job(inputs, out_shapes, aliases, sems, start, None, finish)


def final_job(blocks, axes, shapes):
    nw = len(blocks)

    def start(pos, ins, outs, sems):
        x, y, c, _ = pos
        for w in range(nw):
            mine = _window(outs[w], axes[w], shapes[w], half=c)
            _remote(mine, mine, sems[0].at[w], sems[1].at[w], (x, y, 1 - c)).start()

    def finish(pos, ins, outs, sems):
        x, y, c, _ = pos
        for w in range(nw):
            theirs = _window(outs[w], axes[w], shapes[w], half=1 - c)
            cp = _remote(theirs, theirs, sems[0].at[w], sems[1].at[w], (x, y, 1 - c))
            cp.wait_recv()
            cp.wait_send()

    sems = [pltpu.SemaphoreType.DMA((nw,)), pltpu.SemaphoreType.DMA((nw,))]
    return Job(list(blocks), [_sds(b.shape, b.dtype) for b in blocks], {w: w for w in range(nw)}, sems, start, None,
               finish)


def stack_job(small):
    def peers(pos):
        x, y, c, _ = pos
        for k in range(1, 8):
            yield k - 1, (1 - x if k & 4 else x, 1 - y if k & 2 else y, 1 - c if k & 1 else c)

    def start(pos, ins, outs, sems):
        x, y, c, _ = pos
        mine = outs[0].at[4 * x + 2 * y + c]
        pltpu.make_async_copy(ins[0], mine, sems[2]).start()
        for k, dev in peers(pos):
            _remote(ins[0], mine, sems[0].at[k], sems[1].at[k], dev).start()

    def finish(pos, ins, outs, sems):
        x, y, c, _ = pos
        for k, (px, py, pc) in peers(pos):
            slot = outs[0].at[4 * px + 2 * py + pc]
            cp = _remote(slot, slot, sems[0].at[k], sems[1].at[k], (px, py, pc))
            cp.wait_recv()
            cp.wait_send()
        pltpu.make_async_copy(ins[0], outs[0].at[4 * x + 2 * y + c], sems[2]).wait()

    sems = [pltpu.SemaphoreType.DMA((7,)), pltpu.SemaphoreType.DMA((7,)), pltpu.SemaphoreType.DMA]
    return Job([small], [_sds((8,) + small.shape, small.dtype)], {}, sems, start, None, finish)


def columns_job(block):
    cols = block.shape[1]
    place = lambda out, b: out.at[:, _ds(b * cols, cols, True)]

    def start(pos, ins, outs, sems):
        x, y, c, chips = pos
        pltpu.make_async_copy(ins[0], place(outs[0], 2 * x + y), sems[2]).start()
        for j, (cx, cy) in enumerate(chips):
            _remote(ins[0], place(outs[0], 2 * x + y), sems[0].at[j], sems[1].at[j], (cx, cy, c)).start()

    def finish(pos, ins, outs, sems):
        x, y, c, chips = pos
        for j, (cx, cy) in enumerate(chips):
            got = place(outs[0], 2 * cx + cy)
            cp = _remote(got, got, sems[0].at[j], sems[1].at[j], (cx, cy, c))
            cp.wait_recv()
            cp.wait_send()
        pltpu.make_async_copy(ins[0], place(outs[0], 2 * x + y), sems[2]).wait()

    sems = [pltpu.SemaphoreType.DMA((3,)), pltpu.SemaphoreType.DMA((3,)), pltpu.SemaphoreType.DMA]
    return Job([block], [_sds((block.shape[0], N_CHIPS * cols), block.dtype)], {}, sems, start, None, finish)


def _call(name, body, grid, in_specs, out_specs, out_shape, args, scratch=(), sem=None, jobs=(), place=None,
          carried=None):
    n_in, n_out, n_sc = len(args), len(out_shape), len(scratch)
    carried = dict(carried or {})

    def launch(fn, in_specs, out_specs, out_shape, scratch, aliases, sem, operands):
        if place is None:
            return pl.pallas_call(
                fn, name=name, grid=grid, in_specs=in_specs, out_specs=out_specs, out_shape=out_shape,
                scratch_shapes=scratch, input_output_aliases=aliases, compiler_params=_params(sem))(*operands)
        spec = pltpu.PrefetchScalarGridSpec(num_scalar_prefetch=1, grid=grid, in_specs=in_specs,
                                            out_specs=out_specs, scratch_shapes=scratch)
        return pl.pallas_call(
            lambda p_ref, *refs: fn(*refs), name=name, grid_spec=spec, out_shape=out_shape,
            input_output_aliases={k + 1: v for k, v in aliases.items()}, compiler_params=_params(sem),
        )(place, *operands)

    if not jobs:
        outs = launch(body, list(in_specs), list(out_specs), list(out_shape), list(scratch), carried, sem, args)
        return list(outs), []

    total = math.prod(grid) if grid else 1
    mid = min(total - 1, (2 * total) // 3)

    def split(refs, start, counts):
        out = []
        for n in counts:
            out.append(refs[start:start + n])
            start += n
        return out, start

    def wrapped(*refs):
        c_in = refs[:n_in]
        j_ins, p = split(refs, n_in, [len(j.inputs) for j in jobs])
        c_out = refs[p:p + n_out]
        j_outs, p = split(refs, p + n_out, [len(j.out_shapes) for j in jobs])
        c_sc = refs[p:p + n_sc]
        j_sems, p = split(refs, p + n_sc, [len(j.sems) for j in jobs])
        pos = _place()
        step = 0
        for axis, extent in enumerate(grid):
            step = step * extent + pl.program_id(axis)

        def run(phase):
            for j, ins, outs, sems in zip(jobs, j_ins, j_outs, j_sems):
                fn = getattr(j, phase)
                if fn is not None:
                    fn(pos, ins, outs, sems)

        if total == 1:
            run("start")
            body(*c_in, *c_out, *c_sc)
            run("middle")
            run("finish")
            return
        pl.when(step == 0)(lambda: run("start"))
        body(*c_in, *c_out, *c_sc)
        if any(j.middle is not None for j in jobs):
            pl.when(step == mid)(lambda: run("middle"))
        pl.when(step == total - 1)(lambda: run("finish"))

    aliases, in_at, out_at = carried, n_in, n_out
    for j in jobs:
        for src, dst in j.aliases.items():
            aliases[in_at + src] = out_at + dst
        in_at += len(j.inputs)
        out_at += len(j.out_shapes)
    outs = launch(
        wrapped, list(in_specs) + [ANY] * (in_at - n_in), list(out_specs) + [ANY] * (out_at - n_out),
        list(out_shape) + [s for j in jobs for s in j.out_shapes],
        list(scratch) + [s for j in jobs for s in j.sems], aliases, ("arbitrary",) * len(grid),
        [*args, *[a for j in jobs for a in j.inputs]])
    job_outs, p = split(outs, n_out, [len(j.out_shapes) for j in jobs])
    return list(outs[:n_out]), [list(o) for o in job_outs]


def comm_only(name, jobs):
    def body(dummy_ref, out_ref):
        out_ref[...] = dummy_ref[...]

    dummy = jnp.zeros((8, 128), F32)
    spec = pl.BlockSpec((8, 128), lambda: (0, 0))
    return _call(name, body, (), [spec], [spec], [_sds((8, 128), F32)], [dummy], jobs=jobs)[1]


def _ret(outs, job_outs, jobs, single=True):
    res = outs[0] if single else outs
    return (res, job_outs) if jobs else res


def rmsnorm_fwd(name, x, g, jobs=()):
    s, d = x.shape
    tm = _tile(s, 512, 8)

    def body(x_ref, g_ref, n_ref, r_ref):
        xv = x_ref[...]
        r = lax.rsqrt(jnp.mean(xv * xv, axis=-1, keepdims=True) + EPS)
        n_ref[...] = (xv * r * g_ref[...]).astype(BF16)
        r_ref[...] = r

    row = lambda i: (i, 0)
    outs, job_outs = _call(
        name, body, (s // tm,),
        [pl.BlockSpec((tm, d), row), pl.BlockSpec((1, d), lambda i: (0, 0))],
        [pl.BlockSpec((tm, d), row), pl.BlockSpec((tm, 1), row)],
        [_sds((s, d), BF16), _sds((s, 1), F32)], [x, g], sem=("arbitrary",), jobs=jobs)
    return _ret(outs, job_outs, jobs, single=False)


def rmsnorm_bwd(name, dn, x, r, g, dh_in, jobs=()):
    s, d = x.shape
    tm = _tile(s, 512, 8)

    def body(dn_ref, x_ref, r_ref, g_ref, dh_ref, out_ref, outb_ref, dg_ref):
        i = pl.program_id(0)
        xh = x_ref[...] * r_ref[...]
        dnv = dn_ref[...]
        dxh = dnv * g_ref[...]
        dx = r_ref[...] * (dxh - xh * jnp.mean(dxh * xh, axis=-1, keepdims=True))
        out = dh_ref[...] + dx
        out_ref[...] = out
        outb_ref[...] = out.astype(BF16)
        part = jnp.sum(dnv * xh, axis=0, keepdims=True)

        @pl.when(i == 0)
        def _():
            dg_ref[...] = part

        @pl.when(i > 0)
        def _():
            dg_ref[...] += part

    row = lambda i: (i, 0)
    fixed = lambda i: (0, 0)
    outs, job_outs = _call(
        name, body, (s // tm,),
        [pl.BlockSpec((tm, d), row), pl.BlockSpec((tm, d), row), pl.BlockSpec((tm, 1), row),
         pl.BlockSpec((1, d), fixed), pl.BlockSpec((tm, d), row)],
        [pl.BlockSpec((tm, d), row), pl.BlockSpec((tm, d), row), pl.BlockSpec((1, d), fixed)],
        [_sds((s, d), F32), _sds((s, d), BF16), _sds((1, d), F32)], [dn, x, r, g, dh_in],
        sem=("arbitrary",), jobs=jobs)
    return _ret(outs, job_outs, jobs, single=False)


def gain_grad(name, dn_a, dn_b, x, r):
    s, d = x.shape
    tm = _tile(s, 512, 8)

    def body(a_ref, b_ref, x_ref, r_ref, dg_ref):
        i = pl.program_id(0)
        part = jnp.sum((a_ref[...] + b_ref[...]) * (x_ref[...] * r_ref[...]), axis=0, keepdims=True)

        @pl.when(i == 0)
        def _():
            dg_ref[...] = part

        @pl.when(i > 0)
        def _():
            dg_ref[...] += part

    row = lambda i: (i, 0)
    return _call(
        name, body, (s // tm,),
        [pl.BlockSpec((tm, d), row), pl.BlockSpec((tm, d), row), pl.BlockSpec((tm, d), row),
         pl.BlockSpec((tm, 1), row)],
        [pl.BlockSpec((1, d), lambda i: (0, 0))], [_sds((1, d), F32)], [dn_a, dn_b, x, r],
        sem=("arbitrary",))[0][0]


def loss_head(name, h, g, target):
    s, d = h.shape
    tm = _tile(s, 512, 8)
    nsteps = s // tm

    def body(h_ref, g_ref, t_ref, loss_ref, dh_ref, dhb_ref, dg_ref, sq_ref):
        i = pl.program_id(0)
        hv = h_ref[...]
        gv = g_ref[...]
        r = lax.rsqrt(jnp.mean(hv * hv, axis=-1, keepdims=True) + EPS)
        xh = hv * r
        err = xh * gv - t_ref[...]
        dy = err * (1.0 / d)
        dxh = dy * gv
        dh = r * (dxh - xh * jnp.mean(dxh * xh, axis=-1, keepdims=True))
        dh_ref[...] = dh
        dhb_ref[...] = dh.astype(BF16)
        dg_part = jnp.sum(dy * xh, axis=0, keepdims=True)
        sq_part = jnp.sum(err * err, axis=0, keepdims=True)

        @pl.when(i == 0)
        def _():
            dg_ref[...] = dg_part
            sq_ref[...] = sq_part

        @pl.when(i > 0)
        def _():
            dg_ref[...] += dg_part
            sq_ref[...] += sq_part

        @pl.when(i == nsteps - 1)
        def _():
            total = jnp.sum(sq_ref[...], axis=-1, keepdims=True) * (0.5 / d)
            loss_ref[...] = jnp.broadcast_to(total, loss_ref.shape)

    row = lambda i: (i, 0)
    fixed = lambda i: (0, 0)
    return _call(
        name, body, (nsteps,),
        [pl.BlockSpec((tm, d), row), pl.BlockSpec((1, d), fixed), pl.BlockSpec((tm, d), row)],
        [pl.BlockSpec((8, 128), fixed), pl.BlockSpec((tm, d), row), pl.BlockSpec((tm, d), row),
         pl.BlockSpec((1, d), fixed)],
        [_sds((8, 128), F32), _sds((s, d), F32), _sds((s, d), BF16), _sds((1, d), F32)], [h, g, target],
        scratch=[pltpu.VMEM((1, d), F32)], sem=("arbitrary",))[0]


def _mm(name, grid, in_arrays, in_specs, out_shapes, out_specs, acc_tile, dot, epilogue, jobs=(), place=None):
    nk = grid[2]
    n_in = len(in_arrays)
    n_out = len(out_shapes)

    def body(*refs):
        ins, outs = refs[:n_in], refs[n_in:n_in + n_out]
        if nk == 1:
            epilogue(dot(*ins), ins, outs)
            return
        acc = refs[n_in + n_out]
        k = pl.program_id(2)

        @pl.when(k == 0)
        def _():
            acc[...] = dot(*ins)

        @pl.when(jnp.logical_and(k > 0, k < nk - 1))
        def _():
            acc[...] += dot(*ins)

        @pl.when(k == nk - 1)
        def _():
            epilogue(acc[...] + dot(*ins), ins, outs)

    scratch = [pltpu.VMEM(acc_tile, F32)] if nk > 1 else []
    outs, job_outs = _call(name, body, grid, in_specs, out_specs, out_shapes, in_arrays, scratch=scratch,
                           sem=("parallel", "parallel", "arbitrary"), jobs=jobs, place=place)
    return _ret(outs, job_outs, jobs)


def _store(scale, dtype):
    def epilogue(acc, ins, outs):
        outs[0][...] = (acc * scale if scale != 1.0 else acc).astype(dtype)
    return epilogue


def mm_nn(name, a, w, out_dtype, tm=1024, tn=1024, tk=2048, jobs=()):
    m, kd = a.shape
    n = w.shape[1]
    tm, tn, tk = _tile(m, tm, 8), _tile(n, tn), _tile(kd, tk)
    return _mm(
        name, (n // tn, m // tm, kd // tk), [a, w],
        [pl.BlockSpec((tm, tk), lambda j, i, k: (i, k)), pl.BlockSpec((tk, tn), lambda j, i, k: (k, j))],
        [_sds((m, n), out_dtype)], [pl.BlockSpec((tm, tn), lambda j, i, k: (i, j))], (tm, tn),
        lambda a_ref, w_ref: _dot_nn(a_ref[...], w_ref[...]), _store(1.0, out_dtype), jobs)


def mm_nn_resid(name, a, w, x, scale, tm=1024, tn=1024, tk=1408, jobs=()):
    m, kd = a.shape
    n = w.shape[1]
    tm, tn, tk = _tile(m, tm, 8), _tile(n, tn), _tile(kd, tk)

    def epilogue(acc, ins, outs):
        outs[0][...] = ins[2][...] + scale * acc

    return _mm(
        name, (n // tn, m // tm, kd // tk), [a, w, x],
        [pl.BlockSpec((tm, tk), lambda j, i, k: (i, k)), pl.BlockSpec((tk, tn), lambda j, i, k: (k, j)),
         pl.BlockSpec((tm, tn), lambda j, i, k: (i, j))],
        [_sds((m, n), F32)], [pl.BlockSpec((tm, tn), lambda j, i, k: (i, j))], (tm, tn),
        lambda a_ref, w_ref, x_ref: _dot_nn(a_ref[...], w_ref[...]), epilogue, jobs)


def mm_nt(name, a, w, out_dtype, scale=1.0, tm=1024, tn=1024, tk=2048, jobs=()):
    m, kd = a.shape
    n = w.shape[0]
    tm, tn, tk = _tile(m, tm, 8), _tile(n, tn), _tile(kd, tk)
    return _mm(
        name, (n // tn, m // tm, kd // tk), [a, w],
        [pl.BlockSpec((tm, tk), lambda j, i, k: (i, k)), pl.BlockSpec((tn, tk), lambda j, i, k: (j, k))],
        [_sds((m, n), out_dtype)], [pl.BlockSpec((tm, tn), lambda j, i, k: (i, j))], (tm, tn),
        lambda a_ref, w_ref: _dot_nt(a_ref[...], w_ref[...]), _store(scale, out_dtype), jobs)


def mm_nt_pair_halves(name, a3, w_left, w_right, out_dtype, tm=1024, tn=1024, jobs=()):
    _, m, f = a3.shape
    n = w_left.shape[0]
    w = w_left.shape[1] // N_CHIPS
    tm, tn = _tile(m, tm, 8), _tile(n, tn)
    per_half = f // w

    def dot(a_ref, l_ref, r_ref):
        right = pl.program_id(2) % 2 == 1
        return _dot_nt(a_ref[...], jnp.where(right, r_ref[...], l_ref[...]))

    half = pl.BlockSpec((tn, w), lambda j, i, k: (j, k // 2))
    return _mm(
        name, (n // tn, m // tm, 2 * N_CHIPS), [a3, w_left, w_right],
        [pl.BlockSpec((None, tm, w), lambda j, i, k: (k // per_half, i, k % per_half)), half, half],
        [_sds((m, n), out_dtype)], [pl.BlockSpec((tm, tn), lambda j, i, k: (i, j))], (tm, tn),
        dot, _store(1.0, out_dtype), jobs)


def mm_nt_norm_bwd(name, a, w, x, r, g, dh_in, tm=512, tk=1408, jobs=()):
    pair = a.ndim == 3
    m, kd = a.shape[-2], a.shape[-1]
    d = w.shape[0]
    tm, tk = _tile(m, tm, 8), _tile(kd, tk)
    nkf = kd // tk
    nk = 2 * nkf if pair else nkf
    if pair:
        a_spec = pl.BlockSpec((None, tm, tk), lambda i, k: (k // nkf, i, k % nkf))
    else:
        a_spec = pl.BlockSpec((tm, tk), lambda i, k: (i, k))
    row = lambda i, k: (i, 0)
    fixed = lambda i, k: (0, 0)

    def body(a_ref, w_ref, x_ref, r_ref, g_ref, dh_ref, out_ref, outb_ref, dg_ref, *acc):
        i, k = pl.program_id(0), pl.program_id(1)
        dot = lambda: _dot_nt(a_ref[...], w_ref[...])

        def finish(dn):
            xh = x_ref[...] * r_ref[...]
            dxh = dn * g_ref[...]
            out = dh_ref[...] + r_ref[...] * (dxh - xh * jnp.mean(dxh * xh, axis=-1, keepdims=True))
            out_ref[...] = out
            outb_ref[...] = out.astype(BF16)
            part = jnp.sum(dn * xh, axis=0, keepdims=True)

            @pl.when(i == 0)
            def _():
                dg_ref[...] = part

            @pl.when(i > 0)
            def _():
                dg_ref[...] += part

        if nk == 1:
            finish(dot())
            return

        @pl.when(k == 0)
        def _():
            acc[0][...] = dot()

        @pl.when(jnp.logical_and(k > 0, k < nk - 1))
        def _():
            acc[0][...] += dot()

        @pl.when(k == nk - 1)
        def _():
            finish(acc[0][...] + dot())

    outs, job_outs = _call(
        name, body, (m // tm, nk),
        [a_spec, pl.BlockSpec((d, tk), lambda i, k: (0, k)), pl.BlockSpec((tm, d), row), pl.BlockSpec((tm, 1), row),
         pl.BlockSpec((1, d), fixed), pl.BlockSpec((tm, d), row)],
        [pl.BlockSpec((tm, d), row), pl.BlockSpec((tm, d), row), pl.BlockSpec((1, d), fixed)],
        [_sds((m, d), F32), _sds((m, d), BF16), _sds((1, d), F32)], [a, w, x, r, g, dh_in],
        scratch=[pltpu.VMEM((tm, d), F32)] if nk > 1 else [], sem=("arbitrary", "arbitrary"), jobs=jobs)
    return _ret(outs, job_outs, jobs, single=False)


def mm_tn(name, a, b, out_dtype, scale=1.0, tm=1024, tn=1024, tk=4096, jobs=()):
    kd, m = a.shape
    n = b.shape[1]
    tm, tn, tk = _tile(m, tm), _tile(n, tn), _tile(kd, tk, 16)
    return _mm(
        name, (n // tn, m // tm, kd // tk), [a, b],
        [pl.BlockSpec((tk, tm), lambda j, i, k: (k, i)), pl.BlockSpec((tk, tn), lambda j, i, k: (k, j))],
        [_sds((m, n), out_dtype)], [pl.BlockSpec((tm, tn), lambda j, i, k: (i, j))], (tm, tn),
        lambda a_ref, b_ref: _dot_tn(a_ref[...], b_ref[...]), _store(scale, out_dtype), jobs)


def mm_tn_pair(name, a, b3, out_dtype, tm=1024, tn=512, tk=4096, jobs=()):
    kd, m = a.shape
    f = b3.shape[2]
    tm, tn, tk = _tile(m, tm), _tile(f, tn), _tile(kd, tk, 16)
    nf = f // tn
    return _mm(
        name, (m // tm, 2 * nf, kd // tk), [a, b3],
        [pl.BlockSpec((tk, tm), lambda i, j, k: (k, i)),
         pl.BlockSpec((None, tk, tn), lambda i, j, k: (j // nf, k, j % nf))],
        [_sds((m, 2 * f), out_dtype)], [pl.BlockSpec((tm, tn), lambda i, j, k: (i, j))], (tm, tn),
        lambda a_ref, b_ref: _dot_tn(a_ref[...], b_ref[...]), _store(1.0, out_dtype), jobs)


def mm_tn_pair_half(name, a, b3, out_dtype, place, mine, tm=1024, tn=512, tk=4096, jobs=()):
    kd, m = a.shape
    f = b3.shape[2]
    tm, tn, tk = _tile(m // 2, tm), _tile(f, tn), _tile(kd, tk, 16)
    nf, nbm = f // tn, m // 2 // tm
    which = (lambda p: p[1]) if mine else (lambda p: 1 - p[1])
    return _mm(
        name, (nbm, 2 * nf, kd // tk), [a, b3],
        [pl.BlockSpec((tk, tm), lambda i, j, k, p: (k, i + which(p) * nbm)),
         pl.BlockSpec((None, tk, tn), lambda i, j, k, p: (j // nf, k, j % nf))],
        [_sds((m // 2, 2 * f), out_dtype)], [pl.BlockSpec((tm, tn), lambda i, j, k, p: (i, j))], (tm, tn),
        lambda a_ref, b_ref: _dot_tn(a_ref[...], b_ref[...]), _store(1.0, out_dtype), jobs, place)


def swiglu_fwd(name, n, w_in, tm=1024, tn=512, jobs=(), stride=1, phase=0, prev=None, compact=False):
    s, d = n.shape
    f = w_in.shape[1] // 2 * (stride if compact else 1)
    tm, tn = _tile(s, tm, 8), _tile(f, tn)
    nf = f // tn
    col = lambda j: j * stride + phase
    w_gate = (lambda j: j) if compact else col
    w_up = (lambda j: j + nf // stride) if compact else (lambda j: col(j) + nf)

    def body(n_ref, wg_ref, wu_ref, *rest):
        gu_ref, a_ref = rest[-2:]
        nv = n_ref[...]
        g = _dot_nn(nv, wg_ref[...])
        u = _dot_nn(nv, wu_ref[...])
        gu_ref[0] = g.astype(BF16)
        gu_ref[1] = u.astype(BF16)
        a_ref[...] = (g * jax.nn.sigmoid(g) * u).astype(BF16)

    kept = list(prev) if prev is not None else []
    outs, job_outs = _call(
        name, body, (nf // stride, s // tm),
        [pl.BlockSpec((tm, d), lambda j, i: (i, 0)), pl.BlockSpec((d, tn), lambda j, i: (0, w_gate(j))),
         pl.BlockSpec((d, tn), lambda j, i: (0, w_up(j)))] + [ANY] * len(kept),
        [pl.BlockSpec((2, tm, tn), lambda j, i: (0, i, col(j))), pl.BlockSpec((tm, tn), lambda j, i: (i, col(j)))],
        [_sds((2, s, f), BF16), _sds((s, f), BF16)], [n, w_in, w_in] + kept, sem=("parallel", "parallel"),
        jobs=jobs, carried={3 + k: k for k in range(len(kept))})
    return _ret(outs, job_outs, jobs, single=False)


def swiglu_bwd(name, dh, w_out, gu, scale, tm=1024, tn=512, jobs=()):
    s, d = dh.shape
    f = w_out.shape[0]
    tm, tn = _tile(s, tm, 8), _tile(f, tn)

    sub = _tile(tm, 256, 8)

    def body(dh_ref, w_ref, gu_ref, out_ref):
        for lo in range(0, tm, sub):
            rows = slice(lo, lo + sub)
            da = (_dot_nt(dh_ref[rows, :], w_ref[...]) * scale).astype(BF16)
            g = gu_ref[0, rows, :]
            u = gu_ref[1, rows, :]
            sg = 0.5 * jnp.tanh(0.5 * g) + 0.5
            t = g * sg
            out_ref[0, rows, :] = da * (u * (sg + t * (1.0 - sg)))
            out_ref[1, rows, :] = da * t

    outs, job_outs = _call(
        name, body, (f // tn, s // tm),
        [pl.BlockSpec((tm, d), lambda j, i: (i, 0)), pl.BlockSpec((tn, d), lambda j, i: (j, 0)),
         pl.BlockSpec((2, tm, tn), lambda j, i: (0, i, j))],
        [pl.BlockSpec((2, tm, tn), lambda j, i: (0, i, j))],
        [_sds((2, s, f), BF16)], [dh, w_out, gu], sem=("parallel", "parallel"), jobs=jobs)
    return _ret(outs, job_outs, jobs)


HALO = 16


def _conv_inputs(z_ref, hgc_ref, hhc_ref, i, cw, tm):
    gc = z_ref[:, cw:2 * cw].astype(F32)
    hc = z_ref[:, 2 * cw:3 * cw].astype(F32)
    cin = gc * hc
    halo = hgc_ref[...].astype(F32) * hhc_ref[...].astype(F32) * (i > 0).astype(F32)
    row = lax.broadcasted_iota(jnp.int32, (tm, cw), 0)
    x1 = jnp.where(row == 0, halo[HALO - 1:HALO], pltpu.roll(cin, 1, 0))
    x2 = jnp.where(row == 0, halo[HALO - 2:HALO - 1], jnp.where(row == 1, halo[HALO - 1:HALO], pltpu.roll(cin, 2, 0)))
    return gc, hc, cin, x1, x2


def _tril(w):
    r = lax.broadcasted_iota(jnp.int32, w.shape, 0)
    c = lax.broadcasted_iota(jnp.int32, w.shape, 1)
    return jnp.where(r >= c, w, jnp.zeros_like(w))


def mixer_fwd(name, z, conv_w, conv_b, g_v, w_s, b_t, tm=256, jobs=()):
    s, zc = z.shape
    cw = conv_w.shape[1]
    gw = g_v.shape[1]
    heads = gw // GROUP
    tm = _tile(s, tm)
    hb = tm // HALO

    def body(z_ref, hgc_ref, hhc_ref, cw_ref, cb_ref, gv_ref, ws_ref, bt_ref, y_ref):
        i = pl.program_id(0)
        _, _, cin, x1, x2 = _conv_inputs(z_ref, hgc_ref, hhc_ref, i, cw, tm)
        cv = cb_ref[...] + cw_ref[2:3, :] * cin + cw_ref[1:2, :] * x1 + cw_ref[0:1, :] * x2
        y_ref[:, 0:cw] = (z_ref[:, 0:cw].astype(F32) * cv).astype(BF16)
        for h in range(heads):
            lo = h * GROUP
            vh = z_ref[:, 3 * cw + gw + lo:3 * cw + gw + lo + GROUP].astype(F32)
            rv = lax.rsqrt(jnp.mean(vh * vh, axis=-1, keepdims=True) + EPS)
            vn = (vh * rv * gv_ref[:, lo:lo + GROUP]).astype(BF16)
            w = _tril(ws_ref[h]).astype(BF16)
            for n in range(tm // GROUP):
                rows = slice(n * GROUP, (n + 1) * GROUP)
                sg = _dot_nn(w, vn[rows]) + bt_ref[:, h:h + 1]
                u = z_ref[rows, 3 * cw + lo:3 * cw + lo + GROUP].astype(F32)
                y_ref[rows, cw + lo:cw + lo + GROUP] = (u * sg).astype(BF16)

    fixed2 = lambda i: (0, 0)
    outs, job_outs = _call(
        name, body, (s // tm,),
        [pl.BlockSpec((tm, zc), lambda i: (i, 0)),
         pl.BlockSpec((HALO, cw), lambda i: (jnp.maximum(i * hb - 1, 0), 1)),
         pl.BlockSpec((HALO, cw), lambda i: (jnp.maximum(i * hb - 1, 0), 2)),
         pl.BlockSpec(conv_w.shape, fixed2), pl.BlockSpec(conv_b.shape, fixed2),
         pl.BlockSpec(g_v.shape, fixed2), pl.BlockSpec(w_s.shape, lambda i: (0, 0, 0)),
         pl.BlockSpec(b_t.shape, fixed2)],
        [pl.BlockSpec((tm, cw + gw), lambda i: (i, 0))], [_sds((s, cw + gw), BF16)],
        [z, z, z, conv_w, conv_b, g_v, w_s, b_t], sem=("arbitrary",), jobs=jobs)
    return _ret(outs, job_outs, jobs)


def mixer_bwd(name, z, dy, conv_w, conv_b, g_v, w_s, b_t, tm=256, jobs=()):
    s, zc = z.shape
    cw = conv_w.shape[1]
    gw = g_v.shape[1]
    heads = gw // GROUP
    tm = _tile(s, tm)
    hb = tm // HALO
    nsteps = s // tm
    last_halo = s // HALO - 1

    def body(z_ref, hgc_ref, hhc_ref, ngb_ref, dy_ref, ndy_ref, cw_ref, cb_ref, gv_ref, ws_ref, bt_ref,
             dz_ref, sm_ref, dws_ref, dbt_ref, dsg_ref):
        i = pl.program_id(0)

        @pl.when(i == 0)
        def _():
            sm_ref[...] = jnp.zeros_like(sm_ref)
            dws_ref[...] = jnp.zeros_like(dws_ref)
            dsg_ref[...] = jnp.zeros_like(dsg_ref)

        gc, hc, cin, x1, x2 = _conv_inputs(z_ref, hgc_ref, hhc_ref, i, cw, tm)
        w0, w1, w2 = cw_ref[0:1, :], cw_ref[1:2, :], cw_ref[2:3, :]
        cv = cb_ref[...] + w2 * cin + w1 * x1 + w0 * x2
        gb = z_ref[:, 0:cw].astype(F32)
        dyc = dy_ref[:, 0:cw].astype(F32)
        dz_ref[:, 0:cw] = (dyc * cv).astype(BF16)
        dcv = dyc * gb
        nxt = ndy_ref[...].astype(F32) * ngb_ref[...].astype(F32) * (i < nsteps - 1).astype(F32)
        row = lax.broadcasted_iota(jnp.int32, (tm, cw), 0)
        d1 = jnp.where(row == tm - 1, nxt[0:1], pltpu.roll(dcv, tm - 1, 0))
        d2 = jnp.where(row == tm - 1, nxt[1:2], jnp.where(row == tm - 2, nxt[0:1], pltpu.roll(dcv, tm - 2, 0)))
        dcin = w2 * dcv + w1 * d1 + w0 * d2
        dz_ref[:, cw:2 * cw] = (dcin * hc).astype(BF16)
        dz_ref[:, 2 * cw:3 * cw] = (dcin * gc).astype(BF16)
        sm_ref[0:1, :] += jnp.sum(dcv * x2, axis=0, keepdims=True)
        sm_ref[1:2, :] += jnp.sum(dcv * x1, axis=0, keepdims=True)
        sm_ref[2:3, :] += jnp.sum(dcv * cin, axis=0, keepdims=True)
        sm_ref[3:4, :] += jnp.sum(dcv, axis=0, keepdims=True)

        for h in range(heads):
            lo = h * GROUP
            vcol = slice(3 * cw + gw + lo, 3 * cw + gw + lo + GROUP)
            ucol = slice(3 * cw + lo, 3 * cw + lo + GROUP)
            vh = z_ref[:, vcol].astype(F32)
            rv = lax.rsqrt(jnp.mean(vh * vh, axis=-1, keepdims=True) + EPS)
            xh = vh * rv
            gvh = gv_ref[:, lo:lo + GROUP]
            vn = (xh * gvh).astype(BF16)
            w = _tril(ws_ref[h]).astype(BF16)
            dgv = jnp.zeros((1, GROUP), F32)
            for n in range(tm // GROUP):
                rows = slice(n * GROUP, (n + 1) * GROUP)
                sg = _dot_nn(w, vn[rows]) + bt_ref[:, h:h + 1]
                dyg = dy_ref[rows, cw + lo:cw + lo + GROUP].astype(F32)
                dsg = dyg * z_ref[rows, ucol].astype(F32)
                dz_ref[rows, ucol] = (dyg * sg).astype(BF16)
                dsgb = dsg.astype(BF16)
                dvn = _dot_tn(w, dsgb)
                dws_ref[h] += _dot_nt(dsgb, vn[rows])
                dsg_ref[:, lo:lo + GROUP] += dsg
                xhc = xh[rows]
                dgv = dgv + jnp.sum(dvn * xhc, axis=0, keepdims=True)
                dxh = dvn * gvh
                dv = rv[rows] * (dxh - xhc * jnp.mean(dxh * xhc, axis=-1, keepdims=True))
                dz_ref[rows, vcol] = dv.astype(BF16)
            sm_ref[4:5, lo:lo + GROUP] += dgv

        @pl.when(i == nsteps - 1)
        def _():
            for h in range(heads):
                dws_ref[h] = _tril(dws_ref[h])
                dbt_ref[:, h:h + 1] = jnp.sum(dsg_ref[:, h * GROUP:(h + 1) * GROUP], axis=-1, keepdims=True)

    fixed2 = lambda i: (0, 0)
    fixed3 = lambda i: (0, 0, 0)
    prev = lambda col: (lambda i: (jnp.maximum(i * hb - 1, 0), col))
    nxt_blk = lambda i: (jnp.minimum((i + 1) * hb, last_halo), 0)
    outs, job_outs = _call(
        name, body, (nsteps,),
        [pl.BlockSpec((tm, zc), lambda i: (i, 0)),
         pl.BlockSpec((HALO, cw), prev(1)), pl.BlockSpec((HALO, cw), prev(2)),
         pl.BlockSpec((HALO, cw), nxt_blk),
         pl.BlockSpec((tm, cw + gw), lambda i: (i, 0)), pl.BlockSpec((HALO, cw), nxt_blk),
         pl.BlockSpec(conv_w.shape, fixed2), pl.BlockSpec(conv_b.shape, fixed2),
         pl.BlockSpec(g_v.shape, fixed2), pl.BlockSpec(w_s.shape, fixed3), pl.BlockSpec(b_t.shape, fixed2)],
        [pl.BlockSpec((tm, zc), lambda i: (i, 0)), pl.BlockSpec((8, cw), fixed2),
         pl.BlockSpec(w_s.shape, fixed3), pl.BlockSpec(b_t.shape, fixed2)],
        [_sds((s, zc), BF16), _sds((8, cw), F32), _sds(w_s.shape, F32), _sds(b_t.shape, F32)],
        [z, z, z, z, dy, dy, conv_w, conv_b, g_v, w_s, b_t],
        scratch=[pltpu.VMEM((GROUP, gw), F32)], sem=("arbitrary",), jobs=jobs)
    return _ret(outs, job_outs, jobs, single=False)


def _softmax_rows(sc):
    e = jnp.exp(sc - jnp.max(sc, axis=-1, keepdims=True))
    return e / jnp.sum(e, axis=-1, keepdims=True)


def attn_fwd(name, q, k, v, tm=512, jobs=()):
    s, d = q.shape
    m = k.shape[0]
    hd = d // XA_HEADS
    scale = hd ** -0.5
    tm = _tile(s, tm, 8)

    def body(q_ref, k_ref, v_ref, o_ref):
        for h in range(XA_HEADS):
            cols = slice(h * hd, (h + 1) * hd)
            p = _softmax_rows(_dot_nt(q_ref[:, cols], k_ref[:, cols]) * scale)
            o_ref[:, cols] = _dot_nn(p.astype(BF16), v_ref[:, cols]).astype(BF16)

    outs, job_outs = _call(
        name, body, (s // tm,),
        [pl.BlockSpec((tm, d), lambda i: (i, 0)), pl.BlockSpec((m, d), lambda i: (0, 0)),
         pl.BlockSpec((m, d), lambda i: (0, 0))],
        [pl.BlockSpec((tm, d), lambda i: (i, 0))], [_sds((s, d), BF16)], [q, k, v], sem=("arbitrary",), jobs=jobs)
    return _ret(outs, job_outs, jobs)


def attn_bwd(name, q, k, v, do, tm=512):
    s, d = q.shape
    m = k.shape[0]
    hd = d // XA_HEADS
    scale = hd ** -0.5
    tm = _tile(s, tm, 8)

    def body(q_ref, k_ref, v_ref, do_ref, dq_ref, dk_ref, dv_ref):
        i = pl.program_id(0)

        @pl.when(i == 0)
        def _():
            dk_ref[...] = jnp.zeros_like(dk_ref)
            dv_ref[...] = jnp.zeros_like(dv_ref)

        for h in range(XA_HEADS):
            cols = slice(h * hd, (h + 1) * hd)
            qh = q_ref[:, cols]
            doh = do_ref[:, cols]
            p = _softmax_rows(_dot_nt(qh, k_ref[:, cols]) * scale)
            dp = _dot_nt(doh, v_ref[:, cols])
            ds = (p * (dp - jnp.sum(dp * p, axis=-1, keepdims=True)) * scale).astype(BF16)
            dq_ref[:, cols] = _dot_nn(ds, k_ref[:, cols]).astype(BF16)
            dk_ref[:, cols] += _dot_tn(ds, qh)
            dv_ref[:, cols] += _dot_tn(p.astype(BF16), doh)

    row = lambda i: (i, 0)
    fixed = lambda i: (0, 0)
    return _call(
        name, body, (s // tm,),
        [pl.BlockSpec((tm, d), row), pl.BlockSpec((m, d), fixed), pl.BlockSpec((m, d), fixed),
         pl.BlockSpec((tm, d), row)],
        [pl.BlockSpec((tm, d), row), pl.BlockSpec((m, d), fixed), pl.BlockSpec((m, d), fixed)],
        [_sds((s, d), BF16), _sds((m, d), F32), _sds((m, d), F32)], [q, k, v, do], sem=("arbitrary",))[0]


def _grid2(rows, cols, row_mult):
    tr, tc = _tile(rows, 512, row_mult), _tile(cols, 2048)
    return tr, tc, rows // tr, cols // tc


def cast_place(name, block, axis, place, column_half=None):
    r, c = block.shape
    if column_half is not None:
        c //= 2
    tr, tc, nbr, nbc = _grid2(r, c, 16)
    first = 0 if column_half is None else column_half * nbc
    if axis == 1:
        dst = lambda i, j, p: (i, j + p[0] * nbc)
    else:
        dst = lambda i, j, p: (i + p[0] * nbr, j)

    def body(p_ref, w_ref, out_ref):
        out_ref[...] = w_ref[...].astype(BF16)

    return pl.pallas_call(
        body, name=name,
        grid_spec=pltpu.PrefetchScalarGridSpec(
            num_scalar_prefetch=1, grid=(nbr, nbc),
            in_specs=[pl.BlockSpec((tr, tc), lambda i, j, p: (i, j + first))],
            out_specs=pl.BlockSpec((tr, tc), dst)),
        out_shape=_sds(_full_shape((r, c), axis), BF16),
        compiler_params=_params(("parallel", "parallel")),
    )(place, block)


def pair_add(name, grad, peer, axis, place):
    hr, hc = peer.shape
    tr, tc, nbr, nbc = _grid2(hr, hc, 16)
    same = lambda i, j, p: (i, j)
    if grad.shape == peer.shape:
        mine = same
    elif axis == 1:
        mine = lambda i, j, p: (i + p[1] * nbr, j)
    else:
        mine = lambda i, j, p: (i, j + p[1] * nbc)

    def body(p_ref, g_ref, q_ref, out_ref):
        out_ref[...] = (g_ref[...].astype(F32) + q_ref[...].astype(F32)).astype(BF16)

    return pl.pallas_call(
        body, name=name,
        grid_spec=pltpu.PrefetchScalarGridSpec(
            num_scalar_prefetch=1, grid=(nbr, nbc),
            in_specs=[pl.BlockSpec((tr, tc), mine), pl.BlockSpec((tr, tc), same)],
            out_specs=pl.BlockSpec((tr, tc), same)),
        out_shape=_sds((hr, hc), BF16),
        compiler_params=_params(("parallel", "parallel")),
    )(place, grad, peer)


def cross_sum(name, part, land, axis, shape, place):
    _, sr, sc = land.shape
    tr, tc, nbr, nbc = _grid2(sr, sc, 16)
    if axis == 1:
        own = lambda i, j, p: (i, j + p[0] * nbc)
        dst = lambda i, j, p: (i + p[1] * nbr, j)
    else:
        own = lambda i, j, p: (i + p[0] * nbr, j)
        dst = lambda i, j, p: (i, j + p[1] * nbc)

    def body(p_ref, own_ref, land_ref, out_ref):
        out_ref[...] = ((own_ref[...].astype(F32) + land_ref[0].astype(F32))
                        + (land_ref[1].astype(F32) + land_ref[2].astype(F32)))

    return pl.pallas_call(
        body, name=name,
        grid_spec=pltpu.PrefetchScalarGridSpec(
            num_scalar_prefetch=1, grid=(nbr, nbc),
            in_specs=[pl.BlockSpec((tr, tc), own), pl.BlockSpec((3, tr, tc), lambda i, j, p: (0, i, j))],
            out_specs=pl.BlockSpec((tr, tc), dst)),
        out_shape=_sds(_block(shape, axis), F32),
        compiler_params=_params(("parallel", "parallel")),
    )(place, part, land)


def _adam_math(w, g, m, v):
    m = ADAM_B1 * m + (1.0 - ADAM_B1) * g
    v = ADAM_B2 * v + (1.0 - ADAM_B2) * (g * g)
    m_hat = m / (1.0 - ADAM_B1 ** ADAM_STEP)
    v_hat = v / (1.0 - ADAM_B2 ** ADAM_STEP)
    delta = -ADAM_LR * (m_hat / (jnp.sqrt(v_hat) + ADAM_EPS) + ADAM_WD * w)
    return delta, m, v


def adamw(name, w, g, m, v, jobs=()):
    r, c = w.shape
    tr, tc = _tile(r, 256, 8), _tile(c, 1408)

    def body(w_ref, g_ref, m_ref, v_ref, g_out, d_out, m_out, v_out):
        d, mm, vv = _adam_math(w_ref[...], g_ref[...], m_ref[...], v_ref[...])
        g_out[...] = g_ref[...]
        d_out[...] = d
        m_out[...] = mm
        v_out[...] = vv

    spec = pl.BlockSpec((tr, tc), lambda i, j: (i, j))
    outs, job_outs = _call(name, body, (r // tr, c // tc), [spec] * 4, [spec] * 4, [_sds((r, c), F32)] * 4,
                           [w, g, m, v], sem=("parallel", "parallel"), jobs=jobs)
    return _ret(outs, job_outs, jobs, single=False)


def small_sum(name, stacks):
    def body(*refs):
        for s_ref, out_ref in zip(refs[:len(stacks)], refs[len(stacks):]):
            acc = s_ref[0]
            for d in range(1, s_ref.shape[0]):
                acc = acc + s_ref[d]
            out_ref[...] = acc

    return pl.pallas_call(body, name=name, out_shape=[_sds(s.shape[1:], F32) for s in stacks])(*stacks)


WEIGHTS = ["g_ffn1", "w_ffn1_in", "w_ffn1_out", "g_mix", "w_mix_in", "conv_w", "conv_b", "g_gm_v", "w_spatial",
           "b_spatial", "w_mix_out", "g_xattn", "g_mem", "w_xq", "w_xk", "w_xv", "w_xo", "g_ffn2", "w_ffn2_in",
           "w_ffn2_out", "g_final"]
BIG = {"w_ffn1_in": 1, "w_ffn1_out": 0, "w_mix_in": 1, "w_mix_out": 0, "w_xq": 0, "w_xk": 0, "w_xv": 0, "w_xo": 0,
       "w_ffn2_in": 1, "w_ffn2_out": 0}
SMALL = [n for n in WEIGHTS if n not in BIG]
LATE_SMALL = ["g_ffn1"]
EARLY_SMALL = [n for n in SMALL if n not in LATE_SMALL]


def _pack(arrays):
    flat = jnp.concatenate([a.reshape(-1) for a in arrays])
    rows = -(-flat.shape[0] // 1024) * 8
    return jnp.pad(flat, (0, rows * 128 - flat.shape[0])).reshape(rows, 128)


def _unpack(buf, shapes):
    flat = buf.reshape(-1)
    out, pos = [], 0
    for shp in shapes:
        n = math.prod(shp)
        out.append(flat[pos:pos + n].reshape(shp))
        pos += n
    return out


def kernel(x, mem, g_ffn1, w_ffn1_in, w_ffn1_out, g_mix, w_mix_in, conv_w, conv_b, g_gm_v, w_spatial, b_spatial, w_mix_out, g_xattn, g_mem, w_xq, w_xk, w_xv, w_xo, g_ffn2, w_ffn2_in, w_ffn2_out, g_final, loss_target, m_g_ffn1, m_w_ffn1_in, m_w_ffn1_out, m_g_mix, m_w_mix_in, m_conv_w, m_conv_b, m_g_gm_v, m_w_spatial, m_b_spatial, m_w_mix_out, m_g_xattn, m_g_mem, m_w_xq, m_w_xk, m_w_xv, m_w_xo, m_g_ffn2, m_w_ffn2_in, m_w_ffn2_out, m_g_final, v_g_ffn1, v_w_ffn1_in, v_w_ffn1_out, v_g_mix, v_w_mix_in, v_conv_w, v_conv_b, v_g_gm_v, v_w_spatial, v_b_spatial, v_w_mix_out, v_g_xattn, v_g_mem, v_w_xq, v_w_xk, v_w_xv, v_w_xo, v_g_ffn2, v_w_ffn2_in, v_w_ffn2_out, v_g_final):
    given = dict(locals())
    wts = {n: given[n] for n in WEIGHTS}
    mom = {n: given["m_" + n] for n in WEIGHTS}
    var = {n: given["v_" + n] for n in WEIGHTS}

    xi, yi, ci = lax.axis_index("x"), lax.axis_index("y"), lax.axis_index("c")
    blk = 2 * xi + yi
    place = jnp.stack([blk, ci]).astype(jnp.int32)

    x2, mem2, tgt = x[0], mem[0], loss_target[0]
    w_s, b_t = w_spatial[0], b_spatial[0].T
    gf = g_final[None]

    rest = [n for n in BIG if n != "w_ffn1_in"]
    own = {n: cast_place("cast_" + n, wts[n][0], BIG[n], place) for n in rest}
    own_left = cast_place("cast_w_ffn1_in_left", wts["w_ffn1_in"][0], 1, place, column_half=0)
    own_right = cast_place("cast_w_ffn1_in_right", wts["w_ffn1_in"][0], 1, place, column_half=1)
    shape = {n: own[n].shape for n in rest}
    shape["w_ffn1_in"] = _full_shape(wts["w_ffn1_in"][0].shape, 1)
    full = {}

    def gather_now(name, arrays, axes, collective_id):
        job = gather_job([(a, ax, WHOLE, WHOLE) for a, ax in zip(arrays, axes)])
        return by_sequencer(name, job, "gather", collective_id)[1]

    _, (conv_taps,) = by_sequencer("gather_conv_taps", columns_job(jnp.pad(conv_w[0], ((0, 8 - CONV_K), (0, 0)))),
                                   "chips", TAPS_ID)
    (w1_left,) = gather_now("gather_w_ffn1_in_left", [own_left], [1], GATHER_IDS[0])
    (w1_right,) = gather_now("gather_w_ffn1_in_right", [own_right], [1], GATHER_IDS[1])
    groups = [["w_ffn1_out"], ["w_mix_in", "w_mix_out"], ["w_xq", "w_xk", "w_xv", "w_xo"], ["w_ffn2_in"],
              ["w_ffn2_out"]]
    for g, names in enumerate(groups):
        got = gather_now("gather_" + "_".join(names), [own[n] for n in names], [BIG[n] for n in names],
                         GATHER_IDS[2 + g])
        full.update(zip(names, got))

    half_cols = dict(tm=512, tn=shape["w_ffn1_in"][1] // (2 * N_CHIPS), stride=2, compact=True)
    n1, r1 = rmsnorm_fwd("norm1", x2, g_ffn1)
    halves = swiglu_fwd("ffn1_in_left", n1, w1_left, phase=0, **half_cols)
    gu1, a1 = swiglu_fwd("ffn1_in_right", n1, w1_right, phase=1, prev=halves, **half_cols)
    h1 = mm_nn_resid("ffn1_out", a1, full["w_ffn1_out"], x2, 0.5, tm=512, tk=5632)
    n2, r2 = rmsnorm_fwd("norm2", h1, g_mix)
    z = mm_nn("mix_in", n2, full["w_mix_in"], BF16)
    ycat = mixer_fwd("mixer", z, conv_taps, conv_b, g_gm_v, w_s, b_t)
    h2 = mm_nn_resid("mix_out", ycat, full["w_mix_out"], h1, 1.0, tk=2048)
    n3, r3 = rmsnorm_fwd("norm3", h2, g_xattn)
    mem2, h2 = lax.optimization_barrier((mem2, h2))
    mn, rm = rmsnorm_fwd("norm_mem", mem2, g_mem)
    q = mm_nn("xq", n3, full["w_xq"], BF16)
    k = mm_nn("xk", mn, full["w_xk"], BF16)
    v = mm_nn("xv", mn, full["w_xv"], BF16)
    o = attn_fwd("attn", q, k, v)
    h3 = mm_nn_resid("xo", o, full["w_xo"], h2, 1.0, tk=2048)
    n4, r4 = rmsnorm_fwd("norm4", h3, g_ffn2)
    gu2, a2 = swiglu_fwd("ffn2_in", n4, full["w_ffn2_in"])
    h4 = mm_nn_resid("ffn2_out", a2, full["w_ffn2_out"], h3, 0.5, tm=512, tk=5632)
    loss_blk, dh4, dh4b, dg_final = loss_head("loss_head", h4, gf, tgt)

    dw, peer, part, land, half, reduced, grads = {}, {}, {}, {}, {}, {}, {}
    uses = {"sibling": 0, "chips": 0}

    def tie(first, then):
        return lax.optimization_barrier((first, then))

    def on_sequencer(name, job, peers):
        uses[peers] += 1
        return by_sequencer(name, job, peers, {"sibling": SIBLING_IDS, "chips": CROSS_IDS}[peers][uses[peers] % 2])

    def start_pair(*names):
        kept, got = on_sequencer("pair_" + "_".join(names), pair_job([dw[n] for n in names], [BIG[n] for n in names]),
                                 "sibling")
        for n, k, p in zip(names, kept, got):
            dw[n], peer[n] = k, p

    def start_cross(*names):
        kept, got = on_sequencer("cross_" + "_".join(names),
                                 cross_job([(part[n], BIG[n], shape[n], None, WHOLE) for n in names]), "chips")
        for n, k, l in zip(names, kept, got):
            part[n], land[n] = k, l

    def finish_pair(chain, *names):
        for n in names:
            part[n], chain = tie(pair_add("pair_add_" + n, dw[n], peer[n], BIG[n], place), chain)
        start_cross(*names)
        return chain

    def finish_cross(chain, *names):
        for n in names:
            half[n], chain = tie(cross_sum("cross_sum_" + n, part[n], land[n], BIG[n], shape[n], place), chain)
        _, got = on_sequencer("final_" + "_".join(names),
                              final_job([half[n] for n in names], [BIG[n] for n in names], [shape[n] for n in names]),
                              "sibling")
        reduced.update(zip(names, got))
        return chain

    delta, new_m, new_v = {}, {}, {}

    def update(chain, *names):
        for n in names:
            grads[n], delta[n], new_m[n], new_v[n] = adamw("adamw_" + n, wts[n][0], reduced[n], mom[n][0], var[n][0])
            chain = tie(delta[n], chain)[1]
        return chain

    dgu2 = swiglu_bwd("ffn2_dact", dh4b, full["w_ffn2_out"], gu2, 0.5)
    dw["w_ffn2_in"], dh4b = tie(mm_tn_pair("ffn2_dwin", n4, dgu2, BF16), dh4b)
    start_pair("w_ffn2_in")
    dw["w_ffn2_out"], dgu2 = tie(mm_tn("ffn2_dwout", a2, dh4b, BF16, scale=0.5), dgu2)
    dgu2 = finish_pair(dgu2, "w_ffn2_in")
    start_pair("w_ffn2_out")
    dh3, dh3b, dg_ffn2 = mm_nt_norm_bwd("ffn2_dn", dgu2, full["w_ffn2_in"], h3, r4, g_ffn2, dh4)
    dh3b = finish_pair(dh3b, "w_ffn2_out")

    dw["w_xo"], dh3b = tie(mm_tn("xo_dw", o, dh3b, BF16), dh3b)
    dh3b = finish_cross(dh3b, "w_ffn2_in")
    do = mm_nt("xo_dx", dh3b, full["w_xo"], BF16)
    dq, dk, dv = attn_bwd("attn_bwd", q, k, v, do)
    dkb, dvb = dk.astype(BF16), dv.astype(BF16)
    dw["w_xq"], dq = tie(mm_tn("xq_dw", n3, dq, BF16), dq)
    dq = update(dq, "w_ffn2_in")
    dh2, dh2b, dg_xattn = mm_nt_norm_bwd("xq_dx", dq, full["w_xq"], h2, r3, g_xattn, dh3, tk=1024)
    dw["w_xk"] = mm_tn("xk_dw", mn, dkb, BF16)
    dw["w_xv"] = mm_tn("xv_dw", mn, dvb, BF16)
    dmn_k = mm_nt("xk_dx", dkb, full["w_xk"], F32)
    dmn_v = mm_nt("xv_dx", dvb, full["w_xv"], F32)
    dg_mem = gain_grad("norm_mem_bwd", dmn_k, dmn_v, mem2, rm)

    dh2b = finish_cross(dh2b, "w_ffn2_out")
    dw["w_mix_out"], dh2b = tie(mm_tn("mix_out_dw", ycat, dh2b, BF16), dh2b)
    attn_names = ["w_xo", "w_xq", "w_xk", "w_xv", "w_mix_out"]
    start_pair(*attn_names)
    dycat = mm_nt("mix_out_dx", dh2b, full["w_mix_out"], BF16)
    dycat = finish_pair(dycat, *attn_names)
    dycat = update(dycat, "w_ffn2_out")
    dz, dsmall, dws, dbt = mixer_bwd("mixer_bwd", z, dycat, conv_taps, conv_b, g_gm_v, w_s, b_t)
    dw["w_mix_in"], dz = tie(mm_tn("mix_in_dw", n2, dz, BF16), dz)
    start_pair("w_mix_in")
    dh1, dh1b, dg_mix = mm_nt_norm_bwd("mix_in_dx", dz, full["w_mix_in"], h1, r2, g_mix, dh2, tk=1280)
    dh1b = finish_pair(dh1b, "w_mix_in")
    early = {"g_mix": dg_mix, "conv_w": dsmall[0:CONV_K], "conv_b": dsmall[3:4], "g_gm_v": dsmall[4:5],
             "w_spatial": dws, "b_spatial": dbt.T, "g_xattn": dg_xattn, "g_mem": dg_mem, "g_ffn2": dg_ffn2,
             "g_final": dg_final}
    _, (early_all,) = by_sequencer("stack_early", stack_job(_pack([early[n] for n in EARLY_SMALL])), "all",
                                   STACK_IDS[0])

    dh1b = finish_cross(dh1b, *attn_names)
    dw["w_ffn1_out"], dh1b = tie(mm_tn("ffn1_dwout", a1, dh1b, BF16, scale=0.5), dh1b)
    start_pair("w_ffn1_out")
    dh1b = finish_cross(dh1b, "w_mix_in")
    dgu1 = swiglu_bwd("ffn1_dact", dh1b, full["w_ffn1_out"], gu1, 0.5)
    dgu1 = finish_pair(dgu1, "w_ffn1_out")
    dgu1 = update(dgu1, *attn_names)
    theirs, dgu1 = tie(mm_tn_pair_half("ffn1_dwin_theirs", n1, dgu1, BF16, place, False), dgu1)
    _, (from_sibling,) = on_sequencer("pair_w_ffn1_in", pair_job([theirs], [1], is_half=True), "sibling")
    mine, dgu1 = tie(mm_tn_pair_half("ffn1_dwin_mine", n1, dgu1, BF16, place, True), dgu1)
    part["w_ffn1_in"], dgu1 = tie(pair_add("pair_add_w_ffn1_in", mine, from_sibling, 1, place), dgu1)
    start_cross("w_ffn1_in")
    dgu1 = update(dgu1, "w_mix_in")
    dgu1 = finish_cross(dgu1, "w_ffn1_out")
    dn1 = mm_nt_pair_halves("ffn1_dn", dgu1, w1_left, w1_right, F32)
    dx, _, dg_ffn1 = rmsnorm_bwd("norm1_bwd", dn1, x2, r1, g_ffn1, dh1)
    _, (late_all,) = by_sequencer("stack_late", stack_job(_pack([dg_ffn1])), "all", STACK_IDS[1])
    dx = update(dx, "w_ffn1_out")
    dx = finish_cross(dx, "w_ffn1_in")
    update(dx, "w_ffn1_in")

    early_sum, late_sum = small_sum("small_sum", [early_all, late_all])
    for n, g in zip(EARLY_SMALL, _unpack(early_sum, [early[n].shape for n in EARLY_SMALL])):
        grads[n] = g
    grads["g_ffn1"] = _unpack(late_sum, [dg_ffn1.shape])[0]
    taps_cols = conv_w.shape[2]
    grads["conv_w"] = lax.dynamic_slice_in_dim(grads["conv_w"], blk * taps_cols, taps_cols, axis=1)
    packed = [_pack([src[n] for n in SMALL]) for src in (wts, grads, mom, var)]
    own_shapes = [wts[n].shape for n in SMALL]
    for dst, buf in zip((delta, new_m, new_v), adamw("adamw_small", *packed)[1:]):
        for n, a in zip(SMALL, _unpack(buf, own_shapes)):
            dst[n] = a

    loss = lax.psum(loss_blk[0, 0], ("x", "y", "c"))
    outs = [loss, dx[None]]
    for group in (grads, delta, new_m, new_v):
        outs += [group[n].reshape(wts[n].shape) for n in WEIGHTS]
    return tuple(outs)
```

```python
import math

import jax
import jax.numpy as jnp
from jax import lax
from jax.experimental import pallas as pl
from jax.experimental.pallas import tpu as pltpu
from jax.experimental.pallas import tpu_sc as plsc

F32 = jnp.float32
BF16 = jnp.bfloat16
EPS = 1e-6
GROUP = 128
XA_HEADS = 4
CONV_K = 3
N_CHIPS = 4
VMEM_LIMIT_BYTES = 56 * 1024 * 1024

ADAM_LR = 0.001
ADAM_B1 = 0.9
ADAM_B2 = 0.999
ADAM_EPS = 1e-08
ADAM_WD = 0.01
ADAM_STEP = 10

MESH = pl.DeviceIdType.MESH
ANY = pl.BlockSpec(memory_space=pl.ANY)

GATHER_IDS = tuple(range(1, 8))
CROSS_IDS = (8, 9)
SIBLING_IDS = (10, 11)
STACK_IDS = (12, 13)
TAPS_ID = 14


def _tile(dim, pref, mult=128):
    if dim <= pref:
        return dim
    t = (pref // mult) * mult
    while t >= mult:
        if dim % t == 0:
            return t
        t -= mult
    raise ValueError(f"no tile for {dim} under {pref}")


def _params(sem):
    return pltpu.CompilerParams(dimension_semantics=sem, vmem_limit_bytes=VMEM_LIMIT_BYTES)


def _sds(shape, dtype):
    return jax.ShapeDtypeStruct(shape, dtype)


def _dot_nn(a, b):
    return jnp.dot(a, b, preferred_element_type=F32)


def _dot_nt(a, b):
    return lax.dot_general(a, b, (((1,), (1,)), ((), ())), preferred_element_type=F32)


def _dot_tn(a, b):
    return lax.dot_general(a, b, (((0,), (0,)), ((), ())), preferred_element_type=F32)


class Job:
    def __init__(self, inputs, out_shapes, aliases, sems, start, middle, finish):
        self.inputs, self.out_shapes, self.aliases, self.sems = inputs, out_shapes, aliases, sems
        self.start, self.middle, self.finish = start, middle, finish


def _place():
    x, y, c = lax.axis_index("x"), lax.axis_index("y"), lax.axis_index("c")
    chips = [(1 - x, y), (x, 1 - y), (1 - x, 1 - y)]
    return x, y, c, chips


def _ds(start, size, lane):
    if not isinstance(start, int):
        start = pl.multiple_of(start, 128 if lane else 16)
    return pl.ds(start, size)


WHOLE = (0, 1, 1)


def _window(ref, axis, shape, blk=None, half=None, sub=WHOLE, within=WHOLE):
    n = shape[axis] // N_CHIPS
    hs = shape[1 - axis] // 2
    idx = [slice(None), slice(None)]
    if blk is not None:
        b_first, b_count, b_pieces = within
        b_ext = n // b_pieces
        idx[axis] = _ds(blk * n + b_first * b_ext, b_count * b_ext, axis == 1)
    first, count, pieces = sub
    ext = hs // pieces
    if half is not None:
        idx[1 - axis] = _ds(half * hs + first * ext, count * ext, axis == 0)
    elif pieces > 1:
        idx[1 - axis] = _ds(first * ext, count * ext, axis == 0)
    return ref.at[tuple(idx)]


def _remote(src, dst, send_sem, recv_sem, dev):
    return pltpu.make_async_remote_copy(src_ref=src, dst_ref=dst, send_sem=send_sem, recv_sem=recv_sem,
                                        device_id=dev, device_id_type=MESH)


def _full_shape(block_shape, axis):
    out = list(block_shape)
    out[axis] *= N_CHIPS
    return tuple(out)


def _half_all(shape, axis):
    out = list(shape)
    out[1 - axis] //= 2
    return tuple(out)


def _block(shape, axis):
    out = list(shape)
    out[axis] //= N_CHIPS
    return tuple(out)


def _half_block(shape, axis):
    return _half_all(_block(shape, axis), axis)


def gather_job(items):
    nw = len(items)
    shapes = [item[0].shape for item in items]
    n_sem = 8

    def parts(sub):
        first, count, pieces = sub
        return (2 * first, count, 2 * pieces), (2 * first + count, count, 2 * pieces)

    def start(pos, ins, outs, sems):
        x, y, c, chips = pos
        for w, (_, ax, sub, within) in enumerate(items):
            mine = _window(outs[w], ax, shapes[w], blk=2 * x + y, half=c, sub=sub, within=within)
            for j in range(2):
                _remote(mine, mine, sems[0].at[n_sem * w + j], sems[1].at[n_sem * w + j], (*chips[j], c)).start()

    def middle(pos, ins, outs, sems):
        x, y, c, chips = pos
        for w, (_, ax, sub, within) in enumerate(items):
            for j in range(2):
                cx, cy = chips[j]
                landed = _window(outs[w], ax, shapes[w], blk=2 * cx + cy, half=c, sub=sub, within=within)
                _remote(landed, landed, sems[0].at[n_sem * w + j], sems[1].at[n_sem * w + j], (cx, cy, c)).wait_recv()
                part = _window(outs[w], ax, shapes[w], blk=2 * cx + cy, half=c, sub=parts(sub)[j], within=within)
                _remote(part, part, sems[0].at[n_sem * w + 2 + j], sems[1].at[n_sem * w + 2 + j],
                        (*chips[1 - j], c)).start()
                _remote(landed, landed, sems[0].at[n_sem * w + 4 + j], sems[1].at[n_sem * w + 4 + j],
                        (x, y, 1 - c)).start()

    def finish(pos, ins, outs, sems):
        x, y, c, chips = pos
        sib = (x, y, 1 - c)
        for w, (_, ax, sub, within) in enumerate(items):
            dx, dy = chips[2]
            for j in range(2):
                part = _window(outs[w], ax, shapes[w], blk=2 * dx + dy, half=c, sub=parts(sub)[j], within=within)
                cp = _remote(part, part, sems[0].at[n_sem * w + 2 + j], sems[1].at[n_sem * w + 2 + j], sib)
                cp.wait_recv()
                cp.wait_send()
            diag = _window(outs[w], ax, shapes[w], blk=2 * dx + dy, half=c, sub=sub, within=within)
            _remote(diag, diag, sems[0].at[n_sem * w + 6], sems[1].at[n_sem * w + 6], sib).start()
        for w, (_, ax, sub, within) in enumerate(items):
            for j, (cx, cy) in enumerate(chips):
                passed = _window(outs[w], ax, shapes[w], blk=2 * cx + cy, half=1 - c, sub=sub, within=within)
                cp = _remote(passed, passed, sems[0].at[n_sem * w + 4 + j], sems[1].at[n_sem * w + 4 + j], sib)
                cp.wait_recv()
                cp.wait_send()
            mine = _window(outs[w], ax, shapes[w], blk=2 * x + y, half=c, sub=sub, within=within)
            for j in range(2):
                _remote(mine, mine, sems[0].at[n_sem * w + j], sems[1].at[n_sem * w + j], sib).wait_send()

    sems = [pltpu.SemaphoreType.DMA((n_sem * nw,)), pltpu.SemaphoreType.DMA((n_sem * nw,))]
    return Job([item[0] for item in items], [_sds(item[0].shape, item[0].dtype) for item in items],
               {w: w for w in range(nw)}, sems, start, middle, finish)


def by_sequencer(name, job, peers, collective_id):
    ins = [jax.new_ref(a, memory_space=pltpu.MemorySpace.HBM) for a in job.inputs]
    from_input = {o: i for i, o in job.aliases.items()}
    outs = [ins[from_input[k]] if k in from_input else jax.empty_ref(s, memory_space=pltpu.MemorySpace.HBM)
            for k, s in enumerate(job.out_shapes)]

    @pl.kernel(mesh=plsc.ScalarSubcoreMesh(axis_name="sequencer", num_cores=1), name=name,
               scratch_types=tuple(job.sems), compiler_params=pltpu.CompilerParams(collective_id=collective_id))
    def launch(*sems):
        pos = _place()
        x, y, c, chips = pos
        devs = {"sibling": [(x, y, 1 - c)],
                "chips": [(cx, cy, c) for cx, cy in chips],
                "gather": [(*chips[0], c), (*chips[1], c), (x, y, 1 - c)],
                "all": [(px, py, pc) for px in (x, 1 - x) for py in (y, 1 - y) for pc in (c, 1 - c)][1:]}[peers]
        barrier = pltpu.get_barrier_semaphore()
        for dev in devs:
            pl.semaphore_signal(barrier, inc=1, device_id=dev, device_id_type=MESH)
        pl.semaphore_wait(barrier, len(devs))
        for phase in (job.start, job.middle, job.finish):
            if phase is not None:
                phase(pos, ins, outs, list(sems))

    launch()
    kept = [r[...] for r in ins]
    return kept, [kept[from_input[k]] if k in from_input else r[...] for k, r in enumerate(outs)]


def pair_job(grads, axes, is_half=False):
    nw = len(grads)
    shapes = [g.shape for g in grads]

    def start(pos, ins, outs, sems):
        x, y, c, _ = pos
        for w in range(nw):
            src = ins[w] if is_half else _window(ins[w], axes[w], shapes[w], half=1 - c)
            _remote(src, outs[w], sems[0].at[w], sems[1].at[w], (x, y, 1 - c)).start()

    def finish(pos, ins, outs, sems):
        x, y, c, _ = pos
        for w in range(nw):
            cp = _remote(outs[w], outs[w], sems[0].at[w], sems[1].at[w], (x, y, 1 - c))
            cp.wait_recv()
            cp.wait_send()

    sems = [pltpu.SemaphoreType.DMA((nw,)), pltpu.SemaphoreType.DMA((nw,))]
    out_shapes = [_sds(s if is_half else _half_all(s, a), BF16) for s, a in zip(shapes, axes)]
    return Job(list(grads), out_shapes, {}, sems, start, None, finish)


def cross_job(items):
    nw = len(items)
    inputs, aliases = [], {}
    for w, (part, ax, shape, prev, sub) in enumerate(items):
        inputs.append(part)
        if prev is not None:
            aliases[len(inputs)] = w
            inputs.append(prev)

    def copies(pos, ins, outs, sems):
        x, y, c, chips = pos
        k = 0
        for w, (_, ax, shape, prev, sub) in enumerate(items):
            src = ins[k]
            k += 2 if prev is not None else 1
            for j, (cx, cy) in enumerate(chips):
                slot = _window(outs[w].at[j], ax, shape, sub=sub)
                yield (_remote(_window(src, ax, shape, blk=2 * cx + cy, sub=sub), slot,
                               sems[0].at[3 * w + j], sems[1].at[3 * w + j], (cx, cy, c)),
                       _remote(slot, slot, sems[0].at[3 * w + j], sems[1].at[3 * w + j], (cx, cy, c)))

    def start(pos, ins, outs, sems):
        for send, _ in copies(pos, ins, outs, sems):
            send.start()

    def finish(pos, ins, outs, sems):
        for send, recv in copies(pos, ins, outs, sems):
            recv.wait_recv()
            send.wait_send()

    sems = [pltpu.SemaphoreType.DMA((3 * nw,)), pltpu.SemaphoreType.DMA((3 * nw,))]
    out_shapes = [_sds((3,) + _half_block(shape, ax), BF16) for _, ax, shape, _, _ in items]
    return Job(inputs, out_shapes, aliases, sems, start, None, finish)


def final_job(blocks, axes, shapes):
    nw = len(blocks)

    def start(pos, ins, outs, sems):
        x, y, c, _ = pos
        for w in range(nw):
            mine = _window(outs[w], axes[w], shapes[w], half=c)
            _remote(mine, mine, sems[0].at[w], sems[1].at[w], (x, y, 1 - c)).start()

    def finish(pos, ins, outs, sems):
        x, y, c, _ = pos
        for w in range(nw):
            theirs = _window(outs[w], axes[w], shapes[w], half=1 - c)
            cp = _remote(theirs, theirs, sems[0].at[w], sems[1].at[w], (x, y, 1 - c))
            cp.wait_recv()
            cp.wait_send()

    sems = [pltpu.SemaphoreType.DMA((nw,)), pltpu.SemaphoreType.DMA((nw,))]
    return Job(list(blocks), [_sds(b.shape, b.dtype) for b in blocks], {w: w for w in range(nw)}, sems, start, None,
               finish)


def stack_job(small):
    def peers(pos):
        x, y, c, _ = pos
        for k in range(1, 8):
            yield k - 1, (1 - x if k & 4 else x, 1 - y if k & 2 else y, 1 - c if k & 1 else c)

    def start(pos, ins, outs, sems):
        x, y, c, _ = pos
        mine = outs[0].at[4 * x + 2 * y + c]
        pltpu.make_async_copy(ins[0], mine, sems[2]).start()
        for k, dev in peers(pos):
            _remote(ins[0], mine, sems[0].at[k], sems[1].at[k], dev).start()

    def finish(pos, ins, outs, sems):
        x, y, c, _ = pos
        for k, (px, py, pc) in peers(pos):
            slot = outs[0].at[4 * px + 2 * py + pc]
            cp = _remote(slot, slot, sems[0].at[k], sems[1].at[k], (px, py, pc))
            cp.wait_recv()
            cp.wait_send()
        pltpu.make_async_copy(ins[0], outs[0].at[4 * x + 2 * y + c], sems[2]).wait()

    sems = [pltpu.SemaphoreType.DMA((7,)), pltpu.SemaphoreType.DMA((7,)), pltpu.SemaphoreType.DMA]
    return Job([small], [_sds((8,) + small.shape, small.dtype)], {}, sems, start, None, finish)


def columns_job(block):
    cols = block.shape[1]
    place = lambda out, b: out.at[:, _ds(b * cols, cols, True)]

    def start(pos, ins, outs, sems):
        x, y, c, chips = pos
        pltpu.make_async_copy(ins[0], place(outs[0], 2 * x + y), sems[2]).start()
        for j, (cx, cy) in enumerate(chips):
            _remote(ins[0], place(outs[0], 2 * x + y), sems[0].at[j], sems[1].at[j], (cx, cy, c)).start()

    def finish(pos, ins, outs, sems):
        x, y, c, chips = pos
        for j, (cx, cy) in enumerate(chips):
            got = place(outs[0], 2 * cx + cy)
            cp = _remote(got, got, sems[0].at[j], sems[1].at[j], (cx, cy, c))
            cp.wait_recv()
            cp.wait_send()
        pltpu.make_async_copy(ins[0], place(outs[0], 2 * x + y), sems[2]).wait()

    sems = [pltpu.SemaphoreType.DMA((3,)), pltpu.SemaphoreType.DMA((3,)), pltpu.SemaphoreType.DMA]
    return Job([block], [_sds((block.shape[0], N_CHIPS * cols), block.dtype)], {}, sems, start, None, finish)


def _call(name, body, grid, in_specs, out_specs, out_shape, args, scratch=(), sem=None, jobs=(), place=None,
          carried=None):
    n_in, n_out, n_sc = len(args), len(out_shape), len(scratch)
    carried = dict(carried or {})

    def launch(fn, in_specs, out_specs, out_shape, scratch, aliases, sem, operands):
        if place is None:
            return pl.pallas_call(
                fn, name=name, grid=grid, in_specs=in_specs, out_specs=out_specs, out_shape=out_shape,
                scratch_shapes=scratch, input_output_aliases=aliases, compiler_params=_params(sem))(*operands)
        spec = pltpu.PrefetchScalarGridSpec(num_scalar_prefetch=1, grid=grid, in_specs=in_specs,
                                            out_specs=out_specs, scratch_shapes=scratch)
        return pl.pallas_call(
            lambda p_ref, *refs: fn(*refs), name=name, grid_spec=spec, out_shape=out_shape,
            input_output_aliases={k + 1: v for k, v in aliases.items()}, compiler_params=_params(sem),
        )(place, *operands)

    if not jobs:
        outs = launch(body, list(in_specs), list(out_specs), list(out_shape), list(scratch), carried, sem, args)
        return list(outs), []

    total = math.prod(grid) if grid else 1
    mid = min(total - 1, (2 * total) // 3)

    def split(refs, start, counts):
        out = []
        for n in counts:
            out.append(refs[start:start + n])
            start += n
        return out, start

    def wrapped(*refs):
        c_in = refs[:n_in]
        j_ins, p = split(refs, n_in, [len(j.inputs) for j in jobs])
        c_out = refs[p:p + n_out]
        j_outs, p = split(refs, p + n_out, [len(j.out_shapes) for j in jobs])
        c_sc = refs[p:p + n_sc]
        j_sems, p = split(refs, p + n_sc, [len(j.sems) for j in jobs])
        pos = _place()
        step = 0
        for axis, extent in enumerate(grid):
            step = step * extent + pl.program_id(axis)

        def run(phase):
            for j, ins, outs, sems in zip(jobs, j_ins, j_outs, j_sems):
                fn = getattr(j, phase)
                if fn is not None:
                    fn(pos, ins, outs, sems)

        if total == 1:
            run("start")
            body(*c_in, *c_out, *c_sc)
            run("middle")
            run("finish")
            return
        pl.when(step == 0)(lambda: run("start"))
        body(*c_in, *c_out, *c_sc)
        if any(j.middle is not None for j in jobs):
            pl.when(step == mid)(lambda: run("middle"))
        pl.when(step == total - 1)(lambda: run("finish"))

    aliases, in_at, out_at = carried, n_in, n_out
    for j in jobs:
        for src, dst in j.aliases.items():
            aliases[in_at + src] = out_at + dst
        in_at += len(j.inputs)
        out_at += len(j.out_shapes)
    outs = launch(
        wrapped, list(in_specs) + [ANY] * (in_at - n_in), list(out_specs) + [ANY] * (out_at - n_out),
        list(out_shape) + [s for j in jobs for s in j.out_shapes],
        list(scratch) + [s for j in jobs for s in j.sems], aliases, ("arbitrary",) * len(grid),
        [*args, *[a for j in jobs for a in j.inputs]])
    job_outs, p = split(outs, n_out, [len(j.out_shapes) for j in jobs])
    return list(outs[:n_out]), [list(o) for o in job_outs]


def comm_only(name, jobs):
    def body(dummy_ref, out_ref):
        out_ref[...] = dummy_ref[...]

    dummy = jnp.zeros((8, 128), F32)
    spec = pl.BlockSpec((8, 128), lambda: (0, 0))
    return _call(name, body, (), [spec], [spec], [_sds((8, 128), F32)], [dummy], jobs=jobs)[1]


def _ret(outs, job_outs, jobs, single=True):
    res = outs[0] if single else outs
    return (res, job_outs) if jobs else res


def rmsnorm_fwd(name, x, g, jobs=()):
    s, d = x.shape
    tm = _tile(s, 512, 8)

    def body(x_ref, g_ref, n_ref, r_ref):
        xv = x_ref[...]
        r = lax.rsqrt(jnp.mean(xv * xv, axis=-1, keepdims=True) + EPS)
        n_ref[...] = (xv * r * g_ref[...]).astype(BF16)
        r_ref[...] = r

    row = lambda i: (i, 0)
    outs, job_outs = _call(
        name, body, (s // tm,),
        [pl.BlockSpec((tm, d), row), pl.BlockSpec((1, d), lambda i: (0, 0))],
        [pl.BlockSpec((tm, d), row), pl.BlockSpec((tm, 1), row)],
        [_sds((s, d), BF16), _sds((s, 1), F32)], [x, g], sem=("arbitrary",), jobs=jobs)
    return _ret(outs, job_outs, jobs, single=False)


def rmsnorm_bwd(name, dn, x, r, g, dh_in, jobs=()):
    s, d = x.shape
    tm = _tile(s, 512, 8)

    def body(dn_ref, x_ref, r_ref, g_ref, dh_ref, out_ref, outb_ref, dg_ref):
        i = pl.program_id(0)
        xh = x_ref[...] * r_ref[...]
        dnv = dn_ref[...]
        dxh = dnv * g_ref[...]
        dx = r_ref[...] * (dxh - xh * jnp.mean(dxh * xh, axis=-1, keepdims=True))
        out = dh_ref[...] + dx
        out_ref[...] = out
        outb_ref[...] = out.astype(BF16)
        part = jnp.sum(dnv * xh, axis=0, keepdims=True)

        @pl.when(i == 0)
        def _():
            dg_ref[...] = part

        @pl.when(i > 0)
        def _():
            dg_ref[...] += part

    row = lambda i: (i, 0)
    fixed = lambda i: (0, 0)
    outs, job_outs = _call(
        name, body, (s // tm,),
        [pl.BlockSpec((tm, d), row), pl.BlockSpec((tm, d), row), pl.BlockSpec((tm, 1), row),
         pl.BlockSpec((1, d), fixed), pl.BlockSpec((tm, d), row)],
        [pl.BlockSpec((tm, d), row), pl.BlockSpec((tm, d), row), pl.BlockSpec((1, d), fixed)],
        [_sds((s, d), F32), _sds((s, d), BF16), _sds((1, d), F32)], [dn, x, r, g, dh_in],
        sem=("arbitrary",), jobs=jobs)
    return _ret(outs, job_outs, jobs, single=False)


def gain_grad(name, dn_a, dn_b, x, r):
    s, d = x.shape
    tm = _tile(s, 512, 8)

    def body(a_ref, b_ref, x_ref, r_ref, dg_ref):
        i = pl.program_id(0)
        part = jnp.sum((a_ref[...] + b_ref[...]) * (x_ref[...] * r_ref[...]), axis=0, keepdims=True)

        @pl.when(i == 0)
        def _():
            dg_ref[...] = part

        @pl.when(i > 0)
        def _():
            dg_ref[...] += part

    row = lambda i: (i, 0)
    return _call(
        name, body, (s // tm,),
        [pl.BlockSpec((tm, d), row), pl.BlockSpec((tm, d), row), pl.BlockSpec((tm, d), row),
         pl.BlockSpec((tm, 1), row)],
        [pl.BlockSpec((1, d), lambda i: (0, 0))], [_sds((1, d), F32)], [dn_a, dn_b, x, r],
        sem=("arbitrary",))[0][0]


def loss_head(name, h, g, target):
    s, d = h.shape
    tm = _tile(s, 512, 8)
    nsteps = s // tm

    def body(h_ref, g_ref, t_ref, loss_ref, dh_ref, dhb_ref, dg_ref, sq_ref):
        i = pl.program_id(0)
        hv = h_ref[...]
        gv = g_ref[...]
        r = lax.rsqrt(jnp.mean(hv * hv, axis=-1, keepdims=True) + EPS)
        xh = hv * r
        err = xh * gv - t_ref[...]
        dy = err * (1.0 / d)
        dxh = dy * gv
        dh = r * (dxh - xh * jnp.mean(dxh * xh, axis=-1, keepdims=True))
        dh_ref[...] = dh
        dhb_ref[...] = dh.astype(BF16)
        dg_part = jnp.sum(dy * xh, axis=0, keepdims=True)
        sq_part = jnp.sum(err * err, axis=0, keepdims=True)

        @pl.when(i == 0)
        def _():
            dg_ref[...] = dg_part
            sq_ref[...] = sq_part

        @pl.when(i > 0)
        def _():
            dg_ref[...] += dg_part
            sq_ref[...] += sq_part

        @pl.when(i == nsteps - 1)
        def _():
            total = jnp.sum(sq_ref[...], axis=-1, keepdims=True) * (0.5 / d)
            loss_ref[...] = jnp.broadcast_to(total, loss_ref.shape)

    row = lambda i: (i, 0)
    fixed = lambda i: (0, 0)
    return _call(
        name, body, (nsteps,),
        [pl.BlockSpec((tm, d), row), pl.BlockSpec((1, d), fixed), pl.BlockSpec((tm, d), row)],
        [pl.BlockSpec((8, 128), fixed), pl.BlockSpec((tm, d), row), pl.BlockSpec((tm, d), row),
         pl.BlockSpec((1, d), fixed)],
        [_sds((8, 128), F32), _sds((s, d), F32), _sds((s, d), BF16), _sds((1, d), F32)], [h, g, target],
        scratch=[pltpu.VMEM((1, d), F32)], sem=("arbitrary",))[0]


def _mm(name, grid, in_arrays, in_specs, out_shapes, out_specs, acc_tile, dot, epilogue, jobs=(), place=None):
    nk = grid[2]
    n_in = len(in_arrays)
    n_out = len(out_shapes)

    def body(*refs):
        ins, outs = refs[:n_in], refs[n_in:n_in + n_out]
        if nk == 1:
            epilogue(dot(*ins), ins, outs)
            return
        acc = refs[n_in + n_out]
        k = pl.program_id(2)

        @pl.when(k == 0)
        def _():
            acc[...] = dot(*ins)

        @pl.when(jnp.logical_and(k > 0, k < nk - 1))
        def _():
            acc[...] += dot(*ins)

        @pl.when(k == nk - 1)
        def _():
            epilogue(acc[...] + dot(*ins), ins, outs)

    scratch = [pltpu.VMEM(acc_tile, F32)] if nk > 1 else []
    outs, job_outs = _call(name, body, grid, in_specs, out_specs, out_shapes, in_arrays, scratch=scratch,
                           sem=("parallel", "parallel", "arbitrary"), jobs=jobs, place=place)
    return _ret(outs, job_outs, jobs)


def _store(scale, dtype):
    def epilogue(acc, ins, outs):
        outs[0][...] = (acc * scale if scale != 1.0 else acc).astype(dtype)
    return epilogue


def mm_nn(name, a, w, out_dtype, tm=1024, tn=1024, tk=2048, jobs=()):
    m, kd = a.shape
    n = w.shape[1]
    tm, tn, tk = _tile(m, tm, 8), _tile(n, tn), _tile(kd, tk)
    return _mm(
        name, (n // tn, m // tm, kd // tk), [a, w],
        [pl.BlockSpec((tm, tk), lambda j, i, k: (i, k)), pl.BlockSpec((tk, tn), lambda j, i, k: (k, j))],
        [_sds((m, n), out_dtype)], [pl.BlockSpec((tm, tn), lambda j, i, k: (i, j))], (tm, tn),
        lambda a_ref, w_ref: _dot_nn(a_ref[...], w_ref[...]), _store(1.0, out_dtype), jobs)


def mm_nn_resid(name, a, w, x, scale, tm=1024, tn=1024, tk=1408, jobs=()):
    m, kd = a.shape
    n = w.shape[1]
    tm, tn, tk = _tile(m, tm, 8), _tile(n, tn), _tile(kd, tk)

    def epilogue(acc, ins, outs):
        outs[0][...] = ins[2][...] + scale * acc

    return _mm(
        name, (n // tn, m // tm, kd // tk), [a, w, x],
        [pl.BlockSpec((tm, tk), lambda j, i, k: (i, k)), pl.BlockSpec((tk, tn), lambda j, i, k: (k, j)),
         pl.BlockSpec((tm, tn), lambda j, i, k: (i, j))],
        [_sds((m, n), F32)], [pl.BlockSpec((tm, tn), lambda j, i, k: (i, j))], (tm, tn),
        lambda a_ref, w_ref, x_ref: _dot_nn(a_ref[...], w_ref[...]), epilogue, jobs)


def mm_nt(name, a, w, out_dtype, scale=1.0, tm=1024, tn=1024, tk=2048, jobs=()):
    m, kd = a.shape
    n = w.shape[0]
    tm, tn, tk = _tile(m, tm, 8), _tile(n, tn), _tile(kd, tk)
    return _mm(
        name, (n // tn, m // tm, kd // tk), [a, w],
        [pl.BlockSpec((tm, tk), lambda j, i, k: (i, k)), pl.BlockSpec((tn, tk), lambda j, i, k: (j, k))],
        [_sds((m, n), out_dtype)], [pl.BlockSpec((tm, tn), lambda j, i, k: (i, j))], (tm, tn),
        lambda a_ref, w_ref: _dot_nt(a_ref[...], w_ref[...]), _store(scale, out_dtype), jobs)


def mm_nt_pair_halves(name, a3, w_left, w_right, out_dtype, tm=1024, tn=1024, jobs=()):
    _, m, f = a3.shape
    n = w_left.shape[0]
    w = w_left.shape[1] // N_CHIPS
    tm, tn = _tile(m, tm, 8), _tile(n, tn)
    per_half = f // w

    def dot(a_ref, l_ref, r_ref):
        right = pl.program_id(2) % 2 == 1
        return _dot_nt(a_ref[...], jnp.where(right, r_ref[...], l_ref[...]))

    half = pl.BlockSpec((tn, w), lambda j, i, k: (j, k // 2))
    return _mm(
        name, (n // tn, m // tm, 2 * N_CHIPS), [a3, w_left, w_right],
        [pl.BlockSpec((None, tm, w), lambda j, i, k: (k // per_half, i, k % per_half)), half, half],
        [_sds((m, n), out_dtype)], [pl.BlockSpec((tm, tn), lambda j, i, k: (i, j))], (tm, tn),
        dot, _store(1.0, out_dtype), jobs)


def mm_nt_norm_bwd(name, a, w, x, r, g, dh_in, tm=512, tk=1408, jobs=()):
    pair = a.ndim == 3
    m, kd = a.shape[-2], a.shape[-1]
    d = w.shape[0]
    tm, tk = _tile(m, tm, 8), _tile(kd, tk)
    nkf = kd // tk
    nk = 2 * nkf if pair else nkf
    if pair:
        a_spec = pl.BlockSpec((None, tm, tk), lambda i, k: (k // nkf, i, k % nkf))
    else:
        a_spec = pl.BlockSpec((tm, tk), lambda i, k: (i, k))
    row = lambda i, k: (i, 0)
    fixed = lambda i, k: (0, 0)

    def body(a_ref, w_ref, x_ref, r_ref, g_ref, dh_ref, out_ref, outb_ref, dg_ref, *acc):
        i, k = pl.program_id(0), pl.program_id(1)
        dot = lambda: _dot_nt(a_ref[...], w_ref[...])

        def finish(dn):
            xh = x_ref[...] * r_ref[...]
            dxh = dn * g_ref[...]
            out = dh_ref[...] + r_ref[...] * (dxh - xh * jnp.mean(dxh * xh, axis=-1, keepdims=True))
            out_ref[...] = out
            outb_ref[...] = out.astype(BF16)
            part = jnp.sum(dn * xh, axis=0, keepdims=True)

            @pl.when(i == 0)
            def _():
                dg_ref[...] = part

            @pl.when(i > 0)
            def _():
                dg_ref[...] += part

        if nk == 1:
            finish(dot())
            return

        @pl.when(k == 0)
        def _():
            acc[0][...] = dot()

        @pl.when(jnp.logical_and(k > 0, k < nk - 1))
        def _():
            acc[0][...] += dot()

        @pl.when(k == nk - 1)
        def _():
            finish(acc[0][...] + dot())

    outs, job_outs = _call(
        name, body, (m // tm, nk),
        [a_spec, pl.BlockSpec((d, tk), lambda i, k: (0, k)), pl.BlockSpec((tm, d), row), pl.BlockSpec((tm, 1), row),
         pl.BlockSpec((1, d), fixed), pl.BlockSpec((tm, d), row)],
        [pl.BlockSpec((tm, d), row), pl.BlockSpec((tm, d), row), pl.BlockSpec((1, d), fixed)],
        [_sds((m, d), F32), _sds((m, d), BF16), _sds((1, d), F32)], [a, w, x, r, g, dh_in],
        scratch=[pltpu.VMEM((tm, d), F32)] if nk > 1 else [], sem=("arbitrary", "arbitrary"), jobs=jobs)
    return _ret(outs, job_outs, jobs, single=False)


def mm_tn(name, a, b, out_dtype, scale=1.0, tm=1024, tn=1024, tk=4096, jobs=()):
    kd, m = a.shape
    n = b.shape[1]
    tm, tn, tk = _tile(m, tm), _tile(n, tn), _tile(kd, tk, 16)
    return _mm(
        name, (n // tn, m // tm, kd // tk), [a, b],
        [pl.BlockSpec((tk, tm), lambda j, i, k: (k, i)), pl.BlockSpec((tk, tn), lambda j, i, k: (k, j))],
        [_sds((m, n), out_dtype)], [pl.BlockSpec((tm, tn), lambda j, i, k: (i, j))], (tm, tn),
        lambda a_ref, b_ref: _dot_tn(a_ref[...], b_ref[...]), _store(scale, out_dtype), jobs)


def mm_tn_pair(name, a, b3, out_dtype, tm=1024, tn=512, tk=4096, jobs=()):
    kd, m = a.shape
    f = b3.shape[2]
    tm, tn, tk = _tile(m, tm), _tile(f, tn), _tile(kd, tk, 16)
    nf = f // tn
    return _mm(
        name, (m // tm, 2 * nf, kd // tk), [a, b3],
        [pl.BlockSpec((tk, tm), lambda i, j, k: (k, i)),
         pl.BlockSpec((None, tk, tn), lambda i, j, k: (j // nf, k, j % nf))],
        [_sds((m, 2 * f), out_dtype)], [pl.BlockSpec((tm, tn), lambda i, j, k: (i, j))], (tm, tn),
        lambda a_ref, b_ref: _dot_tn(a_ref[...], b_ref[...]), _store(1.0, out_dtype), jobs)


def mm_tn_pair_half(name, a, b3, out_dtype, place, mine, tm=1024, tn=512, tk=4096, jobs=()):
    kd, m = a.shape
    f = b3.shape[2]
    tm, tn, tk = _tile(m // 2, tm), _tile(f, tn), _tile(kd, tk, 16)
    nf, nbm = f // tn, m // 2 // tm
    which = (lambda p: p[1]) if mine else (lambda p: 1 - p[1])
    return _mm(
        name, (nbm, 2 * nf, kd // tk), [a, b3],
        [pl.BlockSpec((tk, tm), lambda i, j, k, p: (k, i + which(p) * nbm)),
         pl.BlockSpec((None, tk, tn), lambda i, j, k, p: (j // nf, k, j % nf))],
        [_sds((m // 2, 2 * f), out_dtype)], [pl.BlockSpec((tm, tn), lambda i, j, k, p: (i, j))], (tm, tn),
        lambda a_ref, b_ref: _dot_tn(a_ref[...], b_ref[...]), _store(1.0, out_dtype), jobs, place)


def swiglu_fwd(name, n, w_in, tm=1024, tn=512, jobs=(), stride=1, phase=0, prev=None, compact=False):
    s, d = n.shape
    f = w_in.shape[1] // 2 * (stride if compact else 1)
    tm, tn = _tile(s, tm, 8), _tile(f, tn)
    nf = f // tn
    col = lambda j: j * stride + phase
    w_gate = (lambda j: j) if compact else col
    w_up = (lambda j: j + nf // stride) if compact else (lambda j: col(j) + nf)

    def body(n_ref, wg_ref, wu_ref, *rest):
        gu_ref, a_ref = rest[-2:]
        nv = n_ref[...]
        g = _dot_nn(nv, wg_ref[...])
        u = _dot_nn(nv, wu_ref[...])
        gu_ref[0] = g.astype(BF16)
        gu_ref[1] = u.astype(BF16)
        a_ref[...] = (g * jax.nn.sigmoid(g) * u).astype(BF16)

    kept = list(prev) if prev is not None else []
    outs, job_outs = _call(
        name, body, (nf // stride, s // tm),
        [pl.BlockSpec((tm, d), lambda j, i: (i, 0)), pl.BlockSpec((d, tn), lambda j, i: (0, w_gate(j))),
         pl.BlockSpec((d, tn), lambda j, i: (0, w_up(j)))] + [ANY] * len(kept),
        [pl.BlockSpec((2, tm, tn), lambda j, i: (0, i, col(j))), pl.BlockSpec((tm, tn), lambda j, i: (i, col(j)))],
        [_sds((2, s, f), BF16), _sds((s, f), BF16)], [n, w_in, w_in] + kept, sem=("parallel", "parallel"),
        jobs=jobs, carried={3 + k: k for k in range(len(kept))})
    return _ret(outs, job_outs, jobs, single=False)


def swiglu_bwd(name, dh, w_out, gu, scale, tm=1024, tn=512, jobs=()):
    s, d = dh.shape
    f = w_out.shape[0]
    tm, tn = _tile(s, tm, 8), _tile(f, tn)

    sub = _tile(tm, 256, 8)

    def body(dh_ref, w_ref, gu_ref, out_ref):
        for lo in range(0, tm, sub):
            rows = slice(lo, lo + sub)
            da = (_dot_nt(dh_ref[rows, :], w_ref[...]) * scale).astype(BF16)
            g = gu_ref[0, rows, :]
            u = gu_ref[1, rows, :]
            sg = 0.5 * jnp.tanh(0.5 * g) + 0.5
            t = g * sg
            out_ref[0, rows, :] = da * (u * (sg + t * (1.0 - sg)))
            out_ref[1, rows, :] = da * t

    outs, job_outs = _call(
        name, body, (f // tn, s // tm),
        [pl.BlockSpec((tm, d), lambda j, i: (i, 0)), pl.BlockSpec((tn, d), lambda j, i: (j, 0)),
         pl.BlockSpec((2, tm, tn), lambda j, i: (0, i, j))],
        [pl.BlockSpec((2, tm, tn), lambda j, i: (0, i, j))],
        [_sds((2, s, f), BF16)], [dh, w_out, gu], sem=("parallel", "parallel"), jobs=jobs)
    return _ret(outs, job_outs, jobs)


HALO = 16


def _conv_inputs(z_ref, hgc_ref, hhc_ref, i, cw, tm):
    gc = z_ref[:, cw:2 * cw].astype(F32)
    hc = z_ref[:, 2 * cw:3 * cw].astype(F32)
    cin = gc * hc
    halo = hgc_ref[...].astype(F32) * hhc_ref[...].astype(F32) * (i > 0).astype(F32)
    row = lax.broadcasted_iota(jnp.int32, (tm, cw), 0)
    x1 = jnp.where(row == 0, halo[HALO - 1:HALO], pltpu.roll(cin, 1, 0))
    x2 = jnp.where(row == 0, halo[HALO - 2:HALO - 1], jnp.where(row == 1, halo[HALO - 1:HALO], pltpu.roll(cin, 2, 0)))
    return gc, hc, cin, x1, x2


def _tril(w):
    r = lax.broadcasted_iota(jnp.int32, w.shape, 0)
    c = lax.broadcasted_iota(jnp.int32, w.shape, 1)
    return jnp.where(r >= c, w, jnp.zeros_like(w))


def mixer_fwd(name, z, conv_w, conv_b, g_v, w_s, b_t, tm=256, jobs=()):
    s, zc = z.shape
    cw = conv_w.shape[1]
    gw = g_v.shape[1]
    heads = gw // GROUP
    tm = _tile(s, tm)
    hb = tm // HALO

    def body(z_ref, hgc_ref, hhc_ref, cw_ref, cb_ref, gv_ref, ws_ref, bt_ref, y_ref):
        i = pl.program_id(0)
        _, _, cin, x1, x2 = _conv_inputs(z_ref, hgc_ref, hhc_ref, i, cw, tm)
        cv = cb_ref[...] + cw_ref[2:3, :] * cin + cw_ref[1:2, :] * x1 + cw_ref[0:1, :] * x2
        y_ref[:, 0:cw] = (z_ref[:, 0:cw].astype(F32) * cv).astype(BF16)
        for h in range(heads):
            lo = h * GROUP
            vh = z_ref[:, 3 * cw + gw + lo:3 * cw + gw + lo + GROUP].astype(F32)
            rv = lax.rsqrt(jnp.mean(vh * vh, axis=-1, keepdims=True) + EPS)
            vn = (vh * rv * gv_ref[:, lo:lo + GROUP]).astype(BF16)
            w = _tril(ws_ref[h]).astype(BF16)
            for n in range(tm // GROUP):
                rows = slice(n * GROUP, (n + 1) * GROUP)
                sg = _dot_nn(w, vn[rows]) + bt_ref[:, h:h + 1]
                u = z_ref[rows, 3 * cw + lo:3 * cw + lo + GROUP].astype(F32)
                y_ref[rows, cw + lo:cw + lo + GROUP] = (u * sg).astype(BF16)

    fixed2 = lambda i: (0, 0)
    outs, job_outs = _call(
        name, body, (s // tm,),
        [pl.BlockSpec((tm, zc), lambda i: (i, 0)),
         pl.BlockSpec((HALO, cw), lambda i: (jnp.maximum(i * hb - 1, 0), 1)),
         pl.BlockSpec((HALO, cw), lambda i: (jnp.maximum(i * hb - 1, 0), 2)),
         pl.BlockSpec(conv_w.shape, fixed2), pl.BlockSpec(conv_b.shape, fixed2),
         pl.BlockSpec(g_v.shape, fixed2), pl.BlockSpec(w_s.shape, lambda i: (0, 0, 0)),
         pl.BlockSpec(b_t.shape, fixed2)],
        [pl.BlockSpec((tm, cw + gw), lambda i: (i, 0))], [_sds((s, cw + gw), BF16)],
        [z, z, z, conv_w, conv_b, g_v, w_s, b_t], sem=("arbitrary",), jobs=jobs)
    return _ret(outs, job_outs, jobs)


def mixer_bwd(name, z, dy, conv_w, conv_b, g_v, w_s, b_t, tm=256, jobs=()):
    s, zc = z.shape
    cw = conv_w.shape[1]
    gw = g_v.shape[1]
    heads = gw // GROUP
    tm = _tile(s, tm)
    hb = tm // HALO
    nsteps = s // tm
    last_halo = s // HALO - 1

    def body(z_ref, hgc_ref, hhc_ref, ngb_ref, dy_ref, ndy_ref, cw_ref, cb_ref, gv_ref, ws_ref, bt_ref,
             dz_ref, sm_ref, dws_ref, dbt_ref, dsg_ref):
        i = pl.program_id(0)

        @pl.when(i == 0)
        def _():
            sm_ref[...] = jnp.zeros_like(sm_ref)
            dws_ref[...] = jnp.zeros_like(dws_ref)
            dsg_ref[...] = jnp.zeros_like(dsg_ref)

        gc, hc, cin, x1, x2 = _conv_inputs(z_ref, hgc_ref, hhc_ref, i, cw, tm)
        w0, w1, w2 = cw_ref[0:1, :], cw_ref[1:2, :], cw_ref[2:3, :]
        cv = cb_ref[...] + w2 * cin + w1 * x1 + w0 * x2
        gb = z_ref[:, 0:cw].astype(F32)
        dyc = dy_ref[:, 0:cw].astype(F32)
        dz_ref[:, 0:cw] = (dyc * cv).astype(BF16)
        dcv = dyc * gb
        nxt = ndy_ref[...].astype(F32) * ngb_ref[...].astype(F32) * (i < nsteps - 1).astype(F32)
        row = lax.broadcasted_iota(jnp.int32, (tm, cw), 0)
        d1 = jnp.where(row == tm - 1, nxt[0:1], pltpu.roll(dcv, tm - 1, 0))
        d2 = jnp.where(row == tm - 1, nxt[1:2], jnp.where(row == tm - 2, nxt[0:1], pltpu.roll(dcv, tm - 2, 0)))
        dcin = w2 * dcv + w1 * d1 + w0 * d2
        dz_ref[:, cw:2 * cw] = (dcin * hc).astype(BF16)
        dz_ref[:, 2 * cw:3 * cw] = (dcin * gc).astype(BF16)
        sm_ref[0:1, :] += jnp.sum(dcv * x2, axis=0, keepdims=True)
        sm_ref[1:2, :] += jnp.sum(dcv * x1, axis=0, keepdims=True)
        sm_ref[2:3, :] += jnp.sum(dcv * cin, axis=0, keepdims=True)
        sm_ref[3:4, :] += jnp.sum(dcv, axis=0, keepdims=True)

        for h in range(heads):
            lo = h * GROUP
            vcol = slice(3 * cw + gw + lo, 3 * cw + gw + lo + GROUP)
            ucol = slice(3 * cw + lo, 3 * cw + lo + GROUP)
            vh = z_ref[:, vcol].astype(F32)
            rv = lax.rsqrt(jnp.mean(vh * vh, axis=-1, keepdims=True) + EPS)
            xh = vh * rv
            gvh = gv_ref[:, lo:lo + GROUP]
            vn = (xh * gvh).astype(BF16)
            w = _tril(ws_ref[h]).astype(BF16)
            dgv = jnp.zeros((1, GROUP), F32)
            for n in range(tm // GROUP):
                rows = slice(n * GROUP, (n + 1) * GROUP)
                sg = _dot_nn(w, vn[rows]) + bt_ref[:, h:h + 1]
                dyg = dy_ref[rows, cw + lo:cw + lo + GROUP].astype(F32)
                dsg = dyg * z_ref[rows, ucol].astype(F32)
                dz_ref[rows, ucol] = (dyg * sg).astype(BF16)
                dsgb = dsg.astype(BF16)
                dvn = _dot_tn(w, dsgb)
                dws_ref[h] += _dot_nt(dsgb, vn[rows])
                dsg_ref[:, lo:lo + GROUP] += dsg
                xhc = xh[rows]
                dgv = dgv + jnp.sum(dvn * xhc, axis=0, keepdims=True)
                dxh = dvn * gvh
                dv = rv[rows] * (dxh - xhc * jnp.mean(dxh * xhc, axis=-1, keepdims=True))
                dz_ref[rows, vcol] = dv.astype(BF16)
            sm_ref[4:5, lo:lo + GROUP] += dgv

        @pl.when(i == nsteps - 1)
        def _():
            for h in range(heads):
                dws_ref[h] = _tril(dws_ref[h])
                dbt_ref[:, h:h + 1] = jnp.sum(dsg_ref[:, h * GROUP:(h + 1) * GROUP], axis=-1, keepdims=True)

    fixed2 = lambda i: (0, 0)
    fixed3 = lambda i: (0, 0, 0)
    prev = lambda col: (lambda i: (jnp.maximum(i * hb - 1, 0), col))
    nxt_blk = lambda i: (jnp.minimum((i + 1) * hb, last_halo), 0)
    outs, job_outs = _call(
        name, body, (nsteps,),
        [pl.BlockSpec((tm, zc), lambda i: (i, 0)),
         pl.BlockSpec((HALO, cw), prev(1)), pl.BlockSpec((HALO, cw), prev(2)),
         pl.BlockSpec((HALO, cw), nxt_blk),
         pl.BlockSpec((tm, cw + gw), lambda i: (i, 0)), pl.BlockSpec((HALO, cw), nxt_blk),
         pl.BlockSpec(conv_w.shape, fixed2), pl.BlockSpec(conv_b.shape, fixed2),
         pl.BlockSpec(g_v.shape, fixed2), pl.BlockSpec(w_s.shape, fixed3), pl.BlockSpec(b_t.shape, fixed2)],
        [pl.BlockSpec((tm, zc), lambda i: (i, 0)), pl.BlockSpec((8, cw), fixed2),
         pl.BlockSpec(w_s.shape, fixed3), pl.BlockSpec(b_t.shape, fixed2)],
        [_sds((s, zc), BF16), _sds((8, cw), F32), _sds(w_s.shape, F32), _sds(b_t.shape, F32)],
        [z, z, z, z, dy, dy, conv_w, conv_b, g_v, w_s, b_t],
        scratch=[pltpu.VMEM((GROUP, gw), F32)], sem=("arbitrary",), jobs=jobs)
    return _ret(outs, job_outs, jobs, single=False)


def _softmax_rows(sc):
    e = jnp.exp(sc - jnp.max(sc, axis=-1, keepdims=True))
    return e / jnp.sum(e, axis=-1, keepdims=True)


def attn_fwd(name, q, k, v, tm=512, jobs=()):
    s, d = q.shape
    m = k.shape[0]
    hd = d // XA_HEADS
    scale = hd ** -0.5
    tm = _tile(s, tm, 8)

    def body(q_ref, k_ref, v_ref, o_ref):
        for h in range(XA_HEADS):
            cols = slice(h * hd, (h + 1) * hd)
            p = _softmax_rows(_dot_nt(q_ref[:, cols], k_ref[:, cols]) * scale)
            o_ref[:, cols] = _dot_nn(p.astype(BF16), v_ref[:, cols]).astype(BF16)

    outs, job_outs = _call(
        name, body, (s // tm,),
        [pl.BlockSpec((tm, d), lambda i: (i, 0)), pl.BlockSpec((m, d), lambda i: (0, 0)),
         pl.BlockSpec((m, d), lambda i: (0, 0))],
        [pl.BlockSpec((tm, d), lambda i: (i, 0))], [_sds((s, d), BF16)], [q, k, v], sem=("arbitrary",), jobs=jobs)
    return _ret(outs, job_outs, jobs)


def attn_bwd(name, q, k, v, do, tm=512):
    s, d = q.shape
    m = k.shape[0]
    hd = d // XA_HEADS
    scale = hd ** -0.5
    tm = _tile(s, tm, 8)

    def body(q_ref, k_ref, v_ref, do_ref, dq_ref, dk_ref, dv_ref):
        i = pl.program_id(0)

        @pl.when(i == 0)
        def _():
            dk_ref[...] = jnp.zeros_like(dk_ref)
            dv_ref[...] = jnp.zeros_like(dv_ref)

        for h in range(XA_HEADS):
            cols = slice(h * hd, (h + 1) * hd)
            qh = q_ref[:, cols]
            doh = do_ref[:, cols]
            p = _softmax_rows(_dot_nt(qh, k_ref[:, cols]) * scale)
            dp = _dot_nt(doh, v_ref[:, cols])
            ds = (p * (dp - jnp.sum(dp * p, axis=-1, keepdims=True)) * scale).astype(BF16)
            dq_ref[:, cols] = _dot_nn(ds, k_ref[:, cols]).astype(BF16)
            dk_ref[:, cols] += _dot_tn(ds, qh)
            dv_ref[:, cols] += _dot_tn(p.astype(BF16), doh)

    row = lambda i: (i, 0)
    fixed = lambda i: (0, 0)
    return _call(
        name, body, (s // tm,),
        [pl.BlockSpec((tm, d), row), pl.BlockSpec((m, d), fixed), pl.BlockSpec((m, d), fixed),
         pl.BlockSpec((tm, d), row)],
        [pl.BlockSpec((tm, d), row), pl.BlockSpec((m, d), fixed), pl.BlockSpec((m, d), fixed)],
        [_sds((s, d), BF16), _sds((m, d), F32), _sds((m, d), F32)], [q, k, v, do], sem=("arbitrary",))[0]


def _grid2(rows, cols, row_mult):
    tr, tc = _tile(rows, 512, row_mult), _tile(cols, 2048)
    return tr, tc, rows // tr, cols // tc


def cast_place(name, block, axis, place, column_half=None):
    r, c = block.shape
    if column_half is not None:
        c //= 2
    tr, tc, nbr, nbc = _grid2(r, c, 16)
    first = 0 if column_half is None else column_half * nbc
    if axis == 1:
        dst = lambda i, j, p: (i, j + p[0] * nbc)
    else:
        dst = lambda i, j, p: (i + p[0] * nbr, j)

    def body(p_ref, w_ref, out_ref):
        out_ref[...] = w_ref[...].astype(BF16)

    return pl.pallas_call(
        body, name=name,
        grid_spec=pltpu.PrefetchScalarGridSpec(
            num_scalar_prefetch=1, grid=(nbr, nbc),
            in_specs=[pl.BlockSpec((tr, tc), lambda i, j, p: (i, j + first))],
            out_specs=pl.BlockSpec((tr, tc), dst)),
        out_shape=_sds(_full_shape((r, c), axis), BF16),
        compiler_params=_params(("parallel", "parallel")),
    )(place, block)


def pair_add(name, grad, peer, axis, place):
    hr, hc = peer.shape
    tr, tc, nbr, nbc = _grid2(hr, hc, 16)
    same = lambda i, j, p: (i, j)
    if grad.shape == peer.shape:
        mine = same
    elif axis == 1:
        mine = lambda i, j, p: (i + p[1] * nbr, j)
    else:
        mine = lambda i, j, p: (i, j + p[1] * nbc)

    def body(p_ref, g_ref, q_ref, out_ref):
        out_ref[...] = (g_ref[...].astype(F32) + q_ref[...].astype(F32)).astype(BF16)

    return pl.pallas_call(
        body, name=name,
        grid_spec=pltpu.PrefetchScalarGridSpec(
            num_scalar_prefetch=1, grid=(nbr, nbc),
            in_specs=[pl.BlockSpec((tr, tc), mine), pl.BlockSpec((tr, tc), same)],
            out_specs=pl.BlockSpec((tr, tc), same)),
        out_shape=_sds((hr, hc), BF16),
        compiler_params=_params(("parallel", "parallel")),
    )(place, grad, peer)


def cross_sum(name, part, land, axis, shape, place):
    _, sr, sc = land.shape
    tr, tc, nbr, nbc = _grid2(sr, sc, 16)
    if axis == 1:
        own = lambda i, j, p: (i, j + p[0] * nbc)
        dst = lambda i, j, p: (i + p[1] * nbr, j)
    else:
        own = lambda i, j, p: (i + p[0] * nbr, j)
        dst = lambda i, j, p: (i, j + p[1] * nbc)

    def body(p_ref, own_ref, land_ref, out_ref):
        out_ref[...] = ((own_ref[...].astype(F32) + land_ref[0].astype(F32))
                        + (land_ref[1].astype(F32) + land_ref[2].astype(F32)))

    return pl.pallas_call(
        body, name=name,
        grid_spec=pltpu.PrefetchScalarGridSpec(
            num_scalar_prefetch=1, grid=(nbr, nbc),
            in_specs=[pl.BlockSpec((tr, tc), own), pl.BlockSpec((3, tr, tc), lambda i, j, p: (0, i, j))],
            out_specs=pl.BlockSpec((tr, tc), dst)),
        out_shape=_sds(_block(shape, axis), F32),
        compiler_params=_params(("parallel", "parallel")),
    )(place, part, land)


def _adam_math(w, g, m, v):
    m = ADAM_B1 * m + (1.0 - ADAM_B1) * g
    v = ADAM_B2 * v + (1.0 - ADAM_B2) * (g * g)
    m_hat = m / (1.0 - ADAM_B1 ** ADAM_STEP)
    v_hat = v / (1.0 - ADAM_B2 ** ADAM_STEP)
    delta = -ADAM_LR * (m_hat / (jnp.sqrt(v_hat) + ADAM_EPS) + ADAM_WD * w)
    return delta, m, v


def adamw(name, w, g, m, v, jobs=()):
    r, c = w.shape
    tr, tc = _tile(r, 256, 8), _tile(c, 1408)

    def body(w_ref, g_ref, m_ref, v_ref, g_out, d_out, m_out, v_out):
        d, mm, vv = _adam_math(w_ref[...], g_ref[...], m_ref[...], v_ref[...])
        g_out[...] = g_ref[...]
        d_out[...] = d
        m_out[...] = mm
        v_out[...] = vv

    spec = pl.BlockSpec((tr, tc), lambda i, j: (i, j))
    outs, job_outs = _call(name, body, (r // tr, c // tc), [spec] * 4, [spec] * 4, [_sds((r, c), F32)] * 4,
                           [w, g, m, v], sem=("parallel", "parallel"), jobs=jobs)
    return _ret(outs, job_outs, jobs, single=False)


def small_sum(name, stacks):
    def body(*refs):
        for s_ref, out_ref in zip(refs[:len(stacks)], refs[len(stacks):]):
            acc = s_ref[0]
            for d in range(1, s_ref.shape[0]):
                acc = acc + s_ref[d]
            out_ref[...] = acc

    return pl.pallas_call(body, name=name, out_shape=[_sds(s.shape[1:], F32) for s in stacks])(*stacks)


WEIGHTS = ["g_ffn1", "w_ffn1_in", "w_ffn1_out", "g_mix", "w_mix_in", "conv_w", "conv_b", "g_gm_v", "w_spatial",
           "b_spatial", "w_mix_out", "g_xattn", "g_mem", "w_xq", "w_xk", "w_xv", "w_xo", "g_ffn2", "w_ffn2_in",
           "w_ffn2_out", "g_final"]
BIG = {"w_ffn1_in": 1, "w_ffn1_out": 0, "w_mix_in": 1, "w_mix_out": 0, "w_xq": 0, "w_xk": 0, "w_xv": 0, "w_xo": 0,
       "w_ffn2_in": 1, "w_ffn2_out": 0}
SMALL = [n for n in WEIGHTS if n not in BIG]
LATE_SMALL = ["g_ffn1"]
EARLY_SMALL = [n for n in SMALL if n not in LATE_SMALL]


def _pack(arrays):
    flat = jnp.concatenate([a.reshape(-1) for a in arrays])
    rows = -(-flat.shape[0] // 1024) * 8
    return jnp.pad(flat, (0, rows * 128 - flat.shape[0])).reshape(rows, 128)


def _unpack(buf, shapes):
    flat = buf.reshape(-1)
    out, pos = [], 0
    for shp in shapes:
        n = math.prod(shp)
        out.append(flat[pos:pos + n].reshape(shp))
        pos += n
    return out


def kernel(x, mem, g_ffn1, w_ffn1_in, w_ffn1_out, g_mix, w_mix_in, conv_w, conv_b, g_gm_v, w_spatial, b_spatial, w_mix_out, g_xattn, g_mem, w_xq, w_xk, w_xv, w_xo, g_ffn2, w_ffn2_in, w_ffn2_out, g_final, loss_target, m_g_ffn1, m_w_ffn1_in, m_w_ffn1_out, m_g_mix, m_w_mix_in, m_conv_w, m_conv_b, m_g_gm_v, m_w_spatial, m_b_spatial, m_w_mix_out, m_g_xattn, m_g_mem, m_w_xq, m_w_xk, m_w_xv, m_w_xo, m_g_ffn2, m_w_ffn2_in, m_w_ffn2_out, m_g_final, v_g_ffn1, v_w_ffn1_in, v_w_ffn1_out, v_g_mix, v_w_mix_in, v_conv_w, v_conv_b, v_g_gm_v, v_w_spatial, v_b_spatial, v_w_mix_out, v_g_xattn, v_g_mem, v_w_xq, v_w_xk, v_w_xv, v_w_xo, v_g_ffn2, v_w_ffn2_in, v_w_ffn2_out, v_g_final):
    given = dict(locals())
    wts = {n: given[n] for n in WEIGHTS}
    mom = {n: given["m_" + n] for n in WEIGHTS}
    var = {n: given["v_" + n] for n in WEIGHTS}

    xi, yi, ci = lax.axis_index("x"), lax.axis_index("y"), lax.axis_index("c")
    blk = 2 * xi + yi
    place = jnp.stack([blk, ci]).astype(jnp.int32)

    x2, mem2, tgt = x[0], mem[0], loss_target[0]
    w_s, b_t = w_spatial[0], b_spatial[0].T
    gf = g_final[None]

    rest = [n for n in BIG if n != "w_ffn1_in"]
    own = {n: cast_place("cast_" + n, wts[n][0], BIG[n], place) for n in rest}
    own_left = cast_place("cast_w_ffn1_in_left", wts["w_ffn1_in"][0], 1, place, column_half=0)
    own_right = cast_place("cast_w_ffn1_in_right", wts["w_ffn1_in"][0], 1, place, column_half=1)
    shape = {n: own[n].shape for n in rest}
    shape["w_ffn1_in"] = _full_shape(wts["w_ffn1_in"][0].shape, 1)
    full = {}

    def gather_now(name, arrays, axes, collective_id):
        job = gather_job([(a, ax, WHOLE, WHOLE) for a, ax in zip(arrays, axes)])
        return by_sequencer(name, job, "gather", collective_id)[1]

    _, (conv_taps,) = by_sequencer("gather_conv_taps", columns_job(jnp.pad(conv_w[0], ((0, 8 - CONV_K), (0, 0)))),
                                   "chips", TAPS_ID)
    (w1_left,) = gather_now("gather_w_ffn1_in_left", [own_left], [1], GATHER_IDS[0])
    (w1_right,) = gather_now("gather_w_ffn1_in_right", [own_right], [1], GATHER_IDS[1])
    groups = [["w_ffn1_out"], ["w_mix_in", "w_mix_out"], ["w_xq", "w_xk", "w_xv", "w_xo"], ["w_ffn2_in"],
              ["w_ffn2_out"]]
    for g, names in enumerate(groups):
        got = gather_now("gather_" + "_".join(names), [own[n] for n in names], [BIG[n] for n in names],
                         GATHER_IDS[2 + g])
        full.update(zip(names, got))

    half_cols = dict(tm=512, tn=shape["w_ffn1_in"][1] // (2 * N_CHIPS), stride=2, compact=True)
    n1, r1 = rmsnorm_fwd("norm1", x2, g_ffn1)
    halves = swiglu_fwd("ffn1_in_left", n1, w1_left, phase=0, **half_cols)
    gu1, a1 = swiglu_fwd("ffn1_in_right", n1, w1_right, phase=1, prev=halves, **half_cols)
    h1 = mm_nn_resid("ffn1_out", a1, full["w_ffn1_out"], x2, 0.5, tm=512, tk=5632)
    n2, r2 = rmsnorm_fwd("norm2", h1, g_mix)
    z = mm_nn("mix_in", n2, full["w_mix_in"], BF16)
    ycat = mixer_fwd("mixer", z, conv_taps, conv_b, g_gm_v, w_s, b_t)
    h2 = mm_nn_resid("mix_out", ycat, full["w_mix_out"], h1, 1.0, tk=2048)
    n3, r3 = rmsnorm_fwd("norm3", h2, g_xattn)
    mem2, h2 = lax.optimization_barrier((mem2, h2))
    mn, rm = rmsnorm_fwd("norm_mem", mem2, g_mem)
    q = mm_nn("xq", n3, full["w_xq"], BF16)
    k = mm_nn("xk", mn, full["w_xk"], BF16)
    v = mm_nn("xv", mn, full["w_xv"], BF16)
    o = attn_fwd("attn", q, k, v)
    h3 = mm_nn_resid("xo", o, full["w_xo"], h2, 1.0, tk=2048)
    n4, r4 = rmsnorm_fwd("norm4", h3, g_ffn2)
    gu2, a2 = swiglu_fwd("ffn2_in", n4, full["w_ffn2_in"])
    h4 = mm_nn_resid("ffn2_out", a2, full["w_ffn2_out"], h3, 0.5, tm=512, tk=5632)
    loss_blk, dh4, dh4b, dg_final = loss_head("loss_head", h4, gf, tgt)

    dw, peer, part, land, half, reduced, grads = {}, {}, {}, {}, {}, {}, {}
    uses = {"sibling": 0, "chips": 0}

    def tie(first, then):
        return lax.optimization_barrier((first, then))

    def on_sequencer(name, job, peers):
        uses[peers] += 1
        return by_sequencer(name, job, peers, {"sibling": SIBLING_IDS, "chips": CROSS_IDS}[peers][uses[peers] % 2])

    def start_pair(*names):
        kept, got = on_sequencer("pair_" + "_".join(names), pair_job([dw[n] for n in names], [BIG[n] for n in names]),
                                 "sibling")
        for n, k, p in zip(names, kept, got):
            dw[n], peer[n] = k, p

    def start_cross(*names):
        kept, got = on_sequencer("cross_" + "_".join(names),
                                 cross_job([(part[n], BIG[n], shape[n], None, WHOLE) for n in names]), "chips")
        for n, k, l in zip(names, kept, got):
            part[n], land[n] = k, l

    def finish_pair(chain, *names):
        for n in names:
            part[n] = pair_add("pair_add_" + n, dw[n], peer[n], BIG[n], place)
            chain = tie(part[n], chain)[1]
        start_cross(*names)
        return chain

    def finish_cross(chain, *names):
        for n in names:
            half[n], chain = tie(cross_sum("cross_sum_" + n, part[n], land[n], BIG[n], shape[n], place), chain)
        _, got = on_sequencer("final_" + "_".join(names),
                              final_job([half[n] for n in names], [BIG[n] for n in names], [shape[n] for n in names]),
                              "sibling")
        reduced.update(zip(names, got))
        return chain

    delta, new_m, new_v = {}, {}, {}

    def update(chain, *names):
        for n in names:
            grads[n], delta[n], new_m[n], new_v[n] = adamw("adamw_" + n, wts[n][0], reduced[n], mom[n][0], var[n][0])
            chain = tie(delta[n], chain)[1]
        return chain

    dgu2 = swiglu_bwd("ffn2_dact", dh4b, full["w_ffn2_out"], gu2, 0.5)
    dw["w_ffn2_in"], dh4b = tie(mm_tn_pair("ffn2_dwin", n4, dgu2, BF16), dh4b)
    start_pair("w_ffn2_in")
    dw["w_ffn2_out"], dgu2 = tie(mm_tn("ffn2_dwout", a2, dh4b, BF16, scale=0.5), dgu2)
    dgu2 = finish_pair(dgu2, "w_ffn2_in")
    start_pair("w_ffn2_out")
    dh3, dh3b, dg_ffn2 = mm_nt_norm_bwd("ffn2_dn", dgu2, full["w_ffn2_in"], h3, r4, g_ffn2, dh4)
    dh3b = finish_pair(dh3b, "w_ffn2_out")

    dw["w_xo"], dh3b = tie(mm_tn("xo_dw", o, dh3b, BF16), dh3b)
    dh3b = finish_cross(dh3b, "w_ffn2_in")
    do = mm_nt("xo_dx", dh3b, full["w_xo"], BF16)
    dq, dk, dv = attn_bwd("attn_bwd", q, k, v, do)
    dkb, dvb = dk.astype(BF16), dv.astype(BF16)
    dw["w_xq"], dq = tie(mm_tn("xq_dw", n3, dq, BF16), dq)
    dq = update(dq, "w_ffn2_in")
    dh2, dh2b, dg_xattn = mm_nt_norm_bwd("xq_dx", dq, full["w_xq"], h2, r3, g_xattn, dh3, tk=1024)
    dw["w_xk"] = mm_tn("xk_dw", mn, dkb, BF16)
    dw["w_xv"] = mm_tn("xv_dw", mn, dvb, BF16)
    dmn_k = mm_nt("xk_dx", dkb, full["w_xk"], F32)
    dmn_v = mm_nt("xv_dx", dvb, full["w_xv"], F32)
    dg_mem = gain_grad("norm_mem_bwd", dmn_k, dmn_v, mem2, rm)

    dh2b = finish_cross(dh2b, "w_ffn2_out")
    dw["w_mix_out"], dh2b = tie(mm_tn("mix_out_dw", ycat, dh2b, BF16), dh2b)
    attn_names = ["w_xo", "w_xq", "w_xk", "w_xv", "w_mix_out"]
    start_pair(*attn_names)
    dycat = mm_nt("mix_out_dx", dh2b, full["w_mix_out"], BF16)
    dycat = finish_pair(dycat, *attn_names)
    dycat = update(dycat, "w_ffn2_out")
    dz, dsmall, dws, dbt = mixer_bwd("mixer_bwd", z, dycat, conv_taps, conv_b, g_gm_v, w_s, b_t)
    dw["w_mix_in"], dz = tie(mm_tn("mix_in_dw", n2, dz, BF16), dz)
    start_pair("w_mix_in")
    dh1, dh1b, dg_mix = mm_nt_norm_bwd("mix_in_dx", dz, full["w_mix_in"], h1, r2, g_mix, dh2, tk=1280)
    dh1b = finish_pair(dh1b, "w_mix_in")
    early = {"g_mix": dg_mix, "conv_w": dsmall[0:CONV_K], "conv_b": dsmall[3:4], "g_gm_v": dsmall[4:5],
             "w_spatial": dws, "b_spatial": dbt.T, "g_xattn": dg_xattn, "g_mem": dg_mem, "g_ffn2": dg_ffn2,
             "g_final": dg_final}
    _, (early_all,) = by_sequencer("stack_early", stack_job(_pack([early[n] for n in EARLY_SMALL])), "all",
                                   STACK_IDS[0])

    dh1b = finish_cross(dh1b, *attn_names)
    dw["w_ffn1_out"], dh1b = tie(mm_tn("ffn1_dwout", a1, dh1b, BF16, scale=0.5), dh1b)
    start_pair("w_ffn1_out")
    dh1b = finish_cross(dh1b, "w_mix_in")
    dgu1 = swiglu_bwd("ffn1_dact", dh1b, full["w_ffn1_out"], gu1, 0.5)
    dgu1 = finish_pair(dgu1, "w_ffn1_out")
    dgu1 = update(dgu1, *attn_names)
    theirs, dgu1 = tie(mm_tn_pair_half("ffn1_dwin_theirs", n1, dgu1, BF16, place, False), dgu1)
    _, (from_sibling,) = on_sequencer("pair_w_ffn1_in", pair_job([theirs], [1], is_half=True), "sibling")
    mine, dgu1 = tie(mm_tn_pair_half("ffn1_dwin_mine", n1, dgu1, BF16, place, True), dgu1)
    part["w_ffn1_in"] = pair_add("pair_add_w_ffn1_in", mine, from_sibling, 1, place)
    dgu1 = tie(part["w_ffn1_in"], dgu1)[1]
    start_cross("w_ffn1_in")
    dgu1 = update(dgu1, "w_mix_in")
    dgu1 = finish_cross(dgu1, "w_ffn1_out")
    dn1 = mm_nt_pair_halves("ffn1_dn", dgu1, w1_left, w1_right, F32)
    dx, _, dg_ffn1 = rmsnorm_bwd("norm1_bwd", dn1, x2, r1, g_ffn1, dh1)
    _, (late_all,) = by_sequencer("stack_late", stack_job(_pack([dg_ffn1])), "all", STACK_IDS[1])
    dx = update(dx, "w_ffn1_out")
    dx = finish_cross(dx, "w_ffn1_in")
    update(dx, "w_ffn1_in")

    early_sum, late_sum = small_sum("small_sum", [early_all, late_all])
    for n, g in zip(EARLY_SMALL, _unpack(early_sum, [early[n].shape for n in EARLY_SMALL])):
        grads[n] = g
    grads["g_ffn1"] = _unpack(late_sum, [dg_ffn1.shape])[0]
    taps_cols = conv_w.shape[2]
    grads["conv_w"] = lax.dynamic_slice_in_dim(grads["conv_w"], blk * taps_cols, taps_cols, axis=1)
    packed = [_pack([src[n] for n in SMALL]) for src in (wts, grads, mom, var)]
    own_shapes = [wts[n].shape for n in SMALL]
    for dst, buf in zip((delta, new_m, new_v), adamw("adamw_small", *packed)[1:]):
        for n, a in zip(SMALL, _unpack(buf, own_shapes)):
            dst[n] = a

    loss = lax.psum(loss_blk[0, 0], ("x", "y", "c"))
    outs = [loss, dx[None]]
    for group in (grads, delta, new_m, new_v):
        outs += [group[n].reshape(wts[n].shape) for n in WEIGHTS]
    return tuple(outs)
```
